```python
import jax, jax.numpy as jnp
from jax import lax
import numpy as np

D_MODEL = 1024
BATCH = 16
SEQ = 2048
DEPTH = 2

CTX_LEN = 256
GRID_W = 64
F32 = jnp.float32
N_REC = (DEPTH + 1) // 2
N_ATT = DEPTH // 2
N_MOD = 6
NORM_EPS = 1e-6
NEG_INF = -1e30

A_HEADS = 8
A_HD = 64
A_W = A_HEADS * A_HD
DECAY_LORA = 64
AAA_LORA = 64
GATE_LORA = 128
RWKV_COLS = 3 * A_W + DECAY_LORA + AAA_LORA + GATE_LORA
RWKV_SPLITS = (A_W, 2 * A_W, 3 * A_W, 3 * A_W + DECAY_LORA, 3 * A_W + DECAY_LORA + AAA_LORA)
GN_EPS = 64e-5

B_HEADS = 4
B_DK = 128
B_DV = 128
B_W = B_HEADS * B_DK
HGRN_COLS = 5 * B_W
HGRN_CHUNK = 64

REC_IN_COLS = RWKV_COLS + HGRN_COLS
D_MIX = A_W + B_HEADS * B_DV

HQ = 16
HKV = 4
GQ = HQ // HKV
HD = 64
WINDOW = 128
ATT_BLOCK = 128
AX_DIM = HD // 2
ROPE_BASE = 10000.0
ATT_COLS = HQ * HD + 2 * HKV * HD

D_FF = 2816
N_EXPERTS = 8
TOP_K = 2
D_FF_E = 2816
MOE_BLOCK = 128

kernel_name = "hybrid_rwkv7_hgrn2_swa_moe_diffusion_trunk"


def rmsnorm(x, g):
    xf = x.astype(F32)
    y = xf * lax.rsqrt(jnp.mean(xf * xf, axis=-1, keepdims=True) + NORM_EPS)
    return (y * g.astype(F32)).astype(x.dtype)


def modulate(x, g, shift, scale):
    return rmsnorm(x, g) * (1 + scale) + shift


def adaln(cvec, w, b):
    return jnp.split(jax.nn.silu(cvec) @ w + b, N_MOD, axis=-1)


def centred_shift(t):
    tp = jnp.pad(t, ((0, 0), (1, 1), (0, 0)))
    return 0.5 * (tp[:, :-2] + tp[:, 2:])


def swiglu(h, wg, wu, wd):
    return (jax.nn.silu(h @ wg) * (h @ wu)) @ wd


def axial_rope(T):
    rows = T // GRID_W
    row = jnp.repeat(jnp.arange(rows, dtype=F32), GRID_W)
    col = jnp.tile(jnp.arange(GRID_W, dtype=F32), rows)
    inv = ROPE_BASE ** (-jnp.arange(0, AX_DIM, 2, dtype=F32) / AX_DIM)
    ang = jnp.stack([row[:, None] * inv, col[:, None] * inv], axis=1)
    return jnp.cos(ang), jnp.sin(ang)


def apply_rope(t, cos, sin):
    B, T, H, _ = t.shape
    tr = t.astype(F32).reshape(B, T, H, 2, 2, AX_DIM // 2)
    t1, t2 = tr[..., 0, :], tr[..., 1, :]
    c, s = cos[None, :, None], sin[None, :, None]
    out = jnp.stack([t1 * c - t2 * s, t2 * c + t1 * s], axis=-2)
    return out.reshape(B, T, H, HD).astype(t.dtype)


def heads_a(t):
    return t.reshape(t.shape[:-1] + (A_HEADS, A_HD))


def rwkv_features(p, mu, w0, w_up, a0, a_up, g_up, k_k, k_a):
    p = p + mu * (centred_shift(p) - p)
    r, k, v, wd, ad, gd = jnp.split(p, RWKV_SPLITS, axis=-1)
    w_log = -jax.nn.softplus(-(w0[:, None, None, :] + jnp.einsum('btr,drc->dbtc', jnp.tanh(wd), w_up))) - 0.5
    decay = heads_a(jnp.exp(-jnp.exp(w_log)))
    a = jax.nn.sigmoid(a0 + ad @ a_up)
    g = jax.nn.sigmoid(gd) @ g_up
    kk = heads_a(k * k_k)
    kk = kk / jnp.maximum(jnp.sqrt(jnp.sum(kk * kk, axis=-1, keepdims=True)), 1e-12)
    k = k * (1 + (a - 1) * k_a)
    return heads_a(r), heads_a(k), heads_a(v), kk, heads_a(a), decay, g


def rwkv_scan(feats, direction, s0, reverse):
    r, k, v, kk, a, decay, _ = feats
    xs = tuple(jnp.moveaxis(t, 1, 0) for t in (r, decay[direction], k, v, kk, kk * a))

    def step(S, inp):
        r_t, w_t, k_t, v_t, kk_t, b_t = inp
        S = (S * w_t[:, :, None, :]
             - jnp.einsum('bhvk,bhk->bhv', S, kk_t)[..., None] * b_t[:, :, None, :]
             + v_t[..., None] * k_t[:, :, None, :])
        return S, jnp.einsum('bhvk,bhk->bhv', S, r_t)

    s_fin, o = lax.scan(step, s0, xs, reverse=reverse)
    return jnp.moveaxis(o, 0, 1), s_fin


def rwkv_post(o, feats, r_k, ln_w, ln_b):
    r, k, v, _, _, _, g = feats
    mean = jnp.mean(o, axis=-1, keepdims=True)
    var = jnp.mean(jnp.square(o - mean), axis=-1, keepdims=True)
    on = (o - mean) * lax.rsqrt(var + GN_EPS)
    bonus = jnp.sum(r * k * r_k, axis=-1, keepdims=True) * v
    B, T = o.shape[:2]
    return (on.reshape(B, T, A_W) * ln_w + ln_b + bonus.reshape(B, T, A_W)) * g


def hgrn_features(p, lb):
    q, i, f_raw, gate = jnp.split(p, (B_W, 2 * B_W, 4 * B_W), axis=-1)
    B, T = p.shape[:2]
    f_raw = jnp.moveaxis(f_raw.reshape(B, T, 2, B_W), 2, 0)
    lbb = lb[:, None, None, :]
    f = lbb + (1 - lbb) * jax.nn.sigmoid(f_raw)
    hk = lambda t: t.reshape(t.shape[:-1] + (B_HEADS, B_DK))
    return hk(jax.nn.silu(q)), i.reshape(B, T, B_HEADS, B_DV), hk(1 - f), hk(jnp.log(f)), gate


def hgrn2_chunked(q, k, v, logf, s0):
    B, T, H, _ = q.shape
    L = HGRN_CHUNK
    n = T // L
    ch = lambda t: t.reshape(B, n, L, H, t.shape[-1])
    q, k, v, logf = ch(q), ch(k), ch(v), ch(logf)
    G = jnp.cumsum(logf, axis=2)
    Gref = G[:, :, L // 2 - 1:L // 2]
    att = jnp.einsum('bnthk,bnshk->bnhts', q * jnp.exp(G - Gref), k * jnp.exp(Gref - G))
    att = jnp.where(jnp.tril(jnp.ones((L, L), bool)), att, 0.0)
    o_intra = jnp.einsum('bnhts,bnshv->bnthv', att, v)
    G_end = G[:, :, -1]
    kv = jnp.einsum('bnshk,bnshv->bnhkv', k * jnp.exp(G_end[:, :, None] - G), v)

    def step(S, inp):
        dec, kv_c = inp
        return dec[..., None] * S + kv_c, S

    s_fin, s_start = lax.scan(step, s0, (jnp.exp(G_end).swapaxes(0, 1), kv.swapaxes(0, 1)))
    o_inter = jnp.einsum('bnthk,nbhkv->bnthv', q * jnp.exp(G), s_start)
    return (o_intra + o_inter).reshape(B, T, H, v.shape[-1]), s_fin


def hgrn_post(o, gate, g_norm):
    B, T = o.shape[:2]
    return rmsnorm(o, g_norm).reshape(B, T, B_HEADS * B_DV) * jax.nn.silu(gate)


def flip(t):
    return t[:, ::-1]


def rec_mixer(h_ctx, h_lat, w_in, w_out, mu, w0, w_up, a0, a_up, g_up, k_k, k_a, r_k, ln_w, ln_b, lb, hg_norm):
    B = h_lat.shape[0]
    pc = (h_ctx @ w_in).astype(F32)
    pl = (h_lat @ w_in).astype(F32)
    fa_c = rwkv_features(pc[..., :RWKV_COLS], mu, w0, w_up, a0, a_up, g_up, k_k, k_a)
    fa_l = rwkv_features(pl[..., :RWKV_COLS], mu, w0, w_up, a0, a_up, g_up, k_k, k_a)
    s0a = jnp.zeros((B, A_HEADS, A_HD, A_HD), F32)
    oac_f, sac_f = rwkv_scan(fa_c, 0, s0a, False)
    oac_b, sac_b = rwkv_scan(fa_c, 1, s0a, True)
    oal_f, _ = rwkv_scan(fa_l, 0, sac_f, False)
    oal_b, _ = rwkv_scan(fa_l, 1, sac_b, True)
    ya_c = rwkv_post(oac_f + oac_b, fa_c, r_k, ln_w, ln_b)
    ya_l = rwkv_post(oal_f + oal_b, fa_l, r_k, ln_w, ln_b)
    qc, ic, kc, lfc, gc = hgrn_features(pc[..., RWKV_COLS:], lb)
    ql, il, kl, lfl, gl = hgrn_features(pl[..., RWKV_COLS:], lb)
    s0b = jnp.zeros((B, B_HEADS, B_DK, B_DV), F32)
    obc_f, sbc_f = hgrn2_chunked(qc, kc[0], ic, lfc[0], s0b)
    obc_b, sbc_b = hgrn2_chunked(flip(qc), flip(kc[1]), flip(ic), flip(lfc[1]), s0b)
    obl_f, _ = hgrn2_chunked(ql, kl[0], il, lfl[0], sbc_f)
    obl_b, _ = hgrn2_chunked(flip(ql), flip(kl[1]), flip(il), flip(lfl[1]), sbc_b)
    yb_c = hgrn_post(obc_f + flip(obc_b), gc, hg_norm)
    yb_l = hgrn_post(obl_f + flip(obl_b), gl, hg_norm)
    y_c = jnp.concatenate([ya_c, yb_c], axis=-1).astype(h_ctx.dtype) @ w_out
    y_l = jnp.concatenate([ya_l, yb_l], axis=-1).astype(h_lat.dtype) @ w_out
    return y_c, y_l


def window_attn_mixer(h_ctx, h_lat, w_in, w_out, sink, cos, sin):
    B, T, _ = h_lat.shape
    q, k, v = jnp.split(h_lat @ w_in, (HQ * HD, HQ * HD + HKV * HD), axis=-1)
    q = apply_rope(q.reshape(B, T, HQ, HD), cos, sin)
    k = apply_rope(k.reshape(B, T, HKV, HD), cos, sin)
    v = v.reshape(B, T, HKV, HD)
    kc, vc = jnp.split(h_ctx @ w_in[:, HQ * HD:], 2, axis=-1)
    C = h_ctx.shape[1]
    kc = kc.reshape(B, C, HKV, HD)
    vc = vc.reshape(B, C, HKV, HD)
    nb = T // ATT_BLOCK
    qb = q.reshape(B, nb, ATT_BLOCK, HKV, GQ, HD).swapaxes(0, 1)

    def band(t):
        tp = jnp.pad(t, ((0, 0), (ATT_BLOCK, ATT_BLOCK), (0, 0), (0, 0))).reshape(B, nb + 2, ATT_BLOCK, HKV, HD)
        return jnp.concatenate([tp[:, :-2], tp[:, 1:-1], tp[:, 2:]], axis=2).swapaxes(0, 1)

    kb, vb = band(k), band(v)
    qpos = jnp.arange(ATT_BLOCK)
    kpos = jnp.arange(3 * ATT_BLOCK) - ATT_BLOCK
    rel_ok = jnp.abs(qpos[:, None] - kpos[None, :]) <= WINDOW
    sink_l = sink.astype(F32).reshape(HKV, GQ)[None, :, :, None, None]
    scale = HD ** -0.5

    def block(args):
        i, qi, ki, vi = args
        kabs = i * ATT_BLOCK + kpos
        valid = rel_ok & ((kabs >= 0) & (kabs < T))[None, :]
        s_lat = jnp.einsum('bqhgd,bkhd->bhgqk', qi, ki).astype(F32) * scale
        s_lat = jnp.where(valid, s_lat, NEG_INF)
        s_ctx = jnp.einsum('bqhgd,bkhd->bhgqk', qi, kc).astype(F32) * scale
        sk = jnp.broadcast_to(sink_l, s_lat.shape[:-1] + (1,))
        pr = jax.nn.softmax(jnp.concatenate([s_ctx, s_lat, sk], axis=-1), axis=-1)
        return (jnp.einsum('bhgqk,bkhd->bqhgd', pr[..., :C].astype(vc.dtype), vc)
                + jnp.einsum('bhgqk,bkhd->bqhgd', pr[..., C:C + 3 * ATT_BLOCK].astype(vi.dtype), vi))

    o = lax.map(block, (jnp.arange(nb), qb, kb, vb))
    return o.swapaxes(0, 1).reshape(B, T, HQ * HD) @ w_out


def moe_swiglu(h, router_w, router_b, w_gate, w_up, w_down):
    B, T, D = h.shape
    hf = h.reshape(B * T, D)
    m = hf.shape[0] * TOP_K
    probs = jax.nn.softmax((hf @ router_w + router_b).astype(F32), axis=-1)
    top_p, top_e = lax.top_k(probs, TOP_K)
    top_p = top_p / jnp.sum(top_p, axis=-1, keepdims=True)
    eid = top_e.reshape(m)
    order = jnp.argsort(eid)
    e_sorted = eid[order]
    tok = order // TOP_K
    wt = top_p.reshape(m)[order]
    counts = jnp.bincount(eid, length=N_EXPERTS)
    padded = (counts + MOE_BLOCK - 1) // MOE_BLOCK * MOE_BLOCK
    pad_end = jnp.cumsum(padded)
    pad_start = pad_end - padded
    start = jnp.cumsum(counts) - counts
    dest = pad_start[e_sorted] + (jnp.arange(m) - start[e_sorted])
    n_blocks = m // MOE_BLOCK + N_EXPERTS
    buf = jnp.zeros((n_blocks * MOE_BLOCK, D), h.dtype).at[dest].set(hf[tok])
    block_e = jnp.minimum(jnp.searchsorted(pad_end, jnp.arange(n_blocks) * MOE_BLOCK, side='right'), N_EXPERTS - 1)

    def expert_block(args):
        e, xb = args
        return (jax.nn.silu(xb @ w_gate[e]) * (xb @ w_up[e])) @ w_down[e]

    yb = lax.map(expert_block, (block_e, buf.reshape(n_blocks, MOE_BLOCK, D)))
    y = yb.reshape(-1, D)[dest] * wt[:, None].astype(h.dtype)
    return jnp.zeros_like(hf).at[tok].add(y).reshape(B, T, D)


def setup_inputs(seed: int = 0) -> dict:
    key = jax.random.key(seed)
    ks = iter(jax.random.split(key, 40))
    D = D_MODEL

    def nrm(shape, scale):
        return jax.random.normal(next(ks), shape, F32) * scale

    def gain(shape):
        return 1.0 + nrm(shape, 0.02)

    inp = {}
    inp["x"] = nrm((BATCH, SEQ, D), 1.0)
    inp["c"] = nrm((BATCH, D), 1.0)
    inp["ctx"] = nrm((BATCH, CTX_LEN, D), 1.0)
    inp["c_ctx"] = nrm((D,), 1.0)
    inp["mod_w"] = nrm((DEPTH, D, N_MOD * D), 0.5 * D ** -0.5)
    inp["mod_b"] = nrm((DEPTH, N_MOD * D), 0.01)
    inp["norm_mix"] = gain((DEPTH, D))
    inp["norm_ffn"] = gain((DEPTH, D))
    inp["norm_final"] = gain((D,))
    inp["rec_w_in"] = nrm((N_REC, D, REC_IN_COLS), D ** -0.5)
    inp["rec_w_out"] = nrm((N_REC, D_MIX, D), D_MIX ** -0.5)
    inp["rwkv_mu"] = jax.random.uniform(next(ks), (N_REC, RWKV_COLS), F32)
    inp["rwkv_w0"] = jax.random.uniform(next(ks), (N_REC, 2, A_W), F32, minval=-6.0, maxval=-1.0)
    inp["rwkv_w_up"] = nrm((N_REC, 2, DECAY_LORA, A_W), 0.5 * DECAY_LORA ** -0.5)
    inp["rwkv_a0"] = nrm((N_REC, A_W), 0.1)
    inp["rwkv_a_up"] = nrm((N_REC, AAA_LORA, A_W), 0.5 * AAA_LORA ** -0.5)
    inp["rwkv_g_up"] = nrm((N_REC, GATE_LORA, A_W), GATE_LORA ** -0.5)
    inp["rwkv_k_k"] = 0.85 + nrm((N_REC, A_W), 0.02)
    inp["rwkv_k_a"] = gain((N_REC, A_W))
    inp["rwkv_r_k"] = nrm((N_REC, A_HEADS, A_HD), 0.1)
    inp["rwkv_ln_w"] = gain((N_REC, A_W))
    inp["rwkv_ln_b"] = nrm((N_REC, A_W), 0.01)
    inp["hgrn_lb"] = nrm((2, N_REC + 1, B_W), 0.1)
    inp["hgrn_norm"] = gain((N_REC, B_DV))
    inp["ffn_w_gate"] = nrm((N_REC, D, D_FF), D ** -0.5)
    inp["ffn_w_up"] = nrm((N_REC, D, D_FF), D ** -0.5)
    inp["ffn_w_down"] = nrm((N_REC, D_FF, D), D_FF ** -0.5)
    inp["att_w_in"] = nrm((N_ATT, D, ATT_COLS), D ** -0.5)
    inp["att_w_out"] = nrm((N_ATT, HQ * HD, D), (HQ * HD) ** -0.5)
    inp["att_sink"] = nrm((N_ATT, HQ), 0.5)
    inp["moe_router"] = nrm((N_ATT, D, N_EXPERTS), D ** -0.5)
    inp["moe_router_b"] = nrm((N_ATT, N_EXPERTS), 0.01)
    inp["moe_w_gate"] = nrm((N_ATT, N_EXPERTS, D, D_FF_E), D ** -0.5)
    inp["moe_w_up"] = nrm((N_ATT, N_EXPERTS, D, D_FF_E), D ** -0.5)
    inp["moe_w_down"] = nrm((N_ATT, N_EXPERTS, D_FF_E, D), D_FF_E ** -0.5)
    return inp


def reference(x, c, ctx, c_ctx, mod_w, mod_b, norm_mix, norm_ffn, norm_final, rec_w_in, rec_w_out,
              rwkv_mu, rwkv_w0, rwkv_w_up, rwkv_a0, rwkv_a_up, rwkv_g_up, rwkv_k_k, rwkv_k_a, rwkv_r_k,
              rwkv_ln_w, rwkv_ln_b, hgrn_lb, hgrn_norm, ffn_w_gate, ffn_w_up, ffn_w_down,
              att_w_in, att_w_out, att_sink, moe_router, moe_router_b, moe_w_gate, moe_w_up, moe_w_down):
    T = x.shape[1]
    cos, sin = axial_rope(T)
    lb_all = jnp.cumsum(jax.nn.softmax(hgrn_lb.astype(F32), axis=1), axis=1)
    x_lat, x_ctx = x, ctx
    for l in range(DEPTH):
        j = l // 2
        m_lat = [m[:, None, :] for m in adaln(c, mod_w[l], mod_b[l])]
        m_ctx = adaln(c_ctx, mod_w[l], mod_b[l])
        h_lat = modulate(x_lat, norm_mix[l], m_lat[0], m_lat[1])
        h_ctx = modulate(x_ctx, norm_mix[l], m_ctx[0], m_ctx[1])
        if l % 2 == 0:
            y_ctx, y_lat = rec_mixer(h_ctx, h_lat, rec_w_in[j], rec_w_out[j], rwkv_mu[j], rwkv_w0[j], rwkv_w_up[j],
                                     rwkv_a0[j], rwkv_a_up[j], rwkv_g_up[j], rwkv_k_k[j], rwkv_k_a[j], rwkv_r_k[j],
                                     rwkv_ln_w[j], rwkv_ln_b[j], lb_all[:, j], hgrn_norm[j])
            x_lat = x_lat + m_lat[2] * y_lat
            x_ctx = x_ctx + m_ctx[2] * y_ctx
            f_lat = modulate(x_lat, norm_ffn[l], m_lat[3], m_lat[4])
            f_ctx = modulate(x_ctx, norm_ffn[l], m_ctx[3], m_ctx[4])
            x_lat = x_lat + m_lat[5] * swiglu(f_lat, ffn_w_gate[j], ffn_w_up[j], ffn_w_down[j])
            x_ctx = x_ctx + m_ctx[5] * swiglu(f_ctx, ffn_w_gate[j], ffn_w_up[j], ffn_w_down[j])
        else:
            y_lat = window_attn_mixer(h_ctx, h_lat, att_w_in[j], att_w_out[j], att_sink[j], cos, sin)
            x_lat = x_lat + m_lat[2] * y_lat
            f_lat = modulate(x_lat, norm_ffn[l], m_lat[3], m_lat[4])
            x_lat = x_lat + m_lat[5] * moe_swiglu(f_lat, moe_router[j], moe_router_b[j],
                                                  moe_w_gate[j], moe_w_up[j], moe_w_down[j])
    return rmsnorm(x_lat, norm_final)
```

```python
import functools

import jax
import jax.numpy as jnp
from jax import lax
from jax.experimental import pallas as pl
from jax.experimental.pallas import tpu as pltpu

F32 = jnp.float32
BF16 = jnp.bfloat16
HIGHEST = lax.Precision.HIGHEST

N_MOD = 6
NORM_EPS = 1e-6
NEG_INF = -1e30

A_HEADS = 8
A_HD = 64
A_W = A_HEADS * A_HD
DECAY_LORA = 64
AAA_LORA = 64
GATE_LORA = 128
RWKV_COLS = 3 * A_W + DECAY_LORA + AAA_LORA + GATE_LORA
GN_EPS = 64e-5

B_HEADS = 4
B_DK = 128
B_W = B_HEADS * B_DK
HGRN_COLS = 5 * B_W

HQ = 16
HKV = 4
GQ = HQ // HKV
HD = 64
WINDOW = 128
ATT_BLOCK = 128
AX_DIM = HD // 2
ROPE_BASE = 10000.0
GRID_W = 64
Q_COLS = HQ * HD
KV_COLS = HKV * HD
ROPE_COLS = Q_COLS + KV_COLS
ATT_COLS = Q_COLS + 2 * KV_COLS

N_EXPERTS = 8
LANES = 128
CHUNK = 64
TOK_TILE = 256
VMEM_LIMIT = 56 * 1024 * 1024

SEC_R, SEC_K, SEC_V, SEC_KK, SEC_B, SEC_LW0, SEC_LW1, SEC_G, SEC_BONUS = range(9)
N_SEC = 9


def _cparams(sem):
    return pltpu.CompilerParams(dimension_semantics=sem, vmem_limit_bytes=VMEM_LIMIT)


def _bf(x):
    return x.astype(BF16)


def _dot(a, b):
    return jnp.dot(_bf(a), _bf(b), preferred_element_type=F32)


def _dot_nt(a, b):
    return lax.dot_general(_bf(a), _bf(b), (((1,), (1,)), ((), ())), preferred_element_type=F32)


def _dot_tn(a, b):
    return lax.dot_general(_bf(a), _bf(b), (((0,), (0,)), ((), ())), preferred_element_type=F32)


def _dot_f32(a, b):
    return jnp.dot(a, b, preferred_element_type=F32, precision=HIGHEST)


def _dot_tn_f32(a, b):
    return lax.dot_general(a, b, (((0,), (0,)), ((), ())), preferred_element_type=F32, precision=HIGHEST)


def _sigmoid(x):
    return 1.0 / (1.0 + jnp.exp(-x))


def _silu(x):
    return x * _sigmoid(x)


def _segsum(x, ones_bd):
    hi = _bf(x)
    r1 = x - hi.astype(F32)
    mid = _bf(r1)
    lo = _bf(r1 - mid.astype(F32))
    acc = jnp.dot(hi, ones_bd, preferred_element_type=F32)
    acc += jnp.dot(mid, ones_bd, preferred_element_type=F32)
    acc += jnp.dot(lo, ones_bd, preferred_element_type=F32)
    return acc


def _block_ones(width, seg):
    i = jnp.arange(width) // seg
    return (i[:, None] == i[None, :]).astype(BF16)


def _modulate(x, g, modl_ref, modc_ref, row, pos0, ctx_len):
    y = x * lax.rsqrt(jnp.mean(x * x, axis=-1, keepdims=True) + NORM_EPS) * g
    pos = pos0 + lax.broadcasted_iota(jnp.int32, (x.shape[0], 1), 0)
    is_ctx = pos < ctx_len
    shift = jnp.where(is_ctx, modc_ref[0, row:row + 1, :], modl_ref[0, row:row + 1, :])
    scale = jnp.where(is_ctx, modc_ref[0, row + 1:row + 2, :], modl_ref[0, row + 1:row + 2, :])
    return y * (1.0 + scale) + shift


def _gate_vec(modl_ref, modc_ref, row, pos0, n, ctx_len):
    pos = pos0 + lax.broadcasted_iota(jnp.int32, (n, 1), 0)
    return jnp.where(pos < ctx_len, modc_ref[0, row:row + 1, :], modl_ref[0, row:row + 1, :])


def _mod_specs(d, ctx_row, batch_axis=0):
    def lat_map(*idx):
        return (idx[batch_axis], 0, 0)

    def ctx_map(*idx):
        return (ctx_row, 0, 0)

    return [pl.BlockSpec((1, N_MOD, d), lat_map), pl.BlockSpec((1, N_MOD, d), ctx_map)]


def _adaln_kernel(c_ref, w_ref, b_ref, o_ref):
    o_ref[...] = _dot(_silu(c_ref[...]), w_ref[...]) + b_ref[...]


def _adaln(cvec, w, b):
    r, d = cvec.shape
    n = w.shape[1]
    tn = 1024
    return pl.pallas_call(
        _adaln_kernel,
        grid=(n // tn,),
        in_specs=[pl.BlockSpec((r, d), lambda j: (0, 0)),
                  pl.BlockSpec((d, tn), lambda j: (0, j)),
                  pl.BlockSpec((1, tn), lambda j: (0, j))],
        out_specs=pl.BlockSpec((r, tn), lambda j: (0, j)),
        out_shape=jax.ShapeDtypeStruct((r, n), F32),
        compiler_params=_cparams(("parallel",)),
        name="adaln",
    )(cvec, w, b.reshape(1, n))


def _proj_kernel(x_ref, modl_ref, modc_ref, g_ref, w_ref, *rest, ctx_len, tm, splits, rope_cols):
    if rope_cols:
        wrot_ref, cos_ref, sin_ref = rest[:3]
        outs = rest[3:]
    else:
        outs = rest
    pos0 = pl.program_id(1) * tm
    h = _bf(_modulate(x_ref[0], g_ref[...], modl_ref, modc_ref, 0, pos0, ctx_len))
    y = jnp.dot(h, w_ref[...], preferred_element_type=F32)
    if rope_cols:
        yr = jnp.dot(h, wrot_ref[...], preferred_element_type=F32)
        roped = y[:, :rope_cols] * cos_ref[...] + yr * sin_ref[...]
        outs[0][0, :, :rope_cols] = roped
        outs[0][0, :, rope_cols:] = y[:, rope_cols:]
    else:
        lo = 0
        for o_ref, width in zip(outs, splits):
            o_ref[0] = y[:, lo:lo + width]
            lo += width


def _proj(xcat, mods, ctx_row, g, w, splits, ctx_len, rope=None):
    b, s, d = xcat.shape
    n = w.shape[1]
    tm = TOK_TILE
    in_specs = [pl.BlockSpec((1, tm, d), lambda i, j: (i, j, 0))] + _mod_specs(d, ctx_row) + [
        pl.BlockSpec((1, d), lambda i, j: (0, 0)),
        pl.BlockSpec((d, n), lambda i, j: (0, 0))]
    args = [xcat, mods, mods, g.reshape(1, d), w]
    rope_cols = 0
    if rope is not None:
        wrot, cos, sin = rope
        rope_cols = wrot.shape[1]
        in_specs += [pl.BlockSpec((d, rope_cols), lambda i, j: (0, 0)),
                     pl.BlockSpec((tm, rope_cols), lambda i, j: (j, 0)),
                     pl.BlockSpec((tm, rope_cols), lambda i, j: (j, 0))]
        args += [wrot, cos, sin]
    out_shape = [jax.ShapeDtypeStruct((b, s, width), F32) for width in splits]
    out_specs = [pl.BlockSpec((1, tm, width), lambda i, j: (i, j, 0)) for width in splits]
    return pl.pallas_call(
        functools.partial(_proj_kernel, ctx_len=ctx_len, tm=tm, splits=tuple(splits), rope_cols=rope_cols),
        grid=(b, s // tm),
        in_specs=in_specs,
        out_specs=out_specs,
        out_shape=out_shape,
        compiler_params=_cparams(("parallel", "parallel")),
        name="proj_rope" if rope_cols else "proj",
    )(*args)


def _softplus(x):
    return jnp.maximum(x, 0.0) + jnp.log(1.0 + jnp.exp(-jnp.abs(x)))


def _rwkv_feat_kernel(p_ref, pp_ref, pn_ref, mu_ref, w0_ref, wup_ref, a0_ref, aup_ref, gup_ref,
                      kk_ref, ka_ref, rk_ref, ones_ref, o_ref, *, ctx_len, seq_len, tm):
    pos0 = pl.program_id(1) * tm
    p = p_ref[0]
    rows = lax.broadcasted_iota(jnp.int32, (tm, 1), 0)
    pos = rows + pos0
    prev = jnp.where(rows == 0, pp_ref[0, 7:8, :], pltpu.roll(p, 1, 0))
    prev = jnp.where((pos == 0) | (pos == ctx_len), 0.0, prev)
    nxt = jnp.where(rows == tm - 1, pn_ref[0, 0:1, :], pltpu.roll(p, tm - 1, 0))
    nxt = jnp.where((pos == ctx_len - 1) | (pos == seq_len - 1), 0.0, nxt)
    p = p + mu_ref[...] * (0.5 * (prev + nxt) - p)

    r = p[:, 0:A_W]
    k = p[:, A_W:2 * A_W]
    v = p[:, 2 * A_W:3 * A_W]
    lo = 3 * A_W
    wd = p[:, lo:lo + DECAY_LORA]
    ad = p[:, lo + DECAY_LORA:lo + DECAY_LORA + AAA_LORA]
    gd = p[:, lo + DECAY_LORA + AAA_LORA:lo + DECAY_LORA + AAA_LORA + GATE_LORA]

    tw = jnp.tanh(wd)
    for d in range(2):
        w_log = -_softplus(-(w0_ref[d:d + 1, :] + _dot(tw, wup_ref[d]))) - 0.5
        o_ref[0, :, (SEC_LW0 + d) * A_W:(SEC_LW0 + d + 1) * A_W] = -jnp.exp(w_log)
    a = _sigmoid(a0_ref[...] + _dot(ad, aup_ref[...]))
    g = _dot(_sigmoid(gd), gup_ref[...])
    ones_bd = ones_ref[...]
    kk = k * kk_ref[...]
    kk = kk / jnp.maximum(jnp.sqrt(_segsum(kk * kk, ones_bd)), 1e-12)
    k = k * (1.0 + (a - 1.0) * ka_ref[...])
    bonus = _segsum(r * k * rk_ref[...], ones_bd) * v
    for sec, val in ((SEC_R, r), (SEC_K, k), (SEC_V, v), (SEC_KK, kk), (SEC_B, kk * a), (SEC_G, g),
                     (SEC_BONUS, bonus)):
        o_ref[0, :, sec * A_W:(sec + 1) * A_W] = val


def _rwkv_features(p, mu, w0, w_up, a0, a_up, g_up, k_k, k_a, r_k, ctx_len):
    b, s, c = p.shape
    tm = TOK_TILE
    nb8 = s // 8
    row = lambda a: a.reshape(1, -1)
    full = lambda a: pl.BlockSpec(a.shape, lambda i, j: (0,) * a.ndim)
    args = [row(mu), w0, w_up, row(a0), a_up, g_up, row(k_k), row(k_a), row(r_k), _block_ones(A_W, A_HD)]
    return pl.pallas_call(
        functools.partial(_rwkv_feat_kernel, ctx_len=ctx_len, seq_len=s, tm=tm),
        grid=(b, s // tm),
        in_specs=[pl.BlockSpec((1, tm, c), lambda i, j: (i, j, 0)),
                  pl.BlockSpec((1, 8, c), lambda i, j: (i, jnp.maximum(j * (tm // 8) - 1, 0), 0)),
                  pl.BlockSpec((1, 8, c), lambda i, j: (i, jnp.minimum((j + 1) * (tm // 8), nb8 - 1), 0)),
                  ] + [full(a) for a in args],
        out_specs=pl.BlockSpec((1, tm, N_SEC * A_W), lambda i, j: (i, j, 0)),
        out_shape=jax.ShapeDtypeStruct((b, s, N_SEC * A_W), F32),
        compiler_params=_cparams(("parallel", "parallel")),
        name="rwkv_features",
    )(p, p, p, *args)


def _chunk_order(d, j, n_ctx_chunks, n_chunks):
    back = jnp.where(j < n_ctx_chunks, n_ctx_chunks - 1 - j, n_chunks - 1 + n_ctx_chunks - j)
    return jnp.where(d == 0, j, back)


def _order_masks(rev):
    row = lax.broadcasted_iota(jnp.int32, (CHUNK, CHUNK), 0)
    col = lax.broadcasted_iota(jnp.int32, (CHUNK, CHUNK), 1)
    diff = jnp.where(rev, row - col, col - row)
    return diff < 0, diff <= 0, row == col


def _rwkv_chunk_kernel(r_ref, k_ref, v_ref, kk_ref, b_ref, lw_ref, o_ref, s_ref):
    rev = pl.program_id(1) == 1

    @pl.when(pl.program_id(2) == 0)
    def _():
        s_ref[...] = jnp.zeros_like(s_ref)

    strict, incl, eye = _order_masks(rev)
    lw = lw_ref[0]
    g_incl = _dot_f32(incl.astype(F32), lw)
    g_tot = jnp.sum(lw, axis=0, keepdims=True)
    e_g = jnp.exp(g_incl)
    e_ng = jnp.exp(-g_incl)
    e_tail = jnp.exp(g_tot - g_incl)
    kk = kk_ref[0]
    kap = kk * jnp.exp(g_incl - lw)
    rt = r_ref[0] * e_g
    kbar = k_ref[0] * e_ng
    bbar = b_ref[0] * e_ng
    kgam = k_ref[0] * e_tail
    bgam = b_ref[0] * e_tail
    v = v_ref[0]
    gam_t = jnp.exp(_dot_tn_f32(lw, jnp.ones((CHUNK, A_HD), F32)))
    eye_f = eye.astype(F32)

    for h in range(A_HEADS):
        sl = slice(h * A_HD, (h + 1) * A_HD)
        x = jnp.concatenate([kap[:, sl], rt[:, sl]], axis=0)
        qk = _dot_nt(x, kbar[:, sl])
        qb = _dot_nt(x, bbar[:, sl])
        a_vk = jnp.where(strict, qk[:CHUNK], 0.0)
        a_rk = jnp.where(incl, qk[CHUNK:], 0.0)
        n = jnp.where(strict, qb[:CHUNK], 0.0)
        a_rb = jnp.where(incl, qb[CHUNK:], 0.0)
        t_inv = eye_f - n
        pw = n
        for _ in range(5):
            pw = _dot_f32(pw, pw)
            t_inv = t_inv + _dot_f32(t_inv, pw)
        vh = v[:, sl]
        tk = _dot(t_inv, kap[:, sl])
        u = _dot(t_inv, _dot(a_vk, vh))
        s0 = s_ref[h]
        z = _dot(tk, s0) + u
        o_ref[0, 0, :, sl] = _dot(rt[:, sl], s0) + _dot(a_rk, vh) - _dot(a_rb, z)
        s_ref[h] = s0 * gam_t[sl, :] + _dot_tn(kgam[:, sl], vh) - _dot_tn(bgam[:, sl], z)


def _rwkv_scan(feats, ctx_len):
    b, s, _ = feats.shape
    nc, ncc = s // CHUNK, ctx_len // CHUNK

    def sec(idx):
        return pl.BlockSpec((1, CHUNK, A_W), lambda i, d, j: (i, _chunk_order(d, j, ncc, nc), idx))

    lw_spec = pl.BlockSpec((1, CHUNK, A_W), lambda i, d, j: (i, _chunk_order(d, j, ncc, nc), SEC_LW0 + d))
    return pl.pallas_call(
        _rwkv_chunk_kernel,
        grid=(b, 2, nc),
        in_specs=[sec(SEC_R), sec(SEC_K), sec(SEC_V), sec(SEC_KK), sec(SEC_B), lw_spec],
        out_specs=pl.BlockSpec((1, 1, CHUNK, A_W), lambda i, d, j: (i, d, _chunk_order(d, j, ncc, nc), 0)),
        out_shape=jax.ShapeDtypeStruct((b, 2, s, A_W), F32),
        scratch_shapes=[pltpu.VMEM((A_HEADS, A_HD, A_HD), F32)],
        compiler_params=_cparams(("parallel", "parallel", "arbitrary")),
        name="rwkv_scan",
    )(feats, feats, feats, feats, feats, feats)


def _hgrn_chunk_kernel(q_ref, i_ref, f_ref, lb_ref, o_ref, s_ref):
    rev = pl.program_id(1) == 1

    @pl.when(pl.program_id(2) == 0)
    def _():
        s_ref[...] = jnp.zeros_like(s_ref)

    _, incl, _ = _order_masks(rev)
    lb = lb_ref[0]
    f = lb + (1.0 - lb) * _sigmoid(f_ref[0])
    logf = jnp.log(f)
    kf = 1.0 - f
    g_incl = _dot_f32(incl.astype(F32), logf)
    g_tot = jnp.sum(logf, axis=0, keepdims=True)
    g_mid = g_incl[CHUNK // 2 - 1:CHUNK // 2, :]
    q = _silu(q_ref[0])
    q_in = q * jnp.exp(g_incl - g_mid)
    k_in = kf * jnp.exp(g_mid - g_incl)
    q_st = q * jnp.exp(g_incl)
    k_tail = kf * jnp.exp(g_tot - g_incl)
    v = i_ref[0]
    gam_t = jnp.exp(_dot_tn_f32(logf, jnp.ones((CHUNK, B_DK), F32)))
    for h in range(B_HEADS):
        sl = slice(h * B_DK, (h + 1) * B_DK)
        att = jnp.where(incl, _dot_nt(q_in[:, sl], k_in[:, sl]), 0.0)
        s0 = s_ref[h]
        o_ref[0, 0, :, sl] = _dot(att, v[:, sl]) + _dot(q_st[:, sl], s0)
        s_ref[h] = s0 * gam_t[sl, :] + _dot_tn(k_tail[:, sl], v[:, sl])


def _hgrn_scan(p, lb, ctx_len):
    b, s, _ = p.shape
    nc, ncc = s // CHUNK, ctx_len // CHUNK

    def sec(idx):
        return pl.BlockSpec((1, CHUNK, B_W), lambda i, d, j: (i, _chunk_order(d, j, ncc, nc), idx))

    return pl.pallas_call(
        _hgrn_chunk_kernel,
        grid=(b, 2, nc),
        in_specs=[sec(0), sec(1),
                  pl.BlockSpec((1, CHUNK, B_W), lambda i, d, j: (i, _chunk_order(d, j, ncc, nc), 2 + d)),
                  pl.BlockSpec((1, 1, B_W), lambda i, d, j: (d, 0, 0))],
        out_specs=pl.BlockSpec((1, 1, CHUNK, B_W), lambda i, d, j: (i, d, _chunk_order(d, j, ncc, nc), 0)),
        out_shape=jax.ShapeDtypeStruct((b, 2, s, B_W), F32),
        scratch_shapes=[pltpu.VMEM((B_HEADS, B_DK, B_DK), F32)],
        compiler_params=_cparams(("parallel", "parallel", "arbitrary")),
        name="hgrn_scan",
    )(p, p, p, lb)


def _rec_out_kernel(x_ref, modl_ref, modc_ref, oa_ref, g_ref, bonus_ref, ob_ref, gate_ref, lnw_ref, lnb_ref,
                    hgn_ref, ones_a_ref, ones_b_ref, w_ref, o_ref, *, ctx_len, tm):
    pos0 = pl.program_id(1) * tm
    oa = oa_ref[0, 0] + oa_ref[0, 1]
    ones_a = ones_a_ref[...]
    mean = _segsum(oa, ones_a) * (1.0 / A_HD)
    cen = oa - mean
    var = _segsum(cen * cen, ones_a) * (1.0 / A_HD)
    ya = (cen * lax.rsqrt(var + GN_EPS) * lnw_ref[...] + lnb_ref[...] + bonus_ref[0]) * g_ref[0]
    ob = ob_ref[0, 0] + ob_ref[0, 1]
    ms = _segsum(ob * ob, ones_b_ref[...]) * (1.0 / B_DK)
    yb = ob * lax.rsqrt(ms + NORM_EPS) * hgn_ref[...] * _silu(gate_ref[0])
    y = _dot(ya, w_ref[:A_W, :]) + _dot(yb, w_ref[A_W:, :])
    gate = _gate_vec(modl_ref, modc_ref, 2, pos0, tm, ctx_len)
    o_ref[0] = x_ref[0] + gate * y


def _rec_out(xcat, mods, ctx_row, oa, feats, ob, p_hgrn, ln_w, ln_b, hg_norm, w_out, ctx_len):
    b, s, d = xcat.shape
    tm = TOK_TILE
    row = lambda a: a.reshape(1, -1)
    full = lambda a: pl.BlockSpec(a.shape, lambda i, j: (0,) * a.ndim)
    consts = [row(ln_w), row(ln_b), row(jnp.tile(hg_norm, B_HEADS)), _block_ones(A_W, A_HD),
              _block_ones(B_W, B_DK), w_out]
    return pl.pallas_call(
        functools.partial(_rec_out_kernel, ctx_len=ctx_len, tm=tm),
        grid=(b, s // tm),
        in_specs=[pl.BlockSpec((1, tm, d), lambda i, j: (i, j, 0))] + _mod_specs(d, ctx_row) + [
            pl.BlockSpec((1, 2, tm, A_W), lambda i, j: (i, 0, j, 0)),
            pl.BlockSpec((1, tm, A_W), lambda i, j: (i, j, SEC_G)),
            pl.BlockSpec((1, tm, A_W), lambda i, j: (i, j, SEC_BONUS)),
            pl.BlockSpec((1, 2, tm, B_W), lambda i, j: (i, 0, j, 0)),
            pl.BlockSpec((1, tm, B_W), lambda i, j: (i, j, 4)),
        ] + [full(a) for a in consts],
        out_specs=pl.BlockSpec((1, tm, d), lambda i, j: (i, j, 0)),
        out_shape=jax.ShapeDtypeStruct((b, s, d), F32),
        compiler_params=_cparams(("parallel", "parallel")),
        name="rec_out",
    )(xcat, mods, mods, oa, feats, feats, ob, p_hgrn, *consts)


def _res_proj_kernel(x_ref, modl_ref, modc_ref, y_ref, w_ref, o_ref, *, ctx_len, tm):
    pos0 = pl.program_id(1) * tm
    gate = _gate_vec(modl_ref, modc_ref, 2, pos0, tm, ctx_len)
    o_ref[0] = x_ref[0] + gate * _dot(y_ref[0], w_ref[...])


def _res_proj(x, mods, ctx_row, y, w, ctx_len):
    b, s, d = x.shape
    k = y.shape[-1]
    tm = TOK_TILE
    return pl.pallas_call(
        functools.partial(_res_proj_kernel, ctx_len=ctx_len, tm=tm),
        grid=(b, s // tm),
        in_specs=[pl.BlockSpec((1, tm, d), lambda i, j: (i, j, 0))] + _mod_specs(d, ctx_row) + [
            pl.BlockSpec((1, tm, k), lambda i, j: (i, j, 0)),
            pl.BlockSpec((k, d), lambda i, j: (0, 0))],
        out_specs=pl.BlockSpec((1, tm, d), lambda i, j: (i, j, 0)),
        out_shape=jax.ShapeDtypeStruct((b, s, d), F32),
        compiler_params=_cparams(("parallel", "parallel")),
        name="res_proj",
    )(x, mods, mods, y, w)


def _ffn_kernel(x_ref, modl_ref, modc_ref, g_ref, wg_ref, wu_ref, wd_ref, *rest, ctx_len, tm, n_e, n_f,
                routed, final_norm):
    rest = list(rest)
    tokw_ref = rest.pop(0) if routed else None
    gfin_ref = rest.pop(0) if final_norm else None
    o_ref, h_ref, acc_ref = rest
    e = pl.program_id(2)
    f = pl.program_id(3)
    pos0 = pl.program_id(1) * tm

    @pl.when((e == 0) & (f == 0))
    def _():
        h_ref[...] = _bf(_modulate(x_ref[0], g_ref[...], modl_ref, modc_ref, 3, pos0, ctx_len))
        acc_ref[...] = jnp.zeros_like(acc_ref)

    h = h_ref[...]
    act = _silu(jnp.dot(h, wg_ref[0], preferred_element_type=F32)) * jnp.dot(h, wu_ref[0],
                                                                              preferred_element_type=F32)
    if routed:
        tw = tokw_ref[0]
        lane = lax.broadcasted_iota(jnp.int32, tw.shape, 1)
        act = act * jnp.sum(jnp.where(lane == e, tw, 0.0), axis=-1, keepdims=True)
    acc_ref[...] += _dot(act, wd_ref[0])

    @pl.when((e == n_e - 1) & (f == n_f - 1))
    def _():
        gate = _gate_vec(modl_ref, modc_ref, 5, pos0, tm, ctx_len)
        y = x_ref[0] + gate * acc_ref[...]
        if final_norm:
            y = y * lax.rsqrt(jnp.mean(y * y, axis=-1, keepdims=True) + NORM_EPS) * gfin_ref[...]
        o_ref[0] = y


def _ffn(x, mods, ctx_row, g, wg, wu, wd, ctx_len, tm, tf, tokw=None, final_g=None):
    b, s, d = x.shape
    n_e, _, ff = wg.shape
    n_f = ff // tf
    in_specs = [pl.BlockSpec((1, tm, d), lambda i, j, e, f: (i, j, 0))] + _mod_specs(d, ctx_row) + [
        pl.BlockSpec((1, d), lambda i, j, e, f: (0, 0)),
        pl.BlockSpec((1, d, tf), lambda i, j, e, f: (e, 0, f)),
        pl.BlockSpec((1, d, tf), lambda i, j, e, f: (e, 0, f)),
        pl.BlockSpec((1, tf, d), lambda i, j, e, f: (e, f, 0))]
    args = [x, mods, mods, g.reshape(1, d), wg, wu, wd]
    if tokw is not None:
        in_specs.append(pl.BlockSpec((1, tm, LANES), lambda i, j, e, f: (i, j, 0)))
        args.append(tokw)
    if final_g is not None:
        in_specs.append(pl.BlockSpec((1, d), lambda i, j, e, f: (0, 0)))
        args.append(final_g.reshape(1, d))
    return pl.pallas_call(
        functools.partial(_ffn_kernel, ctx_len=ctx_len, tm=tm, n_e=n_e, n_f=n_f, routed=tokw is not None,
                          final_norm=final_g is not None),
        grid=(b, s // tm, n_e, n_f),
        in_specs=in_specs,
        out_specs=pl.BlockSpec((1, tm, d), lambda i, j, e, f: (i, j, 0)),
        out_shape=jax.ShapeDtypeStruct((b, s, d), F32),
        scratch_shapes=[pltpu.VMEM((tm, d), BF16), pltpu.VMEM((tm, d), F32)],
        compiler_params=_cparams(("parallel", "parallel", "arbitrary", "arbitrary")),
        name="moe_ffn" if tokw is not None else "ffn",
    )(*args)


def _attn_kernel(sink_ref, q_ref, kc_ref, vc_ref, kp_ref, kq_ref, kn_ref, vp_ref, vq_ref, vn_ref, o_ref, *,
                 ctx_len, n_lat):
    i = pl.program_id(1)
    nk = ctx_len + 3 * ATT_BLOCK
    k_all = jnp.concatenate([kc_ref[0], kp_ref[0], kq_ref[0], kn_ref[0]], axis=0)
    v_all = jnp.concatenate([vc_ref[0], vp_ref[0], vq_ref[0], vn_ref[0]], axis=0)
    row = lax.broadcasted_iota(jnp.int32, (ATT_BLOCK, nk), 0)
    col = lax.broadcasted_iota(jnp.int32, (ATT_BLOCK, nk), 1)
    rel = col - (ctx_len + ATT_BLOCK)
    kabs = i * ATT_BLOCK + rel
    band = (jnp.abs(row - rel) <= WINDOW) & (kabs >= 0) & (kabs < n_lat)
    valid = (col < ctx_len) | band
    scale = HD ** -0.5
    for hk in range(HKV):
        kh = _bf(k_all[:, hk * HD:(hk + 1) * HD])
        vh = _bf(v_all[:, hk * HD:(hk + 1) * HD])
        for g in range(GQ):
            hq = hk * GQ + g
            qh = q_ref[0, :, hq * HD:(hq + 1) * HD]
            s = jnp.where(valid, _dot_nt(qh, kh) * scale, NEG_INF)
            sk = sink_ref[hq]
            m = jnp.maximum(jnp.max(s, axis=-1, keepdims=True), sk)
            p = jnp.exp(s - m)
            den = jnp.sum(p, axis=-1, keepdims=True) + jnp.exp(sk - m)
            o_ref[0, :, hq * HD:(hq + 1) * HD] = jnp.dot(_bf(p), vh, preferred_element_type=F32) / den


def _attention(qkv, sink, ctx_len):
    b, s, _ = qkv.shape
    n_lat = s - ctx_len
    nb = n_lat // ATT_BLOCK
    cb = ctx_len // ATT_BLOCK
    kcol = Q_COLS // KV_COLS
    vcol = kcol + 1

    def band(colblk, shift):
        return pl.BlockSpec((1, ATT_BLOCK, KV_COLS),
                            lambda bi, i: (bi, cb + jnp.clip(i + shift, 0, nb - 1), colblk))

    return pl.pallas_call(
        functools.partial(_attn_kernel, ctx_len=ctx_len, n_lat=n_lat),
        grid=(b, nb),
        in_specs=[pl.BlockSpec(memory_space=pltpu.SMEM),
                  pl.BlockSpec((1, ATT_BLOCK, Q_COLS), lambda bi, i: (bi, cb + i, 0)),
                  pl.BlockSpec((1, ctx_len, KV_COLS), lambda bi, i: (bi, 0, kcol)),
                  pl.BlockSpec((1, ctx_len, KV_COLS), lambda bi, i: (bi, 0, vcol)),
                  band(kcol, -1), band(kcol, 0), band(kcol, 1),
                  band(vcol, -1), band(vcol, 0), band(vcol, 1)],
        out_specs=pl.BlockSpec((1, ATT_BLOCK, Q_COLS), lambda bi, i: (bi, i, 0)),
        out_shape=jax.ShapeDtypeStruct((b, n_lat, Q_COLS), F32),
        compiler_params=_cparams(("parallel", "parallel")),
        name="attention",
    )(sink, qkv, qkv, qkv, qkv, qkv, qkv, qkv, qkv, qkv)


def _router_kernel(x_ref, modl_ref, modc_ref, g_ref, w_ref, b_ref, o_ref, *, tm):
    f = _modulate(x_ref[0], g_ref[...], modl_ref, modc_ref, 3, 0, 0)
    logits = _dot_f32(f, w_ref[...]) + b_ref[...]
    lane = lax.broadcasted_iota(jnp.int32, logits.shape, 1).astype(F32)
    logits = jnp.where(lane < N_EXPERTS, logits, NEG_INF)
    ex = jnp.exp(logits - jnp.max(logits, axis=-1, keepdims=True))
    probs = ex / jnp.sum(ex, axis=-1, keepdims=True)
    p1 = jnp.max(probs, axis=-1, keepdims=True)
    i1 = jnp.min(jnp.where(probs == p1, lane, float(LANES)), axis=-1, keepdims=True)
    rest = jnp.where(lane == i1, -1.0, probs)
    p2 = jnp.max(rest, axis=-1, keepdims=True)
    i2 = jnp.min(jnp.where(rest == p2, lane, float(LANES)), axis=-1, keepdims=True)
    tot = p1 + p2
    o_ref[0] = jnp.where(lane == i1, p1 / tot, 0.0) + jnp.where(lane == i2, p2 / tot, 0.0)


def _router(x, mods, g, w, bias):
    b, s, d = x.shape
    tm = TOK_TILE
    wpad = jnp.zeros((d, LANES), F32).at[:, :N_EXPERTS].set(w)
    bpad = jnp.zeros((1, LANES), F32).at[0, :N_EXPERTS].set(bias)
    return pl.pallas_call(
        functools.partial(_router_kernel, tm=tm),
        grid=(b, s // tm),
        in_specs=[pl.BlockSpec((1, tm, d), lambda i, j: (i, j, 0))] + _mod_specs(d, 0) + [
            pl.BlockSpec((1, d), lambda i, j: (0, 0)),
            pl.BlockSpec((d, LANES), lambda i, j: (0, 0)),
            pl.BlockSpec((1, LANES), lambda i, j: (0, 0))],
        out_specs=pl.BlockSpec((1, tm, LANES), lambda i, j: (i, j, 0)),
        out_shape=jax.ShapeDtypeStruct((b, s, LANES), F32),
        compiler_params=_cparams(("parallel", "parallel")),
        name="router",
    )(x, mods, mods, g.reshape(1, d), wpad, bpad)


def _rope_tables(n_lat, ctx_len):
    rows = n_lat // GRID_W
    row = jnp.repeat(jnp.arange(rows, dtype=F32), GRID_W)
    col = jnp.tile(jnp.arange(GRID_W, dtype=F32), rows)
    inv = ROPE_BASE ** (-jnp.arange(0, AX_DIM, 2, dtype=F32) / AX_DIM)
    ar, ac = row[:, None] * inv, col[:, None] * inv
    cos = jnp.concatenate([jnp.cos(ar), jnp.cos(ar), jnp.cos(ac), jnp.cos(ac)], axis=-1)
    sin = jnp.concatenate([-jnp.sin(ar), jnp.sin(ar), -jnp.sin(ac), jnp.sin(ac)], axis=-1)
    n_heads = ROPE_COLS // HD
    cos = jnp.concatenate([jnp.ones((ctx_len, HD), F32), cos], axis=0)
    sin = jnp.concatenate([jnp.zeros((ctx_len, HD), F32), sin], axis=0)
    half = AX_DIM // 2
    j = jnp.arange(HD)
    partner = jnp.where((j % AX_DIM) < half, j + half, j - half)
    perm = (jnp.arange(n_heads)[:, None] * HD + partner[None, :]).reshape(-1)
    return jnp.tile(cos, (1, n_heads)), jnp.tile(sin, (1, n_heads)), perm


def kernel(x, c, ctx, c_ctx, mod_w, mod_b, norm_mix, norm_ffn, norm_final, rec_w_in, rec_w_out, rwkv_mu, rwkv_w0, rwkv_w_up, rwkv_a0, rwkv_a_up, rwkv_g_up, rwkv_k_k, rwkv_k_a, rwkv_r_k, rwkv_ln_w, rwkv_ln_b, hgrn_lb, hgrn_norm, ffn_w_gate, ffn_w_up, ffn_w_down, att_w_in, att_w_out, att_sink, moe_router, moe_router_b, moe_w_gate, moe_w_up, moe_w_down):
    bsz, n_lat, d = x.shape
    ctx_len = ctx.shape[1]
    xcat = jnp.concatenate([ctx, x], axis=1)

    n_rows = -(-(bsz + 1) // 8) * 8
    cvec = jnp.zeros((n_rows, d), F32).at[:bsz].set(c).at[bsz].set(c_ctx)
    mods = [_adaln(cvec, mod_w[l], mod_b[l]).reshape(n_rows, N_MOD, d) for l in range(2)]

    p_rwkv, p_hgrn = _proj(xcat, mods[0], bsz, norm_mix[0], _bf(rec_w_in[0]), (RWKV_COLS, HGRN_COLS), ctx_len)
    feats = _rwkv_features(p_rwkv, rwkv_mu[0], rwkv_w0[0], rwkv_w_up[0], rwkv_a0[0], rwkv_a_up[0], rwkv_g_up[0],
                           rwkv_k_k[0], rwkv_k_a[0], rwkv_r_k[0].reshape(-1), ctx_len)
    oa = _rwkv_scan(feats, ctx_len)
    lb = jnp.cumsum(jax.nn.softmax(hgrn_lb.astype(F32), axis=1), axis=1)[:, 0].reshape(2, 1, B_W)
    ob = _hgrn_scan(p_hgrn, lb, ctx_len)
    xcat = _rec_out(xcat, mods[0], bsz, oa, feats, ob, p_hgrn, rwkv_ln_w[0], rwkv_ln_b[0], hgrn_norm[0],
                    _bf(rec_w_out[0]), ctx_len)
    xcat = _ffn(xcat, mods[0], bsz, norm_ffn[0], _bf(ffn_w_gate), _bf(ffn_w_up), _bf(ffn_w_down), ctx_len,
                tm=768, tf=1408)

    cos, sin, perm = _rope_tables(n_lat, ctx_len)
    w_att = att_w_in[0]
    (qkv,) = _proj(xcat, mods[1], bsz, norm_mix[1], _bf(w_att), (ATT_COLS,), ctx_len,
                   rope=(_bf(w_att[:, perm]), cos, sin))
    att = _attention(qkv, att_sink[0], ctx_len)
    x_lat = _res_proj(xcat[:, ctx_len:], mods[1], 0, att, _bf(att_w_out[0]), 0)
    tokw = _router(x_lat, mods[1], norm_ffn[1], moe_router[0], moe_router_b[0])
    return _ffn(x_lat, mods[1], 0, norm_ffn[1], _bf(moe_w_gate[0]), _bf(moe_w_up[0]), _bf(moe_w_down[0]), 0,
                tm=512, tf=1408, tokw=tokw, final_g=norm_final)
```

```python
import functools

import jax
import jax.numpy as jnp
from jax import lax
from jax.experimental import pallas as pl
from jax.experimental.pallas import tpu as pltpu

F32 = jnp.float32
BF16 = jnp.bfloat16
HIGHEST = lax.Precision.HIGHEST

N_MOD = 6
NORM_EPS = 1e-6
NEG_INF = -1e30

A_HEADS = 8
A_HD = 64
A_W = A_HEADS * A_HD
DECAY_LORA = 64
AAA_LORA = 64
GATE_LORA = 128
RWKV_COLS = 3 * A_W + DECAY_LORA + AAA_LORA + GATE_LORA
GN_EPS = 64e-5

B_HEADS = 4
B_DK = 128
B_W = B_HEADS * B_DK
HGRN_COLS = 5 * B_W

HQ = 16
HKV = 4
GQ = HQ // HKV
HD = 64
WINDOW = 128
ATT_BLOCK = 128
AX_DIM = HD // 2
ROPE_BASE = 10000.0
GRID_W = 64
Q_COLS = HQ * HD
KV_COLS = HKV * HD
ROPE_COLS = Q_COLS + KV_COLS
ATT_COLS = Q_COLS + 2 * KV_COLS

N_EXPERTS = 8
LANES = 128
CHUNK = 64
TOK_TILE = 256
VMEM_LIMIT = 56 * 1024 * 1024

SEC_R, SEC_K, SEC_V, SEC_KK, SEC_B, SEC_LW0, SEC_LW1, SEC_G, SEC_BONUS = range(9)
N_SEC = 9


def _cparams(sem):
    return pltpu.CompilerParams(dimension_semantics=sem, vmem_limit_bytes=VMEM_LIMIT)


def _bf(x):
    return x.astype(BF16)


def _dot(a, b):
    return jnp.dot(_bf(a), _bf(b), preferred_element_type=F32)


def _dot_nt(a, b):
    return lax.dot_general(_bf(a), _bf(b), (((1,), (1,)), ((), ())), preferred_element_type=F32)


def _dot_tn(a, b):
    return lax.dot_general(_bf(a), _bf(b), (((0,), (0,)), ((), ())), preferred_element_type=F32)


def _dot_f32(a, b):
    return jnp.dot(a, b, preferred_element_type=F32, precision=HIGHEST)


def _dot_tn_f32(a, b):
    return lax.dot_general(a, b, (((0,), (0,)), ((), ())), preferred_element_type=F32, precision=HIGHEST)


def _sigmoid(x):
    return 1.0 / (1.0 + jnp.exp(-x))


def _silu(x):
    return x * _sigmoid(x)


def _segsum(x, ones_bd):
    hi = _bf(x)
    r1 = x - hi.astype(F32)
    mid = _bf(r1)
    lo = _bf(r1 - mid.astype(F32))
    acc = jnp.dot(hi, ones_bd, preferred_element_type=F32)
    acc += jnp.dot(mid, ones_bd, preferred_element_type=F32)
    acc += jnp.dot(lo, ones_bd, preferred_element_type=F32)
    return acc


def _block_ones(width, seg):
    i = jnp.arange(width) // seg
    return (i[:, None] == i[None, :]).astype(BF16)


def _modulate(x, g, modl_ref, modc_ref, row, pos0, ctx_len):
    y = x * lax.rsqrt(jnp.mean(x * x, axis=-1, keepdims=True) + NORM_EPS) * g
    pos = pos0 + lax.broadcasted_iota(jnp.int32, (x.shape[0], 1), 0)
    is_ctx = pos < ctx_len
    shift = jnp.where(is_ctx, modc_ref[0, row:row + 1, :], modl_ref[0, row:row + 1, :])
    scale = jnp.where(is_ctx, modc_ref[0, row + 1:row + 2, :], modl_ref[0, row + 1:row + 2, :])
    return y * (1.0 + scale) + shift


def _gate_vec(modl_ref, modc_ref, row, pos0, n, ctx_len):
    pos = pos0 + lax.broadcasted_iota(jnp.int32, (n, 1), 0)
    return jnp.where(pos < ctx_len, modc_ref[0, row:row + 1, :], modl_ref[0, row:row + 1, :])


def _mod_specs(d, ctx_row, batch_axis=0):
    def lat_map(*idx):
        return (idx[batch_axis], 0, 0)

    def ctx_map(*idx):
        return (ctx_row, 0, 0)

    return [pl.BlockSpec((1, N_MOD, d), lat_map), pl.BlockSpec((1, N_MOD, d), ctx_map)]


def _adaln_kernel(c_ref, w_ref, b_ref, o_ref):
    o_ref[...] = _dot(_silu(c_ref[...]), w_ref[...]) + b_ref[...]


def _adaln(cvec, w, b):
    r, d = cvec.shape
    n = w.shape[1]
    tn = 1024
    return pl.pallas_call(
        _adaln_kernel,
        grid=(n // tn,),
        in_specs=[pl.BlockSpec((r, d), lambda j: (0, 0)),
                  pl.BlockSpec((d, tn), lambda j: (0, j)),
                  pl.BlockSpec((1, tn), lambda j: (0, j))],
        out_specs=pl.BlockSpec((r, tn), lambda j: (0, j)),
        out_shape=jax.ShapeDtypeStruct((r, n), F32),
        compiler_params=_cparams(("parallel",)),
        name="adaln",
    )(cvec, w, b.reshape(1, n))


def _proj_kernel(x_ref, modl_ref, modc_ref, g_ref, w_ref, *rest, ctx_len, tm, splits, rope_cols):
    if rope_cols:
        wrot_ref, cos_ref, sin_ref = rest[:3]
        outs = rest[3:]
    else:
        outs = rest
    pos0 = pl.program_id(1) * tm
    h = _bf(_modulate(x_ref[0], g_ref[...], modl_ref, modc_ref, 0, pos0, ctx_len))
    y = jnp.dot(h, w_ref[...], preferred_element_type=F32)
    if rope_cols:
        yr = jnp.dot(h, wrot_ref[...], preferred_element_type=F32)
        roped = y[:, :rope_cols] * cos_ref[...] + yr * sin_ref[...]
        outs[0][0, :, :rope_cols] = roped
        outs[0][0, :, rope_cols:] = y[:, rope_cols:]
    else:
        lo = 0
        for o_ref, width in zip(outs, splits):
            o_ref[0] = y[:, lo:lo + width]
            lo += width


def _proj(xcat, mods, ctx_row, g, w, splits, ctx_len, rope=None):
    b, s, d = xcat.shape
    n = w.shape[1]
    tm = TOK_TILE
    in_specs = [pl.BlockSpec((1, tm, d), lambda i, j: (i, j, 0))] + _mod_specs(d, ctx_row) + [
        pl.BlockSpec((1, d), lambda i, j: (0, 0)),
        pl.BlockSpec((d, n), lambda i, j: (0, 0))]
    args = [xcat, mods, mods, g.reshape(1, d), w]
    rope_cols = 0
    if rope is not None:
        wrot, cos, sin = rope
        rope_cols = wrot.shape[1]
        in_specs += [pl.BlockSpec((d, rope_cols), lambda i, j: (0, 0)),
                     pl.BlockSpec((tm, rope_cols), lambda i, j: (j, 0)),
                     pl.BlockSpec((tm, rope_cols), lambda i, j: (j, 0))]
        args += [wrot, cos, sin]
    out_shape = [jax.ShapeDtypeStruct((b, s, width), F32) for width in splits]
    out_specs = [pl.BlockSpec((1, tm, width), lambda i, j: (i, j, 0)) for width in splits]
    return pl.pallas_call(
        functools.partial(_proj_kernel, ctx_len=ctx_len, tm=tm, splits=tuple(splits), rope_cols=rope_cols),
        grid=(b, s // tm),
        in_specs=in_specs,
        out_specs=out_specs,
        out_shape=out_shape,
        compiler_params=_cparams(("parallel", "parallel")),
        name="proj_rope" if rope_cols else "proj",
    )(*args)


def _softplus(x):
    return jnp.maximum(x, 0.0) + jnp.log(1.0 + jnp.exp(-jnp.abs(x)))


def _rwkv_feat_kernel(p_ref, pp_ref, pn_ref, mu_ref, w0_ref, wup_ref, a0_ref, aup_ref, gup_ref,
                      kk_ref, ka_ref, rk_ref, ones_ref, o_ref, *, ctx_len, seq_len, tm):
    pos0 = pl.program_id(1) * tm
    p = p_ref[0]
    rows = lax.broadcasted_iota(jnp.int32, (tm, 1), 0)
    pos = rows + pos0
    prev = jnp.where(rows == 0, pp_ref[0, 7:8, :], pltpu.roll(p, 1, 0))
    prev = jnp.where((pos == 0) | (pos == ctx_len), 0.0, prev)
    nxt = jnp.where(rows == tm - 1, pn_ref[0, 0:1, :], pltpu.roll(p, tm - 1, 0))
    nxt = jnp.where((pos == ctx_len - 1) | (pos == seq_len - 1), 0.0, nxt)
    p = p + mu_ref[...] * (0.5 * (prev + nxt) - p)

    r = p[:, 0:A_W]
    k = p[:, A_W:2 * A_W]
    v = p[:, 2 * A_W:3 * A_W]
    lo = 3 * A_W
    wd = p[:, lo:lo + DECAY_LORA]
    ad = p[:, lo + DECAY_LORA:lo + DECAY_LORA + AAA_LORA]
    gd = p[:, lo + DECAY_LORA + AAA_LORA:lo + DECAY_LORA + AAA_LORA + GATE_LORA]

    tw = jnp.tanh(wd)
    for d in range(2):
        w_log = -_softplus(-(w0_ref[d:d + 1, :] + _dot(tw, wup_ref[d]))) - 0.5
        o_ref[0, :, (SEC_LW0 + d) * A_W:(SEC_LW0 + d + 1) * A_W] = -jnp.exp(w_log)
    a = _sigmoid(a0_ref[...] + _dot(ad, aup_ref[...]))
    g = _dot(_sigmoid(gd), gup_ref[...])
    ones_bd = ones_ref[...]
    kk = k * kk_ref[...]
    kk = kk / jnp.maximum(jnp.sqrt(_segsum(kk * kk, ones_bd)), 1e-12)
    k = k * (1.0 + (a - 1.0) * ka_ref[...])
    bonus = _segsum(r * k * rk_ref[...], ones_bd) * v
    for sec, val in ((SEC_R, r), (SEC_K, k), (SEC_V, v), (SEC_KK, kk), (SEC_B, kk * a), (SEC_G, g),
                     (SEC_BONUS, bonus)):
        o_ref[0, :, sec * A_W:(sec + 1) * A_W] = val


def _rwkv_features(p, mu, w0, w_up, a0, a_up, g_up, k_k, k_a, r_k, ctx_len):
    b, s, c = p.shape
    tm = TOK_TILE
    nb8 = s // 8
    row = lambda a: a.reshape(1, -1)
    full = lambda a: pl.BlockSpec(a.shape, lambda i, j: (0,) * a.ndim)
    args = [row(mu), w0, w_up, row(a0), a_up, g_up, row(k_k), row(k_a), row(r_k), _block_ones(A_W, A_HD)]
    return pl.pallas_call(
        functools.partial(_rwkv_feat_kernel, ctx_len=ctx_len, seq_len=s, tm=tm),
        grid=(b, s // tm),
        in_specs=[pl.BlockSpec((1, tm, c), lambda i, j: (i, j, 0)),
                  pl.BlockSpec((1, 8, c), lambda i, j: (i, jnp.maximum(j * (tm // 8) - 1, 0), 0)),
                  pl.BlockSpec((1, 8, c), lambda i, j: (i, jnp.minimum((j + 1) * (tm // 8), nb8 - 1), 0)),
                  ] + [full(a) for a in args],
        out_specs=pl.BlockSpec((1, tm, N_SEC * A_W), lambda i, j: (i, j, 0)),
        out_shape=jax.ShapeDtypeStruct((b, s, N_SEC * A_W), F32),
        compiler_params=_cparams(("parallel", "parallel")),
        name="rwkv_features",
    )(p, p, p, *args)


def _chunk_order(d, j, n_ctx_chunks, n_chunks):
    back = jnp.where(j < n_ctx_chunks, n_ctx_chunks - 1 - j, n_chunks - 1 + n_ctx_chunks - j)
    return jnp.where(d == 0, j, back)


def _order_masks(rev):
    row = lax.broadcasted_iota(jnp.int32, (CHUNK, CHUNK), 0)
    col = lax.broadcasted_iota(jnp.int32, (CHUNK, CHUNK), 1)
    diff = jnp.where(rev, row - col, col - row)
    return diff < 0, diff <= 0, row == col


def _rwkv_chunk_kernel(r_ref, k_ref, v_ref, kk_ref, b_ref, lw_ref, o_ref, s_ref):
    rev = pl.program_id(1) == 1

    @pl.when(pl.program_id(2) == 0)
    def _():
        s_ref[...] = jnp.zeros_like(s_ref)

    strict, incl, eye = _order_masks(rev)
    lw = lw_ref[0]
    g_incl = _dot_f32(incl.astype(F32), lw)
    g_tot = jnp.sum(lw, axis=0, keepdims=True)
    e_g = jnp.exp(g_incl)
    e_ng = jnp.exp(-g_incl)
    e_tail = jnp.exp(g_tot - g_incl)
    kk = kk_ref[0]
    kap = kk * jnp.exp(g_incl - lw)
    rt = r_ref[0] * e_g
    kbar = k_ref[0] * e_ng
    bbar = b_ref[0] * e_ng
    kgam = k_ref[0] * e_tail
    bgam = b_ref[0] * e_tail
    v = v_ref[0]
    gam_t = jnp.exp(_dot_tn_f32(lw, jnp.ones((CHUNK, A_HD), F32)))
    eye_f = eye.astype(F32)

    heads = range(A_HEADS)
    sls = [slice(h * A_HD, (h + 1) * A_HD) for h in heads]
    x = [jnp.concatenate([kap[:, sl], rt[:, sl]], axis=0) for sl in sls]
    qb = [_dot_nt(x[h], bbar[:, sls[h]]) for h in heads]
    n = [jnp.where(strict, qb[h][:CHUNK], 0.0) for h in heads]
    t_inv = [eye_f - n[h] for h in heads]
    pw = n
    for _ in range(5):
        pw = [_dot(p, p) for p in pw]
        t_inv = [t + _dot(t, p) for t, p in zip(t_inv, pw)]
    qk = [_dot_nt(x[h], kbar[:, sls[h]]) for h in heads]
    vh = [v[:, sl] for sl in sls]
    w1 = [_dot(jnp.where(strict, qk[h][:CHUNK], 0.0), vh[h]) for h in heads]
    tk = [_dot(t_inv[h], kap[:, sls[h]]) for h in heads]
    u = [_dot(t_inv[h], w1[h]) for h in heads]
    s0 = [s_ref[h] for h in heads]
    z = [_dot(tk[h], s0[h]) + u[h] for h in heads]
    for h in heads:
        a_rk = jnp.where(incl, qk[h][CHUNK:], 0.0)
        a_rb = jnp.where(incl, qb[h][CHUNK:], 0.0)
        o_ref[0, 0, :, sls[h]] = _dot(rt[:, sls[h]], s0[h]) + _dot(a_rk, vh[h]) - _dot(a_rb, z[h])
    for h in heads:
        s_ref[h] = s0[h] * gam_t[sls[h], :] + _dot_tn(kgam[:, sls[h]], vh[h]) - _dot_tn(bgam[:, sls[h]], z[h])


def _rwkv_scan(feats, ctx_len):
    b, s, _ = feats.shape
    nc, ncc = s // CHUNK, ctx_len // CHUNK

    def sec(idx):
        return pl.BlockSpec((1, CHUNK, A_W), lambda i, d, j: (i, _chunk_order(d, j, ncc, nc), idx))

    lw_spec = pl.BlockSpec((1, CHUNK, A_W), lambda i, d, j: (i, _chunk_order(d, j, ncc, nc), SEC_LW0 + d))
    return pl.pallas_call(
        _rwkv_chunk_kernel,
        grid=(b, 2, nc),
        in_specs=[sec(SEC_R), sec(SEC_K), sec(SEC_V), sec(SEC_KK), sec(SEC_B), lw_spec],
        out_specs=pl.BlockSpec((1, 1, CHUNK, A_W), lambda i, d, j: (i, d, _chunk_order(d, j, ncc, nc), 0)),
        out_shape=jax.ShapeDtypeStruct((b, 2, s, A_W), F32),
        scratch_shapes=[pltpu.VMEM((A_HEADS, A_HD, A_HD), F32)],
        compiler_params=_cparams(("parallel", "parallel", "arbitrary")),
        name="rwkv_scan",
    )(feats, feats, feats, feats, feats, feats)


def _hgrn_chunk_kernel(q_ref, i_ref, f_ref, lb_ref, o_ref, s_ref):
    rev = pl.program_id(1) == 1

    @pl.when(pl.program_id(2) == 0)
    def _():
        s_ref[...] = jnp.zeros_like(s_ref)

    _, incl, _ = _order_masks(rev)
    lb = lb_ref[0]
    f = lb + (1.0 - lb) * _sigmoid(f_ref[0])
    logf = jnp.log(f)
    kf = 1.0 - f
    g_incl = _dot_f32(incl.astype(F32), logf)
    g_tot = jnp.sum(logf, axis=0, keepdims=True)
    g_mid = g_incl[CHUNK // 2 - 1:CHUNK // 2, :]
    q = _silu(q_ref[0])
    q_in = q * jnp.exp(g_incl - g_mid)
    k_in = kf * jnp.exp(g_mid - g_incl)
    q_st = q * jnp.exp(g_incl)
    k_tail = kf * jnp.exp(g_tot - g_incl)
    v = i_ref[0]
    gam_t = jnp.exp(_dot_tn_f32(logf, jnp.ones((CHUNK, B_DK), F32)))
    for h in range(B_HEADS):
        sl = slice(h * B_DK, (h + 1) * B_DK)
        att = jnp.where(incl, _dot_nt(q_in[:, sl], k_in[:, sl]), 0.0)
        s0 = s_ref[h]
        o_ref[0, 0, :, sl] = _dot(att, v[:, sl]) + _dot(q_st[:, sl], s0)
        s_ref[h] = s0 * gam_t[sl, :] + _dot_tn(k_tail[:, sl], v[:, sl])


def _hgrn_scan(p, lb, ctx_len):
    b, s, _ = p.shape
    nc, ncc = s // CHUNK, ctx_len // CHUNK

    def sec(idx):
        return pl.BlockSpec((1, CHUNK, B_W), lambda i, d, j: (i, _chunk_order(d, j, ncc, nc), idx))

    return pl.pallas_call(
        _hgrn_chunk_kernel,
        grid=(b, 2, nc),
        in_specs=[sec(0), sec(1),
                  pl.BlockSpec((1, CHUNK, B_W), lambda i, d, j: (i, _chunk_order(d, j, ncc, nc), 2 + d)),
                  pl.BlockSpec((1, 1, B_W), lambda i, d, j: (d, 0, 0))],
        out_specs=pl.BlockSpec((1, 1, CHUNK, B_W), lambda i, d, j: (i, d, _chunk_order(d, j, ncc, nc), 0)),
        out_shape=jax.ShapeDtypeStruct((b, 2, s, B_W), F32),
        scratch_shapes=[pltpu.VMEM((B_HEADS, B_DK, B_DK), F32)],
        compiler_params=_cparams(("parallel", "parallel", "arbitrary")),
        name="hgrn_scan",
    )(p, p, p, lb)


def _rec_out_kernel(x_ref, modl_ref, modc_ref, oa_ref, g_ref, bonus_ref, ob_ref, gate_ref, lnw_ref, lnb_ref,
                    hgn_ref, ones_a_ref, ones_b_ref, w_ref, o_ref, *, ctx_len, tm):
    pos0 = pl.program_id(1) * tm
    oa = oa_ref[0, 0] + oa_ref[0, 1]
    ones_a = ones_a_ref[...]
    mean = _segsum(oa, ones_a) * (1.0 / A_HD)
    cen = oa - mean
    var = _segsum(cen * cen, ones_a) * (1.0 / A_HD)
    ya = (cen * lax.rsqrt(var + GN_EPS) * lnw_ref[...] + lnb_ref[...] + bonus_ref[0]) * g_ref[0]
    ob = ob_ref[0, 0] + ob_ref[0, 1]
    ms = _segsum(ob * ob, ones_b_ref[...]) * (1.0 / B_DK)
    yb = ob * lax.rsqrt(ms + NORM_EPS) * hgn_ref[...] * _silu(gate_ref[0])
    y = _dot(ya, w_ref[:A_W, :]) + _dot(yb, w_ref[A_W:, :])
    gate = _gate_vec(modl_ref, modc_ref, 2, pos0, tm, ctx_len)
    o_ref[0] = x_ref[0] + gate * y


def _rec_out(xcat, mods, ctx_row, oa, feats, ob, p_hgrn, ln_w, ln_b, hg_norm, w_out, ctx_len):
    b, s, d = xcat.shape
    tm = TOK_TILE
    row = lambda a: a.reshape(1, -1)
    full = lambda a: pl.BlockSpec(a.shape, lambda i, j: (0,) * a.ndim)
    consts = [row(ln_w), row(ln_b), row(jnp.tile(hg_norm, B_HEADS)), _block_ones(A_W, A_HD),
              _block_ones(B_W, B_DK), w_out]
    return pl.pallas_call(
        functools.partial(_rec_out_kernel, ctx_len=ctx_len, tm=tm),
        grid=(b, s // tm),
        in_specs=[pl.BlockSpec((1, tm, d), lambda i, j: (i, j, 0))] + _mod_specs(d, ctx_row) + [
            pl.BlockSpec((1, 2, tm, A_W), lambda i, j: (i, 0, j, 0)),
            pl.BlockSpec((1, tm, A_W), lambda i, j: (i, j, SEC_G)),
            pl.BlockSpec((1, tm, A_W), lambda i, j: (i, j, SEC_BONUS)),
            pl.BlockSpec((1, 2, tm, B_W), lambda i, j: (i, 0, j, 0)),
            pl.BlockSpec((1, tm, B_W), lambda i, j: (i, j, 4)),
        ] + [full(a) for a in consts],
        out_specs=pl.BlockSpec((1, tm, d), lambda i, j: (i, j, 0)),
        out_shape=jax.ShapeDtypeStruct((b, s, d), F32),
        compiler_params=_cparams(("parallel", "parallel")),
        name="rec_out",
    )(xcat, mods, mods, oa, feats, feats, ob, p_hgrn, *consts)


def _res_proj_kernel(x_ref, modl_ref, modc_ref, y_ref, w_ref, o_ref, *, ctx_len, tm):
    pos0 = pl.program_id(1) * tm
    gate = _gate_vec(modl_ref, modc_ref, 2, pos0, tm, ctx_len)
    o_ref[0] = x_ref[0] + gate * _dot(y_ref[0], w_ref[...])


def _res_proj(x, mods, ctx_row, y, w, ctx_len):
    b, s, d = x.shape
    k = y.shape[-1]
    tm = TOK_TILE
    return pl.pallas_call(
        functools.partial(_res_proj_kernel, ctx_len=ctx_len, tm=tm),
        grid=(b, s // tm),
        in_specs=[pl.BlockSpec((1, tm, d), lambda i, j: (i, j, 0))] + _mod_specs(d, ctx_row) + [
            pl.BlockSpec((1, tm, k), lambda i, j: (i, j, 0)),
            pl.BlockSpec((k, d), lambda i, j: (0, 0))],
        out_specs=pl.BlockSpec((1, tm, d), lambda i, j: (i, j, 0)),
        out_shape=jax.ShapeDtypeStruct((b, s, d), F32),
        compiler_params=_cparams(("parallel", "parallel")),
        name="res_proj",
    )(x, mods, mods, y, w)


def _ffn_kernel(x_ref, modl_ref, modc_ref, g_ref, wg_ref, wu_ref, wd_ref, o_ref, h_ref, acc_ref, *, ctx_len, tm,
                n_f):
    f = pl.program_id(2)
    pos0 = pl.program_id(1) * tm

    @pl.when(f == 0)
    def _():
        h_ref[...] = _bf(_modulate(x_ref[0], g_ref[...], modl_ref, modc_ref, 3, pos0, ctx_len))
        acc_ref[...] = jnp.zeros_like(acc_ref)

    h = h_ref[...]
    act = _silu(jnp.dot(h, wg_ref[...], preferred_element_type=F32)) * jnp.dot(h, wu_ref[...],
                                                                               preferred_element_type=F32)
    acc_ref[...] += _dot(act, wd_ref[...])

    @pl.when(f == n_f - 1)
    def _():
        o_ref[0] = x_ref[0] + _gate_vec(modl_ref, modc_ref, 5, pos0, tm, ctx_len) * acc_ref[...]


def _ffn(x, mods, ctx_row, g, wg, wu, wd, ctx_len, tm, tf):
    b, s, d = x.shape
    ff = wg.shape[1]
    n_f = ff // tf
    return pl.pallas_call(
        functools.partial(_ffn_kernel, ctx_len=ctx_len, tm=tm, n_f=n_f),
        grid=(b, s // tm, n_f),
        in_specs=[pl.BlockSpec((1, tm, d), lambda i, j, f: (i, j, 0))] + _mod_specs(d, ctx_row) + [
            pl.BlockSpec((1, d), lambda i, j, f: (0, 0)),
            pl.BlockSpec((d, tf), lambda i, j, f: (0, f)),
            pl.BlockSpec((d, tf), lambda i, j, f: (0, f)),
            pl.BlockSpec((tf, d), lambda i, j, f: (f, 0))],
        out_specs=pl.BlockSpec((1, tm, d), lambda i, j, f: (i, j, 0)),
        out_shape=jax.ShapeDtypeStruct((b, s, d), F32),
        scratch_shapes=[pltpu.VMEM((tm, d), BF16), pltpu.VMEM((tm, d), F32)],
        compiler_params=_cparams(("parallel", "parallel", "arbitrary")),
        name="ffn",
    )(x, mods, mods, g.reshape(1, d), wg, wu, wd)


def _attn_kernel(sink_ref, q_ref, kc_ref, vc_ref, kp_ref, kq_ref, kn_ref, vp_ref, vq_ref, vn_ref, o_ref, *,
                 ctx_len, n_lat):
    i = pl.program_id(1)
    nk = ctx_len + 3 * ATT_BLOCK
    k_all = jnp.concatenate([kc_ref[0], kp_ref[0], kq_ref[0], kn_ref[0]], axis=0)
    v_all = jnp.concatenate([vc_ref[0], vp_ref[0], vq_ref[0], vn_ref[0]], axis=0)
    row = lax.broadcasted_iota(jnp.int32, (ATT_BLOCK, nk), 0)
    col = lax.broadcasted_iota(jnp.int32, (ATT_BLOCK, nk), 1)
    rel = col - (ctx_len + ATT_BLOCK)
    kabs = i * ATT_BLOCK + rel
    band = (jnp.abs(row - rel) <= WINDOW) & (kabs >= 0) & (kabs < n_lat)
    valid = (col < ctx_len) | band
    scale = HD ** -0.5
    for hk in range(HKV):
        kh = _bf(k_all[:, hk * HD:(hk + 1) * HD])
        vh = _bf(v_all[:, hk * HD:(hk + 1) * HD])
        for g in range(GQ):
            hq = hk * GQ + g
            qh = q_ref[0, :, hq * HD:(hq + 1) * HD]
            s = jnp.where(valid, _dot_nt(qh, kh) * scale, NEG_INF)
            sk = sink_ref[hq]
            m = jnp.maximum(jnp.max(s, axis=-1, keepdims=True), sk)
            p = jnp.exp(s - m)
            den = jnp.sum(p, axis=-1, keepdims=True) + jnp.exp(sk - m)
            o_ref[0, :, hq * HD:(hq + 1) * HD] = jnp.dot(_bf(p), vh, preferred_element_type=F32) / den


def _attention(qkv, sink, ctx_len):
    b, s, _ = qkv.shape
    n_lat = s - ctx_len
    nb = n_lat // ATT_BLOCK
    cb = ctx_len // ATT_BLOCK
    kcol = Q_COLS // KV_COLS
    vcol = kcol + 1

    def band(colblk, shift):
        return pl.BlockSpec((1, ATT_BLOCK, KV_COLS),
                            lambda bi, i: (bi, cb + jnp.clip(i + shift, 0, nb - 1), colblk))

    return pl.pallas_call(
        functools.partial(_attn_kernel, ctx_len=ctx_len, n_lat=n_lat),
        grid=(b, nb),
        in_specs=[pl.BlockSpec(memory_space=pltpu.SMEM),
                  pl.BlockSpec((1, ATT_BLOCK, Q_COLS), lambda bi, i: (bi, cb + i, 0)),
                  pl.BlockSpec((1, ctx_len, KV_COLS), lambda bi, i: (bi, 0, kcol)),
                  pl.BlockSpec((1, ctx_len, KV_COLS), lambda bi, i: (bi, 0, vcol)),
                  band(kcol, -1), band(kcol, 0), band(kcol, 1),
                  band(vcol, -1), band(vcol, 0), band(vcol, 1)],
        out_specs=pl.BlockSpec((1, ATT_BLOCK, Q_COLS), lambda bi, i: (bi, i, 0)),
        out_shape=jax.ShapeDtypeStruct((b, n_lat, Q_COLS), F32),
        compiler_params=_cparams(("parallel", "parallel")),
        name="attention",
    )(sink, qkv, qkv, qkv, qkv, qkv, qkv, qkv, qkv, qkv)


def _router_kernel(x_ref, modl_ref, modc_ref, g_ref, w_ref, b_ref, o_ref, e_ref, f_ref, *, tm):
    f = _modulate(x_ref[0], g_ref[...], modl_ref, modc_ref, 3, 0, 0)
    f_ref[0] = f
    logits = _dot_f32(f, w_ref[...]) + b_ref[...]
    lane = lax.broadcasted_iota(jnp.int32, logits.shape, 1).astype(F32)
    logits = jnp.where(lane < N_EXPERTS, logits, NEG_INF)
    ex = jnp.exp(logits - jnp.max(logits, axis=-1, keepdims=True))
    probs = ex / jnp.sum(ex, axis=-1, keepdims=True)
    p1 = jnp.max(probs, axis=-1, keepdims=True)
    i1 = jnp.min(jnp.where(probs == p1, lane, float(LANES)), axis=-1, keepdims=True)
    rest = jnp.where(lane == i1, -1.0, probs)
    p2 = jnp.max(rest, axis=-1, keepdims=True)
    i2 = jnp.min(jnp.where(rest == p2, lane, float(LANES)), axis=-1, keepdims=True)
    tot = p1 + p2
    o_ref[0] = jnp.where(lane == 0.0, p1 / tot, jnp.where(lane == 1.0, p2 / tot, 0.0))
    e_ref[0] = jnp.where(lane == 0.0, i1, jnp.where(lane == 1.0, i2, 0.0)).astype(jnp.int32)


def _router(x, mods, g, w, bias):
    b, s, d = x.shape
    tm = TOK_TILE
    wpad = jnp.zeros((d, LANES), F32).at[:, :N_EXPERTS].set(w)
    bpad = jnp.zeros((1, LANES), F32).at[0, :N_EXPERTS].set(bias)
    lane_spec = pl.BlockSpec((1, tm, LANES), lambda i, j: (i, j, 0))
    return pl.pallas_call(
        functools.partial(_router_kernel, tm=tm),
        grid=(b, s // tm),
        in_specs=[pl.BlockSpec((1, tm, d), lambda i, j: (i, j, 0))] + _mod_specs(d, 0) + [
            pl.BlockSpec((1, d), lambda i, j: (0, 0)),
            pl.BlockSpec((d, LANES), lambda i, j: (0, 0)),
            pl.BlockSpec((1, LANES), lambda i, j: (0, 0))],
        out_specs=[lane_spec, lane_spec, pl.BlockSpec((1, tm, d), lambda i, j: (i, j, 0))],
        out_shape=[jax.ShapeDtypeStruct((b, s, LANES), F32), jax.ShapeDtypeStruct((b, s, LANES), jnp.int32),
                   jax.ShapeDtypeStruct((b, s, d), F32)],
        compiler_params=_cparams(("parallel", "parallel")),
        name="router",
    )(x, mods, mods, g.reshape(1, d), wpad, bpad)


MOE_ROWS = 256


def _row_gather_kernel(src_ref, f_hbm, o_ref, sem, *, rows):
    base = pl.program_id(0) * rows

    def issue(r, carry):
        pltpu.make_async_copy(f_hbm.at[pl.ds(src_ref[base + r], 1)], o_ref.at[pl.ds(r, 1)], sem).start()
        return carry

    lax.fori_loop(0, rows, issue, 0, unroll=8)
    pltpu.make_async_copy(f_hbm.at[pl.ds(0, rows)], o_ref, sem).wait()


def _row_gather(f, src):
    n_rows = src.shape[0]
    d = f.shape[1]
    rows = MOE_ROWS
    return pl.pallas_call(
        functools.partial(_row_gather_kernel, rows=rows),
        grid_spec=pltpu.PrefetchScalarGridSpec(
            num_scalar_prefetch=1,
            grid=(n_rows // rows,),
            in_specs=[pl.BlockSpec(memory_space=pl.ANY)],
            out_specs=pl.BlockSpec((rows, d), lambda i, src_ref: (i, 0)),
            scratch_shapes=[pltpu.SemaphoreType.DMA(())]),
        out_shape=jax.ShapeDtypeStruct((n_rows, d), f.dtype),
        compiler_params=_cparams(("arbitrary",)),
        name="moe_gather",
    )(src, f)


def _expert_kernel(be_ref, x_ref, wg_ref, wu_ref, wd_ref, o_ref):
    h = _bf(x_ref[...])
    act = _silu(jnp.dot(h, wg_ref[0], preferred_element_type=F32)) * jnp.dot(h, wu_ref[0],
                                                                              preferred_element_type=F32)
    o_ref[...] = _dot(act, wd_ref[0])


def _experts(xs, block_e, wg, wu, wd):
    n_rows, d = xs.shape
    ff = wg.shape[2]
    rows = MOE_ROWS
    return pl.pallas_call(
        _expert_kernel,
        grid_spec=pltpu.PrefetchScalarGridSpec(
            num_scalar_prefetch=1,
            grid=(n_rows // rows,),
            in_specs=[pl.BlockSpec((rows, d), lambda i, be: (i, 0)),
                      pl.BlockSpec((1, d, ff), lambda i, be: (be[i], 0, 0)),
                      pl.BlockSpec((1, d, ff), lambda i, be: (be[i], 0, 0)),
                      pl.BlockSpec((1, ff, d), lambda i, be: (be[i], 0, 0))],
            out_specs=pl.BlockSpec((rows, d), lambda i, be: (i, 0))),
        out_shape=jax.ShapeDtypeStruct((n_rows, d), F32),
        compiler_params=_cparams(("arbitrary",)),
        name="moe_experts",
    )(block_e, xs, wg, wu, wd)


def _combine_kernel(dest_ref, x_ref, modl_ref, modc_ref, w_ref, gfin_ref, ys_hbm, o_ref, y1_ref, y2_ref, sem, *,
                    tm, tiles_per_row):
    base = (pl.program_id(0) * tiles_per_row + pl.program_id(1)) * tm

    def issue(r, carry):
        slot = 2 * (base + r)
        pltpu.make_async_copy(ys_hbm.at[pl.ds(dest_ref[slot], 1)], y1_ref.at[pl.ds(r, 1)], sem).start()
        pltpu.make_async_copy(ys_hbm.at[pl.ds(dest_ref[slot + 1], 1)], y2_ref.at[pl.ds(r, 1)], sem).start()
        return carry

    lax.fori_loop(0, tm, issue, 0, unroll=8)
    pltpu.make_async_copy(ys_hbm.at[pl.ds(0, tm)], y1_ref, sem).wait()
    pltpu.make_async_copy(ys_hbm.at[pl.ds(0, tm)], y2_ref, sem).wait()
    w = w_ref[0]
    moe = w[:, 0:1] * y1_ref[...] + w[:, 1:2] * y2_ref[...]
    y = x_ref[0] + _gate_vec(modl_ref, modc_ref, 5, 0, tm, 0) * moe
    o_ref[0] = y * lax.rsqrt(jnp.mean(y * y, axis=-1, keepdims=True) + NORM_EPS) * gfin_ref[...]


def _combine(x, mods, tokw, dest, ys, final_g):
    b, s, d = x.shape
    tm = TOK_TILE
    tiles = s // tm
    return pl.pallas_call(
        functools.partial(_combine_kernel, tm=tm, tiles_per_row=tiles),
        grid_spec=pltpu.PrefetchScalarGridSpec(
            num_scalar_prefetch=1,
            grid=(b, tiles),
            in_specs=[pl.BlockSpec((1, tm, d), lambda i, j, dr: (i, j, 0)),
                      pl.BlockSpec((1, N_MOD, d), lambda i, j, dr: (i, 0, 0)),
                      pl.BlockSpec((1, N_MOD, d), lambda i, j, dr: (0, 0, 0)),
                      pl.BlockSpec((1, tm, LANES), lambda i, j, dr: (i, j, 0)),
                      pl.BlockSpec((1, d), lambda i, j, dr: (0, 0)),
                      pl.BlockSpec(memory_space=pl.ANY)],
            out_specs=pl.BlockSpec((1, tm, d), lambda i, j, dr: (i, j, 0)),
            scratch_shapes=[pltpu.VMEM((tm, d), F32), pltpu.VMEM((tm, d), F32), pltpu.SemaphoreType.DMA(())]),
        out_shape=jax.ShapeDtypeStruct((b, s, d), F32),
        compiler_params=_cparams(("arbitrary", "arbitrary")),
        name="moe_combine",
    )(dest, x, mods, mods, tokw, final_g.reshape(1, d), ys)


def _moe_routing(eidx):
    m = eidx.shape[0]
    oh = (eidx[:, None] == jnp.arange(N_EXPERTS, dtype=jnp.int32)[None, :]).astype(jnp.int32)
    csum = jnp.cumsum(oh, axis=0)
    rank = jnp.sum((csum - oh) * oh, axis=1)
    counts = csum[-1]
    padded = (counts + MOE_ROWS - 1) // MOE_ROWS * MOE_ROWS
    pad_end = jnp.cumsum(padded)
    dest = (pad_end - padded)[eidx] + rank
    n_blocks = m // MOE_ROWS + N_EXPERTS
    block_e = jnp.minimum(jnp.searchsorted(pad_end, jnp.arange(n_blocks, dtype=jnp.int32) * MOE_ROWS, side='right'),
                          N_EXPERTS - 1).astype(jnp.int32)
    src = jnp.zeros((n_blocks * MOE_ROWS,), jnp.int32).at[dest].set(jnp.arange(m, dtype=jnp.int32) // 2)
    return dest.astype(jnp.int32), src, block_e


def _rope_tables(n_lat, ctx_len):
    rows = n_lat // GRID_W
    row = jnp.repeat(jnp.arange(rows, dtype=F32), GRID_W)
    col = jnp.tile(jnp.arange(GRID_W, dtype=F32), rows)
    inv = ROPE_BASE ** (-jnp.arange(0, AX_DIM, 2, dtype=F32) / AX_DIM)
    ar, ac = row[:, None] * inv, col[:, None] * inv
    cos = jnp.concatenate([jnp.cos(ar), jnp.cos(ar), jnp.cos(ac), jnp.cos(ac)], axis=-1)
    sin = jnp.concatenate([-jnp.sin(ar), jnp.sin(ar), -jnp.sin(ac), jnp.sin(ac)], axis=-1)
    n_heads = ROPE_COLS // HD
    cos = jnp.concatenate([jnp.ones((ctx_len, HD), F32), cos], axis=0)
    sin = jnp.concatenate([jnp.zeros((ctx_len, HD), F32), sin], axis=0)
    half = AX_DIM // 2
    j = jnp.arange(HD)
    partner = jnp.where((j % AX_DIM) < half, j + half, j - half)
    perm = (jnp.arange(n_heads)[:, None] * HD + partner[None, :]).reshape(-1)
    return jnp.tile(cos, (1, n_heads)), jnp.tile(sin, (1, n_heads)), perm


def kernel(x, c, ctx, c_ctx, mod_w, mod_b, norm_mix, norm_ffn, norm_final, rec_w_in, rec_w_out, rwkv_mu, rwkv_w0, rwkv_w_up, rwkv_a0, rwkv_a_up, rwkv_g_up, rwkv_k_k, rwkv_k_a, rwkv_r_k, rwkv_ln_w, rwkv_ln_b, hgrn_lb, hgrn_norm, ffn_w_gate, ffn_w_up, ffn_w_down, att_w_in, att_w_out, att_sink, moe_router, moe_router_b, moe_w_gate, moe_w_up, moe_w_down):
    bsz, n_lat, d = x.shape
    ctx_len = ctx.shape[1]
    xcat = jnp.concatenate([ctx, x], axis=1)

    n_rows = -(-(bsz + 1) // 8) * 8
    cvec = jnp.zeros((n_rows, d), F32).at[:bsz].set(c).at[bsz].set(c_ctx)
    mods = [_adaln(cvec, mod_w[l], mod_b[l]).reshape(n_rows, N_MOD, d) for l in range(2)]

    p_rwkv, p_hgrn = _proj(xcat, mods[0], bsz, norm_mix[0], _bf(rec_w_in[0]), (RWKV_COLS, HGRN_COLS), ctx_len)
    feats = _rwkv_features(p_rwkv, rwkv_mu[0], rwkv_w0[0], rwkv_w_up[0], rwkv_a0[0], rwkv_a_up[0], rwkv_g_up[0],
                           rwkv_k_k[0], rwkv_k_a[0], rwkv_r_k[0].reshape(-1), ctx_len)
    oa = _rwkv_scan(feats, ctx_len)
    lb = jnp.cumsum(jax.nn.softmax(hgrn_lb.astype(F32), axis=1), axis=1)[:, 0].reshape(2, 1, B_W)
    ob = _hgrn_scan(p_hgrn, lb, ctx_len)
    xcat = _rec_out(xcat, mods[0], bsz, oa, feats, ob, p_hgrn, rwkv_ln_w[0], rwkv_ln_b[0], hgrn_norm[0],
                    _bf(rec_w_out[0]), ctx_len)
    xcat = _ffn(xcat, mods[0], bsz, norm_ffn[0], _bf(ffn_w_gate[0]), _bf(ffn_w_up[0]), _bf(ffn_w_down[0]), ctx_len,
                tm=768, tf=1408)

    cos, sin, perm = _rope_tables(n_lat, ctx_len)
    w_att = att_w_in[0]
    (qkv,) = _proj(xcat, mods[1], bsz, norm_mix[1], _bf(w_att), (ATT_COLS,), ctx_len,
                   rope=(_bf(w_att[:, perm]), cos, sin))
    att = _attention(qkv, att_sink[0], ctx_len)
    x_lat = _res_proj(xcat[:, ctx_len:], mods[1], 0, att, _bf(att_w_out[0]), 0)
    tokw, eidx, f_lat = _router(x_lat, mods[1], norm_ffn[1], moe_router[0], moe_router_b[0])
    dest, src, block_e = _moe_routing(eidx[..., :2].reshape(-1))
    xs = _row_gather(f_lat.reshape(bsz * n_lat, d), src)
    ys = _experts(xs, block_e, _bf(moe_w_gate[0]), _bf(moe_w_up[0]), _bf(moe_w_down[0]))
    return _combine(x_lat, mods[1], tokw, dest, ys, norm_final)
```

```python
import functools

import jax
import jax.numpy as jnp
from jax import lax
from jax.experimental import pallas as pl
from jax.experimental.pallas import tpu as pltpu

F32 = jnp.float32
BF16 = jnp.bfloat16
HIGHEST = lax.Precision.HIGHEST

N_MOD = 6
NORM_EPS = 1e-6
NEG_INF = -1e30

A_HEADS = 8
A_HD = 64
A_W = A_HEADS * A_HD
DECAY_LORA = 64
AAA_LORA = 64
GATE_LORA = 128
RWKV_COLS = 3 * A_W + DECAY_LORA + AAA_LORA + GATE_LORA
GN_EPS = 64e-5

B_HEADS = 4
B_DK = 128
B_W = B_HEADS * B_DK
HGRN_COLS = 5 * B_W

HQ = 16
HKV = 4
GQ = HQ // HKV
HD = 64
WINDOW = 128
ATT_BLOCK = 128
AX_DIM = HD // 2
ROPE_BASE = 10000.0
GRID_W = 64
Q_COLS = HQ * HD
KV_COLS = HKV * HD
ROPE_COLS = Q_COLS + KV_COLS
ATT_COLS = Q_COLS + 2 * KV_COLS

N_EXPERTS = 8
LANES = 128
CHUNK = 64
SCAN_BLOCK = 256
MXU_WIDTH = 256
RWKV_GROUP = MXU_WIDTH // A_HD
TOK_TILE = 256
VMEM_LIMIT = 56 * 1024 * 1024

SEC_R, SEC_K, SEC_V, SEC_KK, SEC_B, SEC_LW0, SEC_LW1, SEC_G, SEC_BONUS = range(9)
N_SEC = 9


def _cparams(sem):
    return pltpu.CompilerParams(dimension_semantics=sem, vmem_limit_bytes=VMEM_LIMIT)


def _bf(x):
    return x.astype(BF16)


def _dot(a, b):
    return jnp.dot(_bf(a), _bf(b), preferred_element_type=F32)


def _dot_nt(a, b):
    return lax.dot_general(_bf(a), _bf(b), (((1,), (1,)), ((), ())), preferred_element_type=F32)


def _dot_tn(a, b):
    return lax.dot_general(_bf(a), _bf(b), (((0,), (0,)), ((), ())), preferred_element_type=F32)


def _dot_f32(a, b):
    return jnp.dot(a, b, preferred_element_type=F32, precision=HIGHEST)


def _dot_tn_f32(a, b):
    return lax.dot_general(a, b, (((0,), (0,)), ((), ())), preferred_element_type=F32, precision=HIGHEST)


def _sigmoid(x):
    return 1.0 / (1.0 + jnp.exp(-x))


def _silu(x):
    return x * _sigmoid(x)


def _segsum(x, ones_bd):
    hi = _bf(x)
    r1 = x - hi.astype(F32)
    mid = _bf(r1)
    lo = _bf(r1 - mid.astype(F32))
    acc = jnp.dot(hi, ones_bd, preferred_element_type=F32)
    acc += jnp.dot(mid, ones_bd, preferred_element_type=F32)
    acc += jnp.dot(lo, ones_bd, preferred_element_type=F32)
    return acc


def _block_ones(width, seg):
    i = jnp.arange(width) // seg
    return (i[:, None] == i[None, :]).astype(BF16)


def _modulate(x, g, modl_ref, modc_ref, row, pos0, ctx_len):
    y = x * lax.rsqrt(jnp.mean(x * x, axis=-1, keepdims=True) + NORM_EPS) * g
    pos = pos0 + lax.broadcasted_iota(jnp.int32, (x.shape[0], 1), 0)
    is_ctx = pos < ctx_len
    shift = jnp.where(is_ctx, modc_ref[0, row:row + 1, :], modl_ref[0, row:row + 1, :])
    scale = jnp.where(is_ctx, modc_ref[0, row + 1:row + 2, :], modl_ref[0, row + 1:row + 2, :])
    return y * (1.0 + scale) + shift


def _gate_vec(modl_ref, modc_ref, row, pos0, n, ctx_len):
    pos = pos0 + lax.broadcasted_iota(jnp.int32, (n, 1), 0)
    return jnp.where(pos < ctx_len, modc_ref[0, row:row + 1, :], modl_ref[0, row:row + 1, :])


def _mod_specs(d, ctx_row, batch_axis=0):
    def lat_map(*idx):
        return (idx[batch_axis], 0, 0)

    def ctx_map(*idx):
        return (ctx_row, 0, 0)

    return [pl.BlockSpec((1, N_MOD, d), lat_map), pl.BlockSpec((1, N_MOD, d), ctx_map)]


def _adaln_kernel(c_ref, w_ref, b_ref, o_ref):
    o_ref[...] = _dot(_silu(c_ref[...]), w_ref[...]) + b_ref[...]


def _adaln(cvec, w, b):
    r, d = cvec.shape
    n = w.shape[1]
    tn = 1024
    return pl.pallas_call(
        _adaln_kernel,
        grid=(n // tn,),
        in_specs=[pl.BlockSpec((r, d), lambda j: (0, 0)),
                  pl.BlockSpec((d, tn), lambda j: (0, j)),
                  pl.BlockSpec((1, tn), lambda j: (0, j))],
        out_specs=pl.BlockSpec((r, tn), lambda j: (0, j)),
        out_shape=jax.ShapeDtypeStruct((r, n), F32),
        compiler_params=_cparams(("parallel",)),
        name="adaln",
    )(cvec, w, b.reshape(1, n))


def _proj_kernel(x_ref, modl_ref, modc_ref, g_ref, w_ref, *rest, ctx_len, tm, splits, rope_cols):
    if rope_cols:
        wrot_ref, cos_ref, sin_ref = rest[:3]
        outs = rest[3:]
    else:
        outs = rest
    pos0 = pl.program_id(1) * tm
    h = _bf(_modulate(x_ref[0], g_ref[...], modl_ref, modc_ref, 0, pos0, ctx_len))
    y = jnp.dot(h, w_ref[...], preferred_element_type=F32)
    if rope_cols:
        yr = jnp.dot(h, wrot_ref[...], preferred_element_type=F32)
        roped = y[:, :rope_cols] * cos_ref[...] + yr * sin_ref[...]
        outs[0][0, :, :rope_cols] = roped
        outs[0][0, :, rope_cols:] = y[:, rope_cols:]
    else:
        lo = 0
        for o_ref, width in zip(outs, splits):
            o_ref[0] = y[:, lo:lo + width]
            lo += width


def _proj(xcat, mods, ctx_row, g, w, splits, ctx_len, rope=None):
    b, s, d = xcat.shape
    n = w.shape[1]
    tm = TOK_TILE
    in_specs = [pl.BlockSpec((1, tm, d), lambda i, j: (i, j, 0))] + _mod_specs(d, ctx_row) + [
        pl.BlockSpec((1, d), lambda i, j: (0, 0)),
        pl.BlockSpec((d, n), lambda i, j: (0, 0))]
    args = [xcat, mods, mods, g.reshape(1, d), w]
    rope_cols = 0
    if rope is not None:
        wrot, cos, sin = rope
        rope_cols = wrot.shape[1]
        in_specs += [pl.BlockSpec((d, rope_cols), lambda i, j: (0, 0)),
                     pl.BlockSpec((tm, rope_cols), lambda i, j: (j, 0)),
                     pl.BlockSpec((tm, rope_cols), lambda i, j: (j, 0))]
        args += [wrot, cos, sin]
    out_shape = [jax.ShapeDtypeStruct((b, s, width), F32) for width in splits]
    out_specs = [pl.BlockSpec((1, tm, width), lambda i, j: (i, j, 0)) for width in splits]
    return pl.pallas_call(
        functools.partial(_proj_kernel, ctx_len=ctx_len, tm=tm, splits=tuple(splits), rope_cols=rope_cols),
        grid=(b, s // tm),
        in_specs=in_specs,
        out_specs=out_specs,
        out_shape=out_shape,
        compiler_params=_cparams(("parallel", "parallel")),
        name="proj_rope" if rope_cols else "proj",
    )(*args)


def _softplus(x):
    return jnp.maximum(x, 0.0) + jnp.log(1.0 + jnp.exp(-jnp.abs(x)))


def _rwkv_feat_kernel(p_ref, pp_ref, pn_ref, mu_ref, w0_ref, wup_ref, a0_ref, aup_ref, gup_ref,
                      kk_ref, ka_ref, rk_ref, ones_ref, o_ref, *, ctx_len, seq_len, tm):
    pos0 = pl.program_id(1) * tm
    p = p_ref[0]
    rows = lax.broadcasted_iota(jnp.int32, (tm, 1), 0)
    pos = rows + pos0
    prev = jnp.where(rows == 0, pp_ref[0, 7:8, :], pltpu.roll(p, 1, 0))
    prev = jnp.where((pos == 0) | (pos == ctx_len), 0.0, prev)
    nxt = jnp.where(rows == tm - 1, pn_ref[0, 0:1, :], pltpu.roll(p, tm - 1, 0))
    nxt = jnp.where((pos == ctx_len - 1) | (pos == seq_len - 1), 0.0, nxt)
    p = p + mu_ref[...] * (0.5 * (prev + nxt) - p)

    r = p[:, 0:A_W]
    k = p[:, A_W:2 * A_W]
    v = p[:, 2 * A_W:3 * A_W]
    lo = 3 * A_W
    wd = p[:, lo:lo + DECAY_LORA]
    ad = p[:, lo + DECAY_LORA:lo + DECAY_LORA + AAA_LORA]
    gd = p[:, lo + DECAY_LORA + AAA_LORA:lo + DECAY_LORA + AAA_LORA + GATE_LORA]

    tw = jnp.tanh(wd)
    for d in range(2):
        w_log = -_softplus(-(w0_ref[d:d + 1, :] + _dot(tw, wup_ref[d]))) - 0.5
        o_ref[0, :, (SEC_LW0 + d) * A_W:(SEC_LW0 + d + 1) * A_W] = -jnp.exp(w_log)
    a = _sigmoid(a0_ref[...] + _dot(ad, aup_ref[...]))
    g = _dot(_sigmoid(gd), gup_ref[...])
    ones_bd = ones_ref[...]
    kk = k * kk_ref[...]
    kk = kk / jnp.maximum(jnp.sqrt(_segsum(kk * kk, ones_bd)), 1e-12)
    k = k * (1.0 + (a - 1.0) * ka_ref[...])
    bonus = _segsum(r * k * rk_ref[...], ones_bd) * v
    for sec, val in ((SEC_R, r), (SEC_K, k), (SEC_V, v), (SEC_KK, kk), (SEC_B, kk * a), (SEC_G, g),
                     (SEC_BONUS, bonus)):
        o_ref[0, :, sec * A_W:(sec + 1) * A_W] = val


def _rwkv_features(p, mu, w0, w_up, a0, a_up, g_up, k_k, k_a, r_k, ctx_len):
    b, s, c = p.shape
    tm = TOK_TILE
    nb8 = s // 8
    row = lambda a: a.reshape(1, -1)
    full = lambda a: pl.BlockSpec(a.shape, lambda i, j: (0,) * a.ndim)
    args = [row(mu), w0, w_up, row(a0), a_up, g_up, row(k_k), row(k_a), row(r_k), _block_ones(A_W, A_HD)]
    return pl.pallas_call(
        functools.partial(_rwkv_feat_kernel, ctx_len=ctx_len, seq_len=s, tm=tm),
        grid=(b, s // tm),
        in_specs=[pl.BlockSpec((1, tm, c), lambda i, j: (i, j, 0)),
                  pl.BlockSpec((1, 8, c), lambda i, j: (i, jnp.maximum(j * (tm // 8) - 1, 0), 0)),
                  pl.BlockSpec((1, 8, c), lambda i, j: (i, jnp.minimum((j + 1) * (tm // 8), nb8 - 1), 0)),
                  ] + [full(a) for a in args],
        out_specs=pl.BlockSpec((1, tm, N_SEC * A_W), lambda i, j: (i, j, 0)),
        out_shape=jax.ShapeDtypeStruct((b, s, N_SEC * A_W), F32),
        compiler_params=_cparams(("parallel", "parallel")),
        name="rwkv_features",
    )(p, p, p, *args)


def _chunk_order(d, j, n_ctx_chunks, n_chunks):
    back = jnp.where(j < n_ctx_chunks, n_ctx_chunks - 1 - j, n_chunks - 1 + n_ctx_chunks - j)
    return jnp.where(d == 0, j, back)


def _order_masks(rev):
    row = lax.broadcasted_iota(jnp.int32, (CHUNK, CHUNK), 0)
    col = lax.broadcasted_iota(jnp.int32, (CHUNK, CHUNK), 1)
    diff = jnp.where(rev, row - col, col - row)
    return diff < 0, diff <= 0, row == col


def _dot_split(a, x):
    hi = _bf(x)
    r1 = x - hi.astype(F32)
    mid = _bf(r1)
    lo = _bf(r1 - mid.astype(F32))
    acc = jnp.dot(a, hi, preferred_element_type=F32)
    acc += jnp.dot(a, mid, preferred_element_type=F32)
    acc += jnp.dot(a, lo, preferred_element_type=F32)
    return acc


def _for_each_chunk(rev, body):
    n = SCAN_BLOCK // CHUNK

    def step(i, carry):
        c = jnp.where(rev, n - 1 - i, i)
        body(pl.multiple_of(c * CHUNK, CHUNK))
        return carry

    lax.fori_loop(0, n, step, 0)


def _rwkv_chunk_kernel(r_ref, k_ref, v_ref, kk_ref, b_ref, lw_ref, o_ref, s_ref):
    rev = pl.program_id(1) == 1

    @pl.when(pl.program_id(2) == 0)
    def _():
        s_ref[...] = jnp.zeros_like(s_ref)

    _, incl, _ = _order_masks(rev)
    incl_bf = jnp.where(incl, 1.0, 0.0).astype(BF16)
    gw = RWKV_GROUP * A_HD
    groups = range(A_HEADS // RWKV_GROUP)
    gsl = [slice(g * gw, (g + 1) * gw) for g in groups]
    row = lax.broadcasted_iota(jnp.int32, (CHUNK, gw), 0)
    col = lax.broadcasted_iota(jnp.int32, (CHUNK, gw), 1) % A_HD
    diff = jnp.where(rev, row - col, col - row)
    strict_c = diff < 0
    incl_c = diff <= 0
    eye_c = jnp.where(diff == 0, 1.0, 0.0)
    bd_mask = (lax.broadcasted_iota(jnp.int32, (gw, gw), 0) // A_HD
               == lax.broadcasted_iota(jnp.int32, (gw, gw), 1) // A_HD)

    def bd(x):
        return jnp.where(bd_mask, jnp.concatenate([x] * RWKV_GROUP, axis=0), 0.0)

    def stack(x):
        return jnp.concatenate([x[:, h * A_HD:(h + 1) * A_HD] for h in range(RWKV_GROUP)], axis=0)

    n = SCAN_BLOCK // CHUNK
    rows = [pl.ds(pl.multiple_of(jnp.where(rev, (n - 1 - i) * CHUNK, i * CHUNK), CHUNK), CHUNK) for i in range(n)]
    units = [(i, g) for i in range(n) for g in groups]
    kap, rt, kbar, bbar, kgam, bgam, gam, v = ({} for _ in range(8))
    for i in range(n):
        lw = lw_ref[0, rows[i], :]
        g_incl = _dot_split(incl_bf, lw)
        g_tot = jnp.sum(lw, axis=0, keepdims=True)
        e_ng = jnp.exp(-g_incl)
        e_tail = jnp.exp(g_tot - g_incl)
        k = k_ref[0, rows[i], :]
        b = b_ref[0, rows[i], :]
        kap_i = kk_ref[0, rows[i], :] * jnp.exp(g_incl - lw)
        rt_i = r_ref[0, rows[i], :] * jnp.exp(g_incl)
        v_i = v_ref[0, rows[i], :]
        for g in groups:
            kap[i, g], rt[i, g], v[i, g] = kap_i[:, gsl[g]], rt_i[:, gsl[g]], v_i[:, gsl[g]]
            kbar[i, g], bbar[i, g] = (k * e_ng)[:, gsl[g]], (b * e_ng)[:, gsl[g]]
            kgam[i, g], bgam[i, g] = (k * e_tail)[:, gsl[g]], (b * e_tail)[:, gsl[g]]
            gam[i, g] = jnp.exp(g_tot)[:, gsl[g]]

    x = {u: jnp.concatenate([kap[u], rt[u]], axis=0) for u in units}
    yk = {u: _dot_nt(x[u], bd(kbar[u])) for u in units}
    yb = {u: _dot_nt(x[u], bd(bbar[u])) for u in units}
    a = {u: jnp.where(strict_c, -yb[u][:CHUNK], 0.0) for u in units}
    xs = {u: eye_c + a[u] for u in units}
    pw = {u: _dot(a[u], bd(a[u])) for u in units}
    for _ in range(4):
        st = {u: _dot(jnp.concatenate([pw[u], xs[u]], axis=0), bd(pw[u])) for u in units}
        pw = {u: st[u][:CHUNK] for u in units}
        xs = {u: xs[u] + st[u][CHUNK:] for u in units}
    t_inv = {u: xs[u] + _dot(xs[u], bd(pw[u])) for u in units}
    bd_v = {u: _bf(bd(v[u])) for u in units}
    w1 = {u: _dot(jnp.where(strict_c, yk[u][:CHUNK], 0.0), bd_v[u]) for u in units}
    tk = {u: _dot(t_inv[u], bd(kap[u])) for u in units}
    uu = {u: _dot(t_inv[u], bd(w1[u])) for u in units}
    o_loc = {u: _dot(jnp.where(incl_c, yk[u][CHUNK:], 0.0), bd_v[u]) for u in units}
    a_rb = {u: jnp.where(incl_c, yb[u][CHUNK:], 0.0) for u in units}
    bd_bg = {u: _bf(bd(bgam[u])) for u in units}
    bd_m2 = {u: _bf(bd(_dot_tn(stack(tk[u]), bd_bg[u]))) for u in units}
    cc = {u: _dot_tn(stack(v[u]), bd(kgam[u])) - _dot_tn(stack(uu[u]), bd_bg[u]) for u in units}
    tr = {u: jnp.concatenate([tk[u], rt[u]], axis=0) for u in units}

    s = [s_ref[g] for g in groups]
    for i in range(n):
        for g in groups:
            u = (i, g)
            zr = _dot_nt(tr[u], bd(s[g]))
            z = zr[:CHUNK] + uu[u]
            o_ref[0, 0, rows[i], gsl[g]] = zr[CHUNK:] + o_loc[u] - _dot(a_rb[u], bd(z))
            s[g] = s[g] * gam[u] - _dot(s[g], bd_m2[u]) + cc[u]
    for g in groups:
        s_ref[g] = s[g]


def _rwkv_scan(feats, ctx_len):
    b, s, _ = feats.shape
    nc, ncc = s // SCAN_BLOCK, ctx_len // SCAN_BLOCK

    def sec(idx):
        return pl.BlockSpec((1, SCAN_BLOCK, A_W), lambda i, d, j: (i, _chunk_order(d, j, ncc, nc), idx))

    lw_spec = pl.BlockSpec((1, SCAN_BLOCK, A_W), lambda i, d, j: (i, _chunk_order(d, j, ncc, nc), SEC_LW0 + d))
    return pl.pallas_call(
        _rwkv_chunk_kernel,
        grid=(b, 2, nc),
        in_specs=[sec(SEC_R), sec(SEC_K), sec(SEC_V), sec(SEC_KK), sec(SEC_B), lw_spec],
        out_specs=pl.BlockSpec((1, 1, SCAN_BLOCK, A_W), lambda i, d, j: (i, d, _chunk_order(d, j, ncc, nc), 0)),
        out_shape=jax.ShapeDtypeStruct((b, 2, s, A_W), F32),
        scratch_shapes=[pltpu.VMEM((A_HEADS // RWKV_GROUP, A_HD, RWKV_GROUP * A_HD), F32)],
        compiler_params=_cparams(("parallel", "parallel", "arbitrary")),
        name="rwkv_scan",
    )(feats, feats, feats, feats, feats, feats)


def _hgrn_chunk_kernel(q_ref, i_ref, f_ref, lb_ref, o_ref, s_ref):
    rev = pl.program_id(1) == 1

    @pl.when(pl.program_id(2) == 0)
    def _():
        s_ref[...] = jnp.zeros_like(s_ref)

    _, incl, _ = _order_masks(rev)
    incl_bf = jnp.where(incl, 1.0, 0.0).astype(BF16)
    lb = lb_ref[0]
    heads = range(B_HEADS)
    sls = [slice(h * B_DK, (h + 1) * B_DK) for h in heads]

    def chunk(off):
        rows = pl.ds(off, CHUNK)
        f = lb + (1.0 - lb) * _sigmoid(f_ref[0, rows, :])
        logf = jnp.log(f)
        kf = 1.0 - f
        g_incl = _dot_split(incl_bf, logf)
        g_tot = jnp.sum(logf, axis=0, keepdims=True)
        g_mid = g_incl[CHUNK // 2 - 1:CHUNK // 2, :]
        q = _silu(q_ref[0, rows, :])
        q_in = q * jnp.exp(g_incl - g_mid)
        k_in = kf * jnp.exp(g_mid - g_incl)
        q_st = q * jnp.exp(g_incl)
        k_tail = kf * jnp.exp(g_tot - g_incl)
        gam = jnp.exp(g_tot)
        v = i_ref[0, rows, :]
        att = [jnp.where(incl, _dot_nt(q_in[:, sl], k_in[:, sl]), 0.0) for sl in sls]
        s0 = [s_ref[h] for h in heads]
        for h in heads:
            o_ref[0, 0, rows, sls[h]] = _dot(att[h], v[:, sls[h]]) + _dot_nt(q_st[:, sls[h]], s0[h])
        for h in heads:
            s_ref[h] = s0[h] * gam[:, sls[h]] + _dot_tn(v[:, sls[h]], k_tail[:, sls[h]])

    _for_each_chunk(rev, chunk)


def _hgrn_scan(p, lb, ctx_len):
    b, s, _ = p.shape
    nc, ncc = s // SCAN_BLOCK, ctx_len // SCAN_BLOCK

    def sec(idx):
        return pl.BlockSpec((1, SCAN_BLOCK, B_W), lambda i, d, j: (i, _chunk_order(d, j, ncc, nc), idx))

    return pl.pallas_call(
        _hgrn_chunk_kernel,
        grid=(b, 2, nc),
        in_specs=[sec(0), sec(1),
                  pl.BlockSpec((1, SCAN_BLOCK, B_W), lambda i, d, j: (i, _chunk_order(d, j, ncc, nc), 2 + d)),
                  pl.BlockSpec((1, 1, B_W), lambda i, d, j: (d, 0, 0))],
        out_specs=pl.BlockSpec((1, 1, SCAN_BLOCK, B_W), lambda i, d, j: (i, d, _chunk_order(d, j, ncc, nc), 0)),
        out_shape=jax.ShapeDtypeStruct((b, 2, s, B_W), F32),
        scratch_shapes=[pltpu.VMEM((B_HEADS, B_DK, B_DK), F32)],
        compiler_params=_cparams(("parallel", "parallel", "arbitrary")),
        name="hgrn_scan",
    )(p, p, p, lb)


def _rec_out_kernel(x_ref, modl_ref, modc_ref, oa_ref, g_ref, bonus_ref, ob_ref, gate_ref, lnw_ref, lnb_ref,
                    hgn_ref, ones_a_ref, ones_b_ref, w_ref, o_ref, *, ctx_len, tm):
    pos0 = pl.program_id(1) * tm
    oa = oa_ref[0, 0] + oa_ref[0, 1]
    ones_a = ones_a_ref[...]
    mean = _segsum(oa, ones_a) * (1.0 / A_HD)
    cen = oa - mean
    var = _segsum(cen * cen, ones_a) * (1.0 / A_HD)
    ya = (cen * lax.rsqrt(var + GN_EPS) * lnw_ref[...] + lnb_ref[...] + bonus_ref[0]) * g_ref[0]
    ob = ob_ref[0, 0] + ob_ref[0, 1]
    ms = _segsum(ob * ob, ones_b_ref[...]) * (1.0 / B_DK)
    yb = ob * lax.rsqrt(ms + NORM_EPS) * hgn_ref[...] * _silu(gate_ref[0])
    y = _dot(ya, w_ref[:A_W, :]) + _dot(yb, w_ref[A_W:, :])
    gate = _gate_vec(modl_ref, modc_ref, 2, pos0, tm, ctx_len)
    o_ref[0] = x_ref[0] + gate * y


def _rec_out(xcat, mods, ctx_row, oa, feats, ob, p_hgrn, ln_w, ln_b, hg_norm, w_out, ctx_len):
    b, s, d = xcat.shape
    tm = TOK_TILE
    row = lambda a: a.reshape(1, -1)
    full = lambda a: pl.BlockSpec(a.shape, lambda i, j: (0,) * a.ndim)
    consts = [row(ln_w), row(ln_b), row(jnp.tile(hg_norm, B_HEADS)), _block_ones(A_W, A_HD),
              _block_ones(B_W, B_DK), w_out]
    return pl.pallas_call(
        functools.partial(_rec_out_kernel, ctx_len=ctx_len, tm=tm),
        grid=(b, s // tm),
        in_specs=[pl.BlockSpec((1, tm, d), lambda i, j: (i, j, 0))] + _mod_specs(d, ctx_row) + [
            pl.BlockSpec((1, 2, tm, A_W), lambda i, j: (i, 0, j, 0)),
            pl.BlockSpec((1, tm, A_W), lambda i, j: (i, j, SEC_G)),
            pl.BlockSpec((1, tm, A_W), lambda i, j: (i, j, SEC_BONUS)),
            pl.BlockSpec((1, 2, tm, B_W), lambda i, j: (i, 0, j, 0)),
            pl.BlockSpec((1, tm, B_W), lambda i, j: (i, j, 4)),
        ] + [full(a) for a in consts],
        out_specs=pl.BlockSpec((1, tm, d), lambda i, j: (i, j, 0)),
        out_shape=jax.ShapeDtypeStruct((b, s, d), F32),
        compiler_params=_cparams(("parallel", "parallel")),
        name="rec_out",
    )(xcat, mods, mods, oa, feats, feats, ob, p_hgrn, *consts)


def _res_proj_kernel(x_ref, modl_ref, modc_ref, y_ref, w_ref, o_ref, *, ctx_len, tm):
    pos0 = pl.program_id(1) * tm
    gate = _gate_vec(modl_ref, modc_ref, 2, pos0, tm, ctx_len)
    o_ref[0] = x_ref[0] + gate * _dot(y_ref[0], w_ref[...])


def _res_proj(x, mods, ctx_row, y, w, ctx_len):
    b, s, d = x.shape
    k = y.shape[-1]
    tm = TOK_TILE
    return pl.pallas_call(
        functools.partial(_res_proj_kernel, ctx_len=ctx_len, tm=tm),
        grid=(b, s // tm),
        in_specs=[pl.BlockSpec((1, tm, d), lambda i, j: (i, j, 0))] + _mod_specs(d, ctx_row) + [
            pl.BlockSpec((1, tm, k), lambda i, j: (i, j, 0)),
            pl.BlockSpec((k, d), lambda i, j: (0, 0))],
        out_specs=pl.BlockSpec((1, tm, d), lambda i, j: (i, j, 0)),
        out_shape=jax.ShapeDtypeStruct((b, s, d), F32),
        compiler_params=_cparams(("parallel", "parallel")),
        name="res_proj",
    )(x, mods, mods, y, w)


def _ffn_kernel(x_ref, modl_ref, modc_ref, g_ref, wg_ref, wu_ref, wd_ref, o_ref, h_ref, acc_ref, *, ctx_len, tm,
                n_f):
    f = pl.program_id(2)
    pos0 = pl.program_id(1) * tm

    @pl.when(f == 0)
    def _():
        h_ref[...] = _bf(_modulate(x_ref[0], g_ref[...], modl_ref, modc_ref, 3, pos0, ctx_len))
        acc_ref[...] = jnp.zeros_like(acc_ref)

    h = h_ref[...]
    act = _silu(jnp.dot(h, wg_ref[...], preferred_element_type=F32)) * jnp.dot(h, wu_ref[...],
                                                                               preferred_element_type=F32)
    acc_ref[...] += _dot(act, wd_ref[...])

    @pl.when(f == n_f - 1)
    def _():
        o_ref[0] = x_ref[0] + _gate_vec(modl_ref, modc_ref, 5, pos0, tm, ctx_len) * acc_ref[...]


def _ffn(x, mods, ctx_row, g, wg, wu, wd, ctx_len, tm, tf):
    b, s, d = x.shape
    ff = wg.shape[1]
    n_f = ff // tf
    return pl.pallas_call(
        functools.partial(_ffn_kernel, ctx_len=ctx_len, tm=tm, n_f=n_f),
        grid=(b, s // tm, n_f),
        in_specs=[pl.BlockSpec((1, tm, d), lambda i, j, f: (i, j, 0))] + _mod_specs(d, ctx_row) + [
            pl.BlockSpec((1, d), lambda i, j, f: (0, 0)),
            pl.BlockSpec((d, tf), lambda i, j, f: (0, f)),
            pl.BlockSpec((d, tf), lambda i, j, f: (0, f)),
            pl.BlockSpec((tf, d), lambda i, j, f: (f, 0))],
        out_specs=pl.BlockSpec((1, tm, d), lambda i, j, f: (i, j, 0)),
        out_shape=jax.ShapeDtypeStruct((b, s, d), F32),
        scratch_shapes=[pltpu.VMEM((tm, d), BF16), pltpu.VMEM((tm, d), F32)],
        compiler_params=_cparams(("parallel", "parallel", "arbitrary")),
        name="ffn",
    )(x, mods, mods, g.reshape(1, d), wg, wu, wd)


def _attn_kernel(sink_ref, q_ref, kc_ref, vc_ref, kp_ref, kq_ref, kn_ref, vp_ref, vq_ref, vn_ref, o_ref, *,
                 ctx_len, n_lat):
    i = pl.program_id(1)
    nk = ctx_len + 3 * ATT_BLOCK
    k_all = jnp.concatenate([kc_ref[0], kp_ref[0], kq_ref[0], kn_ref[0]], axis=0)
    v_all = jnp.concatenate([vc_ref[0], vp_ref[0], vq_ref[0], vn_ref[0]], axis=0)
    row = lax.broadcasted_iota(jnp.int32, (ATT_BLOCK, nk), 0)
    col = lax.broadcasted_iota(jnp.int32, (ATT_BLOCK, nk), 1)
    rel = col - (ctx_len + ATT_BLOCK)
    kabs = i * ATT_BLOCK + rel
    band = (jnp.abs(row - rel) <= WINDOW) & (kabs >= 0) & (kabs < n_lat)
    bias = jnp.where((col < ctx_len) | band, 0.0, NEG_INF)
    bias = jnp.concatenate([bias] * GQ, axis=0)
    grp = lax.broadcasted_iota(jnp.int32, (GQ * ATT_BLOCK, 1), 0) // ATT_BLOCK
    scale = HD ** -0.5
    for hk in range(HKV):
        kh = _bf(k_all[:, hk * HD:(hk + 1) * HD])
        vh = _bf(v_all[:, hk * HD:(hk + 1) * HD])
        heads = [hk * GQ + g for g in range(GQ)]
        q = jnp.concatenate([q_ref[0, :, hq * HD:(hq + 1) * HD] for hq in heads], axis=0) * scale
        sk = jnp.zeros((GQ * ATT_BLOCK, 1), F32)
        for g, hq in enumerate(heads):
            sk = jnp.where(grp == g, sink_ref[hq], sk)
        s = _dot_nt(q, kh) + bias
        m = jnp.maximum(jnp.max(s, axis=-1, keepdims=True), sk)
        p = jnp.exp(s - m)
        den = jnp.sum(p, axis=-1, keepdims=True) + jnp.exp(sk - m)
        o = jnp.dot(_bf(p), vh, preferred_element_type=F32) / den
        for g, hq in enumerate(heads):
            o_ref[0, :, hq * HD:(hq + 1) * HD] = o[g * ATT_BLOCK:(g + 1) * ATT_BLOCK]


def _attention(qkv, sink, ctx_len):
    b, s, _ = qkv.shape
    n_lat = s - ctx_len
    nb = n_lat // ATT_BLOCK
    cb = ctx_len // ATT_BLOCK
    kcol = Q_COLS // KV_COLS
    vcol = kcol + 1

    def band(colblk, shift):
        return pl.BlockSpec((1, ATT_BLOCK, KV_COLS),
                            lambda bi, i: (bi, cb + jnp.clip(i + shift, 0, nb - 1), colblk))

    return pl.pallas_call(
        functools.partial(_attn_kernel, ctx_len=ctx_len, n_lat=n_lat),
        grid=(b, nb),
        in_specs=[pl.BlockSpec(memory_space=pltpu.SMEM),
                  pl.BlockSpec((1, ATT_BLOCK, Q_COLS), lambda bi, i: (bi, cb + i, 0)),
                  pl.BlockSpec((1, ctx_len, KV_COLS), lambda bi, i: (bi, 0, kcol)),
                  pl.BlockSpec((1, ctx_len, KV_COLS), lambda bi, i: (bi, 0, vcol)),
                  band(kcol, -1), band(kcol, 0), band(kcol, 1),
                  band(vcol, -1), band(vcol, 0), band(vcol, 1)],
        out_specs=pl.BlockSpec((1, ATT_BLOCK, Q_COLS), lambda bi, i: (bi, i, 0)),
        out_shape=jax.ShapeDtypeStruct((b, n_lat, Q_COLS), F32),
        compiler_params=_cparams(("parallel", "parallel")),
        name="attention",
    )(sink, qkv, qkv, qkv, qkv, qkv, qkv, qkv, qkv, qkv)


def _router_kernel(x_ref, modl_ref, modc_ref, g_ref, w_ref, b_ref, o_ref, e_ref, f_ref, *, tm):
    f = _modulate(x_ref[0], g_ref[...], modl_ref, modc_ref, 3, 0, 0)
    f_ref[0] = f
    logits = _dot_f32(f, w_ref[...]) + b_ref[...]
    lane = lax.broadcasted_iota(jnp.int32, logits.shape, 1).astype(F32)
    logits = jnp.where(lane < N_EXPERTS, logits, NEG_INF)
    ex = jnp.exp(logits - jnp.max(logits, axis=-1, keepdims=True))
    probs = ex / jnp.sum(ex, axis=-1, keepdims=True)
    p1 = jnp.max(probs, axis=-1, keepdims=True)
    i1 = jnp.min(jnp.where(probs == p1, lane, float(LANES)), axis=-1, keepdims=True)
    rest = jnp.where(lane == i1, -1.0, probs)
    p2 = jnp.max(rest, axis=-1, keepdims=True)
    i2 = jnp.min(jnp.where(rest == p2, lane, float(LANES)), axis=-1, keepdims=True)
    tot = p1 + p2
    o_ref[0] = jnp.where(lane == 0.0, p1 / tot, jnp.where(lane == 1.0, p2 / tot, 0.0))
    e_ref[0] = jnp.where(lane == 0.0, i1, jnp.where(lane == 1.0, i2, 0.0)).astype(jnp.int32)


def _router(x, mods, g, w, bias):
    b, s, d = x.shape
    tm = TOK_TILE
    wpad = jnp.zeros((d, LANES), F32).at[:, :N_EXPERTS].set(w)
    bpad = jnp.zeros((1, LANES), F32).at[0, :N_EXPERTS].set(bias)
    lane_spec = pl.BlockSpec((1, tm, LANES), lambda i, j: (i, j, 0))
    return pl.pallas_call(
        functools.partial(_router_kernel, tm=tm),
        grid=(b, s // tm),
        in_specs=[pl.BlockSpec((1, tm, d), lambda i, j: (i, j, 0))] + _mod_specs(d, 0) + [
            pl.BlockSpec((1, d), lambda i, j: (0, 0)),
            pl.BlockSpec((d, LANES), lambda i, j: (0, 0)),
            pl.BlockSpec((1, LANES), lambda i, j: (0, 0))],
        out_specs=[lane_spec, lane_spec, pl.BlockSpec((1, tm, d), lambda i, j: (i, j, 0))],
        out_shape=[jax.ShapeDtypeStruct((b, s, LANES), F32), jax.ShapeDtypeStruct((b, s, LANES), jnp.int32),
                   jax.ShapeDtypeStruct((b, s, d), F32)],
        compiler_params=_cparams(("parallel", "parallel")),
        name="router",
    )(x, mods, mods, g.reshape(1, d), wpad, bpad)


MOE_ROWS = 256


def _row_gather_kernel(src_ref, f_hbm, o_ref, sem, *, rows):
    base = pl.program_id(0) * rows

    def issue(r, carry):
        pltpu.make_async_copy(f_hbm.at[pl.ds(src_ref[base + r], 1)], o_ref.at[pl.ds(r, 1)], sem).start()
        return carry

    lax.fori_loop(0, rows, issue, 0, unroll=8)
    pltpu.make_async_copy(f_hbm.at[pl.ds(0, rows)], o_ref, sem).wait()


def _row_gather(f, src):
    n_rows = src.shape[0]
    d = f.shape[1]
    rows = MOE_ROWS
    return pl.pallas_call(
        functools.partial(_row_gather_kernel, rows=rows),
        grid_spec=pltpu.PrefetchScalarGridSpec(
            num_scalar_prefetch=1,
            grid=(n_rows // rows,),
            in_specs=[pl.BlockSpec(memory_space=pl.ANY)],
            out_specs=pl.BlockSpec((rows, d), lambda i, src_ref: (i, 0)),
            scratch_shapes=[pltpu.SemaphoreType.DMA(())]),
        out_shape=jax.ShapeDtypeStruct((n_rows, d), f.dtype),
        compiler_params=_cparams(("arbitrary",)),
        name="moe_gather",
    )(src, f)


def _expert_kernel(be_ref, x_ref, wg_ref, wu_ref, wd_ref, o_ref):
    h = _bf(x_ref[...])
    act = _silu(jnp.dot(h, wg_ref[0], preferred_element_type=F32)) * jnp.dot(h, wu_ref[0],
                                                                              preferred_element_type=F32)
    o_ref[...] = _dot(act, wd_ref[0])


def _experts(xs, block_e, wg, wu, wd):
    n_rows, d = xs.shape
    ff = wg.shape[2]
    rows = MOE_ROWS
    return pl.pallas_call(
        _expert_kernel,
        grid_spec=pltpu.PrefetchScalarGridSpec(
            num_scalar_prefetch=1,
            grid=(n_rows // rows,),
            in_specs=[pl.BlockSpec((rows, d), lambda i, be: (i, 0)),
                      pl.BlockSpec((1, d, ff), lambda i, be: (be[i], 0, 0)),
                      pl.BlockSpec((1, d, ff), lambda i, be: (be[i], 0, 0)),
                      pl.BlockSpec((1, ff, d), lambda i, be: (be[i], 0, 0))],
            out_specs=pl.BlockSpec((rows, d), lambda i, be: (i, 0))),
        out_shape=jax.ShapeDtypeStruct((n_rows, d), F32),
        compiler_params=_cparams(("arbitrary",)),
        name="moe_experts",
    )(block_e, xs, wg, wu, wd)


def _combine_kernel(dest_ref, x_ref, modl_ref, modc_ref, w_ref, gfin_ref, ys_hbm, o_ref, y1_ref, y2_ref, sem, *,
                    tm, tiles_per_row):
    base = (pl.program_id(0) * tiles_per_row + pl.program_id(1)) * tm

    def issue(r, carry):
        slot = 2 * (base + r)
        pltpu.make_async_copy(ys_hbm.at[pl.ds(dest_ref[slot], 1)], y1_ref.at[pl.ds(r, 1)], sem).start()
        pltpu.make_async_copy(ys_hbm.at[pl.ds(dest_ref[slot + 1], 1)], y2_ref.at[pl.ds(r, 1)], sem).start()
        return carry

    lax.fori_loop(0, tm, issue, 0, unroll=8)
    pltpu.make_async_copy(ys_hbm.at[pl.ds(0, tm)], y1_ref, sem).wait()
    pltpu.make_async_copy(ys_hbm.at[pl.ds(0, tm)], y2_ref, sem).wait()
    w = w_ref[0]
    moe = w[:, 0:1] * y1_ref[...] + w[:, 1:2] * y2_ref[...]
    y = x_ref[0] + _gate_vec(modl_ref, modc_ref, 5, 0, tm, 0) * moe
    o_ref[0] = y * lax.rsqrt(jnp.mean(y * y, axis=-1, keepdims=True) + NORM_EPS) * gfin_ref[...]


def _combine(x, mods, tokw, dest, ys, final_g):
    b, s, d = x.shape
    tm = TOK_TILE
    tiles = s // tm
    return pl.pallas_call(
        functools.partial(_combine_kernel, tm=tm, tiles_per_row=tiles),
        grid_spec=pltpu.PrefetchScalarGridSpec(
            num_scalar_prefetch=1,
            grid=(b, tiles),
            in_specs=[pl.BlockSpec((1, tm, d), lambda i, j, dr: (i, j, 0)),
                      pl.BlockSpec((1, N_MOD, d), lambda i, j, dr: (i, 0, 0)),
                      pl.BlockSpec((1, N_MOD, d), lambda i, j, dr: (0, 0, 0)),
                      pl.BlockSpec((1, tm, LANES), lambda i, j, dr: (i, j, 0)),
                      pl.BlockSpec((1, d), lambda i, j, dr: (0, 0)),
                      pl.BlockSpec(memory_space=pl.ANY)],
            out_specs=pl.BlockSpec((1, tm, d), lambda i, j, dr: (i, j, 0)),
            scratch_shapes=[pltpu.VMEM((tm, d), F32), pltpu.VMEM((tm, d), F32), pltpu.SemaphoreType.DMA(())]),
        out_shape=jax.ShapeDtypeStruct((b, s, d), F32),
        compiler_params=_cparams(("arbitrary", "arbitrary")),
        name="moe_combine",
    )(dest, x, mods, mods, tokw, final_g.reshape(1, d), ys)


def _moe_routing(eidx):
    m = eidx.shape[0]
    oh = (eidx[:, None] == jnp.arange(N_EXPERTS, dtype=jnp.int32)[None, :]).astype(jnp.int32)
    csum = jnp.cumsum(oh, axis=0)
    rank = jnp.sum((csum - oh) * oh, axis=1)
    counts = csum[-1]
    padded = (counts + MOE_ROWS - 1) // MOE_ROWS * MOE_ROWS
    pad_end = jnp.cumsum(padded)
    dest = (pad_end - padded)[eidx] + rank
    n_blocks = m // MOE_ROWS + N_EXPERTS
    block_e = jnp.minimum(jnp.searchsorted(pad_end, jnp.arange(n_blocks, dtype=jnp.int32) * MOE_ROWS, side='right'),
                          N_EXPERTS - 1).astype(jnp.int32)
    src = jnp.zeros((n_blocks * MOE_ROWS,), jnp.int32).at[dest].set(jnp.arange(m, dtype=jnp.int32) // 2)
    return dest.astype(jnp.int32), src, block_e


def _rope_tables(n_lat, ctx_len):
    rows = n_lat // GRID_W
    row = jnp.repeat(jnp.arange(rows, dtype=F32), GRID_W)
    col = jnp.tile(jnp.arange(GRID_W, dtype=F32), rows)
    inv = ROPE_BASE ** (-jnp.arange(0, AX_DIM, 2, dtype=F32) / AX_DIM)
    ar, ac = row[:, None] * inv, col[:, None] * inv
    cos = jnp.concatenate([jnp.cos(ar), jnp.cos(ar), jnp.cos(ac), jnp.cos(ac)], axis=-1)
    sin = jnp.concatenate([-jnp.sin(ar), jnp.sin(ar), -jnp.sin(ac), jnp.sin(ac)], axis=-1)
    n_heads = ROPE_COLS // HD
    cos = jnp.concatenate([jnp.ones((ctx_len, HD), F32), cos], axis=0)
    sin = jnp.concatenate([jnp.zeros((ctx_len, HD), F32), sin], axis=0)
    half = AX_DIM // 2
    j = jnp.arange(HD)
    partner = jnp.where((j % AX_DIM) < half, j + half, j - half)
    perm = (jnp.arange(n_heads)[:, None] * HD + partner[None, :]).reshape(-1)
    return jnp.tile(cos, (1, n_heads)), jnp.tile(sin, (1, n_heads)), perm


def kernel(x, c, ctx, c_ctx, mod_w, mod_b, norm_mix, norm_ffn, norm_final, rec_w_in, rec_w_out, rwkv_mu, rwkv_w0, rwkv_w_up, rwkv_a0, rwkv_a_up, rwkv_g_up, rwkv_k_k, rwkv_k_a, rwkv_r_k, rwkv_ln_w, rwkv_ln_b, hgrn_lb, hgrn_norm, ffn_w_gate, ffn_w_up, ffn_w_down, att_w_in, att_w_out, att_sink, moe_router, moe_router_b, moe_w_gate, moe_w_up, moe_w_down):
    bsz, n_lat, d = x.shape
    ctx_len = ctx.shape[1]
    xcat = jnp.concatenate([ctx, x], axis=1)

    n_rows = -(-(bsz + 1) // 8) * 8
    cvec = jnp.zeros((n_rows, d), F32).at[:bsz].set(c).at[bsz].set(c_ctx)
    mods = [_adaln(cvec, mod_w[l], mod_b[l]).reshape(n_rows, N_MOD, d) for l in range(2)]

    p_rwkv, p_hgrn = _proj(xcat, mods[0], bsz, norm_mix[0], _bf(rec_w_in[0]), (RWKV_COLS, HGRN_COLS), ctx_len)
    feats = _rwkv_features(p_rwkv, rwkv_mu[0], rwkv_w0[0], rwkv_w_up[0], rwkv_a0[0], rwkv_a_up[0], rwkv_g_up[0],
                           rwkv_k_k[0], rwkv_k_a[0], rwkv_r_k[0].reshape(-1), ctx_len)
    oa = _rwkv_scan(feats, ctx_len)
    lb = jnp.cumsum(jax.nn.softmax(hgrn_lb.astype(F32), axis=1), axis=1)[:, 0].reshape(2, 1, B_W)
    ob = _hgrn_scan(p_hgrn, lb, ctx_len)
    xcat = _rec_out(xcat, mods[0], bsz, oa, feats, ob, p_hgrn, rwkv_ln_w[0], rwkv_ln_b[0], hgrn_norm[0],
                    _bf(rec_w_out[0]), ctx_len)
    xcat = _ffn(xcat, mods[0], bsz, norm_ffn[0], _bf(ffn_w_gate[0]), _bf(ffn_w_up[0]), _bf(ffn_w_down[0]), ctx_len,
                tm=768, tf=1408)

    cos, sin, perm = _rope_tables(n_lat, ctx_len)
    w_att = att_w_in[0]
    (qkv,) = _proj(xcat, mods[1], bsz, norm_mix[1], _bf(w_att), (ATT_COLS,), ctx_len,
                   rope=(_bf(w_att[:, perm]), cos, sin))
    att = _attention(qkv, att_sink[0], ctx_len)
    x_lat = _res_proj(xcat[:, ctx_len:], mods[1], 0, att, _bf(att_w_out[0]), 0)
    tokw, eidx, f_lat = _router(x_lat, mods[1], norm_ffn[1], moe_router[0], moe_router_b[0])
    dest, src, block_e = _moe_routing(eidx[..., :2].reshape(-1))
    xs = _row_gather(f_lat.reshape(bsz * n_lat, d), src)
    ys = _experts(xs, block_e, _bf(moe_w_gate[0]), _bf(moe_w_up[0]), _bf(moe_w_down[0]))
    return _combine(x_lat, mods[1], tokw, dest, ys, norm_final)
```

```python
import functools

import jax
import jax.numpy as jnp
from jax import lax
from jax.experimental import pallas as pl
from jax.experimental.pallas import tpu as pltpu

F32 = jnp.float32
BF16 = jnp.bfloat16
HIGHEST = lax.Precision.HIGHEST

N_MOD = 6
NORM_EPS = 1e-6
NEG_INF = -1e30

A_HEADS = 8
A_HD = 64
A_W = A_HEADS * A_HD
DECAY_LORA = 64
AAA_LORA = 64
GATE_LORA = 128
RWKV_COLS = 3 * A_W + DECAY_LORA + AAA_LORA + GATE_LORA
GN_EPS = 64e-5

B_HEADS = 4
B_DK = 128
B_W = B_HEADS * B_DK
HGRN_COLS = 5 * B_W

HQ = 16
HKV = 4
GQ = HQ // HKV
HD = 64
WINDOW = 128
ATT_BLOCK = 128
AX_DIM = HD // 2
ROPE_BASE = 10000.0
GRID_W = 64
Q_COLS = HQ * HD
KV_COLS = HKV * HD
ROPE_COLS = Q_COLS + KV_COLS
ATT_COLS = Q_COLS + 2 * KV_COLS

N_EXPERTS = 8
LANES = 128
CHUNK = 64
SCAN_BLOCK = 256
MXU_WIDTH = 256
RWKV_GROUP = MXU_WIDTH // A_HD
TOK_TILE = 256
VMEM_LIMIT = 56 * 1024 * 1024

SEC_R, SEC_K, SEC_V, SEC_KK, SEC_B, SEC_LW0, SEC_LW1, SEC_G, SEC_BONUS = range(9)
N_SEC = 9


def _cparams(sem):
    return pltpu.CompilerParams(dimension_semantics=sem, vmem_limit_bytes=VMEM_LIMIT)


def _bf(x):
    return x.astype(BF16)


def _dot(a, b):
    return jnp.dot(_bf(a), _bf(b), preferred_element_type=F32)


def _dot_nt(a, b):
    return lax.dot_general(_bf(a), _bf(b), (((1,), (1,)), ((), ())), preferred_element_type=F32)


def _dot_tn(a, b):
    return lax.dot_general(_bf(a), _bf(b), (((0,), (0,)), ((), ())), preferred_element_type=F32)


def _dot_f32(a, b):
    return jnp.dot(a, b, preferred_element_type=F32, precision=HIGHEST)


def _dot_tn_f32(a, b):
    return lax.dot_general(a, b, (((0,), (0,)), ((), ())), preferred_element_type=F32, precision=HIGHEST)


def _sigmoid(x):
    return 1.0 / (1.0 + jnp.exp(-x))


def _silu(x):
    return x * _sigmoid(x)


def _segsum(x, ones_bd):
    hi = _bf(x)
    r1 = x - hi.astype(F32)
    mid = _bf(r1)
    lo = _bf(r1 - mid.astype(F32))
    acc = jnp.dot(hi, ones_bd, preferred_element_type=F32)
    acc += jnp.dot(mid, ones_bd, preferred_element_type=F32)
    acc += jnp.dot(lo, ones_bd, preferred_element_type=F32)
    return acc


def _block_ones(width, seg):
    i = jnp.arange(width) // seg
    return (i[:, None] == i[None, :]).astype(BF16)


def _modulate(x, g, modl_ref, modc_ref, row, pos0, ctx_len):
    y = x * lax.rsqrt(jnp.mean(x * x, axis=-1, keepdims=True) + NORM_EPS) * g
    pos = pos0 + lax.broadcasted_iota(jnp.int32, (x.shape[0], 1), 0)
    is_ctx = pos < ctx_len
    shift = jnp.where(is_ctx, modc_ref[0, row:row + 1, :], modl_ref[0, row:row + 1, :])
    scale = jnp.where(is_ctx, modc_ref[0, row + 1:row + 2, :], modl_ref[0, row + 1:row + 2, :])
    return y * (1.0 + scale) + shift


def _gate_vec(modl_ref, modc_ref, row, pos0, n, ctx_len):
    pos = pos0 + lax.broadcasted_iota(jnp.int32, (n, 1), 0)
    return jnp.where(pos < ctx_len, modc_ref[0, row:row + 1, :], modl_ref[0, row:row + 1, :])


def _mod_specs(d, ctx_row, batch_axis=0):
    def lat_map(*idx):
        return (idx[batch_axis], 0, 0)

    def ctx_map(*idx):
        return (ctx_row, 0, 0)

    return [pl.BlockSpec((1, N_MOD, d), lat_map), pl.BlockSpec((1, N_MOD, d), ctx_map)]


def _adaln_kernel(c_ref, w_ref, b_ref, o_ref):
    o_ref[...] = _dot(_silu(c_ref[...]), w_ref[...]) + b_ref[...]


def _adaln(cvec, w, b):
    r, d = cvec.shape
    n = w.shape[1]
    tn = 1024
    return pl.pallas_call(
        _adaln_kernel,
        grid=(n // tn,),
        in_specs=[pl.BlockSpec((r, d), lambda j: (0, 0)),
                  pl.BlockSpec((d, tn), lambda j: (0, j)),
                  pl.BlockSpec((1, tn), lambda j: (0, j))],
        out_specs=pl.BlockSpec((r, tn), lambda j: (0, j)),
        out_shape=jax.ShapeDtypeStruct((r, n), F32),
        compiler_params=_cparams(("parallel",)),
        name="adaln",
    )(cvec, w, b.reshape(1, n))


def _proj_kernel(x_ref, modl_ref, modc_ref, g_ref, w_ref, *rest, ctx_len, tm, splits, rope_cols):
    if rope_cols:
        wrot_ref, cos_ref, sin_ref = rest[:3]
        outs = rest[3:]
    else:
        outs = rest
    pos0 = pl.program_id(1) * tm
    h = _bf(_modulate(x_ref[0], g_ref[...], modl_ref, modc_ref, 0, pos0, ctx_len))
    y = jnp.dot(h, w_ref[...], preferred_element_type=F32)
    if rope_cols:
        yr = jnp.dot(h, wrot_ref[...], preferred_element_type=F32)
        roped = y[:, :rope_cols] * cos_ref[...] + yr * sin_ref[...]
        outs[0][0, :, :rope_cols] = roped
        outs[0][0, :, rope_cols:] = y[:, rope_cols:]
    else:
        lo = 0
        for o_ref, width in zip(outs, splits):
            o_ref[0] = y[:, lo:lo + width]
            lo += width


def _proj(xcat, mods, ctx_row, g, w, splits, ctx_len, rope=None):
    b, s, d = xcat.shape
    n = w.shape[1]
    tm = TOK_TILE
    in_specs = [pl.BlockSpec((1, tm, d), lambda i, j: (i, j, 0))] + _mod_specs(d, ctx_row) + [
        pl.BlockSpec((1, d), lambda i, j: (0, 0)),
        pl.BlockSpec((d, n), lambda i, j: (0, 0))]
    args = [xcat, mods, mods, g.reshape(1, d), w]
    rope_cols = 0
    if rope is not None:
        wrot, cos, sin = rope
        rope_cols = wrot.shape[1]
        in_specs += [pl.BlockSpec((d, rope_cols), lambda i, j: (0, 0)),
                     pl.BlockSpec((tm, rope_cols), lambda i, j: (j, 0)),
                     pl.BlockSpec((tm, rope_cols), lambda i, j: (j, 0))]
        args += [wrot, cos, sin]
    out_shape = [jax.ShapeDtypeStruct((b, s, width), F32) for width in splits]
    out_specs = [pl.BlockSpec((1, tm, width), lambda i, j: (i, j, 0)) for width in splits]
    return pl.pallas_call(
        functools.partial(_proj_kernel, ctx_len=ctx_len, tm=tm, splits=tuple(splits), rope_cols=rope_cols),
        grid=(b, s // tm),
        in_specs=in_specs,
        out_specs=out_specs,
        out_shape=out_shape,
        compiler_params=_cparams(("parallel", "parallel")),
        name="proj_rope" if rope_cols else "proj",
    )(*args)


def _softplus(x):
    return jnp.maximum(x, 0.0) + jnp.log(1.0 + jnp.exp(-jnp.abs(x)))


def _rwkv_feat_kernel(p_ref, pp_ref, pn_ref, mu_ref, w0_ref, wup_ref, a0_ref, aup_ref, gup_ref,
                      kk_ref, ka_ref, rk_ref, ones_ref, o_ref, *, ctx_len, seq_len, tm):
    pos0 = pl.program_id(1) * tm
    p = p_ref[0]
    rows = lax.broadcasted_iota(jnp.int32, (tm, 1), 0)
    pos = rows + pos0
    prev = jnp.where(rows == 0, pp_ref[0, 7:8, :], pltpu.roll(p, 1, 0))
    prev = jnp.where((pos == 0) | (pos == ctx_len), 0.0, prev)
    nxt = jnp.where(rows == tm - 1, pn_ref[0, 0:1, :], pltpu.roll(p, tm - 1, 0))
    nxt = jnp.where((pos == ctx_len - 1) | (pos == seq_len - 1), 0.0, nxt)
    p = p + mu_ref[...] * (0.5 * (prev + nxt) - p)

    r = p[:, 0:A_W]
    k = p[:, A_W:2 * A_W]
    v = p[:, 2 * A_W:3 * A_W]
    lo = 3 * A_W
    wd = p[:, lo:lo + DECAY_LORA]
    ad = p[:, lo + DECAY_LORA:lo + DECAY_LORA + AAA_LORA]
    gd = p[:, lo + DECAY_LORA + AAA_LORA:lo + DECAY_LORA + AAA_LORA + GATE_LORA]

    tw = jnp.tanh(wd)
    for d in range(2):
        w_log = -_softplus(-(w0_ref[d:d + 1, :] + _dot(tw, wup_ref[d]))) - 0.5
        o_ref[0, :, (SEC_LW0 + d) * A_W:(SEC_LW0 + d + 1) * A_W] = -jnp.exp(w_log)
    a = _sigmoid(a0_ref[...] + _dot(ad, aup_ref[...]))
    g = _dot(_sigmoid(gd), gup_ref[...])
    ones_bd = ones_ref[...]
    kk = k * kk_ref[...]
    kk = kk / jnp.maximum(jnp.sqrt(_segsum(kk * kk, ones_bd)), 1e-12)
    k = k * (1.0 + (a - 1.0) * ka_ref[...])
    bonus = _segsum(r * k * rk_ref[...], ones_bd) * v
    for sec, val in ((SEC_R, r), (SEC_K, k), (SEC_V, v), (SEC_KK, kk), (SEC_B, kk * a), (SEC_G, g),
                     (SEC_BONUS, bonus)):
        o_ref[0, :, sec * A_W:(sec + 1) * A_W] = val


def _rwkv_features(p, mu, w0, w_up, a0, a_up, g_up, k_k, k_a, r_k, ctx_len):
    b, s, c = p.shape
    tm = TOK_TILE
    nb8 = s // 8
    row = lambda a: a.reshape(1, -1)
    full = lambda a: pl.BlockSpec(a.shape, lambda i, j: (0,) * a.ndim)
    args = [row(mu), w0, w_up, row(a0), a_up, g_up, row(k_k), row(k_a), row(r_k), _block_ones(A_W, A_HD)]
    return pl.pallas_call(
        functools.partial(_rwkv_feat_kernel, ctx_len=ctx_len, seq_len=s, tm=tm),
        grid=(b, s // tm),
        in_specs=[pl.BlockSpec((1, tm, c), lambda i, j: (i, j, 0)),
                  pl.BlockSpec((1, 8, c), lambda i, j: (i, jnp.maximum(j * (tm // 8) - 1, 0), 0)),
                  pl.BlockSpec((1, 8, c), lambda i, j: (i, jnp.minimum((j + 1) * (tm // 8), nb8 - 1), 0)),
                  ] + [full(a) for a in args],
        out_specs=pl.BlockSpec((1, tm, N_SEC * A_W), lambda i, j: (i, j, 0)),
        out_shape=jax.ShapeDtypeStruct((b, s, N_SEC * A_W), F32),
        compiler_params=_cparams(("parallel", "parallel")),
        name="rwkv_features",
    )(p, p, p, *args)


def _chunk_order(d, j, n_ctx_chunks, n_chunks):
    back = jnp.where(j < n_ctx_chunks, n_ctx_chunks - 1 - j, n_chunks - 1 + n_ctx_chunks - j)
    return jnp.where(d == 0, j, back)


def _order_masks(rev):
    row = lax.broadcasted_iota(jnp.int32, (CHUNK, CHUNK), 0)
    col = lax.broadcasted_iota(jnp.int32, (CHUNK, CHUNK), 1)
    diff = jnp.where(rev, row - col, col - row)
    return diff < 0, diff <= 0, row == col


def _dot_split(a, x):
    hi = _bf(x)
    r1 = x - hi.astype(F32)
    mid = _bf(r1)
    lo = _bf(r1 - mid.astype(F32))
    acc = jnp.dot(a, hi, preferred_element_type=F32)
    acc += jnp.dot(a, mid, preferred_element_type=F32)
    acc += jnp.dot(a, lo, preferred_element_type=F32)
    return acc


def _rwkv_chunk_kernel(r_ref, k_ref, v_ref, kk_ref, b_ref, lw_ref, o_ref, s_ref):
    rev = pl.program_id(1) == 1

    @pl.when(pl.program_id(2) == 0)
    def _():
        s_ref[...] = jnp.zeros_like(s_ref)

    _, incl, _ = _order_masks(rev)
    incl_bf = jnp.where(incl, 1.0, 0.0).astype(BF16)
    gw = RWKV_GROUP * A_HD
    groups = range(A_HEADS // RWKV_GROUP)
    gsl = [slice(g * gw, (g + 1) * gw) for g in groups]
    row = lax.broadcasted_iota(jnp.int32, (CHUNK, gw), 0)
    col = lax.broadcasted_iota(jnp.int32, (CHUNK, gw), 1) % A_HD
    diff = jnp.where(rev, row - col, col - row)
    strict_c = diff < 0
    incl_c = diff <= 0
    eye_c = jnp.where(diff == 0, 1.0, 0.0)
    bd_mask = (lax.broadcasted_iota(jnp.int32, (gw, gw), 0) // A_HD
               == lax.broadcasted_iota(jnp.int32, (gw, gw), 1) // A_HD)

    def bd(x):
        return jnp.where(bd_mask, jnp.concatenate([x] * RWKV_GROUP, axis=0), 0.0)

    def stack(x):
        return jnp.concatenate([x[:, h * A_HD:(h + 1) * A_HD] for h in range(RWKV_GROUP)], axis=0)

    n = SCAN_BLOCK // CHUNK
    rows = [pl.ds(pl.multiple_of(jnp.where(rev, (n - 1 - i) * CHUNK, i * CHUNK), CHUNK), CHUNK) for i in range(n)]
    units = [(i, g) for i in range(n) for g in groups]
    kap, rt, kbar, bbar, kgam, bgam, gam, v = ({} for _ in range(8))
    for i in range(n):
        lw = lw_ref[0, rows[i], :]
        g_incl = _dot_split(incl_bf, lw)
        g_tot = jnp.sum(lw, axis=0, keepdims=True)
        e_ng = jnp.exp(-g_incl)
        e_tail = jnp.exp(g_tot - g_incl)
        k = k_ref[0, rows[i], :]
        b = b_ref[0, rows[i], :]
        kap_i = kk_ref[0, rows[i], :] * jnp.exp(g_incl - lw)
        rt_i = r_ref[0, rows[i], :] * jnp.exp(g_incl)
        v_i = v_ref[0, rows[i], :]
        for g in groups:
            kap[i, g], rt[i, g], v[i, g] = kap_i[:, gsl[g]], rt_i[:, gsl[g]], v_i[:, gsl[g]]
            kbar[i, g], bbar[i, g] = (k * e_ng)[:, gsl[g]], (b * e_ng)[:, gsl[g]]
            kgam[i, g], bgam[i, g] = (k * e_tail)[:, gsl[g]], (b * e_tail)[:, gsl[g]]
            gam[i, g] = jnp.exp(g_tot)[:, gsl[g]]

    x = {u: jnp.concatenate([kap[u], rt[u]], axis=0) for u in units}
    yk = {u: _dot_nt(x[u], bd(kbar[u])) for u in units}
    yb = {u: _dot_nt(x[u], bd(bbar[u])) for u in units}
    a = {u: jnp.where(strict_c, -yb[u][:CHUNK], 0.0) for u in units}
    xs = {u: eye_c + a[u] for u in units}
    pw = {u: _dot(a[u], bd(a[u])) for u in units}
    for _ in range(4):
        st = {u: _dot(jnp.concatenate([pw[u], xs[u]], axis=0), bd(pw[u])) for u in units}
        pw = {u: st[u][:CHUNK] for u in units}
        xs = {u: xs[u] + st[u][CHUNK:] for u in units}
    t_inv = {u: xs[u] + _dot(xs[u], bd(pw[u])) for u in units}
    bd_v = {u: _bf(bd(v[u])) for u in units}
    w1 = {u: _dot(jnp.where(strict_c, yk[u][:CHUNK], 0.0), bd_v[u]) for u in units}
    tk = {u: _dot(t_inv[u], bd(kap[u])) for u in units}
    uu = {u: _dot(t_inv[u], bd(w1[u])) for u in units}
    o_loc = {u: _dot(jnp.where(incl_c, yk[u][CHUNK:], 0.0), bd_v[u]) for u in units}
    a_rb = {u: jnp.where(incl_c, yb[u][CHUNK:], 0.0) for u in units}
    bd_bg = {u: _bf(bd(bgam[u])) for u in units}
    bd_m2 = {u: _bf(bd(_dot_tn(stack(tk[u]), bd_bg[u]))) for u in units}
    cc = {u: _dot_tn(stack(v[u]), bd(kgam[u])) - _dot_tn(stack(uu[u]), bd_bg[u]) for u in units}
    tr = {u: jnp.concatenate([tk[u], rt[u]], axis=0) for u in units}

    s = [s_ref[g] for g in groups]
    for i in range(n):
        for g in groups:
            u = (i, g)
            zr = _dot_nt(tr[u], bd(s[g]))
            z = zr[:CHUNK] + uu[u]
            o_ref[0, 0, rows[i], gsl[g]] = zr[CHUNK:] + o_loc[u] - _dot(a_rb[u], bd(z))
            s[g] = s[g] * gam[u] - _dot(s[g], bd_m2[u]) + cc[u]
    for g in groups:
        s_ref[g] = s[g]


def _rwkv_scan(feats, ctx_len):
    b, s, _ = feats.shape
    nc, ncc = s // SCAN_BLOCK, ctx_len // SCAN_BLOCK

    def sec(idx):
        return pl.BlockSpec((1, SCAN_BLOCK, A_W), lambda i, d, j: (i, _chunk_order(d, j, ncc, nc), idx))

    lw_spec = pl.BlockSpec((1, SCAN_BLOCK, A_W), lambda i, d, j: (i, _chunk_order(d, j, ncc, nc), SEC_LW0 + d))
    return pl.pallas_call(
        _rwkv_chunk_kernel,
        grid=(b, 2, nc),
        in_specs=[sec(SEC_R), sec(SEC_K), sec(SEC_V), sec(SEC_KK), sec(SEC_B), lw_spec],
        out_specs=pl.BlockSpec((1, 1, SCAN_BLOCK, A_W), lambda i, d, j: (i, d, _chunk_order(d, j, ncc, nc), 0)),
        out_shape=jax.ShapeDtypeStruct((b, 2, s, A_W), F32),
        scratch_shapes=[pltpu.VMEM((A_HEADS // RWKV_GROUP, A_HD, RWKV_GROUP * A_HD), F32)],
        compiler_params=_cparams(("parallel", "parallel", "arbitrary")),
        name="rwkv_scan",
    )(feats, feats, feats, feats, feats, feats)


def _hgrn_chunk_kernel(q_ref, i_ref, f_ref, lb_ref, o_ref, s_ref):
    rev = pl.program_id(1) == 1

    @pl.when(pl.program_id(2) == 0)
    def _():
        s_ref[...] = jnp.zeros_like(s_ref)

    _, incl, _ = _order_masks(rev)
    incl_bf = jnp.where(incl, 1.0, 0.0).astype(BF16)
    lb = lb_ref[0]
    heads = range(B_HEADS)
    sls = [slice(h * B_DK, (h + 1) * B_DK) for h in heads]

    n = SCAN_BLOCK // CHUNK
    rows = [pl.ds(pl.multiple_of(jnp.where(rev, (n - 1 - i) * CHUNK, i * CHUNK), CHUNK), CHUNK) for i in range(n)]
    units = [(i, h) for i in range(n) for h in heads]
    q_in, k_in, q_st, k_tail, gam, v = ({} for _ in range(6))
    for i in range(n):
        f = lb + (1.0 - lb) * _sigmoid(f_ref[0, rows[i], :])
        logf = jnp.log(f)
        kf = 1.0 - f
        g_incl = _dot_split(incl_bf, logf)
        g_tot = jnp.sum(logf, axis=0, keepdims=True)
        g_mid = g_incl[CHUNK // 2 - 1:CHUNK // 2, :]
        q = _silu(q_ref[0, rows[i], :])
        v_i = i_ref[0, rows[i], :]
        for h in heads:
            q_in[i, h] = (q * jnp.exp(g_incl - g_mid))[:, sls[h]]
            k_in[i, h] = (kf * jnp.exp(g_mid - g_incl))[:, sls[h]]
            q_st[i, h] = (q * jnp.exp(g_incl))[:, sls[h]]
            k_tail[i, h] = (kf * jnp.exp(g_tot - g_incl))[:, sls[h]]
            gam[i, h] = jnp.exp(g_tot)[:, sls[h]]
            v[i, h] = v_i[:, sls[h]]
    att = {u: jnp.where(incl, _dot_nt(q_in[u], k_in[u]), 0.0) for u in units}
    o_loc = {u: _dot(att[u], v[u]) for u in units}
    kv = {u: _dot_tn(v[u], k_tail[u]) for u in units}

    s = [s_ref[h] for h in heads]
    for i in range(n):
        for h in heads:
            o_ref[0, 0, rows[i], sls[h]] = o_loc[i, h] + _dot_nt(q_st[i, h], s[h])
            s[h] = s[h] * gam[i, h] + kv[i, h]
    for h in heads:
        s_ref[h] = s[h]


def _hgrn_scan(p, lb, ctx_len):
    b, s, _ = p.shape
    nc, ncc = s // SCAN_BLOCK, ctx_len // SCAN_BLOCK

    def sec(idx):
        return pl.BlockSpec((1, SCAN_BLOCK, B_W), lambda i, d, j: (i, _chunk_order(d, j, ncc, nc), idx))

    return pl.pallas_call(
        _hgrn_chunk_kernel,
        grid=(b, 2, nc),
        in_specs=[sec(0), sec(1),
                  pl.BlockSpec((1, SCAN_BLOCK, B_W), lambda i, d, j: (i, _chunk_order(d, j, ncc, nc), 2 + d)),
                  pl.BlockSpec((1, 1, B_W), lambda i, d, j: (d, 0, 0))],
        out_specs=pl.BlockSpec((1, 1, SCAN_BLOCK, B_W), lambda i, d, j: (i, d, _chunk_order(d, j, ncc, nc), 0)),
        out_shape=jax.ShapeDtypeStruct((b, 2, s, B_W), F32),
        scratch_shapes=[pltpu.VMEM((B_HEADS, B_DK, B_DK), F32)],
        compiler_params=_cparams(("parallel", "parallel", "arbitrary")),
        name="hgrn_scan",
    )(p, p, p, lb)


def _rec_out_kernel(x_ref, modl_ref, modc_ref, oa_ref, g_ref, bonus_ref, ob_ref, gate_ref, lnw_ref, lnb_ref,
                    hgn_ref, ones_a_ref, ones_b_ref, w_ref, o_ref, *, ctx_len, tm):
    pos0 = pl.program_id(1) * tm
    oa = oa_ref[0, 0] + oa_ref[0, 1]
    ones_a = ones_a_ref[...]
    mean = _segsum(oa, ones_a) * (1.0 / A_HD)
    cen = oa - mean
    var = _segsum(cen * cen, ones_a) * (1.0 / A_HD)
    ya = (cen * lax.rsqrt(var + GN_EPS) * lnw_ref[...] + lnb_ref[...] + bonus_ref[0]) * g_ref[0]
    ob = ob_ref[0, 0] + ob_ref[0, 1]
    ms = _segsum(ob * ob, ones_b_ref[...]) * (1.0 / B_DK)
    yb = ob * lax.rsqrt(ms + NORM_EPS) * hgn_ref[...] * _silu(gate_ref[0])
    y = _dot(ya, w_ref[:A_W, :]) + _dot(yb, w_ref[A_W:, :])
    gate = _gate_vec(modl_ref, modc_ref, 2, pos0, tm, ctx_len)
    o_ref[0] = x_ref[0] + gate * y


def _rec_out(xcat, mods, ctx_row, oa, feats, ob, p_hgrn, ln_w, ln_b, hg_norm, w_out, ctx_len):
    b, s, d = xcat.shape
    tm = TOK_TILE
    row = lambda a: a.reshape(1, -1)
    full = lambda a: pl.BlockSpec(a.shape, lambda i, j: (0,) * a.ndim)
    consts = [row(ln_w), row(ln_b), row(jnp.tile(hg_norm, B_HEADS)), _block_ones(A_W, A_HD),
              _block_ones(B_W, B_DK), w_out]
    return pl.pallas_call(
        functools.partial(_rec_out_kernel, ctx_len=ctx_len, tm=tm),
        grid=(b, s // tm),
        in_specs=[pl.BlockSpec((1, tm, d), lambda i, j: (i, j, 0))] + _mod_specs(d, ctx_row) + [
            pl.BlockSpec((1, 2, tm, A_W), lambda i, j: (i, 0, j, 0)),
            pl.BlockSpec((1, tm, A_W), lambda i, j: (i, j, SEC_G)),
            pl.BlockSpec((1, tm, A_W), lambda i, j: (i, j, SEC_BONUS)),
            pl.BlockSpec((1, 2, tm, B_W), lambda i, j: (i, 0, j, 0)),
            pl.BlockSpec((1, tm, B_W), lambda i, j: (i, j, 4)),
        ] + [full(a) for a in consts],
        out_specs=pl.BlockSpec((1, tm, d), lambda i, j: (i, j, 0)),
        out_shape=jax.ShapeDtypeStruct((b, s, d), F32),
        compiler_params=_cparams(("parallel", "parallel")),
        name="rec_out",
    )(xcat, mods, mods, oa, feats, feats, ob, p_hgrn, *consts)


def _res_proj_kernel(x_ref, modl_ref, modc_ref, y_ref, w_ref, o_ref, *, ctx_len, tm):
    pos0 = pl.program_id(1) * tm
    gate = _gate_vec(modl_ref, modc_ref, 2, pos0, tm, ctx_len)
    o_ref[0] = x_ref[0] + gate * _dot(y_ref[0], w_ref[...])


def _res_proj(x, mods, ctx_row, y, w, ctx_len):
    b, s, d = x.shape
    k = y.shape[-1]
    tm = TOK_TILE
    return pl.pallas_call(
        functools.partial(_res_proj_kernel, ctx_len=ctx_len, tm=tm),
        grid=(b, s // tm),
        in_specs=[pl.BlockSpec((1, tm, d), lambda i, j: (i, j, 0))] + _mod_specs(d, ctx_row) + [
            pl.BlockSpec((1, tm, k), lambda i, j: (i, j, 0)),
            pl.BlockSpec((k, d), lambda i, j: (0, 0))],
        out_specs=pl.BlockSpec((1, tm, d), lambda i, j: (i, j, 0)),
        out_shape=jax.ShapeDtypeStruct((b, s, d), F32),
        compiler_params=_cparams(("parallel", "parallel")),
        name="res_proj",
    )(x, mods, mods, y, w)


def _ffn_kernel(x_ref, modl_ref, modc_ref, g_ref, wg_ref, wu_ref, wd_ref, o_ref, h_ref, acc_ref, *, ctx_len, tm,
                n_f):
    f = pl.program_id(2)
    pos0 = pl.program_id(1) * tm

    @pl.when(f == 0)
    def _():
        h_ref[...] = _bf(_modulate(x_ref[0], g_ref[...], modl_ref, modc_ref, 3, pos0, ctx_len))
        acc_ref[...] = jnp.zeros_like(acc_ref)

    h = h_ref[...]
    act = _silu(jnp.dot(h, wg_ref[...], preferred_element_type=F32)) * jnp.dot(h, wu_ref[...],
                                                                               preferred_element_type=F32)
    acc_ref[...] += _dot(act, wd_ref[...])

    @pl.when(f == n_f - 1)
    def _():
        o_ref[0] = x_ref[0] + _gate_vec(modl_ref, modc_ref, 5, pos0, tm, ctx_len) * acc_ref[...]


def _ffn(x, mods, ctx_row, g, wg, wu, wd, ctx_len, tm, tf):
    b, s, d = x.shape
    ff = wg.shape[1]
    n_f = ff // tf
    return pl.pallas_call(
        functools.partial(_ffn_kernel, ctx_len=ctx_len, tm=tm, n_f=n_f),
        grid=(b, s // tm, n_f),
        in_specs=[pl.BlockSpec((1, tm, d), lambda i, j, f: (i, j, 0))] + _mod_specs(d, ctx_row) + [
            pl.BlockSpec((1, d), lambda i, j, f: (0, 0)),
            pl.BlockSpec((d, tf), lambda i, j, f: (0, f)),
            pl.BlockSpec((d, tf), lambda i, j, f: (0, f)),
            pl.BlockSpec((tf, d), lambda i, j, f: (f, 0))],
        out_specs=pl.BlockSpec((1, tm, d), lambda i, j, f: (i, j, 0)),
        out_shape=jax.ShapeDtypeStruct((b, s, d), F32),
        scratch_shapes=[pltpu.VMEM((tm, d), BF16), pltpu.VMEM((tm, d), F32)],
        compiler_params=_cparams(("parallel", "parallel", "arbitrary")),
        name="ffn",
    )(x, mods, mods, g.reshape(1, d), wg, wu, wd)


def _attn_kernel(sink_ref, q_ref, kc_ref, vc_ref, kp_ref, kq_ref, kn_ref, vp_ref, vq_ref, vn_ref, o_ref, *,
                 ctx_len, n_lat):
    i = pl.program_id(1)
    nk = ctx_len + 3 * ATT_BLOCK
    k_all = jnp.concatenate([kc_ref[0], kp_ref[0], kq_ref[0], kn_ref[0]], axis=0)
    v_all = jnp.concatenate([vc_ref[0], vp_ref[0], vq_ref[0], vn_ref[0]], axis=0)
    row = lax.broadcasted_iota(jnp.int32, (ATT_BLOCK, nk), 0)
    col = lax.broadcasted_iota(jnp.int32, (ATT_BLOCK, nk), 1)
    rel = col - (ctx_len + ATT_BLOCK)
    kabs = i * ATT_BLOCK + rel
    band = (jnp.abs(row - rel) <= WINDOW) & (kabs >= 0) & (kabs < n_lat)
    bias = jnp.where((col < ctx_len) | band, 0.0, NEG_INF)
    bias = jnp.concatenate([bias] * GQ, axis=0)
    grp = lax.broadcasted_iota(jnp.int32, (GQ * ATT_BLOCK, 1), 0) // ATT_BLOCK
    scale = HD ** -0.5
    for hk in range(HKV):
        kh = _bf(k_all[:, hk * HD:(hk + 1) * HD])
        vh = _bf(v_all[:, hk * HD:(hk + 1) * HD])
        heads = [hk * GQ + g for g in range(GQ)]
        q = jnp.concatenate([q_ref[0, :, hq * HD:(hq + 1) * HD] for hq in heads], axis=0) * scale
        sk = jnp.zeros((GQ * ATT_BLOCK, 1), F32)
        for g, hq in enumerate(heads):
            sk = jnp.where(grp == g, sink_ref[hq], sk)
        s = _dot_nt(q, kh) + bias
        m = jnp.maximum(jnp.max(s, axis=-1, keepdims=True), sk)
        p = jnp.exp(s - m)
        den = jnp.sum(p, axis=-1, keepdims=True) + jnp.exp(sk - m)
        o = jnp.dot(_bf(p), vh, preferred_element_type=F32) / den
        for g, hq in enumerate(heads):
            o_ref[0, :, hq * HD:(hq + 1) * HD] = o[g * ATT_BLOCK:(g + 1) * ATT_BLOCK]


def _attention(qkv, sink, ctx_len):
    b, s, _ = qkv.shape
    n_lat = s - ctx_len
    nb = n_lat // ATT_BLOCK
    cb = ctx_len // ATT_BLOCK
    kcol = Q_COLS // KV_COLS
    vcol = kcol + 1

    def band(colblk, shift):
        return pl.BlockSpec((1, ATT_BLOCK, KV_COLS),
                            lambda bi, i: (bi, cb + jnp.clip(i + shift, 0, nb - 1), colblk))

    return pl.pallas_call(
        functools.partial(_attn_kernel, ctx_len=ctx_len, n_lat=n_lat),
        grid=(b, nb),
        in_specs=[pl.BlockSpec(memory_space=pltpu.SMEM),
                  pl.BlockSpec((1, ATT_BLOCK, Q_COLS), lambda bi, i: (bi, cb + i, 0)),
                  pl.BlockSpec((1, ctx_len, KV_COLS), lambda bi, i: (bi, 0, kcol)),
                  pl.BlockSpec((1, ctx_len, KV_COLS), lambda bi, i: (bi, 0, vcol)),
                  band(kcol, -1), band(kcol, 0), band(kcol, 1),
                  band(vcol, -1), band(vcol, 0), band(vcol, 1)],
        out_specs=pl.BlockSpec((1, ATT_BLOCK, Q_COLS), lambda bi, i: (bi, i, 0)),
        out_shape=jax.ShapeDtypeStruct((b, n_lat, Q_COLS), F32),
        compiler_params=_cparams(("parallel", "parallel")),
        name="attention",
    )(sink, qkv, qkv, qkv, qkv, qkv, qkv, qkv, qkv, qkv)


def _router_kernel(x_ref, modl_ref, modc_ref, g_ref, w_ref, b_ref, o_ref, e_ref, f_ref, *, tm):
    f = _modulate(x_ref[0], g_ref[...], modl_ref, modc_ref, 3, 0, 0)
    f_ref[0] = f
    logits = _dot_f32(f, w_ref[...]) + b_ref[...]
    lane = lax.broadcasted_iota(jnp.int32, logits.shape, 1).astype(F32)
    logits = jnp.where(lane < N_EXPERTS, logits, NEG_INF)
    ex = jnp.exp(logits - jnp.max(logits, axis=-1, keepdims=True))
    probs = ex / jnp.sum(ex, axis=-1, keepdims=True)
    p1 = jnp.max(probs, axis=-1, keepdims=True)
    i1 = jnp.min(jnp.where(probs == p1, lane, float(LANES)), axis=-1, keepdims=True)
    rest = jnp.where(lane == i1, -1.0, probs)
    p2 = jnp.max(rest, axis=-1, keepdims=True)
    i2 = jnp.min(jnp.where(rest == p2, lane, float(LANES)), axis=-1, keepdims=True)
    tot = p1 + p2
    o_ref[0] = jnp.where(lane == 0.0, p1 / tot, jnp.where(lane == 1.0, p2 / tot, 0.0))
    e_ref[0] = jnp.where(lane == 0.0, i1, jnp.where(lane == 1.0, i2, 0.0)).astype(jnp.int32)


def _router(x, mods, g, w, bias):
    b, s, d = x.shape
    tm = TOK_TILE
    wpad = jnp.zeros((d, LANES), F32).at[:, :N_EXPERTS].set(w)
    bpad = jnp.zeros((1, LANES), F32).at[0, :N_EXPERTS].set(bias)
    lane_spec = pl.BlockSpec((1, tm, LANES), lambda i, j: (i, j, 0))
    return pl.pallas_call(
        functools.partial(_router_kernel, tm=tm),
        grid=(b, s // tm),
        in_specs=[pl.BlockSpec((1, tm, d), lambda i, j: (i, j, 0))] + _mod_specs(d, 0) + [
            pl.BlockSpec((1, d), lambda i, j: (0, 0)),
            pl.BlockSpec((d, LANES), lambda i, j: (0, 0)),
            pl.BlockSpec((1, LANES), lambda i, j: (0, 0))],
        out_specs=[lane_spec, lane_spec, pl.BlockSpec((1, tm, d), lambda i, j: (i, j, 0))],
        out_shape=[jax.ShapeDtypeStruct((b, s, LANES), F32), jax.ShapeDtypeStruct((b, s, LANES), jnp.int32),
                   jax.ShapeDtypeStruct((b, s, d), F32)],
        compiler_params=_cparams(("parallel", "parallel")),
        name="router",
    )(x, mods, mods, g.reshape(1, d), wpad, bpad)


MOE_ROWS = 256


def _row_scatter_kernel(dest_ref, f_ref, xs_in_hbm, xs_hbm, sem, *, tm):
    del xs_in_hbm
    base = pl.program_id(0) * tm

    def issue(r, carry):
        slot = 2 * (base + r)
        pltpu.make_async_copy(f_ref.at[pl.ds(r, 1)], xs_hbm.at[pl.ds(dest_ref[slot], 1)], sem).start()
        pltpu.make_async_copy(f_ref.at[pl.ds(r, 1)], xs_hbm.at[pl.ds(dest_ref[slot + 1], 1)], sem).start()
        return carry

    lax.fori_loop(0, tm, issue, 0, unroll=8)
    for _ in range(2):
        pltpu.make_async_copy(f_ref, xs_hbm.at[pl.ds(0, tm)], sem).wait()


def _row_scatter(f, dest, n_rows):
    n_tok, d = f.shape
    tm = TOK_TILE
    return pl.pallas_call(
        functools.partial(_row_scatter_kernel, tm=tm),
        grid_spec=pltpu.PrefetchScalarGridSpec(
            num_scalar_prefetch=1,
            grid=(n_tok // tm,),
            in_specs=[pl.BlockSpec((tm, d), lambda i, dr: (i, 0)), pl.BlockSpec(memory_space=pl.ANY)],
            out_specs=pl.BlockSpec(memory_space=pl.ANY),
            scratch_shapes=[pltpu.SemaphoreType.DMA(())]),
        out_shape=jax.ShapeDtypeStruct((n_rows, d), f.dtype),
        input_output_aliases={2: 0},
        compiler_params=_cparams(("arbitrary",)),
        name="moe_scatter",
    )(dest, f, jnp.zeros((n_rows, d), f.dtype))


def _expert_kernel(be_ref, x_ref, wg_ref, wu_ref, wd_ref, o_ref):
    h = _bf(x_ref[...])
    act = _silu(jnp.dot(h, wg_ref[0], preferred_element_type=F32)) * jnp.dot(h, wu_ref[0],
                                                                              preferred_element_type=F32)
    o_ref[...] = _dot(act, wd_ref[0])


def _experts(xs, block_e, wg, wu, wd):
    n_rows, d = xs.shape
    ff = wg.shape[2]
    rows = MOE_ROWS
    return pl.pallas_call(
        _expert_kernel,
        grid_spec=pltpu.PrefetchScalarGridSpec(
            num_scalar_prefetch=1,
            grid=(n_rows // rows,),
            in_specs=[pl.BlockSpec((rows, d), lambda i, be: (i, 0)),
                      pl.BlockSpec((1, d, ff), lambda i, be: (be[i], 0, 0)),
                      pl.BlockSpec((1, d, ff), lambda i, be: (be[i], 0, 0)),
                      pl.BlockSpec((1, ff, d), lambda i, be: (be[i], 0, 0))],
            out_specs=pl.BlockSpec((rows, d), lambda i, be: (i, 0))),
        out_shape=jax.ShapeDtypeStruct((n_rows, d), F32),
        compiler_params=_cparams(("arbitrary",)),
        name="moe_experts",
    )(block_e, xs, wg, wu, wd)


def _combine_kernel(dest_ref, x_ref, modl_ref, modc_ref, w_ref, gfin_ref, ys_hbm, o_ref, y1_ref, y2_ref, sem, *,
                    tm, tiles_per_row):
    base = (pl.program_id(0) * tiles_per_row + pl.program_id(1)) * tm

    def issue(r, carry):
        slot = 2 * (base + r)
        pltpu.make_async_copy(ys_hbm.at[pl.ds(dest_ref[slot], 1)], y1_ref.at[pl.ds(r, 1)], sem).start()
        pltpu.make_async_copy(ys_hbm.at[pl.ds(dest_ref[slot + 1], 1)], y2_ref.at[pl.ds(r, 1)], sem).start()
        return carry

    lax.fori_loop(0, tm, issue, 0, unroll=8)
    pltpu.make_async_copy(ys_hbm.at[pl.ds(0, tm)], y1_ref, sem).wait()
    pltpu.make_async_copy(ys_hbm.at[pl.ds(0, tm)], y2_ref, sem).wait()
    w = w_ref[0]
    moe = w[:, 0:1] * y1_ref[...] + w[:, 1:2] * y2_ref[...]
    y = x_ref[0] + _gate_vec(modl_ref, modc_ref, 5, 0, tm, 0) * moe
    o_ref[0] = y * lax.rsqrt(jnp.mean(y * y, axis=-1, keepdims=True) + NORM_EPS) * gfin_ref[...]


def _combine(x, mods, tokw, dest, ys, final_g):
    b, s, d = x.shape
    tm = TOK_TILE
    tiles = s // tm
    return pl.pallas_call(
        functools.partial(_combine_kernel, tm=tm, tiles_per_row=tiles),
        grid_spec=pltpu.PrefetchScalarGridSpec(
            num_scalar_prefetch=1,
            grid=(b, tiles),
            in_specs=[pl.BlockSpec((1, tm, d), lambda i, j, dr: (i, j, 0)),
                      pl.BlockSpec((1, N_MOD, d), lambda i, j, dr: (i, 0, 0)),
                      pl.BlockSpec((1, N_MOD, d), lambda i, j, dr: (0, 0, 0)),
                      pl.BlockSpec((1, tm, LANES), lambda i, j, dr: (i, j, 0)),
                      pl.BlockSpec((1, d), lambda i, j, dr: (0, 0)),
                      pl.BlockSpec(memory_space=pl.ANY)],
            out_specs=pl.BlockSpec((1, tm, d), lambda i, j, dr: (i, j, 0)),
            scratch_shapes=[pltpu.VMEM((tm, d), F32), pltpu.VMEM((tm, d), F32), pltpu.SemaphoreType.DMA(())]),
        out_shape=jax.ShapeDtypeStruct((b, s, d), F32),
        compiler_params=_cparams(("arbitrary", "arbitrary")),
        name="moe_combine",
    )(dest, x, mods, mods, tokw, final_g.reshape(1, d), ys)


def _moe_routing(eidx):
    m = eidx.shape[0]
    oh = (eidx[:, None] == jnp.arange(N_EXPERTS, dtype=jnp.int32)[None, :]).astype(jnp.int32)
    csum = jnp.cumsum(oh, axis=0)
    rank = jnp.sum((csum - oh) * oh, axis=1)
    counts = csum[-1]
    padded = (counts + MOE_ROWS - 1) // MOE_ROWS * MOE_ROWS
    pad_end = jnp.cumsum(padded)
    dest = jnp.sum(oh * (pad_end - padded)[None, :], axis=1) + rank
    n_blocks = m // MOE_ROWS + N_EXPERTS
    starts = jnp.arange(n_blocks, dtype=jnp.int32) * MOE_ROWS
    block_e = jnp.minimum(jnp.sum((starts[:, None] >= pad_end[None, :]).astype(jnp.int32), axis=1), N_EXPERTS - 1)
    return dest.astype(jnp.int32), block_e, n_blocks * MOE_ROWS


def _rope_tables(n_lat, ctx_len):
    rows = n_lat // GRID_W
    row = jnp.repeat(jnp.arange(rows, dtype=F32), GRID_W)
    col = jnp.tile(jnp.arange(GRID_W, dtype=F32), rows)
    inv = ROPE_BASE ** (-jnp.arange(0, AX_DIM, 2, dtype=F32) / AX_DIM)
    ar, ac = row[:, None] * inv, col[:, None] * inv
    cos = jnp.concatenate([jnp.cos(ar), jnp.cos(ar), jnp.cos(ac), jnp.cos(ac)], axis=-1)
    sin = jnp.concatenate([-jnp.sin(ar), jnp.sin(ar), -jnp.sin(ac), jnp.sin(ac)], axis=-1)
    n_heads = ROPE_COLS // HD
    cos = jnp.concatenate([jnp.ones((ctx_len, HD), F32), cos], axis=0)
    sin = jnp.concatenate([jnp.zeros((ctx_len, HD), F32), sin], axis=0)
    half = AX_DIM // 2
    j = jnp.arange(HD)
    partner = jnp.where((j % AX_DIM) < half, j + half, j - half)
    perm = (jnp.arange(n_heads)[:, None] * HD + partner[None, :]).reshape(-1)
    return jnp.tile(cos, (1, n_heads)), jnp.tile(sin, (1, n_heads)), perm


def kernel(x, c, ctx, c_ctx, mod_w, mod_b, norm_mix, norm_ffn, norm_final, rec_w_in, rec_w_out, rwkv_mu, rwkv_w0, rwkv_w_up, rwkv_a0, rwkv_a_up, rwkv_g_up, rwkv_k_k, rwkv_k_a, rwkv_r_k, rwkv_ln_w, rwkv_ln_b, hgrn_lb, hgrn_norm, ffn_w_gate, ffn_w_up, ffn_w_down, att_w_in, att_w_out, att_sink, moe_router, moe_router_b, moe_w_gate, moe_w_up, moe_w_down):
    bsz, n_lat, d = x.shape
    ctx_len = ctx.shape[1]
    xcat = jnp.concatenate([ctx, x], axis=1)

    n_rows = -(-(bsz + 1) // 8) * 8
    cvec = jnp.zeros((n_rows, d), F32).at[:bsz].set(c).at[bsz].set(c_ctx)
    mods = [_adaln(cvec, mod_w[l], mod_b[l]).reshape(n_rows, N_MOD, d) for l in range(2)]

    p_rwkv, p_hgrn = _proj(xcat, mods[0], bsz, norm_mix[0], _bf(rec_w_in[0]), (RWKV_COLS, HGRN_COLS), ctx_len)
    feats = _rwkv_features(p_rwkv, rwkv_mu[0], rwkv_w0[0], rwkv_w_up[0], rwkv_a0[0], rwkv_a_up[0], rwkv_g_up[0],
                           rwkv_k_k[0], rwkv_k_a[0], rwkv_r_k[0].reshape(-1), ctx_len)
    oa = _rwkv_scan(feats, ctx_len)
    lb = jnp.cumsum(jax.nn.softmax(hgrn_lb.astype(F32), axis=1), axis=1)[:, 0].reshape(2, 1, B_W)
    ob = _hgrn_scan(p_hgrn, lb, ctx_len)
    xcat = _rec_out(xcat, mods[0], bsz, oa, feats, ob, p_hgrn, rwkv_ln_w[0], rwkv_ln_b[0], hgrn_norm[0],
                    _bf(rec_w_out[0]), ctx_len)
    xcat = _ffn(xcat, mods[0], bsz, norm_ffn[0], _bf(ffn_w_gate[0]), _bf(ffn_w_up[0]), _bf(ffn_w_down[0]), ctx_len,
                tm=768, tf=1408)

    cos, sin, perm = _rope_tables(n_lat, ctx_len)
    w_att = att_w_in[0]
    (qkv,) = _proj(xcat, mods[1], bsz, norm_mix[1], _bf(w_att), (ATT_COLS,), ctx_len,
                   rope=(_bf(w_att[:, perm]), cos, sin))
    att = _attention(qkv, att_sink[0], ctx_len)
    x_lat = _res_proj(xcat[:, ctx_len:], mods[1], 0, att, _bf(att_w_out[0]), 0)
    tokw, eidx, f_lat = _router(x_lat, mods[1], norm_ffn[1], moe_router[0], moe_router_b[0])
    dest, block_e, n_rows = _moe_routing(eidx[..., :2].reshape(-1))
    xs = _row_scatter(f_lat.reshape(bsz * n_lat, d), dest, n_rows)
    ys = _experts(xs, block_e, _bf(moe_w_gate[0]), _bf(moe_w_up[0]), _bf(moe_w_down[0]))
    return _combine(x_lat, mods[1], tokw, dest, ys, norm_final)
```

```python
import functools

import jax
import jax.numpy as jnp
from jax import lax
from jax.experimental import pallas as pl
from jax.experimental.pallas import tpu as pltpu

F32 = jnp.float32
BF16 = jnp.bfloat16
HIGHEST = lax.Precision.HIGHEST

N_MOD = 6
NORM_EPS = 1e-6
NEG_INF = -1e30

A_HEADS = 8
A_HD = 64
A_W = A_HEADS * A_HD
DECAY_LORA = 64
AAA_LORA = 64
GATE_LORA = 128
RWKV_COLS = 3 * A_W + DECAY_LORA + AAA_LORA + GATE_LORA
GN_EPS = 64e-5

B_HEADS = 4
B_DK = 128
B_W = B_HEADS * B_DK
HGRN_COLS = 5 * B_W

HQ = 16
HKV = 4
GQ = HQ // HKV
HD = 64
WINDOW = 128
ATT_BLOCK = 128
AX_DIM = HD // 2
ROPE_BASE = 10000.0
GRID_W = 64
Q_COLS = HQ * HD
KV_COLS = HKV * HD
ROPE_COLS = Q_COLS + KV_COLS
ATT_COLS = Q_COLS + 2 * KV_COLS

N_EXPERTS = 8
LANES = 128
CHUNK = 64
SCAN_BLOCK = 256
MXU_WIDTH = 256
RWKV_GROUP = MXU_WIDTH // A_HD
TOK_TILE = 256
VMEM_LIMIT = 56 * 1024 * 1024

SEC_R, SEC_K, SEC_V, SEC_KK, SEC_B, SEC_LW0, SEC_LW1, SEC_G, SEC_BONUS = range(9)
N_SEC = 9


def _cparams(sem):
    return pltpu.CompilerParams(dimension_semantics=sem, vmem_limit_bytes=VMEM_LIMIT)


def _bf(x):
    return x.astype(BF16)


def _dot(a, b):
    return jnp.dot(_bf(a), _bf(b), preferred_element_type=F32)


def _dot_nt(a, b):
    return lax.dot_general(_bf(a), _bf(b), (((1,), (1,)), ((), ())), preferred_element_type=F32)


def _dot_tn(a, b):
    return lax.dot_general(_bf(a), _bf(b), (((0,), (0,)), ((), ())), preferred_element_type=F32)


def _dot_f32(a, b):
    return jnp.dot(a, b, preferred_element_type=F32, precision=HIGHEST)


def _dot_tn_f32(a, b):
    return lax.dot_general(a, b, (((0,), (0,)), ((), ())), preferred_element_type=F32, precision=HIGHEST)


def _sigmoid(x):
    return 1.0 / (1.0 + jnp.exp(-x))


def _silu(x):
    return x * _sigmoid(x)


def _segsum(x, ones_bd):
    hi = _bf(x)
    r1 = x - hi.astype(F32)
    mid = _bf(r1)
    lo = _bf(r1 - mid.astype(F32))
    acc = jnp.dot(hi, ones_bd, preferred_element_type=F32)
    acc += jnp.dot(mid, ones_bd, preferred_element_type=F32)
    acc += jnp.dot(lo, ones_bd, preferred_element_type=F32)
    return acc


def _block_ones(width, seg):
    i = jnp.arange(width) // seg
    return (i[:, None] == i[None, :]).astype(BF16)


def _modulate(x, g, modl_ref, modc_ref, row, pos0, ctx_len):
    y = x * lax.rsqrt(jnp.mean(x * x, axis=-1, keepdims=True) + NORM_EPS) * g
    pos = pos0 + lax.broadcasted_iota(jnp.int32, (x.shape[0], 1), 0)
    is_ctx = pos < ctx_len
    shift = jnp.where(is_ctx, modc_ref[0, row:row + 1, :], modl_ref[0, row:row + 1, :])
    scale = jnp.where(is_ctx, modc_ref[0, row + 1:row + 2, :], modl_ref[0, row + 1:row + 2, :])
    return y * (1.0 + scale) + shift


def _gate_vec(modl_ref, modc_ref, row, pos0, n, ctx_len):
    pos = pos0 + lax.broadcasted_iota(jnp.int32, (n, 1), 0)
    return jnp.where(pos < ctx_len, modc_ref[0, row:row + 1, :], modl_ref[0, row:row + 1, :])


def _mod_specs(d, ctx_row, batch_axis=0):
    def lat_map(*idx):
        return (idx[batch_axis], 0, 0)

    def ctx_map(*idx):
        return (ctx_row, 0, 0)

    return [pl.BlockSpec((1, N_MOD, d), lat_map), pl.BlockSpec((1, N_MOD, d), ctx_map)]


def _adaln_kernel(c_ref, w_ref, b_ref, o_ref):
    o_ref[...] = _dot(_silu(c_ref[...]), w_ref[...]) + b_ref[...]


def _adaln(cvec, w, b):
    r, d = cvec.shape
    n = w.shape[1]
    tn = 1024
    return pl.pallas_call(
        _adaln_kernel,
        grid=(n // tn,),
        in_specs=[pl.BlockSpec((r, d), lambda j: (0, 0)),
                  pl.BlockSpec((d, tn), lambda j: (0, j)),
                  pl.BlockSpec((1, tn), lambda j: (0, j))],
        out_specs=pl.BlockSpec((r, tn), lambda j: (0, j)),
        out_shape=jax.ShapeDtypeStruct((r, n), F32),
        compiler_params=_cparams(("parallel",)),
        name="adaln",
    )(cvec, w, b.reshape(1, n))


def _proj_kernel(x_ref, modl_ref, modc_ref, g_ref, w_ref, *rest, ctx_len, tm, splits, rope_cols):
    if rope_cols:
        wrot_ref, cos_ref, sin_ref = rest[:3]
        outs = rest[3:]
    else:
        outs = rest
    pos0 = pl.program_id(1) * tm
    h = _bf(_modulate(x_ref[0], g_ref[...], modl_ref, modc_ref, 0, pos0, ctx_len))
    y = jnp.dot(h, w_ref[...], preferred_element_type=F32)
    if rope_cols:
        yr = jnp.dot(h, wrot_ref[...], preferred_element_type=F32)
        roped = y[:, :rope_cols] * cos_ref[...] + yr * sin_ref[...]
        outs[0][0, :, :rope_cols] = roped
        outs[0][0, :, rope_cols:] = y[:, rope_cols:]
    else:
        lo = 0
        for o_ref, width in zip(outs, splits):
            o_ref[0] = y[:, lo:lo + width]
            lo += width


def _proj(xcat, mods, ctx_row, g, w, splits, ctx_len, rope=None):
    b, s, d = xcat.shape
    n = w.shape[1]
    tm = TOK_TILE
    in_specs = [pl.BlockSpec((1, tm, d), lambda i, j: (i, j, 0))] + _mod_specs(d, ctx_row) + [
        pl.BlockSpec((1, d), lambda i, j: (0, 0)),
        pl.BlockSpec((d, n), lambda i, j: (0, 0))]
    args = [xcat, mods, mods, g.reshape(1, d), w]
    rope_cols = 0
    if rope is not None:
        wrot, cos, sin = rope
        rope_cols = wrot.shape[1]
        in_specs += [pl.BlockSpec((d, rope_cols), lambda i, j: (0, 0)),
                     pl.BlockSpec((tm, rope_cols), lambda i, j: (j, 0)),
                     pl.BlockSpec((tm, rope_cols), lambda i, j: (j, 0))]
        args += [wrot, cos, sin]
    out_shape = [jax.ShapeDtypeStruct((b, s, width), F32) for width in splits]
    out_specs = [pl.BlockSpec((1, tm, width), lambda i, j: (i, j, 0)) for width in splits]
    return pl.pallas_call(
        functools.partial(_proj_kernel, ctx_len=ctx_len, tm=tm, splits=tuple(splits), rope_cols=rope_cols),
        grid=(b, s // tm),
        in_specs=in_specs,
        out_specs=out_specs,
        out_shape=out_shape,
        compiler_params=_cparams(("parallel", "parallel")),
        name="proj_rope" if rope_cols else "proj",
    )(*args)


def _softplus(x):
    return jnp.maximum(x, 0.0) + jnp.log(1.0 + jnp.exp(-jnp.abs(x)))


def _rwkv_feat_kernel(p_ref, pp_ref, pn_ref, mu_ref, w0_ref, wup_ref, a0_ref, aup_ref, gup_ref,
                      kk_ref, ka_ref, rk_ref, ones_ref, o_ref, *, ctx_len, seq_len, tm):
    pos0 = pl.program_id(1) * tm
    p = p_ref[0]
    rows = lax.broadcasted_iota(jnp.int32, (tm, 1), 0)
    pos = rows + pos0
    prev = jnp.where(rows == 0, pp_ref[0, 7:8, :], pltpu.roll(p, 1, 0))
    prev = jnp.where((pos == 0) | (pos == ctx_len), 0.0, prev)
    nxt = jnp.where(rows == tm - 1, pn_ref[0, 0:1, :], pltpu.roll(p, tm - 1, 0))
    nxt = jnp.where((pos == ctx_len - 1) | (pos == seq_len - 1), 0.0, nxt)
    p = p + mu_ref[...] * (0.5 * (prev + nxt) - p)

    r = p[:, 0:A_W]
    k = p[:, A_W:2 * A_W]
    v = p[:, 2 * A_W:3 * A_W]
    lo = 3 * A_W
    wd = p[:, lo:lo + DECAY_LORA]
    ad = p[:, lo + DECAY_LORA:lo + DECAY_LORA + AAA_LORA]
    gd = p[:, lo + DECAY_LORA + AAA_LORA:lo + DECAY_LORA + AAA_LORA + GATE_LORA]

    tw = jnp.tanh(wd)
    for d in range(2):
        w_log = -_softplus(-(w0_ref[d:d + 1, :] + _dot(tw, wup_ref[d]))) - 0.5
        o_ref[0, :, (SEC_LW0 + d) * A_W:(SEC_LW0 + d + 1) * A_W] = -jnp.exp(w_log)
    a = _sigmoid(a0_ref[...] + _dot(ad, aup_ref[...]))
    g = _dot(_sigmoid(gd), gup_ref[...])
    ones_bd = ones_ref[...]
    kk = k * kk_ref[...]
    kk = kk / jnp.maximum(jnp.sqrt(_segsum(kk * kk, ones_bd)), 1e-12)
    k = k * (1.0 + (a - 1.0) * ka_ref[...])
    bonus = _segsum(r * k * rk_ref[...], ones_bd) * v
    for sec, val in ((SEC_R, r), (SEC_K, k), (SEC_V, v), (SEC_KK, kk), (SEC_B, kk * a), (SEC_G, g),
                     (SEC_BONUS, bonus)):
        o_ref[0, :, sec * A_W:(sec + 1) * A_W] = val


def _rwkv_features(p, mu, w0, w_up, a0, a_up, g_up, k_k, k_a, r_k, ctx_len):
    b, s, c = p.shape
    tm = TOK_TILE
    nb8 = s // 8
    row = lambda a: a.reshape(1, -1)
    full = lambda a: pl.BlockSpec(a.shape, lambda i, j: (0,) * a.ndim)
    args = [row(mu), w0, w_up, row(a0), a_up, g_up, row(k_k), row(k_a), row(r_k), _block_ones(A_W, A_HD)]
    return pl.pallas_call(
        functools.partial(_rwkv_feat_kernel, ctx_len=ctx_len, seq_len=s, tm=tm),
        grid=(b, s // tm),
        in_specs=[pl.BlockSpec((1, tm, c), lambda i, j: (i, j, 0)),
                  pl.BlockSpec((1, 8, c), lambda i, j: (i, jnp.maximum(j * (tm // 8) - 1, 0), 0)),
                  pl.BlockSpec((1, 8, c), lambda i, j: (i, jnp.minimum((j + 1) * (tm // 8), nb8 - 1), 0)),
                  ] + [full(a) for a in args],
        out_specs=pl.BlockSpec((1, tm, N_SEC * A_W), lambda i, j: (i, j, 0)),
        out_shape=jax.ShapeDtypeStruct((b, s, N_SEC * A_W), F32),
        compiler_params=_cparams(("parallel", "parallel")),
        name="rwkv_features",
    )(p, p, p, *args)


def _chunk_order(d, j, n_ctx_chunks, n_chunks):
    back = jnp.where(j < n_ctx_chunks, n_ctx_chunks - 1 - j, n_chunks - 1 + n_ctx_chunks - j)
    return jnp.where(d == 0, j, back)


def _order_masks(rev):
    row = lax.broadcasted_iota(jnp.int32, (CHUNK, CHUNK), 0)
    col = lax.broadcasted_iota(jnp.int32, (CHUNK, CHUNK), 1)
    diff = jnp.where(rev, row - col, col - row)
    return diff < 0, diff <= 0, row == col


def _dot_split(a, x):
    hi = _bf(x)
    r1 = x - hi.astype(F32)
    mid = _bf(r1)
    lo = _bf(r1 - mid.astype(F32))
    acc = jnp.dot(a, hi, preferred_element_type=F32)
    acc += jnp.dot(a, mid, preferred_element_type=F32)
    acc += jnp.dot(a, lo, preferred_element_type=F32)
    return acc


def _scan_step_coords(t, n_blocks, n_steps):
    return jnp.minimum(t, n_steps - 1), jnp.maximum(t - 1, 0)


def _rwkv_chunk_kernel(r_ref, k_ref, v_ref, kk_ref, b_ref, lw_ref, o_ref, s_ref, tr_ref, uu_ref, ol_ref, ab_ref,
                       m2_ref, cc_ref, gm_ref, *, n_blocks, n_steps):
    step = pl.program_id(0)
    t_in, t_out = _scan_step_coords(step, n_blocks, n_steps)
    rev = (t_in // n_blocks) % 2 == 1
    rev_out = (t_out // n_blocks) % 2 == 1
    first_out = t_out % n_blocks == 0
    w_slot = step % 2
    r_slot = 1 - w_slot

    @pl.when(step == 0)
    def _():
        s_ref[...] = jnp.zeros_like(s_ref)
        for ref in (tr_ref, uu_ref, ol_ref, ab_ref, m2_ref, cc_ref, gm_ref):
            ref[1] = jnp.zeros(ref.shape[1:], ref.dtype)

    _, incl, _ = _order_masks(rev)
    incl_bf = jnp.where(incl, 1.0, 0.0).astype(BF16)
    gw = RWKV_GROUP * A_HD
    groups = range(A_HEADS // RWKV_GROUP)
    gsl = [slice(g * gw, (g + 1) * gw) for g in groups]
    row = lax.broadcasted_iota(jnp.int32, (CHUNK, gw), 0)
    col = lax.broadcasted_iota(jnp.int32, (CHUNK, gw), 1) % A_HD
    diff = jnp.where(rev, row - col, col - row)
    strict_c = diff < 0
    incl_c = diff <= 0
    eye_c = jnp.where(diff == 0, 1.0, 0.0)
    bd_mask = (lax.broadcasted_iota(jnp.int32, (gw, gw), 0) // A_HD
               == lax.broadcasted_iota(jnp.int32, (gw, gw), 1) // A_HD)

    def bd(x):
        return jnp.where(bd_mask, jnp.concatenate([x] * RWKV_GROUP, axis=0), 0.0)

    def stack(x):
        return jnp.concatenate([x[:, h * A_HD:(h + 1) * A_HD] for h in range(RWKV_GROUP)], axis=0)

    n = SCAN_BLOCK // CHUNK
    rows = [pl.ds(pl.multiple_of(jnp.where(rev, (n - 1 - i) * CHUNK, i * CHUNK), CHUNK), CHUNK) for i in range(n)]
    rows_out = [pl.ds(pl.multiple_of(jnp.where(rev_out, (n - 1 - i) * CHUNK, i * CHUNK), CHUNK), CHUNK)
                for i in range(n)]
    units = [(i, g) for i in range(n) for g in groups]

    s = [jnp.where(first_out, 0.0, s_ref[g]) for g in groups]
    zr = {}
    pieces = []

    def advance(i):
        for g in groups:
            ui = i * len(groups) + g
            zr[i, g] = _dot_nt(tr_ref[r_slot, ui], bd(s[g]))
            s[g] = s[g] * gm_ref[r_slot, ui] - jnp.dot(_bf(s[g]), m2_ref[r_slot, ui],
                                                       preferred_element_type=F32) + cc_ref[r_slot, ui]

    def emit_out(i):
        for g in groups:
            ui = i * len(groups) + g
            z = zr[i, g][:CHUNK] + uu_ref[r_slot, ui]
            o_ref[0, 0, rows_out[i], gsl[g]] = (zr[i, g][CHUNK:] + ol_ref[r_slot, ui]
                                                - _dot(ab_ref[r_slot, ui], bd(z)))

    for i in range(n):
        pieces += [functools.partial(advance, i), functools.partial(emit_out, i)]

    def state_piece():
        if pieces:
            pieces.pop(0)()

    kap, rt, kbar, bbar, kgam, bgam, gam, v = ({} for _ in range(8))
    for i in range(n):
        lw = lw_ref[0, rows[i], :]
        g_incl = _dot_split(incl_bf, lw)
        g_tot = jnp.sum(lw, axis=0, keepdims=True)
        e_ng = jnp.exp(-g_incl)
        e_tail = jnp.exp(g_tot - g_incl)
        k = k_ref[0, rows[i], :]
        b = b_ref[0, rows[i], :]
        kap_i = kk_ref[0, rows[i], :] * jnp.exp(g_incl - lw)
        rt_i = r_ref[0, rows[i], :] * jnp.exp(g_incl)
        v_i = v_ref[0, rows[i], :]
        for g in groups:
            kap[i, g], rt[i, g], v[i, g] = kap_i[:, gsl[g]], rt_i[:, gsl[g]], v_i[:, gsl[g]]
            kbar[i, g], bbar[i, g] = (k * e_ng)[:, gsl[g]], (b * e_ng)[:, gsl[g]]
            kgam[i, g], bgam[i, g] = (k * e_tail)[:, gsl[g]], (b * e_tail)[:, gsl[g]]
            gam[i, g] = jnp.exp(g_tot)[:, gsl[g]]

    x = {u: jnp.concatenate([kap[u], rt[u]], axis=0) for u in units}
    yb = {u: _dot_nt(x[u], bd(bbar[u])) for u in units}
    state_piece()
    a = {u: jnp.where(strict_c, -yb[u][:CHUNK], 0.0) for u in units}
    xs = {u: eye_c + a[u] for u in units}
    pw = {u: _dot(a[u], bd(a[u])) for u in units}
    state_piece()
    for _ in range(4):
        st = {u: _dot(jnp.concatenate([pw[u], xs[u]], axis=0), bd(pw[u])) for u in units}
        pw = {u: st[u][:CHUNK] for u in units}
        xs = {u: xs[u] + st[u][CHUNK:] for u in units}
        state_piece()
    t_inv = {u: xs[u] + _dot(xs[u], bd(pw[u])) for u in units}
    state_piece()
    yk = {u: _dot_nt(x[u], bd(kbar[u])) for u in units}
    bd_v = {u: _bf(bd(v[u])) for u in units}
    w1 = {u: _dot(jnp.where(strict_c, yk[u][:CHUNK], 0.0), bd_v[u]) for u in units}
    state_piece()
    tk = {u: _dot(t_inv[u], bd(kap[u])) for u in units}
    uu = {u: _dot(t_inv[u], bd(w1[u])) for u in units}
    while pieces:
        state_piece()
    for g in groups:
        s_ref[g] = s[g]
    o_loc = {u: _dot(jnp.where(incl_c, yk[u][CHUNK:], 0.0), bd_v[u]) for u in units}
    a_rb = {u: jnp.where(incl_c, yb[u][CHUNK:], 0.0) for u in units}
    bd_bg = {u: _bf(bd(bgam[u])) for u in units}
    bd_m2 = {u: _bf(bd(_dot_tn(stack(tk[u]), bd_bg[u]))) for u in units}
    cc = {u: _dot_tn(stack(v[u]), bd(kgam[u])) - _dot_tn(stack(uu[u]), bd_bg[u]) for u in units}
    for u in units:
        ui = u[0] * len(groups) + u[1]
        tr_ref[w_slot, ui] = jnp.concatenate([tk[u], rt[u]], axis=0)
        uu_ref[w_slot, ui] = uu[u]
        ol_ref[w_slot, ui] = o_loc[u]
        ab_ref[w_slot, ui] = a_rb[u]
        m2_ref[w_slot, ui] = bd_m2[u]
        cc_ref[w_slot, ui] = cc[u]
        gm_ref[w_slot, ui] = gam[u]


def _rwkv_scan(feats, ctx_len):
    b, s, _ = feats.shape
    nc, ncc = s // SCAN_BLOCK, ctx_len // SCAN_BLOCK

    n_steps = b * 2 * nc

    def coords(flat):
        d = (flat // nc) % 2
        return flat // (2 * nc), d, _chunk_order(d, flat % nc, ncc, nc)

    def in_map(sec_idx, per_dir):
        def index_map(t):
            bi, d, blk = coords(_scan_step_coords(t, nc, n_steps)[0])
            return bi, blk, sec_idx + (d if per_dir else 0)
        return index_map

    def out_map(t):
        bi, d, blk = coords(_scan_step_coords(t, nc, n_steps)[1])
        return bi, d, blk, 0

    def sec(idx, per_dir=False):
        return pl.BlockSpec((1, SCAN_BLOCK, A_W), in_map(idx, per_dir))

    n_groups = A_HEADS // RWKV_GROUP
    n_units = SCAN_BLOCK // CHUNK * n_groups
    gw = RWKV_GROUP * A_HD
    return pl.pallas_call(
        functools.partial(_rwkv_chunk_kernel, n_blocks=nc, n_steps=n_steps),
        grid=(n_steps + 1,),
        in_specs=[sec(SEC_R), sec(SEC_K), sec(SEC_V), sec(SEC_KK), sec(SEC_B), sec(SEC_LW0, per_dir=True)],
        out_specs=pl.BlockSpec((1, 1, SCAN_BLOCK, A_W), out_map),
        out_shape=jax.ShapeDtypeStruct((b, 2, s, A_W), F32),
        scratch_shapes=[pltpu.VMEM((n_groups, A_HD, gw), F32),
                        pltpu.VMEM((2, n_units, 2 * CHUNK, gw), F32),
                        pltpu.VMEM((2, n_units, CHUNK, gw), F32),
                        pltpu.VMEM((2, n_units, CHUNK, gw), F32),
                        pltpu.VMEM((2, n_units, CHUNK, gw), F32),
                        pltpu.VMEM((2, n_units, gw, gw), BF16),
                        pltpu.VMEM((2, n_units, CHUNK, gw), F32),
                        pltpu.VMEM((2, n_units, 1, gw), F32)],
        compiler_params=_cparams(("arbitrary",)),
        name="rwkv_scan",
    )(feats, feats, feats, feats, feats, feats)


def _hgrn_chunk_kernel(q_ref, i_ref, f_ref, lb_ref, o_ref, s_ref):
    rev = pl.program_id(1) == 1

    @pl.when(pl.program_id(2) == 0)
    def _():
        s_ref[...] = jnp.zeros_like(s_ref)

    _, incl, _ = _order_masks(rev)
    incl_bf = jnp.where(incl, 1.0, 0.0).astype(BF16)
    lb = lb_ref[0]
    heads = range(B_HEADS)
    sls = [slice(h * B_DK, (h + 1) * B_DK) for h in heads]

    n = SCAN_BLOCK // CHUNK
    rows = [pl.ds(pl.multiple_of(jnp.where(rev, (n - 1 - i) * CHUNK, i * CHUNK), CHUNK), CHUNK) for i in range(n)]
    units = [(i, h) for i in range(n) for h in heads]
    q_in, k_in, q_st, k_tail, gam, v = ({} for _ in range(6))
    for i in range(n):
        f = lb + (1.0 - lb) * _sigmoid(f_ref[0, rows[i], :])
        logf = jnp.log(f)
        kf = 1.0 - f
        g_incl = _dot_split(incl_bf, logf)
        g_tot = jnp.sum(logf, axis=0, keepdims=True)
        g_mid = g_incl[CHUNK // 2 - 1:CHUNK // 2, :]
        q = _silu(q_ref[0, rows[i], :])
        v_i = i_ref[0, rows[i], :]
        for h in heads:
            q_in[i, h] = (q * jnp.exp(g_incl - g_mid))[:, sls[h]]
            k_in[i, h] = (kf * jnp.exp(g_mid - g_incl))[:, sls[h]]
            q_st[i, h] = (q * jnp.exp(g_incl))[:, sls[h]]
            k_tail[i, h] = (kf * jnp.exp(g_tot - g_incl))[:, sls[h]]
            gam[i, h] = jnp.exp(g_tot)[:, sls[h]]
            v[i, h] = v_i[:, sls[h]]
    att = {u: jnp.where(incl, _dot_nt(q_in[u], k_in[u]), 0.0) for u in units}
    o_loc = {u: _dot(att[u], v[u]) for u in units}
    kv = {u: _dot_tn(v[u], k_tail[u]) for u in units}

    s = [s_ref[h] for h in heads]
    for i in range(n):
        for h in heads:
            o_ref[0, 0, rows[i], sls[h]] = o_loc[i, h] + _dot_nt(q_st[i, h], s[h])
            s[h] = s[h] * gam[i, h] + kv[i, h]
    for h in heads:
        s_ref[h] = s[h]


def _hgrn_scan(p, lb, ctx_len):
    b, s, _ = p.shape
    nc, ncc = s // SCAN_BLOCK, ctx_len // SCAN_BLOCK

    def sec(idx):
        return pl.BlockSpec((1, SCAN_BLOCK, B_W), lambda i, d, j: (i, _chunk_order(d, j, ncc, nc), idx))

    return pl.pallas_call(
        _hgrn_chunk_kernel,
        grid=(b, 2, nc),
        in_specs=[sec(0), sec(1),
                  pl.BlockSpec((1, SCAN_BLOCK, B_W), lambda i, d, j: (i, _chunk_order(d, j, ncc, nc), 2 + d)),
                  pl.BlockSpec((1, 1, B_W), lambda i, d, j: (d, 0, 0))],
        out_specs=pl.BlockSpec((1, 1, SCAN_BLOCK, B_W), lambda i, d, j: (i, d, _chunk_order(d, j, ncc, nc), 0)),
        out_shape=jax.ShapeDtypeStruct((b, 2, s, B_W), F32),
        scratch_shapes=[pltpu.VMEM((B_HEADS, B_DK, B_DK), F32)],
        compiler_params=_cparams(("parallel", "parallel", "arbitrary")),
        name="hgrn_scan",
    )(p, p, p, lb)


def _rec_out_kernel(x_ref, modl_ref, modc_ref, oa_ref, g_ref, bonus_ref, ob_ref, gate_ref, lnw_ref, lnb_ref,
                    hgn_ref, ones_a_ref, ones_b_ref, w_ref, o_ref, *, ctx_len, tm):
    pos0 = pl.program_id(1) * tm
    oa = oa_ref[0, 0] + oa_ref[0, 1]
    ones_a = ones_a_ref[...]
    mean = _segsum(oa, ones_a) * (1.0 / A_HD)
    cen = oa - mean
    var = _segsum(cen * cen, ones_a) * (1.0 / A_HD)
    ya = (cen * lax.rsqrt(var + GN_EPS) * lnw_ref[...] + lnb_ref[...] + bonus_ref[0]) * g_ref[0]
    ob = ob_ref[0, 0] + ob_ref[0, 1]
    ms = _segsum(ob * ob, ones_b_ref[...]) * (1.0 / B_DK)
    yb = ob * lax.rsqrt(ms + NORM_EPS) * hgn_ref[...] * _silu(gate_ref[0])
    y = _dot(ya, w_ref[:A_W, :]) + _dot(yb, w_ref[A_W:, :])
    gate = _gate_vec(modl_ref, modc_ref, 2, pos0, tm, ctx_len)
    o_ref[0] = x_ref[0] + gate * y


def _rec_out(xcat, mods, ctx_row, oa, feats, ob, p_hgrn, ln_w, ln_b, hg_norm, w_out, ctx_len):
    b, s, d = xcat.shape
    tm = TOK_TILE
    row = lambda a: a.reshape(1, -1)
    full = lambda a: pl.BlockSpec(a.shape, lambda i, j: (0,) * a.ndim)
    consts = [row(ln_w), row(ln_b), row(jnp.tile(hg_norm, B_HEADS)), _block_ones(A_W, A_HD),
              _block_ones(B_W, B_DK), w_out]
    return pl.pallas_call(
        functools.partial(_rec_out_kernel, ctx_len=ctx_len, tm=tm),
        grid=(b, s // tm),
        in_specs=[pl.BlockSpec((1, tm, d), lambda i, j: (i, j, 0))] + _mod_specs(d, ctx_row) + [
            pl.BlockSpec((1, 2, tm, A_W), lambda i, j: (i, 0, j, 0)),
            pl.BlockSpec((1, tm, A_W), lambda i, j: (i, j, SEC_G)),
            pl.BlockSpec((1, tm, A_W), lambda i, j: (i, j, SEC_BONUS)),
            pl.BlockSpec((1, 2, tm, B_W), lambda i, j: (i, 0, j, 0)),
            pl.BlockSpec((1, tm, B_W), lambda i, j: (i, j, 4)),
        ] + [full(a) for a in consts],
        out_specs=pl.BlockSpec((1, tm, d), lambda i, j: (i, j, 0)),
        out_shape=jax.ShapeDtypeStruct((b, s, d), F32),
        compiler_params=_cparams(("parallel", "parallel")),
        name="rec_out",
    )(xcat, mods, mods, oa, feats, feats, ob, p_hgrn, *consts)


def _res_proj_kernel(x_ref, modl_ref, modc_ref, y_ref, w_ref, o_ref, *, ctx_len, tm):
    pos0 = pl.program_id(1) * tm
    gate = _gate_vec(modl_ref, modc_ref, 2, pos0, tm, ctx_len)
    o_ref[0] = x_ref[0] + gate * _dot(y_ref[0], w_ref[...])


def _res_proj(x, mods, ctx_row, y, w, ctx_len, x_skip=0):
    b, s, k = y.shape
    d = x.shape[-1]
    tm = TOK_TILE
    skip = x_skip // tm
    return pl.pallas_call(
        functools.partial(_res_proj_kernel, ctx_len=ctx_len, tm=tm),
        grid=(b, s // tm),
        in_specs=[pl.BlockSpec((1, tm, d), lambda i, j: (i, j + skip, 0))] + _mod_specs(d, ctx_row) + [
            pl.BlockSpec((1, tm, k), lambda i, j: (i, j, 0)),
            pl.BlockSpec((k, d), lambda i, j: (0, 0))],
        out_specs=pl.BlockSpec((1, tm, d), lambda i, j: (i, j, 0)),
        out_shape=jax.ShapeDtypeStruct((b, s, d), F32),
        compiler_params=_cparams(("parallel", "parallel")),
        name="res_proj",
    )(x, mods, mods, y, w)


def _ffn_kernel(x_ref, modl_ref, modc_ref, g_ref, wg_ref, wu_ref, wd_ref, o_ref, h_ref, acc_ref, *, ctx_len, tm,
                n_f):
    f = pl.program_id(2)
    pos0 = pl.program_id(1) * tm

    @pl.when(f == 0)
    def _():
        h_ref[...] = _bf(_modulate(x_ref[0], g_ref[...], modl_ref, modc_ref, 3, pos0, ctx_len))
        acc_ref[...] = jnp.zeros_like(acc_ref)

    h = h_ref[...]
    act = _silu(jnp.dot(h, wg_ref[...], preferred_element_type=F32)) * jnp.dot(h, wu_ref[...],
                                                                               preferred_element_type=F32)
    acc_ref[...] += _dot(act, wd_ref[...])

    @pl.when(f == n_f - 1)
    def _():
        o_ref[0] = x_ref[0] + _gate_vec(modl_ref, modc_ref, 5, pos0, tm, ctx_len) * acc_ref[...]


def _ffn(x, mods, ctx_row, g, wg, wu, wd, ctx_len, tm, tf):
    b, s, d = x.shape
    ff = wg.shape[1]
    n_f = ff // tf
    return pl.pallas_call(
        functools.partial(_ffn_kernel, ctx_len=ctx_len, tm=tm, n_f=n_f),
        grid=(b, s // tm, n_f),
        in_specs=[pl.BlockSpec((1, tm, d), lambda i, j, f: (i, j, 0))] + _mod_specs(d, ctx_row) + [
            pl.BlockSpec((1, d), lambda i, j, f: (0, 0)),
            pl.BlockSpec((d, tf), lambda i, j, f: (0, f)),
            pl.BlockSpec((d, tf), lambda i, j, f: (0, f)),
            pl.BlockSpec((tf, d), lambda i, j, f: (f, 0))],
        out_specs=pl.BlockSpec((1, tm, d), lambda i, j, f: (i, j, 0)),
        out_shape=jax.ShapeDtypeStruct((b, s, d), F32),
        scratch_shapes=[pltpu.VMEM((tm, d), BF16), pltpu.VMEM((tm, d), F32)],
        compiler_params=_cparams(("parallel", "parallel", "arbitrary")),
        name="ffn",
    )(x, mods, mods, g.reshape(1, d), wg, wu, wd)


def _attn_kernel(sink_ref, q_ref, kc_ref, vc_ref, kp_ref, kq_ref, kn_ref, vp_ref, vq_ref, vn_ref, o_ref, *,
                 ctx_len, n_lat):
    i = pl.program_id(1)
    nk = ctx_len + 3 * ATT_BLOCK
    k_all = jnp.concatenate([kc_ref[0], kp_ref[0], kq_ref[0], kn_ref[0]], axis=0)
    v_all = jnp.concatenate([vc_ref[0], vp_ref[0], vq_ref[0], vn_ref[0]], axis=0)
    row = lax.broadcasted_iota(jnp.int32, (ATT_BLOCK, nk), 0)
    col = lax.broadcasted_iota(jnp.int32, (ATT_BLOCK, nk), 1)
    rel = col - (ctx_len + ATT_BLOCK)
    kabs = i * ATT_BLOCK + rel
    band = (jnp.abs(row - rel) <= WINDOW) & (kabs >= 0) & (kabs < n_lat)
    bias = jnp.where((col < ctx_len) | band, 0.0, NEG_INF)
    bias = jnp.concatenate([bias] * GQ, axis=0)
    grp = lax.broadcasted_iota(jnp.int32, (GQ * ATT_BLOCK, 1), 0) // ATT_BLOCK
    scale = HD ** -0.5
    for hk in range(HKV):
        kh = _bf(k_all[:, hk * HD:(hk + 1) * HD])
        vh = _bf(v_all[:, hk * HD:(hk + 1) * HD])
        heads = [hk * GQ + g for g in range(GQ)]
        q = jnp.concatenate([q_ref[0, :, hq * HD:(hq + 1) * HD] for hq in heads], axis=0) * scale
        sk = jnp.zeros((GQ * ATT_BLOCK, 1), F32)
        for g, hq in enumerate(heads):
            sk = jnp.where(grp == g, sink_ref[hq], sk)
        s = _dot_nt(q, kh) + bias
        m = jnp.maximum(jnp.max(s, axis=-1, keepdims=True), sk)
        p = jnp.exp(s - m)
        den = jnp.sum(p, axis=-1, keepdims=True) + jnp.exp(sk - m)
        o = jnp.dot(_bf(p), vh, preferred_element_type=F32) / den
        for g, hq in enumerate(heads):
            o_ref[0, :, hq * HD:(hq + 1) * HD] = o[g * ATT_BLOCK:(g + 1) * ATT_BLOCK]


def _attention(qkv, sink, ctx_len):
    b, s, _ = qkv.shape
    n_lat = s - ctx_len
    nb = n_lat // ATT_BLOCK
    cb = ctx_len // ATT_BLOCK
    kcol = Q_COLS // KV_COLS
    vcol = kcol + 1

    def band(colblk, shift):
        return pl.BlockSpec((1, ATT_BLOCK, KV_COLS),
                            lambda bi, i: (bi, cb + jnp.clip(i + shift, 0, nb - 1), colblk))

    return pl.pallas_call(
        functools.partial(_attn_kernel, ctx_len=ctx_len, n_lat=n_lat),
        grid=(b, nb),
        in_specs=[pl.BlockSpec(memory_space=pltpu.SMEM),
                  pl.BlockSpec((1, ATT_BLOCK, Q_COLS), lambda bi, i: (bi, cb + i, 0)),
                  pl.BlockSpec((1, ctx_len, KV_COLS), lambda bi, i: (bi, 0, kcol)),
                  pl.BlockSpec((1, ctx_len, KV_COLS), lambda bi, i: (bi, 0, vcol)),
                  band(kcol, -1), band(kcol, 0), band(kcol, 1),
                  band(vcol, -1), band(vcol, 0), band(vcol, 1)],
        out_specs=pl.BlockSpec((1, ATT_BLOCK, Q_COLS), lambda bi, i: (bi, i, 0)),
        out_shape=jax.ShapeDtypeStruct((b, n_lat, Q_COLS), F32),
        compiler_params=_cparams(("parallel", "parallel")),
        name="attention",
    )(sink, qkv, qkv, qkv, qkv, qkv, qkv, qkv, qkv, qkv)


def _router_kernel(x_ref, modl_ref, modc_ref, g_ref, w_ref, b_ref, o_ref, e_ref, f_ref, *, tm):
    f = _modulate(x_ref[0], g_ref[...], modl_ref, modc_ref, 3, 0, 0)
    f_ref[0] = f
    logits = _dot_f32(f, w_ref[...]) + b_ref[...]
    lane = lax.broadcasted_iota(jnp.int32, logits.shape, 1).astype(F32)
    logits = jnp.where(lane < N_EXPERTS, logits, NEG_INF)
    ex = jnp.exp(logits - jnp.max(logits, axis=-1, keepdims=True))
    probs = ex / jnp.sum(ex, axis=-1, keepdims=True)
    p1 = jnp.max(probs, axis=-1, keepdims=True)
    i1 = jnp.min(jnp.where(probs == p1, lane, float(LANES)), axis=-1, keepdims=True)
    rest = jnp.where(lane == i1, -1.0, probs)
    p2 = jnp.max(rest, axis=-1, keepdims=True)
    i2 = jnp.min(jnp.where(rest == p2, lane, float(LANES)), axis=-1, keepdims=True)
    tot = p1 + p2
    o_ref[0] = jnp.where(lane == 0.0, p1 / tot, jnp.where(lane == 1.0, p2 / tot, 0.0))
    e_ref[0] = jnp.where(lane == 0.0, i1, jnp.where(lane == 1.0, i2, 0.0)).astype(jnp.int32)


def _router(x, mods, g, w, bias):
    b, s, d = x.shape
    tm = TOK_TILE
    wpad = jnp.zeros((d, LANES), F32).at[:, :N_EXPERTS].set(w)
    bpad = jnp.zeros((1, LANES), F32).at[0, :N_EXPERTS].set(bias)
    lane_spec = pl.BlockSpec((1, tm, LANES), lambda i, j: (i, j, 0))
    return pl.pallas_call(
        functools.partial(_router_kernel, tm=tm),
        grid=(b, s // tm),
        in_specs=[pl.BlockSpec((1, tm, d), lambda i, j: (i, j, 0))] + _mod_specs(d, 0) + [
            pl.BlockSpec((1, d), lambda i, j: (0, 0)),
            pl.BlockSpec((d, LANES), lambda i, j: (0, 0)),
            pl.BlockSpec((1, LANES), lambda i, j: (0, 0))],
        out_specs=[lane_spec, lane_spec, pl.BlockSpec((1, tm, d), lambda i, j: (i, j, 0))],
        out_shape=[jax.ShapeDtypeStruct((b, s, LANES), F32), jax.ShapeDtypeStruct((b, s, LANES), jnp.int32),
                   jax.ShapeDtypeStruct((b, s, d), F32)],
        compiler_params=_cparams(("parallel", "parallel")),
        name="router",
    )(x, mods, mods, g.reshape(1, d), wpad, bpad)


MOE_ROWS = 256


def _row_scatter_kernel(dest_ref, f_ref, xs_in_hbm, xs_hbm, sem, *, tm):
    del xs_in_hbm
    base = pl.program_id(0) * tm

    def issue(r, carry):
        slot = 2 * (base + r)
        pltpu.make_async_copy(f_ref.at[pl.ds(r, 1)], xs_hbm.at[pl.ds(dest_ref[slot], 1)], sem).start()
        pltpu.make_async_copy(f_ref.at[pl.ds(r, 1)], xs_hbm.at[pl.ds(dest_ref[slot + 1], 1)], sem).start()
        return carry

    lax.fori_loop(0, tm, issue, 0, unroll=8)
    for _ in range(2):
        pltpu.make_async_copy(f_ref, xs_hbm.at[pl.ds(0, tm)], sem).wait()


def _row_scatter(f, dest, n_rows):
    n_tok, d = f.shape
    tm = TOK_TILE
    return pl.pallas_call(
        functools.partial(_row_scatter_kernel, tm=tm),
        grid_spec=pltpu.PrefetchScalarGridSpec(
            num_scalar_prefetch=1,
            grid=(n_tok // tm,),
            in_specs=[pl.BlockSpec((tm, d), lambda i, dr: (i, 0)), pl.BlockSpec(memory_space=pl.ANY)],
            out_specs=pl.BlockSpec(memory_space=pl.ANY),
            scratch_shapes=[pltpu.SemaphoreType.DMA(())]),
        out_shape=jax.ShapeDtypeStruct((n_rows, d), f.dtype),
        input_output_aliases={2: 0},
        compiler_params=_cparams(("arbitrary",)),
        name="moe_scatter",
    )(dest, f, jnp.zeros((n_rows, d), f.dtype))


def _expert_kernel(be_ref, x_ref, wg_ref, wu_ref, wd_ref, o_ref):
    h = _bf(x_ref[...])
    act = _silu(jnp.dot(h, wg_ref[0], preferred_element_type=F32)) * jnp.dot(h, wu_ref[0],
                                                                              preferred_element_type=F32)
    o_ref[...] = _dot(act, wd_ref[0])


def _experts(xs, block_e, wg, wu, wd):
    n_rows, d = xs.shape
    ff = wg.shape[2]
    rows = MOE_ROWS
    return pl.pallas_call(
        _expert_kernel,
        grid_spec=pltpu.PrefetchScalarGridSpec(
            num_scalar_prefetch=1,
            grid=(n_rows // rows,),
            in_specs=[pl.BlockSpec((rows, d), lambda i, be: (i, 0)),
                      pl.BlockSpec((1, d, ff), lambda i, be: (be[i], 0, 0)),
                      pl.BlockSpec((1, d, ff), lambda i, be: (be[i], 0, 0)),
                      pl.BlockSpec((1, ff, d), lambda i, be: (be[i], 0, 0))],
            out_specs=pl.BlockSpec((rows, d), lambda i, be: (i, 0))),
        out_shape=jax.ShapeDtypeStruct((n_rows, d), F32),
        compiler_params=_cparams(("arbitrary",)),
        name="moe_experts",
    )(block_e, xs, wg, wu, wd)


def _combine_kernel(dest_ref, x_ref, modl_ref, modc_ref, w_ref, gfin_ref, ys_hbm, o_ref, y1_ref, y2_ref, sem, *,
                    tm, tiles_per_row):
    base = (pl.program_id(0) * tiles_per_row + pl.program_id(1)) * tm

    def issue(r, carry):
        slot = 2 * (base + r)
        pltpu.make_async_copy(ys_hbm.at[pl.ds(dest_ref[slot], 1)], y1_ref.at[pl.ds(r, 1)], sem).start()
        pltpu.make_async_copy(ys_hbm.at[pl.ds(dest_ref[slot + 1], 1)], y2_ref.at[pl.ds(r, 1)], sem).start()
        return carry

    lax.fori_loop(0, tm, issue, 0, unroll=8)
    pltpu.make_async_copy(ys_hbm.at[pl.ds(0, tm)], y1_ref, sem).wait()
    pltpu.make_async_copy(ys_hbm.at[pl.ds(0, tm)], y2_ref, sem).wait()
    w = w_ref[0]
    moe = w[:, 0:1] * y1_ref[...] + w[:, 1:2] * y2_ref[...]
    y = x_ref[0] + _gate_vec(modl_ref, modc_ref, 5, 0, tm, 0) * moe
    o_ref[0] = y * lax.rsqrt(jnp.mean(y * y, axis=-1, keepdims=True) + NORM_EPS) * gfin_ref[...]


def _combine(x, mods, tokw, dest, ys, final_g):
    b, s, d = x.shape
    tm = TOK_TILE
    tiles = s // tm
    return pl.pallas_call(
        functools.partial(_combine_kernel, tm=tm, tiles_per_row=tiles),
        grid_spec=pltpu.PrefetchScalarGridSpec(
            num_scalar_prefetch=1,
            grid=(b, tiles),
            in_specs=[pl.BlockSpec((1, tm, d), lambda i, j, dr: (i, j, 0)),
                      pl.BlockSpec((1, N_MOD, d), lambda i, j, dr: (i, 0, 0)),
                      pl.BlockSpec((1, N_MOD, d), lambda i, j, dr: (0, 0, 0)),
                      pl.BlockSpec((1, tm, LANES), lambda i, j, dr: (i, j, 0)),
                      pl.BlockSpec((1, d), lambda i, j, dr: (0, 0)),
                      pl.BlockSpec(memory_space=pl.ANY)],
            out_specs=pl.BlockSpec((1, tm, d), lambda i, j, dr: (i, j, 0)),
            scratch_shapes=[pltpu.VMEM((tm, d), F32), pltpu.VMEM((tm, d), F32), pltpu.SemaphoreType.DMA(())]),
        out_shape=jax.ShapeDtypeStruct((b, s, d), F32),
        compiler_params=_cparams(("arbitrary", "arbitrary")),
        name="moe_combine",
    )(dest, x, mods, mods, tokw, final_g.reshape(1, d), ys)


def _moe_routing(eidx):
    m = eidx.shape[0]
    oh = (eidx[:, None] == jnp.arange(N_EXPERTS, dtype=jnp.int32)[None, :]).astype(jnp.int32)
    csum = jnp.cumsum(oh, axis=0)
    rank = jnp.sum((csum - oh) * oh, axis=1)
    counts = csum[-1]
    padded = (counts + MOE_ROWS - 1) // MOE_ROWS * MOE_ROWS
    pad_end = jnp.cumsum(padded)
    dest = jnp.sum(oh * (pad_end - padded)[None, :], axis=1) + rank
    n_blocks = m // MOE_ROWS + N_EXPERTS
    starts = jnp.arange(n_blocks, dtype=jnp.int32) * MOE_ROWS
    block_e = jnp.minimum(jnp.sum((starts[:, None] >= pad_end[None, :]).astype(jnp.int32), axis=1), N_EXPERTS - 1)
    return dest.astype(jnp.int32), block_e, n_blocks * MOE_ROWS


def _rope_tables(n_lat, ctx_len):
    rows = n_lat // GRID_W
    row = jnp.repeat(jnp.arange(rows, dtype=F32), GRID_W)
    col = jnp.tile(jnp.arange(GRID_W, dtype=F32), rows)
    inv = ROPE_BASE ** (-jnp.arange(0, AX_DIM, 2, dtype=F32) / AX_DIM)
    ar, ac = row[:, None] * inv, col[:, None] * inv
    cos = jnp.concatenate([jnp.cos(ar), jnp.cos(ar), jnp.cos(ac), jnp.cos(ac)], axis=-1)
    sin = jnp.concatenate([-jnp.sin(ar), jnp.sin(ar), -jnp.sin(ac), jnp.sin(ac)], axis=-1)
    n_heads = ROPE_COLS // HD
    cos = jnp.concatenate([jnp.ones((ctx_len, HD), F32), cos], axis=0)
    sin = jnp.concatenate([jnp.zeros((ctx_len, HD), F32), sin], axis=0)
    half = AX_DIM // 2
    j = jnp.arange(HD)
    partner = jnp.where((j % AX_DIM) < half, j + half, j - half)
    perm = (jnp.arange(n_heads)[:, None] * HD + partner[None, :]).reshape(-1)
    return jnp.tile(cos, (1, n_heads)), jnp.tile(sin, (1, n_heads)), perm


def kernel(x, c, ctx, c_ctx, mod_w, mod_b, norm_mix, norm_ffn, norm_final, rec_w_in, rec_w_out, rwkv_mu, rwkv_w0, rwkv_w_up, rwkv_a0, rwkv_a_up, rwkv_g_up, rwkv_k_k, rwkv_k_a, rwkv_r_k, rwkv_ln_w, rwkv_ln_b, hgrn_lb, hgrn_norm, ffn_w_gate, ffn_w_up, ffn_w_down, att_w_in, att_w_out, att_sink, moe_router, moe_router_b, moe_w_gate, moe_w_up, moe_w_down):
    bsz, n_lat, d = x.shape
    ctx_len = ctx.shape[1]
    xcat = jnp.concatenate([ctx, x], axis=1)

    n_rows = -(-(bsz + 1) // 8) * 8
    cvec = jnp.zeros((n_rows, d), F32).at[:bsz].set(c).at[bsz].set(c_ctx)
    mods = [_adaln(cvec, mod_w[l], mod_b[l]).reshape(n_rows, N_MOD, d) for l in range(2)]

    p_rwkv, p_hgrn = _proj(xcat, mods[0], bsz, norm_mix[0], _bf(rec_w_in[0]), (RWKV_COLS, HGRN_COLS), ctx_len)
    feats = _rwkv_features(p_rwkv, rwkv_mu[0], rwkv_w0[0], rwkv_w_up[0], rwkv_a0[0], rwkv_a_up[0], rwkv_g_up[0],
                           rwkv_k_k[0], rwkv_k_a[0], rwkv_r_k[0].reshape(-1), ctx_len)
    oa = _rwkv_scan(feats, ctx_len)
    lb = jnp.cumsum(jax.nn.softmax(hgrn_lb.astype(F32), axis=1), axis=1)[:, 0].reshape(2, 1, B_W)
    ob = _hgrn_scan(p_hgrn, lb, ctx_len)
    xcat = _rec_out(xcat, mods[0], bsz, oa, feats, ob, p_hgrn, rwkv_ln_w[0], rwkv_ln_b[0], hgrn_norm[0],
                    _bf(rec_w_out[0]), ctx_len)
    xcat = _ffn(xcat, mods[0], bsz, norm_ffn[0], _bf(ffn_w_gate[0]), _bf(ffn_w_up[0]), _bf(ffn_w_down[0]), ctx_len,
                tm=384, tf=2816)

    cos, sin, perm = _rope_tables(n_lat, ctx_len)
    w_att = att_w_in[0]
    (qkv,) = _proj(xcat, mods[1], bsz, norm_mix[1], _bf(w_att), (ATT_COLS,), ctx_len,
                   rope=(_bf(w_att[:, perm]), cos, sin))
    att = _attention(qkv, att_sink[0], ctx_len)
    x_lat = _res_proj(xcat, mods[1], 0, att, _bf(att_w_out[0]), 0, x_skip=ctx_len)
    tokw, eidx, f_lat = _router(x_lat, mods[1], norm_ffn[1], moe_router[0], moe_router_b[0])
    dest, block_e, n_rows = _moe_routing(eidx[..., :2].reshape(-1))
    xs = _row_scatter(f_lat.reshape(bsz * n_lat, d), dest, n_rows)
    ys = _experts(xs, block_e, _bf(moe_w_gate[0]), _bf(moe_w_up[0]), _bf(moe_w_down[0]))
    return _combine(x_lat, mods[1], tokw, dest, ys, norm_final)
```

```python
import functools

import jax
import jax.numpy as jnp
from jax import lax
from jax.experimental import pallas as pl
from jax.experimental.pallas import tpu as pltpu

F32 = jnp.float32
BF16 = jnp.bfloat16
HIGHEST = lax.Precision.HIGHEST

N_MOD = 6
NORM_EPS = 1e-6
NEG_INF = -1e30

A_HEADS = 8
A_HD = 64
A_W = A_HEADS * A_HD
DECAY_LORA = 64
AAA_LORA = 64
GATE_LORA = 128
RWKV_COLS = 3 * A_W + DECAY_LORA + AAA_LORA + GATE_LORA
GN_EPS = 64e-5

B_HEADS = 4
B_DK = 128
B_W = B_HEADS * B_DK
HGRN_COLS = 5 * B_W

HQ = 16
HKV = 4
GQ = HQ // HKV
HD = 64
WINDOW = 128
ATT_BLOCK = 128
AX_DIM = HD // 2
ROPE_BASE = 10000.0
GRID_W = 64
Q_COLS = HQ * HD
KV_COLS = HKV * HD
ROPE_COLS = Q_COLS + KV_COLS
ATT_COLS = Q_COLS + 2 * KV_COLS

N_EXPERTS = 8
LANES = 128
CHUNK = 64
SCAN_BLOCK = 256
MXU_WIDTH = 256
RWKV_GROUP = MXU_WIDTH // A_HD
TOK_TILE = 256
VMEM_LIMIT = 56 * 1024 * 1024

SEC_R, SEC_K, SEC_V, SEC_KK, SEC_B, SEC_LW0, SEC_LW1, SEC_G, SEC_BONUS = range(9)
N_SEC = 9


def _cparams(sem):
    return pltpu.CompilerParams(dimension_semantics=sem, vmem_limit_bytes=VMEM_LIMIT)


def _bf(x):
    return x.astype(BF16)


def _dot(a, b):
    return jnp.dot(_bf(a), _bf(b), preferred_element_type=F32)


def _dot_nt(a, b):
    return lax.dot_general(_bf(a), _bf(b), (((1,), (1,)), ((), ())), preferred_element_type=F32)


def _dot_tn(a, b):
    return lax.dot_general(_bf(a), _bf(b), (((0,), (0,)), ((), ())), preferred_element_type=F32)


def _dot_f32(a, b):
    return jnp.dot(a, b, preferred_element_type=F32, precision=HIGHEST)


def _dot_tn_f32(a, b):
    return lax.dot_general(a, b, (((0,), (0,)), ((), ())), preferred_element_type=F32, precision=HIGHEST)


def _sigmoid(x):
    return 1.0 / (1.0 + jnp.exp(-x))


def _silu(x):
    return x * _sigmoid(x)


def _segsum(x, ones_bd):
    hi = _bf(x)
    r1 = x - hi.astype(F32)
    mid = _bf(r1)
    lo = _bf(r1 - mid.astype(F32))
    acc = jnp.dot(hi, ones_bd, preferred_element_type=F32)
    acc += jnp.dot(mid, ones_bd, preferred_element_type=F32)
    acc += jnp.dot(lo, ones_bd, preferred_element_type=F32)
    return acc


def _block_ones(width, seg):
    i = jnp.arange(width) // seg
    return (i[:, None] == i[None, :]).astype(BF16)


def _modulate(x, g, modl_ref, modc_ref, row, pos0, ctx_len):
    y = x * lax.rsqrt(jnp.mean(x * x, axis=-1, keepdims=True) + NORM_EPS) * g
    pos = pos0 + lax.broadcasted_iota(jnp.int32, (x.shape[0], 1), 0)
    is_ctx = pos < ctx_len
    shift = jnp.where(is_ctx, modc_ref[0, row:row + 1, :], modl_ref[0, row:row + 1, :])
    scale = jnp.where(is_ctx, modc_ref[0, row + 1:row + 2, :], modl_ref[0, row + 1:row + 2, :])
    return y * (1.0 + scale) + shift


def _gate_vec(modl_ref, modc_ref, row, pos0, n, ctx_len):
    pos = pos0 + lax.broadcasted_iota(jnp.int32, (n, 1), 0)
    return jnp.where(pos < ctx_len, modc_ref[0, row:row + 1, :], modl_ref[0, row:row + 1, :])


def _mod_specs(d, ctx_row, batch_axis=0):
    def lat_map(*idx):
        return (idx[batch_axis], 0, 0)

    def ctx_map(*idx):
        return (ctx_row, 0, 0)

    return [pl.BlockSpec((1, N_MOD, d), lat_map), pl.BlockSpec((1, N_MOD, d), ctx_map)]


def _adaln_kernel(c_ref, w_ref, b_ref, o_ref):
    o_ref[...] = _dot(_silu(c_ref[...]), w_ref[...]) + b_ref[...]


def _adaln(cvec, w, b):
    r, d = cvec.shape
    n = w.shape[1]
    tn = 1024
    return pl.pallas_call(
        _adaln_kernel,
        grid=(n // tn,),
        in_specs=[pl.BlockSpec((r, d), lambda j: (0, 0)),
                  pl.BlockSpec((d, tn), lambda j: (0, j)),
                  pl.BlockSpec((1, tn), lambda j: (0, j))],
        out_specs=pl.BlockSpec((r, tn), lambda j: (0, j)),
        out_shape=jax.ShapeDtypeStruct((r, n), F32),
        compiler_params=_cparams(("parallel",)),
        name="adaln",
    )(cvec, w, b.reshape(1, n))


def _proj_kernel(x_ref, modl_ref, modc_ref, g_ref, w_ref, *rest, ctx_len, tm, splits, rope_cols):
    if rope_cols:
        wrot_ref, cos_ref, sin_ref = rest[:3]
        outs = rest[3:]
    else:
        outs = rest
    pos0 = pl.program_id(1) * tm
    h = _bf(_modulate(x_ref[0], g_ref[...], modl_ref, modc_ref, 0, pos0, ctx_len))
    y = jnp.dot(h, w_ref[...], preferred_element_type=F32)
    if rope_cols:
        yr = jnp.dot(h, wrot_ref[...], preferred_element_type=F32)
        roped = y[:, :rope_cols] * cos_ref[...] + yr * sin_ref[...]
        outs[0][0, :, :rope_cols] = roped
        outs[0][0, :, rope_cols:] = y[:, rope_cols:]
    else:
        lo = 0
        for o_ref, width in zip(outs, splits):
            o_ref[0] = y[:, lo:lo + width]
            lo += width


def _proj(xcat, mods, ctx_row, g, w, splits, ctx_len, rope=None):
    b, s, d = xcat.shape
    n = w.shape[1]
    tm = TOK_TILE
    in_specs = [pl.BlockSpec((1, tm, d), lambda i, j: (i, j, 0))] + _mod_specs(d, ctx_row) + [
        pl.BlockSpec((1, d), lambda i, j: (0, 0)),
        pl.BlockSpec((d, n), lambda i, j: (0, 0))]
    args = [xcat, mods, mods, g.reshape(1, d), w]
    rope_cols = 0
    if rope is not None:
        wrot, cos, sin = rope
        rope_cols = wrot.shape[1]
        in_specs += [pl.BlockSpec((d, rope_cols), lambda i, j: (0, 0)),
                     pl.BlockSpec((tm, rope_cols), lambda i, j: (j, 0)),
                     pl.BlockSpec((tm, rope_cols), lambda i, j: (j, 0))]
        args += [wrot, cos, sin]
    out_shape = [jax.ShapeDtypeStruct((b, s, width), F32) for width in splits]
    out_specs = [pl.BlockSpec((1, tm, width), lambda i, j: (i, j, 0)) for width in splits]
    return pl.pallas_call(
        functools.partial(_proj_kernel, ctx_len=ctx_len, tm=tm, splits=tuple(splits), rope_cols=rope_cols),
        grid=(b, s // tm),
        in_specs=in_specs,
        out_specs=out_specs,
        out_shape=out_shape,
        compiler_params=_cparams(("parallel", "parallel")),
        name="proj_rope" if rope_cols else "proj",
    )(*args)


def _softplus(x):
    return jnp.maximum(x, 0.0) + jnp.log(1.0 + jnp.exp(-jnp.abs(x)))


def _rwkv_feat_kernel(p_ref, pp_ref, pn_ref, mu_ref, w0_ref, wup_ref, a0_ref, aup_ref, gup_ref,
                      kk_ref, ka_ref, rk_ref, ones_ref, o_ref, *, ctx_len, seq_len, tm):
    pos0 = pl.program_id(1) * tm
    p = p_ref[0]
    rows = lax.broadcasted_iota(jnp.int32, (tm, 1), 0)
    pos = rows + pos0
    prev = jnp.where(rows == 0, pp_ref[0, 7:8, :], pltpu.roll(p, 1, 0))
    prev = jnp.where((pos == 0) | (pos == ctx_len), 0.0, prev)
    nxt = jnp.where(rows == tm - 1, pn_ref[0, 0:1, :], pltpu.roll(p, tm - 1, 0))
    nxt = jnp.where((pos == ctx_len - 1) | (pos == seq_len - 1), 0.0, nxt)
    p = p + mu_ref[...] * (0.5 * (prev + nxt) - p)

    r = p[:, 0:A_W]
    k = p[:, A_W:2 * A_W]
    v = p[:, 2 * A_W:3 * A_W]
    lo = 3 * A_W
    wd = p[:, lo:lo + DECAY_LORA]
    ad = p[:, lo + DECAY_LORA:lo + DECAY_LORA + AAA_LORA]
    gd = p[:, lo + DECAY_LORA + AAA_LORA:lo + DECAY_LORA + AAA_LORA + GATE_LORA]

    tw = jnp.tanh(wd)
    for d in range(2):
        w_log = -_softplus(-(w0_ref[d:d + 1, :] + _dot(tw, wup_ref[d]))) - 0.5
        o_ref[0, :, (SEC_LW0 + d) * A_W:(SEC_LW0 + d + 1) * A_W] = -jnp.exp(w_log)
    a = _sigmoid(a0_ref[...] + _dot(ad, aup_ref[...]))
    g = _dot(_sigmoid(gd), gup_ref[...])
    ones_bd = ones_ref[...]
    kk = k * kk_ref[...]
    kk = kk / jnp.maximum(jnp.sqrt(_segsum(kk * kk, ones_bd)), 1e-12)
    k = k * (1.0 + (a - 1.0) * ka_ref[...])
    bonus = _segsum(r * k * rk_ref[...], ones_bd) * v
    for sec, val in ((SEC_R, r), (SEC_K, k), (SEC_V, v), (SEC_KK, kk), (SEC_B, kk * a), (SEC_G, g),
                     (SEC_BONUS, bonus)):
        o_ref[0, :, sec * A_W:(sec + 1) * A_W] = val


def _rwkv_features(p, mu, w0, w_up, a0, a_up, g_up, k_k, k_a, r_k, ctx_len):
    b, s, c = p.shape
    tm = TOK_TILE
    nb8 = s // 8
    row = lambda a: a.reshape(1, -1)
    full = lambda a: pl.BlockSpec(a.shape, lambda i, j: (0,) * a.ndim)
    args = [row(mu), w0, w_up, row(a0), a_up, g_up, row(k_k), row(k_a), row(r_k), _block_ones(A_W, A_HD)]
    return pl.pallas_call(
        functools.partial(_rwkv_feat_kernel, ctx_len=ctx_len, seq_len=s, tm=tm),
        grid=(b, s // tm),
        in_specs=[pl.BlockSpec((1, tm, c), lambda i, j: (i, j, 0)),
                  pl.BlockSpec((1, 8, c), lambda i, j: (i, jnp.maximum(j * (tm // 8) - 1, 0), 0)),
                  pl.BlockSpec((1, 8, c), lambda i, j: (i, jnp.minimum((j + 1) * (tm // 8), nb8 - 1), 0)),
                  ] + [full(a) for a in args],
        out_specs=pl.BlockSpec((1, tm, N_SEC * A_W), lambda i, j: (i, j, 0)),
        out_shape=jax.ShapeDtypeStruct((b, s, N_SEC * A_W), F32),
        compiler_params=_cparams(("parallel", "parallel")),
        name="rwkv_features",
    )(p, p, p, *args)


def _chunk_order(d, j, n_ctx_chunks, n_chunks):
    back = jnp.where(j < n_ctx_chunks, n_ctx_chunks - 1 - j, n_chunks - 1 + n_ctx_chunks - j)
    return jnp.where(d == 0, j, back)


def _order_masks(rev):
    row = lax.broadcasted_iota(jnp.int32, (CHUNK, CHUNK), 0)
    col = lax.broadcasted_iota(jnp.int32, (CHUNK, CHUNK), 1)
    diff = jnp.where(rev, row - col, col - row)
    return diff < 0, diff <= 0, row == col


def _dot_split(a, x):
    hi = _bf(x)
    r1 = x - hi.astype(F32)
    mid = _bf(r1)
    lo = _bf(r1 - mid.astype(F32))
    acc = jnp.dot(a, hi, preferred_element_type=F32)
    acc += jnp.dot(a, mid, preferred_element_type=F32)
    acc += jnp.dot(a, lo, preferred_element_type=F32)
    return acc


def _scan_step_coords(t, n_blocks, n_steps):
    return jnp.minimum(t, n_steps - 1), jnp.maximum(t - 1, 0)


def _rwkv_chunk_kernel(r_ref, k_ref, v_ref, kk_ref, b_ref, lw_ref, o_ref, s_ref, tr_ref, uu_ref, ol_ref, ab_ref,
                       m2_ref, cc_ref, gm_ref, *, n_blocks, n_steps):
    step = pl.program_id(0)
    t_in, t_out = _scan_step_coords(step, n_blocks, n_steps)
    rev = (t_in // n_blocks) % 2 == 1
    rev_out = (t_out // n_blocks) % 2 == 1
    first_out = t_out % n_blocks == 0
    w_slot = step % 2
    r_slot = 1 - w_slot

    @pl.when(step == 0)
    def _():
        s_ref[...] = jnp.zeros_like(s_ref)
        for ref in (tr_ref, uu_ref, ol_ref, ab_ref, m2_ref, cc_ref, gm_ref):
            ref[1] = jnp.zeros(ref.shape[1:], ref.dtype)

    _, incl, _ = _order_masks(rev)
    incl_bf = jnp.where(incl, 1.0, 0.0).astype(BF16)
    gw = RWKV_GROUP * A_HD
    groups = range(A_HEADS // RWKV_GROUP)
    gsl = [slice(g * gw, (g + 1) * gw) for g in groups]
    row = lax.broadcasted_iota(jnp.int32, (CHUNK, gw), 0)
    col = lax.broadcasted_iota(jnp.int32, (CHUNK, gw), 1) % A_HD
    diff = jnp.where(rev, row - col, col - row)
    strict_c = diff < 0
    incl_c = diff <= 0
    eye_c = jnp.where(diff == 0, 1.0, 0.0)
    bd_mask = (lax.broadcasted_iota(jnp.int32, (gw, gw), 0) // A_HD
               == lax.broadcasted_iota(jnp.int32, (gw, gw), 1) // A_HD)

    def bd(x):
        return jnp.where(bd_mask, jnp.concatenate([x] * RWKV_GROUP, axis=0), 0.0)

    def stack(x):
        return jnp.concatenate([x[:, h * A_HD:(h + 1) * A_HD] for h in range(RWKV_GROUP)], axis=0)

    n = SCAN_BLOCK // CHUNK
    rows = [pl.ds(pl.multiple_of(jnp.where(rev, (n - 1 - i) * CHUNK, i * CHUNK), CHUNK), CHUNK) for i in range(n)]
    rows_out = [pl.ds(pl.multiple_of(jnp.where(rev_out, (n - 1 - i) * CHUNK, i * CHUNK), CHUNK), CHUNK)
                for i in range(n)]
    units = [(i, g) for i in range(n) for g in groups]

    s = [jnp.where(first_out, 0.0, s_ref[g]) for g in groups]
    zr = {}
    pieces = []

    def advance(i):
        for g in groups:
            ui = i * len(groups) + g
            zr[i, g] = _dot_nt(tr_ref[r_slot, ui], bd(s[g]))
            s[g] = s[g] * gm_ref[r_slot, ui] - jnp.dot(_bf(s[g]), m2_ref[r_slot, ui],
                                                       preferred_element_type=F32) + cc_ref[r_slot, ui]

    def emit_out(i):
        for g in groups:
            ui = i * len(groups) + g
            z = zr[i, g][:CHUNK] + uu_ref[r_slot, ui]
            o_ref[0, 0, rows_out[i], gsl[g]] = (zr[i, g][CHUNK:] + ol_ref[r_slot, ui]
                                                - _dot(ab_ref[r_slot, ui], bd(z)))

    for i in range(n):
        pieces += [functools.partial(advance, i), functools.partial(emit_out, i)]

    def state_piece():
        if pieces:
            pieces.pop(0)()

    kap, rt, kbar, bbar, kgam, bgam, gam, v = ({} for _ in range(8))
    for i in range(n):
        lw = lw_ref[0, rows[i], :]
        g_incl = _dot_split(incl_bf, lw)
        g_tot = jnp.sum(lw, axis=0, keepdims=True)
        e_ng = jnp.exp(-g_incl)
        e_tail = jnp.exp(g_tot - g_incl)
        k = k_ref[0, rows[i], :]
        b = b_ref[0, rows[i], :]
        kap_i = kk_ref[0, rows[i], :] * jnp.exp(g_incl - lw)
        rt_i = r_ref[0, rows[i], :] * jnp.exp(g_incl)
        v_i = v_ref[0, rows[i], :]
        for g in groups:
            kap[i, g], rt[i, g], v[i, g] = kap_i[:, gsl[g]], rt_i[:, gsl[g]], v_i[:, gsl[g]]
            kbar[i, g], bbar[i, g] = (k * e_ng)[:, gsl[g]], (b * e_ng)[:, gsl[g]]
            kgam[i, g], bgam[i, g] = (k * e_tail)[:, gsl[g]], (b * e_tail)[:, gsl[g]]
            gam[i, g] = jnp.exp(g_tot)[:, gsl[g]]

    x = {u: jnp.concatenate([kap[u], rt[u]], axis=0) for u in units}
    yb = {u: _dot_nt(x[u], bd(bbar[u])) for u in units}
    state_piece()
    a = {u: jnp.where(strict_c, -yb[u][:CHUNK], 0.0) for u in units}
    xs = {u: eye_c + a[u] for u in units}
    pw = {u: _dot(a[u], bd(a[u])) for u in units}
    state_piece()
    for _ in range(4):
        st = {u: _dot(jnp.concatenate([pw[u], xs[u]], axis=0), bd(pw[u])) for u in units}
        pw = {u: st[u][:CHUNK] for u in units}
        xs = {u: xs[u] + st[u][CHUNK:] for u in units}
        state_piece()
    t_inv = {u: xs[u] + _dot(xs[u], bd(pw[u])) for u in units}
    state_piece()
    yk = {u: _dot_nt(x[u], bd(kbar[u])) for u in units}
    ykm = {u: jnp.concatenate([jnp.where(strict_c, yk[u][:CHUNK], 0.0), jnp.where(incl_c, yk[u][CHUNK:], 0.0)],
                              axis=0) for u in units}
    wo = {u: _dot(ykm[u], bd(v[u])) for u in units}
    w1 = {u: wo[u][:CHUNK] for u in units}
    o_loc = {u: wo[u][CHUNK:] for u in units}
    state_piece()
    tk = {u: _dot(t_inv[u], bd(kap[u])) for u in units}
    uu = {u: _dot(t_inv[u], bd(w1[u])) for u in units}
    while pieces:
        state_piece()
    for g in groups:
        s_ref[g] = s[g]
    a_rb = {u: jnp.where(incl_c, yb[u][CHUNK:], 0.0) for u in units}
    tu = {u: _dot_tn(jnp.concatenate([stack(tk[u]), stack(uu[u])], axis=1), bd(bgam[u])) for u in units}
    bd_m2 = {u: _bf(bd(tu[u][:A_HD])) for u in units}
    cc = {u: _dot_tn(stack(v[u]), bd(kgam[u])) - tu[u][A_HD:] for u in units}
    for u in units:
        ui = u[0] * len(groups) + u[1]
        tr_ref[w_slot, ui] = jnp.concatenate([tk[u], rt[u]], axis=0)
        uu_ref[w_slot, ui] = uu[u]
        ol_ref[w_slot, ui] = o_loc[u]
        ab_ref[w_slot, ui] = a_rb[u]
        m2_ref[w_slot, ui] = bd_m2[u]
        cc_ref[w_slot, ui] = cc[u]
        gm_ref[w_slot, ui] = gam[u]


def _rwkv_scan(feats, ctx_len):
    b, s, _ = feats.shape
    nc, ncc = s // SCAN_BLOCK, ctx_len // SCAN_BLOCK

    n_steps = b * 2 * nc

    def coords(flat):
        d = (flat // nc) % 2
        return flat // (2 * nc), d, _chunk_order(d, flat % nc, ncc, nc)

    def in_map(sec_idx, per_dir):
        def index_map(t):
            bi, d, blk = coords(_scan_step_coords(t, nc, n_steps)[0])
            return bi, blk, sec_idx + (d if per_dir else 0)
        return index_map

    def out_map(t):
        bi, d, blk = coords(_scan_step_coords(t, nc, n_steps)[1])
        return bi, d, blk, 0

    def sec(idx, per_dir=False):
        return pl.BlockSpec((1, SCAN_BLOCK, A_W), in_map(idx, per_dir))

    n_groups = A_HEADS // RWKV_GROUP
    n_units = SCAN_BLOCK // CHUNK * n_groups
    gw = RWKV_GROUP * A_HD
    return pl.pallas_call(
        functools.partial(_rwkv_chunk_kernel, n_blocks=nc, n_steps=n_steps),
        grid=(n_steps + 1,),
        in_specs=[sec(SEC_R), sec(SEC_K), sec(SEC_V), sec(SEC_KK), sec(SEC_B), sec(SEC_LW0, per_dir=True)],
        out_specs=pl.BlockSpec((1, 1, SCAN_BLOCK, A_W), out_map),
        out_shape=jax.ShapeDtypeStruct((b, 2, s, A_W), F32),
        scratch_shapes=[pltpu.VMEM((n_groups, A_HD, gw), F32),
                        pltpu.VMEM((2, n_units, 2 * CHUNK, gw), F32),
                        pltpu.VMEM((2, n_units, CHUNK, gw), F32),
                        pltpu.VMEM((2, n_units, CHUNK, gw), F32),
                        pltpu.VMEM((2, n_units, CHUNK, gw), F32),
                        pltpu.VMEM((2, n_units, gw, gw), BF16),
                        pltpu.VMEM((2, n_units, CHUNK, gw), F32),
                        pltpu.VMEM((2, n_units, 1, gw), F32)],
        compiler_params=_cparams(("arbitrary",)),
        name="rwkv_scan",
    )(feats, feats, feats, feats, feats, feats)


def _hgrn_chunk_kernel(q_ref, i_ref, f_ref, lb_ref, o_ref, s_ref):
    rev = pl.program_id(1) == 1

    @pl.when(pl.program_id(2) == 0)
    def _():
        s_ref[...] = jnp.zeros_like(s_ref)

    _, incl, _ = _order_masks(rev)
    incl_bf = jnp.where(incl, 1.0, 0.0).astype(BF16)
    lb = lb_ref[0]
    heads = range(B_HEADS)
    sls = [slice(h * B_DK, (h + 1) * B_DK) for h in heads]

    n = SCAN_BLOCK // CHUNK
    rows = [pl.ds(pl.multiple_of(jnp.where(rev, (n - 1 - i) * CHUNK, i * CHUNK), CHUNK), CHUNK) for i in range(n)]
    units = [(i, h) for i in range(n) for h in heads]
    q_in, k_in, q_st, k_tail, gam, v = ({} for _ in range(6))
    for i in range(n):
        f = lb + (1.0 - lb) * _sigmoid(f_ref[0, rows[i], :])
        logf = jnp.log(f)
        kf = 1.0 - f
        g_incl = _dot_split(incl_bf, logf)
        g_tot = jnp.sum(logf, axis=0, keepdims=True)
        g_mid = g_incl[CHUNK // 2 - 1:CHUNK // 2, :]
        q = _silu(q_ref[0, rows[i], :])
        v_i = i_ref[0, rows[i], :]
        for h in heads:
            q_in[i, h] = (q * jnp.exp(g_incl - g_mid))[:, sls[h]]
            k_in[i, h] = (kf * jnp.exp(g_mid - g_incl))[:, sls[h]]
            q_st[i, h] = (q * jnp.exp(g_incl))[:, sls[h]]
            k_tail[i, h] = (kf * jnp.exp(g_tot - g_incl))[:, sls[h]]
            gam[i, h] = jnp.exp(g_tot)[:, sls[h]]
            v[i, h] = v_i[:, sls[h]]
    att = {u: jnp.where(incl, _dot_nt(q_in[u], k_in[u]), 0.0) for u in units}
    o_loc = {u: _dot(att[u], v[u]) for u in units}
    kv = {u: _dot_tn(v[u], k_tail[u]) for u in units}

    s = [s_ref[h] for h in heads]
    for i in range(n):
        for h in heads:
            o_ref[0, 0, rows[i], sls[h]] = o_loc[i, h] + _dot_nt(q_st[i, h], s[h])
            s[h] = s[h] * gam[i, h] + kv[i, h]
    for h in heads:
        s_ref[h] = s[h]


def _hgrn_scan(p, lb, ctx_len):
    b, s, _ = p.shape
    nc, ncc = s // SCAN_BLOCK, ctx_len // SCAN_BLOCK

    def sec(idx):
        return pl.BlockSpec((1, SCAN_BLOCK, B_W), lambda i, d, j: (i, _chunk_order(d, j, ncc, nc), idx))

    return pl.pallas_call(
        _hgrn_chunk_kernel,
        grid=(b, 2, nc),
        in_specs=[sec(0), sec(1),
                  pl.BlockSpec((1, SCAN_BLOCK, B_W), lambda i, d, j: (i, _chunk_order(d, j, ncc, nc), 2 + d)),
                  pl.BlockSpec((1, 1, B_W), lambda i, d, j: (d, 0, 0))],
        out_specs=pl.BlockSpec((1, 1, SCAN_BLOCK, B_W), lambda i, d, j: (i, d, _chunk_order(d, j, ncc, nc), 0)),
        out_shape=jax.ShapeDtypeStruct((b, 2, s, B_W), F32),
        scratch_shapes=[pltpu.VMEM((B_HEADS, B_DK, B_DK), F32)],
        compiler_params=_cparams(("parallel", "parallel", "arbitrary")),
        name="hgrn_scan",
    )(p, p, p, lb)


def _rec_out_kernel(x_ref, modl_ref, modc_ref, oa_ref, g_ref, bonus_ref, ob_ref, gate_ref, lnw_ref, lnb_ref,
                    hgn_ref, ones_a_ref, ones_b_ref, w_ref, o_ref, *, ctx_len, tm):
    pos0 = pl.program_id(1) * tm
    oa = oa_ref[0, 0] + oa_ref[0, 1]
    ones_a = ones_a_ref[...]
    mean = _segsum(oa, ones_a) * (1.0 / A_HD)
    cen = oa - mean
    var = _segsum(cen * cen, ones_a) * (1.0 / A_HD)
    ya = (cen * lax.rsqrt(var + GN_EPS) * lnw_ref[...] + lnb_ref[...] + bonus_ref[0]) * g_ref[0]
    ob = ob_ref[0, 0] + ob_ref[0, 1]
    ms = _segsum(ob * ob, ones_b_ref[...]) * (1.0 / B_DK)
    yb = ob * lax.rsqrt(ms + NORM_EPS) * hgn_ref[...] * _silu(gate_ref[0])
    y = _dot(ya, w_ref[:A_W, :]) + _dot(yb, w_ref[A_W:, :])
    gate = _gate_vec(modl_ref, modc_ref, 2, pos0, tm, ctx_len)
    o_ref[0] = x_ref[0] + gate * y


def _rec_out(xcat, mods, ctx_row, oa, feats, ob, p_hgrn, ln_w, ln_b, hg_norm, w_out, ctx_len):
    b, s, d = xcat.shape
    tm = TOK_TILE
    row = lambda a: a.reshape(1, -1)
    full = lambda a: pl.BlockSpec(a.shape, lambda i, j: (0,) * a.ndim)
    consts = [row(ln_w), row(ln_b), row(jnp.tile(hg_norm, B_HEADS)), _block_ones(A_W, A_HD),
              _block_ones(B_W, B_DK), w_out]
    return pl.pallas_call(
        functools.partial(_rec_out_kernel, ctx_len=ctx_len, tm=tm),
        grid=(b, s // tm),
        in_specs=[pl.BlockSpec((1, tm, d), lambda i, j: (i, j, 0))] + _mod_specs(d, ctx_row) + [
            pl.BlockSpec((1, 2, tm, A_W), lambda i, j: (i, 0, j, 0)),
            pl.BlockSpec((1, tm, A_W), lambda i, j: (i, j, SEC_G)),
            pl.BlockSpec((1, tm, A_W), lambda i, j: (i, j, SEC_BONUS)),
            pl.BlockSpec((1, 2, tm, B_W), lambda i, j: (i, 0, j, 0)),
            pl.BlockSpec((1, tm, B_W), lambda i, j: (i, j, 4)),
        ] + [full(a) for a in consts],
        out_specs=pl.BlockSpec((1, tm, d), lambda i, j: (i, j, 0)),
        out_shape=jax.ShapeDtypeStruct((b, s, d), F32),
        compiler_params=_cparams(("parallel", "parallel")),
        name="rec_out",
    )(xcat, mods, mods, oa, feats, feats, ob, p_hgrn, *consts)


def _res_proj_kernel(x_ref, modl_ref, modc_ref, y_ref, w_ref, o_ref, *, ctx_len, tm):
    pos0 = pl.program_id(1) * tm
    gate = _gate_vec(modl_ref, modc_ref, 2, pos0, tm, ctx_len)
    o_ref[0] = x_ref[0] + gate * _dot(y_ref[0], w_ref[...])


def _res_proj(x, mods, ctx_row, y, w, ctx_len, x_skip=0):
    b, s, k = y.shape
    d = x.shape[-1]
    tm = TOK_TILE
    skip = x_skip // tm
    return pl.pallas_call(
        functools.partial(_res_proj_kernel, ctx_len=ctx_len, tm=tm),
        grid=(b, s // tm),
        in_specs=[pl.BlockSpec((1, tm, d), lambda i, j: (i, j + skip, 0))] + _mod_specs(d, ctx_row) + [
            pl.BlockSpec((1, tm, k), lambda i, j: (i, j, 0)),
            pl.BlockSpec((k, d), lambda i, j: (0, 0))],
        out_specs=pl.BlockSpec((1, tm, d), lambda i, j: (i, j, 0)),
        out_shape=jax.ShapeDtypeStruct((b, s, d), F32),
        compiler_params=_cparams(("parallel", "parallel")),
        name="res_proj",
    )(x, mods, mods, y, w)


def _ffn_kernel(x_ref, modl_ref, modc_ref, g_ref, wg_ref, wu_ref, wd_ref, o_ref, h_ref, acc_ref, *, ctx_len, tm,
                n_f):
    f = pl.program_id(2)
    pos0 = pl.program_id(1) * tm

    @pl.when(f == 0)
    def _():
        h_ref[...] = _bf(_modulate(x_ref[0], g_ref[...], modl_ref, modc_ref, 3, pos0, ctx_len))
        acc_ref[...] = jnp.zeros_like(acc_ref)

    h = h_ref[...]
    act = _silu(jnp.dot(h, wg_ref[...], preferred_element_type=F32)) * jnp.dot(h, wu_ref[...],
                                                                               preferred_element_type=F32)
    acc_ref[...] += _dot(act, wd_ref[...])

    @pl.when(f == n_f - 1)
    def _():
        o_ref[0] = x_ref[0] + _gate_vec(modl_ref, modc_ref, 5, pos0, tm, ctx_len) * acc_ref[...]


def _ffn(x, mods, ctx_row, g, wg, wu, wd, ctx_len, tm, tf):
    b, s, d = x.shape
    ff = wg.shape[1]
    n_f = ff // tf
    return pl.pallas_call(
        functools.partial(_ffn_kernel, ctx_len=ctx_len, tm=tm, n_f=n_f),
        grid=(b, s // tm, n_f),
        in_specs=[pl.BlockSpec((1, tm, d), lambda i, j, f: (i, j, 0))] + _mod_specs(d, ctx_row) + [
            pl.BlockSpec((1, d), lambda i, j, f: (0, 0)),
            pl.BlockSpec((d, tf), lambda i, j, f: (0, f)),
            pl.BlockSpec((d, tf), lambda i, j, f: (0, f)),
            pl.BlockSpec((tf, d), lambda i, j, f: (f, 0))],
        out_specs=pl.BlockSpec((1, tm, d), lambda i, j, f: (i, j, 0)),
        out_shape=jax.ShapeDtypeStruct((b, s, d), F32),
        scratch_shapes=[pltpu.VMEM((tm, d), BF16), pltpu.VMEM((tm, d), F32)],
        compiler_params=_cparams(("parallel", "parallel", "arbitrary")),
        name="ffn",
    )(x, mods, mods, g.reshape(1, d), wg, wu, wd)


def _attn_kernel(sink_ref, q_ref, kc_ref, vc_ref, kp_ref, kq_ref, kn_ref, vp_ref, vq_ref, vn_ref, o_ref, *,
                 ctx_len, n_lat):
    i = pl.program_id(1)
    nk = ctx_len + 3 * ATT_BLOCK
    k_all = jnp.concatenate([kc_ref[0], kp_ref[0], kq_ref[0], kn_ref[0]], axis=0)
    v_all = jnp.concatenate([vc_ref[0], vp_ref[0], vq_ref[0], vn_ref[0]], axis=0)
    row = lax.broadcasted_iota(jnp.int32, (ATT_BLOCK, nk), 0)
    col = lax.broadcasted_iota(jnp.int32, (ATT_BLOCK, nk), 1)
    rel = col - (ctx_len + ATT_BLOCK)
    kabs = i * ATT_BLOCK + rel
    band = (jnp.abs(row - rel) <= WINDOW) & (kabs >= 0) & (kabs < n_lat)
    bias = jnp.where((col < ctx_len) | band, 0.0, NEG_INF)
    bias = jnp.concatenate([bias] * GQ, axis=0)
    grp = lax.broadcasted_iota(jnp.int32, (GQ * ATT_BLOCK, 1), 0) // ATT_BLOCK
    scale = HD ** -0.5
    for hk in range(HKV):
        kh = _bf(k_all[:, hk * HD:(hk + 1) * HD])
        vh = _bf(v_all[:, hk * HD:(hk + 1) * HD])
        heads = [hk * GQ + g for g in range(GQ)]
        q = jnp.concatenate([q_ref[0, :, hq * HD:(hq + 1) * HD] for hq in heads], axis=0) * scale
        sk = jnp.zeros((GQ * ATT_BLOCK, 1), F32)
        for g, hq in enumerate(heads):
            sk = jnp.where(grp == g, sink_ref[hq], sk)
        s = _dot_nt(q, kh) + bias
        m = jnp.maximum(jnp.max(s, axis=-1, keepdims=True), sk)
        p = jnp.exp(s - m)
        den = jnp.sum(p, axis=-1, keepdims=True) + jnp.exp(sk - m)
        o = jnp.dot(_bf(p), vh, preferred_element_type=F32) / den
        for g, hq in enumerate(heads):
            o_ref[0, :, hq * HD:(hq + 1) * HD] = o[g * ATT_BLOCK:(g + 1) * ATT_BLOCK]


def _attention(qkv, sink, ctx_len):
    b, s, _ = qkv.shape
    n_lat = s - ctx_len
    nb = n_lat // ATT_BLOCK
    cb = ctx_len // ATT_BLOCK
    kcol = Q_COLS // KV_COLS
    vcol = kcol + 1

    def band(colblk, shift):
        return pl.BlockSpec((1, ATT_BLOCK, KV_COLS),
                            lambda bi, i: (bi, cb + jnp.clip(i + shift, 0, nb - 1), colblk))

    return pl.pallas_call(
        functools.partial(_attn_kernel, ctx_len=ctx_len, n_lat=n_lat),
        grid=(b, nb),
        in_specs=[pl.BlockSpec(memory_space=pltpu.SMEM),
                  pl.BlockSpec((1, ATT_BLOCK, Q_COLS), lambda bi, i: (bi, cb + i, 0)),
                  pl.BlockSpec((1, ctx_len, KV_COLS), lambda bi, i: (bi, 0, kcol)),
                  pl.BlockSpec((1, ctx_len, KV_COLS), lambda bi, i: (bi, 0, vcol)),
                  band(kcol, -1), band(kcol, 0), band(kcol, 1),
                  band(vcol, -1), band(vcol, 0), band(vcol, 1)],
        out_specs=pl.BlockSpec((1, ATT_BLOCK, Q_COLS), lambda bi, i: (bi, i, 0)),
        out_shape=jax.ShapeDtypeStruct((b, n_lat, Q_COLS), F32),
        compiler_params=_cparams(("parallel", "parallel")),
        name="attention",
    )(sink, qkv, qkv, qkv, qkv, qkv, qkv, qkv, qkv, qkv)


def _router_kernel(x_ref, modl_ref, modc_ref, g_ref, w_ref, b_ref, o_ref, e_ref, f_ref, *, tm):
    f = _modulate(x_ref[0], g_ref[...], modl_ref, modc_ref, 3, 0, 0)
    f_ref[0] = f
    logits = _dot_f32(f, w_ref[...]) + b_ref[...]
    lane = lax.broadcasted_iota(jnp.int32, logits.shape, 1).astype(F32)
    logits = jnp.where(lane < N_EXPERTS, logits, NEG_INF)
    ex = jnp.exp(logits - jnp.max(logits, axis=-1, keepdims=True))
    probs = ex / jnp.sum(ex, axis=-1, keepdims=True)
    p1 = jnp.max(probs, axis=-1, keepdims=True)
    i1 = jnp.min(jnp.where(probs == p1, lane, float(LANES)), axis=-1, keepdims=True)
    rest = jnp.where(lane == i1, -1.0, probs)
    p2 = jnp.max(rest, axis=-1, keepdims=True)
    i2 = jnp.min(jnp.where(rest == p2, lane, float(LANES)), axis=-1, keepdims=True)
    tot = p1 + p2
    o_ref[0] = jnp.where(lane == 0.0, p1 / tot, jnp.where(lane == 1.0, p2 / tot, 0.0))
    e_ref[0] = jnp.where(lane == 0.0, i1, jnp.where(lane == 1.0, i2, 0.0)).astype(jnp.int32)


def _router(x, mods, g, w, bias):
    b, s, d = x.shape
    tm = TOK_TILE
    wpad = jnp.zeros((d, LANES), F32).at[:, :N_EXPERTS].set(w)
    bpad = jnp.zeros((1, LANES), F32).at[0, :N_EXPERTS].set(bias)
    lane_spec = pl.BlockSpec((1, tm, LANES), lambda i, j: (i, j, 0))
    return pl.pallas_call(
        functools.partial(_router_kernel, tm=tm),
        grid=(b, s // tm),
        in_specs=[pl.BlockSpec((1, tm, d), lambda i, j: (i, j, 0))] + _mod_specs(d, 0) + [
            pl.BlockSpec((1, d), lambda i, j: (0, 0)),
            pl.BlockSpec((d, LANES), lambda i, j: (0, 0)),
            pl.BlockSpec((1, LANES), lambda i, j: (0, 0))],
        out_specs=[lane_spec, lane_spec, pl.BlockSpec((1, tm, d), lambda i, j: (i, j, 0))],
        out_shape=[jax.ShapeDtypeStruct((b, s, LANES), F32), jax.ShapeDtypeStruct((b, s, LANES), jnp.int32),
                   jax.ShapeDtypeStruct((b, s, d), F32)],
        compiler_params=_cparams(("parallel", "parallel")),
        name="router",
    )(x, mods, mods, g.reshape(1, d), wpad, bpad)


MOE_ROWS = 256


def _row_scatter_kernel(dest_ref, pad_ref, f_ref, xs_hbm, zero_ref, sem, *, tm, n_pad):
    base = pl.program_id(0) * tm

    @pl.when(pl.program_id(0) == 0)
    def _():
        zero_ref[...] = jnp.zeros_like(zero_ref)

        def zero_row(r, carry):
            pltpu.make_async_copy(zero_ref, xs_hbm.at[pl.ds(pad_ref[r], 1)], sem).start()
            return carry

        lax.fori_loop(0, n_pad, zero_row, 0, unroll=8)
        for _ in range(n_pad // tm):
            pltpu.make_async_copy(f_ref, xs_hbm.at[pl.ds(0, tm)], sem).wait()

    def issue(r, carry):
        slot = 2 * (base + r)
        pltpu.make_async_copy(f_ref.at[pl.ds(r, 1)], xs_hbm.at[pl.ds(dest_ref[slot], 1)], sem).start()
        pltpu.make_async_copy(f_ref.at[pl.ds(r, 1)], xs_hbm.at[pl.ds(dest_ref[slot + 1], 1)], sem).start()
        return carry

    lax.fori_loop(0, tm, issue, 0, unroll=8)
    for _ in range(2):
        pltpu.make_async_copy(f_ref, xs_hbm.at[pl.ds(0, tm)], sem).wait()


def _row_scatter(f, dest, pad_rows, n_rows):
    n_tok, d = f.shape
    tm = TOK_TILE
    n_pad = pad_rows.shape[0]
    assert n_pad % tm == 0 and 2 * n_tok + n_pad == n_rows
    return pl.pallas_call(
        functools.partial(_row_scatter_kernel, tm=tm, n_pad=n_pad),
        grid_spec=pltpu.PrefetchScalarGridSpec(
            num_scalar_prefetch=2,
            grid=(n_tok // tm,),
            in_specs=[pl.BlockSpec((tm, d), lambda i, dr, pr: (i, 0))],
            out_specs=pl.BlockSpec(memory_space=pl.ANY),
            scratch_shapes=[pltpu.VMEM((1, d), f.dtype), pltpu.SemaphoreType.DMA(())]),
        out_shape=jax.ShapeDtypeStruct((n_rows, d), f.dtype),
        compiler_params=_cparams(("arbitrary",)),
        name="moe_scatter",
    )(dest, pad_rows, f)


def _expert_kernel(be_ref, x_ref, wg_ref, wu_ref, wd_ref, o_ref):
    h = _bf(x_ref[...])
    act = _silu(jnp.dot(h, wg_ref[0], preferred_element_type=F32)) * jnp.dot(h, wu_ref[0],
                                                                              preferred_element_type=F32)
    o_ref[...] = _dot(act, wd_ref[0])


def _experts(xs, block_e, wg, wu, wd):
    n_rows, d = xs.shape
    ff = wg.shape[2]
    rows = MOE_ROWS
    return pl.pallas_call(
        _expert_kernel,
        grid_spec=pltpu.PrefetchScalarGridSpec(
            num_scalar_prefetch=1,
            grid=(n_rows // rows,),
            in_specs=[pl.BlockSpec((rows, d), lambda i, be: (i, 0)),
                      pl.BlockSpec((1, d, ff), lambda i, be: (be[i], 0, 0)),
                      pl.BlockSpec((1, d, ff), lambda i, be: (be[i], 0, 0)),
                      pl.BlockSpec((1, ff, d), lambda i, be: (be[i], 0, 0))],
            out_specs=pl.BlockSpec((rows, d), lambda i, be: (i, 0))),
        out_shape=jax.ShapeDtypeStruct((n_rows, d), F32),
        compiler_params=_cparams(("arbitrary",)),
        name="moe_experts",
    )(block_e, xs, wg, wu, wd)


def _combine_kernel(dest_ref, x_ref, modl_ref, modc_ref, w_ref, gfin_ref, ys_hbm, o_ref, y1_ref, y2_ref, sem, *,
                    tm, tiles_per_row):
    tile = pl.program_id(0) * tiles_per_row + pl.program_id(1)
    n_tiles = pl.num_programs(0) * tiles_per_row
    buf = tile % 2

    def fetch(t, b):
        def issue(r, carry):
            slot = 2 * (t * tm + r)
            pltpu.make_async_copy(ys_hbm.at[pl.ds(dest_ref[slot], 1)], y1_ref.at[b, pl.ds(r, 1)], sem.at[b]).start()
            pltpu.make_async_copy(ys_hbm.at[pl.ds(dest_ref[slot + 1], 1)], y2_ref.at[b, pl.ds(r, 1)],
                                  sem.at[b]).start()
            return carry

        lax.fori_loop(0, tm, issue, 0, unroll=8)

    @pl.when(tile == 0)
    def _():
        fetch(0, 0)

    @pl.when(tile + 1 < n_tiles)
    def _():
        fetch(tile + 1, 1 - buf)

    pltpu.make_async_copy(ys_hbm.at[pl.ds(0, tm)], y1_ref.at[buf], sem.at[buf]).wait()
    pltpu.make_async_copy(ys_hbm.at[pl.ds(0, tm)], y2_ref.at[buf], sem.at[buf]).wait()
    w = w_ref[0]
    moe = w[:, 0:1] * y1_ref[buf] + w[:, 1:2] * y2_ref[buf]
    y = x_ref[0] + _gate_vec(modl_ref, modc_ref, 5, 0, tm, 0) * moe
    o_ref[0] = y * lax.rsqrt(jnp.mean(y * y, axis=-1, keepdims=True) + NORM_EPS) * gfin_ref[...]


def _combine(x, mods, tokw, dest, ys, final_g):
    b, s, d = x.shape
    tm = TOK_TILE
    tiles = s // tm
    return pl.pallas_call(
        functools.partial(_combine_kernel, tm=tm, tiles_per_row=tiles),
        grid_spec=pltpu.PrefetchScalarGridSpec(
            num_scalar_prefetch=1,
            grid=(b, tiles),
            in_specs=[pl.BlockSpec((1, tm, d), lambda i, j, dr: (i, j, 0)),
                      pl.BlockSpec((1, N_MOD, d), lambda i, j, dr: (i, 0, 0)),
                      pl.BlockSpec((1, N_MOD, d), lambda i, j, dr: (0, 0, 0)),
                      pl.BlockSpec((1, tm, LANES), lambda i, j, dr: (i, j, 0)),
                      pl.BlockSpec((1, d), lambda i, j, dr: (0, 0)),
                      pl.BlockSpec(memory_space=pl.ANY)],
            out_specs=pl.BlockSpec((1, tm, d), lambda i, j, dr: (i, j, 0)),
            scratch_shapes=[pltpu.VMEM((2, tm, d), F32), pltpu.VMEM((2, tm, d), F32),
                            pltpu.SemaphoreType.DMA((2,))]),
        out_shape=jax.ShapeDtypeStruct((b, s, d), F32),
        compiler_params=_cparams(("arbitrary", "arbitrary")),
        name="moe_combine",
    )(dest, x, mods, mods, tokw, final_g.reshape(1, d), ys)


def _moe_routing(eidx):
    m = eidx.shape[0]
    oh = (eidx[:, None] == jnp.arange(N_EXPERTS, dtype=jnp.int32)[None, :]).astype(jnp.int32)
    csum = jnp.cumsum(oh, axis=0)
    rank = jnp.sum((csum - oh) * oh, axis=1)
    counts = csum[-1]
    padded = (counts + MOE_ROWS - 1) // MOE_ROWS * MOE_ROWS
    pad_end = jnp.cumsum(padded)
    dest = jnp.sum(oh * (pad_end - padded)[None, :], axis=1) + rank
    n_blocks = m // MOE_ROWS + N_EXPERTS
    starts = jnp.arange(n_blocks, dtype=jnp.int32) * MOE_ROWS
    block_e = jnp.minimum(jnp.sum((starts[:, None] >= pad_end[None, :]).astype(jnp.int32), axis=1), N_EXPERTS - 1)
    pad_cnt = padded - counts
    pad_cum = jnp.cumsum(pad_cnt)
    idx = jnp.arange(N_EXPERTS * MOE_ROWS, dtype=jnp.int32)
    owner = jnp.sum((idx[:, None] >= pad_cum[None, :]).astype(jnp.int32), axis=1)
    own = (jnp.minimum(owner, N_EXPERTS - 1)[:, None] == jnp.arange(N_EXPERTS)[None, :]).astype(jnp.int32)
    in_expert = jnp.sum(own * (pad_end - pad_cnt)[None, :], axis=1) + idx - jnp.sum(
        own * (pad_cum - pad_cnt)[None, :], axis=1)
    pad_rows = jnp.where(owner < N_EXPERTS, in_expert, pad_end[-1] + idx - pad_cum[-1])
    return dest.astype(jnp.int32), block_e, pad_rows.astype(jnp.int32), n_blocks * MOE_ROWS


def _rope_tables(n_lat, ctx_len):
    rows = n_lat // GRID_W
    row = jnp.repeat(jnp.arange(rows, dtype=F32), GRID_W)
    col = jnp.tile(jnp.arange(GRID_W, dtype=F32), rows)
    inv = ROPE_BASE ** (-jnp.arange(0, AX_DIM, 2, dtype=F32) / AX_DIM)
    ar, ac = row[:, None] * inv, col[:, None] * inv
    cos = jnp.concatenate([jnp.cos(ar), jnp.cos(ar), jnp.cos(ac), jnp.cos(ac)], axis=-1)
    sin = jnp.concatenate([-jnp.sin(ar), jnp.sin(ar), -jnp.sin(ac), jnp.sin(ac)], axis=-1)
    n_heads = ROPE_COLS // HD
    cos = jnp.concatenate([jnp.ones((ctx_len, HD), F32), cos], axis=0)
    sin = jnp.concatenate([jnp.zeros((ctx_len, HD), F32), sin], axis=0)
    half = AX_DIM // 2
    j = jnp.arange(HD)
    partner = jnp.where((j % AX_DIM) < half, j + half, j - half)
    perm = (jnp.arange(n_heads)[:, None] * HD + partner[None, :]).reshape(-1)
    return jnp.tile(cos, (1, n_heads)), jnp.tile(sin, (1, n_heads)), perm


def kernel(x, c, ctx, c_ctx, mod_w, mod_b, norm_mix, norm_ffn, norm_final, rec_w_in, rec_w_out, rwkv_mu, rwkv_w0, rwkv_w_up, rwkv_a0, rwkv_a_up, rwkv_g_up, rwkv_k_k, rwkv_k_a, rwkv_r_k, rwkv_ln_w, rwkv_ln_b, hgrn_lb, hgrn_norm, ffn_w_gate, ffn_w_up, ffn_w_down, att_w_in, att_w_out, att_sink, moe_router, moe_router_b, moe_w_gate, moe_w_up, moe_w_down):
    bsz, n_lat, d = x.shape
    ctx_len = ctx.shape[1]
    xcat = jnp.concatenate([ctx, x], axis=1)

    n_rows = -(-(bsz + 1) // 8) * 8
    cvec = jnp.zeros((n_rows, d), F32).at[:bsz].set(c).at[bsz].set(c_ctx)
    mods = [_adaln(cvec, mod_w[l], mod_b[l]).reshape(n_rows, N_MOD, d) for l in range(2)]

    p_rwkv, p_hgrn = _proj(xcat, mods[0], bsz, norm_mix[0], _bf(rec_w_in[0]), (RWKV_COLS, HGRN_COLS), ctx_len)
    feats = _rwkv_features(p_rwkv, rwkv_mu[0], rwkv_w0[0], rwkv_w_up[0], rwkv_a0[0], rwkv_a_up[0], rwkv_g_up[0],
                           rwkv_k_k[0], rwkv_k_a[0], rwkv_r_k[0].reshape(-1), ctx_len)
    oa = _rwkv_scan(feats, ctx_len)
    lb = jnp.cumsum(jax.nn.softmax(hgrn_lb.astype(F32), axis=1), axis=1)[:, 0].reshape(2, 1, B_W)
    ob = _hgrn_scan(p_hgrn, lb, ctx_len)
    xcat = _rec_out(xcat, mods[0], bsz, oa, feats, ob, p_hgrn, rwkv_ln_w[0], rwkv_ln_b[0], hgrn_norm[0],
                    _bf(rec_w_out[0]), ctx_len)
    xcat = _ffn(xcat, mods[0], bsz, norm_ffn[0], _bf(ffn_w_gate[0]), _bf(ffn_w_up[0]), _bf(ffn_w_down[0]), ctx_len,
                tm=384, tf=2816)

    cos, sin, perm = _rope_tables(n_lat, ctx_len)
    w_att = att_w_in[0]
    (qkv,) = _proj(xcat, mods[1], bsz, norm_mix[1], _bf(w_att), (ATT_COLS,), ctx_len,
                   rope=(_bf(w_att[:, perm]), cos, sin))
    att = _attention(qkv, att_sink[0], ctx_len)
    x_lat = _res_proj(xcat, mods[1], 0, att, _bf(att_w_out[0]), 0, x_skip=ctx_len)
    tokw, eidx, f_lat = _router(x_lat, mods[1], norm_ffn[1], moe_router[0], moe_router_b[0])
    dest, block_e, pad_rows, n_rows = _moe_routing(eidx[..., :2].reshape(-1))
    xs = _row_scatter(f_lat.reshape(bsz * n_lat, d), dest, pad_rows, n_rows)
    ys = _experts(xs, block_e, _bf(moe_w_gate[0]), _bf(moe_w_up[0]), _bf(moe_w_down[0]))
    return _combine(x_lat, mods[1], tokw, dest, ys, norm_final)
```

```python
import functools

import jax
import jax.numpy as jnp
from jax import lax
from jax.experimental import pallas as pl
from jax.experimental.pallas import tpu as pltpu

F32 = jnp.float32
BF16 = jnp.bfloat16
HIGHEST = lax.Precision.HIGHEST

N_MOD = 6
NORM_EPS = 1e-6
NEG_INF = -1e30

A_HEADS = 8
A_HD = 64
A_W = A_HEADS * A_HD
DECAY_LORA = 64
AAA_LORA = 64
GATE_LORA = 128
RWKV_COLS = 3 * A_W + DECAY_LORA + AAA_LORA + GATE_LORA
GN_EPS = 64e-5

B_HEADS = 4
B_DK = 128
B_W = B_HEADS * B_DK
HGRN_COLS = 5 * B_W

HQ = 16
HKV = 4
GQ = HQ // HKV
HD = 64
WINDOW = 128
ATT_BLOCK = 128
AX_DIM = HD // 2
ROPE_BASE = 10000.0
GRID_W = 64
Q_COLS = HQ * HD
KV_COLS = HKV * HD
ROPE_COLS = Q_COLS + KV_COLS
ATT_COLS = Q_COLS + 2 * KV_COLS

N_EXPERTS = 8
LANES = 128
CHUNK = 64
SCAN_BLOCK = 256
MXU_WIDTH = 256
RWKV_GROUP = MXU_WIDTH // A_HD
TOK_TILE = 256
VMEM_LIMIT = 56 * 1024 * 1024

OP_KAP, OP_RT, OP_KBAR, OP_BBAR, OP_KGAM, OP_BGAM = range(6)
N_OPS = 6


def _cparams(sem):
    return pltpu.CompilerParams(dimension_semantics=sem, vmem_limit_bytes=VMEM_LIMIT)


def _bf(x):
    return x.astype(BF16)


def _dot(a, b):
    return jnp.dot(_bf(a), _bf(b), preferred_element_type=F32)


def _dot_nt(a, b):
    return lax.dot_general(_bf(a), _bf(b), (((1,), (1,)), ((), ())), preferred_element_type=F32)


def _dot_tn(a, b):
    return lax.dot_general(_bf(a), _bf(b), (((0,), (0,)), ((), ())), preferred_element_type=F32)


def _dot_f32(a, b):
    return jnp.dot(a, b, preferred_element_type=F32, precision=HIGHEST)


def _dot_tn_f32(a, b):
    return lax.dot_general(a, b, (((0,), (0,)), ((), ())), preferred_element_type=F32, precision=HIGHEST)


def _sigmoid(x):
    return 1.0 / (1.0 + jnp.exp(-x))


def _silu(x):
    return x * _sigmoid(x)


def _segsum(x, ones_bd):
    hi = _bf(x)
    lo = _bf(x - hi.astype(F32))
    return jnp.dot(hi, ones_bd, preferred_element_type=F32) + jnp.dot(lo, ones_bd, preferred_element_type=F32)


def _dot_split(a, x):
    hi = _bf(x)
    r1 = x - hi.astype(F32)
    mid = _bf(r1)
    lo = _bf(r1 - mid.astype(F32))
    acc = jnp.dot(a, hi, preferred_element_type=F32)
    acc += jnp.dot(a, mid, preferred_element_type=F32)
    acc += jnp.dot(a, lo, preferred_element_type=F32)
    return acc


def _block_ones(width, seg):
    i = jnp.arange(width) // seg
    return (i[:, None] == i[None, :]).astype(BF16)


def _modulate(x, g, modl_ref, modc_ref, row, pos0, ctx_len):
    y = x * lax.rsqrt(jnp.mean(x * x, axis=-1, keepdims=True) + NORM_EPS) * g
    pos = pos0 + lax.broadcasted_iota(jnp.int32, (x.shape[0], 1), 0)
    is_ctx = pos < ctx_len
    shift = jnp.where(is_ctx, modc_ref[0, row:row + 1, :], modl_ref[0, row:row + 1, :])
    scale = jnp.where(is_ctx, modc_ref[0, row + 1:row + 2, :], modl_ref[0, row + 1:row + 2, :])
    return y * (1.0 + scale) + shift


def _gate_vec(modl_ref, modc_ref, row, pos0, n, ctx_len):
    pos = pos0 + lax.broadcasted_iota(jnp.int32, (n, 1), 0)
    return jnp.where(pos < ctx_len, modc_ref[0, row:row + 1, :], modl_ref[0, row:row + 1, :])


def _mod_specs(d, ctx_row, batch_axis=0):
    def lat_map(*idx):
        return (idx[batch_axis], 0, 0)

    def ctx_map(*idx):
        return (ctx_row, 0, 0)

    return [pl.BlockSpec((1, N_MOD, d), lat_map), pl.BlockSpec((1, N_MOD, d), ctx_map)]


def _adaln_kernel(c_ref, w_ref, b_ref, o_ref):
    o_ref[...] = _dot(_silu(c_ref[...]), w_ref[...]) + b_ref[...]


def _adaln(cvec, w, b):
    r, d = cvec.shape
    n = w.shape[1]
    tn = 1024
    return pl.pallas_call(
        _adaln_kernel,
        grid=(n // tn,),
        in_specs=[pl.BlockSpec((r, d), lambda j: (0, 0)),
                  pl.BlockSpec((d, tn), lambda j: (0, j)),
                  pl.BlockSpec((1, tn), lambda j: (0, j))],
        out_specs=pl.BlockSpec((r, tn), lambda j: (0, j)),
        out_shape=jax.ShapeDtypeStruct((r, n), F32),
        compiler_params=_cparams(("parallel",)),
        name="adaln",
    )(cvec, w, b.reshape(1, n))


def _proj_kernel(x_ref, modl_ref, modc_ref, g_ref, w_ref, *rest, ctx_len, tm, splits, rope_cols):
    if rope_cols:
        wrot_ref, cos_ref, sin_ref = rest[:3]
        outs = rest[3:]
    else:
        outs = rest
    pos0 = pl.program_id(1) * tm
    h = _bf(_modulate(x_ref[0], g_ref[...], modl_ref, modc_ref, 0, pos0, ctx_len))
    y = jnp.dot(h, w_ref[...], preferred_element_type=F32)
    if rope_cols:
        yr = jnp.dot(h, wrot_ref[...], preferred_element_type=F32)
        roped = y[:, :rope_cols] * cos_ref[...] + yr * sin_ref[...]
        outs[0][0, :, :rope_cols] = roped
        outs[0][0, :, rope_cols:] = y[:, rope_cols:]
    else:
        lo = 0
        for o_ref, width in zip(outs, splits):
            o_ref[0] = y[:, lo:lo + width]
            lo += width


def _proj(xcat, mods, ctx_row, g, w, splits, ctx_len, rope=None):
    b, s, d = xcat.shape
    n = w.shape[1]
    tm = TOK_TILE
    in_specs = [pl.BlockSpec((1, tm, d), lambda i, j: (i, j, 0))] + _mod_specs(d, ctx_row) + [
        pl.BlockSpec((1, d), lambda i, j: (0, 0)),
        pl.BlockSpec((d, n), lambda i, j: (0, 0))]
    args = [xcat, mods, mods, g.reshape(1, d), w]
    rope_cols = 0
    if rope is not None:
        wrot, cos, sin = rope
        rope_cols = wrot.shape[1]
        in_specs += [pl.BlockSpec((d, rope_cols), lambda i, j: (0, 0)),
                     pl.BlockSpec((tm, rope_cols), lambda i, j: (j, 0)),
                     pl.BlockSpec((tm, rope_cols), lambda i, j: (j, 0))]
        args += [wrot, cos, sin]
    out_shape = [jax.ShapeDtypeStruct((b, s, width), F32) for width in splits]
    out_specs = [pl.BlockSpec((1, tm, width), lambda i, j: (i, j, 0)) for width in splits]
    return pl.pallas_call(
        functools.partial(_proj_kernel, ctx_len=ctx_len, tm=tm, splits=tuple(splits), rope_cols=rope_cols),
        grid=(b, s // tm),
        in_specs=in_specs,
        out_specs=out_specs,
        out_shape=out_shape,
        compiler_params=_cparams(("parallel", "parallel")),
        name="proj_rope" if rope_cols else "proj",
    )(*args)


def _softplus(x):
    return jnp.maximum(x, 0.0) + jnp.log(1.0 + jnp.exp(-jnp.abs(x)))


def _rwkv_feat_kernel(p_ref, pp_ref, pn_ref, mu_ref, w0_ref, wup_ref, a0_ref, aup_ref, gup_ref,
                      kk_ref, ka_ref, rk_ref, ones_ref, ops_ref, v_ref, gb_ref, gam_ref, *, ctx_len, seq_len, tm):
    pos0 = pl.program_id(1) * tm
    p = p_ref[0]
    rows = lax.broadcasted_iota(jnp.int32, (tm, 1), 0)
    pos = rows + pos0
    prev = jnp.where(rows == 0, pp_ref[0, 7:8, :], pltpu.roll(p, 1, 0))
    prev = jnp.where((pos == 0) | (pos == ctx_len), 0.0, prev)
    nxt = jnp.where(rows == tm - 1, pn_ref[0, 0:1, :], pltpu.roll(p, tm - 1, 0))
    nxt = jnp.where((pos == ctx_len - 1) | (pos == seq_len - 1), 0.0, nxt)
    p = p + mu_ref[...] * (0.5 * (prev + nxt) - p)

    r = p[:, 0:A_W]
    k = p[:, A_W:2 * A_W]
    v = p[:, 2 * A_W:3 * A_W]
    lo = 3 * A_W
    wd = p[:, lo:lo + DECAY_LORA]
    ad = p[:, lo + DECAY_LORA:lo + DECAY_LORA + AAA_LORA]
    gd = p[:, lo + DECAY_LORA + AAA_LORA:lo + DECAY_LORA + AAA_LORA + GATE_LORA]

    tw = jnp.tanh(wd)
    a = _sigmoid(a0_ref[...] + _dot(ad, aup_ref[...]))
    ones_bd = ones_ref[...]
    kk = k * kk_ref[...]
    kk = kk / jnp.maximum(jnp.sqrt(_segsum(kk * kk, ones_bd)), 1e-12)
    k = k * (1.0 + (a - 1.0) * ka_ref[...])
    b = kk * a
    v_ref[0] = _bf(v)
    gb_ref[0, :, :A_W] = _dot(_sigmoid(gd), gup_ref[...])
    gb_ref[0, :, A_W:] = _segsum(r * k * rk_ref[...], ones_bd) * v

    trow = lax.broadcasted_iota(jnp.int32, (tm, tm), 0)
    tcol = lax.broadcasted_iota(jnp.int32, (tm, tm), 1)
    same = (trow // CHUNK) == (tcol // CHUNK)
    m_tot = jnp.where(same, 1.0, 0.0).astype(BF16)
    for d in range(2):
        w_log = -_softplus(-(w0_ref[d:d + 1, :] + _dot(tw, wup_ref[d]))) - 0.5
        lw = -jnp.exp(w_log)
        before = (tcol <= trow) if d == 0 else (tcol >= trow)
        g_incl = _dot_split(jnp.where(same & before, 1.0, 0.0).astype(BF16), lw)
        g_tot = _dot_split(m_tot, lw)
        e_ng = jnp.exp(-g_incl)
        e_tail = jnp.exp(g_tot - g_incl)
        operands = {OP_KAP: kk * jnp.exp(g_incl - lw),
                    OP_RT: r * jnp.exp(g_incl),
                    OP_KBAR: k * e_ng, OP_BBAR: b * e_ng,
                    OP_KGAM: k * e_tail, OP_BGAM: b * e_tail}
        for sec, val in operands.items():
            ops_ref[0, d, :, sec * A_W:(sec + 1) * A_W] = _bf(val)
        gam_ref[0, d, 0] = jnp.exp(jnp.concatenate([g_tot[c * CHUNK:c * CHUNK + 1] for c in range(tm // CHUNK)],
                                                   axis=0))


def _rwkv_features(p, mu, w0, w_up, a0, a_up, g_up, k_k, k_a, r_k, ctx_len):
    b, s, c = p.shape
    tm = SCAN_BLOCK
    nb8 = s // 8
    row = lambda a: a.reshape(1, -1)
    full = lambda a: pl.BlockSpec(a.shape, lambda i, j: (0,) * a.ndim)
    args = [row(mu), w0, w_up, row(a0), a_up, g_up, row(k_k), row(k_a), row(r_k), _block_ones(A_W, A_HD)]
    return pl.pallas_call(
        functools.partial(_rwkv_feat_kernel, ctx_len=ctx_len, seq_len=s, tm=tm),
        grid=(b, s // tm),
        in_specs=[pl.BlockSpec((1, tm, c), lambda i, j: (i, j, 0)),
                  pl.BlockSpec((1, 8, c), lambda i, j: (i, jnp.maximum(j * (tm // 8) - 1, 0), 0)),
                  pl.BlockSpec((1, 8, c), lambda i, j: (i, jnp.minimum((j + 1) * (tm // 8), nb8 - 1), 0)),
                  ] + [full(a) for a in args],
        out_specs=[pl.BlockSpec((1, 2, tm, N_OPS * A_W), lambda i, j: (i, 0, j, 0)),
                   pl.BlockSpec((1, tm, A_W), lambda i, j: (i, j, 0)),
                   pl.BlockSpec((1, tm, 2 * A_W), lambda i, j: (i, j, 0)),
                   pl.BlockSpec((1, 2, 1, tm // CHUNK, A_W), lambda i, j: (i, 0, j, 0, 0))],
        out_shape=[jax.ShapeDtypeStruct((b, 2, s, N_OPS * A_W), BF16),
                   jax.ShapeDtypeStruct((b, s, A_W), BF16),
                   jax.ShapeDtypeStruct((b, s, 2 * A_W), F32),
                   jax.ShapeDtypeStruct((b, 2, s // tm, tm // CHUNK, A_W), F32)],
        compiler_params=_cparams(("parallel", "parallel")),
        name="rwkv_features",
    )(p, p, p, *args)


def _chunk_order(d, j, n_ctx_chunks, n_chunks):
    back = jnp.where(j < n_ctx_chunks, n_ctx_chunks - 1 - j, n_chunks - 1 + n_ctx_chunks - j)
    return jnp.where(d == 0, j, back)


def _order_masks(rev):
    row = lax.broadcasted_iota(jnp.int32, (CHUNK, CHUNK), 0)
    col = lax.broadcasted_iota(jnp.int32, (CHUNK, CHUNK), 1)
    diff = jnp.where(rev, row - col, col - row)
    return diff < 0, diff <= 0, row == col


def _scan_step_coords(t, n_blocks, n_steps):
    return jnp.minimum(t, n_steps - 1), jnp.maximum(t - 1, 0)


def _rwkv_chunk_kernel(ops_ref, v_ref, gam_ref, o_ref, s_ref, tr_ref, uu_ref, ol_ref, ab_ref,
                       m2_ref, cc_ref, gm_ref, *, n_blocks, n_steps):
    step = pl.program_id(0)
    t_in, t_out = _scan_step_coords(step, n_blocks, n_steps)
    rev = (t_in // n_blocks) % 2 == 1
    rev_out = (t_out // n_blocks) % 2 == 1
    first_out = t_out % n_blocks == 0
    w_slot = step % 2
    r_slot = 1 - w_slot

    @pl.when(step == 0)
    def _():
        s_ref[...] = jnp.zeros_like(s_ref)
        for ref in (tr_ref, uu_ref, ol_ref, ab_ref, m2_ref, cc_ref, gm_ref):
            ref[1] = jnp.zeros(ref.shape[1:], ref.dtype)

    gw = RWKV_GROUP * A_HD
    groups = range(A_HEADS // RWKV_GROUP)
    gsl = [slice(g * gw, (g + 1) * gw) for g in groups]
    row = lax.broadcasted_iota(jnp.int32, (CHUNK, gw), 0)
    col = lax.broadcasted_iota(jnp.int32, (CHUNK, gw), 1) % A_HD
    diff = jnp.where(rev, row - col, col - row)
    strict_c = diff < 0
    incl_c = diff <= 0
    eye_c = jnp.where(diff == 0, 1.0, 0.0)
    bd_mask = (lax.broadcasted_iota(jnp.int32, (gw, gw), 0) // A_HD
               == lax.broadcasted_iota(jnp.int32, (gw, gw), 1) // A_HD)

    def bd(x):
        return jnp.where(bd_mask, jnp.concatenate([x] * RWKV_GROUP, axis=0), jnp.zeros((), x.dtype))

    def stack(x):
        return jnp.concatenate([x[:, h * A_HD:(h + 1) * A_HD] for h in range(RWKV_GROUP)], axis=0)

    n = SCAN_BLOCK // CHUNK
    rows = [pl.ds(pl.multiple_of(jnp.where(rev, (n - 1 - i) * CHUNK, i * CHUNK), CHUNK), CHUNK) for i in range(n)]
    rows_out = [pl.ds(pl.multiple_of(jnp.where(rev_out, (n - 1 - i) * CHUNK, i * CHUNK), CHUNK), CHUNK)
                for i in range(n)]
    units = [(i, g) for i in range(n) for g in groups]

    s = [jnp.where(first_out, 0.0, s_ref[g]) for g in groups]
    zr = {}
    pieces = []

    def advance(i):
        for g in groups:
            ui = i * len(groups) + g
            zr[i, g] = _dot_nt(tr_ref[r_slot, ui], bd(s[g]))
            s[g] = s[g] * gm_ref[r_slot, ui] - jnp.dot(_bf(s[g]), m2_ref[r_slot, ui],
                                                       preferred_element_type=F32) + cc_ref[r_slot, ui]

    def emit_out(i):
        for g in groups:
            ui = i * len(groups) + g
            z = zr[i, g][:CHUNK] + uu_ref[r_slot, ui]
            o_ref[0, 0, rows_out[i], gsl[g]] = (zr[i, g][CHUNK:] + ol_ref[r_slot, ui]
                                                - _dot(ab_ref[r_slot, ui], bd(z)))

    for i in range(n):
        pieces += [functools.partial(advance, i), functools.partial(emit_out, i)]

    def state_piece():
        if pieces:
            pieces.pop(0)()

    def operand(sec, u):
        i, g = u
        return ops_ref[0, 0, rows[i], sec * A_W + g * gw:sec * A_W + (g + 1) * gw]

    kap = {u: operand(OP_KAP, u) for u in units}
    rt = {u: operand(OP_RT, u) for u in units}
    kbar = {u: operand(OP_KBAR, u) for u in units}
    bbar = {u: operand(OP_BBAR, u) for u in units}
    kgam = {u: operand(OP_KGAM, u) for u in units}
    bgam = {u: operand(OP_BGAM, u) for u in units}
    v = {(i, g): v_ref[0, rows[i], gsl[g]] for i, g in units}
    chunk_of = [jnp.where(rev, n - 1 - i, i) for i in range(n)]
    gam = {(i, g): gam_ref[0, 0, 0, pl.ds(chunk_of[i], 1), gsl[g]] for i, g in units}

    x = {u: jnp.concatenate([kap[u], rt[u]], axis=0) for u in units}
    yb = {u: _dot_nt(x[u], bd(bbar[u])) for u in units}
    state_piece()
    a = {u: jnp.where(strict_c, -yb[u][:CHUNK], 0.0) for u in units}
    xs = {u: eye_c + a[u] for u in units}
    pw = {u: _dot(a[u], bd(a[u])) for u in units}
    state_piece()
    for _ in range(4):
        st = {u: _dot(jnp.concatenate([pw[u], xs[u]], axis=0), bd(pw[u])) for u in units}
        pw = {u: st[u][:CHUNK] for u in units}
        xs = {u: xs[u] + st[u][CHUNK:] for u in units}
        state_piece()
    t_inv = {u: xs[u] + _dot(xs[u], bd(pw[u])) for u in units}
    state_piece()
    yk = {u: _dot_nt(x[u], bd(kbar[u])) for u in units}
    ykm = {u: jnp.concatenate([jnp.where(strict_c, yk[u][:CHUNK], 0.0), jnp.where(incl_c, yk[u][CHUNK:], 0.0)],
                              axis=0) for u in units}
    wo = {u: _dot(ykm[u], bd(v[u])) for u in units}
    w1 = {u: wo[u][:CHUNK] for u in units}
    o_loc = {u: wo[u][CHUNK:] for u in units}
    state_piece()
    tk = {u: _dot(t_inv[u], bd(kap[u])) for u in units}
    uu = {u: _dot(t_inv[u], bd(w1[u])) for u in units}
    while pieces:
        state_piece()
    for g in groups:
        s_ref[g] = s[g]
    a_rb = {u: jnp.where(incl_c, yb[u][CHUNK:], 0.0) for u in units}
    tu = {u: _dot_tn(jnp.concatenate([stack(tk[u]), stack(uu[u])], axis=1), bd(bgam[u])) for u in units}
    bd_m2 = {u: _bf(bd(tu[u][:A_HD])) for u in units}
    cc = {u: _dot_tn(stack(v[u]), bd(kgam[u])) - tu[u][A_HD:] for u in units}
    for u in units:
        ui = u[0] * len(groups) + u[1]
        tr_ref[w_slot, ui] = jnp.concatenate([_bf(tk[u]), rt[u]], axis=0)
        uu_ref[w_slot, ui] = uu[u]
        ol_ref[w_slot, ui] = o_loc[u]
        ab_ref[w_slot, ui] = _bf(a_rb[u])
        m2_ref[w_slot, ui] = bd_m2[u]
        cc_ref[w_slot, ui] = cc[u]
        gm_ref[w_slot, ui] = gam[u]


def _rwkv_scan(ops, vv, gam, ctx_len):
    b, _, s, _ = ops.shape
    nc, ncc = s // SCAN_BLOCK, ctx_len // SCAN_BLOCK

    n_steps = b * 2 * nc

    def coords(flat):
        d = (flat // nc) % 2
        return flat // (2 * nc), d, _chunk_order(d, flat % nc, ncc, nc)

    def coords_in(t):
        return coords(_scan_step_coords(t, nc, n_steps)[0])

    def out_map(t):
        bi, d, blk = coords(_scan_step_coords(t, nc, n_steps)[1])
        return bi, d, blk, 0

    def ops_map(t):
        bi, d, blk = coords_in(t)
        return bi, d, blk, 0

    def v_map(t):
        bi, _, blk = coords_in(t)
        return bi, blk, 0

    def gam_map(t):
        bi, d, blk = coords_in(t)
        return bi, d, blk, 0, 0

    n_groups = A_HEADS // RWKV_GROUP
    n_chunks = SCAN_BLOCK // CHUNK
    n_units = n_chunks * n_groups
    gw = RWKV_GROUP * A_HD
    return pl.pallas_call(
        functools.partial(_rwkv_chunk_kernel, n_blocks=nc, n_steps=n_steps),
        grid=(n_steps + 1,),
        in_specs=[pl.BlockSpec((1, 1, SCAN_BLOCK, N_OPS * A_W), ops_map),
                  pl.BlockSpec((1, SCAN_BLOCK, A_W), v_map),
                  pl.BlockSpec((1, 1, 1, n_chunks, A_W), gam_map)],
        out_specs=pl.BlockSpec((1, 1, SCAN_BLOCK, A_W), out_map),
        out_shape=jax.ShapeDtypeStruct((b, 2, s, A_W), F32),
        scratch_shapes=[pltpu.VMEM((n_groups, A_HD, gw), F32),
                        pltpu.VMEM((2, n_units, 2 * CHUNK, gw), BF16),
                        pltpu.VMEM((2, n_units, CHUNK, gw), F32),
                        pltpu.VMEM((2, n_units, CHUNK, gw), F32),
                        pltpu.VMEM((2, n_units, CHUNK, gw), BF16),
                        pltpu.VMEM((2, n_units, gw, gw), BF16),
                        pltpu.VMEM((2, n_units, CHUNK, gw), F32),
                        pltpu.VMEM((2, n_units, 1, gw), F32)],
        compiler_params=_cparams(("arbitrary",)),
        name="rwkv_scan",
    )(ops, vv, gam)


def _hgrn_chunk_kernel(q_ref, i_ref, f_ref, lb_ref, o_ref, s_ref):
    rev = pl.program_id(1) == 1

    @pl.when(pl.program_id(2) == 0)
    def _():
        s_ref[...] = jnp.zeros_like(s_ref)

    _, incl, _ = _order_masks(rev)
    incl_bf = jnp.where(incl, 1.0, 0.0).astype(BF16)
    lb = lb_ref[0]
    heads = range(B_HEADS)
    sls = [slice(h * B_DK, (h + 1) * B_DK) for h in heads]

    n = SCAN_BLOCK // CHUNK
    rows = [pl.ds(pl.multiple_of(jnp.where(rev, (n - 1 - i) * CHUNK, i * CHUNK), CHUNK), CHUNK) for i in range(n)]
    units = [(i, h) for i in range(n) for h in heads]
    q_in, k_in, q_st, k_tail, gam, v = ({} for _ in range(6))
    for i in range(n):
        f = lb + (1.0 - lb) * _sigmoid(f_ref[0, rows[i], :])
        logf = jnp.log(f)
        kf = 1.0 - f
        g_incl = _dot_split(incl_bf, logf)
        g_tot = jnp.sum(logf, axis=0, keepdims=True)
        g_mid = g_incl[CHUNK // 2 - 1:CHUNK // 2, :]
        q = _silu(q_ref[0, rows[i], :])
        v_i = i_ref[0, rows[i], :]
        for h in heads:
            q_in[i, h] = (q * jnp.exp(g_incl - g_mid))[:, sls[h]]
            k_in[i, h] = (kf * jnp.exp(g_mid - g_incl))[:, sls[h]]
            q_st[i, h] = (q * jnp.exp(g_incl))[:, sls[h]]
            k_tail[i, h] = (kf * jnp.exp(g_tot - g_incl))[:, sls[h]]
            gam[i, h] = jnp.exp(g_tot)[:, sls[h]]
            v[i, h] = v_i[:, sls[h]]
    att = {u: jnp.where(incl, _dot_nt(q_in[u], k_in[u]), 0.0) for u in units}
    o_loc = {u: _dot(att[u], v[u]) for u in units}
    kv = {u: _dot_tn(v[u], k_tail[u]) for u in units}

    s = [s_ref[h] for h in heads]
    for i in range(n):
        for h in heads:
            o_ref[0, 0, rows[i], sls[h]] = o_loc[i, h] + _dot_nt(q_st[i, h], s[h])
            s[h] = s[h] * gam[i, h] + kv[i, h]
    for h in heads:
        s_ref[h] = s[h]


def _hgrn_scan(p, lb, ctx_len):
    b, s, _ = p.shape
    nc, ncc = s // SCAN_BLOCK, ctx_len // SCAN_BLOCK

    def sec(idx):
        return pl.BlockSpec((1, SCAN_BLOCK, B_W), lambda i, d, j: (i, _chunk_order(d, j, ncc, nc), idx))

    return pl.pallas_call(
        _hgrn_chunk_kernel,
        grid=(b, 2, nc),
        in_specs=[sec(0), sec(1),
                  pl.BlockSpec((1, SCAN_BLOCK, B_W), lambda i, d, j: (i, _chunk_order(d, j, ncc, nc), 2 + d)),
                  pl.BlockSpec((1, 1, B_W), lambda i, d, j: (d, 0, 0))],
        out_specs=pl.BlockSpec((1, 1, SCAN_BLOCK, B_W), lambda i, d, j: (i, d, _chunk_order(d, j, ncc, nc), 0)),
        out_shape=jax.ShapeDtypeStruct((b, 2, s, B_W), F32),
        scratch_shapes=[pltpu.VMEM((B_HEADS, B_DK, B_DK), F32)],
        compiler_params=_cparams(("parallel", "parallel", "arbitrary")),
        name="hgrn_scan",
    )(p, p, p, lb)


def _rec_out_kernel(x_ref, modl_ref, modc_ref, oa_ref, g_ref, bonus_ref, ob_ref, gate_ref, lnw_ref, lnb_ref,
                    hgn_ref, ones_a_ref, ones_b_ref, w_ref, o_ref, *, ctx_len, tm):
    pos0 = pl.program_id(1) * tm
    oa = oa_ref[0, 0] + oa_ref[0, 1]
    ones_a = ones_a_ref[...]
    mean = _segsum(oa, ones_a) * (1.0 / A_HD)
    cen = oa - mean
    var = _segsum(cen * cen, ones_a) * (1.0 / A_HD)
    ya = (cen * lax.rsqrt(var + GN_EPS) * lnw_ref[...] + lnb_ref[...] + bonus_ref[0]) * g_ref[0]
    ob = ob_ref[0, 0] + ob_ref[0, 1]
    ms = _segsum(ob * ob, ones_b_ref[...]) * (1.0 / B_DK)
    yb = ob * lax.rsqrt(ms + NORM_EPS) * hgn_ref[...] * _silu(gate_ref[0])
    y = _dot(ya, w_ref[:A_W, :]) + _dot(yb, w_ref[A_W:, :])
    gate = _gate_vec(modl_ref, modc_ref, 2, pos0, tm, ctx_len)
    o_ref[0] = x_ref[0] + gate * y


def _rec_out(xcat, mods, ctx_row, oa, g_bonus, ob, p_hgrn, ln_w, ln_b, hg_norm, w_out, ctx_len):
    b, s, d = xcat.shape
    tm = TOK_TILE
    row = lambda a: a.reshape(1, -1)
    full = lambda a: pl.BlockSpec(a.shape, lambda i, j: (0,) * a.ndim)
    consts = [row(ln_w), row(ln_b), row(jnp.tile(hg_norm, B_HEADS)), _block_ones(A_W, A_HD),
              _block_ones(B_W, B_DK), w_out]
    return pl.pallas_call(
        functools.partial(_rec_out_kernel, ctx_len=ctx_len, tm=tm),
        grid=(b, s // tm),
        in_specs=[pl.BlockSpec((1, tm, d), lambda i, j: (i, j, 0))] + _mod_specs(d, ctx_row) + [
            pl.BlockSpec((1, 2, tm, A_W), lambda i, j: (i, 0, j, 0)),
            pl.BlockSpec((1, tm, A_W), lambda i, j: (i, j, 0)),
            pl.BlockSpec((1, tm, A_W), lambda i, j: (i, j, 1)),
            pl.BlockSpec((1, 2, tm, B_W), lambda i, j: (i, 0, j, 0)),
            pl.BlockSpec((1, tm, B_W), lambda i, j: (i, j, 4)),
        ] + [full(a) for a in consts],
        out_specs=pl.BlockSpec((1, tm, d), lambda i, j: (i, j, 0)),
        out_shape=jax.ShapeDtypeStruct((b, s, d), F32),
        compiler_params=_cparams(("parallel", "parallel")),
        name="rec_out",
    )(xcat, mods, mods, oa, g_bonus, g_bonus, ob, p_hgrn, *consts)


def _res_proj_kernel(x_ref, modl_ref, modc_ref, y_ref, w_ref, o_ref, *, ctx_len, tm):
    pos0 = pl.program_id(1) * tm
    gate = _gate_vec(modl_ref, modc_ref, 2, pos0, tm, ctx_len)
    o_ref[0] = x_ref[0] + gate * _dot(y_ref[0], w_ref[...])


def _res_proj(x, mods, ctx_row, y, w, ctx_len, x_skip=0):
    b, s, k = y.shape
    d = x.shape[-1]
    tm = TOK_TILE
    skip = x_skip // tm
    return pl.pallas_call(
        functools.partial(_res_proj_kernel, ctx_len=ctx_len, tm=tm),
        grid=(b, s // tm),
        in_specs=[pl.BlockSpec((1, tm, d), lambda i, j: (i, j + skip, 0))] + _mod_specs(d, ctx_row) + [
            pl.BlockSpec((1, tm, k), lambda i, j: (i, j, 0)),
            pl.BlockSpec((k, d), lambda i, j: (0, 0))],
        out_specs=pl.BlockSpec((1, tm, d), lambda i, j: (i, j, 0)),
        out_shape=jax.ShapeDtypeStruct((b, s, d), F32),
        compiler_params=_cparams(("parallel", "parallel")),
        name="res_proj",
    )(x, mods, mods, y, w)


def _ffn_kernel(x_ref, modl_ref, modc_ref, g_ref, wg_ref, wu_ref, wd_ref, o_ref, h_ref, acc_ref, *, ctx_len, tm,
                n_f):
    f = pl.program_id(2)
    pos0 = pl.program_id(1) * tm

    @pl.when(f == 0)
    def _():
        h_ref[...] = _bf(_modulate(x_ref[0], g_ref[...], modl_ref, modc_ref, 3, pos0, ctx_len))
        acc_ref[...] = jnp.zeros_like(acc_ref)

    h = h_ref[...]
    act = _silu(jnp.dot(h, wg_ref[...], preferred_element_type=F32)) * jnp.dot(h, wu_ref[...],
                                                                               preferred_element_type=F32)
    acc_ref[...] += _dot(act, wd_ref[...])

    @pl.when(f == n_f - 1)
    def _():
        o_ref[0] = x_ref[0] + _gate_vec(modl_ref, modc_ref, 5, pos0, tm, ctx_len) * acc_ref[...]


def _ffn(x, mods, ctx_row, g, wg, wu, wd, ctx_len, tm, tf):
    b, s, d = x.shape
    ff = wg.shape[1]
    n_f = ff // tf
    return pl.pallas_call(
        functools.partial(_ffn_kernel, ctx_len=ctx_len, tm=tm, n_f=n_f),
        grid=(b, s // tm, n_f),
        in_specs=[pl.BlockSpec((1, tm, d), lambda i, j, f: (i, j, 0))] + _mod_specs(d, ctx_row) + [
            pl.BlockSpec((1, d), lambda i, j, f: (0, 0)),
            pl.BlockSpec((d, tf), lambda i, j, f: (0, f)),
            pl.BlockSpec((d, tf), lambda i, j, f: (0, f)),
            pl.BlockSpec((tf, d), lambda i, j, f: (f, 0))],
        out_specs=pl.BlockSpec((1, tm, d), lambda i, j, f: (i, j, 0)),
        out_shape=jax.ShapeDtypeStruct((b, s, d), F32),
        scratch_shapes=[pltpu.VMEM((tm, d), BF16), pltpu.VMEM((tm, d), F32)],
        compiler_params=_cparams(("parallel", "parallel", "arbitrary")),
        name="ffn",
    )(x, mods, mods, g.reshape(1, d), wg, wu, wd)


def _attn_kernel(sink_ref, q_ref, kc_ref, vc_ref, kp_ref, kq_ref, kn_ref, vp_ref, vq_ref, vn_ref, o_ref, *,
                 ctx_len, n_lat):
    i = pl.program_id(1)
    nk = ctx_len + 3 * ATT_BLOCK
    k_all = jnp.concatenate([kc_ref[0], kp_ref[0], kq_ref[0], kn_ref[0]], axis=0)
    v_all = jnp.concatenate([vc_ref[0], vp_ref[0], vq_ref[0], vn_ref[0]], axis=0)
    row = lax.broadcasted_iota(jnp.int32, (ATT_BLOCK, nk), 0)
    col = lax.broadcasted_iota(jnp.int32, (ATT_BLOCK, nk), 1)
    rel = col - (ctx_len + ATT_BLOCK)
    kabs = i * ATT_BLOCK + rel
    band = (jnp.abs(row - rel) <= WINDOW) & (kabs >= 0) & (kabs < n_lat)
    bias = jnp.where((col < ctx_len) | band, 0.0, NEG_INF)
    bias = jnp.concatenate([bias] * GQ, axis=0)
    grp = lax.broadcasted_iota(jnp.int32, (GQ * ATT_BLOCK, 1), 0) // ATT_BLOCK
    scale = HD ** -0.5
    for hk in range(HKV):
        kh = _bf(k_all[:, hk * HD:(hk + 1) * HD])
        vh = _bf(v_all[:, hk * HD:(hk + 1) * HD])
        heads = [hk * GQ + g for g in range(GQ)]
        q = jnp.concatenate([q_ref[0, :, hq * HD:(hq + 1) * HD] for hq in heads], axis=0) * scale
        sk = jnp.zeros((GQ * ATT_BLOCK, 1), F32)
        for g, hq in enumerate(heads):
            sk = jnp.where(grp == g, sink_ref[hq], sk)
        s = _dot_nt(q, kh) + bias
        m = jnp.maximum(jnp.max(s, axis=-1, keepdims=True), sk)
        p = jnp.exp(s - m)
        den = jnp.sum(p, axis=-1, keepdims=True) + jnp.exp(sk - m)
        o = jnp.dot(_bf(p), vh, preferred_element_type=F32) / den
        for g, hq in enumerate(heads):
            o_ref[0, :, hq * HD:(hq + 1) * HD] = o[g * ATT_BLOCK:(g + 1) * ATT_BLOCK]


def _attention(qkv, sink, ctx_len):
    b, s, _ = qkv.shape
    n_lat = s - ctx_len
    nb = n_lat // ATT_BLOCK
    cb = ctx_len // ATT_BLOCK
    kcol = Q_COLS // KV_COLS
    vcol = kcol + 1

    def band(colblk, shift):
        return pl.BlockSpec((1, ATT_BLOCK, KV_COLS),
                            lambda bi, i: (bi, cb + jnp.clip(i + shift, 0, nb - 1), colblk))

    return pl.pallas_call(
        functools.partial(_attn_kernel, ctx_len=ctx_len, n_lat=n_lat),
        grid=(b, nb),
        in_specs=[pl.BlockSpec(memory_space=pltpu.SMEM),
                  pl.BlockSpec((1, ATT_BLOCK, Q_COLS), lambda bi, i: (bi, cb + i, 0)),
                  pl.BlockSpec((1, ctx_len, KV_COLS), lambda bi, i: (bi, 0, kcol)),
                  pl.BlockSpec((1, ctx_len, KV_COLS), lambda bi, i: (bi, 0, vcol)),
                  band(kcol, -1), band(kcol, 0), band(kcol, 1),
                  band(vcol, -1), band(vcol, 0), band(vcol, 1)],
        out_specs=pl.BlockSpec((1, ATT_BLOCK, Q_COLS), lambda bi, i: (bi, i, 0)),
        out_shape=jax.ShapeDtypeStruct((b, n_lat, Q_COLS), F32),
        compiler_params=_cparams(("parallel", "parallel")),
        name="attention",
    )(sink, qkv, qkv, qkv, qkv, qkv, qkv, qkv, qkv, qkv)


def _router_kernel(x_ref, modl_ref, modc_ref, g_ref, w_ref, b_ref, o_ref, e_ref, f_ref, *, tm):
    f = _modulate(x_ref[0], g_ref[...], modl_ref, modc_ref, 3, 0, 0)
    f_ref[0] = f
    logits = _dot_f32(f, w_ref[...]) + b_ref[...]
    lane = lax.broadcasted_iota(jnp.int32, logits.shape, 1).astype(F32)
    logits = jnp.where(lane < N_EXPERTS, logits, NEG_INF)
    ex = jnp.exp(logits - jnp.max(logits, axis=-1, keepdims=True))
    probs = ex / jnp.sum(ex, axis=-1, keepdims=True)
    p1 = jnp.max(probs, axis=-1, keepdims=True)
    i1 = jnp.min(jnp.where(probs == p1, lane, float(LANES)), axis=-1, keepdims=True)
    rest = jnp.where(lane == i1, -1.0, probs)
    p2 = jnp.max(rest, axis=-1, keepdims=True)
    i2 = jnp.min(jnp.where(rest == p2, lane, float(LANES)), axis=-1, keepdims=True)
    tot = p1 + p2
    o_ref[0] = jnp.where(lane == 0.0, p1 / tot, jnp.where(lane == 1.0, p2 / tot, 0.0))
    e_ref[0] = jnp.where(lane == 0.0, i1, jnp.where(lane == 1.0, i2, 0.0)).astype(jnp.int32)


def _router(x, mods, g, w, bias):
    b, s, d = x.shape
    tm = TOK_TILE
    wpad = jnp.zeros((d, LANES), F32).at[:, :N_EXPERTS].set(w)
    bpad = jnp.zeros((1, LANES), F32).at[0, :N_EXPERTS].set(bias)
    lane_spec = pl.BlockSpec((1, tm, LANES), lambda i, j: (i, j, 0))
    return pl.pallas_call(
        functools.partial(_router_kernel, tm=tm),
        grid=(b, s // tm),
        in_specs=[pl.BlockSpec((1, tm, d), lambda i, j: (i, j, 0))] + _mod_specs(d, 0) + [
            pl.BlockSpec((1, d), lambda i, j: (0, 0)),
            pl.BlockSpec((d, LANES), lambda i, j: (0, 0)),
            pl.BlockSpec((1, LANES), lambda i, j: (0, 0))],
        out_specs=[lane_spec, lane_spec, pl.BlockSpec((1, tm, d), lambda i, j: (i, j, 0))],
        out_shape=[jax.ShapeDtypeStruct((b, s, LANES), F32), jax.ShapeDtypeStruct((b, s, LANES), jnp.int32),
                   jax.ShapeDtypeStruct((b, s, d), F32)],
        compiler_params=_cparams(("parallel", "parallel")),
        name="router",
    )(x, mods, mods, g.reshape(1, d), wpad, bpad)


MOE_ROWS = 256


def _row_scatter_kernel(dest_ref, pad_ref, f_ref, xs_hbm, zero_ref, sem, *, tm, n_pad):
    base = pl.program_id(0) * tm

    @pl.when(pl.program_id(0) == 0)
    def _():
        zero_ref[...] = jnp.zeros_like(zero_ref)

        def zero_row(r, carry):
            pltpu.make_async_copy(zero_ref, xs_hbm.at[pl.ds(pad_ref[r], 1)], sem).start()
            return carry

        lax.fori_loop(0, n_pad, zero_row, 0, unroll=8)
        for _ in range(n_pad // tm):
            pltpu.make_async_copy(f_ref, xs_hbm.at[pl.ds(0, tm)], sem).wait()

    def issue(r, carry):
        slot = 2 * (base + r)
        pltpu.make_async_copy(f_ref.at[pl.ds(r, 1)], xs_hbm.at[pl.ds(dest_ref[slot], 1)], sem).start()
        pltpu.make_async_copy(f_ref.at[pl.ds(r, 1)], xs_hbm.at[pl.ds(dest_ref[slot + 1], 1)], sem).start()
        return carry

    lax.fori_loop(0, tm, issue, 0, unroll=8)
    for _ in range(2):
        pltpu.make_async_copy(f_ref, xs_hbm.at[pl.ds(0, tm)], sem).wait()


def _row_scatter(f, dest, pad_rows, n_rows):
    n_tok, d = f.shape
    tm = TOK_TILE
    n_pad = pad_rows.shape[0]
    assert n_pad % tm == 0 and 2 * n_tok + n_pad == n_rows
    return pl.pallas_call(
        functools.partial(_row_scatter_kernel, tm=tm, n_pad=n_pad),
        grid_spec=pltpu.PrefetchScalarGridSpec(
            num_scalar_prefetch=2,
            grid=(n_tok // tm,),
            in_specs=[pl.BlockSpec((tm, d), lambda i, dr, pr: (i, 0))],
            out_specs=pl.BlockSpec(memory_space=pl.ANY),
            scratch_shapes=[pltpu.VMEM((1, d), f.dtype), pltpu.SemaphoreType.DMA(())]),
        out_shape=jax.ShapeDtypeStruct((n_rows, d), f.dtype),
        compiler_params=_cparams(("arbitrary",)),
        name="moe_scatter",
    )(dest, pad_rows, f)


def _expert_kernel(be_ref, x_ref, wg_ref, wu_ref, wd_ref, o_ref):
    h = _bf(x_ref[...])
    act = _silu(jnp.dot(h, wg_ref[0], preferred_element_type=F32)) * jnp.dot(h, wu_ref[0],
                                                                              preferred_element_type=F32)
    o_ref[...] = _dot(act, wd_ref[0])


def _experts(xs, block_e, wg, wu, wd):
    n_rows, d = xs.shape
    ff = wg.shape[2]
    rows = MOE_ROWS
    return pl.pallas_call(
        _expert_kernel,
        grid_spec=pltpu.PrefetchScalarGridSpec(
            num_scalar_prefetch=1,
            grid=(n_rows // rows,),
            in_specs=[pl.BlockSpec((rows, d), lambda i, be: (i, 0)),
                      pl.BlockSpec((1, d, ff), lambda i, be: (be[i], 0, 0)),
                      pl.BlockSpec((1, d, ff), lambda i, be: (be[i], 0, 0)),
                      pl.BlockSpec((1, ff, d), lambda i, be: (be[i], 0, 0))],
            out_specs=pl.BlockSpec((rows, d), lambda i, be: (i, 0))),
        out_shape=jax.ShapeDtypeStruct((n_rows, d), F32),
        compiler_params=_cparams(("arbitrary",)),
        name="moe_experts",
    )(block_e, xs, wg, wu, wd)


def _combine_kernel(dest_ref, x_ref, modl_ref, modc_ref, w_ref, gfin_ref, ys_hbm, o_ref, y1_ref, y2_ref, sem, *,
                    tm, tiles_per_row):
    tile = pl.program_id(0) * tiles_per_row + pl.program_id(1)
    n_tiles = pl.num_programs(0) * tiles_per_row
    buf = tile % 2

    def fetch(t, b):
        def issue(r, carry):
            slot = 2 * (t * tm + r)
            pltpu.make_async_copy(ys_hbm.at[pl.ds(dest_ref[slot], 1)], y1_ref.at[b, pl.ds(r, 1)], sem.at[b]).start()
            pltpu.make_async_copy(ys_hbm.at[pl.ds(dest_ref[slot + 1], 1)], y2_ref.at[b, pl.ds(r, 1)],
                                  sem.at[b]).start()
            return carry

        lax.fori_loop(0, tm, issue, 0, unroll=8)

    @pl.when(tile == 0)
    def _():
        fetch(0, 0)

    @pl.when(tile + 1 < n_tiles)
    def _():
        fetch(tile + 1, 1 - buf)

    pltpu.make_async_copy(ys_hbm.at[pl.ds(0, tm)], y1_ref.at[buf], sem.at[buf]).wait()
    pltpu.make_async_copy(ys_hbm.at[pl.ds(0, tm)], y2_ref.at[buf], sem.at[buf]).wait()
    w = w_ref[0]
    moe = w[:, 0:1] * y1_ref[buf] + w[:, 1:2] * y2_ref[buf]
    y = x_ref[0] + _gate_vec(modl_ref, modc_ref, 5, 0, tm, 0) * moe
    o_ref[0] = y * lax.rsqrt(jnp.mean(y * y, axis=-1, keepdims=True) + NORM_EPS) * gfin_ref[...]


def _combine(x, mods, tokw, dest, ys, final_g):
    b, s, d = x.shape
    tm = TOK_TILE
    tiles = s // tm
    return pl.pallas_call(
        functools.partial(_combine_kernel, tm=tm, tiles_per_row=tiles),
        grid_spec=pltpu.PrefetchScalarGridSpec(
            num_scalar_prefetch=1,
            grid=(b, tiles),
            in_specs=[pl.BlockSpec((1, tm, d), lambda i, j, dr: (i, j, 0)),
                      pl.BlockSpec((1, N_MOD, d), lambda i, j, dr: (i, 0, 0)),
                      pl.BlockSpec((1, N_MOD, d), lambda i, j, dr: (0, 0, 0)),
                      pl.BlockSpec((1, tm, LANES), lambda i, j, dr: (i, j, 0)),
                      pl.BlockSpec((1, d), lambda i, j, dr: (0, 0)),
                      pl.BlockSpec(memory_space=pl.ANY)],
            out_specs=pl.BlockSpec((1, tm, d), lambda i, j, dr: (i, j, 0)),
            scratch_shapes=[pltpu.VMEM((2, tm, d), F32), pltpu.VMEM((2, tm, d), F32),
                            pltpu.SemaphoreType.DMA((2,))]),
        out_shape=jax.ShapeDtypeStruct((b, s, d), F32),
        compiler_params=_cparams(("arbitrary", "arbitrary")),
        name="moe_combine",
    )(dest, x, mods, mods, tokw, final_g.reshape(1, d), ys)


def _moe_routing(eidx):
    m = eidx.shape[0]
    oh = (eidx[:, None] == jnp.arange(N_EXPERTS, dtype=jnp.int32)[None, :]).astype(jnp.int32)
    csum = jnp.cumsum(oh, axis=0)
    rank = jnp.sum((csum - oh) * oh, axis=1)
    counts = csum[-1]
    padded = (counts + MOE_ROWS - 1) // MOE_ROWS * MOE_ROWS
    pad_end = jnp.cumsum(padded)
    dest = jnp.sum(oh * (pad_end - padded)[None, :], axis=1) + rank
    n_blocks = m // MOE_ROWS + N_EXPERTS
    starts = jnp.arange(n_blocks, dtype=jnp.int32) * MOE_ROWS
    block_e = jnp.minimum(jnp.sum((starts[:, None] >= pad_end[None, :]).astype(jnp.int32), axis=1), N_EXPERTS - 1)
    pad_cnt = padded - counts
    pad_cum = jnp.cumsum(pad_cnt)
    idx = jnp.arange(N_EXPERTS * MOE_ROWS, dtype=jnp.int32)
    owner = jnp.sum((idx[:, None] >= pad_cum[None, :]).astype(jnp.int32), axis=1)
    own = (jnp.minimum(owner, N_EXPERTS - 1)[:, None] == jnp.arange(N_EXPERTS)[None, :]).astype(jnp.int32)
    in_expert = jnp.sum(own * (pad_end - pad_cnt)[None, :], axis=1) + idx - jnp.sum(
        own * (pad_cum - pad_cnt)[None, :], axis=1)
    pad_rows = jnp.where(owner < N_EXPERTS, in_expert, pad_end[-1] + idx - pad_cum[-1])
    return dest.astype(jnp.int32), block_e, pad_rows.astype(jnp.int32), n_blocks * MOE_ROWS


def _rope_tables(n_lat, ctx_len):
    rows = n_lat // GRID_W
    row = jnp.repeat(jnp.arange(rows, dtype=F32), GRID_W)
    col = jnp.tile(jnp.arange(GRID_W, dtype=F32), rows)
    inv = ROPE_BASE ** (-jnp.arange(0, AX_DIM, 2, dtype=F32) / AX_DIM)
    ar, ac = row[:, None] * inv, col[:, None] * inv
    cos = jnp.concatenate([jnp.cos(ar), jnp.cos(ar), jnp.cos(ac), jnp.cos(ac)], axis=-1)
    sin = jnp.concatenate([-jnp.sin(ar), jnp.sin(ar), -jnp.sin(ac), jnp.sin(ac)], axis=-1)
    n_heads = ROPE_COLS // HD
    cos = jnp.concatenate([jnp.ones((ctx_len, HD), F32), cos], axis=0)
    sin = jnp.concatenate([jnp.zeros((ctx_len, HD), F32), sin], axis=0)
    half = AX_DIM // 2
    j = jnp.arange(HD)
    partner = jnp.where((j % AX_DIM) < half, j + half, j - half)
    perm = (jnp.arange(n_heads)[:, None] * HD + partner[None, :]).reshape(-1)
    return jnp.tile(cos, (1, n_heads)), jnp.tile(sin, (1, n_heads)), perm


def kernel(x, c, ctx, c_ctx, mod_w, mod_b, norm_mix, norm_ffn, norm_final, rec_w_in, rec_w_out, rwkv_mu, rwkv_w0, rwkv_w_up, rwkv_a0, rwkv_a_up, rwkv_g_up, rwkv_k_k, rwkv_k_a, rwkv_r_k, rwkv_ln_w, rwkv_ln_b, hgrn_lb, hgrn_norm, ffn_w_gate, ffn_w_up, ffn_w_down, att_w_in, att_w_out, att_sink, moe_router, moe_router_b, moe_w_gate, moe_w_up, moe_w_down):
    bsz, n_lat, d = x.shape
    ctx_len = ctx.shape[1]
    xcat = jnp.concatenate([ctx, x], axis=1)

    n_rows = -(-(bsz + 1) // 8) * 8
    cvec = jnp.zeros((n_rows, d), F32).at[:bsz].set(c).at[bsz].set(c_ctx)
    mods = [_adaln(cvec, mod_w[l], mod_b[l]).reshape(n_rows, N_MOD, d) for l in range(2)]

    p_rwkv, p_hgrn = _proj(xcat, mods[0], bsz, norm_mix[0], _bf(rec_w_in[0]), (RWKV_COLS, HGRN_COLS), ctx_len)
    ops, vv, g_bonus, gam = _rwkv_features(p_rwkv, rwkv_mu[0], rwkv_w0[0], rwkv_w_up[0], rwkv_a0[0], rwkv_a_up[0],
                                           rwkv_g_up[0], rwkv_k_k[0], rwkv_k_a[0], rwkv_r_k[0].reshape(-1), ctx_len)
    oa = _rwkv_scan(ops, vv, gam, ctx_len)
    lb = jnp.cumsum(jax.nn.softmax(hgrn_lb.astype(F32), axis=1), axis=1)[:, 0].reshape(2, 1, B_W)
    ob = _hgrn_scan(p_hgrn, lb, ctx_len)
    xcat = _rec_out(xcat, mods[0], bsz, oa, g_bonus, ob, p_hgrn, rwkv_ln_w[0], rwkv_ln_b[0], hgrn_norm[0],
                    _bf(rec_w_out[0]), ctx_len)
    xcat = _ffn(xcat, mods[0], bsz, norm_ffn[0], _bf(ffn_w_gate[0]), _bf(ffn_w_up[0]), _bf(ffn_w_down[0]), ctx_len,
                tm=384, tf=2816)

    cos, sin, perm = _rope_tables(n_lat, ctx_len)
    w_att = att_w_in[0]
    (qkv,) = _proj(xcat, mods[1], bsz, norm_mix[1], _bf(w_att), (ATT_COLS,), ctx_len,
                   rope=(_bf(w_att[:, perm]), cos, sin))
    att = _attention(qkv, att_sink[0], ctx_len)
    x_lat = _res_proj(xcat, mods[1], 0, att, _bf(att_w_out[0]), 0, x_skip=ctx_len)
    tokw, eidx, f_lat = _router(x_lat, mods[1], norm_ffn[1], moe_router[0], moe_router_b[0])
    dest, block_e, pad_rows, n_rows = _moe_routing(eidx[..., :2].reshape(-1))
    xs = _row_scatter(f_lat.reshape(bsz * n_lat, d), dest, pad_rows, n_rows)
    ys = _experts(xs, block_e, _bf(moe_w_gate[0]), _bf(moe_w_up[0]), _bf(moe_w_down[0]))
    return _combine(x_lat, mods[1], tokw, dest, ys, norm_final)
```

```python
import functools

import jax
import jax.numpy as jnp
from jax import lax
from jax.experimental import pallas as pl
from jax.experimental.pallas import tpu as pltpu

F32 = jnp.float32
BF16 = jnp.bfloat16

N_MOD = 6
NORM_EPS = 1e-6
NEG_INF = -1e30

A_HEADS = 8
A_HD = 64
A_W = A_HEADS * A_HD
DECAY_LORA = 64
AAA_LORA = 64
GATE_LORA = 128
RWKV_COLS = 3 * A_W + DECAY_LORA + AAA_LORA + GATE_LORA
GN_EPS = 64e-5

B_HEADS = 4
B_DK = 128
B_W = B_HEADS * B_DK
HGRN_COLS = 5 * B_W

HQ = 16
HKV = 4
GQ = HQ // HKV
HD = 64
WINDOW = 128
ATT_BLOCK = 128
AX_DIM = HD // 2
ROPE_BASE = 10000.0
GRID_W = 64
Q_COLS = HQ * HD
KV_COLS = HKV * HD
ROPE_COLS = Q_COLS + KV_COLS
ATT_COLS = Q_COLS + 2 * KV_COLS

N_EXPERTS = 8
LANES = 128
CHUNK = 64
SCAN_BLOCK = 256
MXU_WIDTH = 256
RWKV_GROUP = MXU_WIDTH // A_HD
TOK_TILE = 256
VMEM_LIMIT = 56 * 1024 * 1024

OP_KAP, OP_RT, OP_KBAR, OP_BBAR, OP_KGAM, OP_BGAM = range(6)
N_OPS = 6


def _cparams(sem):
    return pltpu.CompilerParams(dimension_semantics=sem, vmem_limit_bytes=VMEM_LIMIT)


def _bf(x):
    return x.astype(BF16)


def _dot(a, b):
    return jnp.dot(_bf(a), _bf(b), preferred_element_type=F32)


def _dot_nt(a, b):
    return lax.dot_general(_bf(a), _bf(b), (((1,), (1,)), ((), ())), preferred_element_type=F32)


def _dot_tn(a, b):
    return lax.dot_general(_bf(a), _bf(b), (((0,), (0,)), ((), ())), preferred_element_type=F32)


def _dot_f32(a, b):
    a_hi, b_hi = _bf(a), _bf(b)
    a_lo, b_lo = _bf(a - a_hi.astype(F32)), _bf(b - b_hi.astype(F32))
    acc = jnp.dot(a_hi, b_hi, preferred_element_type=F32)
    acc += jnp.dot(a_hi, b_lo, preferred_element_type=F32)
    acc += jnp.dot(a_lo, b_hi, preferred_element_type=F32)
    return acc


def _sigmoid(x):
    return 1.0 / (1.0 + jnp.exp(-x))


def _silu(x):
    return x * _sigmoid(x)


def _segsum(x, ones_bd):
    hi = _bf(x)
    lo = _bf(x - hi.astype(F32))
    return jnp.dot(hi, ones_bd, preferred_element_type=F32) + jnp.dot(lo, ones_bd, preferred_element_type=F32)


def _dot_split(a, x):
    hi = _bf(x)
    r1 = x - hi.astype(F32)
    mid = _bf(r1)
    lo = _bf(r1 - mid.astype(F32))
    acc = jnp.dot(a, hi, preferred_element_type=F32)
    acc += jnp.dot(a, mid, preferred_element_type=F32)
    acc += jnp.dot(a, lo, preferred_element_type=F32)
    return acc


def _block_ones(width, seg):
    i = jnp.arange(width) // seg
    return (i[:, None] == i[None, :]).astype(BF16)


def _modulate(x, g, modl_ref, modc_ref, row, pos0, ctx_len):
    y = x * lax.rsqrt(jnp.mean(x * x, axis=-1, keepdims=True) + NORM_EPS) * g
    pos = pos0 + lax.broadcasted_iota(jnp.int32, (x.shape[0], 1), 0)
    is_ctx = pos < ctx_len
    shift = jnp.where(is_ctx, modc_ref[0, row:row + 1, :], modl_ref[0, row:row + 1, :])
    scale = jnp.where(is_ctx, modc_ref[0, row + 1:row + 2, :], modl_ref[0, row + 1:row + 2, :])
    return y * (1.0 + scale) + shift


def _gate_vec(modl_ref, modc_ref, row, pos0, n, ctx_len):
    pos = pos0 + lax.broadcasted_iota(jnp.int32, (n, 1), 0)
    return jnp.where(pos < ctx_len, modc_ref[0, row:row + 1, :], modl_ref[0, row:row + 1, :])


def _mod_specs(d, ctx_row, batch_axis=0):
    def lat_map(*idx):
        return (idx[batch_axis], 0, 0)

    def ctx_map(*idx):
        return (ctx_row, 0, 0)

    return [pl.BlockSpec((1, N_MOD, d), lat_map), pl.BlockSpec((1, N_MOD, d), ctx_map)]


def _adaln_kernel(c_ref, w_ref, b_ref, o_ref):
    o_ref[...] = _dot(_silu(c_ref[...]), w_ref[...]) + b_ref[...]


def _adaln(cvec, w, b):
    r, d = cvec.shape
    n = w.shape[1]
    tn = 1024
    return pl.pallas_call(
        _adaln_kernel,
        grid=(n // tn,),
        in_specs=[pl.BlockSpec((r, d), lambda j: (0, 0)),
                  pl.BlockSpec((d, tn), lambda j: (0, j)),
                  pl.BlockSpec((1, tn), lambda j: (0, j))],
        out_specs=pl.BlockSpec((r, tn), lambda j: (0, j)),
        out_shape=jax.ShapeDtypeStruct((r, n), F32),
        compiler_params=_cparams(("parallel",)),
        name="adaln",
    )(cvec, w, b.reshape(1, n))


def _proj_kernel(x_ref, modl_ref, modc_ref, g_ref, w_ref, *rest, ctx_len, tm, splits, rope_cols):
    if rope_cols:
        wrot_ref, cos_ref, sin_ref = rest[:3]
        outs = rest[3:]
    else:
        outs = rest
    pos0 = pl.program_id(1) * tm
    h = _bf(_modulate(x_ref[0], g_ref[...], modl_ref, modc_ref, 0, pos0, ctx_len))
    y = jnp.dot(h, w_ref[...], preferred_element_type=F32)
    if rope_cols:
        yr = jnp.dot(h, wrot_ref[...], preferred_element_type=F32)
        roped = y[:, :rope_cols] * cos_ref[...] + yr * sin_ref[...]
        outs[0][0, :, :rope_cols] = roped
        outs[0][0, :, rope_cols:] = y[:, rope_cols:]
    else:
        lo = 0
        for o_ref, width in zip(outs, splits):
            o_ref[0] = y[:, lo:lo + width]
            lo += width


def _proj(xcat, mods, ctx_row, g, w, splits, ctx_len, rope=None):
    b, s, d = xcat.shape
    n = w.shape[1]
    tm = TOK_TILE
    in_specs = [pl.BlockSpec((1, tm, d), lambda i, j: (i, j, 0))] + _mod_specs(d, ctx_row) + [
        pl.BlockSpec((1, d), lambda i, j: (0, 0)),
        pl.BlockSpec((d, n), lambda i, j: (0, 0))]
    args = [xcat, mods, mods, g.reshape(1, d), w]
    rope_cols = 0
    if rope is not None:
        wrot, cos, sin = rope
        rope_cols = wrot.shape[1]
        in_specs += [pl.BlockSpec((d, rope_cols), lambda i, j: (0, 0)),
                     pl.BlockSpec((tm, rope_cols), lambda i, j: (j, 0)),
                     pl.BlockSpec((tm, rope_cols), lambda i, j: (j, 0))]
        args += [wrot, cos, sin]
    out_shape = [jax.ShapeDtypeStruct((b, s, width), F32) for width in splits]
    out_specs = [pl.BlockSpec((1, tm, width), lambda i, j: (i, j, 0)) for width in splits]
    return pl.pallas_call(
        functools.partial(_proj_kernel, ctx_len=ctx_len, tm=tm, splits=tuple(splits), rope_cols=rope_cols),
        grid=(b, s // tm),
        in_specs=in_specs,
        out_specs=out_specs,
        out_shape=out_shape,
        compiler_params=_cparams(("parallel", "parallel")),
        name="proj_rope" if rope_cols else "proj",
    )(*args)


def _softplus(x):
    return jnp.maximum(x, 0.0) + jnp.log(1.0 + jnp.exp(-jnp.abs(x)))


def _rwkv_feat_kernel(p_ref, pp_ref, pn_ref, mu_ref, w0_ref, wup_ref, a0_ref, aup_ref, gup_ref,
                      kk_ref, ka_ref, rk_ref, ones_ref, ops_ref, v_ref, gb_ref, gam_ref, *, ctx_len, seq_len, tm):
    pos0 = pl.program_id(1) * tm
    p = p_ref[0]
    rows = lax.broadcasted_iota(jnp.int32, (tm, 1), 0)
    starts_seq = (pos0 == 0) | (pos0 == ctx_len)
    ends_seq = (pos0 + tm == ctx_len) | (pos0 + tm == seq_len)
    prev_halo = jnp.where(starts_seq, 0.0, pp_ref[0, 7:8, :])
    next_halo = jnp.where(ends_seq, 0.0, pn_ref[0, 0:1, :])
    prev = jnp.where(rows == 0, prev_halo, pltpu.roll(p, 1, 0))
    nxt = jnp.where(rows == tm - 1, next_halo, pltpu.roll(p, tm - 1, 0))
    p = p + mu_ref[...] * (0.5 * (prev + nxt) - p)

    r = p[:, 0:A_W]
    k = p[:, A_W:2 * A_W]
    v = p[:, 2 * A_W:3 * A_W]
    lo = 3 * A_W
    wd = p[:, lo:lo + DECAY_LORA]
    ad = p[:, lo + DECAY_LORA:lo + DECAY_LORA + AAA_LORA]
    gd = p[:, lo + DECAY_LORA + AAA_LORA:lo + DECAY_LORA + AAA_LORA + GATE_LORA]

    tw = jnp.tanh(wd)
    a = _sigmoid(a0_ref[...] + _dot(ad, aup_ref[...]))
    ones_bd = ones_ref[...]
    kk = k * kk_ref[...]
    kk = kk / jnp.maximum(jnp.sqrt(_segsum(kk * kk, ones_bd)), 1e-12)
    k = k * (1.0 + (a - 1.0) * ka_ref[...])
    b = kk * a
    v_ref[0] = _bf(v)
    gb_ref[0, :, :A_W] = _dot(_sigmoid(gd), gup_ref[...])
    gb_ref[0, :, A_W:] = _segsum(r * k * rk_ref[...], ones_bd) * v

    trow = lax.broadcasted_iota(jnp.int32, (tm, tm), 0)
    tcol = lax.broadcasted_iota(jnp.int32, (tm, tm), 1)
    same = (trow // CHUNK) == (tcol // CHUNK)
    n_chunks = tm // CHUNK
    for d in range(2):
        w_log = -_softplus(-(w0_ref[d:d + 1, :] + _dot(tw, wup_ref[d]))) - 0.5
        lw = -jnp.exp(w_log)
        before = (tcol <= trow) if d == 0 else (tcol >= trow)
        g_incl = _dot_split(jnp.where(same & before, 1.0, 0.0).astype(BF16), lw)
        last = [c * CHUNK + (CHUNK - 1 if d == 0 else 0) for c in range(n_chunks)]
        g_end = jnp.concatenate([g_incl[t:t + 1] for t in last], axis=0)
        g_tot = jnp.concatenate([jnp.broadcast_to(g_incl[t:t + 1], (CHUNK, A_W)) for t in last], axis=0)
        e_ng = jnp.exp(-g_incl)
        e_tail = jnp.exp(g_tot - g_incl)
        operands = {OP_KAP: kk * jnp.exp(g_incl - lw),
                    OP_RT: r * jnp.exp(g_incl),
                    OP_KBAR: k * e_ng, OP_BBAR: b * e_ng,
                    OP_KGAM: k * e_tail, OP_BGAM: b * e_tail}
        for sec, val in operands.items():
            ops_ref[0, d, :, sec * A_W:(sec + 1) * A_W] = _bf(val)
        gam_ref[0, d, 0] = jnp.exp(g_end)


def _rwkv_features(p, mu, w0, w_up, a0, a_up, g_up, k_k, k_a, r_k, ctx_len):
    b, s, c = p.shape
    tm = SCAN_BLOCK
    nb8 = s // 8
    row = lambda a: a.reshape(1, -1)
    full = lambda a: pl.BlockSpec(a.shape, lambda i, j: (0,) * a.ndim)
    args = [row(mu), w0, w_up, row(a0), a_up, g_up, row(k_k), row(k_a), row(r_k), _block_ones(A_W, A_HD)]
    return pl.pallas_call(
        functools.partial(_rwkv_feat_kernel, ctx_len=ctx_len, seq_len=s, tm=tm),
        grid=(b, s // tm),
        in_specs=[pl.BlockSpec((1, tm, c), lambda i, j: (i, j, 0)),
                  pl.BlockSpec((1, 8, c), lambda i, j: (i, jnp.maximum(j * (tm // 8) - 1, 0), 0)),
                  pl.BlockSpec((1, 8, c), lambda i, j: (i, jnp.minimum((j + 1) * (tm // 8), nb8 - 1), 0)),
                  ] + [full(a) for a in args],
        out_specs=[pl.BlockSpec((1, 2, tm, N_OPS * A_W), lambda i, j: (i, 0, j, 0)),
                   pl.BlockSpec((1, tm, A_W), lambda i, j: (i, j, 0)),
                   pl.BlockSpec((1, tm, 2 * A_W), lambda i, j: (i, j, 0)),
                   pl.BlockSpec((1, 2, 1, tm // CHUNK, A_W), lambda i, j: (i, 0, j, 0, 0))],
        out_shape=[jax.ShapeDtypeStruct((b, 2, s, N_OPS * A_W), BF16),
                   jax.ShapeDtypeStruct((b, s, A_W), BF16),
                   jax.ShapeDtypeStruct((b, s, 2 * A_W), F32),
                   jax.ShapeDtypeStruct((b, 2, s // tm, tm // CHUNK, A_W), F32)],
        compiler_params=_cparams(("parallel", "parallel")),
        name="rwkv_features",
    )(p, p, p, *args)


def _chunk_order(d, j, n_ctx_chunks, n_chunks):
    back = jnp.where(j < n_ctx_chunks, n_ctx_chunks - 1 - j, n_chunks - 1 + n_ctx_chunks - j)
    return jnp.where(d == 0, j, back)


def _order_masks(rev):
    row = lax.broadcasted_iota(jnp.int32, (CHUNK, CHUNK), 0)
    col = lax.broadcasted_iota(jnp.int32, (CHUNK, CHUNK), 1)
    diff = jnp.where(rev, row - col, col - row)
    return diff < 0, diff <= 0, row == col


def _scan_step_coords(t, n_blocks, n_steps):
    return jnp.minimum(t, n_steps - 1), jnp.maximum(t - 1, 0)


def _rwkv_chunk_kernel(ops_ref, v_ref, gam_ref, o_ref, s_ref, tr_ref, uu_ref, ol_ref, ab_ref,
                       m2_ref, cc_ref, gm_ref, *, n_blocks, n_steps):
    step = pl.program_id(0)
    t_in, t_out = _scan_step_coords(step, n_blocks, n_steps)
    rev = (t_in // n_blocks) % 2 == 1
    rev_out = (t_out // n_blocks) % 2 == 1
    first_out = t_out % n_blocks == 0
    w_slot = step % 2
    r_slot = 1 - w_slot

    @pl.when(step == 0)
    def _():
        s_ref[...] = jnp.zeros_like(s_ref)
        for ref in (tr_ref, uu_ref, ol_ref, ab_ref, m2_ref, cc_ref, gm_ref):
            ref[1] = jnp.zeros(ref.shape[1:], ref.dtype)

    gw = RWKV_GROUP * A_HD
    groups = range(A_HEADS // RWKV_GROUP)
    gsl = [slice(g * gw, (g + 1) * gw) for g in groups]
    row = lax.broadcasted_iota(jnp.int32, (CHUNK, gw), 0)
    col = lax.broadcasted_iota(jnp.int32, (CHUNK, gw), 1) % A_HD
    diff = jnp.where(rev, row - col, col - row)
    strict_c = diff < 0
    incl_c = diff <= 0
    eye_c = jnp.where(diff == 0, 1.0, 0.0)
    bd_mask = (lax.broadcasted_iota(jnp.int32, (gw, gw), 0) // A_HD
               == lax.broadcasted_iota(jnp.int32, (gw, gw), 1) // A_HD)

    def bd(x):
        return jnp.where(bd_mask, jnp.concatenate([x] * RWKV_GROUP, axis=0), jnp.zeros((), x.dtype))

    def stack(x):
        return jnp.concatenate([x[:, h * A_HD:(h + 1) * A_HD] for h in range(RWKV_GROUP)], axis=0)

    n = SCAN_BLOCK // CHUNK
    rows = [pl.ds(pl.multiple_of(jnp.where(rev, (n - 1 - i) * CHUNK, i * CHUNK), CHUNK), CHUNK) for i in range(n)]
    rows_out = [pl.ds(pl.multiple_of(jnp.where(rev_out, (n - 1 - i) * CHUNK, i * CHUNK), CHUNK), CHUNK)
                for i in range(n)]
    units = [(i, g) for i in range(n) for g in groups]

    s = [jnp.where(first_out, 0.0, s_ref[g]) for g in groups]
    zr = {}
    pieces = []

    def advance(i):
        for g in groups:
            ui = i * len(groups) + g
            zr[i, g] = _dot_nt(tr_ref[r_slot, ui], bd(s[g]))
            s[g] = s[g] * gm_ref[r_slot, ui] - jnp.dot(_bf(s[g]), m2_ref[r_slot, ui],
                                                       preferred_element_type=F32) + cc_ref[r_slot, ui]

    def emit_out(i):
        for g in groups:
            ui = i * len(groups) + g
            z = zr[i, g][:CHUNK] + uu_ref[r_slot, ui]
            o_ref[0, 0, rows_out[i], gsl[g]] = (zr[i, g][CHUNK:] + ol_ref[r_slot, ui]
                                                - _dot(ab_ref[r_slot, ui], bd(z)))

    for i in range(n):
        pieces += [functools.partial(advance, i), functools.partial(emit_out, i)]

    def state_piece():
        if pieces:
            pieces.pop(0)()

    def operand(sec, u):
        i, g = u
        return ops_ref[0, 0, rows[i], sec * A_W + g * gw:sec * A_W + (g + 1) * gw]

    kap = {u: operand(OP_KAP, u) for u in units}
    rt = {u: operand(OP_RT, u) for u in units}
    kbar = {u: operand(OP_KBAR, u) for u in units}
    bbar = {u: operand(OP_BBAR, u) for u in units}
    kgam = {u: operand(OP_KGAM, u) for u in units}
    bgam = {u: operand(OP_BGAM, u) for u in units}
    v = {(i, g): v_ref[0, rows[i], gsl[g]] for i, g in units}
    chunk_of = [jnp.where(rev, n - 1 - i, i) for i in range(n)]
    gam = {(i, g): gam_ref[0, 0, 0, pl.ds(chunk_of[i], 1), gsl[g]] for i, g in units}

    x = {u: jnp.concatenate([kap[u], rt[u]], axis=0) for u in units}
    yb = {u: _dot_nt(x[u], bd(bbar[u])) for u in units}
    state_piece()
    a = {u: jnp.where(strict_c, -yb[u][:CHUNK], 0.0) for u in units}
    xs = {u: eye_c + a[u] for u in units}
    pw = {u: _dot(a[u], bd(a[u])) for u in units}
    state_piece()
    for _ in range(4):
        st = {u: _dot(jnp.concatenate([pw[u], xs[u]], axis=0), bd(pw[u])) for u in units}
        pw = {u: st[u][:CHUNK] for u in units}
        xs = {u: xs[u] + st[u][CHUNK:] for u in units}
        state_piece()
    t_inv = {u: xs[u] + _dot(xs[u], bd(pw[u])) for u in units}
    state_piece()
    yk = {u: _dot_nt(x[u], bd(kbar[u])) for u in units}
    ykm = {u: jnp.concatenate([jnp.where(strict_c, yk[u][:CHUNK], 0.0), jnp.where(incl_c, yk[u][CHUNK:], 0.0)],
                              axis=0) for u in units}
    wo = {u: _dot(ykm[u], bd(v[u])) for u in units}
    w1 = {u: wo[u][:CHUNK] for u in units}
    o_loc = {u: wo[u][CHUNK:] for u in units}
    state_piece()
    tk = {u: _dot(t_inv[u], bd(kap[u])) for u in units}
    uu = {u: _dot(t_inv[u], bd(w1[u])) for u in units}
    while pieces:
        state_piece()
    for g in groups:
        s_ref[g] = s[g]
    a_rb = {u: jnp.where(incl_c, yb[u][CHUNK:], 0.0) for u in units}
    tu = {u: _dot_tn(jnp.concatenate([stack(tk[u]), stack(uu[u])], axis=1), bd(bgam[u])) for u in units}
    bd_m2 = {u: _bf(bd(tu[u][:A_HD])) for u in units}
    cc = {u: _dot_tn(stack(v[u]), bd(kgam[u])) - tu[u][A_HD:] for u in units}
    for u in units:
        ui = u[0] * len(groups) + u[1]
        tr_ref[w_slot, ui] = jnp.concatenate([_bf(tk[u]), rt[u]], axis=0)
        uu_ref[w_slot, ui] = uu[u]
        ol_ref[w_slot, ui] = o_loc[u]
        ab_ref[w_slot, ui] = _bf(a_rb[u])
        m2_ref[w_slot, ui] = bd_m2[u]
        cc_ref[w_slot, ui] = cc[u]
        gm_ref[w_slot, ui] = gam[u]


def _rwkv_scan(ops, vv, gam, ctx_len):
    b, _, s, _ = ops.shape
    nc, ncc = s // SCAN_BLOCK, ctx_len // SCAN_BLOCK

    n_steps = b * 2 * nc

    def coords(flat):
        d = (flat // nc) % 2
        return flat // (2 * nc), d, _chunk_order(d, flat % nc, ncc, nc)

    def coords_in(t):
        return coords(_scan_step_coords(t, nc, n_steps)[0])

    def out_map(t):
        bi, d, blk = coords(_scan_step_coords(t, nc, n_steps)[1])
        return bi, d, blk, 0

    def ops_map(t):
        bi, d, blk = coords_in(t)
        return bi, d, blk, 0

    def v_map(t):
        bi, _, blk = coords_in(t)
        return bi, blk, 0

    def gam_map(t):
        bi, d, blk = coords_in(t)
        return bi, d, blk, 0, 0

    n_groups = A_HEADS // RWKV_GROUP
    n_chunks = SCAN_BLOCK // CHUNK
    n_units = n_chunks * n_groups
    gw = RWKV_GROUP * A_HD
    return pl.pallas_call(
        functools.partial(_rwkv_chunk_kernel, n_blocks=nc, n_steps=n_steps),
        grid=(n_steps + 1,),
        in_specs=[pl.BlockSpec((1, 1, SCAN_BLOCK, N_OPS * A_W), ops_map),
                  pl.BlockSpec((1, SCAN_BLOCK, A_W), v_map),
                  pl.BlockSpec((1, 1, 1, n_chunks, A_W), gam_map)],
        out_specs=pl.BlockSpec((1, 1, SCAN_BLOCK, A_W), out_map),
        out_shape=jax.ShapeDtypeStruct((b, 2, s, A_W), F32),
        scratch_shapes=[pltpu.VMEM((n_groups, A_HD, gw), F32),
                        pltpu.VMEM((2, n_units, 2 * CHUNK, gw), BF16),
                        pltpu.VMEM((2, n_units, CHUNK, gw), F32),
                        pltpu.VMEM((2, n_units, CHUNK, gw), F32),
                        pltpu.VMEM((2, n_units, CHUNK, gw), BF16),
                        pltpu.VMEM((2, n_units, gw, gw), BF16),
                        pltpu.VMEM((2, n_units, CHUNK, gw), F32),
                        pltpu.VMEM((2, n_units, 1, gw), F32)],
        compiler_params=_cparams(("arbitrary",)),
        name="rwkv_scan",
    )(ops, vv, gam)


def _hgrn_chunk_kernel(q_ref, i_ref, f_ref, lb_ref, o_ref, s_ref):
    rev = pl.program_id(1) == 1

    @pl.when(pl.program_id(2) == 0)
    def _():
        s_ref[...] = jnp.zeros_like(s_ref)

    _, incl, _ = _order_masks(rev)
    incl_bf = jnp.where(incl, 1.0, 0.0).astype(BF16)
    lb = lb_ref[0]
    heads = range(B_HEADS)
    sls = [slice(h * B_DK, (h + 1) * B_DK) for h in heads]

    n = SCAN_BLOCK // CHUNK
    rows = [pl.ds(pl.multiple_of(jnp.where(rev, (n - 1 - i) * CHUNK, i * CHUNK), CHUNK), CHUNK) for i in range(n)]
    units = [(i, h) for i in range(n) for h in heads]
    q_in, k_in, q_st, k_tail, gam, v = ({} for _ in range(6))
    for i in range(n):
        f = lb + (1.0 - lb) * _sigmoid(f_ref[0, rows[i], :])
        logf = jnp.log(f)
        kf = 1.0 - f
        g_incl = _dot_split(incl_bf, logf)
        g_tot = jnp.sum(logf, axis=0, keepdims=True)
        g_mid = g_incl[CHUNK // 2 - 1:CHUNK // 2, :]
        q = _silu(q_ref[0, rows[i], :])
        v_i = i_ref[0, rows[i], :]
        q_in_i = q * jnp.exp(g_incl - g_mid)
        k_in_i = kf * jnp.exp(g_mid - g_incl)
        q_st_i = q_in_i * jnp.exp(g_mid)
        k_tail_i = k_in_i * jnp.exp(g_tot - g_mid)
        for h in heads:
            q_in[i, h], k_in[i, h] = q_in_i[:, sls[h]], k_in_i[:, sls[h]]
            q_st[i, h], k_tail[i, h] = q_st_i[:, sls[h]], k_tail_i[:, sls[h]]
            gam[i, h] = jnp.exp(g_tot)[:, sls[h]]
            v[i, h] = v_i[:, sls[h]]
    att = {u: jnp.where(incl, _dot_nt(q_in[u], k_in[u]), 0.0) for u in units}
    o_loc = {u: _dot(att[u], v[u]) for u in units}
    kv = {u: _dot_tn(v[u], k_tail[u]) for u in units}

    s = [s_ref[h] for h in heads]
    for i in range(n):
        for h in heads:
            o_ref[0, 0, rows[i], sls[h]] = o_loc[i, h] + _dot_nt(q_st[i, h], s[h])
            s[h] = s[h] * gam[i, h] + kv[i, h]
    for h in heads:
        s_ref[h] = s[h]


def _hgrn_scan(p, lb, ctx_len):
    b, s, _ = p.shape
    nc, ncc = s // SCAN_BLOCK, ctx_len // SCAN_BLOCK

    def sec(idx):
        return pl.BlockSpec((1, SCAN_BLOCK, B_W), lambda i, d, j: (i, _chunk_order(d, j, ncc, nc), idx))

    return pl.pallas_call(
        _hgrn_chunk_kernel,
        grid=(b, 2, nc),
        in_specs=[sec(0), sec(1),
                  pl.BlockSpec((1, SCAN_BLOCK, B_W), lambda i, d, j: (i, _chunk_order(d, j, ncc, nc), 2 + d)),
                  pl.BlockSpec((1, 1, B_W), lambda i, d, j: (d, 0, 0))],
        out_specs=pl.BlockSpec((1, 1, SCAN_BLOCK, B_W), lambda i, d, j: (i, d, _chunk_order(d, j, ncc, nc), 0)),
        out_shape=jax.ShapeDtypeStruct((b, 2, s, B_W), F32),
        scratch_shapes=[pltpu.VMEM((B_HEADS, B_DK, B_DK), F32)],
        compiler_params=_cparams(("parallel", "parallel", "arbitrary")),
        name="hgrn_scan",
    )(p, p, p, lb)


def _rec_out_kernel(x_ref, modl_ref, modc_ref, oa_ref, g_ref, bonus_ref, ob_ref, gate_ref, lnw_ref, lnb_ref,
                    hgn_ref, ones_a_ref, ones_b_ref, w_ref, o_ref, *, ctx_len, tm):
    pos0 = pl.program_id(1) * tm
    oa = oa_ref[0, 0] + oa_ref[0, 1]
    ones_a = ones_a_ref[...]
    mean = _segsum(oa, ones_a) * (1.0 / A_HD)
    cen = oa - mean
    var = _segsum(cen * cen, ones_a) * (1.0 / A_HD)
    ya = (cen * lax.rsqrt(var + GN_EPS) * lnw_ref[...] + lnb_ref[...] + bonus_ref[0]) * g_ref[0]
    ob = ob_ref[0, 0] + ob_ref[0, 1]
    ms = _segsum(ob * ob, ones_b_ref[...]) * (1.0 / B_DK)
    yb = ob * lax.rsqrt(ms + NORM_EPS) * hgn_ref[...] * _silu(gate_ref[0])
    y = _dot(ya, w_ref[:A_W, :]) + _dot(yb, w_ref[A_W:, :])
    gate = _gate_vec(modl_ref, modc_ref, 2, pos0, tm, ctx_len)
    o_ref[0] = x_ref[0] + gate * y


def _rec_out(xcat, mods, ctx_row, oa, g_bonus, ob, p_hgrn, ln_w, ln_b, hg_norm, w_out, ctx_len):
    b, s, d = xcat.shape
    tm = TOK_TILE
    row = lambda a: a.reshape(1, -1)
    full = lambda a: pl.BlockSpec(a.shape, lambda i, j: (0,) * a.ndim)
    consts = [row(ln_w), row(ln_b), row(jnp.tile(hg_norm, B_HEADS)), _block_ones(A_W, A_HD),
              _block_ones(B_W, B_DK), w_out]
    return pl.pallas_call(
        functools.partial(_rec_out_kernel, ctx_len=ctx_len, tm=tm),
        grid=(b, s // tm),
        in_specs=[pl.BlockSpec((1, tm, d), lambda i, j: (i, j, 0))] + _mod_specs(d, ctx_row) + [
            pl.BlockSpec((1, 2, tm, A_W), lambda i, j: (i, 0, j, 0)),
            pl.BlockSpec((1, tm, A_W), lambda i, j: (i, j, 0)),
            pl.BlockSpec((1, tm, A_W), lambda i, j: (i, j, 1)),
            pl.BlockSpec((1, 2, tm, B_W), lambda i, j: (i, 0, j, 0)),
            pl.BlockSpec((1, tm, B_W), lambda i, j: (i, j, 4)),
        ] + [full(a) for a in consts],
        out_specs=pl.BlockSpec((1, tm, d), lambda i, j: (i, j, 0)),
        out_shape=jax.ShapeDtypeStruct((b, s, d), F32),
        compiler_params=_cparams(("parallel", "parallel")),
        name="rec_out",
    )(xcat, mods, mods, oa, g_bonus, g_bonus, ob, p_hgrn, *consts)


def _res_proj_kernel(x_ref, modl_ref, modc_ref, y_ref, w_ref, o_ref, *, ctx_len, tm):
    pos0 = pl.program_id(1) * tm
    gate = _gate_vec(modl_ref, modc_ref, 2, pos0, tm, ctx_len)
    o_ref[0] = x_ref[0] + gate * _dot(y_ref[0], w_ref[...])


def _res_proj(x, mods, ctx_row, y, w, ctx_len, x_skip=0):
    b, s, k = y.shape
    d = x.shape[-1]
    tm = TOK_TILE
    skip = x_skip // tm
    return pl.pallas_call(
        functools.partial(_res_proj_kernel, ctx_len=ctx_len, tm=tm),
        grid=(b, s // tm),
        in_specs=[pl.BlockSpec((1, tm, d), lambda i, j: (i, j + skip, 0))] + _mod_specs(d, ctx_row) + [
            pl.BlockSpec((1, tm, k), lambda i, j: (i, j, 0)),
            pl.BlockSpec((k, d), lambda i, j: (0, 0))],
        out_specs=pl.BlockSpec((1, tm, d), lambda i, j: (i, j, 0)),
        out_shape=jax.ShapeDtypeStruct((b, s, d), F32),
        compiler_params=_cparams(("parallel", "parallel")),
        name="res_proj",
    )(x, mods, mods, y, w)


def _ffn_kernel(x_ref, modl_ref, modc_ref, g_ref, wg_ref, wu_ref, wd_ref, o_ref, h_ref, acc_ref, *, ctx_len, tm,
                n_f):
    f = pl.program_id(2)
    pos0 = pl.program_id(1) * tm

    @pl.when(f == 0)
    def _():
        h_ref[...] = _bf(_modulate(x_ref[0], g_ref[...], modl_ref, modc_ref, 3, pos0, ctx_len))
        acc_ref[...] = jnp.zeros_like(acc_ref)

    h = h_ref[...]
    act = _silu(jnp.dot(h, wg_ref[...], preferred_element_type=F32)) * jnp.dot(h, wu_ref[...],
                                                                               preferred_element_type=F32)
    acc_ref[...] += _dot(act, wd_ref[...])

    @pl.when(f == n_f - 1)
    def _():
        o_ref[0] = x_ref[0] + _gate_vec(modl_ref, modc_ref, 5, pos0, tm, ctx_len) * acc_ref[...]


def _ffn(x, mods, ctx_row, g, wg, wu, wd, ctx_len, tm, tf):
    b, s, d = x.shape
    ff = wg.shape[1]
    n_f = ff // tf
    return pl.pallas_call(
        functools.partial(_ffn_kernel, ctx_len=ctx_len, tm=tm, n_f=n_f),
        grid=(b, s // tm, n_f),
        in_specs=[pl.BlockSpec((1, tm, d), lambda i, j, f: (i, j, 0))] + _mod_specs(d, ctx_row) + [
            pl.BlockSpec((1, d), lambda i, j, f: (0, 0)),
            pl.BlockSpec((d, tf), lambda i, j, f: (0, f)),
            pl.BlockSpec((d, tf), lambda i, j, f: (0, f)),
            pl.BlockSpec((tf, d), lambda i, j, f: (f, 0))],
        out_specs=pl.BlockSpec((1, tm, d), lambda i, j, f: (i, j, 0)),
        out_shape=jax.ShapeDtypeStruct((b, s, d), F32),
        scratch_shapes=[pltpu.VMEM((tm, d), BF16), pltpu.VMEM((tm, d), F32)],
        compiler_params=_cparams(("parallel", "parallel", "arbitrary")),
        name="ffn",
    )(x, mods, mods, g.reshape(1, d), wg, wu, wd)


def _attn_kernel(sink_ref, q_ref, kc_ref, vc_ref, kp_ref, kq_ref, kn_ref, vp_ref, vq_ref, vn_ref, o_ref, *,
                 ctx_len, n_lat):
    i = pl.program_id(1)
    nk = ctx_len + 3 * ATT_BLOCK
    k_all = jnp.concatenate([kc_ref[0], kp_ref[0], kq_ref[0], kn_ref[0]], axis=0)
    v_all = jnp.concatenate([vc_ref[0], vp_ref[0], vq_ref[0], vn_ref[0]], axis=0)
    row = lax.broadcasted_iota(jnp.int32, (ATT_BLOCK, nk), 0)
    col = lax.broadcasted_iota(jnp.int32, (ATT_BLOCK, nk), 1)
    rel = col - (ctx_len + ATT_BLOCK)
    kabs = i * ATT_BLOCK + rel
    band = (jnp.abs(row - rel) <= WINDOW) & (kabs >= 0) & (kabs < n_lat)
    bias = jnp.where((col < ctx_len) | band, 0.0, NEG_INF)
    bias = jnp.concatenate([bias] * GQ, axis=0)
    grp = lax.broadcasted_iota(jnp.int32, (GQ * ATT_BLOCK, 1), 0) // ATT_BLOCK
    scale = HD ** -0.5
    for hk in range(HKV):
        kh = _bf(k_all[:, hk * HD:(hk + 1) * HD])
        vh = _bf(v_all[:, hk * HD:(hk + 1) * HD])
        heads = [hk * GQ + g for g in range(GQ)]
        q = jnp.concatenate([q_ref[0, :, hq * HD:(hq + 1) * HD] for hq in heads], axis=0) * scale
        sk = jnp.zeros((GQ * ATT_BLOCK, 1), F32)
        for g, hq in enumerate(heads):
            sk = jnp.where(grp == g, sink_ref[hq], sk)
        s = _dot_nt(q, kh) + bias
        m = jnp.maximum(jnp.max(s, axis=-1, keepdims=True), sk)
        p = jnp.exp(s - m)
        den = jnp.sum(p, axis=-1, keepdims=True) + jnp.exp(sk - m)
        o = jnp.dot(_bf(p), vh, preferred_element_type=F32) / den
        for g, hq in enumerate(heads):
            o_ref[0, :, hq * HD:(hq + 1) * HD] = o[g * ATT_BLOCK:(g + 1) * ATT_BLOCK]


def _attention(qkv, sink, ctx_len):
    b, s, _ = qkv.shape
    n_lat = s - ctx_len
    nb = n_lat // ATT_BLOCK
    cb = ctx_len // ATT_BLOCK
    kcol = Q_COLS // KV_COLS
    vcol = kcol + 1

    def band(colblk, shift):
        return pl.BlockSpec((1, ATT_BLOCK, KV_COLS),
                            lambda bi, i: (bi, cb + jnp.clip(i + shift, 0, nb - 1), colblk))

    return pl.pallas_call(
        functools.partial(_attn_kernel, ctx_len=ctx_len, n_lat=n_lat),
        grid=(b, nb),
        in_specs=[pl.BlockSpec(memory_space=pltpu.SMEM),
                  pl.BlockSpec((1, ATT_BLOCK, Q_COLS), lambda bi, i: (bi, cb + i, 0)),
                  pl.BlockSpec((1, ctx_len, KV_COLS), lambda bi, i: (bi, 0, kcol)),
                  pl.BlockSpec((1, ctx_len, KV_COLS), lambda bi, i: (bi, 0, vcol)),
                  band(kcol, -1), band(kcol, 0), band(kcol, 1),
                  band(vcol, -1), band(vcol, 0), band(vcol, 1)],
        out_specs=pl.BlockSpec((1, ATT_BLOCK, Q_COLS), lambda bi, i: (bi, i, 0)),
        out_shape=jax.ShapeDtypeStruct((b, n_lat, Q_COLS), F32),
        compiler_params=_cparams(("parallel", "parallel")),
        name="attention",
    )(sink, qkv, qkv, qkv, qkv, qkv, qkv, qkv, qkv, qkv)


def _router_kernel(x_ref, modl_ref, modc_ref, g_ref, w_ref, b_ref, o_ref, e_ref, f_ref, *, tm):
    f = _modulate(x_ref[0], g_ref[...], modl_ref, modc_ref, 3, 0, 0)
    f_ref[0] = f
    logits = _dot_f32(f, w_ref[...]) + b_ref[...]
    lane = lax.broadcasted_iota(jnp.int32, logits.shape, 1).astype(F32)
    logits = jnp.where(lane < N_EXPERTS, logits, NEG_INF)
    ex = jnp.exp(logits - jnp.max(logits, axis=-1, keepdims=True))
    probs = ex / jnp.sum(ex, axis=-1, keepdims=True)
    p1 = jnp.max(probs, axis=-1, keepdims=True)
    i1 = jnp.min(jnp.where(probs == p1, lane, float(LANES)), axis=-1, keepdims=True)
    rest = jnp.where(lane == i1, -1.0, probs)
    p2 = jnp.max(rest, axis=-1, keepdims=True)
    i2 = jnp.min(jnp.where(rest == p2, lane, float(LANES)), axis=-1, keepdims=True)
    tot = p1 + p2
    o_ref[0] = jnp.where(lane == 0.0, p1 / tot, jnp.where(lane == 1.0, p2 / tot, 0.0))
    e_ref[0] = jnp.where(lane == 0.0, i1, jnp.where(lane == 1.0, i2, 0.0)).astype(jnp.int32)


def _router(x, mods, g, w, bias):
    b, s, d = x.shape
    tm = TOK_TILE
    wpad = jnp.zeros((d, LANES), F32).at[:, :N_EXPERTS].set(w)
    bpad = jnp.zeros((1, LANES), F32).at[0, :N_EXPERTS].set(bias)
    lane_spec = pl.BlockSpec((1, tm, LANES), lambda i, j: (i, j, 0))
    return pl.pallas_call(
        functools.partial(_router_kernel, tm=tm),
        grid=(b, s // tm),
        in_specs=[pl.BlockSpec((1, tm, d), lambda i, j: (i, j, 0))] + _mod_specs(d, 0) + [
            pl.BlockSpec((1, d), lambda i, j: (0, 0)),
            pl.BlockSpec((d, LANES), lambda i, j: (0, 0)),
            pl.BlockSpec((1, LANES), lambda i, j: (0, 0))],
        out_specs=[lane_spec, lane_spec, pl.BlockSpec((1, tm, d), lambda i, j: (i, j, 0))],
        out_shape=[jax.ShapeDtypeStruct((b, s, LANES), F32), jax.ShapeDtypeStruct((b, s, LANES), jnp.int32),
                   jax.ShapeDtypeStruct((b, s, d), F32)],
        compiler_params=_cparams(("parallel", "parallel")),
        name="router",
    )(x, mods, mods, g.reshape(1, d), wpad, bpad)


MOE_ROWS = 256


def _row_scatter_kernel(dest_ref, pad_ref, f_ref, xs_hbm, zero_ref, sem, *, tm, n_pad):
    base = pl.program_id(0) * tm

    @pl.when(pl.program_id(0) == 0)
    def _():
        zero_ref[...] = jnp.zeros_like(zero_ref)

        def zero_row(r, carry):
            pltpu.make_async_copy(zero_ref, xs_hbm.at[pl.ds(pad_ref[r], 1)], sem).start()
            return carry

        lax.fori_loop(0, n_pad, zero_row, 0, unroll=8)
        for _ in range(n_pad // tm):
            pltpu.make_async_copy(f_ref, xs_hbm.at[pl.ds(0, tm)], sem).wait()

    def issue(r, carry):
        slot = 2 * (base + r)
        pltpu.make_async_copy(f_ref.at[pl.ds(r, 1)], xs_hbm.at[pl.ds(dest_ref[slot], 1)], sem).start()
        pltpu.make_async_copy(f_ref.at[pl.ds(r, 1)], xs_hbm.at[pl.ds(dest_ref[slot + 1], 1)], sem).start()
        return carry

    lax.fori_loop(0, tm, issue, 0, unroll=8)
    for _ in range(2):
        pltpu.make_async_copy(f_ref, xs_hbm.at[pl.ds(0, tm)], sem).wait()


def _row_scatter(f, dest, pad_rows, n_rows):
    n_tok, d = f.shape
    tm = TOK_TILE
    n_pad = pad_rows.shape[0]
    assert n_pad % tm == 0 and 2 * n_tok + n_pad == n_rows
    return pl.pallas_call(
        functools.partial(_row_scatter_kernel, tm=tm, n_pad=n_pad),
        grid_spec=pltpu.PrefetchScalarGridSpec(
            num_scalar_prefetch=2,
            grid=(n_tok // tm,),
            in_specs=[pl.BlockSpec((tm, d), lambda i, dr, pr: (i, 0))],
            out_specs=pl.BlockSpec(memory_space=pl.ANY),
            scratch_shapes=[pltpu.VMEM((1, d), f.dtype), pltpu.SemaphoreType.DMA(())]),
        out_shape=jax.ShapeDtypeStruct((n_rows, d), f.dtype),
        compiler_params=_cparams(("arbitrary",)),
        name="moe_scatter",
    )(dest, pad_rows, f)


def _expert_kernel(be_ref, x_ref, wg_ref, wu_ref, wd_ref, o_ref):
    h = _bf(x_ref[...])
    act = _silu(jnp.dot(h, wg_ref[0], preferred_element_type=F32)) * jnp.dot(h, wu_ref[0],
                                                                              preferred_element_type=F32)
    o_ref[...] = _dot(act, wd_ref[0])


def _experts(xs, block_e, wg, wu, wd):
    n_rows, d = xs.shape
    ff = wg.shape[2]
    rows = MOE_ROWS
    return pl.pallas_call(
        _expert_kernel,
        grid_spec=pltpu.PrefetchScalarGridSpec(
            num_scalar_prefetch=1,
            grid=(n_rows // rows,),
            in_specs=[pl.BlockSpec((rows, d), lambda i, be: (i, 0)),
                      pl.BlockSpec((1, d, ff), lambda i, be: (be[i], 0, 0)),
                      pl.BlockSpec((1, d, ff), lambda i, be: (be[i], 0, 0)),
                      pl.BlockSpec((1, ff, d), lambda i, be: (be[i], 0, 0))],
            out_specs=pl.BlockSpec((rows, d), lambda i, be: (i, 0))),
        out_shape=jax.ShapeDtypeStruct((n_rows, d), F32),
        compiler_params=_cparams(("arbitrary",)),
        name="moe_experts",
    )(block_e, xs, wg, wu, wd)


def _combine_kernel(dest_ref, x_ref, modl_ref, modc_ref, w_ref, gfin_ref, ys_hbm, o_ref, y1_ref, y2_ref, sem, *,
                    tm, tiles_per_row):
    tile = pl.program_id(0) * tiles_per_row + pl.program_id(1)
    n_tiles = pl.num_programs(0) * tiles_per_row
    buf = tile % 2

    def fetch(t, b):
        def issue(r, carry):
            slot = 2 * (t * tm + r)
            pltpu.make_async_copy(ys_hbm.at[pl.ds(dest_ref[slot], 1)], y1_ref.at[b, pl.ds(r, 1)], sem.at[b]).start()
            pltpu.make_async_copy(ys_hbm.at[pl.ds(dest_ref[slot + 1], 1)], y2_ref.at[b, pl.ds(r, 1)],
                                  sem.at[b]).start()
            return carry

        lax.fori_loop(0, tm, issue, 0, unroll=8)

    @pl.when(tile == 0)
    def _():
        fetch(0, 0)

    @pl.when(tile + 1 < n_tiles)
    def _():
        fetch(tile + 1, 1 - buf)

    pltpu.make_async_copy(ys_hbm.at[pl.ds(0, tm)], y1_ref.at[buf], sem.at[buf]).wait()
    pltpu.make_async_copy(ys_hbm.at[pl.ds(0, tm)], y2_ref.at[buf], sem.at[buf]).wait()
    w = w_ref[0]
    moe = w[:, 0:1] * y1_ref[buf] + w[:, 1:2] * y2_ref[buf]
    y = x_ref[0] + _gate_vec(modl_ref, modc_ref, 5, 0, tm, 0) * moe
    o_ref[0] = y * lax.rsqrt(jnp.mean(y * y, axis=-1, keepdims=True) + NORM_EPS) * gfin_ref[...]


def _combine(x, mods, tokw, dest, ys, final_g):
    b, s, d = x.shape
    tm = TOK_TILE
    tiles = s // tm
    return pl.pallas_call(
        functools.partial(_combine_kernel, tm=tm, tiles_per_row=tiles),
        grid_spec=pltpu.PrefetchScalarGridSpec(
            num_scalar_prefetch=1,
            grid=(b, tiles),
            in_specs=[pl.BlockSpec((1, tm, d), lambda i, j, dr: (i, j, 0)),
                      pl.BlockSpec((1, N_MOD, d), lambda i, j, dr: (i, 0, 0)),
                      pl.BlockSpec((1, N_MOD, d), lambda i, j, dr: (0, 0, 0)),
                      pl.BlockSpec((1, tm, LANES), lambda i, j, dr: (i, j, 0)),
                      pl.BlockSpec((1, d), lambda i, j, dr: (0, 0)),
                      pl.BlockSpec(memory_space=pl.ANY)],
            out_specs=pl.BlockSpec((1, tm, d), lambda i, j, dr: (i, j, 0)),
            scratch_shapes=[pltpu.VMEM((2, tm, d), F32), pltpu.VMEM((2, tm, d), F32),
                            pltpu.SemaphoreType.DMA((2,))]),
        out_shape=jax.ShapeDtypeStruct((b, s, d), F32),
        compiler_params=_cparams(("arbitrary", "arbitrary")),
        name="moe_combine",
    )(dest, x, mods, mods, tokw, final_g.reshape(1, d), ys)


def _moe_routing(eidx):
    m = eidx.shape[0]
    oh = (eidx[:, None] == jnp.arange(N_EXPERTS, dtype=jnp.int32)[None, :]).astype(jnp.int32)
    csum = jnp.cumsum(oh, axis=0)
    rank = jnp.sum((csum - oh) * oh, axis=1)
    counts = csum[-1]
    padded = (counts + MOE_ROWS - 1) // MOE_ROWS * MOE_ROWS
    pad_end = jnp.cumsum(padded)
    dest = jnp.sum(oh * (pad_end - padded)[None, :], axis=1) + rank
    n_blocks = m // MOE_ROWS + N_EXPERTS
    starts = jnp.arange(n_blocks, dtype=jnp.int32) * MOE_ROWS
    block_e = jnp.minimum(jnp.sum((starts[:, None] >= pad_end[None, :]).astype(jnp.int32), axis=1), N_EXPERTS - 1)
    pad_cnt = padded - counts
    pad_cum = jnp.cumsum(pad_cnt)
    idx = jnp.arange(N_EXPERTS * MOE_ROWS, dtype=jnp.int32)
    owner = jnp.sum((idx[:, None] >= pad_cum[None, :]).astype(jnp.int32), axis=1)
    own = (jnp.minimum(owner, N_EXPERTS - 1)[:, None] == jnp.arange(N_EXPERTS)[None, :]).astype(jnp.int32)
    in_expert = jnp.sum(own * (pad_end - pad_cnt)[None, :], axis=1) + idx - jnp.sum(
        own * (pad_cum - pad_cnt)[None, :], axis=1)
    pad_rows = jnp.where(owner < N_EXPERTS, in_expert, pad_end[-1] + idx - pad_cum[-1])
    return dest.astype(jnp.int32), block_e, pad_rows.astype(jnp.int32), n_blocks * MOE_ROWS


def _rope_tables(n_lat, ctx_len):
    rows = n_lat // GRID_W
    row = jnp.repeat(jnp.arange(rows, dtype=F32), GRID_W)
    col = jnp.tile(jnp.arange(GRID_W, dtype=F32), rows)
    inv = ROPE_BASE ** (-jnp.arange(0, AX_DIM, 2, dtype=F32) / AX_DIM)
    ar, ac = row[:, None] * inv, col[:, None] * inv
    cos = jnp.concatenate([jnp.cos(ar), jnp.cos(ar), jnp.cos(ac), jnp.cos(ac)], axis=-1)
    sin = jnp.concatenate([-jnp.sin(ar), jnp.sin(ar), -jnp.sin(ac), jnp.sin(ac)], axis=-1)
    n_heads = ROPE_COLS // HD
    cos = jnp.concatenate([jnp.ones((ctx_len, HD), F32), cos], axis=0)
    sin = jnp.concatenate([jnp.zeros((ctx_len, HD), F32), sin], axis=0)
    half = AX_DIM // 2
    j = jnp.arange(HD)
    partner = jnp.where((j % AX_DIM) < half, j + half, j - half)
    perm = (jnp.arange(n_heads)[:, None] * HD + partner[None, :]).reshape(-1)
    return jnp.tile(cos, (1, n_heads)), jnp.tile(sin, (1, n_heads)), perm


def kernel(x, c, ctx, c_ctx, mod_w, mod_b, norm_mix, norm_ffn, norm_final, rec_w_in, rec_w_out, rwkv_mu, rwkv_w0, rwkv_w_up, rwkv_a0, rwkv_a_up, rwkv_g_up, rwkv_k_k, rwkv_k_a, rwkv_r_k, rwkv_ln_w, rwkv_ln_b, hgrn_lb, hgrn_norm, ffn_w_gate, ffn_w_up, ffn_w_down, att_w_in, att_w_out, att_sink, moe_router, moe_router_b, moe_w_gate, moe_w_up, moe_w_down):
    bsz, n_lat, d = x.shape
    ctx_len = ctx.shape[1]
    xcat = jnp.concatenate([ctx, x], axis=1)

    n_rows = -(-(bsz + 1) // 8) * 8
    cvec = jnp.zeros((n_rows, d), F32).at[:bsz].set(c).at[bsz].set(c_ctx)
    mods = [_adaln(cvec, mod_w[l], mod_b[l]).reshape(n_rows, N_MOD, d) for l in range(2)]

    p_rwkv, p_hgrn = _proj(xcat, mods[0], bsz, norm_mix[0], _bf(rec_w_in[0]), (RWKV_COLS, HGRN_COLS), ctx_len)
    ops, vv, g_bonus, gam = _rwkv_features(p_rwkv, rwkv_mu[0], rwkv_w0[0], rwkv_w_up[0], rwkv_a0[0], rwkv_a_up[0],
                                           rwkv_g_up[0], rwkv_k_k[0], rwkv_k_a[0], rwkv_r_k[0].reshape(-1), ctx_len)
    oa = _rwkv_scan(ops, vv, gam, ctx_len)
    lb = jnp.cumsum(jax.nn.softmax(hgrn_lb.astype(F32), axis=1), axis=1)[:, 0].reshape(2, 1, B_W)
    ob = _hgrn_scan(p_hgrn, lb, ctx_len)
    xcat = _rec_out(xcat, mods[0], bsz, oa, g_bonus, ob, p_hgrn, rwkv_ln_w[0], rwkv_ln_b[0], hgrn_norm[0],
                    _bf(rec_w_out[0]), ctx_len)
    xcat = _ffn(xcat, mods[0], bsz, norm_ffn[0], _bf(ffn_w_gate[0]), _bf(ffn_w_up[0]), _bf(ffn_w_down[0]), ctx_len,
                tm=384, tf=2816)

    cos, sin, perm = _rope_tables(n_lat, ctx_len)
    w_att = att_w_in[0]
    (qkv,) = _proj(xcat, mods[1], bsz, norm_mix[1], _bf(w_att), (ATT_COLS,), ctx_len,
                   rope=(_bf(w_att[:, perm]), cos, sin))
    att = _attention(qkv, att_sink[0], ctx_len)
    x_lat = _res_proj(xcat, mods[1], 0, att, _bf(att_w_out[0]), 0, x_skip=ctx_len)
    tokw, eidx, f_lat = _router(x_lat, mods[1], norm_ffn[1], moe_router[0], moe_router_b[0])
    dest, block_e, pad_rows, n_rows = _moe_routing(eidx[..., :2].reshape(-1))
    xs = _row_scatter(f_lat.reshape(bsz * n_lat, d), dest, pad_rows, n_rows)
    ys = _experts(xs, block_e, _bf(moe_w_gate[0]), _bf(moe_w_up[0]), _bf(moe_w_down[0]))
    return _combine(x_lat, mods[1], tokw, dest, ys, norm_final)
```

```python
import functools

import jax
import jax.numpy as jnp
from jax import lax
from jax.experimental import pallas as pl
from jax.experimental.pallas import tpu as pltpu

F32 = jnp.float32
BF16 = jnp.bfloat16

N_MOD = 6
NORM_EPS = 1e-6
NEG_INF = -1e30

A_HEADS = 8
A_HD = 64
A_W = A_HEADS * A_HD
DECAY_LORA = 64
AAA_LORA = 64
GATE_LORA = 128
RWKV_COLS = 3 * A_W + DECAY_LORA + AAA_LORA + GATE_LORA
GN_EPS = 64e-5

B_HEADS = 4
B_DK = 128
B_W = B_HEADS * B_DK
HGRN_COLS = 5 * B_W

HQ = 16
HKV = 4
GQ = HQ // HKV
HD = 64
WINDOW = 128
ATT_BLOCK = 128
AX_DIM = HD // 2
ROPE_BASE = 10000.0
GRID_W = 64
Q_COLS = HQ * HD
KV_COLS = HKV * HD
ROPE_COLS = Q_COLS + KV_COLS
ATT_COLS = Q_COLS + 2 * KV_COLS

N_EXPERTS = 8
LANES = 128
CHUNK = 64
SCAN_BLOCK = 256
MXU_WIDTH = 256
RWKV_GROUP = MXU_WIDTH // A_HD
TOK_TILE = 256
VMEM_LIMIT = 56 * 1024 * 1024

OP_KAP, OP_RT, OP_KBAR, OP_BBAR, OP_KGAM, OP_BGAM = range(6)
N_OPS = 6


def _cparams(sem):
    return pltpu.CompilerParams(dimension_semantics=sem, vmem_limit_bytes=VMEM_LIMIT)


def _bf(x):
    return x.astype(BF16)


def _dot(a, b):
    return jnp.dot(_bf(a), _bf(b), preferred_element_type=F32)


def _dot_nt(a, b):
    return lax.dot_general(_bf(a), _bf(b), (((1,), (1,)), ((), ())), preferred_element_type=F32)


def _dot_tn(a, b):
    return lax.dot_general(_bf(a), _bf(b), (((0,), (0,)), ((), ())), preferred_element_type=F32)


def _dot_f32(a, b):
    a_hi, b_hi = _bf(a), _bf(b)
    a_lo, b_lo = _bf(a - a_hi.astype(F32)), _bf(b - b_hi.astype(F32))
    acc = jnp.dot(a_hi, b_hi, preferred_element_type=F32)
    acc += jnp.dot(a_hi, b_lo, preferred_element_type=F32)
    acc += jnp.dot(a_lo, b_hi, preferred_element_type=F32)
    return acc


def _sigmoid(x):
    return 1.0 / (1.0 + jnp.exp(-x))


def _silu(x):
    return x * _sigmoid(x)


def _segsum(x, ones_bd):
    hi = _bf(x)
    lo = _bf(x - hi.astype(F32))
    return jnp.dot(hi, ones_bd, preferred_element_type=F32) + jnp.dot(lo, ones_bd, preferred_element_type=F32)


def _dot_split(a, x):
    hi = _bf(x)
    r1 = x - hi.astype(F32)
    mid = _bf(r1)
    lo = _bf(r1 - mid.astype(F32))
    acc = jnp.dot(a, hi, preferred_element_type=F32)
    acc += jnp.dot(a, mid, preferred_element_type=F32)
    acc += jnp.dot(a, lo, preferred_element_type=F32)
    return acc


def _block_ones(width, seg):
    i = jnp.arange(width) // seg
    return (i[:, None] == i[None, :]).astype(BF16)


def _modulate(x, g, modl_ref, modc_ref, row, pos0, ctx_len, pos=None):
    y = x * lax.rsqrt(jnp.mean(x * x, axis=-1, keepdims=True) + NORM_EPS) * g
    if pos is None:
        pos = pos0 + lax.broadcasted_iota(jnp.int32, (x.shape[0], 1), 0)
    is_ctx = pos < ctx_len
    shift = jnp.where(is_ctx, modc_ref[0, row:row + 1, :], modl_ref[0, row:row + 1, :])
    scale = jnp.where(is_ctx, modc_ref[0, row + 1:row + 2, :], modl_ref[0, row + 1:row + 2, :])
    return y * (1.0 + scale) + shift


def _gate_vec(modl_ref, modc_ref, row, pos0, n, ctx_len):
    pos = pos0 + lax.broadcasted_iota(jnp.int32, (n, 1), 0)
    return jnp.where(pos < ctx_len, modc_ref[0, row:row + 1, :], modl_ref[0, row:row + 1, :])


def _mod_specs(d, ctx_row, batch_axis=0):
    def lat_map(*idx):
        return (idx[batch_axis], 0, 0)

    def ctx_map(*idx):
        return (ctx_row, 0, 0)

    return [pl.BlockSpec((1, N_MOD, d), lat_map), pl.BlockSpec((1, N_MOD, d), ctx_map)]


def _adaln_kernel(c_ref, w_ref, b_ref, o_ref):
    o_ref[...] = _dot(_silu(c_ref[...]), w_ref[...]) + b_ref[...]


def _adaln(cvec, w, b):
    r, d = cvec.shape
    n = w.shape[1]
    tn = 1024
    return pl.pallas_call(
        _adaln_kernel,
        grid=(n // tn,),
        in_specs=[pl.BlockSpec((r, d), lambda j: (0, 0)),
                  pl.BlockSpec((d, tn), lambda j: (0, j)),
                  pl.BlockSpec((1, tn), lambda j: (0, j))],
        out_specs=pl.BlockSpec((r, tn), lambda j: (0, j)),
        out_shape=jax.ShapeDtypeStruct((r, n), F32),
        compiler_params=_cparams(("parallel",)),
        name="adaln",
    )(cvec, w, b.reshape(1, n))


def _proj_rope_kernel(x_ref, modl_ref, modc_ref, g_ref, w_ref, wrot_ref, cos_ref, sin_ref, o_ref, *, ctx_len, tm,
                      rope_cols):
    pos0 = pl.program_id(1) * tm
    h = _bf(_modulate(x_ref[0], g_ref[...], modl_ref, modc_ref, 0, pos0, ctx_len))
    y = jnp.dot(h, w_ref[...], preferred_element_type=F32)
    yr = jnp.dot(h, wrot_ref[...], preferred_element_type=F32)
    o_ref[0, :, :rope_cols] = y[:, :rope_cols] * cos_ref[...] + yr * sin_ref[...]
    o_ref[0, :, rope_cols:] = y[:, rope_cols:]


def _proj_rope(xcat, mods, ctx_row, g, w, wrot, cos, sin, ctx_len):
    b, s, d = xcat.shape
    n = w.shape[1]
    tm = TOK_TILE
    rope_cols = wrot.shape[1]
    return pl.pallas_call(
        functools.partial(_proj_rope_kernel, ctx_len=ctx_len, tm=tm, rope_cols=rope_cols),
        grid=(b, s // tm),
        in_specs=[pl.BlockSpec((1, tm, d), lambda i, j: (i, j, 0))] + _mod_specs(d, ctx_row) + [
            pl.BlockSpec((1, d), lambda i, j: (0, 0)),
            pl.BlockSpec((d, n), lambda i, j: (0, 0)),
            pl.BlockSpec((d, rope_cols), lambda i, j: (0, 0)),
            pl.BlockSpec((tm, rope_cols), lambda i, j: (j, 0)),
            pl.BlockSpec((tm, rope_cols), lambda i, j: (j, 0))],
        out_specs=pl.BlockSpec((1, tm, n), lambda i, j: (i, j, 0)),
        out_shape=jax.ShapeDtypeStruct((b, s, n), F32),
        compiler_params=_cparams(("parallel", "parallel")),
        name="proj_rope",
    )(xcat, mods, mods, g.reshape(1, d), w, wrot, cos, sin)


def _softplus(x):
    return jnp.maximum(x, 0.0) + jnp.log(1.0 + jnp.exp(-jnp.abs(x)))


def _rec_in_kernel(x_ref, xp_ref, xn_ref, modl_ref, modc_ref, g_ref, w_ref, mu_ref, w0_ref, wup_ref, a0_ref,
                   aup_ref, gup_ref, kk_ref, ka_ref, rk_ref, ones_ref, ops_ref, v_ref, gb_ref, gam_ref, ph_ref, *,
                   ctx_len, seq_len, tm):
    pos0 = pl.program_id(1) * tm
    gvec = g_ref[...]
    h = _bf(_modulate(x_ref[0], gvec, modl_ref, modc_ref, 0, pos0, ctx_len))
    p = jnp.dot(h, w_ref[:, :RWKV_COLS], preferred_element_type=F32)

    hgrn_pieces = [(RWKV_COLS + c, min(B_W, HGRN_COLS - c)) for c in range(0, HGRN_COLS, B_W)]

    def hgrn_piece():
        if hgrn_pieces:
            lo, width = hgrn_pieces.pop(0)
            ph_ref[0, :, lo - RWKV_COLS:lo - RWKV_COLS + width] = jnp.dot(h, w_ref[:, lo:lo + width],
                                                                          preferred_element_type=F32)

    halo = lax.broadcasted_iota(jnp.int32, (16, 1), 0)
    halo_pos = jnp.where(halo < 8, pos0 - 8 + halo, pos0 + tm - 8 + halo)
    xh = jnp.concatenate([xp_ref[0], xn_ref[0]], axis=0)
    p_halo = jnp.dot(_bf(_modulate(xh, gvec, modl_ref, modc_ref, 0, 0, ctx_len, pos=halo_pos)),
                     w_ref[:, :RWKV_COLS], preferred_element_type=F32)
    rows = lax.broadcasted_iota(jnp.int32, (tm, 1), 0)
    starts_seq = (pos0 == 0) | (pos0 == ctx_len)
    ends_seq = (pos0 + tm == ctx_len) | (pos0 + tm == seq_len)
    prev_halo = jnp.where(starts_seq, 0.0, p_halo[7:8, :])
    next_halo = jnp.where(ends_seq, 0.0, p_halo[8:9, :])
    prev = jnp.where(rows == 0, prev_halo, pltpu.roll(p, 1, 0))
    nxt = jnp.where(rows == tm - 1, next_halo, pltpu.roll(p, tm - 1, 0))
    p = p + mu_ref[...] * (0.5 * (prev + nxt) - p)
    hgrn_piece()

    r = p[:, 0:A_W]
    k = p[:, A_W:2 * A_W]
    v = p[:, 2 * A_W:3 * A_W]
    lo = 3 * A_W
    wd = p[:, lo:lo + DECAY_LORA]
    ad = p[:, lo + DECAY_LORA:lo + DECAY_LORA + AAA_LORA]
    gd = p[:, lo + DECAY_LORA + AAA_LORA:lo + DECAY_LORA + AAA_LORA + GATE_LORA]

    tw = jnp.tanh(wd)
    a = _sigmoid(a0_ref[...] + _dot(ad, aup_ref[...]))
    ones_bd = ones_ref[...]
    kk = k * kk_ref[...]
    kk = kk / jnp.maximum(jnp.sqrt(_segsum(kk * kk, ones_bd)), 1e-12)
    k = k * (1.0 + (a - 1.0) * ka_ref[...])
    b = kk * a
    v_ref[0] = _bf(v)
    gb_ref[0, :, :A_W] = _dot(_sigmoid(gd), gup_ref[...])
    gb_ref[0, :, A_W:] = _segsum(r * k * rk_ref[...], ones_bd) * v
    hgrn_piece()

    trow = lax.broadcasted_iota(jnp.int32, (tm, tm), 0)
    tcol = lax.broadcasted_iota(jnp.int32, (tm, tm), 1)
    same = (trow // CHUNK) == (tcol // CHUNK)
    n_chunks = tm // CHUNK
    for d in range(2):
        w_log = -_softplus(-(w0_ref[d:d + 1, :] + _dot(tw, wup_ref[d]))) - 0.5
        lw = -jnp.exp(w_log)
        before = (tcol <= trow) if d == 0 else (tcol >= trow)
        g_incl = _dot_split(jnp.where(same & before, 1.0, 0.0).astype(BF16), lw)
        last = [c * CHUNK + (CHUNK - 1 if d == 0 else 0) for c in range(n_chunks)]
        g_end = jnp.concatenate([g_incl[t:t + 1] for t in last], axis=0)
        g_tot = jnp.concatenate([jnp.broadcast_to(g_incl[t:t + 1], (CHUNK, A_W)) for t in last], axis=0)
        hgrn_piece()
        e_ng = jnp.exp(-g_incl)
        e_tail = jnp.exp(g_tot - g_incl)
        operands = {OP_KAP: kk * jnp.exp(g_incl - lw),
                    OP_RT: r * jnp.exp(g_incl),
                    OP_KBAR: k * e_ng, OP_BBAR: b * e_ng,
                    OP_KGAM: k * e_tail, OP_BGAM: b * e_tail}
        for sec, val in operands.items():
            ops_ref[0, d, :, sec * A_W:(sec + 1) * A_W] = _bf(val)
        gam_ref[0, d, 0] = jnp.exp(g_end)
        hgrn_piece()
    while hgrn_pieces:
        hgrn_piece()


def _rec_in(xcat, mods, ctx_row, g, w, mu, w0, w_up, a0, a_up, g_up, k_k, k_a, r_k, ctx_len):
    b, s, d = xcat.shape
    tm = SCAN_BLOCK
    assert ctx_len % tm == 0 and s % tm == 0
    nb8 = s // 8
    row = lambda a: a.reshape(1, -1)
    full = lambda a: pl.BlockSpec(a.shape, lambda i, j: (0,) * a.ndim)
    args = [row(g), w, row(mu), w0, w_up, row(a0), a_up, g_up, row(k_k), row(k_a), row(r_k),
            _block_ones(A_W, A_HD)]
    return pl.pallas_call(
        functools.partial(_rec_in_kernel, ctx_len=ctx_len, seq_len=s, tm=tm),
        grid=(b, s // tm),
        in_specs=[pl.BlockSpec((1, tm, d), lambda i, j: (i, j, 0)),
                  pl.BlockSpec((1, 8, d), lambda i, j: (i, jnp.maximum(j * (tm // 8) - 1, 0), 0)),
                  pl.BlockSpec((1, 8, d), lambda i, j: (i, jnp.minimum((j + 1) * (tm // 8), nb8 - 1), 0)),
                  ] + _mod_specs(d, ctx_row) + [full(a) for a in args],
        out_specs=[pl.BlockSpec((1, 2, tm, N_OPS * A_W), lambda i, j: (i, 0, j, 0)),
                   pl.BlockSpec((1, tm, A_W), lambda i, j: (i, j, 0)),
                   pl.BlockSpec((1, tm, 2 * A_W), lambda i, j: (i, j, 0)),
                   pl.BlockSpec((1, 2, 1, tm // CHUNK, A_W), lambda i, j: (i, 0, j, 0, 0)),
                   pl.BlockSpec((1, tm, HGRN_COLS), lambda i, j: (i, j, 0))],
        out_shape=[jax.ShapeDtypeStruct((b, 2, s, N_OPS * A_W), BF16),
                   jax.ShapeDtypeStruct((b, s, A_W), BF16),
                   jax.ShapeDtypeStruct((b, s, 2 * A_W), F32),
                   jax.ShapeDtypeStruct((b, 2, s // tm, tm // CHUNK, A_W), F32),
                   jax.ShapeDtypeStruct((b, s, HGRN_COLS), F32)],
        compiler_params=_cparams(("parallel", "parallel")),
        name="rec_in",
    )(xcat, xcat, xcat, mods, mods, *args)


def _chunk_order(d, j, n_ctx_chunks, n_chunks):
    back = jnp.where(j < n_ctx_chunks, n_ctx_chunks - 1 - j, n_chunks - 1 + n_ctx_chunks - j)
    return jnp.where(d == 0, j, back)


def _order_masks(rev):
    row = lax.broadcasted_iota(jnp.int32, (CHUNK, CHUNK), 0)
    col = lax.broadcasted_iota(jnp.int32, (CHUNK, CHUNK), 1)
    diff = jnp.where(rev, row - col, col - row)
    return diff < 0, diff <= 0, row == col


def _scan_step_coords(t, n_blocks, n_steps):
    return jnp.minimum(t, n_steps - 1), jnp.maximum(t - 1, 0)


def _rwkv_chunk_kernel(ops_ref, v_ref, gam_ref, o_ref, s_ref, tr_ref, uu_ref, ol_ref, ab_ref,
                       m2_ref, cc_ref, gm_ref, *, n_blocks, n_steps):
    step = pl.program_id(0)
    t_in, t_out = _scan_step_coords(step, n_blocks, n_steps)
    rev = (t_in // n_blocks) % 2 == 1
    rev_out = (t_out // n_blocks) % 2 == 1
    first_out = t_out % n_blocks == 0
    w_slot = step % 2
    r_slot = 1 - w_slot

    @pl.when(step == 0)
    def _():
        s_ref[...] = jnp.zeros_like(s_ref)
        for ref in (tr_ref, uu_ref, ol_ref, ab_ref, m2_ref, cc_ref, gm_ref):
            ref[1] = jnp.zeros(ref.shape[1:], ref.dtype)

    gw = RWKV_GROUP * A_HD
    groups = range(A_HEADS // RWKV_GROUP)
    gsl = [slice(g * gw, (g + 1) * gw) for g in groups]
    row = lax.broadcasted_iota(jnp.int32, (CHUNK, gw), 0)
    col = lax.broadcasted_iota(jnp.int32, (CHUNK, gw), 1) % A_HD
    diff = jnp.where(rev, row - col, col - row)
    strict_c = diff < 0
    incl_c = diff <= 0
    eye_c = jnp.where(diff == 0, 1.0, 0.0)
    bd_mask = (lax.broadcasted_iota(jnp.int32, (gw, gw), 0) // A_HD
               == lax.broadcasted_iota(jnp.int32, (gw, gw), 1) // A_HD)

    def bd(x):
        return jnp.where(bd_mask, jnp.concatenate([x] * RWKV_GROUP, axis=0), jnp.zeros((), x.dtype))

    def stack(x):
        return jnp.concatenate([x[:, h * A_HD:(h + 1) * A_HD] for h in range(RWKV_GROUP)], axis=0)

    n = SCAN_BLOCK // CHUNK
    rows = [pl.ds(pl.multiple_of(jnp.where(rev, (n - 1 - i) * CHUNK, i * CHUNK), CHUNK), CHUNK) for i in range(n)]
    rows_out = [pl.ds(pl.multiple_of(jnp.where(rev_out, (n - 1 - i) * CHUNK, i * CHUNK), CHUNK), CHUNK)
                for i in range(n)]
    units = [(i, g) for i in range(n) for g in groups]

    s = [jnp.where(first_out, 0.0, s_ref[g]) for g in groups]
    zr = {}
    pieces = []

    def advance(i):
        for g in groups:
            ui = i * len(groups) + g
            zr[i, g] = _dot_nt(tr_ref[r_slot, ui], bd(s[g]))
            s[g] = s[g] * gm_ref[r_slot, ui] - jnp.dot(_bf(s[g]), m2_ref[r_slot, ui],
                                                       preferred_element_type=F32) + cc_ref[r_slot, ui]

    def emit_out(i):
        for g in groups:
            ui = i * len(groups) + g
            z = zr[i, g][:CHUNK] + uu_ref[r_slot, ui]
            o_ref[0, 0, rows_out[i], gsl[g]] = (zr[i, g][CHUNK:] + ol_ref[r_slot, ui]
                                                - _dot(ab_ref[r_slot, ui], bd(z)))

    for i in range(n):
        pieces += [functools.partial(advance, i), functools.partial(emit_out, i)]

    def state_piece():
        if pieces:
            pieces.pop(0)()

    def operand(sec, u):
        i, g = u
        return ops_ref[0, 0, rows[i], sec * A_W + g * gw:sec * A_W + (g + 1) * gw]

    kap = {u: operand(OP_KAP, u) for u in units}
    rt = {u: operand(OP_RT, u) for u in units}
    kbar = {u: operand(OP_KBAR, u) for u in units}
    bbar = {u: operand(OP_BBAR, u) for u in units}
    kgam = {u: operand(OP_KGAM, u) for u in units}
    bgam = {u: operand(OP_BGAM, u) for u in units}
    v = {(i, g): v_ref[0, rows[i], gsl[g]] for i, g in units}
    chunk_of = [jnp.where(rev, n - 1 - i, i) for i in range(n)]
    gam = {(i, g): gam_ref[0, 0, 0, pl.ds(chunk_of[i], 1), gsl[g]] for i, g in units}

    x = {u: jnp.concatenate([kap[u], rt[u]], axis=0) for u in units}
    yb = {u: _dot_nt(x[u], bd(bbar[u])) for u in units}
    state_piece()
    a = {u: jnp.where(strict_c, -yb[u][:CHUNK], 0.0) for u in units}
    xs = {u: eye_c + a[u] for u in units}
    pw = {u: _dot(a[u], bd(a[u])) for u in units}
    state_piece()
    for _ in range(4):
        st = {u: _dot(jnp.concatenate([pw[u], xs[u]], axis=0), bd(pw[u])) for u in units}
        pw = {u: st[u][:CHUNK] for u in units}
        xs = {u: xs[u] + st[u][CHUNK:] for u in units}
        state_piece()
    t_inv = {u: xs[u] + _dot(xs[u], bd(pw[u])) for u in units}
    state_piece()
    yk = {u: _dot_nt(x[u], bd(kbar[u])) for u in units}
    ykm = {u: jnp.concatenate([jnp.where(strict_c, yk[u][:CHUNK], 0.0), jnp.where(incl_c, yk[u][CHUNK:], 0.0)],
                              axis=0) for u in units}
    wo = {u: _dot(ykm[u], bd(v[u])) for u in units}
    w1 = {u: wo[u][:CHUNK] for u in units}
    o_loc = {u: wo[u][CHUNK:] for u in units}
    state_piece()
    tk = {u: _dot(t_inv[u], bd(kap[u])) for u in units}
    uu = {u: _dot(t_inv[u], bd(w1[u])) for u in units}
    while pieces:
        state_piece()
    for g in groups:
        s_ref[g] = s[g]
    a_rb = {u: jnp.where(incl_c, yb[u][CHUNK:], 0.0) for u in units}
    tu = {u: _dot_tn(jnp.concatenate([stack(tk[u]), stack(uu[u])], axis=1), bd(bgam[u])) for u in units}
    bd_m2 = {u: _bf(bd(tu[u][:A_HD])) for u in units}
    cc = {u: _dot_tn(stack(v[u]), bd(kgam[u])) - tu[u][A_HD:] for u in units}
    for u in units:
        ui = u[0] * len(groups) + u[1]
        tr_ref[w_slot, ui] = jnp.concatenate([_bf(tk[u]), rt[u]], axis=0)
        uu_ref[w_slot, ui] = uu[u]
        ol_ref[w_slot, ui] = o_loc[u]
        ab_ref[w_slot, ui] = _bf(a_rb[u])
        m2_ref[w_slot, ui] = bd_m2[u]
        cc_ref[w_slot, ui] = cc[u]
        gm_ref[w_slot, ui] = gam[u]


def _rwkv_scan(ops, vv, gam, ctx_len):
    b, _, s, _ = ops.shape
    nc, ncc = s // SCAN_BLOCK, ctx_len // SCAN_BLOCK

    n_steps = b * 2 * nc

    def coords(flat):
        d = (flat // nc) % 2
        return flat // (2 * nc), d, _chunk_order(d, flat % nc, ncc, nc)

    def coords_in(t):
        return coords(_scan_step_coords(t, nc, n_steps)[0])

    def out_map(t):
        bi, d, blk = coords(_scan_step_coords(t, nc, n_steps)[1])
        return bi, d, blk, 0

    def ops_map(t):
        bi, d, blk = coords_in(t)
        return bi, d, blk, 0

    def v_map(t):
        bi, _, blk = coords_in(t)
        return bi, blk, 0

    def gam_map(t):
        bi, d, blk = coords_in(t)
        return bi, d, blk, 0, 0

    n_groups = A_HEADS // RWKV_GROUP
    n_chunks = SCAN_BLOCK // CHUNK
    n_units = n_chunks * n_groups
    gw = RWKV_GROUP * A_HD
    return pl.pallas_call(
        functools.partial(_rwkv_chunk_kernel, n_blocks=nc, n_steps=n_steps),
        grid=(n_steps + 1,),
        in_specs=[pl.BlockSpec((1, 1, SCAN_BLOCK, N_OPS * A_W), ops_map),
                  pl.BlockSpec((1, SCAN_BLOCK, A_W), v_map),
                  pl.BlockSpec((1, 1, 1, n_chunks, A_W), gam_map)],
        out_specs=pl.BlockSpec((1, 1, SCAN_BLOCK, A_W), out_map),
        out_shape=jax.ShapeDtypeStruct((b, 2, s, A_W), F32),
        scratch_shapes=[pltpu.VMEM((n_groups, A_HD, gw), F32),
                        pltpu.VMEM((2, n_units, 2 * CHUNK, gw), BF16),
                        pltpu.VMEM((2, n_units, CHUNK, gw), F32),
                        pltpu.VMEM((2, n_units, CHUNK, gw), F32),
                        pltpu.VMEM((2, n_units, CHUNK, gw), BF16),
                        pltpu.VMEM((2, n_units, gw, gw), BF16),
                        pltpu.VMEM((2, n_units, CHUNK, gw), F32),
                        pltpu.VMEM((2, n_units, 1, gw), F32)],
        compiler_params=_cparams(("arbitrary",)),
        name="rwkv_scan",
    )(ops, vv, gam)


def _hgrn_chunk_kernel(q_ref, i_ref, f_ref, lb_ref, o_ref, s_ref):
    rev = pl.program_id(1) == 1

    @pl.when(pl.program_id(2) == 0)
    def _():
        s_ref[...] = jnp.zeros_like(s_ref)

    _, incl, _ = _order_masks(rev)
    incl_bf = jnp.where(incl, 1.0, 0.0).astype(BF16)
    lb = lb_ref[0]
    heads = range(B_HEADS)
    sls = [slice(h * B_DK, (h + 1) * B_DK) for h in heads]

    n = SCAN_BLOCK // CHUNK
    rows = [pl.ds(pl.multiple_of(jnp.where(rev, (n - 1 - i) * CHUNK, i * CHUNK), CHUNK), CHUNK) for i in range(n)]
    units = [(i, h) for i in range(n) for h in heads]
    q_in, k_in, q_st, k_tail, gam, v = ({} for _ in range(6))
    for i in range(n):
        f = lb + (1.0 - lb) * _sigmoid(f_ref[0, rows[i], :])
        logf = jnp.log(f)
        kf = 1.0 - f
        g_incl = _dot_split(incl_bf, logf)
        g_tot = jnp.sum(logf, axis=0, keepdims=True)
        g_mid = g_incl[CHUNK // 2 - 1:CHUNK // 2, :]
        q = _silu(q_ref[0, rows[i], :])
        v_i = i_ref[0, rows[i], :]
        q_in_i = q * jnp.exp(g_incl - g_mid)
        k_in_i = kf * jnp.exp(g_mid - g_incl)
        q_st_i = q_in_i * jnp.exp(g_mid)
        k_tail_i = k_in_i * jnp.exp(g_tot - g_mid)
        for h in heads:
            q_in[i, h], k_in[i, h] = q_in_i[:, sls[h]], k_in_i[:, sls[h]]
            q_st[i, h], k_tail[i, h] = q_st_i[:, sls[h]], k_tail_i[:, sls[h]]
            gam[i, h] = jnp.exp(g_tot)[:, sls[h]]
            v[i, h] = v_i[:, sls[h]]
    att = {u: jnp.where(incl, _dot_nt(q_in[u], k_in[u]), 0.0) for u in units}
    o_loc = {u: _dot(att[u], v[u]) for u in units}
    kv = {u: _dot_tn(v[u], k_tail[u]) for u in units}

    s = [s_ref[h] for h in heads]
    for i in range(n):
        for h in heads:
            o_ref[0, 0, rows[i], sls[h]] = o_loc[i, h] + _dot_nt(q_st[i, h], s[h])
            s[h] = s[h] * gam[i, h] + kv[i, h]
    for h in heads:
        s_ref[h] = s[h]


def _hgrn_scan(p, lb, ctx_len):
    b, s, _ = p.shape
    nc, ncc = s // SCAN_BLOCK, ctx_len // SCAN_BLOCK

    def sec(idx):
        return pl.BlockSpec((1, SCAN_BLOCK, B_W), lambda i, d, j: (i, _chunk_order(d, j, ncc, nc), idx))

    return pl.pallas_call(
        _hgrn_chunk_kernel,
        grid=(b, 2, nc),
        in_specs=[sec(0), sec(1),
                  pl.BlockSpec((1, SCAN_BLOCK, B_W), lambda i, d, j: (i, _chunk_order(d, j, ncc, nc), 2 + d)),
                  pl.BlockSpec((1, 1, B_W), lambda i, d, j: (d, 0, 0))],
        out_specs=pl.BlockSpec((1, 1, SCAN_BLOCK, B_W), lambda i, d, j: (i, d, _chunk_order(d, j, ncc, nc), 0)),
        out_shape=jax.ShapeDtypeStruct((b, 2, s, B_W), F32),
        scratch_shapes=[pltpu.VMEM((B_HEADS, B_DK, B_DK), F32)],
        compiler_params=_cparams(("parallel", "parallel", "arbitrary")),
        name="hgrn_scan",
    )(p, p, p, lb)


def _rec_out_kernel(x_ref, modl_ref, modc_ref, oa_ref, g_ref, bonus_ref, ob_ref, gate_ref, lnw_ref, lnb_ref,
                    hgn_ref, ones_a_ref, ones_b_ref, w_ref, o_ref, *, ctx_len, tm):
    pos0 = pl.program_id(1) * tm
    oa = oa_ref[0, 0] + oa_ref[0, 1]
    ones_a = ones_a_ref[...]
    mean = _segsum(oa, ones_a) * (1.0 / A_HD)
    cen = oa - mean
    var = _segsum(cen * cen, ones_a) * (1.0 / A_HD)
    ya = (cen * lax.rsqrt(var + GN_EPS) * lnw_ref[...] + lnb_ref[...] + bonus_ref[0]) * g_ref[0]
    ob = ob_ref[0, 0] + ob_ref[0, 1]
    ms = _segsum(ob * ob, ones_b_ref[...]) * (1.0 / B_DK)
    yb = ob * lax.rsqrt(ms + NORM_EPS) * hgn_ref[...] * _silu(gate_ref[0])
    y = _dot(ya, w_ref[:A_W, :]) + _dot(yb, w_ref[A_W:, :])
    gate = _gate_vec(modl_ref, modc_ref, 2, pos0, tm, ctx_len)
    o_ref[0] = x_ref[0] + gate * y


def _rec_out(xcat, mods, ctx_row, oa, g_bonus, ob, p_hgrn, ln_w, ln_b, hg_norm, w_out, ctx_len):
    b, s, d = xcat.shape
    tm = TOK_TILE
    row = lambda a: a.reshape(1, -1)
    full = lambda a: pl.BlockSpec(a.shape, lambda i, j: (0,) * a.ndim)
    consts = [row(ln_w), row(ln_b), row(jnp.tile(hg_norm, B_HEADS)), _block_ones(A_W, A_HD),
              _block_ones(B_W, B_DK), w_out]
    return pl.pallas_call(
        functools.partial(_rec_out_kernel, ctx_len=ctx_len, tm=tm),
        grid=(b, s // tm),
        in_specs=[pl.BlockSpec((1, tm, d), lambda i, j: (i, j, 0))] + _mod_specs(d, ctx_row) + [
            pl.BlockSpec((1, 2, tm, A_W), lambda i, j: (i, 0, j, 0)),
            pl.BlockSpec((1, tm, A_W), lambda i, j: (i, j, 0)),
            pl.BlockSpec((1, tm, A_W), lambda i, j: (i, j, 1)),
            pl.BlockSpec((1, 2, tm, B_W), lambda i, j: (i, 0, j, 0)),
            pl.BlockSpec((1, tm, B_W), lambda i, j: (i, j, 4)),
        ] + [full(a) for a in consts],
        out_specs=pl.BlockSpec((1, tm, d), lambda i, j: (i, j, 0)),
        out_shape=jax.ShapeDtypeStruct((b, s, d), F32),
        compiler_params=_cparams(("parallel", "parallel")),
        name="rec_out",
    )(xcat, mods, mods, oa, g_bonus, g_bonus, ob, p_hgrn, *consts)


def _res_proj_kernel(x_ref, modl_ref, modc_ref, y_ref, w_ref, o_ref, *, ctx_len, tm):
    pos0 = pl.program_id(1) * tm
    gate = _gate_vec(modl_ref, modc_ref, 2, pos0, tm, ctx_len)
    o_ref[0] = x_ref[0] + gate * _dot(y_ref[0], w_ref[...])


def _res_proj(x, mods, ctx_row, y, w, ctx_len, x_skip=0):
    b, s, k = y.shape
    d = x.shape[-1]
    tm = TOK_TILE
    skip = x_skip // tm
    return pl.pallas_call(
        functools.partial(_res_proj_kernel, ctx_len=ctx_len, tm=tm),
        grid=(b, s // tm),
        in_specs=[pl.BlockSpec((1, tm, d), lambda i, j: (i, j + skip, 0))] + _mod_specs(d, ctx_row) + [
            pl.BlockSpec((1, tm, k), lambda i, j: (i, j, 0)),
            pl.BlockSpec((k, d), lambda i, j: (0, 0))],
        out_specs=pl.BlockSpec((1, tm, d), lambda i, j: (i, j, 0)),
        out_shape=jax.ShapeDtypeStruct((b, s, d), F32),
        compiler_params=_cparams(("parallel", "parallel")),
        name="res_proj",
    )(x, mods, mods, y, w)


def _ffn_kernel(x_ref, modl_ref, modc_ref, g_ref, wg_ref, wu_ref, wd_ref, o_ref, h_ref, acc_ref, *, ctx_len, tm,
                n_f):
    f = pl.program_id(2)
    pos0 = pl.program_id(1) * tm

    @pl.when(f == 0)
    def _():
        h_ref[...] = _bf(_modulate(x_ref[0], g_ref[...], modl_ref, modc_ref, 3, pos0, ctx_len))
        acc_ref[...] = jnp.zeros_like(acc_ref)

    h = h_ref[...]
    act = _silu(jnp.dot(h, wg_ref[...], preferred_element_type=F32)) * jnp.dot(h, wu_ref[...],
                                                                               preferred_element_type=F32)
    acc_ref[...] += _dot(act, wd_ref[...])

    @pl.when(f == n_f - 1)
    def _():
        o_ref[0] = x_ref[0] + _gate_vec(modl_ref, modc_ref, 5, pos0, tm, ctx_len) * acc_ref[...]


def _ffn(x, mods, ctx_row, g, wg, wu, wd, ctx_len, tm, tf):
    b, s, d = x.shape
    ff = wg.shape[1]
    n_f = ff // tf
    return pl.pallas_call(
        functools.partial(_ffn_kernel, ctx_len=ctx_len, tm=tm, n_f=n_f),
        grid=(b, s // tm, n_f),
        in_specs=[pl.BlockSpec((1, tm, d), lambda i, j, f: (i, j, 0))] + _mod_specs(d, ctx_row) + [
            pl.BlockSpec((1, d), lambda i, j, f: (0, 0)),
            pl.BlockSpec((d, tf), lambda i, j, f: (0, f)),
            pl.BlockSpec((d, tf), lambda i, j, f: (0, f)),
            pl.BlockSpec((tf, d), lambda i, j, f: (f, 0))],
        out_specs=pl.BlockSpec((1, tm, d), lambda i, j, f: (i, j, 0)),
        out_shape=jax.ShapeDtypeStruct((b, s, d), F32),
        scratch_shapes=[pltpu.VMEM((tm, d), BF16), pltpu.VMEM((tm, d), F32)],
        compiler_params=_cparams(("parallel", "parallel", "arbitrary")),
        name="ffn",
    )(x, mods, mods, g.reshape(1, d), wg, wu, wd)


def _attn_kernel(sink_ref, q_ref, kc_ref, vc_ref, kp_ref, kq_ref, kn_ref, vp_ref, vq_ref, vn_ref, o_ref, *,
                 ctx_len, n_lat):
    i = pl.program_id(1)
    nk = ctx_len + 3 * ATT_BLOCK
    k_all = jnp.concatenate([kc_ref[0], kp_ref[0], kq_ref[0], kn_ref[0]], axis=0)
    v_all = jnp.concatenate([vc_ref[0], vp_ref[0], vq_ref[0], vn_ref[0]], axis=0)
    row = lax.broadcasted_iota(jnp.int32, (ATT_BLOCK, nk), 0)
    col = lax.broadcasted_iota(jnp.int32, (ATT_BLOCK, nk), 1)
    rel = col - (ctx_len + ATT_BLOCK)
    kabs = i * ATT_BLOCK + rel
    band = (jnp.abs(row - rel) <= WINDOW) & (kabs >= 0) & (kabs < n_lat)
    bias = jnp.where((col < ctx_len) | band, 0.0, NEG_INF)
    bias = jnp.concatenate([bias] * GQ, axis=0)
    grp = lax.broadcasted_iota(jnp.int32, (GQ * ATT_BLOCK, 1), 0) // ATT_BLOCK
    scale = HD ** -0.5
    for hk in range(HKV):
        kh = _bf(k_all[:, hk * HD:(hk + 1) * HD])
        vh = _bf(v_all[:, hk * HD:(hk + 1) * HD])
        heads = [hk * GQ + g for g in range(GQ)]
        q = jnp.concatenate([q_ref[0, :, hq * HD:(hq + 1) * HD] for hq in heads], axis=0) * scale
        sk = jnp.zeros((GQ * ATT_BLOCK, 1), F32)
        for g, hq in enumerate(heads):
            sk = jnp.where(grp == g, sink_ref[hq], sk)
        s = _dot_nt(q, kh) + bias
        m = jnp.maximum(jnp.max(s, axis=-1, keepdims=True), sk)
        p = jnp.exp(s - m)
        den = jnp.sum(p, axis=-1, keepdims=True) + jnp.exp(sk - m)
        o = jnp.dot(_bf(p), vh, preferred_element_type=F32) / den
        for g, hq in enumerate(heads):
            o_ref[0, :, hq * HD:(hq + 1) * HD] = o[g * ATT_BLOCK:(g + 1) * ATT_BLOCK]


def _attention(qkv, sink, ctx_len):
    b, s, _ = qkv.shape
    n_lat = s - ctx_len
    nb = n_lat // ATT_BLOCK
    cb = ctx_len // ATT_BLOCK
    kcol = Q_COLS // KV_COLS
    vcol = kcol + 1

    def band(colblk, shift):
        return pl.BlockSpec((1, ATT_BLOCK, KV_COLS),
                            lambda bi, i: (bi, cb + jnp.clip(i + shift, 0, nb - 1), colblk))

    return pl.pallas_call(
        functools.partial(_attn_kernel, ctx_len=ctx_len, n_lat=n_lat),
        grid=(b, nb),
        in_specs=[pl.BlockSpec(memory_space=pltpu.SMEM),
                  pl.BlockSpec((1, ATT_BLOCK, Q_COLS), lambda bi, i: (bi, cb + i, 0)),
                  pl.BlockSpec((1, ctx_len, KV_COLS), lambda bi, i: (bi, 0, kcol)),
                  pl.BlockSpec((1, ctx_len, KV_COLS), lambda bi, i: (bi, 0, vcol)),
                  band(kcol, -1), band(kcol, 0), band(kcol, 1),
                  band(vcol, -1), band(vcol, 0), band(vcol, 1)],
        out_specs=pl.BlockSpec((1, ATT_BLOCK, Q_COLS), lambda bi, i: (bi, i, 0)),
        out_shape=jax.ShapeDtypeStruct((b, n_lat, Q_COLS), F32),
        compiler_params=_cparams(("parallel", "parallel")),
        name="attention",
    )(sink, qkv, qkv, qkv, qkv, qkv, qkv, qkv, qkv, qkv)


def _router_kernel(x_ref, modl_ref, modc_ref, g_ref, w_ref, b_ref, o_ref, e_ref, f_ref, *, tm):
    f = _modulate(x_ref[0], g_ref[...], modl_ref, modc_ref, 3, 0, 0)
    f_ref[0] = f
    logits = _dot_f32(f, w_ref[...]) + b_ref[...]
    lane = lax.broadcasted_iota(jnp.int32, logits.shape, 1).astype(F32)
    logits = jnp.where(lane < N_EXPERTS, logits, NEG_INF)
    ex = jnp.exp(logits - jnp.max(logits, axis=-1, keepdims=True))
    probs = ex / jnp.sum(ex, axis=-1, keepdims=True)
    p1 = jnp.max(probs, axis=-1, keepdims=True)
    i1 = jnp.min(jnp.where(probs == p1, lane, float(LANES)), axis=-1, keepdims=True)
    rest = jnp.where(lane == i1, -1.0, probs)
    p2 = jnp.max(rest, axis=-1, keepdims=True)
    i2 = jnp.min(jnp.where(rest == p2, lane, float(LANES)), axis=-1, keepdims=True)
    tot = p1 + p2
    o_ref[0] = jnp.where(lane == 0.0, p1 / tot, jnp.where(lane == 1.0, p2 / tot, 0.0))
    e_ref[0] = jnp.where(lane == 0.0, i1, jnp.where(lane == 1.0, i2, 0.0)).astype(jnp.int32)


def _router(x, mods, g, w, bias):
    b, s, d = x.shape
    tm = TOK_TILE
    wpad = jnp.zeros((d, LANES), F32).at[:, :N_EXPERTS].set(w)
    bpad = jnp.zeros((1, LANES), F32).at[0, :N_EXPERTS].set(bias)
    lane_spec = pl.BlockSpec((1, tm, LANES), lambda i, j: (i, j, 0))
    return pl.pallas_call(
        functools.partial(_router_kernel, tm=tm),
        grid=(b, s // tm),
        in_specs=[pl.BlockSpec((1, tm, d), lambda i, j: (i, j, 0))] + _mod_specs(d, 0) + [
            pl.BlockSpec((1, d), lambda i, j: (0, 0)),
            pl.BlockSpec((d, LANES), lambda i, j: (0, 0)),
            pl.BlockSpec((1, LANES), lambda i, j: (0, 0))],
        out_specs=[lane_spec, lane_spec, pl.BlockSpec((1, tm, d), lambda i, j: (i, j, 0))],
        out_shape=[jax.ShapeDtypeStruct((b, s, LANES), F32), jax.ShapeDtypeStruct((b, s, LANES), jnp.int32),
                   jax.ShapeDtypeStruct((b, s, d), F32)],
        compiler_params=_cparams(("parallel", "parallel")),
        name="router",
    )(x, mods, mods, g.reshape(1, d), wpad, bpad)


MOE_ROWS = 256


def _row_scatter_kernel(dest_ref, pad_ref, f_ref, xs_hbm, zero_ref, sem, *, tm, n_pad):
    base = pl.program_id(0) * tm

    @pl.when(pl.program_id(0) == 0)
    def _():
        zero_ref[...] = jnp.zeros_like(zero_ref)

        def zero_row(r, carry):
            pltpu.make_async_copy(zero_ref, xs_hbm.at[pl.ds(pad_ref[r], 1)], sem).start()
            return carry

        lax.fori_loop(0, n_pad, zero_row, 0, unroll=8)
        for _ in range(n_pad // tm):
            pltpu.make_async_copy(f_ref, xs_hbm.at[pl.ds(0, tm)], sem).wait()

    def issue(r, carry):
        slot = 2 * (base + r)
        pltpu.make_async_copy(f_ref.at[pl.ds(r, 1)], xs_hbm.at[pl.ds(dest_ref[slot], 1)], sem).start()
        pltpu.make_async_copy(f_ref.at[pl.ds(r, 1)], xs_hbm.at[pl.ds(dest_ref[slot + 1], 1)], sem).start()
        return carry

    lax.fori_loop(0, tm, issue, 0, unroll=8)
    for _ in range(2):
        pltpu.make_async_copy(f_ref, xs_hbm.at[pl.ds(0, tm)], sem).wait()


def _row_scatter(f, dest, pad_rows, n_rows):
    n_tok, d = f.shape
    tm = TOK_TILE
    n_pad = pad_rows.shape[0]
    assert n_pad % tm == 0 and 2 * n_tok + n_pad == n_rows
    return pl.pallas_call(
        functools.partial(_row_scatter_kernel, tm=tm, n_pad=n_pad),
        grid_spec=pltpu.PrefetchScalarGridSpec(
            num_scalar_prefetch=2,
            grid=(n_tok // tm,),
            in_specs=[pl.BlockSpec((tm, d), lambda i, dr, pr: (i, 0))],
            out_specs=pl.BlockSpec(memory_space=pl.ANY),
            scratch_shapes=[pltpu.VMEM((1, d), f.dtype), pltpu.SemaphoreType.DMA(())]),
        out_shape=jax.ShapeDtypeStruct((n_rows, d), f.dtype),
        compiler_params=_cparams(("arbitrary",)),
        name="moe_scatter",
    )(dest, pad_rows, f)


def _expert_kernel(be_ref, x_ref, wg_ref, wu_ref, wd_ref, o_ref):
    h = _bf(x_ref[...])
    act = _silu(jnp.dot(h, wg_ref[0], preferred_element_type=F32)) * jnp.dot(h, wu_ref[0],
                                                                              preferred_element_type=F32)
    o_ref[...] = _dot(act, wd_ref[0])


def _experts(xs, block_e, wg, wu, wd):
    n_rows, d = xs.shape
    ff = wg.shape[2]
    rows = MOE_ROWS
    return pl.pallas_call(
        _expert_kernel,
        grid_spec=pltpu.PrefetchScalarGridSpec(
            num_scalar_prefetch=1,
            grid=(n_rows // rows,),
            in_specs=[pl.BlockSpec((rows, d), lambda i, be: (i, 0)),
                      pl.BlockSpec((1, d, ff), lambda i, be: (be[i], 0, 0)),
                      pl.BlockSpec((1, d, ff), lambda i, be: (be[i], 0, 0)),
                      pl.BlockSpec((1, ff, d), lambda i, be: (be[i], 0, 0))],
            out_specs=pl.BlockSpec((rows, d), lambda i, be: (i, 0))),
        out_shape=jax.ShapeDtypeStruct((n_rows, d), F32),
        compiler_params=_cparams(("arbitrary",)),
        name="moe_experts",
    )(block_e, xs, wg, wu, wd)


def _combine_kernel(dest_ref, x_ref, modl_ref, modc_ref, w_ref, gfin_ref, ys_hbm, o_ref, y1_ref, y2_ref, sem, *,
                    tm, tiles_per_row):
    tile = pl.program_id(0) * tiles_per_row + pl.program_id(1)
    n_tiles = pl.num_programs(0) * tiles_per_row
    buf = tile % 2

    def fetch(t, b):
        def issue(r, carry):
            slot = 2 * (t * tm + r)
            pltpu.make_async_copy(ys_hbm.at[pl.ds(dest_ref[slot], 1)], y1_ref.at[b, pl.ds(r, 1)], sem.at[b]).start()
            pltpu.make_async_copy(ys_hbm.at[pl.ds(dest_ref[slot + 1], 1)], y2_ref.at[b, pl.ds(r, 1)],
                                  sem.at[b]).start()
            return carry

        lax.fori_loop(0, tm, issue, 0, unroll=8)

    @pl.when(tile == 0)
    def _():
        fetch(0, 0)

    @pl.when(tile + 1 < n_tiles)
    def _():
        fetch(tile + 1, 1 - buf)

    pltpu.make_async_copy(ys_hbm.at[pl.ds(0, tm)], y1_ref.at[buf], sem.at[buf]).wait()
    pltpu.make_async_copy(ys_hbm.at[pl.ds(0, tm)], y2_ref.at[buf], sem.at[buf]).wait()
    w = w_ref[0]
    moe = w[:, 0:1] * y1_ref[buf] + w[:, 1:2] * y2_ref[buf]
    y = x_ref[0] + _gate_vec(modl_ref, modc_ref, 5, 0, tm, 0) * moe
    o_ref[0] = y * lax.rsqrt(jnp.mean(y * y, axis=-1, keepdims=True) + NORM_EPS) * gfin_ref[...]


def _combine(x, mods, tokw, dest, ys, final_g):
    b, s, d = x.shape
    tm = TOK_TILE
    tiles = s // tm
    return pl.pallas_call(
        functools.partial(_combine_kernel, tm=tm, tiles_per_row=tiles),
        grid_spec=pltpu.PrefetchScalarGridSpec(
            num_scalar_prefetch=1,
            grid=(b, tiles),
            in_specs=[pl.BlockSpec((1, tm, d), lambda i, j, dr: (i, j, 0)),
                      pl.BlockSpec((1, N_MOD, d), lambda i, j, dr: (i, 0, 0)),
                      pl.BlockSpec((1, N_MOD, d), lambda i, j, dr: (0, 0, 0)),
                      pl.BlockSpec((1, tm, LANES), lambda i, j, dr: (i, j, 0)),
                      pl.BlockSpec((1, d), lambda i, j, dr: (0, 0)),
                      pl.BlockSpec(memory_space=pl.ANY)],
            out_specs=pl.BlockSpec((1, tm, d), lambda i, j, dr: (i, j, 0)),
            scratch_shapes=[pltpu.VMEM((2, tm, d), F32), pltpu.VMEM((2, tm, d), F32),
                            pltpu.SemaphoreType.DMA((2,))]),
        out_shape=jax.ShapeDtypeStruct((b, s, d), F32),
        compiler_params=_cparams(("arbitrary", "arbitrary")),
        name="moe_combine",
    )(dest, x, mods, mods, tokw, final_g.reshape(1, d), ys)


def _moe_routing(eidx):
    m = eidx.shape[0]
    oh = (eidx[:, None] == jnp.arange(N_EXPERTS, dtype=jnp.int32)[None, :]).astype(jnp.int32)
    csum = jnp.cumsum(oh, axis=0)
    rank = jnp.sum((csum - oh) * oh, axis=1)
    counts = csum[-1]
    padded = (counts + MOE_ROWS - 1) // MOE_ROWS * MOE_ROWS
    pad_end = jnp.cumsum(padded)
    dest = jnp.sum(oh * (pad_end - padded)[None, :], axis=1) + rank
    n_blocks = m // MOE_ROWS + N_EXPERTS
    starts = jnp.arange(n_blocks, dtype=jnp.int32) * MOE_ROWS
    block_e = jnp.minimum(jnp.sum((starts[:, None] >= pad_end[None, :]).astype(jnp.int32), axis=1), N_EXPERTS - 1)
    pad_cnt = padded - counts
    pad_cum = jnp.cumsum(pad_cnt)
    idx = jnp.arange(N_EXPERTS * MOE_ROWS, dtype=jnp.int32)
    owner = jnp.sum((idx[:, None] >= pad_cum[None, :]).astype(jnp.int32), axis=1)
    own = (jnp.minimum(owner, N_EXPERTS - 1)[:, None] == jnp.arange(N_EXPERTS)[None, :]).astype(jnp.int32)
    in_expert = jnp.sum(own * (pad_end - pad_cnt)[None, :], axis=1) + idx - jnp.sum(
        own * (pad_cum - pad_cnt)[None, :], axis=1)
    pad_rows = jnp.where(owner < N_EXPERTS, in_expert, pad_end[-1] + idx - pad_cum[-1])
    return dest.astype(jnp.int32), block_e, pad_rows.astype(jnp.int32), n_blocks * MOE_ROWS


def _rope_tables(n_lat, ctx_len):
    rows = n_lat // GRID_W
    row = jnp.repeat(jnp.arange(rows, dtype=F32), GRID_W)
    col = jnp.tile(jnp.arange(GRID_W, dtype=F32), rows)
    inv = ROPE_BASE ** (-jnp.arange(0, AX_DIM, 2, dtype=F32) / AX_DIM)
    ar, ac = row[:, None] * inv, col[:, None] * inv
    cos = jnp.concatenate([jnp.cos(ar), jnp.cos(ar), jnp.cos(ac), jnp.cos(ac)], axis=-1)
    sin = jnp.concatenate([-jnp.sin(ar), jnp.sin(ar), -jnp.sin(ac), jnp.sin(ac)], axis=-1)
    n_heads = ROPE_COLS // HD
    cos = jnp.concatenate([jnp.ones((ctx_len, HD), F32), cos], axis=0)
    sin = jnp.concatenate([jnp.zeros((ctx_len, HD), F32), sin], axis=0)
    half = AX_DIM // 2
    j = jnp.arange(HD)
    partner = jnp.where((j % AX_DIM) < half, j + half, j - half)
    perm = (jnp.arange(n_heads)[:, None] * HD + partner[None, :]).reshape(-1)
    return jnp.tile(cos, (1, n_heads)), jnp.tile(sin, (1, n_heads)), perm


def kernel(x, c, ctx, c_ctx, mod_w, mod_b, norm_mix, norm_ffn, norm_final, rec_w_in, rec_w_out, rwkv_mu, rwkv_w0, rwkv_w_up, rwkv_a0, rwkv_a_up, rwkv_g_up, rwkv_k_k, rwkv_k_a, rwkv_r_k, rwkv_ln_w, rwkv_ln_b, hgrn_lb, hgrn_norm, ffn_w_gate, ffn_w_up, ffn_w_down, att_w_in, att_w_out, att_sink, moe_router, moe_router_b, moe_w_gate, moe_w_up, moe_w_down):
    bsz, n_lat, d = x.shape
    ctx_len = ctx.shape[1]
    xcat = jnp.concatenate([ctx, x], axis=1)

    n_rows = -(-(bsz + 1) // 8) * 8
    cvec = jnp.zeros((n_rows, d), F32).at[:bsz].set(c).at[bsz].set(c_ctx)
    mods = [_adaln(cvec, mod_w[l], mod_b[l]).reshape(n_rows, N_MOD, d) for l in range(2)]

    ops, vv, g_bonus, gam, p_hgrn = _rec_in(xcat, mods[0], bsz, norm_mix[0], _bf(rec_w_in[0]), rwkv_mu[0], rwkv_w0[0],
                                            rwkv_w_up[0], rwkv_a0[0], rwkv_a_up[0], rwkv_g_up[0], rwkv_k_k[0],
                                            rwkv_k_a[0], rwkv_r_k[0].reshape(-1), ctx_len)
    oa = _rwkv_scan(ops, vv, gam, ctx_len)
    lb = jnp.cumsum(jax.nn.softmax(hgrn_lb.astype(F32), axis=1), axis=1)[:, 0].reshape(2, 1, B_W)
    ob = _hgrn_scan(p_hgrn, lb, ctx_len)
    xcat = _rec_out(xcat, mods[0], bsz, oa, g_bonus, ob, p_hgrn, rwkv_ln_w[0], rwkv_ln_b[0], hgrn_norm[0],
                    _bf(rec_w_out[0]), ctx_len)
    xcat = _ffn(xcat, mods[0], bsz, norm_ffn[0], _bf(ffn_w_gate[0]), _bf(ffn_w_up[0]), _bf(ffn_w_down[0]), ctx_len,
                tm=384, tf=2816)

    cos, sin, perm = _rope_tables(n_lat, ctx_len)
    w_att = att_w_in[0]
    qkv = _proj_rope(xcat, mods[1], bsz, norm_mix[1], _bf(w_att), _bf(w_att[:, perm]), cos, sin, ctx_len)
    att = _attention(qkv, att_sink[0], ctx_len)
    x_lat = _res_proj(xcat, mods[1], 0, att, _bf(att_w_out[0]), 0, x_skip=ctx_len)
    tokw, eidx, f_lat = _router(x_lat, mods[1], norm_ffn[1], moe_router[0], moe_router_b[0])
    dest, block_e, pad_rows, n_rows = _moe_routing(eidx[..., :2].reshape(-1))
    xs = _row_scatter(f_lat.reshape(bsz * n_lat, d), dest, pad_rows, n_rows)
    ys = _experts(xs, block_e, _bf(moe_w_gate[0]), _bf(moe_w_up[0]), _bf(moe_w_down[0]))
    return _combine(x_lat, mods[1], tokw, dest, ys, norm_final)
```

```python
import functools

import jax
import jax.numpy as jnp
from jax import lax
from jax.experimental import pallas as pl
from jax.experimental.pallas import tpu as pltpu

F32 = jnp.float32
BF16 = jnp.bfloat16

N_MOD = 6
NORM_EPS = 1e-6
NEG_INF = -1e30

A_HEADS = 8
A_HD = 64
A_W = A_HEADS * A_HD
DECAY_LORA = 64
AAA_LORA = 64
GATE_LORA = 128
RWKV_COLS = 3 * A_W + DECAY_LORA + AAA_LORA + GATE_LORA
GN_EPS = 64e-5

B_HEADS = 4
B_DK = 128
B_W = B_HEADS * B_DK
HGRN_COLS = 5 * B_W

HQ = 16
HKV = 4
GQ = HQ // HKV
HD = 64
WINDOW = 128
ATT_BLOCK = 128
ATT_STACK = GQ
AX_DIM = HD // 2
ROPE_BASE = 10000.0
GRID_W = 64
Q_COLS = HQ * HD
KV_COLS = HKV * HD
ROPE_COLS = Q_COLS + KV_COLS
ATT_COLS = Q_COLS + 2 * KV_COLS

N_EXPERTS = 8
LANES = 128
CHUNK = 64
SCAN_BLOCK = 256
MXU_WIDTH = 256
RWKV_GROUP = MXU_WIDTH // A_HD
TOK_TILE = 256
VMEM_LIMIT = 56 * 1024 * 1024

OP_KAP, OP_RT, OP_KBAR, OP_BBAR, OP_KGAM, OP_BGAM = range(6)
N_OPS = 6


def _cparams(sem):
    return pltpu.CompilerParams(dimension_semantics=sem, vmem_limit_bytes=VMEM_LIMIT)


def _bf(x):
    return x.astype(BF16)


def _dot(a, b):
    return jnp.dot(_bf(a), _bf(b), preferred_element_type=F32)


def _dot_nt(a, b):
    return lax.dot_general(_bf(a), _bf(b), (((1,), (1,)), ((), ())), preferred_element_type=F32)


def _dot_tn(a, b):
    return lax.dot_general(_bf(a), _bf(b), (((0,), (0,)), ((), ())), preferred_element_type=F32)


def _dot_f32(a, b):
    a_hi, b_hi = _bf(a), _bf(b)
    a_lo, b_lo = _bf(a - a_hi.astype(F32)), _bf(b - b_hi.astype(F32))
    acc = jnp.dot(a_hi, b_hi, preferred_element_type=F32)
    acc += jnp.dot(a_hi, b_lo, preferred_element_type=F32)
    acc += jnp.dot(a_lo, b_hi, preferred_element_type=F32)
    return acc


def _sigmoid(x):
    return 1.0 / (1.0 + jnp.exp(-x))


def _silu(x):
    return x * _sigmoid(x)


def _segsum(x, ones_bd):
    hi = _bf(x)
    lo = _bf(x - hi.astype(F32))
    return jnp.dot(hi, ones_bd, preferred_element_type=F32) + jnp.dot(lo, ones_bd, preferred_element_type=F32)


def _dot_split(a, x):
    hi = _bf(x)
    r1 = x - hi.astype(F32)
    mid = _bf(r1)
    lo = _bf(r1 - mid.astype(F32))
    acc = jnp.dot(a, hi, preferred_element_type=F32)
    acc += jnp.dot(a, mid, preferred_element_type=F32)
    acc += jnp.dot(a, lo, preferred_element_type=F32)
    return acc


def _block_ones(width, seg):
    i = jnp.arange(width) // seg
    return (i[:, None] == i[None, :]).astype(BF16)


def _modulate(x, g, modl_ref, modc_ref, row, pos0, ctx_len, pos=None):
    y = x * lax.rsqrt(jnp.mean(x * x, axis=-1, keepdims=True) + NORM_EPS) * g
    if pos is None:
        pos = pos0 + lax.broadcasted_iota(jnp.int32, (x.shape[0], 1), 0)
    is_ctx = pos < ctx_len
    shift = jnp.where(is_ctx, modc_ref[0, row:row + 1, :], modl_ref[0, row:row + 1, :])
    scale = jnp.where(is_ctx, modc_ref[0, row + 1:row + 2, :], modl_ref[0, row + 1:row + 2, :])
    return y * (1.0 + scale) + shift


def _gate_vec(modl_ref, modc_ref, row, pos0, n, ctx_len):
    pos = pos0 + lax.broadcasted_iota(jnp.int32, (n, 1), 0)
    return jnp.where(pos < ctx_len, modc_ref[0, row:row + 1, :], modl_ref[0, row:row + 1, :])


def _mod_specs(d, ctx_row, batch_axis=0):
    def lat_map(*idx):
        return (idx[batch_axis], 0, 0)

    def ctx_map(*idx):
        return (ctx_row, 0, 0)

    return [pl.BlockSpec((1, N_MOD, d), lat_map), pl.BlockSpec((1, N_MOD, d), ctx_map)]


def _adaln_kernel(c_ref, w_ref, b_ref, o_ref):
    o_ref[...] = _dot(_silu(c_ref[...]), w_ref[...]) + b_ref[...]


def _adaln(cvec, w, b):
    r, d = cvec.shape
    n = w.shape[1]
    tn = 1024
    return pl.pallas_call(
        _adaln_kernel,
        grid=(n // tn,),
        in_specs=[pl.BlockSpec((r, d), lambda j: (0, 0)),
                  pl.BlockSpec((d, tn), lambda j: (0, j)),
                  pl.BlockSpec((1, tn), lambda j: (0, j))],
        out_specs=pl.BlockSpec((r, tn), lambda j: (0, j)),
        out_shape=jax.ShapeDtypeStruct((r, n), F32),
        compiler_params=_cparams(("parallel",)),
        name="adaln",
    )(cvec, w, b.reshape(1, n))


def _proj_rope_kernel(x_ref, modl_ref, modc_ref, g_ref, w_ref, wrot_ref, cos_ref, sin_ref, o_ref, *, ctx_len, tm,
                      rope_cols):
    pos0 = pl.program_id(1) * tm
    h = _bf(_modulate(x_ref[0], g_ref[...], modl_ref, modc_ref, 0, pos0, ctx_len))
    y = jnp.dot(h, w_ref[...], preferred_element_type=F32)
    yr = jnp.dot(h, wrot_ref[...], preferred_element_type=F32)
    o_ref[0, :, :rope_cols] = y[:, :rope_cols] * cos_ref[...] + yr * sin_ref[...]
    o_ref[0, :, rope_cols:] = y[:, rope_cols:]


def _proj_rope(xcat, mods, ctx_row, g, w, wrot, cos, sin, ctx_len):
    b, s, d = xcat.shape
    n = w.shape[1]
    tm = TOK_TILE
    rope_cols = wrot.shape[1]
    return pl.pallas_call(
        functools.partial(_proj_rope_kernel, ctx_len=ctx_len, tm=tm, rope_cols=rope_cols),
        grid=(b, s // tm),
        in_specs=[pl.BlockSpec((1, tm, d), lambda i, j: (i, j, 0))] + _mod_specs(d, ctx_row) + [
            pl.BlockSpec((1, d), lambda i, j: (0, 0)),
            pl.BlockSpec((d, n), lambda i, j: (0, 0)),
            pl.BlockSpec((d, rope_cols), lambda i, j: (0, 0)),
            pl.BlockSpec((tm, rope_cols), lambda i, j: (j, 0)),
            pl.BlockSpec((tm, rope_cols), lambda i, j: (j, 0))],
        out_specs=pl.BlockSpec((1, tm, n), lambda i, j: (i, j, 0)),
        out_shape=jax.ShapeDtypeStruct((b, s, n), F32),
        compiler_params=_cparams(("parallel", "parallel")),
        name="proj_rope",
    )(xcat, mods, mods, g.reshape(1, d), w, wrot, cos, sin)


def _softplus(x):
    return jnp.maximum(x, 0.0) + jnp.log(1.0 + jnp.exp(-jnp.abs(x)))


def _stream_specs(tm, d, ctx_len):
    assert ctx_len == tm
    return [pl.BlockSpec((1, tm, d), lambda i, j: (i, 0, 0)),
            pl.BlockSpec((1, tm, d), lambda i, j: (i, jnp.maximum(j - 1, 0), 0))]


def _stream_tile(c_ref, x_ref):
    return jnp.where(pl.program_id(1) == 0, c_ref[0], x_ref[0])


def _rec_in_kernel(c_ref, x_ref, xp_ref, xn_ref, modl_ref, modc_ref, g_ref, w_ref, mu_ref, w0_ref, wup_ref, a0_ref,
                   aup_ref, gup_ref, kk_ref, ka_ref, rk_ref, ones_ref, ops_ref, v_ref, gb_ref, gam_ref, ph_ref, *,
                   ctx_len, seq_len, tm):
    pos0 = pl.program_id(1) * tm
    gvec = g_ref[...]
    h = _bf(_modulate(_stream_tile(c_ref, x_ref), gvec, modl_ref, modc_ref, 0, pos0, ctx_len))
    p = jnp.dot(h, w_ref[:, :RWKV_COLS], preferred_element_type=F32)

    hgrn_pieces = [(RWKV_COLS + c, min(B_W, HGRN_COLS - c)) for c in range(0, HGRN_COLS, B_W)]

    def hgrn_piece():
        if hgrn_pieces:
            lo, width = hgrn_pieces.pop(0)
            ph_ref[0, :, lo - RWKV_COLS:lo - RWKV_COLS + width] = jnp.dot(h, w_ref[:, lo:lo + width],
                                                                          preferred_element_type=F32)

    halo = lax.broadcasted_iota(jnp.int32, (16, 1), 0)
    halo_pos = jnp.where(halo < 8, pos0 - 8 + halo, pos0 + tm - 8 + halo)
    xh = jnp.concatenate([xp_ref[0], xn_ref[0]], axis=0)
    p_halo = jnp.dot(_bf(_modulate(xh, gvec, modl_ref, modc_ref, 0, 0, ctx_len, pos=halo_pos)),
                     w_ref[:, :RWKV_COLS], preferred_element_type=F32)
    rows = lax.broadcasted_iota(jnp.int32, (tm, 1), 0)
    starts_seq = (pos0 == 0) | (pos0 == ctx_len)
    ends_seq = (pos0 + tm == ctx_len) | (pos0 + tm == seq_len)
    prev_halo = jnp.where(starts_seq, 0.0, p_halo[7:8, :])
    next_halo = jnp.where(ends_seq, 0.0, p_halo[8:9, :])
    prev = jnp.where(rows == 0, prev_halo, pltpu.roll(p, 1, 0))
    nxt = jnp.where(rows == tm - 1, next_halo, pltpu.roll(p, tm - 1, 0))
    p = p + mu_ref[...] * (0.5 * (prev + nxt) - p)
    hgrn_piece()

    r = p[:, 0:A_W]
    k = p[:, A_W:2 * A_W]
    v = p[:, 2 * A_W:3 * A_W]
    lo = 3 * A_W
    wd = p[:, lo:lo + DECAY_LORA]
    ad = p[:, lo + DECAY_LORA:lo + DECAY_LORA + AAA_LORA]
    gd = p[:, lo + DECAY_LORA + AAA_LORA:lo + DECAY_LORA + AAA_LORA + GATE_LORA]

    tw = jnp.tanh(wd)
    a = _sigmoid(a0_ref[...] + _dot(ad, aup_ref[...]))
    ones_bd = ones_ref[...]
    kk = k * kk_ref[...]
    kk = kk / jnp.maximum(jnp.sqrt(_segsum(kk * kk, ones_bd)), 1e-12)
    k = k * (1.0 + (a - 1.0) * ka_ref[...])
    b = kk * a
    v_ref[0] = _bf(v)
    gb_ref[0, :, :A_W] = _dot(_sigmoid(gd), gup_ref[...])
    gb_ref[0, :, A_W:] = _segsum(r * k * rk_ref[...], ones_bd) * v
    hgrn_piece()

    trow = lax.broadcasted_iota(jnp.int32, (tm, tm), 0)
    tcol = lax.broadcasted_iota(jnp.int32, (tm, tm), 1)
    same = (trow // CHUNK) == (tcol // CHUNK)
    n_chunks = tm // CHUNK
    for d in range(2):
        w_log = -_softplus(-(w0_ref[d:d + 1, :] + _dot(tw, wup_ref[d]))) - 0.5
        lw = -jnp.exp(w_log)
        before = (tcol <= trow) if d == 0 else (tcol >= trow)
        g_incl = _dot_split(jnp.where(same & before, 1.0, 0.0).astype(BF16), lw)
        last = [c * CHUNK + (CHUNK - 1 if d == 0 else 0) for c in range(n_chunks)]
        g_end = jnp.concatenate([g_incl[t:t + 1] for t in last], axis=0)
        g_tot = jnp.concatenate([jnp.broadcast_to(g_incl[t:t + 1], (CHUNK, A_W)) for t in last], axis=0)
        hgrn_piece()
        e_ng = jnp.exp(-g_incl)
        e_tail = jnp.exp(g_tot - g_incl)
        operands = {OP_KAP: kk * jnp.exp(g_incl - lw),
                    OP_RT: r * jnp.exp(g_incl),
                    OP_KBAR: k * e_ng, OP_BBAR: b * e_ng,
                    OP_KGAM: k * e_tail, OP_BGAM: b * e_tail}
        for sec, val in operands.items():
            ops_ref[0, d, :, sec * A_W:(sec + 1) * A_W] = _bf(val)
        gam_ref[0, d, 0] = jnp.exp(g_end)
        hgrn_piece()
    while hgrn_pieces:
        hgrn_piece()


def _rec_in(ctx, x, mods, ctx_row, g, w, mu, w0, w_up, a0, a_up, g_up, k_k, k_a, r_k):
    b, n_lat, d = x.shape
    ctx_len = ctx.shape[1]
    s = ctx_len + n_lat
    tm = SCAN_BLOCK
    assert n_lat % tm == 0
    nb8 = n_lat // 8
    row = lambda a: a.reshape(1, -1)
    full = lambda a: pl.BlockSpec(a.shape, lambda i, j: (0,) * a.ndim)
    args = [row(g), w, row(mu), w0, w_up, row(a0), a_up, g_up, row(k_k), row(k_a), row(r_k),
            _block_ones(A_W, A_HD)]
    return pl.pallas_call(
        functools.partial(_rec_in_kernel, ctx_len=ctx_len, seq_len=s, tm=tm),
        grid=(b, s // tm),
        in_specs=_stream_specs(tm, d, ctx_len) + [
            pl.BlockSpec((1, 8, d), lambda i, j: (i, jnp.maximum((j - 1) * (tm // 8) - 1, 0), 0)),
            pl.BlockSpec((1, 8, d), lambda i, j: (i, jnp.minimum(j * (tm // 8), nb8 - 1), 0)),
        ] + _mod_specs(d, ctx_row) + [full(a) for a in args],
        out_specs=[pl.BlockSpec((1, 2, tm, N_OPS * A_W), lambda i, j: (i, 0, j, 0)),
                   pl.BlockSpec((1, tm, A_W), lambda i, j: (i, j, 0)),
                   pl.BlockSpec((1, tm, 2 * A_W), lambda i, j: (i, j, 0)),
                   pl.BlockSpec((1, 2, 1, tm // CHUNK, A_W), lambda i, j: (i, 0, j, 0, 0)),
                   pl.BlockSpec((1, tm, HGRN_COLS), lambda i, j: (i, j, 0))],
        out_shape=[jax.ShapeDtypeStruct((b, 2, s, N_OPS * A_W), BF16),
                   jax.ShapeDtypeStruct((b, s, A_W), BF16),
                   jax.ShapeDtypeStruct((b, s, 2 * A_W), F32),
                   jax.ShapeDtypeStruct((b, 2, s // tm, tm // CHUNK, A_W), F32),
                   jax.ShapeDtypeStruct((b, s, HGRN_COLS), F32)],
        compiler_params=_cparams(("parallel", "parallel")),
        name="rec_in",
    )(ctx, x, x, x, mods, mods, *args)


def _chunk_order(d, j, n_ctx_chunks, n_chunks):
    back = jnp.where(j < n_ctx_chunks, n_ctx_chunks - 1 - j, n_chunks - 1 + n_ctx_chunks - j)
    return jnp.where(d == 0, j, back)


def _order_masks(rev):
    row = lax.broadcasted_iota(jnp.int32, (CHUNK, CHUNK), 0)
    col = lax.broadcasted_iota(jnp.int32, (CHUNK, CHUNK), 1)
    diff = jnp.where(rev, row - col, col - row)
    return diff < 0, diff <= 0, row == col


def _scan_step_coords(t, n_blocks, n_steps):
    return jnp.minimum(t, n_steps - 1), jnp.maximum(t - 1, 0)


def _rwkv_chunk_kernel(ops_ref, v_ref, gam_ref, o_ref, s_ref, tr_ref, uu_ref, ol_ref, ab_ref,
                       m2_ref, cc_ref, gm_ref, *, n_blocks, n_steps):
    step = pl.program_id(0)
    t_in, t_out = _scan_step_coords(step, n_blocks, n_steps)
    rev = (t_in // n_blocks) % 2 == 1
    rev_out = (t_out // n_blocks) % 2 == 1
    first_out = t_out % n_blocks == 0
    w_slot = step % 2
    r_slot = 1 - w_slot

    @pl.when(step == 0)
    def _():
        s_ref[...] = jnp.zeros_like(s_ref)
        for ref in (tr_ref, uu_ref, ol_ref, ab_ref, m2_ref, cc_ref, gm_ref):
            ref[1] = jnp.zeros(ref.shape[1:], ref.dtype)

    gw = RWKV_GROUP * A_HD
    groups = range(A_HEADS // RWKV_GROUP)
    gsl = [slice(g * gw, (g + 1) * gw) for g in groups]
    row = lax.broadcasted_iota(jnp.int32, (CHUNK, gw), 0)
    col = lax.broadcasted_iota(jnp.int32, (CHUNK, gw), 1) % A_HD
    diff = jnp.where(rev, row - col, col - row)
    strict_c = diff < 0
    incl_c = diff <= 0
    eye_c = jnp.where(diff == 0, 1.0, 0.0)
    bd_mask = (lax.broadcasted_iota(jnp.int32, (gw, gw), 0) // A_HD
               == lax.broadcasted_iota(jnp.int32, (gw, gw), 1) // A_HD)

    def bd(x):
        return jnp.where(bd_mask, jnp.concatenate([x] * RWKV_GROUP, axis=0), jnp.zeros((), x.dtype))

    def stack(x):
        return jnp.concatenate([x[:, h * A_HD:(h + 1) * A_HD] for h in range(RWKV_GROUP)], axis=0)

    n = SCAN_BLOCK // CHUNK
    rows = [pl.ds(pl.multiple_of(jnp.where(rev, (n - 1 - i) * CHUNK, i * CHUNK), CHUNK), CHUNK) for i in range(n)]
    rows_out = [pl.ds(pl.multiple_of(jnp.where(rev_out, (n - 1 - i) * CHUNK, i * CHUNK), CHUNK), CHUNK)
                for i in range(n)]
    units = [(i, g) for i in range(n) for g in groups]

    s = [jnp.where(first_out, 0.0, s_ref[g]) for g in groups]
    zr = {}
    pieces = []

    def advance(i):
        for g in groups:
            ui = i * len(groups) + g
            zr[i, g] = _dot_nt(tr_ref[r_slot, ui], bd(s[g]))
            s[g] = s[g] * gm_ref[r_slot, ui] - jnp.dot(_bf(s[g]), m2_ref[r_slot, ui],
                                                       preferred_element_type=F32) + cc_ref[r_slot, ui]

    def emit_out(i):
        for g in groups:
            ui = i * len(groups) + g
            z = zr[i, g][:CHUNK] + uu_ref[r_slot, ui]
            o_ref[0, 0, rows_out[i], gsl[g]] = (zr[i, g][CHUNK:] + ol_ref[r_slot, ui]
                                                - _dot(ab_ref[r_slot, ui], bd(z)))

    for i in range(n):
        pieces += [functools.partial(advance, i), functools.partial(emit_out, i)]

    def state_piece():
        if pieces:
            pieces.pop(0)()

    def operand(sec, u):
        i, g = u
        return ops_ref[0, 0, rows[i], sec * A_W + g * gw:sec * A_W + (g + 1) * gw]

    kap = {u: operand(OP_KAP, u) for u in units}
    rt = {u: operand(OP_RT, u) for u in units}
    kbar = {u: operand(OP_KBAR, u) for u in units}
    bbar = {u: operand(OP_BBAR, u) for u in units}
    kgam = {u: operand(OP_KGAM, u) for u in units}
    bgam = {u: operand(OP_BGAM, u) for u in units}
    v = {(i, g): v_ref[0, rows[i], gsl[g]] for i, g in units}
    chunk_of = [jnp.where(rev, n - 1 - i, i) for i in range(n)]
    gam = {(i, g): gam_ref[0, 0, 0, pl.ds(chunk_of[i], 1), gsl[g]] for i, g in units}

    x = {u: jnp.concatenate([kap[u], rt[u]], axis=0) for u in units}
    yb = {u: _dot_nt(x[u], bd(bbar[u])) for u in units}
    state_piece()
    a = {u: jnp.where(strict_c, -yb[u][:CHUNK], 0.0) for u in units}
    xs = {u: eye_c + a[u] for u in units}
    pw = {u: _dot(a[u], bd(a[u])) for u in units}
    state_piece()
    for _ in range(4):
        st = {u: _dot(jnp.concatenate([pw[u], xs[u]], axis=0), bd(pw[u])) for u in units}
        pw = {u: st[u][:CHUNK] for u in units}
        xs = {u: xs[u] + st[u][CHUNK:] for u in units}
        state_piece()
    t_inv = {u: xs[u] + _dot(xs[u], bd(pw[u])) for u in units}
    state_piece()
    yk = {u: _dot_nt(x[u], bd(kbar[u])) for u in units}
    ykm = {u: jnp.concatenate([jnp.where(strict_c, yk[u][:CHUNK], 0.0), jnp.where(incl_c, yk[u][CHUNK:], 0.0)],
                              axis=0) for u in units}
    wo = {u: _dot(ykm[u], bd(v[u])) for u in units}
    w1 = {u: wo[u][:CHUNK] for u in units}
    o_loc = {u: wo[u][CHUNK:] for u in units}
    state_piece()
    tk = {u: _dot(t_inv[u], bd(kap[u])) for u in units}
    uu = {u: _dot(t_inv[u], bd(w1[u])) for u in units}
    while pieces:
        state_piece()
    for g in groups:
        s_ref[g] = s[g]
    a_rb = {u: jnp.where(incl_c, yb[u][CHUNK:], 0.0) for u in units}
    tu = {u: _dot_tn(jnp.concatenate([stack(tk[u]), stack(uu[u])], axis=1), bd(bgam[u])) for u in units}
    bd_m2 = {u: _bf(bd(tu[u][:A_HD])) for u in units}
    cc = {u: _dot_tn(stack(v[u]), bd(kgam[u])) - tu[u][A_HD:] for u in units}
    for u in units:
        ui = u[0] * len(groups) + u[1]
        tr_ref[w_slot, ui] = jnp.concatenate([_bf(tk[u]), rt[u]], axis=0)
        uu_ref[w_slot, ui] = uu[u]
        ol_ref[w_slot, ui] = o_loc[u]
        ab_ref[w_slot, ui] = _bf(a_rb[u])
        m2_ref[w_slot, ui] = bd_m2[u]
        cc_ref[w_slot, ui] = cc[u]
        gm_ref[w_slot, ui] = gam[u]


def _rwkv_scan(ops, vv, gam, ctx_len):
    b, _, s, _ = ops.shape
    nc, ncc = s // SCAN_BLOCK, ctx_len // SCAN_BLOCK

    n_steps = b * 2 * nc

    def coords(flat):
        d = (flat // nc) % 2
        return flat // (2 * nc), d, _chunk_order(d, flat % nc, ncc, nc)

    def coords_in(t):
        return coords(_scan_step_coords(t, nc, n_steps)[0])

    def out_map(t):
        bi, d, blk = coords(_scan_step_coords(t, nc, n_steps)[1])
        return bi, d, blk, 0

    def ops_map(t):
        bi, d, blk = coords_in(t)
        return bi, d, blk, 0

    def v_map(t):
        bi, _, blk = coords_in(t)
        return bi, blk, 0

    def gam_map(t):
        bi, d, blk = coords_in(t)
        return bi, d, blk, 0, 0

    n_groups = A_HEADS // RWKV_GROUP
    n_chunks = SCAN_BLOCK // CHUNK
    n_units = n_chunks * n_groups
    gw = RWKV_GROUP * A_HD
    return pl.pallas_call(
        functools.partial(_rwkv_chunk_kernel, n_blocks=nc, n_steps=n_steps),
        grid=(n_steps + 1,),
        in_specs=[pl.BlockSpec((1, 1, SCAN_BLOCK, N_OPS * A_W), ops_map),
                  pl.BlockSpec((1, SCAN_BLOCK, A_W), v_map),
                  pl.BlockSpec((1, 1, 1, n_chunks, A_W), gam_map)],
        out_specs=pl.BlockSpec((1, 1, SCAN_BLOCK, A_W), out_map),
        out_shape=jax.ShapeDtypeStruct((b, 2, s, A_W), F32),
        scratch_shapes=[pltpu.VMEM((n_groups, A_HD, gw), F32),
                        pltpu.VMEM((2, n_units, 2 * CHUNK, gw), BF16),
                        pltpu.VMEM((2, n_units, CHUNK, gw), F32),
                        pltpu.VMEM((2, n_units, CHUNK, gw), F32),
                        pltpu.VMEM((2, n_units, CHUNK, gw), BF16),
                        pltpu.VMEM((2, n_units, gw, gw), BF16),
                        pltpu.VMEM((2, n_units, CHUNK, gw), F32),
                        pltpu.VMEM((2, n_units, 1, gw), F32)],
        compiler_params=_cparams(("arbitrary",)),
        name="rwkv_scan",
    )(ops, vv, gam)


def _hgrn_chunk_kernel(q_ref, i_ref, f_ref, lb_ref, o_ref, s_ref):
    rev = pl.program_id(1) == 1

    @pl.when(pl.program_id(2) == 0)
    def _():
        s_ref[...] = jnp.zeros_like(s_ref)

    _, incl, _ = _order_masks(rev)
    incl_bf = jnp.where(incl, 1.0, 0.0).astype(BF16)
    lb = lb_ref[0]
    heads = range(B_HEADS)
    sls = [slice(h * B_DK, (h + 1) * B_DK) for h in heads]

    n = SCAN_BLOCK // CHUNK
    rows = [pl.ds(pl.multiple_of(jnp.where(rev, (n - 1 - i) * CHUNK, i * CHUNK), CHUNK), CHUNK) for i in range(n)]
    units = [(i, h) for i in range(n) for h in heads]
    q_in, k_in, q_st, k_tail, gam, v = ({} for _ in range(6))
    for i in range(n):
        f = lb + (1.0 - lb) * _sigmoid(f_ref[0, rows[i], :])
        logf = jnp.log(f)
        kf = 1.0 - f
        g_incl = _dot_split(incl_bf, logf)
        g_tot = jnp.sum(logf, axis=0, keepdims=True)
        g_mid = g_incl[CHUNK // 2 - 1:CHUNK // 2, :]
        q = _silu(q_ref[0, rows[i], :])
        v_i = i_ref[0, rows[i], :]
        q_in_i = q * jnp.exp(g_incl - g_mid)
        k_in_i = kf * jnp.exp(g_mid - g_incl)
        q_st_i = q_in_i * jnp.exp(g_mid)
        k_tail_i = k_in_i * jnp.exp(g_tot - g_mid)
        for h in heads:
            q_in[i, h], k_in[i, h] = q_in_i[:, sls[h]], k_in_i[:, sls[h]]
            q_st[i, h], k_tail[i, h] = q_st_i[:, sls[h]], k_tail_i[:, sls[h]]
            gam[i, h] = jnp.exp(g_tot)[:, sls[h]]
            v[i, h] = v_i[:, sls[h]]
    att = {u: jnp.where(incl, _dot_nt(q_in[u], k_in[u]), 0.0) for u in units}
    o_loc = {u: _dot(att[u], v[u]) for u in units}
    kv = {u: _dot_tn(v[u], k_tail[u]) for u in units}

    s = [s_ref[h] for h in heads]
    for i in range(n):
        for h in heads:
            o_ref[0, 0, rows[i], sls[h]] = o_loc[i, h] + _dot_nt(q_st[i, h], s[h])
            s[h] = s[h] * gam[i, h] + kv[i, h]
    for h in heads:
        s_ref[h] = s[h]


def _hgrn_scan(p, lb, ctx_len):
    b, s, _ = p.shape
    nc, ncc = s // SCAN_BLOCK, ctx_len // SCAN_BLOCK

    def sec(idx):
        return pl.BlockSpec((1, SCAN_BLOCK, B_W), lambda i, d, j: (i, _chunk_order(d, j, ncc, nc), idx))

    return pl.pallas_call(
        _hgrn_chunk_kernel,
        grid=(b, 2, nc),
        in_specs=[sec(0), sec(1),
                  pl.BlockSpec((1, SCAN_BLOCK, B_W), lambda i, d, j: (i, _chunk_order(d, j, ncc, nc), 2 + d)),
                  pl.BlockSpec((1, 1, B_W), lambda i, d, j: (d, 0, 0))],
        out_specs=pl.BlockSpec((1, 1, SCAN_BLOCK, B_W), lambda i, d, j: (i, d, _chunk_order(d, j, ncc, nc), 0)),
        out_shape=jax.ShapeDtypeStruct((b, 2, s, B_W), F32),
        scratch_shapes=[pltpu.VMEM((B_HEADS, B_DK, B_DK), F32)],
        compiler_params=_cparams(("parallel", "parallel", "arbitrary")),
        name="hgrn_scan",
    )(p, p, p, lb)


def _rec_out_kernel(c_ref, x_ref, modl_ref, modc_ref, oa_ref, g_ref, bonus_ref, ob_ref, gate_ref, lnw_ref, lnb_ref,
                    hgn_ref, ones_a_ref, ones_b_ref, w_ref, o_ref, *, ctx_len, tm):
    pos0 = pl.program_id(1) * tm
    oa = oa_ref[0, 0] + oa_ref[0, 1]
    ones_a = ones_a_ref[...]
    mean = _segsum(oa, ones_a) * (1.0 / A_HD)
    cen = oa - mean
    var = _segsum(cen * cen, ones_a) * (1.0 / A_HD)
    ya = (cen * lax.rsqrt(var + GN_EPS) * lnw_ref[...] + lnb_ref[...] + bonus_ref[0]) * g_ref[0]
    ob = ob_ref[0, 0] + ob_ref[0, 1]
    ms = _segsum(ob * ob, ones_b_ref[...]) * (1.0 / B_DK)
    yb = ob * lax.rsqrt(ms + NORM_EPS) * hgn_ref[...] * _silu(gate_ref[0])
    y = _dot(ya, w_ref[:A_W, :]) + _dot(yb, w_ref[A_W:, :])
    gate = _gate_vec(modl_ref, modc_ref, 2, pos0, tm, ctx_len)
    o_ref[0] = _stream_tile(c_ref, x_ref) + gate * y


def _rec_out(ctx, x, mods, ctx_row, oa, g_bonus, ob, p_hgrn, ln_w, ln_b, hg_norm, w_out):
    b, n_lat, d = x.shape
    ctx_len = ctx.shape[1]
    s = ctx_len + n_lat
    tm = TOK_TILE
    row = lambda a: a.reshape(1, -1)
    full = lambda a: pl.BlockSpec(a.shape, lambda i, j: (0,) * a.ndim)
    consts = [row(ln_w), row(ln_b), row(jnp.tile(hg_norm, B_HEADS)), _block_ones(A_W, A_HD),
              _block_ones(B_W, B_DK), w_out]
    return pl.pallas_call(
        functools.partial(_rec_out_kernel, ctx_len=ctx_len, tm=tm),
        grid=(b, s // tm),
        in_specs=_stream_specs(tm, d, ctx_len) + _mod_specs(d, ctx_row) + [
            pl.BlockSpec((1, 2, tm, A_W), lambda i, j: (i, 0, j, 0)),
            pl.BlockSpec((1, tm, A_W), lambda i, j: (i, j, 0)),
            pl.BlockSpec((1, tm, A_W), lambda i, j: (i, j, 1)),
            pl.BlockSpec((1, 2, tm, B_W), lambda i, j: (i, 0, j, 0)),
            pl.BlockSpec((1, tm, B_W), lambda i, j: (i, j, 4)),
        ] + [full(a) for a in consts],
        out_specs=pl.BlockSpec((1, tm, d), lambda i, j: (i, j, 0)),
        out_shape=jax.ShapeDtypeStruct((b, s, d), F32),
        compiler_params=_cparams(("parallel", "parallel")),
        name="rec_out",
    )(ctx, x, mods, mods, oa, g_bonus, g_bonus, ob, p_hgrn, *consts)


def _res_proj_kernel(x_ref, modl_ref, modc_ref, y_ref, w_ref, o_ref, *, ctx_len, tm):
    pos0 = pl.program_id(1) * tm
    gate = _gate_vec(modl_ref, modc_ref, 2, pos0, tm, ctx_len)
    o_ref[0] = x_ref[0] + gate * _dot(y_ref[0], w_ref[...])


def _res_proj(x, mods, ctx_row, y, w, ctx_len, x_skip=0):
    b, s, k = y.shape
    d = x.shape[-1]
    tm = TOK_TILE
    skip = x_skip // tm
    return pl.pallas_call(
        functools.partial(_res_proj_kernel, ctx_len=ctx_len, tm=tm),
        grid=(b, s // tm),
        in_specs=[pl.BlockSpec((1, tm, d), lambda i, j: (i, j + skip, 0))] + _mod_specs(d, ctx_row) + [
            pl.BlockSpec((1, tm, k), lambda i, j: (i, j, 0)),
            pl.BlockSpec((k, d), lambda i, j: (0, 0))],
        out_specs=pl.BlockSpec((1, tm, d), lambda i, j: (i, j, 0)),
        out_shape=jax.ShapeDtypeStruct((b, s, d), F32),
        compiler_params=_cparams(("parallel", "parallel")),
        name="res_proj",
    )(x, mods, mods, y, w)


def _ffn_kernel(x_ref, modl_ref, modc_ref, g_ref, wg_ref, wu_ref, wd_ref, o_ref, h_ref, acc_ref, *, ctx_len, tm,
                n_f):
    f = pl.program_id(2)
    pos0 = pl.program_id(1) * tm

    @pl.when(f == 0)
    def _():
        h_ref[...] = _bf(_modulate(x_ref[0], g_ref[...], modl_ref, modc_ref, 3, pos0, ctx_len))
        acc_ref[...] = jnp.zeros_like(acc_ref)

    h = h_ref[...]
    act = _silu(jnp.dot(h, wg_ref[...], preferred_element_type=F32)) * jnp.dot(h, wu_ref[...],
                                                                               preferred_element_type=F32)
    acc_ref[...] += _dot(act, wd_ref[...])

    @pl.when(f == n_f - 1)
    def _():
        o_ref[0] = x_ref[0] + _gate_vec(modl_ref, modc_ref, 5, pos0, tm, ctx_len) * acc_ref[...]


def _ffn(x, mods, ctx_row, g, wg, wu, wd, ctx_len, tm, tf):
    b, s, d = x.shape
    ff = wg.shape[1]
    n_f = ff // tf
    return pl.pallas_call(
        functools.partial(_ffn_kernel, ctx_len=ctx_len, tm=tm, n_f=n_f),
        grid=(b, s // tm, n_f),
        in_specs=[pl.BlockSpec((1, tm, d), lambda i, j, f: (i, j, 0))] + _mod_specs(d, ctx_row) + [
            pl.BlockSpec((1, d), lambda i, j, f: (0, 0)),
            pl.BlockSpec((d, tf), lambda i, j, f: (0, f)),
            pl.BlockSpec((d, tf), lambda i, j, f: (0, f)),
            pl.BlockSpec((tf, d), lambda i, j, f: (f, 0))],
        out_specs=pl.BlockSpec((1, tm, d), lambda i, j, f: (i, j, 0)),
        out_shape=jax.ShapeDtypeStruct((b, s, d), F32),
        scratch_shapes=[pltpu.VMEM((tm, d), BF16), pltpu.VMEM((tm, d), F32)],
        compiler_params=_cparams(("parallel", "parallel", "arbitrary")),
        name="ffn",
    )(x, mods, mods, g.reshape(1, d), wg, wu, wd)


def _attn_kernel(sink_ref, q_ref, kc_ref, vc_ref, kp_ref, kq_ref, kn_ref, vp_ref, vq_ref, vn_ref, o_ref, *,
                 ctx_len, n_lat):
    i = pl.program_id(1)
    nk = ctx_len + 3 * ATT_BLOCK
    k_all = jnp.concatenate([kc_ref[0], kp_ref[0], kq_ref[0], kn_ref[0]], axis=0)
    v_all = jnp.concatenate([vc_ref[0], vp_ref[0], vq_ref[0], vn_ref[0]], axis=0)
    row = lax.broadcasted_iota(jnp.int32, (ATT_BLOCK, nk), 0)
    col = lax.broadcasted_iota(jnp.int32, (ATT_BLOCK, nk), 1)
    rel = col - (ctx_len + ATT_BLOCK)
    kabs = i * ATT_BLOCK + rel
    band = (jnp.abs(row - rel) <= WINDOW) & (kabs >= 0) & (kabs < n_lat)
    bias = jnp.where((col < ctx_len) | band, 0.0, NEG_INF)
    bias = jnp.concatenate([bias] * ATT_STACK, axis=0)
    grp = lax.broadcasted_iota(jnp.int32, (ATT_STACK * ATT_BLOCK, 1), 0) // ATT_BLOCK
    scale = HD ** -0.5
    for hq0 in range(0, HQ, ATT_STACK):
        hk = hq0 // GQ
        kh = _bf(k_all[:, hk * HD:(hk + 1) * HD])
        vh = _bf(v_all[:, hk * HD:(hk + 1) * HD])
        heads = [hq0 + g for g in range(ATT_STACK)]
        q = jnp.concatenate([q_ref[0, :, hq * HD:(hq + 1) * HD] for hq in heads], axis=0) * scale
        sk = jnp.zeros((ATT_STACK * ATT_BLOCK, 1), F32)
        for g, hq in enumerate(heads):
            sk = jnp.where(grp == g, sink_ref[hq], sk)
        s = _dot_nt(q, kh) + bias
        m = jnp.maximum(jnp.max(s, axis=-1, keepdims=True), sk)
        p = jnp.exp(s - m)
        den = jnp.sum(p, axis=-1, keepdims=True) + jnp.exp(sk - m)
        o = jnp.dot(_bf(p), vh, preferred_element_type=F32) / den
        for g, hq in enumerate(heads):
            o_ref[0, :, hq * HD:(hq + 1) * HD] = o[g * ATT_BLOCK:(g + 1) * ATT_BLOCK]


def _attention(qkv, sink, ctx_len):
    b, s, _ = qkv.shape
    n_lat = s - ctx_len
    nb = n_lat // ATT_BLOCK
    cb = ctx_len // ATT_BLOCK
    kcol = Q_COLS // KV_COLS
    vcol = kcol + 1

    def band(colblk, shift):
        return pl.BlockSpec((1, ATT_BLOCK, KV_COLS),
                            lambda bi, i: (bi, cb + jnp.clip(i + shift, 0, nb - 1), colblk))

    return pl.pallas_call(
        functools.partial(_attn_kernel, ctx_len=ctx_len, n_lat=n_lat),
        grid=(b, nb),
        in_specs=[pl.BlockSpec(memory_space=pltpu.SMEM),
                  pl.BlockSpec((1, ATT_BLOCK, Q_COLS), lambda bi, i: (bi, cb + i, 0)),
                  pl.BlockSpec((1, ctx_len, KV_COLS), lambda bi, i: (bi, 0, kcol)),
                  pl.BlockSpec((1, ctx_len, KV_COLS), lambda bi, i: (bi, 0, vcol)),
                  band(kcol, -1), band(kcol, 0), band(kcol, 1),
                  band(vcol, -1), band(vcol, 0), band(vcol, 1)],
        out_specs=pl.BlockSpec((1, ATT_BLOCK, Q_COLS), lambda bi, i: (bi, i, 0)),
        out_shape=jax.ShapeDtypeStruct((b, n_lat, Q_COLS), F32),
        compiler_params=_cparams(("parallel", "parallel")),
        name="attention",
    )(sink, qkv, qkv, qkv, qkv, qkv, qkv, qkv, qkv, qkv)


def _router_kernel(x_ref, modl_ref, modc_ref, g_ref, w_ref, b_ref, o_ref, e_ref, f_ref, *, tm):
    f = _modulate(x_ref[0], g_ref[...], modl_ref, modc_ref, 3, 0, 0)
    f_ref[0] = f
    logits = _dot_f32(f, w_ref[...]) + b_ref[...]
    lane = lax.broadcasted_iota(jnp.int32, logits.shape, 1).astype(F32)
    logits = jnp.where(lane < N_EXPERTS, logits, NEG_INF)
    ex = jnp.exp(logits - jnp.max(logits, axis=-1, keepdims=True))
    probs = ex / jnp.sum(ex, axis=-1, keepdims=True)
    p1 = jnp.max(probs, axis=-1, keepdims=True)
    i1 = jnp.min(jnp.where(probs == p1, lane, float(LANES)), axis=-1, keepdims=True)
    rest = jnp.where(lane == i1, -1.0, probs)
    p2 = jnp.max(rest, axis=-1, keepdims=True)
    i2 = jnp.min(jnp.where(rest == p2, lane, float(LANES)), axis=-1, keepdims=True)
    tot = p1 + p2
    o_ref[0] = jnp.where(lane == 0.0, p1 / tot, jnp.where(lane == 1.0, p2 / tot, 0.0))
    e_ref[0] = jnp.where(lane == 0.0, i1, jnp.where(lane == 1.0, i2, 0.0)).astype(jnp.int32)


def _router(x, mods, g, w, bias):
    b, s, d = x.shape
    tm = TOK_TILE
    wpad = jnp.zeros((d, LANES), F32).at[:, :N_EXPERTS].set(w)
    bpad = jnp.zeros((1, LANES), F32).at[0, :N_EXPERTS].set(bias)
    lane_spec = pl.BlockSpec((1, tm, LANES), lambda i, j: (i, j, 0))
    return pl.pallas_call(
        functools.partial(_router_kernel, tm=tm),
        grid=(b, s // tm),
        in_specs=[pl.BlockSpec((1, tm, d), lambda i, j: (i, j, 0))] + _mod_specs(d, 0) + [
            pl.BlockSpec((1, d), lambda i, j: (0, 0)),
            pl.BlockSpec((d, LANES), lambda i, j: (0, 0)),
            pl.BlockSpec((1, LANES), lambda i, j: (0, 0))],
        out_specs=[lane_spec, lane_spec, pl.BlockSpec((1, tm, d), lambda i, j: (i, j, 0))],
        out_shape=[jax.ShapeDtypeStruct((b, s, LANES), F32), jax.ShapeDtypeStruct((b, s, LANES), jnp.int32),
                   jax.ShapeDtypeStruct((b, s, d), F32)],
        compiler_params=_cparams(("parallel", "parallel")),
        name="router",
    )(x, mods, mods, g.reshape(1, d), wpad, bpad)


MOE_ROWS = 256


def _row_scatter_kernel(dest_ref, pad_ref, f_ref, xs_hbm, zero_ref, sem, *, tm, n_pad):
    base = pl.program_id(0) * tm

    @pl.when(pl.program_id(0) == 0)
    def _():
        zero_ref[...] = jnp.zeros_like(zero_ref)

        def zero_row(r, carry):
            pltpu.make_async_copy(zero_ref, xs_hbm.at[pl.ds(pad_ref[r], 1)], sem).start()
            return carry

        lax.fori_loop(0, n_pad, zero_row, 0, unroll=8)
        for _ in range(n_pad // tm):
            pltpu.make_async_copy(f_ref, xs_hbm.at[pl.ds(0, tm)], sem).wait()

    def issue(r, carry):
        slot = 2 * (base + r)
        pltpu.make_async_copy(f_ref.at[pl.ds(r, 1)], xs_hbm.at[pl.ds(dest_ref[slot], 1)], sem).start()
        pltpu.make_async_copy(f_ref.at[pl.ds(r, 1)], xs_hbm.at[pl.ds(dest_ref[slot + 1], 1)], sem).start()
        return carry

    lax.fori_loop(0, tm, issue, 0, unroll=8)
    for _ in range(2):
        pltpu.make_async_copy(f_ref, xs_hbm.at[pl.ds(0, tm)], sem).wait()


def _row_scatter(f, dest, pad_rows, n_rows):
    n_tok, d = f.shape
    tm = TOK_TILE
    n_pad = pad_rows.shape[0]
    assert n_pad % tm == 0 and 2 * n_tok + n_pad == n_rows
    return pl.pallas_call(
        functools.partial(_row_scatter_kernel, tm=tm, n_pad=n_pad),
        grid_spec=pltpu.PrefetchScalarGridSpec(
            num_scalar_prefetch=2,
            grid=(n_tok // tm,),
            in_specs=[pl.BlockSpec((tm, d), lambda i, dr, pr: (i, 0))],
            out_specs=pl.BlockSpec(memory_space=pl.ANY),
            scratch_shapes=[pltpu.VMEM((1, d), f.dtype), pltpu.SemaphoreType.DMA(())]),
        out_shape=jax.ShapeDtypeStruct((n_rows, d), f.dtype),
        compiler_params=_cparams(("arbitrary",)),
        name="moe_scatter",
    )(dest, pad_rows, f)


def _expert_kernel(be_ref, x_ref, wg_ref, wu_ref, wd_ref, o_ref):
    h = _bf(x_ref[...])
    act = _silu(jnp.dot(h, wg_ref[0], preferred_element_type=F32)) * jnp.dot(h, wu_ref[0],
                                                                              preferred_element_type=F32)
    o_ref[...] = _dot(act, wd_ref[0])


def _experts(xs, block_e, wg, wu, wd):
    n_rows, d = xs.shape
    ff = wg.shape[2]
    rows = MOE_ROWS
    return pl.pallas_call(
        _expert_kernel,
        grid_spec=pltpu.PrefetchScalarGridSpec(
            num_scalar_prefetch=1,
            grid=(n_rows // rows,),
            in_specs=[pl.BlockSpec((rows, d), lambda i, be: (i, 0)),
                      pl.BlockSpec((1, d, ff), lambda i, be: (be[i], 0, 0)),
                      pl.BlockSpec((1, d, ff), lambda i, be: (be[i], 0, 0)),
                      pl.BlockSpec((1, ff, d), lambda i, be: (be[i], 0, 0))],
            out_specs=pl.BlockSpec((rows, d), lambda i, be: (i, 0))),
        out_shape=jax.ShapeDtypeStruct((n_rows, d), F32),
        compiler_params=_cparams(("arbitrary",)),
        name="moe_experts",
    )(block_e, xs, wg, wu, wd)


def _combine_kernel(dest_ref, x_ref, modl_ref, modc_ref, w_ref, gfin_ref, ys_hbm, o_ref, y1_ref, y2_ref, sem, *,
                    tm, tiles_per_row):
    tile = pl.program_id(0) * tiles_per_row + pl.program_id(1)
    n_tiles = pl.num_programs(0) * tiles_per_row
    buf = tile % 2

    def fetch(t, b):
        def issue(r, carry):
            slot = 2 * (t * tm + r)
            pltpu.make_async_copy(ys_hbm.at[pl.ds(dest_ref[slot], 1)], y1_ref.at[b, pl.ds(r, 1)], sem.at[b]).start()
            pltpu.make_async_copy(ys_hbm.at[pl.ds(dest_ref[slot + 1], 1)], y2_ref.at[b, pl.ds(r, 1)],
                                  sem.at[b]).start()
            return carry

        lax.fori_loop(0, tm, issue, 0, unroll=8)

    @pl.when(tile == 0)
    def _():
        fetch(0, 0)

    @pl.when(tile + 1 < n_tiles)
    def _():
        fetch(tile + 1, 1 - buf)

    pltpu.make_async_copy(ys_hbm.at[pl.ds(0, tm)], y1_ref.at[buf], sem.at[buf]).wait()
    pltpu.make_async_copy(ys_hbm.at[pl.ds(0, tm)], y2_ref.at[buf], sem.at[buf]).wait()
    w = w_ref[0]
    moe = w[:, 0:1] * y1_ref[buf] + w[:, 1:2] * y2_ref[buf]
    y = x_ref[0] + _gate_vec(modl_ref, modc_ref, 5, 0, tm, 0) * moe
    o_ref[0] = y * lax.rsqrt(jnp.mean(y * y, axis=-1, keepdims=True) + NORM_EPS) * gfin_ref[...]


def _combine(x, mods, tokw, dest, ys, final_g):
    b, s, d = x.shape
    tm = TOK_TILE
    tiles = s // tm
    return pl.pallas_call(
        functools.partial(_combine_kernel, tm=tm, tiles_per_row=tiles),
        grid_spec=pltpu.PrefetchScalarGridSpec(
            num_scalar_prefetch=1,
            grid=(b, tiles),
            in_specs=[pl.BlockSpec((1, tm, d), lambda i, j, dr: (i, j, 0)),
                      pl.BlockSpec((1, N_MOD, d), lambda i, j, dr: (i, 0, 0)),
                      pl.BlockSpec((1, N_MOD, d), lambda i, j, dr: (0, 0, 0)),
                      pl.BlockSpec((1, tm, LANES), lambda i, j, dr: (i, j, 0)),
                      pl.BlockSpec((1, d), lambda i, j, dr: (0, 0)),
                      pl.BlockSpec(memory_space=pl.ANY)],
            out_specs=pl.BlockSpec((1, tm, d), lambda i, j, dr: (i, j, 0)),
            scratch_shapes=[pltpu.VMEM((2, tm, d), F32), pltpu.VMEM((2, tm, d), F32),
                            pltpu.SemaphoreType.DMA((2,))]),
        out_shape=jax.ShapeDtypeStruct((b, s, d), F32),
        compiler_params=_cparams(("arbitrary", "arbitrary")),
        name="moe_combine",
    )(dest, x, mods, mods, tokw, final_g.reshape(1, d), ys)


def _moe_routing(eidx):
    m = eidx.shape[0]
    oh = (eidx[:, None] == jnp.arange(N_EXPERTS, dtype=jnp.int32)[None, :]).astype(jnp.int32)
    csum = jnp.cumsum(oh, axis=0)
    rank = jnp.sum((csum - oh) * oh, axis=1)
    counts = csum[-1]
    padded = (counts + MOE_ROWS - 1) // MOE_ROWS * MOE_ROWS
    pad_end = jnp.cumsum(padded)
    dest = jnp.sum(oh * (pad_end - padded)[None, :], axis=1) + rank
    n_blocks = m // MOE_ROWS + N_EXPERTS
    starts = jnp.arange(n_blocks, dtype=jnp.int32) * MOE_ROWS
    block_e = jnp.minimum(jnp.sum((starts[:, None] >= pad_end[None, :]).astype(jnp.int32), axis=1), N_EXPERTS - 1)
    pad_cnt = padded - counts
    pad_cum = jnp.cumsum(pad_cnt)
    idx = jnp.arange(N_EXPERTS * MOE_ROWS, dtype=jnp.int32)
    owner = jnp.sum((idx[:, None] >= pad_cum[None, :]).astype(jnp.int32), axis=1)
    own = (jnp.minimum(owner, N_EXPERTS - 1)[:, None] == jnp.arange(N_EXPERTS)[None, :]).astype(jnp.int32)
    in_expert = jnp.sum(own * (pad_end - pad_cnt)[None, :], axis=1) + idx - jnp.sum(
        own * (pad_cum - pad_cnt)[None, :], axis=1)
    pad_rows = jnp.where(owner < N_EXPERTS, in_expert, pad_end[-1] + idx - pad_cum[-1])
    return dest.astype(jnp.int32), block_e, pad_rows.astype(jnp.int32), n_blocks * MOE_ROWS


def _rope_tables(n_lat, ctx_len):
    rows = n_lat // GRID_W
    row = jnp.repeat(jnp.arange(rows, dtype=F32), GRID_W)
    col = jnp.tile(jnp.arange(GRID_W, dtype=F32), rows)
    inv = ROPE_BASE ** (-jnp.arange(0, AX_DIM, 2, dtype=F32) / AX_DIM)
    ar, ac = row[:, None] * inv, col[:, None] * inv
    cos = jnp.concatenate([jnp.cos(ar), jnp.cos(ar), jnp.cos(ac), jnp.cos(ac)], axis=-1)
    sin = jnp.concatenate([-jnp.sin(ar), jnp.sin(ar), -jnp.sin(ac), jnp.sin(ac)], axis=-1)
    n_heads = ROPE_COLS // HD
    cos = jnp.concatenate([jnp.ones((ctx_len, HD), F32), cos], axis=0)
    sin = jnp.concatenate([jnp.zeros((ctx_len, HD), F32), sin], axis=0)
    half = AX_DIM // 2
    j = jnp.arange(HD)
    partner = jnp.where((j % AX_DIM) < half, j + half, j - half)
    perm = (jnp.arange(n_heads)[:, None] * HD + partner[None, :]).reshape(-1)
    return jnp.tile(cos, (1, n_heads)), jnp.tile(sin, (1, n_heads)), perm


def kernel(x, c, ctx, c_ctx, mod_w, mod_b, norm_mix, norm_ffn, norm_final, rec_w_in, rec_w_out, rwkv_mu, rwkv_w0, rwkv_w_up, rwkv_a0, rwkv_a_up, rwkv_g_up, rwkv_k_k, rwkv_k_a, rwkv_r_k, rwkv_ln_w, rwkv_ln_b, hgrn_lb, hgrn_norm, ffn_w_gate, ffn_w_up, ffn_w_down, att_w_in, att_w_out, att_sink, moe_router, moe_router_b, moe_w_gate, moe_w_up, moe_w_down):
    bsz, n_lat, d = x.shape
    ctx_len = ctx.shape[1]

    n_rows = -(-(bsz + 1) // 8) * 8
    cvec = jnp.zeros((n_rows, d), F32).at[:bsz].set(c).at[bsz].set(c_ctx)
    mods = [_adaln(cvec, mod_w[l], mod_b[l]).reshape(n_rows, N_MOD, d) for l in range(2)]

    ops, vv, g_bonus, gam, p_hgrn = _rec_in(ctx, x, mods[0], bsz, norm_mix[0], _bf(rec_w_in[0]), rwkv_mu[0],
                                            rwkv_w0[0], rwkv_w_up[0], rwkv_a0[0], rwkv_a_up[0], rwkv_g_up[0],
                                            rwkv_k_k[0], rwkv_k_a[0], rwkv_r_k[0].reshape(-1))
    oa = _rwkv_scan(ops, vv, gam, ctx_len)
    lb = jnp.cumsum(jax.nn.softmax(hgrn_lb.astype(F32), axis=1), axis=1)[:, 0].reshape(2, 1, B_W)
    ob = _hgrn_scan(p_hgrn, lb, ctx_len)
    xcat = _rec_out(ctx, x, mods[0], bsz, oa, g_bonus, ob, p_hgrn, rwkv_ln_w[0], rwkv_ln_b[0], hgrn_norm[0],
                    _bf(rec_w_out[0]))
    xcat = _ffn(xcat, mods[0], bsz, norm_ffn[0], _bf(ffn_w_gate[0]), _bf(ffn_w_up[0]), _bf(ffn_w_down[0]), ctx_len,
                tm=384, tf=2816)

    cos, sin, perm = _rope_tables(n_lat, ctx_len)
    w_att = att_w_in[0]
    qkv = _proj_rope(xcat, mods[1], bsz, norm_mix[1], _bf(w_att), _bf(w_att[:, perm]), cos, sin, ctx_len)
    att = _attention(qkv, att_sink[0], ctx_len)
    x_lat = _res_proj(xcat, mods[1], 0, att, _bf(att_w_out[0]), 0, x_skip=ctx_len)
    tokw, eidx, f_lat = _router(x_lat, mods[1], norm_ffn[1], moe_router[0], moe_router_b[0])
    dest, block_e, pad_rows, n_rows = _moe_routing(eidx[..., :2].reshape(-1))
    xs = _row_scatter(f_lat.reshape(bsz * n_lat, d), dest, pad_rows, n_rows)
    ys = _experts(xs, block_e, _bf(moe_w_gate[0]), _bf(moe_w_up[0]), _bf(moe_w_down[0]))
    return _combine(x_lat, mods[1], tokw, dest, ys, norm_final)
```

```python
import functools
import math

import jax
import jax.numpy as jnp
from jax import lax
from jax.experimental import pallas as pl
from jax.experimental.pallas import tpu as pltpu

F32 = jnp.float32
BF16 = jnp.bfloat16

N_MOD = 6
NORM_EPS = 1e-6
NEG_INF = -1e30

A_HEADS = 8
A_HD = 64
A_W = A_HEADS * A_HD
DECAY_LORA = 64
AAA_LORA = 64
GATE_LORA = 128
RWKV_COLS = 3 * A_W + DECAY_LORA + AAA_LORA + GATE_LORA
GN_EPS = 64e-5

B_HEADS = 4
B_DK = 128
B_W = B_HEADS * B_DK
HGRN_COLS = 5 * B_W

HQ = 16
HKV = 4
GQ = HQ // HKV
HD = 64
WINDOW = 128
ATT_BLOCK = 128
ATT_STACK = GQ
AX_DIM = HD // 2
ROPE_BASE = 10000.0
GRID_W = 64
Q_COLS = HQ * HD
KV_COLS = HKV * HD
ROPE_COLS = Q_COLS + KV_COLS
ATT_COLS = Q_COLS + 2 * KV_COLS

N_EXPERTS = 8
LANES = 128
CHUNK = 64
SCAN_BLOCK = 256
MXU_WIDTH = 256
RWKV_GROUP = MXU_WIDTH // A_HD
TOK_TILE = 256
VMEM_LIMIT = 56 * 1024 * 1024

OP_KAP, OP_RT, OP_KBAR, OP_BBAR, OP_KGAM, OP_BGAM = range(6)
N_OPS = 6


def _cparams(sem):
    return pltpu.CompilerParams(dimension_semantics=sem, vmem_limit_bytes=VMEM_LIMIT)


def _bf(x):
    return x.astype(BF16)


def _dot(a, b):
    return jnp.dot(_bf(a), _bf(b), preferred_element_type=F32)


def _dot_nt(a, b):
    return lax.dot_general(_bf(a), _bf(b), (((1,), (1,)), ((), ())), preferred_element_type=F32)


def _dot_tn(a, b):
    return lax.dot_general(_bf(a), _bf(b), (((0,), (0,)), ((), ())), preferred_element_type=F32)


def _dot_f32(a, b):
    a_hi, b_hi = _bf(a), _bf(b)
    a_lo, b_lo = _bf(a - a_hi.astype(F32)), _bf(b - b_hi.astype(F32))
    acc = jnp.dot(a_hi, b_hi, preferred_element_type=F32)
    acc += jnp.dot(a_hi, b_lo, preferred_element_type=F32)
    acc += jnp.dot(a_lo, b_hi, preferred_element_type=F32)
    return acc


def _sigmoid(x):
    return 0.5 * jnp.tanh(0.5 * x) + 0.5


def _silu(x):
    return x * _sigmoid(x)


def _segsum(x, ones_bd):
    hi = _bf(x)
    lo = _bf(x - hi.astype(F32))
    return jnp.dot(hi, ones_bd, preferred_element_type=F32) + jnp.dot(lo, ones_bd, preferred_element_type=F32)


def _dot_split(a, x):
    hi = _bf(x)
    lo = _bf(x - hi.astype(F32))
    return jnp.dot(a, hi, preferred_element_type=F32) + jnp.dot(a, lo, preferred_element_type=F32)


def _block_ones(width, seg):
    i = jnp.arange(width) // seg
    return (i[:, None] == i[None, :]).astype(BF16)


def _modulate(x, g, modl_ref, modc_ref, row, pos0, ctx_len, pos=None):
    y = x * lax.rsqrt(jnp.mean(x * x, axis=-1, keepdims=True) + NORM_EPS) * g
    if pos is None:
        pos = pos0 + lax.broadcasted_iota(jnp.int32, (x.shape[0], 1), 0)
    is_ctx = pos < ctx_len
    shift = jnp.where(is_ctx, modc_ref[0, row:row + 1, :], modl_ref[0, row:row + 1, :])
    scale = jnp.where(is_ctx, modc_ref[0, row + 1:row + 2, :], modl_ref[0, row + 1:row + 2, :])
    return y * (1.0 + scale) + shift


def _gate_vec(modl_ref, modc_ref, row, pos0, n, ctx_len):
    pos = pos0 + lax.broadcasted_iota(jnp.int32, (n, 1), 0)
    return jnp.where(pos < ctx_len, modc_ref[0, row:row + 1, :], modl_ref[0, row:row + 1, :])


def _mod_specs(d, ctx_row, batch_axis=0):
    def lat_map(*idx):
        return (idx[batch_axis], 0, 0)

    def ctx_map(*idx):
        return (ctx_row, 0, 0)

    return [pl.BlockSpec((1, N_MOD, d), lat_map), pl.BlockSpec((1, N_MOD, d), ctx_map)]


def _adaln_kernel(c_ref, w_ref, b_ref, o_ref):
    o_ref[...] = _dot(_silu(c_ref[...]), w_ref[...]) + b_ref[...]


def _adaln(cvec, w, b):
    r, d = cvec.shape
    n = w.shape[1]
    tn = 1024
    return pl.pallas_call(
        _adaln_kernel,
        grid=(n // tn,),
        in_specs=[pl.BlockSpec((r, d), lambda j: (0, 0)),
                  pl.BlockSpec((d, tn), lambda j: (0, j)),
                  pl.BlockSpec((1, tn), lambda j: (0, j))],
        out_specs=pl.BlockSpec((r, tn), lambda j: (0, j)),
        out_shape=jax.ShapeDtypeStruct((r, n), F32),
        compiler_params=_cparams(("parallel",)),
        name="adaln",
    )(cvec, w, b.reshape(1, n))


def _proj_rope_kernel(x_ref, modl_ref, modc_ref, g_ref, w_ref, wrot_ref, cos_ref, sin_ref, o_ref, *, ctx_len, tm,
                      rope_cols):
    pos0 = pl.program_id(1) * tm
    h = _bf(_modulate(x_ref[0], g_ref[...], modl_ref, modc_ref, 0, pos0, ctx_len))
    y = jnp.dot(h, w_ref[...], preferred_element_type=F32)
    yr = jnp.dot(h, wrot_ref[...], preferred_element_type=F32)
    o_ref[0, :, :rope_cols] = y[:, :rope_cols] * cos_ref[...] + yr * sin_ref[...]
    o_ref[0, :, rope_cols:] = y[:, rope_cols:]


def _proj_rope(xcat, mods, ctx_row, g, w, wrot, cos, sin, ctx_len):
    b, s, d = xcat.shape
    n = w.shape[1]
    tm = TOK_TILE
    rope_cols = wrot.shape[1]
    return pl.pallas_call(
        functools.partial(_proj_rope_kernel, ctx_len=ctx_len, tm=tm, rope_cols=rope_cols),
        grid=(b, s // tm),
        in_specs=[pl.BlockSpec((1, tm, d), lambda i, j: (i, j, 0))] + _mod_specs(d, ctx_row) + [
            pl.BlockSpec((1, d), lambda i, j: (0, 0)),
            pl.BlockSpec((d, n), lambda i, j: (0, 0)),
            pl.BlockSpec((d, rope_cols), lambda i, j: (0, 0)),
            pl.BlockSpec((tm, rope_cols), lambda i, j: (j, 0)),
            pl.BlockSpec((tm, rope_cols), lambda i, j: (j, 0))],
        out_specs=pl.BlockSpec((1, tm, n), lambda i, j: (i, j, 0)),
        out_shape=jax.ShapeDtypeStruct((b, s, n), F32),
        compiler_params=_cparams(("parallel", "parallel")),
        name="proj_rope",
    )(xcat, mods, mods, g.reshape(1, d), w, wrot, cos, sin)


DECAY_SCALE = math.exp(-0.5)


def _stream_specs(tm, d, ctx_len):
    assert ctx_len == tm
    return [pl.BlockSpec((1, tm, d), lambda i, j: (i, 0, 0)),
            pl.BlockSpec((1, tm, d), lambda i, j: (i, jnp.maximum(j - 1, 0), 0))]


def _stream_tile(c_ref, x_ref):
    return jnp.where(pl.program_id(1) == 0, c_ref[0], x_ref[0])


def _rec_in_kernel(c_ref, x_ref, xp_ref, xn_ref, modl_ref, modc_ref, g_ref, w_ref, mu_ref, w0_ref, wup_ref, a0_ref,
                   aup_ref, gup_ref, kk_ref, ka_ref, rk_ref, ones_ref, ops_ref, v_ref, gb_ref, gam_ref, ph_ref, *,
                   ctx_len, seq_len, tm):
    pos0 = pl.program_id(1) * tm
    gvec = g_ref[...]
    h = _bf(_modulate(_stream_tile(c_ref, x_ref), gvec, modl_ref, modc_ref, 0, pos0, ctx_len))
    p = jnp.dot(h, w_ref[:, :RWKV_COLS], preferred_element_type=F32)

    hgrn_pieces = [(RWKV_COLS + c, min(B_W, HGRN_COLS - c)) for c in range(0, HGRN_COLS, B_W)]

    def hgrn_piece():
        if hgrn_pieces:
            lo, width = hgrn_pieces.pop(0)
            ph_ref[0, :, lo - RWKV_COLS:lo - RWKV_COLS + width] = jnp.dot(h, w_ref[:, lo:lo + width],
                                                                          preferred_element_type=F32)

    halo = lax.broadcasted_iota(jnp.int32, (16, 1), 0)
    halo_pos = jnp.where(halo < 8, pos0 - 8 + halo, pos0 + tm - 8 + halo)
    xh = jnp.concatenate([xp_ref[0], xn_ref[0]], axis=0)
    p_halo = jnp.dot(_bf(_modulate(xh, gvec, modl_ref, modc_ref, 0, 0, ctx_len, pos=halo_pos)),
                     w_ref[:, :RWKV_COLS], preferred_element_type=F32)
    rows = lax.broadcasted_iota(jnp.int32, (tm, 1), 0)
    starts_seq = (pos0 == 0) | (pos0 == ctx_len)
    ends_seq = (pos0 + tm == ctx_len) | (pos0 + tm == seq_len)
    prev_halo = jnp.where(starts_seq, 0.0, p_halo[7:8, :])
    next_halo = jnp.where(ends_seq, 0.0, p_halo[8:9, :])
    prev = jnp.where(rows == 0, prev_halo, pltpu.roll(p, 1, 0))
    nxt = jnp.where(rows == tm - 1, next_halo, pltpu.roll(p, tm - 1, 0))
    p = p + mu_ref[...] * (0.5 * (prev + nxt) - p)
    hgrn_piece()

    r = p[:, 0:A_W]
    k = p[:, A_W:2 * A_W]
    v = p[:, 2 * A_W:3 * A_W]
    lo = 3 * A_W
    wd = p[:, lo:lo + DECAY_LORA]
    ad = p[:, lo + DECAY_LORA:lo + DECAY_LORA + AAA_LORA]
    gd = p[:, lo + DECAY_LORA + AAA_LORA:lo + DECAY_LORA + AAA_LORA + GATE_LORA]

    tw = jnp.tanh(wd)
    a = _sigmoid(a0_ref[...] + _dot(ad, aup_ref[...]))
    ones_bd = ones_ref[...]
    kk = k * kk_ref[...]
    kk = kk / jnp.maximum(jnp.sqrt(_segsum(kk * kk, ones_bd)), 1e-12)
    k = k * (1.0 + (a - 1.0) * ka_ref[...])
    b = kk * a
    v_ref[0] = _bf(v)
    gb_ref[0, :, :A_W] = _dot(_sigmoid(gd), gup_ref[...])
    gb_ref[0, :, A_W:] = _segsum(r * k * rk_ref[...], ones_bd) * v
    hgrn_piece()

    trow = lax.broadcasted_iota(jnp.int32, (tm, tm), 0)
    tcol = lax.broadcasted_iota(jnp.int32, (tm, tm), 1)
    same = (trow // CHUNK) == (tcol // CHUNK)
    n_chunks = tm // CHUNK
    for d in range(2):
        lw = -DECAY_SCALE * _sigmoid(w0_ref[d:d + 1, :] + _dot(tw, wup_ref[d]))
        before = (tcol <= trow) if d == 0 else (tcol >= trow)
        g_incl = _dot_split(jnp.where(same & before, 1.0, 0.0).astype(BF16), lw)
        last = [c * CHUNK + (CHUNK - 1 if d == 0 else 0) for c in range(n_chunks)]
        gam = jnp.exp(jnp.concatenate([g_incl[t:t + 1] for t in last], axis=0))
        gam_ref[0, d, 0] = gam
        hgrn_piece()
        e_ng = jnp.exp(-g_incl)
        e_tail = e_ng * jnp.concatenate([jnp.broadcast_to(gam[c:c + 1], (CHUNK, A_W)) for c in range(n_chunks)],
                                        axis=0)
        operands = {OP_KAP: kk * jnp.exp(g_incl - lw),
                    OP_RT: r * jnp.exp(g_incl),
                    OP_KBAR: k * e_ng, OP_BBAR: b * e_ng,
                    OP_KGAM: k * e_tail, OP_BGAM: b * e_tail}
        for sec, val in operands.items():
            ops_ref[0, d, :, sec * A_W:(sec + 1) * A_W] = _bf(val)
        hgrn_piece()
    while hgrn_pieces:
        hgrn_piece()


def _rec_in(ctx, x, mods, ctx_row, g, w, mu, w0, w_up, a0, a_up, g_up, k_k, k_a, r_k):
    b, n_lat, d = x.shape
    ctx_len = ctx.shape[1]
    s = ctx_len + n_lat
    tm = SCAN_BLOCK
    assert n_lat % tm == 0
    nb8 = n_lat // 8
    row = lambda a: a.reshape(1, -1)
    full = lambda a: pl.BlockSpec(a.shape, lambda i, j: (0,) * a.ndim)
    args = [row(g), w, row(mu), w0, w_up, row(a0), a_up, g_up, row(k_k), row(k_a), row(r_k),
            _block_ones(A_W, A_HD)]
    return pl.pallas_call(
        functools.partial(_rec_in_kernel, ctx_len=ctx_len, seq_len=s, tm=tm),
        grid=(b, s // tm),
        in_specs=_stream_specs(tm, d, ctx_len) + [
            pl.BlockSpec((1, 8, d), lambda i, j: (i, jnp.maximum((j - 1) * (tm // 8) - 1, 0), 0)),
            pl.BlockSpec((1, 8, d), lambda i, j: (i, jnp.minimum(j * (tm // 8), nb8 - 1), 0)),
        ] + _mod_specs(d, ctx_row) + [full(a) for a in args],
        out_specs=[pl.BlockSpec((1, 2, tm, N_OPS * A_W), lambda i, j: (i, 0, j, 0)),
                   pl.BlockSpec((1, tm, A_W), lambda i, j: (i, j, 0)),
                   pl.BlockSpec((1, tm, 2 * A_W), lambda i, j: (i, j, 0)),
                   pl.BlockSpec((1, 2, 1, tm // CHUNK, A_W), lambda i, j: (i, 0, j, 0, 0)),
                   pl.BlockSpec((1, tm, HGRN_COLS), lambda i, j: (i, j, 0))],
        out_shape=[jax.ShapeDtypeStruct((b, 2, s, N_OPS * A_W), BF16),
                   jax.ShapeDtypeStruct((b, s, A_W), BF16),
                   jax.ShapeDtypeStruct((b, s, 2 * A_W), F32),
                   jax.ShapeDtypeStruct((b, 2, s // tm, tm // CHUNK, A_W), F32),
                   jax.ShapeDtypeStruct((b, s, HGRN_COLS), F32)],
        compiler_params=_cparams(("parallel", "parallel")),
        name="rec_in",
    )(ctx, x, x, x, mods, mods, *args)


def _chunk_order(d, j, n_ctx_chunks, n_chunks):
    back = jnp.where(j < n_ctx_chunks, n_ctx_chunks - 1 - j, n_chunks - 1 + n_ctx_chunks - j)
    return jnp.where(d == 0, j, back)


def _order_masks(rev):
    row = lax.broadcasted_iota(jnp.int32, (CHUNK, CHUNK), 0)
    col = lax.broadcasted_iota(jnp.int32, (CHUNK, CHUNK), 1)
    diff = jnp.where(rev, row - col, col - row)
    return diff < 0, diff <= 0, row == col


def _scan_step_coords(t, n_blocks, n_steps):
    return jnp.minimum(t, n_steps - 1), jnp.maximum(t - 1, 0)


def _rwkv_chunk_kernel(ops_ref, v_ref, gam_ref, o_ref, s_ref, tr_ref, uu_ref, ol_ref, ab_ref,
                       m2_ref, cc_ref, gm_ref, *, n_blocks, n_steps):
    step = pl.program_id(0)
    t_in, t_out = _scan_step_coords(step, n_blocks, n_steps)
    rev = (t_in // n_blocks) % 2 == 1
    rev_out = (t_out // n_blocks) % 2 == 1
    first_out = t_out % n_blocks == 0
    w_slot = step % 2
    r_slot = 1 - w_slot

    @pl.when(step == 0)
    def _():
        s_ref[...] = jnp.zeros_like(s_ref)
        for ref in (tr_ref, uu_ref, ol_ref, ab_ref, m2_ref, cc_ref, gm_ref):
            ref[1] = jnp.zeros(ref.shape[1:], ref.dtype)

    gw = RWKV_GROUP * A_HD
    groups = range(A_HEADS // RWKV_GROUP)
    gsl = [slice(g * gw, (g + 1) * gw) for g in groups]
    row = lax.broadcasted_iota(jnp.int32, (CHUNK, gw), 0)
    col = lax.broadcasted_iota(jnp.int32, (CHUNK, gw), 1) % A_HD
    diff = jnp.where(rev, row - col, col - row)
    strict_c = diff < 0
    incl_c = diff <= 0
    eye_c = jnp.where(diff == 0, 1.0, 0.0)
    bd_mask = (lax.broadcasted_iota(jnp.int32, (gw, gw), 0) // A_HD
               == lax.broadcasted_iota(jnp.int32, (gw, gw), 1) // A_HD)

    def bd(x):
        return jnp.where(bd_mask, jnp.concatenate([x] * RWKV_GROUP, axis=0), jnp.zeros((), x.dtype))

    def stack(x):
        return jnp.concatenate([x[:, h * A_HD:(h + 1) * A_HD] for h in range(RWKV_GROUP)], axis=0)

    n = SCAN_BLOCK // CHUNK
    rows = [pl.ds(pl.multiple_of(jnp.where(rev, (n - 1 - i) * CHUNK, i * CHUNK), CHUNK), CHUNK) for i in range(n)]
    rows_out = [pl.ds(pl.multiple_of(jnp.where(rev_out, (n - 1 - i) * CHUNK, i * CHUNK), CHUNK), CHUNK)
                for i in range(n)]
    units = [(i, g) for i in range(n) for g in groups]

    s = [jnp.where(first_out, 0.0, s_ref[g]) for g in groups]
    zr = {}
    pieces = []

    def advance(i):
        for g in groups:
            ui = i * len(groups) + g
            zr[i, g] = _dot_nt(tr_ref[r_slot, ui], bd(s[g]))
            s[g] = s[g] * gm_ref[r_slot, ui] - jnp.dot(_bf(s[g]), m2_ref[r_slot, ui],
                                                       preferred_element_type=F32) + cc_ref[r_slot, ui]

    def emit_out(i):
        for g in groups:
            ui = i * len(groups) + g
            z = zr[i, g][:CHUNK] + uu_ref[r_slot, ui]
            o_ref[0, 0, rows_out[i], gsl[g]] = (zr[i, g][CHUNK:] + ol_ref[r_slot, ui]
                                                - _dot(ab_ref[r_slot, ui], bd(z)))

    for i in range(n):
        pieces += [functools.partial(advance, i), functools.partial(emit_out, i)]

    def state_piece():
        if pieces:
            pieces.pop(0)()

    def operand(sec, u):
        i, g = u
        return ops_ref[0, 0, rows[i], sec * A_W + g * gw:sec * A_W + (g + 1) * gw]

    kap = {u: operand(OP_KAP, u) for u in units}
    rt = {u: operand(OP_RT, u) for u in units}
    kbar = {u: operand(OP_KBAR, u) for u in units}
    bbar = {u: operand(OP_BBAR, u) for u in units}
    kgam = {u: operand(OP_KGAM, u) for u in units}
    bgam = {u: operand(OP_BGAM, u) for u in units}
    v = {(i, g): v_ref[0, rows[i], gsl[g]] for i, g in units}
    chunk_of = [jnp.where(rev, n - 1 - i, i) for i in range(n)]
    gam = {(i, g): gam_ref[0, 0, 0, pl.ds(chunk_of[i], 1), gsl[g]] for i, g in units}

    x = {u: jnp.concatenate([kap[u], rt[u]], axis=0) for u in units}
    yb = {u: _dot_nt(x[u], bd(bbar[u])) for u in units}
    state_piece()
    a = {u: jnp.where(strict_c, -yb[u][:CHUNK], 0.0) for u in units}
    xs = {u: eye_c + a[u] for u in units}
    pw = {u: _dot(a[u], bd(a[u])) for u in units}
    state_piece()
    for _ in range(4):
        st = {u: _dot(jnp.concatenate([pw[u], xs[u]], axis=0), bd(pw[u])) for u in units}
        pw = {u: st[u][:CHUNK] for u in units}
        xs = {u: xs[u] + st[u][CHUNK:] for u in units}
        state_piece()
    t_inv = {u: xs[u] + _dot(xs[u], bd(pw[u])) for u in units}
    state_piece()
    yk = {u: _dot_nt(x[u], bd(kbar[u])) for u in units}
    ykm = {u: jnp.concatenate([jnp.where(strict_c, yk[u][:CHUNK], 0.0), jnp.where(incl_c, yk[u][CHUNK:], 0.0)],
                              axis=0) for u in units}
    wo = {u: _dot(ykm[u], bd(v[u])) for u in units}
    w1 = {u: wo[u][:CHUNK] for u in units}
    o_loc = {u: wo[u][CHUNK:] for u in units}
    state_piece()
    tk = {u: _dot(t_inv[u], bd(kap[u])) for u in units}
    uu = {u: _dot(t_inv[u], bd(w1[u])) for u in units}
    while pieces:
        state_piece()
    for g in groups:
        s_ref[g] = s[g]
    a_rb = {u: jnp.where(incl_c, yb[u][CHUNK:], 0.0) for u in units}
    tu = {u: _dot_tn(jnp.concatenate([stack(tk[u]), stack(uu[u])], axis=1), bd(bgam[u])) for u in units}
    bd_m2 = {u: _bf(bd(tu[u][:A_HD])) for u in units}
    cc = {u: _dot_tn(stack(v[u]), bd(kgam[u])) - tu[u][A_HD:] for u in units}
    for u in units:
        ui = u[0] * len(groups) + u[1]
        tr_ref[w_slot, ui] = jnp.concatenate([_bf(tk[u]), rt[u]], axis=0)
        uu_ref[w_slot, ui] = uu[u]
        ol_ref[w_slot, ui] = o_loc[u]
        ab_ref[w_slot, ui] = _bf(a_rb[u])
        m2_ref[w_slot, ui] = bd_m2[u]
        cc_ref[w_slot, ui] = cc[u]
        gm_ref[w_slot, ui] = gam[u]


def _rwkv_scan(ops, vv, gam, ctx_len):
    b, _, s, _ = ops.shape
    nc, ncc = s // SCAN_BLOCK, ctx_len // SCAN_BLOCK

    n_steps = b * 2 * nc

    def coords(flat):
        d = (flat // nc) % 2
        return flat // (2 * nc), d, _chunk_order(d, flat % nc, ncc, nc)

    def coords_in(t):
        return coords(_scan_step_coords(t, nc, n_steps)[0])

    def out_map(t):
        bi, d, blk = coords(_scan_step_coords(t, nc, n_steps)[1])
        return bi, d, blk, 0

    def ops_map(t):
        bi, d, blk = coords_in(t)
        return bi, d, blk, 0

    def v_map(t):
        bi, _, blk = coords_in(t)
        return bi, blk, 0

    def gam_map(t):
        bi, d, blk = coords_in(t)
        return bi, d, blk, 0, 0

    n_groups = A_HEADS // RWKV_GROUP
    n_chunks = SCAN_BLOCK // CHUNK
    n_units = n_chunks * n_groups
    gw = RWKV_GROUP * A_HD
    return pl.pallas_call(
        functools.partial(_rwkv_chunk_kernel, n_blocks=nc, n_steps=n_steps),
        grid=(n_steps + 1,),
        in_specs=[pl.BlockSpec((1, 1, SCAN_BLOCK, N_OPS * A_W), ops_map),
                  pl.BlockSpec((1, SCAN_BLOCK, A_W), v_map),
                  pl.BlockSpec((1, 1, 1, n_chunks, A_W), gam_map)],
        out_specs=pl.BlockSpec((1, 1, SCAN_BLOCK, A_W), out_map),
        out_shape=jax.ShapeDtypeStruct((b, 2, s, A_W), F32),
        scratch_shapes=[pltpu.VMEM((n_groups, A_HD, gw), F32),
                        pltpu.VMEM((2, n_units, 2 * CHUNK, gw), BF16),
                        pltpu.VMEM((2, n_units, CHUNK, gw), F32),
                        pltpu.VMEM((2, n_units, CHUNK, gw), F32),
                        pltpu.VMEM((2, n_units, CHUNK, gw), BF16),
                        pltpu.VMEM((2, n_units, gw, gw), BF16),
                        pltpu.VMEM((2, n_units, CHUNK, gw), F32),
                        pltpu.VMEM((2, n_units, 1, gw), F32)],
        compiler_params=_cparams(("arbitrary",)),
        name="rwkv_scan",
    )(ops, vv, gam)


def _hgrn_chunk_kernel(q_ref, i_ref, f_ref, lb_ref, o_ref, s_ref):
    rev = pl.program_id(1) == 1

    @pl.when(pl.program_id(2) == 0)
    def _():
        s_ref[...] = jnp.zeros_like(s_ref)

    _, incl, _ = _order_masks(rev)
    incl_bf = jnp.where(incl, 1.0, 0.0).astype(BF16)
    lb = lb_ref[0]
    heads = range(B_HEADS)
    sls = [slice(h * B_DK, (h + 1) * B_DK) for h in heads]

    n = SCAN_BLOCK // CHUNK
    rows = [pl.ds(pl.multiple_of(jnp.where(rev, (n - 1 - i) * CHUNK, i * CHUNK), CHUNK), CHUNK) for i in range(n)]
    units = [(i, h) for i in range(n) for h in heads]
    q_in, k_in, q_st, k_tail, gam, v = ({} for _ in range(6))
    for i in range(n):
        f = lb + (1.0 - lb) * _sigmoid(f_ref[0, rows[i], :])
        logf = jnp.log(f)
        kf = 1.0 - f
        g_incl = _dot_split(incl_bf, logf)
        g_tot = jnp.sum(logf, axis=0, keepdims=True)
        g_mid = g_incl[CHUNK // 2 - 1:CHUNK // 2, :]
        q = _silu(q_ref[0, rows[i], :])
        v_i = i_ref[0, rows[i], :]
        q_in_i = q * jnp.exp(g_incl - g_mid)
        k_in_i = kf * jnp.exp(g_mid - g_incl)
        q_st_i = q_in_i * jnp.exp(g_mid)
        k_tail_i = k_in_i * jnp.exp(g_tot - g_mid)
        for h in heads:
            q_in[i, h], k_in[i, h] = q_in_i[:, sls[h]], k_in_i[:, sls[h]]
            q_st[i, h], k_tail[i, h] = q_st_i[:, sls[h]], k_tail_i[:, sls[h]]
            gam[i, h] = jnp.exp(g_tot)[:, sls[h]]
            v[i, h] = v_i[:, sls[h]]
    att = {u: jnp.where(incl, _dot_nt(q_in[u], k_in[u]), 0.0) for u in units}
    o_loc = {u: _dot(att[u], v[u]) for u in units}
    kv = {u: _dot_tn(v[u], k_tail[u]) for u in units}

    s = [s_ref[h] for h in heads]
    for i in range(n):
        for h in heads:
            o_ref[0, 0, rows[i], sls[h]] = o_loc[i, h] + _dot_nt(q_st[i, h], s[h])
            s[h] = s[h] * gam[i, h] + kv[i, h]
    for h in heads:
        s_ref[h] = s[h]


def _hgrn_scan(p, lb, ctx_len):
    b, s, _ = p.shape
    nc, ncc = s // SCAN_BLOCK, ctx_len // SCAN_BLOCK

    def sec(idx):
        return pl.BlockSpec((1, SCAN_BLOCK, B_W), lambda i, d, j: (i, _chunk_order(d, j, ncc, nc), idx))

    return pl.pallas_call(
        _hgrn_chunk_kernel,
        grid=(b, 2, nc),
        in_specs=[sec(0), sec(1),
                  pl.BlockSpec((1, SCAN_BLOCK, B_W), lambda i, d, j: (i, _chunk_order(d, j, ncc, nc), 2 + d)),
                  pl.BlockSpec((1, 1, B_W), lambda i, d, j: (d, 0, 0))],
        out_specs=pl.BlockSpec((1, 1, SCAN_BLOCK, B_W), lambda i, d, j: (i, d, _chunk_order(d, j, ncc, nc), 0)),
        out_shape=jax.ShapeDtypeStruct((b, 2, s, B_W), F32),
        scratch_shapes=[pltpu.VMEM((B_HEADS, B_DK, B_DK), F32)],
        compiler_params=_cparams(("parallel", "parallel", "arbitrary")),
        name="hgrn_scan",
    )(p, p, p, lb)


def _rec_out_kernel(c_ref, x_ref, modl_ref, modc_ref, oa_ref, g_ref, bonus_ref, ob_ref, gate_ref, lnw_ref, lnb_ref,
                    hgn_ref, ones_a_ref, ones_b_ref, w_ref, o_ref, *, ctx_len, tm):
    pos0 = pl.program_id(1) * tm
    oa = oa_ref[0, 0] + oa_ref[0, 1]
    ones_a = ones_a_ref[...]
    mean = _segsum(oa, ones_a) * (1.0 / A_HD)
    cen = oa - mean
    var = _segsum(cen * cen, ones_a) * (1.0 / A_HD)
    ya = (cen * lax.rsqrt(var + GN_EPS) * lnw_ref[...] + lnb_ref[...] + bonus_ref[0]) * g_ref[0]
    ob = ob_ref[0, 0] + ob_ref[0, 1]
    ms = _segsum(ob * ob, ones_b_ref[...]) * (1.0 / B_DK)
    yb = ob * lax.rsqrt(ms + NORM_EPS) * hgn_ref[...] * _silu(gate_ref[0])
    y = _dot(ya, w_ref[:A_W, :]) + _dot(yb, w_ref[A_W:, :])
    gate = _gate_vec(modl_ref, modc_ref, 2, pos0, tm, ctx_len)
    o_ref[0] = _stream_tile(c_ref, x_ref) + gate * y


def _rec_out(ctx, x, mods, ctx_row, oa, g_bonus, ob, p_hgrn, ln_w, ln_b, hg_norm, w_out):
    b, n_lat, d = x.shape
    ctx_len = ctx.shape[1]
    s = ctx_len + n_lat
    tm = TOK_TILE
    row = lambda a: a.reshape(1, -1)
    full = lambda a: pl.BlockSpec(a.shape, lambda i, j: (0,) * a.ndim)
    consts = [row(ln_w), row(ln_b), row(jnp.tile(hg_norm, B_HEADS)), _block_ones(A_W, A_HD),
              _block_ones(B_W, B_DK), w_out]
    return pl.pallas_call(
        functools.partial(_rec_out_kernel, ctx_len=ctx_len, tm=tm),
        grid=(b, s // tm),
        in_specs=_stream_specs(tm, d, ctx_len) + _mod_specs(d, ctx_row) + [
            pl.BlockSpec((1, 2, tm, A_W), lambda i, j: (i, 0, j, 0)),
            pl.BlockSpec((1, tm, A_W), lambda i, j: (i, j, 0)),
            pl.BlockSpec((1, tm, A_W), lambda i, j: (i, j, 1)),
            pl.BlockSpec((1, 2, tm, B_W), lambda i, j: (i, 0, j, 0)),
            pl.BlockSpec((1, tm, B_W), lambda i, j: (i, j, 4)),
        ] + [full(a) for a in consts],
        out_specs=pl.BlockSpec((1, tm, d), lambda i, j: (i, j, 0)),
        out_shape=jax.ShapeDtypeStruct((b, s, d), F32),
        compiler_params=_cparams(("parallel", "parallel")),
        name="rec_out",
    )(ctx, x, mods, mods, oa, g_bonus, g_bonus, ob, p_hgrn, *consts)


def _res_proj_kernel(x_ref, modl_ref, modc_ref, y_ref, w_ref, o_ref, *, ctx_len, tm):
    pos0 = pl.program_id(1) * tm
    gate = _gate_vec(modl_ref, modc_ref, 2, pos0, tm, ctx_len)
    o_ref[0] = x_ref[0] + gate * _dot(y_ref[0], w_ref[...])


def _res_proj(x, mods, ctx_row, y, w, ctx_len, x_skip=0):
    b, s, k = y.shape
    d = x.shape[-1]
    tm = TOK_TILE
    skip = x_skip // tm
    return pl.pallas_call(
        functools.partial(_res_proj_kernel, ctx_len=ctx_len, tm=tm),
        grid=(b, s // tm),
        in_specs=[pl.BlockSpec((1, tm, d), lambda i, j: (i, j + skip, 0))] + _mod_specs(d, ctx_row) + [
            pl.BlockSpec((1, tm, k), lambda i, j: (i, j, 0)),
            pl.BlockSpec((k, d), lambda i, j: (0, 0))],
        out_specs=pl.BlockSpec((1, tm, d), lambda i, j: (i, j, 0)),
        out_shape=jax.ShapeDtypeStruct((b, s, d), F32),
        compiler_params=_cparams(("parallel", "parallel")),
        name="res_proj",
    )(x, mods, mods, y, w)


def _ffn_kernel(x_ref, modl_ref, modc_ref, g_ref, wg_ref, wu_ref, wd_ref, o_ref, h_ref, acc_ref, *, ctx_len, tm,
                n_f):
    f = pl.program_id(2)
    pos0 = pl.program_id(1) * tm

    @pl.when(f == 0)
    def _():
        h_ref[...] = _bf(_modulate(x_ref[0], g_ref[...], modl_ref, modc_ref, 3, pos0, ctx_len))
        acc_ref[...] = jnp.zeros_like(acc_ref)

    h = h_ref[...]
    act = _silu(jnp.dot(h, wg_ref[...], preferred_element_type=F32)) * jnp.dot(h, wu_ref[...],
                                                                               preferred_element_type=F32)
    acc_ref[...] += _dot(act, wd_ref[...])

    @pl.when(f == n_f - 1)
    def _():
        o_ref[0] = x_ref[0] + _gate_vec(modl_ref, modc_ref, 5, pos0, tm, ctx_len) * acc_ref[...]


def _ffn(x, mods, ctx_row, g, wg, wu, wd, ctx_len, tm, tf):
    b, s, d = x.shape
    ff = wg.shape[1]
    n_f = ff // tf
    return pl.pallas_call(
        functools.partial(_ffn_kernel, ctx_len=ctx_len, tm=tm, n_f=n_f),
        grid=(b, s // tm, n_f),
        in_specs=[pl.BlockSpec((1, tm, d), lambda i, j, f: (i, j, 0))] + _mod_specs(d, ctx_row) + [
            pl.BlockSpec((1, d), lambda i, j, f: (0, 0)),
            pl.BlockSpec((d, tf), lambda i, j, f: (0, f)),
            pl.BlockSpec((d, tf), lambda i, j, f: (0, f)),
            pl.BlockSpec((tf, d), lambda i, j, f: (f, 0))],
        out_specs=pl.BlockSpec((1, tm, d), lambda i, j, f: (i, j, 0)),
        out_shape=jax.ShapeDtypeStruct((b, s, d), F32),
        scratch_shapes=[pltpu.VMEM((tm, d), BF16), pltpu.VMEM((tm, d), F32)],
        compiler_params=_cparams(("parallel", "parallel", "arbitrary")),
        name="ffn",
    )(x, mods, mods, g.reshape(1, d), wg, wu, wd)


def _attn_kernel(sink_ref, q_ref, kc_ref, vc_ref, kp_ref, kq_ref, kn_ref, vp_ref, vq_ref, vn_ref, o_ref, *,
                 ctx_len, n_lat):
    i = pl.program_id(1)
    nk = ctx_len + 3 * ATT_BLOCK
    k_all = jnp.concatenate([kc_ref[0], kp_ref[0], kq_ref[0], kn_ref[0]], axis=0)
    v_all = jnp.concatenate([vc_ref[0], vp_ref[0], vq_ref[0], vn_ref[0]], axis=0)
    row = lax.broadcasted_iota(jnp.int32, (ATT_BLOCK, nk), 0)
    col = lax.broadcasted_iota(jnp.int32, (ATT_BLOCK, nk), 1)
    rel = col - (ctx_len + ATT_BLOCK)
    kabs = i * ATT_BLOCK + rel
    band = (jnp.abs(row - rel) <= WINDOW) & (kabs >= 0) & (kabs < n_lat)
    bias = jnp.where((col < ctx_len) | band, 0.0, NEG_INF)
    bias = jnp.concatenate([bias] * ATT_STACK, axis=0)
    grp = lax.broadcasted_iota(jnp.int32, (ATT_STACK * ATT_BLOCK, 1), 0) // ATT_BLOCK
    scale = HD ** -0.5
    for hq0 in range(0, HQ, ATT_STACK):
        hk = hq0 // GQ
        kh = _bf(k_all[:, hk * HD:(hk + 1) * HD])
        vh = _bf(v_all[:, hk * HD:(hk + 1) * HD])
        heads = [hq0 + g for g in range(ATT_STACK)]
        q = jnp.concatenate([q_ref[0, :, hq * HD:(hq + 1) * HD] for hq in heads], axis=0) * scale
        sk = jnp.zeros((ATT_STACK * ATT_BLOCK, 1), F32)
        for g, hq in enumerate(heads):
            sk = jnp.where(grp == g, sink_ref[hq], sk)
        s = _dot_nt(q, kh) + bias
        m = jnp.maximum(jnp.max(s, axis=-1, keepdims=True), sk)
        p = jnp.exp(s - m)
        den = jnp.sum(p, axis=-1, keepdims=True) + jnp.exp(sk - m)
        o = jnp.dot(_bf(p), vh, preferred_element_type=F32) / den
        for g, hq in enumerate(heads):
            o_ref[0, :, hq * HD:(hq + 1) * HD] = o[g * ATT_BLOCK:(g + 1) * ATT_BLOCK]


def _attention(qkv, sink, ctx_len):
    b, s, _ = qkv.shape
    n_lat = s - ctx_len
    nb = n_lat // ATT_BLOCK
    cb = ctx_len // ATT_BLOCK
    kcol = Q_COLS // KV_COLS
    vcol = kcol + 1

    def band(colblk, shift):
        return pl.BlockSpec((1, ATT_BLOCK, KV_COLS),
                            lambda bi, i: (bi, cb + jnp.clip(i + shift, 0, nb - 1), colblk))

    return pl.pallas_call(
        functools.partial(_attn_kernel, ctx_len=ctx_len, n_lat=n_lat),
        grid=(b, nb),
        in_specs=[pl.BlockSpec(memory_space=pltpu.SMEM),
                  pl.BlockSpec((1, ATT_BLOCK, Q_COLS), lambda bi, i: (bi, cb + i, 0)),
                  pl.BlockSpec((1, ctx_len, KV_COLS), lambda bi, i: (bi, 0, kcol)),
                  pl.BlockSpec((1, ctx_len, KV_COLS), lambda bi, i: (bi, 0, vcol)),
                  band(kcol, -1), band(kcol, 0), band(kcol, 1),
                  band(vcol, -1), band(vcol, 0), band(vcol, 1)],
        out_specs=pl.BlockSpec((1, ATT_BLOCK, Q_COLS), lambda bi, i: (bi, i, 0)),
        out_shape=jax.ShapeDtypeStruct((b, n_lat, Q_COLS), F32),
        compiler_params=_cparams(("parallel", "parallel")),
        name="attention",
    )(sink, qkv, qkv, qkv, qkv, qkv, qkv, qkv, qkv, qkv)


def _router_kernel(x_ref, modl_ref, modc_ref, g_ref, w_ref, b_ref, o_ref, e_ref, f_ref, *, tm):
    f = _modulate(x_ref[0], g_ref[...], modl_ref, modc_ref, 3, 0, 0)
    f_ref[0] = f
    logits = _dot_f32(f, w_ref[...]) + b_ref[...]
    lane = lax.broadcasted_iota(jnp.int32, logits.shape, 1).astype(F32)
    logits = jnp.where(lane < N_EXPERTS, logits, NEG_INF)
    ex = jnp.exp(logits - jnp.max(logits, axis=-1, keepdims=True))
    probs = ex / jnp.sum(ex, axis=-1, keepdims=True)
    p1 = jnp.max(probs, axis=-1, keepdims=True)
    i1 = jnp.min(jnp.where(probs == p1, lane, float(LANES)), axis=-1, keepdims=True)
    rest = jnp.where(lane == i1, -1.0, probs)
    p2 = jnp.max(rest, axis=-1, keepdims=True)
    i2 = jnp.min(jnp.where(rest == p2, lane, float(LANES)), axis=-1, keepdims=True)
    tot = p1 + p2
    o_ref[0] = jnp.where(lane == 0.0, p1 / tot, jnp.where(lane == 1.0, p2 / tot, 0.0))
    e_ref[0] = jnp.where(lane == 0.0, i1, jnp.where(lane == 1.0, i2, 0.0)).astype(jnp.int32)


def _router(x, mods, g, w, bias):
    b, s, d = x.shape
    tm = TOK_TILE
    wpad = jnp.zeros((d, LANES), F32).at[:, :N_EXPERTS].set(w)
    bpad = jnp.zeros((1, LANES), F32).at[0, :N_EXPERTS].set(bias)
    lane_spec = pl.BlockSpec((1, tm, LANES), lambda i, j: (i, j, 0))
    return pl.pallas_call(
        functools.partial(_router_kernel, tm=tm),
        grid=(b, s // tm),
        in_specs=[pl.BlockSpec((1, tm, d), lambda i, j: (i, j, 0))] + _mod_specs(d, 0) + [
            pl.BlockSpec((1, d), lambda i, j: (0, 0)),
            pl.BlockSpec((d, LANES), lambda i, j: (0, 0)),
            pl.BlockSpec((1, LANES), lambda i, j: (0, 0))],
        out_specs=[lane_spec, lane_spec, pl.BlockSpec((1, tm, d), lambda i, j: (i, j, 0))],
        out_shape=[jax.ShapeDtypeStruct((b, s, LANES), F32), jax.ShapeDtypeStruct((b, s, LANES), jnp.int32),
                   jax.ShapeDtypeStruct((b, s, d), F32)],
        compiler_params=_cparams(("parallel", "parallel")),
        name="router",
    )(x, mods, mods, g.reshape(1, d), wpad, bpad)


MOE_ROWS = 256


def _row_scatter_kernel(dest_ref, pad_ref, f_ref, xs_hbm, zero_ref, sem, *, tm, n_pad):
    base = pl.program_id(0) * tm

    @pl.when(pl.program_id(0) == 0)
    def _():
        zero_ref[...] = jnp.zeros_like(zero_ref)

        def zero_row(r, carry):
            pltpu.make_async_copy(zero_ref, xs_hbm.at[pl.ds(pad_ref[r], 1)], sem).start()
            return carry

        lax.fori_loop(0, n_pad, zero_row, 0, unroll=8)
        for _ in range(n_pad // tm):
            pltpu.make_async_copy(f_ref, xs_hbm.at[pl.ds(0, tm)], sem).wait()

    def issue(r, carry):
        slot = 2 * (base + r)
        pltpu.make_async_copy(f_ref.at[pl.ds(r, 1)], xs_hbm.at[pl.ds(dest_ref[slot], 1)], sem).start()
        pltpu.make_async_copy(f_ref.at[pl.ds(r, 1)], xs_hbm.at[pl.ds(dest_ref[slot + 1], 1)], sem).start()
        return carry

    lax.fori_loop(0, tm, issue, 0, unroll=8)
    for _ in range(2):
        pltpu.make_async_copy(f_ref, xs_hbm.at[pl.ds(0, tm)], sem).wait()


def _row_scatter(f, dest, pad_rows, n_rows):
    n_tok, d = f.shape
    tm = TOK_TILE
    n_pad = pad_rows.shape[0]
    assert n_pad % tm == 0 and 2 * n_tok + n_pad == n_rows
    return pl.pallas_call(
        functools.partial(_row_scatter_kernel, tm=tm, n_pad=n_pad),
        grid_spec=pltpu.PrefetchScalarGridSpec(
            num_scalar_prefetch=2,
            grid=(n_tok // tm,),
            in_specs=[pl.BlockSpec((tm, d), lambda i, dr, pr: (i, 0))],
            out_specs=pl.BlockSpec(memory_space=pl.ANY),
            scratch_shapes=[pltpu.VMEM((1, d), f.dtype), pltpu.SemaphoreType.DMA(())]),
        out_shape=jax.ShapeDtypeStruct((n_rows, d), f.dtype),
        compiler_params=_cparams(("arbitrary",)),
        name="moe_scatter",
    )(dest, pad_rows, f)


def _expert_kernel(be_ref, x_ref, wg_ref, wu_ref, wd_ref, o_ref):
    h = _bf(x_ref[...])
    act = _silu(jnp.dot(h, wg_ref[0], preferred_element_type=F32)) * jnp.dot(h, wu_ref[0],
                                                                              preferred_element_type=F32)
    o_ref[...] = _dot(act, wd_ref[0])


def _experts(xs, block_e, wg, wu, wd):
    n_rows, d = xs.shape
    ff = wg.shape[2]
    rows = MOE_ROWS
    return pl.pallas_call(
        _expert_kernel,
        grid_spec=pltpu.PrefetchScalarGridSpec(
            num_scalar_prefetch=1,
            grid=(n_rows // rows,),
            in_specs=[pl.BlockSpec((rows, d), lambda i, be: (i, 0)),
                      pl.BlockSpec((1, d, ff), lambda i, be: (be[i], 0, 0)),
                      pl.BlockSpec((1, d, ff), lambda i, be: (be[i], 0, 0)),
                      pl.BlockSpec((1, ff, d), lambda i, be: (be[i], 0, 0))],
            out_specs=pl.BlockSpec((rows, d), lambda i, be: (i, 0))),
        out_shape=jax.ShapeDtypeStruct((n_rows, d), F32),
        compiler_params=_cparams(("arbitrary",)),
        name="moe_experts",
    )(block_e, xs, wg, wu, wd)


def _combine_kernel(dest_ref, x_ref, modl_ref, modc_ref, w_ref, gfin_ref, ys_hbm, o_ref, y1_ref, y2_ref, sem, *,
                    tm, tiles_per_row):
    tile = pl.program_id(0) * tiles_per_row + pl.program_id(1)
    n_tiles = pl.num_programs(0) * tiles_per_row
    buf = tile % 2

    def fetch(t, b):
        def issue(r, carry):
            slot = 2 * (t * tm + r)
            pltpu.make_async_copy(ys_hbm.at[pl.ds(dest_ref[slot], 1)], y1_ref.at[b, pl.ds(r, 1)], sem.at[b]).start()
            pltpu.make_async_copy(ys_hbm.at[pl.ds(dest_ref[slot + 1], 1)], y2_ref.at[b, pl.ds(r, 1)],
                                  sem.at[b]).start()
            return carry

        lax.fori_loop(0, tm, issue, 0, unroll=8)

    @pl.when(tile == 0)
    def _():
        fetch(0, 0)

    @pl.when(tile + 1 < n_tiles)
    def _():
        fetch(tile + 1, 1 - buf)

    pltpu.make_async_copy(ys_hbm.at[pl.ds(0, tm)], y1_ref.at[buf], sem.at[buf]).wait()
    pltpu.make_async_copy(ys_hbm.at[pl.ds(0, tm)], y2_ref.at[buf], sem.at[buf]).wait()
    w = w_ref[0]
    moe = w[:, 0:1] * y1_ref[buf] + w[:, 1:2] * y2_ref[buf]
    y = x_ref[0] + _gate_vec(modl_ref, modc_ref, 5, 0, tm, 0) * moe
    o_ref[0] = y * lax.rsqrt(jnp.mean(y * y, axis=-1, keepdims=True) + NORM_EPS) * gfin_ref[...]


def _combine(x, mods, tokw, dest, ys, final_g):
    b, s, d = x.shape
    tm = TOK_TILE
    tiles = s // tm
    return pl.pallas_call(
        functools.partial(_combine_kernel, tm=tm, tiles_per_row=tiles),
        grid_spec=pltpu.PrefetchScalarGridSpec(
            num_scalar_prefetch=1,
            grid=(b, tiles),
            in_specs=[pl.BlockSpec((1, tm, d), lambda i, j, dr: (i, j, 0)),
                      pl.BlockSpec((1, N_MOD, d), lambda i, j, dr: (i, 0, 0)),
                      pl.BlockSpec((1, N_MOD, d), lambda i, j, dr: (0, 0, 0)),
                      pl.BlockSpec((1, tm, LANES), lambda i, j, dr: (i, j, 0)),
                      pl.BlockSpec((1, d), lambda i, j, dr: (0, 0)),
                      pl.BlockSpec(memory_space=pl.ANY)],
            out_specs=pl.BlockSpec((1, tm, d), lambda i, j, dr: (i, j, 0)),
            scratch_shapes=[pltpu.VMEM((2, tm, d), F32), pltpu.VMEM((2, tm, d), F32),
                            pltpu.SemaphoreType.DMA((2,))]),
        out_shape=jax.ShapeDtypeStruct((b, s, d), F32),
        compiler_params=_cparams(("arbitrary", "arbitrary")),
        name="moe_combine",
    )(dest, x, mods, mods, tokw, final_g.reshape(1, d), ys)


def _moe_routing(eidx):
    m = eidx.shape[0]
    oh = (eidx[:, None] == jnp.arange(N_EXPERTS, dtype=jnp.int32)[None, :]).astype(jnp.int32)
    csum = jnp.cumsum(oh, axis=0)
    rank = jnp.sum((csum - oh) * oh, axis=1)
    counts = csum[-1]
    padded = (counts + MOE_ROWS - 1) // MOE_ROWS * MOE_ROWS
    pad_end = jnp.cumsum(padded)
    dest = jnp.sum(oh * (pad_end - padded)[None, :], axis=1) + rank
    n_blocks = m // MOE_ROWS + N_EXPERTS
    starts = jnp.arange(n_blocks, dtype=jnp.int32) * MOE_ROWS
    block_e = jnp.minimum(jnp.sum((starts[:, None] >= pad_end[None, :]).astype(jnp.int32), axis=1), N_EXPERTS - 1)
    pad_cnt = padded - counts
    pad_cum = jnp.cumsum(pad_cnt)
    idx = jnp.arange(N_EXPERTS * MOE_ROWS, dtype=jnp.int32)
    owner = jnp.sum((idx[:, None] >= pad_cum[None, :]).astype(jnp.int32), axis=1)
    own = (jnp.minimum(owner, N_EXPERTS - 1)[:, None] == jnp.arange(N_EXPERTS)[None, :]).astype(jnp.int32)
    in_expert = jnp.sum(own * (pad_end - pad_cnt)[None, :], axis=1) + idx - jnp.sum(
        own * (pad_cum - pad_cnt)[None, :], axis=1)
    pad_rows = jnp.where(owner < N_EXPERTS, in_expert, pad_end[-1] + idx - pad_cum[-1])
    return dest.astype(jnp.int32), block_e, pad_rows.astype(jnp.int32), n_blocks * MOE_ROWS


def _rope_tables(n_lat, ctx_len):
    rows = n_lat // GRID_W
    row = jnp.repeat(jnp.arange(rows, dtype=F32), GRID_W)
    col = jnp.tile(jnp.arange(GRID_W, dtype=F32), rows)
    inv = ROPE_BASE ** (-jnp.arange(0, AX_DIM, 2, dtype=F32) / AX_DIM)
    ar, ac = row[:, None] * inv, col[:, None] * inv
    cos = jnp.concatenate([jnp.cos(ar), jnp.cos(ar), jnp.cos(ac), jnp.cos(ac)], axis=-1)
    sin = jnp.concatenate([-jnp.sin(ar), jnp.sin(ar), -jnp.sin(ac), jnp.sin(ac)], axis=-1)
    n_heads = ROPE_COLS // HD
    cos = jnp.concatenate([jnp.ones((ctx_len, HD), F32), cos], axis=0)
    sin = jnp.concatenate([jnp.zeros((ctx_len, HD), F32), sin], axis=0)
    half = AX_DIM // 2
    j = jnp.arange(HD)
    partner = jnp.where((j % AX_DIM) < half, j + half, j - half)
    perm = (jnp.arange(n_heads)[:, None] * HD + partner[None, :]).reshape(-1)
    return jnp.tile(cos, (1, n_heads)), jnp.tile(sin, (1, n_heads)), perm


def kernel(x, c, ctx, c_ctx, mod_w, mod_b, norm_mix, norm_ffn, norm_final, rec_w_in, rec_w_out, rwkv_mu, rwkv_w0, rwkv_w_up, rwkv_a0, rwkv_a_up, rwkv_g_up, rwkv_k_k, rwkv_k_a, rwkv_r_k, rwkv_ln_w, rwkv_ln_b, hgrn_lb, hgrn_norm, ffn_w_gate, ffn_w_up, ffn_w_down, att_w_in, att_w_out, att_sink, moe_router, moe_router_b, moe_w_gate, moe_w_up, moe_w_down):
    bsz, n_lat, d = x.shape
    ctx_len = ctx.shape[1]

    n_rows = -(-(bsz + 1) // 8) * 8
    cvec = jnp.zeros((n_rows, d), F32).at[:bsz].set(c).at[bsz].set(c_ctx)
    mods = [_adaln(cvec, mod_w[l], mod_b[l]).reshape(n_rows, N_MOD, d) for l in range(2)]

    ops, vv, g_bonus, gam, p_hgrn = _rec_in(ctx, x, mods[0], bsz, norm_mix[0], _bf(rec_w_in[0]), rwkv_mu[0],
                                            rwkv_w0[0], rwkv_w_up[0], rwkv_a0[0], rwkv_a_up[0], rwkv_g_up[0],
                                            rwkv_k_k[0], rwkv_k_a[0], rwkv_r_k[0].reshape(-1))
    oa = _rwkv_scan(ops, vv, gam, ctx_len)
    lb = jnp.cumsum(jax.nn.softmax(hgrn_lb.astype(F32), axis=1), axis=1)[:, 0].reshape(2, 1, B_W)
    ob = _hgrn_scan(p_hgrn, lb, ctx_len)
    xcat = _rec_out(ctx, x, mods[0], bsz, oa, g_bonus, ob, p_hgrn, rwkv_ln_w[0], rwkv_ln_b[0], hgrn_norm[0],
                    _bf(rec_w_out[0]))
    xcat = _ffn(xcat, mods[0], bsz, norm_ffn[0], _bf(ffn_w_gate[0]), _bf(ffn_w_up[0]), _bf(ffn_w_down[0]), ctx_len,
                tm=384, tf=2816)

    cos, sin, perm = _rope_tables(n_lat, ctx_len)
    w_att = att_w_in[0]
    qkv = _proj_rope(xcat, mods[1], bsz, norm_mix[1], _bf(w_att), _bf(w_att[:, perm]), cos, sin, ctx_len)
    att = _attention(qkv, att_sink[0], ctx_len)
    x_lat = _res_proj(xcat, mods[1], 0, att, _bf(att_w_out[0]), 0, x_skip=ctx_len)
    tokw, eidx, f_lat = _router(x_lat, mods[1], norm_ffn[1], moe_router[0], moe_router_b[0])
    dest, block_e, pad_rows, n_rows = _moe_routing(eidx[..., :2].reshape(-1))
    xs = _row_scatter(f_lat.reshape(bsz * n_lat, d), dest, pad_rows, n_rows)
    ys = _experts(xs, block_e, _bf(moe_w_gate[0]), _bf(moe_w_up[0]), _bf(moe_w_down[0]))
    return _combine(x_lat, mods[1], tokw, dest, ys, norm_final)
```

```python
import functools
import math

import jax
import jax.numpy as jnp
from jax import lax
from jax.experimental import pallas as pl
from jax.experimental.pallas import tpu as pltpu

F32 = jnp.float32
BF16 = jnp.bfloat16

N_MOD = 6
NORM_EPS = 1e-6
NEG_INF = -1e30

A_HEADS = 8
A_HD = 64
A_W = A_HEADS * A_HD
DECAY_LORA = 64
AAA_LORA = 64
GATE_LORA = 128
RWKV_COLS = 3 * A_W + DECAY_LORA + AAA_LORA + GATE_LORA
GN_EPS = 64e-5

B_HEADS = 4
B_DK = 128
B_W = B_HEADS * B_DK
HGRN_COLS = 5 * B_W

HQ = 16
HKV = 4
GQ = HQ // HKV
HD = 64
WINDOW = 128
ATT_BLOCK = 128
ATT_STACK = GQ
AX_DIM = HD // 2
ROPE_BASE = 10000.0
GRID_W = 64
Q_COLS = HQ * HD
KV_COLS = HKV * HD
ROPE_COLS = Q_COLS + KV_COLS
ATT_COLS = Q_COLS + 2 * KV_COLS

N_EXPERTS = 8
LANES = 128
CHUNK = 64
SCAN_BLOCK = 256
MXU_WIDTH = 256
RWKV_GROUP = MXU_WIDTH // A_HD
TOK_TILE = 256
VMEM_LIMIT = 56 * 1024 * 1024

OP_KAP, OP_RT, OP_KBAR, OP_BBAR, OP_KGAM, OP_BGAM = range(6)
N_OPS = 6


def _cparams(sem):
    return pltpu.CompilerParams(dimension_semantics=sem, vmem_limit_bytes=VMEM_LIMIT)


def _bf(x):
    return x.astype(BF16)


def _dot(a, b):
    return jnp.dot(_bf(a), _bf(b), preferred_element_type=F32)


def _dot_nt(a, b):
    return lax.dot_general(_bf(a), _bf(b), (((1,), (1,)), ((), ())), preferred_element_type=F32)


def _dot_tn(a, b):
    return lax.dot_general(_bf(a), _bf(b), (((0,), (0,)), ((), ())), preferred_element_type=F32)


def _dot_f32(a, b):
    a_hi, b_hi = _bf(a), _bf(b)
    a_lo, b_lo = _bf(a - a_hi.astype(F32)), _bf(b - b_hi.astype(F32))
    acc = jnp.dot(a_hi, b_hi, preferred_element_type=F32)
    acc += jnp.dot(a_hi, b_lo, preferred_element_type=F32)
    acc += jnp.dot(a_lo, b_hi, preferred_element_type=F32)
    return acc


def _sigmoid(x):
    return 0.5 * jnp.tanh(0.5 * x) + 0.5


def _silu(x):
    return x * _sigmoid(x)


def _segsum(x, ones_bd):
    hi = _bf(x)
    lo = _bf(x - hi.astype(F32))
    return jnp.dot(hi, ones_bd, preferred_element_type=F32) + jnp.dot(lo, ones_bd, preferred_element_type=F32)


def _dot_split(a, x):
    hi = _bf(x)
    lo = _bf(x - hi.astype(F32))
    return jnp.dot(a, hi, preferred_element_type=F32) + jnp.dot(a, lo, preferred_element_type=F32)


def _block_ones(width, seg):
    i = jnp.arange(width) // seg
    return (i[:, None] == i[None, :]).astype(BF16)


def _modulate(x, g, modl_ref, modc_ref, row, pos0, ctx_len, pos=None):
    y = x * lax.rsqrt(jnp.mean(x * x, axis=-1, keepdims=True) + NORM_EPS) * g
    if pos is None:
        pos = pos0 + lax.broadcasted_iota(jnp.int32, (x.shape[0], 1), 0)
    is_ctx = pos < ctx_len
    shift = jnp.where(is_ctx, modc_ref[0, row:row + 1, :], modl_ref[0, row:row + 1, :])
    scale = jnp.where(is_ctx, modc_ref[0, row + 1:row + 2, :], modl_ref[0, row + 1:row + 2, :])
    return y * (1.0 + scale) + shift


def _gate_vec(modl_ref, modc_ref, row, pos0, n, ctx_len):
    pos = pos0 + lax.broadcasted_iota(jnp.int32, (n, 1), 0)
    return jnp.where(pos < ctx_len, modc_ref[0, row:row + 1, :], modl_ref[0, row:row + 1, :])


def _mod_specs(d, ctx_row):
    return [pl.BlockSpec((1, N_MOD, d), lambda i, *_: (i, 0, 0)),
            pl.BlockSpec((1, N_MOD, d), lambda *_: (ctx_row, 0, 0))]


def _adaln_kernel(c_ref, w_ref, b_ref, o_ref):
    o_ref[...] = _dot(_silu(c_ref[...]), w_ref[...]) + b_ref[...]


def _adaln(cvec, w, b):
    r, d = cvec.shape
    n = w.shape[1]
    tn = 1024
    return pl.pallas_call(
        _adaln_kernel,
        grid=(n // tn,),
        in_specs=[pl.BlockSpec((r, d), lambda j: (0, 0)),
                  pl.BlockSpec((d, tn), lambda j: (0, j)),
                  pl.BlockSpec((1, tn), lambda j: (0, j))],
        out_specs=pl.BlockSpec((r, tn), lambda j: (0, j)),
        out_shape=jax.ShapeDtypeStruct((r, n), F32),
        compiler_params=_cparams(("parallel",)),
        name="adaln",
    )(cvec, w, b.reshape(1, n))


def _proj_rope_kernel(x_ref, modl_ref, modc_ref, g_ref, w_ref, wrot_ref, cos_ref, sin_ref, o_ref, *, ctx_len, tm,
                      rope_cols):
    pos0 = pl.program_id(1) * tm
    h = _bf(_modulate(x_ref[0], g_ref[...], modl_ref, modc_ref, 0, pos0, ctx_len))
    y = jnp.dot(h, w_ref[...], preferred_element_type=F32)
    yr = jnp.dot(h, wrot_ref[...], preferred_element_type=F32)
    o_ref[0, :, :rope_cols] = y[:, :rope_cols] * cos_ref[...] + yr * sin_ref[...]
    o_ref[0, :, rope_cols:] = y[:, rope_cols:]


def _proj_rope(xcat, mods, ctx_row, g, w, wrot, cos, sin, ctx_len):
    b, s, d = xcat.shape
    n = w.shape[1]
    tm = TOK_TILE
    rope_cols = wrot.shape[1]
    return pl.pallas_call(
        functools.partial(_proj_rope_kernel, ctx_len=ctx_len, tm=tm, rope_cols=rope_cols),
        grid=(b, s // tm),
        in_specs=[pl.BlockSpec((1, tm, d), lambda i, j: (i, j, 0))] + _mod_specs(d, ctx_row) + [
            pl.BlockSpec((1, d), lambda i, j: (0, 0)),
            pl.BlockSpec((d, n), lambda i, j: (0, 0)),
            pl.BlockSpec((d, rope_cols), lambda i, j: (0, 0)),
            pl.BlockSpec((tm, rope_cols), lambda i, j: (j, 0)),
            pl.BlockSpec((tm, rope_cols), lambda i, j: (j, 0))],
        out_specs=pl.BlockSpec((1, tm, n), lambda i, j: (i, j, 0)),
        out_shape=jax.ShapeDtypeStruct((b, s, n), F32),
        compiler_params=_cparams(("parallel", "parallel")),
        name="proj_rope",
    )(xcat, mods, mods, g.reshape(1, d), w, wrot, cos, sin)


DECAY_SCALE = math.exp(-0.5)


def _stream_specs(tm, d, ctx_len):
    assert ctx_len == tm
    return [pl.BlockSpec((1, tm, d), lambda i, j: (i, 0, 0)),
            pl.BlockSpec((1, tm, d), lambda i, j: (i, jnp.maximum(j - 1, 0), 0))]


def _stream_tile(c_ref, x_ref):
    return jnp.where(pl.program_id(1) == 0, c_ref[0], x_ref[0])


def _rec_in_kernel(c_ref, x_ref, xp_ref, xn_ref, modl_ref, modc_ref, g_ref, w_ref, mu_ref, w0_ref, wup_ref, a0_ref,
                   aup_ref, gup_ref, kk_ref, ka_ref, rk_ref, ones_ref, ops_ref, v_ref, gb_ref, gam_ref, ph_ref, *,
                   ctx_len, seq_len, tm):
    pos0 = pl.program_id(1) * tm
    gvec = g_ref[...]
    h = _bf(_modulate(_stream_tile(c_ref, x_ref), gvec, modl_ref, modc_ref, 0, pos0, ctx_len))
    p = jnp.dot(h, w_ref[:, :RWKV_COLS], preferred_element_type=F32)

    hgrn_pieces = [(RWKV_COLS + c, min(B_W, HGRN_COLS - c)) for c in range(0, HGRN_COLS, B_W)]

    def hgrn_piece():
        if hgrn_pieces:
            lo, width = hgrn_pieces.pop(0)
            ph_ref[0, :, lo - RWKV_COLS:lo - RWKV_COLS + width] = jnp.dot(h, w_ref[:, lo:lo + width],
                                                                          preferred_element_type=F32)

    halo = lax.broadcasted_iota(jnp.int32, (16, 1), 0)
    halo_pos = jnp.where(halo < 8, pos0 - 8 + halo, pos0 + tm - 8 + halo)
    xh = jnp.concatenate([xp_ref[0], xn_ref[0]], axis=0)
    p_halo = jnp.dot(_bf(_modulate(xh, gvec, modl_ref, modc_ref, 0, 0, ctx_len, pos=halo_pos)),
                     w_ref[:, :RWKV_COLS], preferred_element_type=F32)
    rows = lax.broadcasted_iota(jnp.int32, (tm, 1), 0)
    starts_seq = (pos0 == 0) | (pos0 == ctx_len)
    ends_seq = (pos0 + tm == ctx_len) | (pos0 + tm == seq_len)
    prev_halo = jnp.where(starts_seq, 0.0, p_halo[7:8, :])
    next_halo = jnp.where(ends_seq, 0.0, p_halo[8:9, :])
    prev = jnp.where(rows == 0, prev_halo, pltpu.roll(p, 1, 0))
    nxt = jnp.where(rows == tm - 1, next_halo, pltpu.roll(p, tm - 1, 0))
    p = p + mu_ref[...] * (0.5 * (prev + nxt) - p)
    hgrn_piece()

    r = p[:, 0:A_W]
    k = p[:, A_W:2 * A_W]
    v = p[:, 2 * A_W:3 * A_W]
    lo = 3 * A_W
    wd = p[:, lo:lo + DECAY_LORA]
    ad = p[:, lo + DECAY_LORA:lo + DECAY_LORA + AAA_LORA]
    gd = p[:, lo + DECAY_LORA + AAA_LORA:lo + DECAY_LORA + AAA_LORA + GATE_LORA]

    tw = jnp.tanh(wd)
    a = _sigmoid(a0_ref[...] + _dot(ad, aup_ref[...]))
    ones_bd = ones_ref[...]
    kk = k * kk_ref[...]
    kk = kk / jnp.maximum(jnp.sqrt(_segsum(kk * kk, ones_bd)), 1e-12)
    k = k * (1.0 + (a - 1.0) * ka_ref[...])
    b = kk * a
    v_ref[0] = _bf(v)
    gb_ref[0, :, :A_W] = _dot(_sigmoid(gd), gup_ref[...])
    gb_ref[0, :, A_W:] = _segsum(r * k * rk_ref[...], ones_bd) * v
    hgrn_piece()

    trow = lax.broadcasted_iota(jnp.int32, (tm, tm), 0)
    tcol = lax.broadcasted_iota(jnp.int32, (tm, tm), 1)
    same = (trow // CHUNK) == (tcol // CHUNK)
    n_chunks = tm // CHUNK
    for d in range(2):
        lw = -DECAY_SCALE * _sigmoid(w0_ref[d:d + 1, :] + _dot(tw, wup_ref[d]))
        before = (tcol <= trow) if d == 0 else (tcol >= trow)
        g_incl = _dot_split(jnp.where(same & before, 1.0, 0.0).astype(BF16), lw)
        last = [c * CHUNK + (CHUNK - 1 if d == 0 else 0) for c in range(n_chunks)]
        gam = jnp.exp(jnp.concatenate([g_incl[t:t + 1] for t in last], axis=0))
        gam_ref[0, d, 0] = gam
        hgrn_piece()
        e_ng = jnp.exp(-g_incl)
        e_tail = e_ng * jnp.concatenate([jnp.broadcast_to(gam[c:c + 1], (CHUNK, A_W)) for c in range(n_chunks)],
                                        axis=0)
        operands = {OP_KAP: kk * jnp.exp(g_incl - lw),
                    OP_RT: r * jnp.exp(g_incl),
                    OP_KBAR: k * e_ng, OP_BBAR: b * e_ng,
                    OP_KGAM: k * e_tail, OP_BGAM: b * e_tail}
        for sec, val in operands.items():
            ops_ref[0, d, :, sec * A_W:(sec + 1) * A_W] = _bf(val)
        hgrn_piece()
    while hgrn_pieces:
        hgrn_piece()


def _rec_in(ctx, x, mods, ctx_row, g, w, mu, w0, w_up, a0, a_up, g_up, k_k, k_a, r_k):
    b, n_lat, d = x.shape
    ctx_len = ctx.shape[1]
    s = ctx_len + n_lat
    tm = SCAN_BLOCK
    assert n_lat % tm == 0
    nb8 = n_lat // 8
    row = lambda a: a.reshape(1, -1)
    full = lambda a: pl.BlockSpec(a.shape, lambda i, j: (0,) * a.ndim)
    args = [row(g), w, row(mu), w0, w_up, row(a0), a_up, g_up, row(k_k), row(k_a), row(r_k),
            _block_ones(A_W, A_HD)]
    return pl.pallas_call(
        functools.partial(_rec_in_kernel, ctx_len=ctx_len, seq_len=s, tm=tm),
        grid=(b, s // tm),
        in_specs=_stream_specs(tm, d, ctx_len) + [
            pl.BlockSpec((1, 8, d), lambda i, j: (i, jnp.maximum((j - 1) * (tm // 8) - 1, 0), 0)),
            pl.BlockSpec((1, 8, d), lambda i, j: (i, jnp.minimum(j * (tm // 8), nb8 - 1), 0)),
        ] + _mod_specs(d, ctx_row) + [full(a) for a in args],
        out_specs=[pl.BlockSpec((1, 2, tm, N_OPS * A_W), lambda i, j: (i, 0, j, 0)),
                   pl.BlockSpec((1, tm, A_W), lambda i, j: (i, j, 0)),
                   pl.BlockSpec((1, tm, 2 * A_W), lambda i, j: (i, j, 0)),
                   pl.BlockSpec((1, 2, 1, tm // CHUNK, A_W), lambda i, j: (i, 0, j, 0, 0)),
                   pl.BlockSpec((1, tm, HGRN_COLS), lambda i, j: (i, j, 0))],
        out_shape=[jax.ShapeDtypeStruct((b, 2, s, N_OPS * A_W), BF16),
                   jax.ShapeDtypeStruct((b, s, A_W), BF16),
                   jax.ShapeDtypeStruct((b, s, 2 * A_W), F32),
                   jax.ShapeDtypeStruct((b, 2, s // tm, tm // CHUNK, A_W), F32),
                   jax.ShapeDtypeStruct((b, s, HGRN_COLS), F32)],
        compiler_params=_cparams(("parallel", "parallel")),
        name="rec_in",
    )(ctx, x, x, x, mods, mods, *args)


def _chunk_order(d, j, n_ctx_chunks, n_chunks):
    back = jnp.where(j < n_ctx_chunks, n_ctx_chunks - 1 - j, n_chunks - 1 + n_ctx_chunks - j)
    return jnp.where(d == 0, j, back)


def _incl_mask(rev):
    row = lax.broadcasted_iota(jnp.int32, (CHUNK, CHUNK), 0)
    col = lax.broadcasted_iota(jnp.int32, (CHUNK, CHUNK), 1)
    return jnp.where(rev, row - col, col - row) <= 0


def _scan_step_coords(t, n_steps):
    return jnp.minimum(t, n_steps - 1), jnp.maximum(t - 1, 0)


def _rwkv_chunk_kernel(ops_ref, v_ref, gam_ref, o_ref, s_ref, tr_ref, uu_ref, ol_ref, ab_ref,
                       m2_ref, cc_ref, gm_ref, *, n_blocks, n_steps):
    step = pl.program_id(0)
    t_in, t_out = _scan_step_coords(step, n_steps)
    rev = (t_in // n_blocks) % 2 == 1
    rev_out = (t_out // n_blocks) % 2 == 1
    first_out = t_out % n_blocks == 0
    w_slot = step % 2
    r_slot = 1 - w_slot

    @pl.when(step == 0)
    def _():
        s_ref[...] = jnp.zeros_like(s_ref)
        for ref in (tr_ref, uu_ref, ol_ref, ab_ref, m2_ref, cc_ref, gm_ref):
            ref[1] = jnp.zeros(ref.shape[1:], ref.dtype)

    gw = RWKV_GROUP * A_HD
    groups = range(A_HEADS // RWKV_GROUP)
    gsl = [slice(g * gw, (g + 1) * gw) for g in groups]
    row = lax.broadcasted_iota(jnp.int32, (CHUNK, gw), 0)
    col = lax.broadcasted_iota(jnp.int32, (CHUNK, gw), 1) % A_HD
    diff = jnp.where(rev, row - col, col - row)
    strict_c = diff < 0
    incl_c = diff <= 0
    eye_c = jnp.where(diff == 0, 1.0, 0.0)
    bd_mask = (lax.broadcasted_iota(jnp.int32, (gw, gw), 0) // A_HD
               == lax.broadcasted_iota(jnp.int32, (gw, gw), 1) // A_HD)

    def bd(x):
        return jnp.where(bd_mask, jnp.concatenate([x] * RWKV_GROUP, axis=0), jnp.zeros((), x.dtype))

    def stack(x):
        return jnp.concatenate([x[:, h * A_HD:(h + 1) * A_HD] for h in range(RWKV_GROUP)], axis=0)

    n = SCAN_BLOCK // CHUNK
    rows = [pl.ds(pl.multiple_of(jnp.where(rev, (n - 1 - i) * CHUNK, i * CHUNK), CHUNK), CHUNK) for i in range(n)]
    rows_out = [pl.ds(pl.multiple_of(jnp.where(rev_out, (n - 1 - i) * CHUNK, i * CHUNK), CHUNK), CHUNK)
                for i in range(n)]
    units = [(i, g) for i in range(n) for g in groups]

    s = [jnp.where(first_out, 0.0, s_ref[g]) for g in groups]
    zr = {}
    pieces = []

    def advance(i):
        for g in groups:
            ui = i * len(groups) + g
            zr[i, g] = _dot_nt(tr_ref[r_slot, ui], bd(s[g]))
            s[g] = s[g] * gm_ref[r_slot, ui] - jnp.dot(_bf(s[g]), m2_ref[r_slot, ui],
                                                       preferred_element_type=F32) + cc_ref[r_slot, ui]

    def emit_out(i):
        for g in groups:
            ui = i * len(groups) + g
            z = zr[i, g][:CHUNK] + uu_ref[r_slot, ui]
            o_ref[0, 0, rows_out[i], gsl[g]] = (zr[i, g][CHUNK:] + ol_ref[r_slot, ui]
                                                - _dot(ab_ref[r_slot, ui], bd(z)))

    for i in range(n):
        pieces += [functools.partial(advance, i), functools.partial(emit_out, i)]

    def state_piece():
        if pieces:
            pieces.pop(0)()

    def operand(sec, u):
        i, g = u
        return ops_ref[0, 0, rows[i], sec * A_W + g * gw:sec * A_W + (g + 1) * gw]

    kap = {u: operand(OP_KAP, u) for u in units}
    rt = {u: operand(OP_RT, u) for u in units}
    kbar = {u: operand(OP_KBAR, u) for u in units}
    bbar = {u: operand(OP_BBAR, u) for u in units}
    kgam = {u: operand(OP_KGAM, u) for u in units}
    bgam = {u: operand(OP_BGAM, u) for u in units}
    v = {(i, g): v_ref[0, rows[i], gsl[g]] for i, g in units}
    chunk_of = [jnp.where(rev, n - 1 - i, i) for i in range(n)]
    gam = {(i, g): gam_ref[0, 0, 0, pl.ds(chunk_of[i], 1), gsl[g]] for i, g in units}

    x = {u: jnp.concatenate([kap[u], rt[u]], axis=0) for u in units}
    yb = {u: _dot_nt(x[u], bd(bbar[u])) for u in units}
    state_piece()
    a = {u: jnp.where(strict_c, -yb[u][:CHUNK], 0.0) for u in units}
    xs = {u: eye_c + a[u] for u in units}
    pw = {u: _dot(a[u], bd(a[u])) for u in units}
    state_piece()
    for _ in range(4):
        st = {u: _dot(jnp.concatenate([pw[u], xs[u]], axis=0), bd(pw[u])) for u in units}
        pw = {u: st[u][:CHUNK] for u in units}
        xs = {u: xs[u] + st[u][CHUNK:] for u in units}
        state_piece()
    t_inv = {u: xs[u] + _dot(xs[u], bd(pw[u])) for u in units}
    state_piece()
    yk = {u: _dot_nt(x[u], bd(kbar[u])) for u in units}
    ykm = {u: jnp.concatenate([jnp.where(strict_c, yk[u][:CHUNK], 0.0), jnp.where(incl_c, yk[u][CHUNK:], 0.0)],
                              axis=0) for u in units}
    wo = {u: _dot(ykm[u], bd(v[u])) for u in units}
    w1 = {u: wo[u][:CHUNK] for u in units}
    o_loc = {u: wo[u][CHUNK:] for u in units}
    state_piece()
    tk = {u: _dot(t_inv[u], bd(kap[u])) for u in units}
    uu = {u: _dot(t_inv[u], bd(w1[u])) for u in units}
    while pieces:
        state_piece()
    for g in groups:
        s_ref[g] = s[g]
    a_rb = {u: jnp.where(incl_c, yb[u][CHUNK:], 0.0) for u in units}
    tu = {u: _dot_tn(jnp.concatenate([stack(tk[u]), stack(uu[u])], axis=1), bd(bgam[u])) for u in units}
    bd_m2 = {u: _bf(bd(tu[u][:A_HD])) for u in units}
    cc = {u: _dot_tn(stack(v[u]), bd(kgam[u])) - tu[u][A_HD:] for u in units}
    for u in units:
        ui = u[0] * len(groups) + u[1]
        tr_ref[w_slot, ui] = jnp.concatenate([_bf(tk[u]), rt[u]], axis=0)
        uu_ref[w_slot, ui] = uu[u]
        ol_ref[w_slot, ui] = o_loc[u]
        ab_ref[w_slot, ui] = _bf(a_rb[u])
        m2_ref[w_slot, ui] = bd_m2[u]
        cc_ref[w_slot, ui] = cc[u]
        gm_ref[w_slot, ui] = gam[u]


def _rwkv_scan(ops, vv, gam, ctx_len):
    b, _, s, _ = ops.shape
    nc, ncc = s // SCAN_BLOCK, ctx_len // SCAN_BLOCK

    n_steps = b * 2 * nc

    def coords(flat):
        d = (flat // nc) % 2
        return flat // (2 * nc), d, _chunk_order(d, flat % nc, ncc, nc)

    def coords_in(t):
        return coords(_scan_step_coords(t, n_steps)[0])

    def out_map(t):
        bi, d, blk = coords(_scan_step_coords(t, n_steps)[1])
        return bi, d, blk, 0

    def ops_map(t):
        bi, d, blk = coords_in(t)
        return bi, d, blk, 0

    def v_map(t):
        bi, _, blk = coords_in(t)
        return bi, blk, 0

    def gam_map(t):
        bi, d, blk = coords_in(t)
        return bi, d, blk, 0, 0

    n_groups = A_HEADS // RWKV_GROUP
    n_chunks = SCAN_BLOCK // CHUNK
    n_units = n_chunks * n_groups
    gw = RWKV_GROUP * A_HD
    return pl.pallas_call(
        functools.partial(_rwkv_chunk_kernel, n_blocks=nc, n_steps=n_steps),
        grid=(n_steps + 1,),
        in_specs=[pl.BlockSpec((1, 1, SCAN_BLOCK, N_OPS * A_W), ops_map),
                  pl.BlockSpec((1, SCAN_BLOCK, A_W), v_map),
                  pl.BlockSpec((1, 1, 1, n_chunks, A_W), gam_map)],
        out_specs=pl.BlockSpec((1, 1, SCAN_BLOCK, A_W), out_map),
        out_shape=jax.ShapeDtypeStruct((b, 2, s, A_W), F32),
        scratch_shapes=[pltpu.VMEM((n_groups, A_HD, gw), F32),
                        pltpu.VMEM((2, n_units, 2 * CHUNK, gw), BF16),
                        pltpu.VMEM((2, n_units, CHUNK, gw), F32),
                        pltpu.VMEM((2, n_units, CHUNK, gw), F32),
                        pltpu.VMEM((2, n_units, CHUNK, gw), BF16),
                        pltpu.VMEM((2, n_units, gw, gw), BF16),
                        pltpu.VMEM((2, n_units, CHUNK, gw), F32),
                        pltpu.VMEM((2, n_units, 1, gw), F32)],
        compiler_params=_cparams(("arbitrary",)),
        name="rwkv_scan",
    )(ops, vv, gam)


def _hgrn_chunk_kernel(q_ref, i_ref, f_ref, lb_ref, o_ref, s_ref):
    rev = pl.program_id(1) == 1

    @pl.when(pl.program_id(2) == 0)
    def _():
        s_ref[...] = jnp.zeros_like(s_ref)

    incl = _incl_mask(rev)
    incl_bf = jnp.where(incl, 1.0, 0.0).astype(BF16)
    lb = lb_ref[0]
    heads = range(B_HEADS)
    sls = [slice(h * B_DK, (h + 1) * B_DK) for h in heads]

    n = SCAN_BLOCK // CHUNK
    rows = [pl.ds(pl.multiple_of(jnp.where(rev, (n - 1 - i) * CHUNK, i * CHUNK), CHUNK), CHUNK) for i in range(n)]
    units = [(i, h) for i in range(n) for h in heads]
    q_in, k_in, q_st, k_tail, gam, v = ({} for _ in range(6))
    for i in range(n):
        f = lb + (1.0 - lb) * _sigmoid(f_ref[0, rows[i], :])
        logf = jnp.log(f)
        kf = 1.0 - f
        g_incl = _dot_split(incl_bf, logf)
        g_tot = jnp.sum(logf, axis=0, keepdims=True)
        g_mid = g_incl[CHUNK // 2 - 1:CHUNK // 2, :]
        q = _silu(q_ref[0, rows[i], :])
        v_i = i_ref[0, rows[i], :]
        q_in_i = q * jnp.exp(g_incl - g_mid)
        k_in_i = kf * jnp.exp(g_mid - g_incl)
        q_st_i = q_in_i * jnp.exp(g_mid)
        k_tail_i = k_in_i * jnp.exp(g_tot - g_mid)
        for h in heads:
            q_in[i, h], k_in[i, h] = q_in_i[:, sls[h]], k_in_i[:, sls[h]]
            q_st[i, h], k_tail[i, h] = q_st_i[:, sls[h]], k_tail_i[:, sls[h]]
            gam[i, h] = jnp.exp(g_tot)[:, sls[h]]
            v[i, h] = v_i[:, sls[h]]
    att = {u: jnp.where(incl, _dot_nt(q_in[u], k_in[u]), 0.0) for u in units}
    o_loc = {u: _dot(att[u], v[u]) for u in units}
    kv = {u: _dot_tn(v[u], k_tail[u]) for u in units}

    s = [s_ref[h] for h in heads]
    for i in range(n):
        for h in heads:
            o_ref[0, 0, rows[i], sls[h]] = o_loc[i, h] + _dot_nt(q_st[i, h], s[h])
            s[h] = s[h] * gam[i, h] + kv[i, h]
    for h in heads:
        s_ref[h] = s[h]


def _hgrn_scan(p, lb, ctx_len):
    b, s, _ = p.shape
    nc, ncc = s // SCAN_BLOCK, ctx_len // SCAN_BLOCK

    def sec(idx):
        return pl.BlockSpec((1, SCAN_BLOCK, B_W), lambda i, d, j: (i, _chunk_order(d, j, ncc, nc), idx))

    return pl.pallas_call(
        _hgrn_chunk_kernel,
        grid=(b, 2, nc),
        in_specs=[sec(0), sec(1),
                  pl.BlockSpec((1, SCAN_BLOCK, B_W), lambda i, d, j: (i, _chunk_order(d, j, ncc, nc), 2 + d)),
                  pl.BlockSpec((1, 1, B_W), lambda i, d, j: (d, 0, 0))],
        out_specs=pl.BlockSpec((1, 1, SCAN_BLOCK, B_W), lambda i, d, j: (i, d, _chunk_order(d, j, ncc, nc), 0)),
        out_shape=jax.ShapeDtypeStruct((b, 2, s, B_W), F32),
        scratch_shapes=[pltpu.VMEM((B_HEADS, B_DK, B_DK), F32)],
        compiler_params=_cparams(("parallel", "parallel", "arbitrary")),
        name="hgrn_scan",
    )(p, p, p, lb)


def _rec_out_kernel(c_ref, x_ref, modl_ref, modc_ref, oa_ref, g_ref, bonus_ref, ob_ref, gate_ref, lnw_ref, lnb_ref,
                    hgn_ref, ones_a_ref, ones_b_ref, w_ref, o_ref, *, ctx_len, tm):
    pos0 = pl.program_id(1) * tm
    oa = oa_ref[0, 0] + oa_ref[0, 1]
    ones_a = ones_a_ref[...]
    mean = _segsum(oa, ones_a) * (1.0 / A_HD)
    cen = oa - mean
    var = _segsum(cen * cen, ones_a) * (1.0 / A_HD)
    ya = (cen * lax.rsqrt(var + GN_EPS) * lnw_ref[...] + lnb_ref[...] + bonus_ref[0]) * g_ref[0]
    ob = ob_ref[0, 0] + ob_ref[0, 1]
    ms = _segsum(ob * ob, ones_b_ref[...]) * (1.0 / B_DK)
    yb = ob * lax.rsqrt(ms + NORM_EPS) * hgn_ref[...] * _silu(gate_ref[0])
    y = _dot(ya, w_ref[:A_W, :]) + _dot(yb, w_ref[A_W:, :])
    gate = _gate_vec(modl_ref, modc_ref, 2, pos0, tm, ctx_len)
    o_ref[0] = _stream_tile(c_ref, x_ref) + gate * y


def _rec_out(ctx, x, mods, ctx_row, oa, g_bonus, ob, p_hgrn, ln_w, ln_b, hg_norm, w_out):
    b, n_lat, d = x.shape
    ctx_len = ctx.shape[1]
    s = ctx_len + n_lat
    tm = TOK_TILE
    row = lambda a: a.reshape(1, -1)
    full = lambda a: pl.BlockSpec(a.shape, lambda i, j: (0,) * a.ndim)
    consts = [row(ln_w), row(ln_b), row(jnp.tile(hg_norm, B_HEADS)), _block_ones(A_W, A_HD),
              _block_ones(B_W, B_DK), w_out]
    return pl.pallas_call(
        functools.partial(_rec_out_kernel, ctx_len=ctx_len, tm=tm),
        grid=(b, s // tm),
        in_specs=_stream_specs(tm, d, ctx_len) + _mod_specs(d, ctx_row) + [
            pl.BlockSpec((1, 2, tm, A_W), lambda i, j: (i, 0, j, 0)),
            pl.BlockSpec((1, tm, A_W), lambda i, j: (i, j, 0)),
            pl.BlockSpec((1, tm, A_W), lambda i, j: (i, j, 1)),
            pl.BlockSpec((1, 2, tm, B_W), lambda i, j: (i, 0, j, 0)),
            pl.BlockSpec((1, tm, B_W), lambda i, j: (i, j, 4)),
        ] + [full(a) for a in consts],
        out_specs=pl.BlockSpec((1, tm, d), lambda i, j: (i, j, 0)),
        out_shape=jax.ShapeDtypeStruct((b, s, d), F32),
        compiler_params=_cparams(("parallel", "parallel")),
        name="rec_out",
    )(ctx, x, mods, mods, oa, g_bonus, g_bonus, ob, p_hgrn, *consts)


def _att_out_kernel(x_ref, modl_ref, modc_ref, y_ref, w_ref, g_ref, rw_ref, rb_ref, o_ref, tw_ref, e_ref, f_ref, *,
                    tm):
    x = x_ref[0] + _gate_vec(modl_ref, modc_ref, 2, 0, tm, 0) * _dot(y_ref[0], w_ref[...])
    o_ref[0] = x
    f = _modulate(x, g_ref[...], modl_ref, modc_ref, 3, 0, 0)
    f_ref[0] = f
    logits = _dot_f32(f, rw_ref[...]) + rb_ref[...]
    lane = lax.broadcasted_iota(jnp.int32, logits.shape, 1).astype(F32)
    logits = jnp.where(lane < N_EXPERTS, logits, NEG_INF)
    ex = jnp.exp(logits - jnp.max(logits, axis=-1, keepdims=True))
    probs = ex / jnp.sum(ex, axis=-1, keepdims=True)
    p1 = jnp.max(probs, axis=-1, keepdims=True)
    i1 = jnp.min(jnp.where(probs == p1, lane, float(LANES)), axis=-1, keepdims=True)
    rest = jnp.where(lane == i1, -1.0, probs)
    p2 = jnp.max(rest, axis=-1, keepdims=True)
    i2 = jnp.min(jnp.where(rest == p2, lane, float(LANES)), axis=-1, keepdims=True)
    tot = p1 + p2
    tw_ref[0] = jnp.where(lane == 0.0, p1 / tot, jnp.where(lane == 1.0, p2 / tot, 0.0))
    e_ref[0] = jnp.where(lane == 0.0, i1, jnp.where(lane == 1.0, i2, 0.0)).astype(jnp.int32)


def _att_out(xcat, mods, y, w, g, router_w, router_b, x_skip):
    b, s, k = y.shape
    d = xcat.shape[-1]
    tm = TOK_TILE
    skip = x_skip // tm
    wpad = jnp.zeros((d, LANES), F32).at[:, :N_EXPERTS].set(router_w)
    bpad = jnp.zeros((1, LANES), F32).at[0, :N_EXPERTS].set(router_b)
    tile_spec = pl.BlockSpec((1, tm, d), lambda i, j: (i, j, 0))
    lane_spec = pl.BlockSpec((1, tm, LANES), lambda i, j: (i, j, 0))
    return pl.pallas_call(
        functools.partial(_att_out_kernel, tm=tm),
        grid=(b, s // tm),
        in_specs=[pl.BlockSpec((1, tm, d), lambda i, j: (i, j + skip, 0))] + _mod_specs(d, 0) + [
            pl.BlockSpec((1, tm, k), lambda i, j: (i, j, 0)),
            pl.BlockSpec((k, d), lambda i, j: (0, 0)),
            pl.BlockSpec((1, d), lambda i, j: (0, 0)),
            pl.BlockSpec((d, LANES), lambda i, j: (0, 0)),
            pl.BlockSpec((1, LANES), lambda i, j: (0, 0))],
        out_specs=[tile_spec, lane_spec, lane_spec, tile_spec],
        out_shape=[jax.ShapeDtypeStruct((b, s, d), F32), jax.ShapeDtypeStruct((b, s, LANES), F32),
                   jax.ShapeDtypeStruct((b, s, LANES), jnp.int32), jax.ShapeDtypeStruct((b, s, d), F32)],
        compiler_params=_cparams(("parallel", "parallel")),
        name="att_out_router",
    )(xcat, mods, mods, y, w, g.reshape(1, d), wpad, bpad)


def _ffn_kernel(x_ref, modl_ref, modc_ref, g_ref, wg_ref, wu_ref, wd_ref, o_ref, h_ref, acc_ref, *, ctx_len, tm,
                n_f):
    f = pl.program_id(2)
    pos0 = pl.program_id(1) * tm

    @pl.when(f == 0)
    def _():
        h_ref[...] = _bf(_modulate(x_ref[0], g_ref[...], modl_ref, modc_ref, 3, pos0, ctx_len))
        acc_ref[...] = jnp.zeros_like(acc_ref)

    h = h_ref[...]
    act = _silu(jnp.dot(h, wg_ref[...], preferred_element_type=F32)) * jnp.dot(h, wu_ref[...],
                                                                               preferred_element_type=F32)
    acc_ref[...] += _dot(act, wd_ref[...])

    @pl.when(f == n_f - 1)
    def _():
        o_ref[0] = x_ref[0] + _gate_vec(modl_ref, modc_ref, 5, pos0, tm, ctx_len) * acc_ref[...]


def _ffn(x, mods, ctx_row, g, wg, wu, wd, ctx_len, tm, tf):
    b, s, d = x.shape
    ff = wg.shape[1]
    n_f = ff // tf
    return pl.pallas_call(
        functools.partial(_ffn_kernel, ctx_len=ctx_len, tm=tm, n_f=n_f),
        grid=(b, s // tm, n_f),
        in_specs=[pl.BlockSpec((1, tm, d), lambda i, j, f: (i, j, 0))] + _mod_specs(d, ctx_row) + [
            pl.BlockSpec((1, d), lambda i, j, f: (0, 0)),
            pl.BlockSpec((d, tf), lambda i, j, f: (0, f)),
            pl.BlockSpec((d, tf), lambda i, j, f: (0, f)),
            pl.BlockSpec((tf, d), lambda i, j, f: (f, 0))],
        out_specs=pl.BlockSpec((1, tm, d), lambda i, j, f: (i, j, 0)),
        out_shape=jax.ShapeDtypeStruct((b, s, d), F32),
        scratch_shapes=[pltpu.VMEM((tm, d), BF16), pltpu.VMEM((tm, d), F32)],
        compiler_params=_cparams(("parallel", "parallel", "arbitrary")),
        name="ffn",
    )(x, mods, mods, g.reshape(1, d), wg, wu, wd)


def _attn_kernel(sink_ref, q_ref, kc_ref, vc_ref, kp_ref, kq_ref, kn_ref, vp_ref, vq_ref, vn_ref, o_ref, *,
                 ctx_len, n_lat):
    i = pl.program_id(1)
    nk = ctx_len + 3 * ATT_BLOCK
    k_all = jnp.concatenate([kc_ref[0], kp_ref[0], kq_ref[0], kn_ref[0]], axis=0)
    v_all = jnp.concatenate([vc_ref[0], vp_ref[0], vq_ref[0], vn_ref[0]], axis=0)
    row = lax.broadcasted_iota(jnp.int32, (ATT_BLOCK, nk), 0)
    col = lax.broadcasted_iota(jnp.int32, (ATT_BLOCK, nk), 1)
    rel = col - (ctx_len + ATT_BLOCK)
    kabs = i * ATT_BLOCK + rel
    band = (jnp.abs(row - rel) <= WINDOW) & (kabs >= 0) & (kabs < n_lat)
    bias = jnp.where((col < ctx_len) | band, 0.0, NEG_INF)
    bias = jnp.concatenate([bias] * ATT_STACK, axis=0)
    grp = lax.broadcasted_iota(jnp.int32, (ATT_STACK * ATT_BLOCK, 1), 0) // ATT_BLOCK
    scale = HD ** -0.5
    for hq0 in range(0, HQ, ATT_STACK):
        hk = hq0 // GQ
        kh = _bf(k_all[:, hk * HD:(hk + 1) * HD])
        vh = _bf(v_all[:, hk * HD:(hk + 1) * HD])
        heads = [hq0 + g for g in range(ATT_STACK)]
        q = jnp.concatenate([q_ref[0, :, hq * HD:(hq + 1) * HD] for hq in heads], axis=0) * scale
        sk = jnp.zeros((ATT_STACK * ATT_BLOCK, 1), F32)
        for g, hq in enumerate(heads):
            sk = jnp.where(grp == g, sink_ref[hq], sk)
        s = _dot_nt(q, kh) + bias
        m = jnp.maximum(jnp.max(s, axis=-1, keepdims=True), sk)
        p = jnp.exp(s - m)
        den = jnp.sum(p, axis=-1, keepdims=True) + jnp.exp(sk - m)
        o = jnp.dot(_bf(p), vh, preferred_element_type=F32) / den
        for g, hq in enumerate(heads):
            o_ref[0, :, hq * HD:(hq + 1) * HD] = o[g * ATT_BLOCK:(g + 1) * ATT_BLOCK]


def _attention(qkv, sink, ctx_len):
    b, s, _ = qkv.shape
    n_lat = s - ctx_len
    nb = n_lat // ATT_BLOCK
    cb = ctx_len // ATT_BLOCK
    kcol = Q_COLS // KV_COLS
    vcol = kcol + 1

    def band(colblk, shift):
        return pl.BlockSpec((1, ATT_BLOCK, KV_COLS),
                            lambda bi, i: (bi, cb + jnp.clip(i + shift, 0, nb - 1), colblk))

    return pl.pallas_call(
        functools.partial(_attn_kernel, ctx_len=ctx_len, n_lat=n_lat),
        grid=(b, nb),
        in_specs=[pl.BlockSpec(memory_space=pltpu.SMEM),
                  pl.BlockSpec((1, ATT_BLOCK, Q_COLS), lambda bi, i: (bi, cb + i, 0)),
                  pl.BlockSpec((1, ctx_len, KV_COLS), lambda bi, i: (bi, 0, kcol)),
                  pl.BlockSpec((1, ctx_len, KV_COLS), lambda bi, i: (bi, 0, vcol)),
                  band(kcol, -1), band(kcol, 0), band(kcol, 1),
                  band(vcol, -1), band(vcol, 0), band(vcol, 1)],
        out_specs=pl.BlockSpec((1, ATT_BLOCK, Q_COLS), lambda bi, i: (bi, i, 0)),
        out_shape=jax.ShapeDtypeStruct((b, n_lat, Q_COLS), F32),
        compiler_params=_cparams(("parallel", "parallel")),
        name="attention",
    )(sink, qkv, qkv, qkv, qkv, qkv, qkv, qkv, qkv, qkv)


MOE_ROWS = 256


def _row_scatter_kernel(dest_ref, pad_ref, f_ref, xs_hbm, zero_ref, sem, *, tm, n_pad):
    base = pl.program_id(0) * tm

    @pl.when(pl.program_id(0) == 0)
    def _():
        zero_ref[...] = jnp.zeros_like(zero_ref)

        def zero_row(r, carry):
            pltpu.make_async_copy(zero_ref, xs_hbm.at[pl.ds(pad_ref[r], 1)], sem).start()
            return carry

        lax.fori_loop(0, n_pad, zero_row, 0, unroll=8)
        for _ in range(n_pad // tm):
            pltpu.make_async_copy(f_ref, xs_hbm.at[pl.ds(0, tm)], sem).wait()

    def issue(r, carry):
        slot = 2 * (base + r)
        pltpu.make_async_copy(f_ref.at[pl.ds(r, 1)], xs_hbm.at[pl.ds(dest_ref[slot], 1)], sem).start()
        pltpu.make_async_copy(f_ref.at[pl.ds(r, 1)], xs_hbm.at[pl.ds(dest_ref[slot + 1], 1)], sem).start()
        return carry

    lax.fori_loop(0, tm, issue, 0, unroll=8)
    for _ in range(2):
        pltpu.make_async_copy(f_ref, xs_hbm.at[pl.ds(0, tm)], sem).wait()


def _row_scatter(f, dest, pad_rows, n_rows):
    n_tok, d = f.shape
    tm = TOK_TILE
    n_pad = pad_rows.shape[0]
    assert n_pad % tm == 0 and 2 * n_tok + n_pad == n_rows
    return pl.pallas_call(
        functools.partial(_row_scatter_kernel, tm=tm, n_pad=n_pad),
        grid_spec=pltpu.PrefetchScalarGridSpec(
            num_scalar_prefetch=2,
            grid=(n_tok // tm,),
            in_specs=[pl.BlockSpec((tm, d), lambda i, dr, pr: (i, 0))],
            out_specs=pl.BlockSpec(memory_space=pl.ANY),
            scratch_shapes=[pltpu.VMEM((1, d), f.dtype), pltpu.SemaphoreType.DMA(())]),
        out_shape=jax.ShapeDtypeStruct((n_rows, d), f.dtype),
        compiler_params=_cparams(("arbitrary",)),
        name="moe_scatter",
    )(dest, pad_rows, f)


def _expert_kernel(be_ref, x_ref, wg_ref, wu_ref, wd_ref, o_ref):
    h = _bf(x_ref[...])
    act = _silu(jnp.dot(h, wg_ref[0], preferred_element_type=F32)) * jnp.dot(h, wu_ref[0],
                                                                              preferred_element_type=F32)
    o_ref[...] = _dot(act, wd_ref[0])


def _experts(xs, block_e, wg, wu, wd):
    n_rows, d = xs.shape
    ff = wg.shape[2]
    rows = MOE_ROWS
    return pl.pallas_call(
        _expert_kernel,
        grid_spec=pltpu.PrefetchScalarGridSpec(
            num_scalar_prefetch=1,
            grid=(n_rows // rows,),
            in_specs=[pl.BlockSpec((rows, d), lambda i, be: (i, 0)),
                      pl.BlockSpec((1, d, ff), lambda i, be: (be[i], 0, 0)),
                      pl.BlockSpec((1, d, ff), lambda i, be: (be[i], 0, 0)),
                      pl.BlockSpec((1, ff, d), lambda i, be: (be[i], 0, 0))],
            out_specs=pl.BlockSpec((rows, d), lambda i, be: (i, 0))),
        out_shape=jax.ShapeDtypeStruct((n_rows, d), F32),
        compiler_params=_cparams(("arbitrary",)),
        name="moe_experts",
    )(block_e, xs, wg, wu, wd)


def _combine_kernel(dest_ref, x_ref, modl_ref, modc_ref, w_ref, gfin_ref, ys_hbm, o_ref, y1_ref, y2_ref, sem, *,
                    tm, tiles_per_row):
    tile = pl.program_id(0) * tiles_per_row + pl.program_id(1)
    n_tiles = pl.num_programs(0) * tiles_per_row
    buf = tile % 2

    def fetch(t, b):
        def issue(r, carry):
            slot = 2 * (t * tm + r)
            pltpu.make_async_copy(ys_hbm.at[pl.ds(dest_ref[slot], 1)], y1_ref.at[b, pl.ds(r, 1)], sem.at[b]).start()
            pltpu.make_async_copy(ys_hbm.at[pl.ds(dest_ref[slot + 1], 1)], y2_ref.at[b, pl.ds(r, 1)],
                                  sem.at[b]).start()
            return carry

        lax.fori_loop(0, tm, issue, 0, unroll=8)

    @pl.when(tile == 0)
    def _():
        fetch(0, 0)

    @pl.when(tile + 1 < n_tiles)
    def _():
        fetch(tile + 1, 1 - buf)

    pltpu.make_async_copy(ys_hbm.at[pl.ds(0, tm)], y1_ref.at[buf], sem.at[buf]).wait()
    pltpu.make_async_copy(ys_hbm.at[pl.ds(0, tm)], y2_ref.at[buf], sem.at[buf]).wait()
    w = w_ref[0]
    moe = w[:, 0:1] * y1_ref[buf] + w[:, 1:2] * y2_ref[buf]
    y = x_ref[0] + _gate_vec(modl_ref, modc_ref, 5, 0, tm, 0) * moe
    o_ref[0] = y * lax.rsqrt(jnp.mean(y * y, axis=-1, keepdims=True) + NORM_EPS) * gfin_ref[...]


def _combine(x, mods, tokw, dest, ys, final_g):
    b, s, d = x.shape
    tm = TOK_TILE
    tiles = s // tm
    return pl.pallas_call(
        functools.partial(_combine_kernel, tm=tm, tiles_per_row=tiles),
        grid_spec=pltpu.PrefetchScalarGridSpec(
            num_scalar_prefetch=1,
            grid=(b, tiles),
            in_specs=[pl.BlockSpec((1, tm, d), lambda i, j, dr: (i, j, 0)),
                      pl.BlockSpec((1, N_MOD, d), lambda i, j, dr: (i, 0, 0)),
                      pl.BlockSpec((1, N_MOD, d), lambda i, j, dr: (0, 0, 0)),
                      pl.BlockSpec((1, tm, LANES), lambda i, j, dr: (i, j, 0)),
                      pl.BlockSpec((1, d), lambda i, j, dr: (0, 0)),
                      pl.BlockSpec(memory_space=pl.ANY)],
            out_specs=pl.BlockSpec((1, tm, d), lambda i, j, dr: (i, j, 0)),
            scratch_shapes=[pltpu.VMEM((2, tm, d), F32), pltpu.VMEM((2, tm, d), F32),
                            pltpu.SemaphoreType.DMA((2,))]),
        out_shape=jax.ShapeDtypeStruct((b, s, d), F32),
        compiler_params=_cparams(("arbitrary", "arbitrary")),
        name="moe_combine",
    )(dest, x, mods, mods, tokw, final_g.reshape(1, d), ys)


def _moe_routing(eidx):
    m = eidx.shape[0]
    oh = (eidx[:, None] == jnp.arange(N_EXPERTS, dtype=jnp.int32)[None, :]).astype(jnp.int32)
    csum = jnp.cumsum(oh, axis=0)
    rank = jnp.sum((csum - oh) * oh, axis=1)
    counts = csum[-1]
    padded = (counts + MOE_ROWS - 1) // MOE_ROWS * MOE_ROWS
    pad_end = jnp.cumsum(padded)
    dest = jnp.sum(oh * (pad_end - padded)[None, :], axis=1) + rank
    n_blocks = m // MOE_ROWS + N_EXPERTS
    starts = jnp.arange(n_blocks, dtype=jnp.int32) * MOE_ROWS
    block_e = jnp.minimum(jnp.sum((starts[:, None] >= pad_end[None, :]).astype(jnp.int32), axis=1), N_EXPERTS - 1)
    pad_cnt = padded - counts
    pad_cum = jnp.cumsum(pad_cnt)
    idx = jnp.arange(N_EXPERTS * MOE_ROWS, dtype=jnp.int32)
    owner = jnp.sum((idx[:, None] >= pad_cum[None, :]).astype(jnp.int32), axis=1)
    own = (jnp.minimum(owner, N_EXPERTS - 1)[:, None] == jnp.arange(N_EXPERTS)[None, :]).astype(jnp.int32)
    in_expert = jnp.sum(own * (pad_end - pad_cnt)[None, :], axis=1) + idx - jnp.sum(
        own * (pad_cum - pad_cnt)[None, :], axis=1)
    pad_rows = jnp.where(owner < N_EXPERTS, in_expert, pad_end[-1] + idx - pad_cum[-1])
    return dest.astype(jnp.int32), block_e, pad_rows.astype(jnp.int32), n_blocks * MOE_ROWS


def _rope_tables(n_lat, ctx_len):
    rows = n_lat // GRID_W
    row = jnp.repeat(jnp.arange(rows, dtype=F32), GRID_W)
    col = jnp.tile(jnp.arange(GRID_W, dtype=F32), rows)
    inv = ROPE_BASE ** (-jnp.arange(0, AX_DIM, 2, dtype=F32) / AX_DIM)
    ar, ac = row[:, None] * inv, col[:, None] * inv
    cos = jnp.concatenate([jnp.cos(ar), jnp.cos(ar), jnp.cos(ac), jnp.cos(ac)], axis=-1)
    sin = jnp.concatenate([-jnp.sin(ar), jnp.sin(ar), -jnp.sin(ac), jnp.sin(ac)], axis=-1)
    n_heads = ROPE_COLS // HD
    cos = jnp.concatenate([jnp.ones((ctx_len, HD), F32), cos], axis=0)
    sin = jnp.concatenate([jnp.zeros((ctx_len, HD), F32), sin], axis=0)
    half = AX_DIM // 2
    j = jnp.arange(HD)
    partner = jnp.where((j % AX_DIM) < half, j + half, j - half)
    perm = (jnp.arange(n_heads)[:, None] * HD + partner[None, :]).reshape(-1)
    return jnp.tile(cos, (1, n_heads)), jnp.tile(sin, (1, n_heads)), perm


def kernel(x, c, ctx, c_ctx, mod_w, mod_b, norm_mix, norm_ffn, norm_final, rec_w_in, rec_w_out, rwkv_mu, rwkv_w0, rwkv_w_up, rwkv_a0, rwkv_a_up, rwkv_g_up, rwkv_k_k, rwkv_k_a, rwkv_r_k, rwkv_ln_w, rwkv_ln_b, hgrn_lb, hgrn_norm, ffn_w_gate, ffn_w_up, ffn_w_down, att_w_in, att_w_out, att_sink, moe_router, moe_router_b, moe_w_gate, moe_w_up, moe_w_down):
    bsz, n_lat, d = x.shape
    ctx_len = ctx.shape[1]

    n_rows = -(-(bsz + 1) // 8) * 8
    cvec = jnp.zeros((n_rows, d), F32).at[:bsz].set(c).at[bsz].set(c_ctx)
    mods = [_adaln(cvec, mod_w[l], mod_b[l]).reshape(n_rows, N_MOD, d) for l in range(2)]

    ops, vv, g_bonus, gam, p_hgrn = _rec_in(ctx, x, mods[0], bsz, norm_mix[0], _bf(rec_w_in[0]), rwkv_mu[0],
                                            rwkv_w0[0], rwkv_w_up[0], rwkv_a0[0], rwkv_a_up[0], rwkv_g_up[0],
                                            rwkv_k_k[0], rwkv_k_a[0], rwkv_r_k[0].reshape(-1))
    oa = _rwkv_scan(ops, vv, gam, ctx_len)
    lb = jnp.cumsum(jax.nn.softmax(hgrn_lb.astype(F32), axis=1), axis=1)[:, 0].reshape(2, 1, B_W)
    ob = _hgrn_scan(p_hgrn, lb, ctx_len)
    xcat = _rec_out(ctx, x, mods[0], bsz, oa, g_bonus, ob, p_hgrn, rwkv_ln_w[0], rwkv_ln_b[0], hgrn_norm[0],
                    _bf(rec_w_out[0]))
    xcat = _ffn(xcat, mods[0], bsz, norm_ffn[0], _bf(ffn_w_gate[0]), _bf(ffn_w_up[0]), _bf(ffn_w_down[0]), ctx_len,
                tm=384, tf=2816)

    cos, sin, perm = _rope_tables(n_lat, ctx_len)
    w_att = att_w_in[0]
    qkv = _proj_rope(xcat, mods[1], bsz, norm_mix[1], _bf(w_att), _bf(w_att[:, perm]), cos, sin, ctx_len)
    att = _attention(qkv, att_sink[0], ctx_len)
    x_lat, tokw, eidx, f_lat = _att_out(xcat, mods[1], att, _bf(att_w_out[0]), norm_ffn[1], moe_router[0],
                                        moe_router_b[0], x_skip=ctx_len)
    dest, block_e, pad_rows, n_rows = _moe_routing(eidx[..., :2].reshape(-1))
    xs = _row_scatter(f_lat.reshape(bsz * n_lat, d), dest, pad_rows, n_rows)
    ys = _experts(xs, block_e, _bf(moe_w_gate[0]), _bf(moe_w_up[0]), _bf(moe_w_down[0]))
    return _combine(x_lat, mods[1], tokw, dest, ys, norm_final)
```

```python
import functools
import math

import jax
import jax.numpy as jnp
from jax import lax
from jax.experimental import pallas as pl
from jax.experimental.pallas import tpu as pltpu

F32 = jnp.float32
BF16 = jnp.bfloat16

N_MOD = 6
NORM_EPS = 1e-6
NEG_INF = -1e30

A_HEADS = 8
A_HD = 64
A_W = A_HEADS * A_HD
DECAY_LORA = 64
AAA_LORA = 64
GATE_LORA = 128
RWKV_COLS = 3 * A_W + DECAY_LORA + AAA_LORA + GATE_LORA
GN_EPS = 64e-5

B_HEADS = 4
B_DK = 128
B_W = B_HEADS * B_DK
HGRN_COLS = 5 * B_W

HQ = 16
HKV = 4
GQ = HQ // HKV
HD = 64
WINDOW = 128
ATT_BLOCK = 128
ATT_STACK = GQ
AX_DIM = HD // 2
ROPE_BASE = 10000.0
GRID_W = 64
Q_COLS = HQ * HD
KV_COLS = HKV * HD
ROPE_COLS = Q_COLS + KV_COLS
ATT_COLS = Q_COLS + 2 * KV_COLS

N_EXPERTS = 8
LANES = 128
CHUNK = 64
SCAN_BLOCK = 256
MXU_WIDTH = 256
RWKV_GROUP = MXU_WIDTH // A_HD
TOK_TILE = 256
VMEM_LIMIT = 56 * 1024 * 1024

OP_KAP, OP_RT, OP_KBAR, OP_BBAR, OP_KGAM, OP_BGAM = range(6)
N_OPS = 6


def _cparams(sem):
    return pltpu.CompilerParams(dimension_semantics=sem, vmem_limit_bytes=VMEM_LIMIT)


def _bf(x):
    return x.astype(BF16)


def _dot(a, b):
    return jnp.dot(_bf(a), _bf(b), preferred_element_type=F32)


def _dot_nt(a, b):
    return lax.dot_general(_bf(a), _bf(b), (((1,), (1,)), ((), ())), preferred_element_type=F32)


def _dot_tn(a, b):
    return lax.dot_general(_bf(a), _bf(b), (((0,), (0,)), ((), ())), preferred_element_type=F32)


def _dot_f32(a, b):
    a_hi, b_hi = _bf(a), _bf(b)
    a_lo, b_lo = _bf(a - a_hi.astype(F32)), _bf(b - b_hi.astype(F32))
    acc = jnp.dot(a_hi, b_hi, preferred_element_type=F32)
    acc += jnp.dot(a_hi, b_lo, preferred_element_type=F32)
    acc += jnp.dot(a_lo, b_hi, preferred_element_type=F32)
    return acc


def _sigmoid(x):
    return 0.5 * jnp.tanh(0.5 * x) + 0.5


def _silu(x):
    return x * _sigmoid(x)


def _segsum(x, ones_bd):
    hi = _bf(x)
    lo = _bf(x - hi.astype(F32))
    return jnp.dot(hi, ones_bd, preferred_element_type=F32) + jnp.dot(lo, ones_bd, preferred_element_type=F32)


def _dot_split(a, x):
    hi = _bf(x)
    lo = _bf(x - hi.astype(F32))
    return jnp.dot(a, hi, preferred_element_type=F32) + jnp.dot(a, lo, preferred_element_type=F32)


def _block_ones(width, seg):
    i = jnp.arange(width) // seg
    return (i[:, None] == i[None, :]).astype(BF16)


def _modulate(x, g, modl_ref, modc_ref, row, pos0, ctx_len, pos=None):
    y = x * lax.rsqrt(jnp.mean(x * x, axis=-1, keepdims=True) + NORM_EPS) * g
    if pos is None:
        pos = pos0 + lax.broadcasted_iota(jnp.int32, (x.shape[0], 1), 0)
    is_ctx = pos < ctx_len
    shift = jnp.where(is_ctx, modc_ref[0, row:row + 1, :], modl_ref[0, row:row + 1, :])
    scale = jnp.where(is_ctx, modc_ref[0, row + 1:row + 2, :], modl_ref[0, row + 1:row + 2, :])
    return y * (1.0 + scale) + shift


def _gate_vec(modl_ref, modc_ref, row, pos0, n, ctx_len):
    pos = pos0 + lax.broadcasted_iota(jnp.int32, (n, 1), 0)
    return jnp.where(pos < ctx_len, modc_ref[0, row:row + 1, :], modl_ref[0, row:row + 1, :])


def _mod_specs(d, ctx_row):
    return [pl.BlockSpec((1, N_MOD, d), lambda i, *_: (i, 0, 0)),
            pl.BlockSpec((1, N_MOD, d), lambda *_: (ctx_row, 0, 0))]


def _adaln_kernel(c_ref, w_ref, b_ref, o_ref):
    o_ref[...] = _dot(_silu(c_ref[...]), w_ref[...]) + b_ref[...]


def _adaln(cvec, w, b):
    r, d = cvec.shape
    n = w.shape[1]
    tn = 1024
    return pl.pallas_call(
        _adaln_kernel,
        grid=(n // tn,),
        in_specs=[pl.BlockSpec((r, d), lambda j: (0, 0)),
                  pl.BlockSpec((d, tn), lambda j: (0, j)),
                  pl.BlockSpec((1, tn), lambda j: (0, j))],
        out_specs=pl.BlockSpec((r, tn), lambda j: (0, j)),
        out_shape=jax.ShapeDtypeStruct((r, n), F32),
        compiler_params=_cparams(("parallel",)),
        name="adaln",
    )(cvec, w, b.reshape(1, n))


def _proj_rope_kernel(x_ref, modl_ref, modc_ref, g_ref, w_ref, wrot_ref, cos_ref, sin_ref, o_ref, *, ctx_len, tm,
                      rope_cols):
    pos0 = pl.program_id(1) * tm
    h = _bf(_modulate(x_ref[0], g_ref[...], modl_ref, modc_ref, 0, pos0, ctx_len))
    y = jnp.dot(h, w_ref[...], preferred_element_type=F32)
    yr = jnp.dot(h, wrot_ref[...], preferred_element_type=F32)
    o_ref[0, :, :rope_cols] = y[:, :rope_cols] * cos_ref[...] + yr * sin_ref[...]
    o_ref[0, :, rope_cols:] = y[:, rope_cols:]


def _proj_rope(xcat, mods, ctx_row, g, w, wrot, cos, sin, ctx_len):
    b, s, d = xcat.shape
    n = w.shape[1]
    tm = TOK_TILE
    rope_cols = wrot.shape[1]
    return pl.pallas_call(
        functools.partial(_proj_rope_kernel, ctx_len=ctx_len, tm=tm, rope_cols=rope_cols),
        grid=(b, s // tm),
        in_specs=[pl.BlockSpec((1, tm, d), lambda i, j: (i, j, 0))] + _mod_specs(d, ctx_row) + [
            pl.BlockSpec((1, d), lambda i, j: (0, 0)),
            pl.BlockSpec((d, n), lambda i, j: (0, 0)),
            pl.BlockSpec((d, rope_cols), lambda i, j: (0, 0)),
            pl.BlockSpec((tm, rope_cols), lambda i, j: (j, 0)),
            pl.BlockSpec((tm, rope_cols), lambda i, j: (j, 0))],
        out_specs=pl.BlockSpec((1, tm, n), lambda i, j: (i, j, 0)),
        out_shape=jax.ShapeDtypeStruct((b, s, n), F32),
        compiler_params=_cparams(("parallel", "parallel")),
        name="proj_rope",
    )(xcat, mods, mods, g.reshape(1, d), w, wrot, cos, sin)


DECAY_SCALE = math.exp(-0.5)


def _stream_specs(tm, d, ctx_len):
    assert ctx_len == tm
    return [pl.BlockSpec((1, tm, d), lambda i, j: (i, 0, 0)),
            pl.BlockSpec((1, tm, d), lambda i, j: (i, jnp.maximum(j - 1, 0), 0))]


def _stream_tile(c_ref, x_ref):
    return jnp.where(pl.program_id(1) == 0, c_ref[0], x_ref[0])


def _rec_in_kernel(c_ref, x_ref, xp_ref, xn_ref, modl_ref, modc_ref, g_ref, w_ref, mu_ref, w0_ref, wup_ref, a0_ref,
                   aup_ref, gup_ref, kk_ref, ka_ref, rk_ref, ones_ref, ops_ref, v_ref, gb_ref, gam_ref, ph_ref, *,
                   ctx_len, seq_len, tm):
    pos0 = pl.program_id(1) * tm
    gvec = g_ref[...]
    h = _bf(_modulate(_stream_tile(c_ref, x_ref), gvec, modl_ref, modc_ref, 0, pos0, ctx_len))
    p = jnp.dot(h, w_ref[:, :RWKV_COLS], preferred_element_type=F32)

    hgrn_pieces = [(RWKV_COLS + c, min(B_W, HGRN_COLS - c)) for c in range(0, HGRN_COLS, B_W)]

    def hgrn_piece():
        if hgrn_pieces:
            lo, width = hgrn_pieces.pop(0)
            ph_ref[0, :, lo - RWKV_COLS:lo - RWKV_COLS + width] = jnp.dot(h, w_ref[:, lo:lo + width],
                                                                          preferred_element_type=F32)

    halo = lax.broadcasted_iota(jnp.int32, (16, 1), 0)
    halo_pos = jnp.where(halo < 8, pos0 - 8 + halo, pos0 + tm - 8 + halo)
    xh = jnp.concatenate([xp_ref[0], xn_ref[0]], axis=0)
    p_halo = jnp.dot(_bf(_modulate(xh, gvec, modl_ref, modc_ref, 0, 0, ctx_len, pos=halo_pos)),
                     w_ref[:, :RWKV_COLS], preferred_element_type=F32)
    rows = lax.broadcasted_iota(jnp.int32, (tm, 1), 0)
    starts_seq = (pos0 == 0) | (pos0 == ctx_len)
    ends_seq = (pos0 + tm == ctx_len) | (pos0 + tm == seq_len)
    prev_halo = jnp.where(starts_seq, 0.0, p_halo[7:8, :])
    next_halo = jnp.where(ends_seq, 0.0, p_halo[8:9, :])
    prev = jnp.where(rows == 0, prev_halo, pltpu.roll(p, 1, 0))
    nxt = jnp.where(rows == tm - 1, next_halo, pltpu.roll(p, tm - 1, 0))
    p = p + mu_ref[...] * (0.5 * (prev + nxt) - p)
    hgrn_piece()

    r = p[:, 0:A_W]
    k = p[:, A_W:2 * A_W]
    v = p[:, 2 * A_W:3 * A_W]
    lo = 3 * A_W
    wd = p[:, lo:lo + DECAY_LORA]
    ad = p[:, lo + DECAY_LORA:lo + DECAY_LORA + AAA_LORA]
    gd = p[:, lo + DECAY_LORA + AAA_LORA:lo + DECAY_LORA + AAA_LORA + GATE_LORA]

    tw = jnp.tanh(wd)
    a = _sigmoid(a0_ref[...] + _dot(ad, aup_ref[...]))
    ones_bd = ones_ref[...]
    kk = k * kk_ref[...]
    kk = kk / jnp.maximum(jnp.sqrt(_segsum(kk * kk, ones_bd)), 1e-12)
    k = k * (1.0 + (a - 1.0) * ka_ref[...])
    b = kk * a
    v_ref[0] = _bf(v)
    gb_ref[0, :, :A_W] = _dot(_sigmoid(gd), gup_ref[...])
    gb_ref[0, :, A_W:] = _segsum(r * k * rk_ref[...], ones_bd) * v
    hgrn_piece()

    trow = lax.broadcasted_iota(jnp.int32, (tm, tm), 0)
    tcol = lax.broadcasted_iota(jnp.int32, (tm, tm), 1)
    same = (trow // CHUNK) == (tcol // CHUNK)
    n_chunks = tm // CHUNK
    for d in range(2):
        lw = -DECAY_SCALE * _sigmoid(w0_ref[d:d + 1, :] + _dot(tw, wup_ref[d]))
        before = (tcol <= trow) if d == 0 else (tcol >= trow)
        g_incl = _dot_split(jnp.where(same & before, 1.0, 0.0).astype(BF16), lw)
        last = [c * CHUNK + (CHUNK - 1 if d == 0 else 0) for c in range(n_chunks)]
        gam = jnp.exp(jnp.concatenate([g_incl[t:t + 1] for t in last], axis=0))
        gam_ref[0, d, 0] = gam
        hgrn_piece()
        e_ng = jnp.exp(-g_incl)
        e_tail = e_ng * jnp.concatenate([jnp.broadcast_to(gam[c:c + 1], (CHUNK, A_W)) for c in range(n_chunks)],
                                        axis=0)
        operands = {OP_KAP: kk * jnp.exp(g_incl - lw),
                    OP_RT: r * jnp.exp(g_incl),
                    OP_KBAR: k * e_ng, OP_BBAR: b * e_ng,
                    OP_KGAM: k * e_tail, OP_BGAM: b * e_tail}
        for sec, val in operands.items():
            ops_ref[0, d, :, sec * A_W:(sec + 1) * A_W] = _bf(val)
        hgrn_piece()
    while hgrn_pieces:
        hgrn_piece()


def _rec_in(ctx, x, mods, ctx_row, g, w, mu, w0, w_up, a0, a_up, g_up, k_k, k_a, r_k):
    b, n_lat, d = x.shape
    ctx_len = ctx.shape[1]
    s = ctx_len + n_lat
    tm = SCAN_BLOCK
    assert n_lat % tm == 0
    nb8 = n_lat // 8
    row = lambda a: a.reshape(1, -1)
    full = lambda a: pl.BlockSpec(a.shape, lambda i, j: (0,) * a.ndim)
    args = [row(g), w, row(mu), w0, w_up, row(a0), a_up, g_up, row(k_k), row(k_a), row(r_k),
            _block_ones(A_W, A_HD)]
    return pl.pallas_call(
        functools.partial(_rec_in_kernel, ctx_len=ctx_len, seq_len=s, tm=tm),
        grid=(b, s // tm),
        in_specs=_stream_specs(tm, d, ctx_len) + [
            pl.BlockSpec((1, 8, d), lambda i, j: (i, jnp.maximum((j - 1) * (tm // 8) - 1, 0), 0)),
            pl.BlockSpec((1, 8, d), lambda i, j: (i, jnp.minimum(j * (tm // 8), nb8 - 1), 0)),
        ] + _mod_specs(d, ctx_row) + [full(a) for a in args],
        out_specs=[pl.BlockSpec((1, 2, tm, N_OPS * A_W), lambda i, j: (i, 0, j, 0)),
                   pl.BlockSpec((1, tm, A_W), lambda i, j: (i, j, 0)),
                   pl.BlockSpec((1, tm, 2 * A_W), lambda i, j: (i, j, 0)),
                   pl.BlockSpec((1, 2, 1, tm // CHUNK, A_W), lambda i, j: (i, 0, j, 0, 0)),
                   pl.BlockSpec((1, tm, HGRN_COLS), lambda i, j: (i, j, 0))],
        out_shape=[jax.ShapeDtypeStruct((b, 2, s, N_OPS * A_W), BF16),
                   jax.ShapeDtypeStruct((b, s, A_W), BF16),
                   jax.ShapeDtypeStruct((b, s, 2 * A_W), F32),
                   jax.ShapeDtypeStruct((b, 2, s // tm, tm // CHUNK, A_W), F32),
                   jax.ShapeDtypeStruct((b, s, HGRN_COLS), F32)],
        compiler_params=_cparams(("parallel", "parallel")),
        name="rec_in",
    )(ctx, x, x, x, mods, mods, *args)


def _chunk_order(d, j, n_ctx_chunks, n_chunks):
    back = jnp.where(j < n_ctx_chunks, n_ctx_chunks - 1 - j, n_chunks - 1 + n_ctx_chunks - j)
    return jnp.where(d == 0, j, back)


def _incl_mask(rev):
    row = lax.broadcasted_iota(jnp.int32, (CHUNK, CHUNK), 0)
    col = lax.broadcasted_iota(jnp.int32, (CHUNK, CHUNK), 1)
    return jnp.where(rev, row - col, col - row) <= 0


def _scan_step_coords(t, n_steps):
    return jnp.minimum(t, n_steps - 1), jnp.maximum(t - 1, 0)


def _rwkv_chunk_kernel(ops_ref, v_ref, gam_ref, o_ref, s_ref, tr_ref, uu_ref, ol_ref, ab_ref,
                       m2_ref, cc_ref, gm_ref, *, n_blocks, n_steps):
    step = pl.program_id(0)
    t_in, t_out = _scan_step_coords(step, n_steps)
    rev = (t_in // n_blocks) % 2 == 1
    rev_out = (t_out // n_blocks) % 2 == 1
    first_out = t_out % n_blocks == 0
    w_slot = step % 2
    r_slot = 1 - w_slot

    @pl.when(step == 0)
    def _():
        s_ref[...] = jnp.zeros_like(s_ref)
        for ref in (tr_ref, uu_ref, ol_ref, ab_ref, m2_ref, cc_ref, gm_ref):
            ref[1] = jnp.zeros(ref.shape[1:], ref.dtype)

    gw = RWKV_GROUP * A_HD
    groups = range(A_HEADS // RWKV_GROUP)
    gsl = [slice(g * gw, (g + 1) * gw) for g in groups]
    row = lax.broadcasted_iota(jnp.int32, (CHUNK, gw), 0)
    col = lax.broadcasted_iota(jnp.int32, (CHUNK, gw), 1) % A_HD
    diff = jnp.where(rev, row - col, col - row)
    strict_c = diff < 0
    incl_c = diff <= 0
    eye_c = jnp.where(diff == 0, 1.0, 0.0)
    bd_mask = (lax.broadcasted_iota(jnp.int32, (gw, gw), 0) // A_HD
               == lax.broadcasted_iota(jnp.int32, (gw, gw), 1) // A_HD)

    def bd(x):
        return jnp.where(bd_mask, jnp.concatenate([x] * RWKV_GROUP, axis=0), jnp.zeros((), x.dtype))

    def stack(x):
        return jnp.concatenate([x[:, h * A_HD:(h + 1) * A_HD] for h in range(RWKV_GROUP)], axis=0)

    n = SCAN_BLOCK // CHUNK
    rows = [pl.ds(pl.multiple_of(jnp.where(rev, (n - 1 - i) * CHUNK, i * CHUNK), CHUNK), CHUNK) for i in range(n)]
    rows_out = [pl.ds(pl.multiple_of(jnp.where(rev_out, (n - 1 - i) * CHUNK, i * CHUNK), CHUNK), CHUNK)
                for i in range(n)]
    units = [(i, g) for i in range(n) for g in groups]

    s = [jnp.where(first_out, 0.0, s_ref[g]) for g in groups]
    zr = {}
    pieces = []

    def advance(i):
        for g in groups:
            ui = i * len(groups) + g
            zr[i, g] = _dot_nt(tr_ref[r_slot, ui], bd(s[g]))
            s[g] = s[g] * gm_ref[r_slot, ui] - jnp.dot(_bf(s[g]), m2_ref[r_slot, ui],
                                                       preferred_element_type=F32) + cc_ref[r_slot, ui]

    def emit_out(i):
        for g in groups:
            ui = i * len(groups) + g
            z = zr[i, g][:CHUNK] + uu_ref[r_slot, ui]
            o_ref[0, 0, rows_out[i], gsl[g]] = (zr[i, g][CHUNK:] + ol_ref[r_slot, ui]
                                                - _dot(ab_ref[r_slot, ui], bd(z)))

    for i in range(n):
        pieces += [functools.partial(advance, i), functools.partial(emit_out, i)]

    def state_piece():
        if pieces:
            pieces.pop(0)()

    def operand(sec, u):
        i, g = u
        return ops_ref[0, 0, rows[i], sec * A_W + g * gw:sec * A_W + (g + 1) * gw]

    kap = {u: operand(OP_KAP, u) for u in units}
    rt = {u: operand(OP_RT, u) for u in units}
    kbar = {u: operand(OP_KBAR, u) for u in units}
    bbar = {u: operand(OP_BBAR, u) for u in units}
    kgam = {u: operand(OP_KGAM, u) for u in units}
    bgam = {u: operand(OP_BGAM, u) for u in units}
    v = {(i, g): v_ref[0, rows[i], gsl[g]] for i, g in units}
    chunk_of = [jnp.where(rev, n - 1 - i, i) for i in range(n)]
    gam = {(i, g): gam_ref[0, 0, 0, pl.ds(chunk_of[i], 1), gsl[g]] for i, g in units}

    x = {u: jnp.concatenate([kap[u], rt[u]], axis=0) for u in units}
    yb = {u: _dot_nt(x[u], bd(bbar[u])) for u in units}
    state_piece()
    a = {u: jnp.where(strict_c, -yb[u][:CHUNK], 0.0) for u in units}
    xs = {u: eye_c + a[u] for u in units}
    pw = {u: _dot(a[u], bd(a[u])) for u in units}
    state_piece()
    for _ in range(4):
        st = {u: _dot(jnp.concatenate([pw[u], xs[u]], axis=0), bd(pw[u])) for u in units}
        pw = {u: st[u][:CHUNK] for u in units}
        xs = {u: xs[u] + st[u][CHUNK:] for u in units}
        state_piece()
    t_inv = {u: xs[u] + _dot(xs[u], bd(pw[u])) for u in units}
    state_piece()
    yk = {u: _dot_nt(x[u], bd(kbar[u])) for u in units}
    ykm = {u: jnp.concatenate([jnp.where(strict_c, yk[u][:CHUNK], 0.0), jnp.where(incl_c, yk[u][CHUNK:], 0.0)],
                              axis=0) for u in units}
    wo = {u: _dot(ykm[u], bd(v[u])) for u in units}
    w1 = {u: wo[u][:CHUNK] for u in units}
    o_loc = {u: wo[u][CHUNK:] for u in units}
    state_piece()
    tk = {u: _dot(t_inv[u], bd(kap[u])) for u in units}
    uu = {u: _dot(t_inv[u], bd(w1[u])) for u in units}
    while pieces:
        state_piece()
    for g in groups:
        s_ref[g] = s[g]
    a_rb = {u: jnp.where(incl_c, yb[u][CHUNK:], 0.0) for u in units}
    tu = {u: _dot_tn(jnp.concatenate([stack(tk[u]), stack(uu[u])], axis=1), bd(bgam[u])) for u in units}
    bd_m2 = {u: _bf(bd(tu[u][:A_HD])) for u in units}
    cc = {u: _dot_tn(stack(v[u]), bd(kgam[u])) - tu[u][A_HD:] for u in units}
    for u in units:
        ui = u[0] * len(groups) + u[1]
        tr_ref[w_slot, ui] = jnp.concatenate([_bf(tk[u]), rt[u]], axis=0)
        uu_ref[w_slot, ui] = uu[u]
        ol_ref[w_slot, ui] = o_loc[u]
        ab_ref[w_slot, ui] = _bf(a_rb[u])
        m2_ref[w_slot, ui] = bd_m2[u]
        cc_ref[w_slot, ui] = cc[u]
        gm_ref[w_slot, ui] = gam[u]


def _rwkv_scan(ops, vv, gam, ctx_len):
    b, _, s, _ = ops.shape
    nc, ncc = s // SCAN_BLOCK, ctx_len // SCAN_BLOCK

    n_steps = b * 2 * nc

    def coords(flat):
        d = (flat // nc) % 2
        return flat // (2 * nc), d, _chunk_order(d, flat % nc, ncc, nc)

    def coords_in(t):
        return coords(_scan_step_coords(t, n_steps)[0])

    def out_map(t):
        bi, d, blk = coords(_scan_step_coords(t, n_steps)[1])
        return bi, d, blk, 0

    def ops_map(t):
        bi, d, blk = coords_in(t)
        return bi, d, blk, 0

    def v_map(t):
        bi, _, blk = coords_in(t)
        return bi, blk, 0

    def gam_map(t):
        bi, d, blk = coords_in(t)
        return bi, d, blk, 0, 0

    n_groups = A_HEADS // RWKV_GROUP
    n_chunks = SCAN_BLOCK // CHUNK
    n_units = n_chunks * n_groups
    gw = RWKV_GROUP * A_HD
    return pl.pallas_call(
        functools.partial(_rwkv_chunk_kernel, n_blocks=nc, n_steps=n_steps),
        grid=(n_steps + 1,),
        in_specs=[pl.BlockSpec((1, 1, SCAN_BLOCK, N_OPS * A_W), ops_map),
                  pl.BlockSpec((1, SCAN_BLOCK, A_W), v_map),
                  pl.BlockSpec((1, 1, 1, n_chunks, A_W), gam_map)],
        out_specs=pl.BlockSpec((1, 1, SCAN_BLOCK, A_W), out_map),
        out_shape=jax.ShapeDtypeStruct((b, 2, s, A_W), F32),
        scratch_shapes=[pltpu.VMEM((n_groups, A_HD, gw), F32),
                        pltpu.VMEM((2, n_units, 2 * CHUNK, gw), BF16),
                        pltpu.VMEM((2, n_units, CHUNK, gw), F32),
                        pltpu.VMEM((2, n_units, CHUNK, gw), F32),
                        pltpu.VMEM((2, n_units, CHUNK, gw), BF16),
                        pltpu.VMEM((2, n_units, gw, gw), BF16),
                        pltpu.VMEM((2, n_units, CHUNK, gw), F32),
                        pltpu.VMEM((2, n_units, 1, gw), F32)],
        compiler_params=_cparams(("arbitrary",)),
        name="rwkv_scan",
    )(ops, vv, gam)


def _hgrn_chunk_kernel(q_ref, i_ref, f_ref, lb_ref, o_ref, s_ref):
    rev = pl.program_id(1) == 1

    @pl.when(pl.program_id(2) == 0)
    def _():
        s_ref[...] = jnp.zeros_like(s_ref)

    incl = _incl_mask(rev)
    incl_bf = jnp.where(incl, 1.0, 0.0).astype(BF16)
    lb = lb_ref[0]
    heads = range(B_HEADS)
    sls = [slice(h * B_DK, (h + 1) * B_DK) for h in heads]

    n = SCAN_BLOCK // CHUNK
    rows = [pl.ds(pl.multiple_of(jnp.where(rev, (n - 1 - i) * CHUNK, i * CHUNK), CHUNK), CHUNK) for i in range(n)]
    units = [(i, h) for i in range(n) for h in heads]
    q_in, k_in, q_st, k_tail, gam, v = ({} for _ in range(6))
    for i in range(n):
        f = lb + (1.0 - lb) * _sigmoid(f_ref[0, rows[i], :])
        logf = jnp.log(f)
        kf = 1.0 - f
        g_incl = _dot_split(incl_bf, logf)
        g_tot = jnp.sum(logf, axis=0, keepdims=True)
        g_mid = g_incl[CHUNK // 2 - 1:CHUNK // 2, :]
        q = _silu(q_ref[0, rows[i], :])
        v_i = i_ref[0, rows[i], :]
        q_in_i = q * jnp.exp(g_incl - g_mid)
        k_in_i = kf * jnp.exp(g_mid - g_incl)
        q_st_i = q_in_i * jnp.exp(g_mid)
        k_tail_i = k_in_i * jnp.exp(g_tot - g_mid)
        for h in heads:
            q_in[i, h], k_in[i, h] = q_in_i[:, sls[h]], k_in_i[:, sls[h]]
            q_st[i, h], k_tail[i, h] = q_st_i[:, sls[h]], k_tail_i[:, sls[h]]
            gam[i, h] = jnp.exp(g_tot)[:, sls[h]]
            v[i, h] = v_i[:, sls[h]]
    att = {u: jnp.where(incl, _dot_nt(q_in[u], k_in[u]), 0.0) for u in units}
    o_loc = {u: _dot(att[u], v[u]) for u in units}
    kv = {u: _dot_tn(v[u], k_tail[u]) for u in units}

    s = [s_ref[h] for h in heads]
    for i in range(n):
        for h in heads:
            o_ref[0, 0, rows[i], sls[h]] = o_loc[i, h] + _dot_nt(q_st[i, h], s[h])
            s[h] = s[h] * gam[i, h] + kv[i, h]
    for h in heads:
        s_ref[h] = s[h]


def _hgrn_scan(p, lb, ctx_len):
    b, s, _ = p.shape
    nc, ncc = s // SCAN_BLOCK, ctx_len // SCAN_BLOCK

    def sec(idx):
        return pl.BlockSpec((1, SCAN_BLOCK, B_W), lambda i, d, j: (i, _chunk_order(d, j, ncc, nc), idx))

    return pl.pallas_call(
        _hgrn_chunk_kernel,
        grid=(b, 2, nc),
        in_specs=[sec(0), sec(1),
                  pl.BlockSpec((1, SCAN_BLOCK, B_W), lambda i, d, j: (i, _chunk_order(d, j, ncc, nc), 2 + d)),
                  pl.BlockSpec((1, 1, B_W), lambda i, d, j: (d, 0, 0))],
        out_specs=pl.BlockSpec((1, 1, SCAN_BLOCK, B_W), lambda i, d, j: (i, d, _chunk_order(d, j, ncc, nc), 0)),
        out_shape=jax.ShapeDtypeStruct((b, 2, s, B_W), F32),
        scratch_shapes=[pltpu.VMEM((B_HEADS, B_DK, B_DK), F32)],
        compiler_params=_cparams(("parallel", "parallel", "arbitrary")),
        name="hgrn_scan",
    )(p, p, p, lb)


def _rec_out_kernel(c_ref, x_ref, modl_ref, modc_ref, oa_ref, g_ref, bonus_ref, ob_ref, gate_ref, lnw_ref, lnb_ref,
                    hgn_ref, ones_a_ref, ones_b_ref, w_ref, gffn_ref, wg_ref, wu_ref, wd_ref, o_ref, *, ctx_len, tm):
    pos0 = pl.program_id(1) * tm
    oa = oa_ref[0, 0] + oa_ref[0, 1]
    ones_a = ones_a_ref[...]
    mean = _segsum(oa, ones_a) * (1.0 / A_HD)
    cen = oa - mean
    var = _segsum(cen * cen, ones_a) * (1.0 / A_HD)
    ya = (cen * lax.rsqrt(var + GN_EPS) * lnw_ref[...] + lnb_ref[...] + bonus_ref[0]) * g_ref[0]
    ob = ob_ref[0, 0] + ob_ref[0, 1]
    ms = _segsum(ob * ob, ones_b_ref[...]) * (1.0 / B_DK)
    yb = ob * lax.rsqrt(ms + NORM_EPS) * hgn_ref[...] * _silu(gate_ref[0])
    y = _dot(ya, w_ref[:A_W, :]) + _dot(yb, w_ref[A_W:, :])
    gate = _gate_vec(modl_ref, modc_ref, 2, pos0, tm, ctx_len)
    x1 = _stream_tile(c_ref, x_ref) + gate * y
    h = _bf(_modulate(x1, gffn_ref[...], modl_ref, modc_ref, 3, pos0, ctx_len))
    act = _silu(jnp.dot(h, wg_ref[...], preferred_element_type=F32)) * jnp.dot(h, wu_ref[...],
                                                                               preferred_element_type=F32)
    o_ref[0] = x1 + _gate_vec(modl_ref, modc_ref, 5, pos0, tm, ctx_len) * _dot(act, wd_ref[...])


def _rec_out(ctx, x, mods, ctx_row, oa, g_bonus, ob, p_hgrn, ln_w, ln_b, hg_norm, w_out, g_ffn, wg, wu, wd):
    b, n_lat, d = x.shape
    ctx_len = ctx.shape[1]
    s = ctx_len + n_lat
    tm = TOK_TILE
    row = lambda a: a.reshape(1, -1)
    full = lambda a: pl.BlockSpec(a.shape, lambda i, j: (0,) * a.ndim, pipeline_mode=pl.Buffered(1))
    consts = [row(ln_w), row(ln_b), row(jnp.tile(hg_norm, B_HEADS)), _block_ones(A_W, A_HD),
              _block_ones(B_W, B_DK), w_out, row(g_ffn), wg, wu, wd]
    return pl.pallas_call(
        functools.partial(_rec_out_kernel, ctx_len=ctx_len, tm=tm),
        grid=(b, s // tm),
        in_specs=_stream_specs(tm, d, ctx_len) + _mod_specs(d, ctx_row) + [
            pl.BlockSpec((1, 2, tm, A_W), lambda i, j: (i, 0, j, 0)),
            pl.BlockSpec((1, tm, A_W), lambda i, j: (i, j, 0)),
            pl.BlockSpec((1, tm, A_W), lambda i, j: (i, j, 1)),
            pl.BlockSpec((1, 2, tm, B_W), lambda i, j: (i, 0, j, 0)),
            pl.BlockSpec((1, tm, B_W), lambda i, j: (i, j, 4)),
        ] + [full(a) for a in consts],
        out_specs=pl.BlockSpec((1, tm, d), lambda i, j: (i, j, 0)),
        out_shape=jax.ShapeDtypeStruct((b, s, d), F32),
        compiler_params=_cparams(("parallel", "parallel")),
        name="rec_out",
    )(ctx, x, mods, mods, oa, g_bonus, g_bonus, ob, p_hgrn, *consts)


def _att_out_kernel(x_ref, modl_ref, modc_ref, y_ref, w_ref, g_ref, rw_ref, rb_ref, o_ref, tw_ref, e_ref, f_ref, *,
                    tm):
    x = x_ref[0] + _gate_vec(modl_ref, modc_ref, 2, 0, tm, 0) * _dot(y_ref[0], w_ref[...])
    o_ref[0] = x
    f = _modulate(x, g_ref[...], modl_ref, modc_ref, 3, 0, 0)
    f_ref[0] = f
    logits = _dot_f32(f, rw_ref[...]) + rb_ref[...]
    lane = lax.broadcasted_iota(jnp.int32, logits.shape, 1).astype(F32)
    logits = jnp.where(lane < N_EXPERTS, logits, NEG_INF)
    ex = jnp.exp(logits - jnp.max(logits, axis=-1, keepdims=True))
    probs = ex / jnp.sum(ex, axis=-1, keepdims=True)
    p1 = jnp.max(probs, axis=-1, keepdims=True)
    i1 = jnp.min(jnp.where(probs == p1, lane, float(LANES)), axis=-1, keepdims=True)
    rest = jnp.where(lane == i1, -1.0, probs)
    p2 = jnp.max(rest, axis=-1, keepdims=True)
    i2 = jnp.min(jnp.where(rest == p2, lane, float(LANES)), axis=-1, keepdims=True)
    tot = p1 + p2
    tw_ref[0] = jnp.where(lane == 0.0, p1 / tot, jnp.where(lane == 1.0, p2 / tot, 0.0))
    e_ref[0] = jnp.where(lane == 0.0, i1, jnp.where(lane == 1.0, i2, 0.0)).astype(jnp.int32)


def _att_out(xcat, mods, y, w, g, router_w, router_b, x_skip):
    b, s, k = y.shape
    d = xcat.shape[-1]
    tm = TOK_TILE
    skip = x_skip // tm
    wpad = jnp.zeros((d, LANES), F32).at[:, :N_EXPERTS].set(router_w)
    bpad = jnp.zeros((1, LANES), F32).at[0, :N_EXPERTS].set(router_b)
    tile_spec = pl.BlockSpec((1, tm, d), lambda i, j: (i, j, 0))
    lane_spec = pl.BlockSpec((1, tm, LANES), lambda i, j: (i, j, 0))
    return pl.pallas_call(
        functools.partial(_att_out_kernel, tm=tm),
        grid=(b, s // tm),
        in_specs=[pl.BlockSpec((1, tm, d), lambda i, j: (i, j + skip, 0))] + _mod_specs(d, 0) + [
            pl.BlockSpec((1, tm, k), lambda i, j: (i, j, 0)),
            pl.BlockSpec((k, d), lambda i, j: (0, 0)),
            pl.BlockSpec((1, d), lambda i, j: (0, 0)),
            pl.BlockSpec((d, LANES), lambda i, j: (0, 0)),
            pl.BlockSpec((1, LANES), lambda i, j: (0, 0))],
        out_specs=[tile_spec, lane_spec, lane_spec, tile_spec],
        out_shape=[jax.ShapeDtypeStruct((b, s, d), F32), jax.ShapeDtypeStruct((b, s, LANES), F32),
                   jax.ShapeDtypeStruct((b, s, LANES), jnp.int32), jax.ShapeDtypeStruct((b, s, d), F32)],
        compiler_params=_cparams(("parallel", "parallel")),
        name="att_out_router",
    )(xcat, mods, mods, y, w, g.reshape(1, d), wpad, bpad)


def _attn_kernel(sink_ref, q_ref, kc_ref, vc_ref, kp_ref, kq_ref, kn_ref, vp_ref, vq_ref, vn_ref, o_ref, *,
                 ctx_len, n_lat):
    i = pl.program_id(1)
    nk = ctx_len + 3 * ATT_BLOCK
    k_all = jnp.concatenate([kc_ref[0], kp_ref[0], kq_ref[0], kn_ref[0]], axis=0)
    v_all = jnp.concatenate([vc_ref[0], vp_ref[0], vq_ref[0], vn_ref[0]], axis=0)
    row = lax.broadcasted_iota(jnp.int32, (ATT_BLOCK, nk), 0)
    col = lax.broadcasted_iota(jnp.int32, (ATT_BLOCK, nk), 1)
    rel = col - (ctx_len + ATT_BLOCK)
    kabs = i * ATT_BLOCK + rel
    band = (jnp.abs(row - rel) <= WINDOW) & (kabs >= 0) & (kabs < n_lat)
    bias = jnp.where((col < ctx_len) | band, 0.0, NEG_INF)
    bias = jnp.concatenate([bias] * ATT_STACK, axis=0)
    grp = lax.broadcasted_iota(jnp.int32, (ATT_STACK * ATT_BLOCK, 1), 0) // ATT_BLOCK
    scale = HD ** -0.5
    for hq0 in range(0, HQ, ATT_STACK):
        hk = hq0 // GQ
        kh = _bf(k_all[:, hk * HD:(hk + 1) * HD])
        vh = _bf(v_all[:, hk * HD:(hk + 1) * HD])
        heads = [hq0 + g for g in range(ATT_STACK)]
        q = jnp.concatenate([q_ref[0, :, hq * HD:(hq + 1) * HD] for hq in heads], axis=0) * scale
        sk = jnp.zeros((ATT_STACK * ATT_BLOCK, 1), F32)
        for g, hq in enumerate(heads):
            sk = jnp.where(grp == g, sink_ref[hq], sk)
        s = _dot_nt(q, kh) + bias
        m = jnp.maximum(jnp.max(s, axis=-1, keepdims=True), sk)
        p = jnp.exp(s - m)
        den = jnp.sum(p, axis=-1, keepdims=True) + jnp.exp(sk - m)
        o = jnp.dot(_bf(p), vh, preferred_element_type=F32) / den
        for g, hq in enumerate(heads):
            o_ref[0, :, hq * HD:(hq + 1) * HD] = o[g * ATT_BLOCK:(g + 1) * ATT_BLOCK]


def _attention(qkv, sink, ctx_len):
    b, s, _ = qkv.shape
    n_lat = s - ctx_len
    nb = n_lat // ATT_BLOCK
    cb = ctx_len // ATT_BLOCK
    kcol = Q_COLS // KV_COLS
    vcol = kcol + 1

    def band(colblk, shift):
        return pl.BlockSpec((1, ATT_BLOCK, KV_COLS),
                            lambda bi, i: (bi, cb + jnp.clip(i + shift, 0, nb - 1), colblk))

    return pl.pallas_call(
        functools.partial(_attn_kernel, ctx_len=ctx_len, n_lat=n_lat),
        grid=(b, nb),
        in_specs=[pl.BlockSpec(memory_space=pltpu.SMEM),
                  pl.BlockSpec((1, ATT_BLOCK, Q_COLS), lambda bi, i: (bi, cb + i, 0)),
                  pl.BlockSpec((1, ctx_len, KV_COLS), lambda bi, i: (bi, 0, kcol)),
                  pl.BlockSpec((1, ctx_len, KV_COLS), lambda bi, i: (bi, 0, vcol)),
                  band(kcol, -1), band(kcol, 0), band(kcol, 1),
                  band(vcol, -1), band(vcol, 0), band(vcol, 1)],
        out_specs=pl.BlockSpec((1, ATT_BLOCK, Q_COLS), lambda bi, i: (bi, i, 0)),
        out_shape=jax.ShapeDtypeStruct((b, n_lat, Q_COLS), F32),
        compiler_params=_cparams(("parallel", "parallel")),
        name="attention",
    )(sink, qkv, qkv, qkv, qkv, qkv, qkv, qkv, qkv, qkv)


MOE_ROWS = 256


def _row_scatter_kernel(dest_ref, pad_ref, f_ref, xs_hbm, zero_ref, sem, *, tm, n_pad):
    base = pl.program_id(0) * tm

    @pl.when(pl.program_id(0) == 0)
    def _():
        zero_ref[...] = jnp.zeros_like(zero_ref)

        def zero_row(r, carry):
            pltpu.make_async_copy(zero_ref, xs_hbm.at[pl.ds(pad_ref[r], 1)], sem).start()
            return carry

        lax.fori_loop(0, n_pad, zero_row, 0, unroll=8)
        for _ in range(n_pad // tm):
            pltpu.make_async_copy(f_ref, xs_hbm.at[pl.ds(0, tm)], sem).wait()

    def issue(r, carry):
        slot = 2 * (base + r)
        pltpu.make_async_copy(f_ref.at[pl.ds(r, 1)], xs_hbm.at[pl.ds(dest_ref[slot], 1)], sem).start()
        pltpu.make_async_copy(f_ref.at[pl.ds(r, 1)], xs_hbm.at[pl.ds(dest_ref[slot + 1], 1)], sem).start()
        return carry

    lax.fori_loop(0, tm, issue, 0, unroll=8)
    for _ in range(2):
        pltpu.make_async_copy(f_ref, xs_hbm.at[pl.ds(0, tm)], sem).wait()


def _row_scatter(f, dest, pad_rows, n_rows):
    n_tok, d = f.shape
    tm = TOK_TILE
    n_pad = pad_rows.shape[0]
    assert n_pad % tm == 0 and 2 * n_tok + n_pad == n_rows
    return pl.pallas_call(
        functools.partial(_row_scatter_kernel, tm=tm, n_pad=n_pad),
        grid_spec=pltpu.PrefetchScalarGridSpec(
            num_scalar_prefetch=2,
            grid=(n_tok // tm,),
            in_specs=[pl.BlockSpec((tm, d), lambda i, dr, pr: (i, 0))],
            out_specs=pl.BlockSpec(memory_space=pl.ANY),
            scratch_shapes=[pltpu.VMEM((1, d), f.dtype), pltpu.SemaphoreType.DMA(())]),
        out_shape=jax.ShapeDtypeStruct((n_rows, d), f.dtype),
        compiler_params=_cparams(("arbitrary",)),
        name="moe_scatter",
    )(dest, pad_rows, f)


def _expert_kernel(be_ref, x_ref, wg_ref, wu_ref, wd_ref, o_ref):
    h = _bf(x_ref[...])
    act = _silu(jnp.dot(h, wg_ref[0], preferred_element_type=F32)) * jnp.dot(h, wu_ref[0],
                                                                              preferred_element_type=F32)
    o_ref[...] = _dot(act, wd_ref[0])


def _experts(xs, block_e, wg, wu, wd):
    n_rows, d = xs.shape
    ff = wg.shape[2]
    rows = MOE_ROWS
    return pl.pallas_call(
        _expert_kernel,
        grid_spec=pltpu.PrefetchScalarGridSpec(
            num_scalar_prefetch=1,
            grid=(n_rows // rows,),
            in_specs=[pl.BlockSpec((rows, d), lambda i, be: (i, 0)),
                      pl.BlockSpec((1, d, ff), lambda i, be: (be[i], 0, 0)),
                      pl.BlockSpec((1, d, ff), lambda i, be: (be[i], 0, 0)),
                      pl.BlockSpec((1, ff, d), lambda i, be: (be[i], 0, 0))],
            out_specs=pl.BlockSpec((rows, d), lambda i, be: (i, 0))),
        out_shape=jax.ShapeDtypeStruct((n_rows, d), F32),
        compiler_params=_cparams(("arbitrary",)),
        name="moe_experts",
    )(block_e, xs, wg, wu, wd)


def _combine_kernel(dest_ref, x_ref, modl_ref, modc_ref, w_ref, gfin_ref, ys_hbm, o_ref, y1_ref, y2_ref, sem, *,
                    tm, tiles_per_row):
    tile = pl.program_id(0) * tiles_per_row + pl.program_id(1)
    n_tiles = pl.num_programs(0) * tiles_per_row
    buf = tile % 2

    def fetch(t, b):
        def issue(r, carry):
            slot = 2 * (t * tm + r)
            pltpu.make_async_copy(ys_hbm.at[pl.ds(dest_ref[slot], 1)], y1_ref.at[b, pl.ds(r, 1)], sem.at[b]).start()
            pltpu.make_async_copy(ys_hbm.at[pl.ds(dest_ref[slot + 1], 1)], y2_ref.at[b, pl.ds(r, 1)],
                                  sem.at[b]).start()
            return carry

        lax.fori_loop(0, tm, issue, 0, unroll=8)

    @pl.when(tile == 0)
    def _():
        fetch(0, 0)

    @pl.when(tile + 1 < n_tiles)
    def _():
        fetch(tile + 1, 1 - buf)

    pltpu.make_async_copy(ys_hbm.at[pl.ds(0, tm)], y1_ref.at[buf], sem.at[buf]).wait()
    pltpu.make_async_copy(ys_hbm.at[pl.ds(0, tm)], y2_ref.at[buf], sem.at[buf]).wait()
    w = w_ref[0]
    moe = w[:, 0:1] * y1_ref[buf] + w[:, 1:2] * y2_ref[buf]
    y = x_ref[0] + _gate_vec(modl_ref, modc_ref, 5, 0, tm, 0) * moe
    o_ref[0] = y * lax.rsqrt(jnp.mean(y * y, axis=-1, keepdims=True) + NORM_EPS) * gfin_ref[...]


def _combine(x, mods, tokw, dest, ys, final_g):
    b, s, d = x.shape
    tm = TOK_TILE
    tiles = s // tm
    return pl.pallas_call(
        functools.partial(_combine_kernel, tm=tm, tiles_per_row=tiles),
        grid_spec=pltpu.PrefetchScalarGridSpec(
            num_scalar_prefetch=1,
            grid=(b, tiles),
            in_specs=[pl.BlockSpec((1, tm, d), lambda i, j, dr: (i, j, 0)),
                      pl.BlockSpec((1, N_MOD, d), lambda i, j, dr: (i, 0, 0)),
                      pl.BlockSpec((1, N_MOD, d), lambda i, j, dr: (0, 0, 0)),
                      pl.BlockSpec((1, tm, LANES), lambda i, j, dr: (i, j, 0)),
                      pl.BlockSpec((1, d), lambda i, j, dr: (0, 0)),
                      pl.BlockSpec(memory_space=pl.ANY)],
            out_specs=pl.BlockSpec((1, tm, d), lambda i, j, dr: (i, j, 0)),
            scratch_shapes=[pltpu.VMEM((2, tm, d), F32), pltpu.VMEM((2, tm, d), F32),
                            pltpu.SemaphoreType.DMA((2,))]),
        out_shape=jax.ShapeDtypeStruct((b, s, d), F32),
        compiler_params=_cparams(("arbitrary", "arbitrary")),
        name="moe_combine",
    )(dest, x, mods, mods, tokw, final_g.reshape(1, d), ys)


def _moe_routing(eidx):
    m = eidx.shape[0]
    oh = (eidx[:, None] == jnp.arange(N_EXPERTS, dtype=jnp.int32)[None, :]).astype(jnp.int32)
    csum = jnp.cumsum(oh, axis=0)
    rank = jnp.sum((csum - oh) * oh, axis=1)
    counts = csum[-1]
    padded = (counts + MOE_ROWS - 1) // MOE_ROWS * MOE_ROWS
    pad_end = jnp.cumsum(padded)
    dest = jnp.sum(oh * (pad_end - padded)[None, :], axis=1) + rank
    n_blocks = m // MOE_ROWS + N_EXPERTS
    starts = jnp.arange(n_blocks, dtype=jnp.int32) * MOE_ROWS
    block_e = jnp.minimum(jnp.sum((starts[:, None] >= pad_end[None, :]).astype(jnp.int32), axis=1), N_EXPERTS - 1)
    pad_cnt = padded - counts
    pad_cum = jnp.cumsum(pad_cnt)
    idx = jnp.arange(N_EXPERTS * MOE_ROWS, dtype=jnp.int32)
    owner = jnp.sum((idx[:, None] >= pad_cum[None, :]).astype(jnp.int32), axis=1)
    own = (jnp.minimum(owner, N_EXPERTS - 1)[:, None] == jnp.arange(N_EXPERTS)[None, :]).astype(jnp.int32)
    in_expert = jnp.sum(own * (pad_end - pad_cnt)[None, :], axis=1) + idx - jnp.sum(
        own * (pad_cum - pad_cnt)[None, :], axis=1)
    pad_rows = jnp.where(owner < N_EXPERTS, in_expert, pad_end[-1] + idx - pad_cum[-1])
    return dest.astype(jnp.int32), block_e, pad_rows.astype(jnp.int32), n_blocks * MOE_ROWS


def _rope_tables(n_lat, ctx_len):
    rows = n_lat // GRID_W
    row = jnp.repeat(jnp.arange(rows, dtype=F32), GRID_W)
    col = jnp.tile(jnp.arange(GRID_W, dtype=F32), rows)
    inv = ROPE_BASE ** (-jnp.arange(0, AX_DIM, 2, dtype=F32) / AX_DIM)
    ar, ac = row[:, None] * inv, col[:, None] * inv
    cos = jnp.concatenate([jnp.cos(ar), jnp.cos(ar), jnp.cos(ac), jnp.cos(ac)], axis=-1)
    sin = jnp.concatenate([-jnp.sin(ar), jnp.sin(ar), -jnp.sin(ac), jnp.sin(ac)], axis=-1)
    n_heads = ROPE_COLS // HD
    cos = jnp.concatenate([jnp.ones((ctx_len, HD), F32), cos], axis=0)
    sin = jnp.concatenate([jnp.zeros((ctx_len, HD), F32), sin], axis=0)
    half = AX_DIM // 2
    j = jnp.arange(HD)
    partner = jnp.where((j % AX_DIM) < half, j + half, j - half)
    perm = (jnp.arange(n_heads)[:, None] * HD + partner[None, :]).reshape(-1)
    return jnp.tile(cos, (1, n_heads)), jnp.tile(sin, (1, n_heads)), perm


def kernel(x, c, ctx, c_ctx, mod_w, mod_b, norm_mix, norm_ffn, norm_final, rec_w_in, rec_w_out, rwkv_mu, rwkv_w0, rwkv_w_up, rwkv_a0, rwkv_a_up, rwkv_g_up, rwkv_k_k, rwkv_k_a, rwkv_r_k, rwkv_ln_w, rwkv_ln_b, hgrn_lb, hgrn_norm, ffn_w_gate, ffn_w_up, ffn_w_down, att_w_in, att_w_out, att_sink, moe_router, moe_router_b, moe_w_gate, moe_w_up, moe_w_down):
    bsz, n_lat, d = x.shape
    ctx_len = ctx.shape[1]

    n_rows = -(-(bsz + 1) // 8) * 8
    cvec = jnp.zeros((n_rows, d), F32).at[:bsz].set(c).at[bsz].set(c_ctx)
    mods = [_adaln(cvec, mod_w[l], mod_b[l]).reshape(n_rows, N_MOD, d) for l in range(2)]

    ops, vv, g_bonus, gam, p_hgrn = _rec_in(ctx, x, mods[0], bsz, norm_mix[0], _bf(rec_w_in[0]), rwkv_mu[0],
                                            rwkv_w0[0], rwkv_w_up[0], rwkv_a0[0], rwkv_a_up[0], rwkv_g_up[0],
                                            rwkv_k_k[0], rwkv_k_a[0], rwkv_r_k[0].reshape(-1))
    oa = _rwkv_scan(ops, vv, gam, ctx_len)
    lb = jnp.cumsum(jax.nn.softmax(hgrn_lb.astype(F32), axis=1), axis=1)[:, 0].reshape(2, 1, B_W)
    ob = _hgrn_scan(p_hgrn, lb, ctx_len)
    xcat = _rec_out(ctx, x, mods[0], bsz, oa, g_bonus, ob, p_hgrn, rwkv_ln_w[0], rwkv_ln_b[0], hgrn_norm[0],
                    _bf(rec_w_out[0]), norm_ffn[0], _bf(ffn_w_gate[0]), _bf(ffn_w_up[0]), _bf(ffn_w_down[0]))

    cos, sin, perm = _rope_tables(n_lat, ctx_len)
    w_att = att_w_in[0]
    qkv = _proj_rope(xcat, mods[1], bsz, norm_mix[1], _bf(w_att), _bf(w_att[:, perm]), cos, sin, ctx_len)
    att = _attention(qkv, att_sink[0], ctx_len)
    x_lat, tokw, eidx, f_lat = _att_out(xcat, mods[1], att, _bf(att_w_out[0]), norm_ffn[1], moe_router[0],
                                        moe_router_b[0], x_skip=ctx_len)
    dest, block_e, pad_rows, n_rows = _moe_routing(eidx[..., :2].reshape(-1))
    xs = _row_scatter(f_lat.reshape(bsz * n_lat, d), dest, pad_rows, n_rows)
    ys = _experts(xs, block_e, _bf(moe_w_gate[0]), _bf(moe_w_up[0]), _bf(moe_w_down[0]))
    return _combine(x_lat, mods[1], tokw, dest, ys, norm_final)
```

```python
import functools
import math

import jax
import jax.numpy as jnp
from jax import lax
from jax.experimental import pallas as pl
from jax.experimental.pallas import tpu as pltpu

F32 = jnp.float32
BF16 = jnp.bfloat16

N_MOD = 6
NORM_EPS = 1e-6
NEG_INF = -1e30

A_HEADS = 8
A_HD = 64
A_W = A_HEADS * A_HD
DECAY_LORA = 64
AAA_LORA = 64
GATE_LORA = 128
RWKV_COLS = 3 * A_W + DECAY_LORA + AAA_LORA + GATE_LORA
GN_EPS = 64e-5

B_HEADS = 4
B_DK = 128
B_W = B_HEADS * B_DK
HGRN_COLS = 5 * B_W

HQ = 16
HKV = 4
GQ = HQ // HKV
HD = 64
WINDOW = 128
ATT_BLOCK = 128
ATT_STACK = GQ
AX_DIM = HD // 2
ROPE_BASE = 10000.0
GRID_W = 64
Q_COLS = HQ * HD
KV_COLS = HKV * HD
ROPE_COLS = Q_COLS + KV_COLS
ATT_COLS = Q_COLS + 2 * KV_COLS

N_EXPERTS = 8
LANES = 128
CHUNK = 64
SCAN_BLOCK = 256
MXU_WIDTH = 256
RWKV_GROUP = MXU_WIDTH // A_HD
TOK_TILE = 256
PROJ_TILE = 768
VMEM_LIMIT = 56 * 1024 * 1024

OP_KAP, OP_RT, OP_KBAR, OP_BBAR, OP_KGAM, OP_BGAM = range(6)
N_OPS = 6


def _cparams(sem):
    return pltpu.CompilerParams(dimension_semantics=sem, vmem_limit_bytes=VMEM_LIMIT)


def _bf(x):
    return x.astype(BF16)


def _dot(a, b):
    return jnp.dot(_bf(a), _bf(b), preferred_element_type=F32)


def _dot_nt(a, b):
    return lax.dot_general(_bf(a), _bf(b), (((1,), (1,)), ((), ())), preferred_element_type=F32)


def _dot_tn(a, b):
    return lax.dot_general(_bf(a), _bf(b), (((0,), (0,)), ((), ())), preferred_element_type=F32)


def _dot_f32(a, b):
    a_hi, b_hi = _bf(a), _bf(b)
    a_lo, b_lo = _bf(a - a_hi.astype(F32)), _bf(b - b_hi.astype(F32))
    acc = jnp.dot(a_hi, b_hi, preferred_element_type=F32)
    acc += jnp.dot(a_hi, b_lo, preferred_element_type=F32)
    acc += jnp.dot(a_lo, b_hi, preferred_element_type=F32)
    return acc


def _sigmoid(x):
    return 0.5 * jnp.tanh(0.5 * x) + 0.5


def _silu(x):
    return x * _sigmoid(x)


def _segsum(x, ones_bd):
    hi = _bf(x)
    lo = _bf(x - hi.astype(F32))
    return jnp.dot(hi, ones_bd, preferred_element_type=F32) + jnp.dot(lo, ones_bd, preferred_element_type=F32)


def _dot_split(a, x):
    hi = _bf(x)
    lo = _bf(x - hi.astype(F32))
    return jnp.dot(a, hi, preferred_element_type=F32) + jnp.dot(a, lo, preferred_element_type=F32)


def _block_ones(width, seg):
    i = jnp.arange(width) // seg
    return (i[:, None] == i[None, :]).astype(BF16)


def _modulate(x, g, modl_ref, modc_ref, row, pos0, ctx_len, pos=None):
    y = x * lax.rsqrt(jnp.mean(x * x, axis=-1, keepdims=True) + NORM_EPS) * g
    if pos is None:
        pos = pos0 + lax.broadcasted_iota(jnp.int32, (x.shape[0], 1), 0)
    is_ctx = pos < ctx_len
    shift = jnp.where(is_ctx, modc_ref[0, row:row + 1, :], modl_ref[0, row:row + 1, :])
    scale = jnp.where(is_ctx, modc_ref[0, row + 1:row + 2, :], modl_ref[0, row + 1:row + 2, :])
    return y * (1.0 + scale) + shift


def _gate_vec(modl_ref, modc_ref, row, pos0, n, ctx_len):
    pos = pos0 + lax.broadcasted_iota(jnp.int32, (n, 1), 0)
    return jnp.where(pos < ctx_len, modc_ref[0, row:row + 1, :], modl_ref[0, row:row + 1, :])


def _mod_specs(d, ctx_row):
    return [pl.BlockSpec((1, N_MOD, d), lambda i, *_: (i, 0, 0)),
            pl.BlockSpec((1, N_MOD, d), lambda *_: (ctx_row, 0, 0))]


def _adaln_kernel(c_ref, w_ref, b_ref, o_ref):
    o_ref[...] = _dot(_silu(c_ref[...]), w_ref[...]) + b_ref[...]


def _adaln(cvec, w, b):
    r, d = cvec.shape
    n = w.shape[1]
    tn = 1024
    return pl.pallas_call(
        _adaln_kernel,
        grid=(n // tn,),
        in_specs=[pl.BlockSpec((r, d), lambda j: (0, 0)),
                  pl.BlockSpec((d, tn), lambda j: (0, j)),
                  pl.BlockSpec((1, tn), lambda j: (0, j))],
        out_specs=pl.BlockSpec((r, tn), lambda j: (0, j)),
        out_shape=jax.ShapeDtypeStruct((r, n), F32),
        compiler_params=_cparams(("parallel",)),
        name="adaln",
    )(cvec, w, b.reshape(1, n))


def _proj_rope_kernel(x_ref, modl_ref, modc_ref, g_ref, w_ref, wrot_ref, cos_ref, sin_ref, o_ref, *, ctx_len, tm,
                      rope_cols):
    pos0 = pl.program_id(1) * tm
    h = _bf(_modulate(x_ref[0], g_ref[...], modl_ref, modc_ref, 0, pos0, ctx_len))
    y = jnp.dot(h, w_ref[...], preferred_element_type=F32)
    yr = jnp.dot(h, wrot_ref[...], preferred_element_type=F32)
    o_ref[0, :, :rope_cols] = y[:, :rope_cols] * cos_ref[...] + yr * sin_ref[...]
    o_ref[0, :, rope_cols:] = y[:, rope_cols:]


def _proj_rope(xcat, mods, ctx_row, g, w, wrot, cos, sin, ctx_len):
    b, s, d = xcat.shape
    n = w.shape[1]
    tm = PROJ_TILE
    rope_cols = wrot.shape[1]
    return pl.pallas_call(
        functools.partial(_proj_rope_kernel, ctx_len=ctx_len, tm=tm, rope_cols=rope_cols),
        grid=(b, s // tm),
        in_specs=[pl.BlockSpec((1, tm, d), lambda i, j: (i, j, 0))] + _mod_specs(d, ctx_row) + [
            pl.BlockSpec((1, d), lambda i, j: (0, 0)),
            pl.BlockSpec((d, n), lambda i, j: (0, 0)),
            pl.BlockSpec((d, rope_cols), lambda i, j: (0, 0)),
            pl.BlockSpec((tm, rope_cols), lambda i, j: (j, 0)),
            pl.BlockSpec((tm, rope_cols), lambda i, j: (j, 0))],
        out_specs=pl.BlockSpec((1, tm, n), lambda i, j: (i, j, 0)),
        out_shape=jax.ShapeDtypeStruct((b, s, n), F32),
        compiler_params=_cparams(("parallel", "parallel")),
        name="proj_rope",
    )(xcat, mods, mods, g.reshape(1, d), w, wrot, cos, sin)


DECAY_SCALE = math.exp(-0.5)


def _stream_specs(tm, d, ctx_len):
    assert ctx_len == tm
    return [pl.BlockSpec((1, tm, d), lambda i, j: (i, 0, 0)),
            pl.BlockSpec((1, tm, d), lambda i, j: (i, jnp.maximum(j - 1, 0), 0))]


def _stream_tile(c_ref, x_ref):
    return jnp.where(pl.program_id(1) == 0, c_ref[0], x_ref[0])


def _rec_in_kernel(c_ref, x_ref, xp_ref, xn_ref, modl_ref, modc_ref, g_ref, w_ref, mu_ref, w0_ref, wup_ref, a0_ref,
                   aup_ref, gup_ref, kk_ref, ka_ref, rk_ref, ones_ref, ops_ref, v_ref, gb_ref, gam_ref, ph_ref, *,
                   ctx_len, seq_len, tm):
    pos0 = pl.program_id(1) * tm
    gvec = g_ref[...]
    h = _bf(_modulate(_stream_tile(c_ref, x_ref), gvec, modl_ref, modc_ref, 0, pos0, ctx_len))
    p = jnp.dot(h, w_ref[:, :RWKV_COLS], preferred_element_type=F32)

    hgrn_pieces = [(RWKV_COLS + c, min(B_W, HGRN_COLS - c)) for c in range(0, HGRN_COLS, B_W)]

    def hgrn_piece():
        if hgrn_pieces:
            lo, width = hgrn_pieces.pop(0)
            ph_ref[0, :, lo - RWKV_COLS:lo - RWKV_COLS + width] = jnp.dot(h, w_ref[:, lo:lo + width],
                                                                          preferred_element_type=F32)

    halo = lax.broadcasted_iota(jnp.int32, (16, 1), 0)
    halo_pos = jnp.where(halo < 8, pos0 - 8 + halo, pos0 + tm - 8 + halo)
    xh = jnp.concatenate([xp_ref[0], xn_ref[0]], axis=0)
    p_halo = jnp.dot(_bf(_modulate(xh, gvec, modl_ref, modc_ref, 0, 0, ctx_len, pos=halo_pos)),
                     w_ref[:, :RWKV_COLS], preferred_element_type=F32)
    rows = lax.broadcasted_iota(jnp.int32, (tm, 1), 0)
    starts_seq = (pos0 == 0) | (pos0 == ctx_len)
    ends_seq = (pos0 + tm == ctx_len) | (pos0 + tm == seq_len)
    prev_halo = jnp.where(starts_seq, 0.0, p_halo[7:8, :])
    next_halo = jnp.where(ends_seq, 0.0, p_halo[8:9, :])
    prev = jnp.where(rows == 0, prev_halo, pltpu.roll(p, 1, 0))
    nxt = jnp.where(rows == tm - 1, next_halo, pltpu.roll(p, tm - 1, 0))
    p = p + mu_ref[...] * (0.5 * (prev + nxt) - p)
    hgrn_piece()

    r = p[:, 0:A_W]
    k = p[:, A_W:2 * A_W]
    v = p[:, 2 * A_W:3 * A_W]
    lo = 3 * A_W
    wd = p[:, lo:lo + DECAY_LORA]
    ad = p[:, lo + DECAY_LORA:lo + DECAY_LORA + AAA_LORA]
    gd = p[:, lo + DECAY_LORA + AAA_LORA:lo + DECAY_LORA + AAA_LORA + GATE_LORA]

    tw = jnp.tanh(wd)
    a = _sigmoid(a0_ref[...] + _dot(ad, aup_ref[...]))
    ones_bd = ones_ref[...]
    kk = k * kk_ref[...]
    kk = kk / jnp.maximum(jnp.sqrt(_segsum(kk * kk, ones_bd)), 1e-12)
    k = k * (1.0 + (a - 1.0) * ka_ref[...])
    b = kk * a
    v_ref[0] = _bf(v)
    gb_ref[0, :, :A_W] = _dot(_sigmoid(gd), gup_ref[...])
    gb_ref[0, :, A_W:] = _segsum(r * k * rk_ref[...], ones_bd) * v
    hgrn_piece()

    trow = lax.broadcasted_iota(jnp.int32, (tm, tm), 0)
    tcol = lax.broadcasted_iota(jnp.int32, (tm, tm), 1)
    same = (trow // CHUNK) == (tcol // CHUNK)
    n_chunks = tm // CHUNK
    for d in range(2):
        lw = -DECAY_SCALE * _sigmoid(w0_ref[d:d + 1, :] + _dot(tw, wup_ref[d]))
        before = (tcol <= trow) if d == 0 else (tcol >= trow)
        g_incl = _dot_split(jnp.where(same & before, 1.0, 0.0).astype(BF16), lw)
        last = [c * CHUNK + (CHUNK - 1 if d == 0 else 0) for c in range(n_chunks)]
        gam = jnp.exp(jnp.concatenate([g_incl[t:t + 1] for t in last], axis=0))
        gam_ref[0, d, 0] = gam
        hgrn_piece()
        e_ng = jnp.exp(-g_incl)
        e_tail = e_ng * jnp.concatenate([jnp.broadcast_to(gam[c:c + 1], (CHUNK, A_W)) for c in range(n_chunks)],
                                        axis=0)
        operands = {OP_KAP: kk * jnp.exp(g_incl - lw),
                    OP_RT: r * jnp.exp(g_incl),
                    OP_KBAR: k * e_ng, OP_BBAR: b * e_ng,
                    OP_KGAM: k * e_tail, OP_BGAM: b * e_tail}
        for sec, val in operands.items():
            ops_ref[0, d, :, sec * A_W:(sec + 1) * A_W] = _bf(val)
        hgrn_piece()
    while hgrn_pieces:
        hgrn_piece()


def _rec_in(ctx, x, mods, ctx_row, g, w, mu, w0, w_up, a0, a_up, g_up, k_k, k_a, r_k):
    b, n_lat, d = x.shape
    ctx_len = ctx.shape[1]
    s = ctx_len + n_lat
    tm = SCAN_BLOCK
    assert n_lat % tm == 0
    nb8 = n_lat // 8
    row = lambda a: a.reshape(1, -1)
    full = lambda a: pl.BlockSpec(a.shape, lambda i, j: (0,) * a.ndim)
    args = [row(g), w, row(mu), w0, w_up, row(a0), a_up, g_up, row(k_k), row(k_a), row(r_k),
            _block_ones(A_W, A_HD)]
    return pl.pallas_call(
        functools.partial(_rec_in_kernel, ctx_len=ctx_len, seq_len=s, tm=tm),
        grid=(b, s // tm),
        in_specs=_stream_specs(tm, d, ctx_len) + [
            pl.BlockSpec((1, 8, d), lambda i, j: (i, jnp.maximum((j - 1) * (tm // 8) - 1, 0), 0)),
            pl.BlockSpec((1, 8, d), lambda i, j: (i, jnp.minimum(j * (tm // 8), nb8 - 1), 0)),
        ] + _mod_specs(d, ctx_row) + [full(a) for a in args],
        out_specs=[pl.BlockSpec((1, 2, tm, N_OPS * A_W), lambda i, j: (i, 0, j, 0)),
                   pl.BlockSpec((1, tm, A_W), lambda i, j: (i, j, 0)),
                   pl.BlockSpec((1, tm, 2 * A_W), lambda i, j: (i, j, 0)),
                   pl.BlockSpec((1, 2, 1, tm // CHUNK, A_W), lambda i, j: (i, 0, j, 0, 0)),
                   pl.BlockSpec((1, tm, HGRN_COLS), lambda i, j: (i, j, 0))],
        out_shape=[jax.ShapeDtypeStruct((b, 2, s, N_OPS * A_W), BF16),
                   jax.ShapeDtypeStruct((b, s, A_W), BF16),
                   jax.ShapeDtypeStruct((b, s, 2 * A_W), F32),
                   jax.ShapeDtypeStruct((b, 2, s // tm, tm // CHUNK, A_W), F32),
                   jax.ShapeDtypeStruct((b, s, HGRN_COLS), F32)],
        compiler_params=_cparams(("parallel", "parallel")),
        name="rec_in",
    )(ctx, x, x, x, mods, mods, *args)


def _chunk_order(d, j, n_ctx_chunks, n_chunks):
    back = jnp.where(j < n_ctx_chunks, n_ctx_chunks - 1 - j, n_chunks - 1 + n_ctx_chunks - j)
    return jnp.where(d == 0, j, back)


def _incl_mask(rev):
    row = lax.broadcasted_iota(jnp.int32, (CHUNK, CHUNK), 0)
    col = lax.broadcasted_iota(jnp.int32, (CHUNK, CHUNK), 1)
    return jnp.where(rev, row - col, col - row) <= 0


def _scan_step_coords(t, n_steps):
    return jnp.minimum(t, n_steps - 1), jnp.maximum(t - 1, 0)


def _rwkv_chunk_kernel(ops_ref, v_ref, gam_ref, o_ref, s_ref, tr_ref, uu_ref, ol_ref, ab_ref,
                       m2_ref, cc_ref, gm_ref, *, n_blocks, n_steps):
    step = pl.program_id(0)
    t_in, t_out = _scan_step_coords(step, n_steps)
    rev = (t_in // n_blocks) % 2 == 1
    rev_out = (t_out // n_blocks) % 2 == 1
    first_out = t_out % n_blocks == 0
    w_slot = step % 2
    r_slot = 1 - w_slot

    @pl.when(step == 0)
    def _():
        s_ref[...] = jnp.zeros_like(s_ref)
        for ref in (tr_ref, uu_ref, ol_ref, ab_ref, m2_ref, cc_ref, gm_ref):
            ref[1] = jnp.zeros(ref.shape[1:], ref.dtype)

    gw = RWKV_GROUP * A_HD
    groups = range(A_HEADS // RWKV_GROUP)
    gsl = [slice(g * gw, (g + 1) * gw) for g in groups]
    row = lax.broadcasted_iota(jnp.int32, (CHUNK, gw), 0)
    col = lax.broadcasted_iota(jnp.int32, (CHUNK, gw), 1) % A_HD
    diff = jnp.where(rev, row - col, col - row)
    strict_c = diff < 0
    incl_c = diff <= 0
    eye_c = jnp.where(diff == 0, 1.0, 0.0)
    bd_mask = (lax.broadcasted_iota(jnp.int32, (gw, gw), 0) // A_HD
               == lax.broadcasted_iota(jnp.int32, (gw, gw), 1) // A_HD)

    def bd(x):
        return jnp.where(bd_mask, jnp.concatenate([x] * RWKV_GROUP, axis=0), jnp.zeros((), x.dtype))

    def stack(x):
        return jnp.concatenate([x[:, h * A_HD:(h + 1) * A_HD] for h in range(RWKV_GROUP)], axis=0)

    n = SCAN_BLOCK // CHUNK
    rows = [pl.ds(pl.multiple_of(jnp.where(rev, (n - 1 - i) * CHUNK, i * CHUNK), CHUNK), CHUNK) for i in range(n)]
    rows_out = [pl.ds(pl.multiple_of(jnp.where(rev_out, (n - 1 - i) * CHUNK, i * CHUNK), CHUNK), CHUNK)
                for i in range(n)]
    units = [(i, g) for i in range(n) for g in groups]

    s = [jnp.where(first_out, 0.0, s_ref[g]) for g in groups]
    zr = {}
    pieces = []

    def advance(i):
        for g in groups:
            ui = i * len(groups) + g
            zr[i, g] = _dot_nt(tr_ref[r_slot, ui], bd(s[g]))
            s[g] = s[g] * gm_ref[r_slot, ui] - jnp.dot(_bf(s[g]), m2_ref[r_slot, ui],
                                                       preferred_element_type=F32) + cc_ref[r_slot, ui]

    def emit_out(i):
        for g in groups:
            ui = i * len(groups) + g
            z = zr[i, g][:CHUNK] + uu_ref[r_slot, ui]
            o_ref[0, 0, rows_out[i], gsl[g]] = (zr[i, g][CHUNK:] + ol_ref[r_slot, ui]
                                                - _dot(ab_ref[r_slot, ui], bd(z)))

    for i in range(n):
        pieces += [functools.partial(advance, i), functools.partial(emit_out, i)]

    def state_piece():
        if pieces:
            pieces.pop(0)()

    def operand(sec, u):
        i, g = u
        return ops_ref[0, 0, rows[i], sec * A_W + g * gw:sec * A_W + (g + 1) * gw]

    kap = {u: operand(OP_KAP, u) for u in units}
    rt = {u: operand(OP_RT, u) for u in units}
    kbar = {u: operand(OP_KBAR, u) for u in units}
    bbar = {u: operand(OP_BBAR, u) for u in units}
    kgam = {u: operand(OP_KGAM, u) for u in units}
    bgam = {u: operand(OP_BGAM, u) for u in units}
    v = {(i, g): v_ref[0, rows[i], gsl[g]] for i, g in units}
    chunk_of = [jnp.where(rev, n - 1 - i, i) for i in range(n)]
    gam = {(i, g): gam_ref[0, 0, 0, pl.ds(chunk_of[i], 1), gsl[g]] for i, g in units}

    x = {u: jnp.concatenate([kap[u], rt[u]], axis=0) for u in units}
    yb = {u: _dot_nt(x[u], bd(bbar[u])) for u in units}
    state_piece()
    a = {u: jnp.where(strict_c, -yb[u][:CHUNK], 0.0) for u in units}
    xs = {u: eye_c + a[u] for u in units}
    pw = {u: _dot(a[u], bd(a[u])) for u in units}
    state_piece()
    for _ in range(4):
        st = {u: _dot(jnp.concatenate([pw[u], xs[u]], axis=0), bd(pw[u])) for u in units}
        pw = {u: st[u][:CHUNK] for u in units}
        xs = {u: xs[u] + st[u][CHUNK:] for u in units}
        state_piece()
    t_inv = {u: xs[u] + _dot(xs[u], bd(pw[u])) for u in units}
    state_piece()
    yk = {u: _dot_nt(x[u], bd(kbar[u])) for u in units}
    ykm = {u: jnp.concatenate([jnp.where(strict_c, yk[u][:CHUNK], 0.0), jnp.where(incl_c, yk[u][CHUNK:], 0.0)],
                              axis=0) for u in units}
    wo = {u: _dot(ykm[u], bd(v[u])) for u in units}
    w1 = {u: wo[u][:CHUNK] for u in units}
    o_loc = {u: wo[u][CHUNK:] for u in units}
    state_piece()
    tk = {u: _dot(t_inv[u], bd(kap[u])) for u in units}
    uu = {u: _dot(t_inv[u], bd(w1[u])) for u in units}
    while pieces:
        state_piece()
    for g in groups:
        s_ref[g] = s[g]
    a_rb = {u: jnp.where(incl_c, yb[u][CHUNK:], 0.0) for u in units}
    tu = {u: _dot_tn(jnp.concatenate([stack(tk[u]), stack(uu[u])], axis=1), bd(bgam[u])) for u in units}
    bd_m2 = {u: _bf(bd(tu[u][:A_HD])) for u in units}
    cc = {u: _dot_tn(stack(v[u]), bd(kgam[u])) - tu[u][A_HD:] for u in units}
    for u in units:
        ui = u[0] * len(groups) + u[1]
        tr_ref[w_slot, ui] = jnp.concatenate([_bf(tk[u]), rt[u]], axis=0)
        uu_ref[w_slot, ui] = uu[u]
        ol_ref[w_slot, ui] = o_loc[u]
        ab_ref[w_slot, ui] = _bf(a_rb[u])
        m2_ref[w_slot, ui] = bd_m2[u]
        cc_ref[w_slot, ui] = cc[u]
        gm_ref[w_slot, ui] = gam[u]


def _rwkv_scan(ops, vv, gam, ctx_len):
    b, _, s, _ = ops.shape
    nc, ncc = s // SCAN_BLOCK, ctx_len // SCAN_BLOCK

    n_steps = b * 2 * nc

    def coords(flat):
        d = (flat // nc) % 2
        return flat // (2 * nc), d, _chunk_order(d, flat % nc, ncc, nc)

    def coords_in(t):
        return coords(_scan_step_coords(t, n_steps)[0])

    def out_map(t):
        bi, d, blk = coords(_scan_step_coords(t, n_steps)[1])
        return bi, d, blk, 0

    def ops_map(t):
        bi, d, blk = coords_in(t)
        return bi, d, blk, 0

    def v_map(t):
        bi, _, blk = coords_in(t)
        return bi, blk, 0

    def gam_map(t):
        bi, d, blk = coords_in(t)
        return bi, d, blk, 0, 0

    n_groups = A_HEADS // RWKV_GROUP
    n_chunks = SCAN_BLOCK // CHUNK
    n_units = n_chunks * n_groups
    gw = RWKV_GROUP * A_HD
    return pl.pallas_call(
        functools.partial(_rwkv_chunk_kernel, n_blocks=nc, n_steps=n_steps),
        grid=(n_steps + 1,),
        in_specs=[pl.BlockSpec((1, 1, SCAN_BLOCK, N_OPS * A_W), ops_map),
                  pl.BlockSpec((1, SCAN_BLOCK, A_W), v_map),
                  pl.BlockSpec((1, 1, 1, n_chunks, A_W), gam_map)],
        out_specs=pl.BlockSpec((1, 1, SCAN_BLOCK, A_W), out_map),
        out_shape=jax.ShapeDtypeStruct((b, 2, s, A_W), F32),
        scratch_shapes=[pltpu.VMEM((n_groups, A_HD, gw), F32),
                        pltpu.VMEM((2, n_units, 2 * CHUNK, gw), BF16),
                        pltpu.VMEM((2, n_units, CHUNK, gw), F32),
                        pltpu.VMEM((2, n_units, CHUNK, gw), F32),
                        pltpu.VMEM((2, n_units, CHUNK, gw), BF16),
                        pltpu.VMEM((2, n_units, gw, gw), BF16),
                        pltpu.VMEM((2, n_units, CHUNK, gw), F32),
                        pltpu.VMEM((2, n_units, 1, gw), F32)],
        compiler_params=_cparams(("arbitrary",)),
        name="rwkv_scan",
    )(ops, vv, gam)


def _hgrn_chunk_kernel(q_ref, i_ref, f_ref, lb_ref, o_ref, s_ref):
    rev = pl.program_id(1) == 1

    @pl.when(pl.program_id(2) == 0)
    def _():
        s_ref[...] = jnp.zeros_like(s_ref)

    incl = _incl_mask(rev)
    incl_bf = jnp.where(incl, 1.0, 0.0).astype(BF16)
    lb = lb_ref[0]
    heads = range(B_HEADS)
    sls = [slice(h * B_DK, (h + 1) * B_DK) for h in heads]

    n = SCAN_BLOCK // CHUNK
    rows = [pl.ds(pl.multiple_of(jnp.where(rev, (n - 1 - i) * CHUNK, i * CHUNK), CHUNK), CHUNK) for i in range(n)]
    units = [(i, h) for i in range(n) for h in heads]
    q_in, k_in, q_st, k_tail, gam, v = ({} for _ in range(6))
    for i in range(n):
        f = lb + (1.0 - lb) * _sigmoid(f_ref[0, rows[i], :])
        logf = jnp.log(f)
        kf = 1.0 - f
        g_incl = _dot_split(incl_bf, logf)
        g_tot = jnp.sum(logf, axis=0, keepdims=True)
        g_mid = g_incl[CHUNK // 2 - 1:CHUNK // 2, :]
        q = _silu(q_ref[0, rows[i], :])
        v_i = i_ref[0, rows[i], :]
        q_in_i = q * jnp.exp(g_incl - g_mid)
        k_in_i = kf * jnp.exp(g_mid - g_incl)
        q_st_i = q_in_i * jnp.exp(g_mid)
        k_tail_i = k_in_i * jnp.exp(g_tot - g_mid)
        for h in heads:
            q_in[i, h], k_in[i, h] = q_in_i[:, sls[h]], k_in_i[:, sls[h]]
            q_st[i, h], k_tail[i, h] = q_st_i[:, sls[h]], k_tail_i[:, sls[h]]
            gam[i, h] = jnp.exp(g_tot)[:, sls[h]]
            v[i, h] = v_i[:, sls[h]]
    att = {u: jnp.where(incl, _dot_nt(q_in[u], k_in[u]), 0.0) for u in units}
    o_loc = {u: _dot(att[u], v[u]) for u in units}
    kv = {u: _dot_tn(v[u], k_tail[u]) for u in units}

    s = [s_ref[h] for h in heads]
    for i in range(n):
        for h in heads:
            o_ref[0, 0, rows[i], sls[h]] = o_loc[i, h] + _dot_nt(q_st[i, h], s[h])
            s[h] = s[h] * gam[i, h] + kv[i, h]
    for h in heads:
        s_ref[h] = s[h]


def _hgrn_scan(p, lb, ctx_len):
    b, s, _ = p.shape
    nc, ncc = s // SCAN_BLOCK, ctx_len // SCAN_BLOCK

    def sec(idx):
        return pl.BlockSpec((1, SCAN_BLOCK, B_W), lambda i, d, j: (i, _chunk_order(d, j, ncc, nc), idx))

    return pl.pallas_call(
        _hgrn_chunk_kernel,
        grid=(b, 2, nc),
        in_specs=[sec(0), sec(1),
                  pl.BlockSpec((1, SCAN_BLOCK, B_W), lambda i, d, j: (i, _chunk_order(d, j, ncc, nc), 2 + d)),
                  pl.BlockSpec((1, 1, B_W), lambda i, d, j: (d, 0, 0))],
        out_specs=pl.BlockSpec((1, 1, SCAN_BLOCK, B_W), lambda i, d, j: (i, d, _chunk_order(d, j, ncc, nc), 0)),
        out_shape=jax.ShapeDtypeStruct((b, 2, s, B_W), F32),
        scratch_shapes=[pltpu.VMEM((B_HEADS, B_DK, B_DK), F32)],
        compiler_params=_cparams(("parallel", "parallel", "arbitrary")),
        name="hgrn_scan",
    )(p, p, p, lb)


def _rec_out_kernel(c_ref, x_ref, modl_ref, modc_ref, oa_ref, g_ref, bonus_ref, ob_ref, gate_ref, lnw_ref, lnb_ref,
                    hgn_ref, ones_a_ref, ones_b_ref, w_ref, gffn_ref, wg_ref, wu_ref, wd_ref, o_ref, *, ctx_len, tm):
    pos0 = pl.program_id(1) * tm
    oa = oa_ref[0, 0] + oa_ref[0, 1]
    ones_a = ones_a_ref[...]
    mean = _segsum(oa, ones_a) * (1.0 / A_HD)
    cen = oa - mean
    var = _segsum(cen * cen, ones_a) * (1.0 / A_HD)
    ya = (cen * lax.rsqrt(var + GN_EPS) * lnw_ref[...] + lnb_ref[...] + bonus_ref[0]) * g_ref[0]
    ob = ob_ref[0, 0] + ob_ref[0, 1]
    ms = _segsum(ob * ob, ones_b_ref[...]) * (1.0 / B_DK)
    yb = ob * lax.rsqrt(ms + NORM_EPS) * hgn_ref[...] * _silu(gate_ref[0])
    y = _dot(ya, w_ref[:A_W, :]) + _dot(yb, w_ref[A_W:, :])
    gate = _gate_vec(modl_ref, modc_ref, 2, pos0, tm, ctx_len)
    x1 = _stream_tile(c_ref, x_ref) + gate * y
    h = _bf(_modulate(x1, gffn_ref[...], modl_ref, modc_ref, 3, pos0, ctx_len))
    act = _silu(jnp.dot(h, wg_ref[...], preferred_element_type=F32)) * jnp.dot(h, wu_ref[...],
                                                                               preferred_element_type=F32)
    o_ref[0] = x1 + _gate_vec(modl_ref, modc_ref, 5, pos0, tm, ctx_len) * _dot(act, wd_ref[...])


def _rec_out(ctx, x, mods, ctx_row, oa, g_bonus, ob, p_hgrn, ln_w, ln_b, hg_norm, w_out, g_ffn, wg, wu, wd):
    b, n_lat, d = x.shape
    ctx_len = ctx.shape[1]
    s = ctx_len + n_lat
    tm = TOK_TILE
    row = lambda a: a.reshape(1, -1)
    full = lambda a: pl.BlockSpec(a.shape, lambda i, j: (0,) * a.ndim, pipeline_mode=pl.Buffered(1))
    consts = [row(ln_w), row(ln_b), row(jnp.tile(hg_norm, B_HEADS)), _block_ones(A_W, A_HD),
              _block_ones(B_W, B_DK), w_out, row(g_ffn), wg, wu, wd]
    return pl.pallas_call(
        functools.partial(_rec_out_kernel, ctx_len=ctx_len, tm=tm),
        grid=(b, s // tm),
        in_specs=_stream_specs(tm, d, ctx_len) + _mod_specs(d, ctx_row) + [
            pl.BlockSpec((1, 2, tm, A_W), lambda i, j: (i, 0, j, 0)),
            pl.BlockSpec((1, tm, A_W), lambda i, j: (i, j, 0)),
            pl.BlockSpec((1, tm, A_W), lambda i, j: (i, j, 1)),
            pl.BlockSpec((1, 2, tm, B_W), lambda i, j: (i, 0, j, 0)),
            pl.BlockSpec((1, tm, B_W), lambda i, j: (i, j, 4)),
        ] + [full(a) for a in consts],
        out_specs=pl.BlockSpec((1, tm, d), lambda i, j: (i, j, 0)),
        out_shape=jax.ShapeDtypeStruct((b, s, d), F32),
        compiler_params=_cparams(("parallel", "parallel")),
        name="rec_out",
    )(ctx, x, mods, mods, oa, g_bonus, g_bonus, ob, p_hgrn, *consts)


def _att_out_kernel(x_ref, modl_ref, modc_ref, y_ref, w_ref, g_ref, rw_ref, rb_ref, o_ref, tw_ref, e_ref, f_ref, *,
                    tm):
    x = x_ref[0] + _gate_vec(modl_ref, modc_ref, 2, 0, tm, 0) * _dot(y_ref[0], w_ref[...])
    o_ref[0] = x
    f = _modulate(x, g_ref[...], modl_ref, modc_ref, 3, 0, 0)
    f_ref[0] = f
    logits = _dot_f32(f, rw_ref[...]) + rb_ref[...]
    lane = lax.broadcasted_iota(jnp.int32, logits.shape, 1).astype(F32)
    logits = jnp.where(lane < N_EXPERTS, logits, NEG_INF)
    ex = jnp.exp(logits - jnp.max(logits, axis=-1, keepdims=True))
    probs = ex / jnp.sum(ex, axis=-1, keepdims=True)
    p1 = jnp.max(probs, axis=-1, keepdims=True)
    i1 = jnp.min(jnp.where(probs == p1, lane, float(LANES)), axis=-1, keepdims=True)
    rest = jnp.where(lane == i1, -1.0, probs)
    p2 = jnp.max(rest, axis=-1, keepdims=True)
    i2 = jnp.min(jnp.where(rest == p2, lane, float(LANES)), axis=-1, keepdims=True)
    tot = p1 + p2
    tw_ref[0] = jnp.where(lane == 0.0, p1 / tot, jnp.where(lane == 1.0, p2 / tot, 0.0))
    e_ref[0] = jnp.where(lane == 0.0, i1, jnp.where(lane == 1.0, i2, 0.0)).astype(jnp.int32)


def _att_out(xcat, mods, y, w, g, router_w, router_b, x_skip):
    b, s, k = y.shape
    d = xcat.shape[-1]
    tm = TOK_TILE
    skip = x_skip // tm
    wpad = jnp.zeros((d, LANES), F32).at[:, :N_EXPERTS].set(router_w)
    bpad = jnp.zeros((1, LANES), F32).at[0, :N_EXPERTS].set(router_b)
    tile_spec = pl.BlockSpec((1, tm, d), lambda i, j: (i, j, 0))
    lane_spec = pl.BlockSpec((1, tm, LANES), lambda i, j: (i, j, 0))
    return pl.pallas_call(
        functools.partial(_att_out_kernel, tm=tm),
        grid=(b, s // tm),
        in_specs=[pl.BlockSpec((1, tm, d), lambda i, j: (i, j + skip, 0))] + _mod_specs(d, 0) + [
            pl.BlockSpec((1, tm, k), lambda i, j: (i, j, 0)),
            pl.BlockSpec((k, d), lambda i, j: (0, 0)),
            pl.BlockSpec((1, d), lambda i, j: (0, 0)),
            pl.BlockSpec((d, LANES), lambda i, j: (0, 0)),
            pl.BlockSpec((1, LANES), lambda i, j: (0, 0))],
        out_specs=[tile_spec, lane_spec, lane_spec, tile_spec],
        out_shape=[jax.ShapeDtypeStruct((b, s, d), F32), jax.ShapeDtypeStruct((b, s, LANES), F32),
                   jax.ShapeDtypeStruct((b, s, LANES), jnp.int32), jax.ShapeDtypeStruct((b, s, d), F32)],
        compiler_params=_cparams(("parallel", "parallel")),
        name="att_out_router",
    )(xcat, mods, mods, y, w, g.reshape(1, d), wpad, bpad)


def _attn_kernel(sink_ref, q_ref, kc_ref, vc_ref, kp_ref, kq_ref, kn_ref, vp_ref, vq_ref, vn_ref, o_ref, *,
                 ctx_len, n_lat):
    i = pl.program_id(1)
    nk = ctx_len + 3 * ATT_BLOCK
    k_all = jnp.concatenate([kc_ref[0], kp_ref[0], kq_ref[0], kn_ref[0]], axis=0)
    v_all = jnp.concatenate([vc_ref[0], vp_ref[0], vq_ref[0], vn_ref[0]], axis=0)
    row = lax.broadcasted_iota(jnp.int32, (ATT_BLOCK, nk), 0)
    col = lax.broadcasted_iota(jnp.int32, (ATT_BLOCK, nk), 1)
    rel = col - (ctx_len + ATT_BLOCK)
    kabs = i * ATT_BLOCK + rel
    band = (jnp.abs(row - rel) <= WINDOW) & (kabs >= 0) & (kabs < n_lat)
    bias = jnp.where((col < ctx_len) | band, 0.0, NEG_INF)
    bias = jnp.concatenate([bias] * ATT_STACK, axis=0)
    grp = lax.broadcasted_iota(jnp.int32, (ATT_STACK * ATT_BLOCK, 1), 0) // ATT_BLOCK
    scale = HD ** -0.5
    for hq0 in range(0, HQ, ATT_STACK):
        hk = hq0 // GQ
        kh = _bf(k_all[:, hk * HD:(hk + 1) * HD])
        vh = _bf(v_all[:, hk * HD:(hk + 1) * HD])
        heads = [hq0 + g for g in range(ATT_STACK)]
        q = jnp.concatenate([q_ref[0, :, hq * HD:(hq + 1) * HD] for hq in heads], axis=0) * scale
        sk = jnp.zeros((ATT_STACK * ATT_BLOCK, 1), F32)
        for g, hq in enumerate(heads):
            sk = jnp.where(grp == g, sink_ref[hq], sk)
        s = _dot_nt(q, kh) + bias
        m = jnp.maximum(jnp.max(s, axis=-1, keepdims=True), sk)
        p = jnp.exp(s - m)
        den = jnp.sum(p, axis=-1, keepdims=True) + jnp.exp(sk - m)
        o = jnp.dot(_bf(p), vh, preferred_element_type=F32) / den
        for g, hq in enumerate(heads):
            o_ref[0, :, hq * HD:(hq + 1) * HD] = o[g * ATT_BLOCK:(g + 1) * ATT_BLOCK]


def _attention(qkv, sink, ctx_len):
    b, s, _ = qkv.shape
    n_lat = s - ctx_len
    nb = n_lat // ATT_BLOCK
    cb = ctx_len // ATT_BLOCK
    kcol = Q_COLS // KV_COLS
    vcol = kcol + 1

    def band(colblk, shift):
        return pl.BlockSpec((1, ATT_BLOCK, KV_COLS),
                            lambda bi, i: (bi, cb + jnp.clip(i + shift, 0, nb - 1), colblk))

    return pl.pallas_call(
        functools.partial(_attn_kernel, ctx_len=ctx_len, n_lat=n_lat),
        grid=(b, nb),
        in_specs=[pl.BlockSpec(memory_space=pltpu.SMEM),
                  pl.BlockSpec((1, ATT_BLOCK, Q_COLS), lambda bi, i: (bi, cb + i, 0)),
                  pl.BlockSpec((1, ctx_len, KV_COLS), lambda bi, i: (bi, 0, kcol)),
                  pl.BlockSpec((1, ctx_len, KV_COLS), lambda bi, i: (bi, 0, vcol)),
                  band(kcol, -1), band(kcol, 0), band(kcol, 1),
                  band(vcol, -1), band(vcol, 0), band(vcol, 1)],
        out_specs=pl.BlockSpec((1, ATT_BLOCK, Q_COLS), lambda bi, i: (bi, i, 0)),
        out_shape=jax.ShapeDtypeStruct((b, n_lat, Q_COLS), F32),
        compiler_params=_cparams(("parallel", "parallel")),
        name="attention",
    )(sink, qkv, qkv, qkv, qkv, qkv, qkv, qkv, qkv, qkv)


MOE_ROWS = 256


def _row_scatter_kernel(dest_ref, pad_ref, f_ref, xs_hbm, zero_ref, sem, *, tm, n_pad):
    base = pl.program_id(0) * tm

    @pl.when(pl.program_id(0) == 0)
    def _():
        zero_ref[...] = jnp.zeros_like(zero_ref)

        def zero_row(r, carry):
            pltpu.make_async_copy(zero_ref, xs_hbm.at[pl.ds(pad_ref[r], 1)], sem).start()
            return carry

        lax.fori_loop(0, n_pad, zero_row, 0, unroll=8)
        for _ in range(n_pad // tm):
            pltpu.make_async_copy(f_ref, xs_hbm.at[pl.ds(0, tm)], sem).wait()

    def issue(r, carry):
        slot = 2 * (base + r)
        pltpu.make_async_copy(f_ref.at[pl.ds(r, 1)], xs_hbm.at[pl.ds(dest_ref[slot], 1)], sem).start()
        pltpu.make_async_copy(f_ref.at[pl.ds(r, 1)], xs_hbm.at[pl.ds(dest_ref[slot + 1], 1)], sem).start()
        return carry

    lax.fori_loop(0, tm, issue, 0, unroll=8)
    for _ in range(2):
        pltpu.make_async_copy(f_ref, xs_hbm.at[pl.ds(0, tm)], sem).wait()


def _row_scatter(f, dest, pad_rows, n_rows):
    n_tok, d = f.shape
    tm = TOK_TILE
    n_pad = pad_rows.shape[0]
    assert n_pad % tm == 0 and 2 * n_tok + n_pad == n_rows
    return pl.pallas_call(
        functools.partial(_row_scatter_kernel, tm=tm, n_pad=n_pad),
        grid_spec=pltpu.PrefetchScalarGridSpec(
            num_scalar_prefetch=2,
            grid=(n_tok // tm,),
            in_specs=[pl.BlockSpec((tm, d), lambda i, dr, pr: (i, 0))],
            out_specs=pl.BlockSpec(memory_space=pl.ANY),
            scratch_shapes=[pltpu.VMEM((1, d), f.dtype), pltpu.SemaphoreType.DMA(())]),
        out_shape=jax.ShapeDtypeStruct((n_rows, d), f.dtype),
        compiler_params=_cparams(("arbitrary",)),
        name="moe_scatter",
    )(dest, pad_rows, f)


def _expert_kernel(be_ref, x_ref, wg_ref, wu_ref, wd_ref, o_ref):
    h = _bf(x_ref[...])
    act = _silu(jnp.dot(h, wg_ref[0], preferred_element_type=F32)) * jnp.dot(h, wu_ref[0],
                                                                              preferred_element_type=F32)
    o_ref[...] = _dot(act, wd_ref[0])


def _experts(xs, block_e, wg, wu, wd):
    n_rows, d = xs.shape
    ff = wg.shape[2]
    rows = MOE_ROWS
    return pl.pallas_call(
        _expert_kernel,
        grid_spec=pltpu.PrefetchScalarGridSpec(
            num_scalar_prefetch=1,
            grid=(n_rows // rows,),
            in_specs=[pl.BlockSpec((rows, d), lambda i, be: (i, 0)),
                      pl.BlockSpec((1, d, ff), lambda i, be: (be[i], 0, 0)),
                      pl.BlockSpec((1, d, ff), lambda i, be: (be[i], 0, 0)),
                      pl.BlockSpec((1, ff, d), lambda i, be: (be[i], 0, 0))],
            out_specs=pl.BlockSpec((rows, d), lambda i, be: (i, 0))),
        out_shape=jax.ShapeDtypeStruct((n_rows, d), F32),
        compiler_params=_cparams(("arbitrary",)),
        name="moe_experts",
    )(block_e, xs, wg, wu, wd)


def _combine_kernel(dest_ref, x_ref, modl_ref, modc_ref, w_ref, gfin_ref, ys_hbm, o_ref, y1_ref, y2_ref, sem, *,
                    tm, tiles_per_row):
    tile = pl.program_id(0) * tiles_per_row + pl.program_id(1)
    n_tiles = pl.num_programs(0) * tiles_per_row
    buf = tile % 2

    def fetch(t, b):
        def issue(r, carry):
            slot = 2 * (t * tm + r)
            pltpu.make_async_copy(ys_hbm.at[pl.ds(dest_ref[slot], 1)], y1_ref.at[b, pl.ds(r, 1)], sem.at[b]).start()
            pltpu.make_async_copy(ys_hbm.at[pl.ds(dest_ref[slot + 1], 1)], y2_ref.at[b, pl.ds(r, 1)],
                                  sem.at[b]).start()
            return carry

        lax.fori_loop(0, tm, issue, 0, unroll=8)

    @pl.when(tile == 0)
    def _():
        fetch(0, 0)

    @pl.when(tile + 1 < n_tiles)
    def _():
        fetch(tile + 1, 1 - buf)

    pltpu.make_async_copy(ys_hbm.at[pl.ds(0, tm)], y1_ref.at[buf], sem.at[buf]).wait()
    pltpu.make_async_copy(ys_hbm.at[pl.ds(0, tm)], y2_ref.at[buf], sem.at[buf]).wait()
    w = w_ref[0]
    moe = w[:, 0:1] * y1_ref[buf] + w[:, 1:2] * y2_ref[buf]
    y = x_ref[0] + _gate_vec(modl_ref, modc_ref, 5, 0, tm, 0) * moe
    o_ref[0] = y * lax.rsqrt(jnp.mean(y * y, axis=-1, keepdims=True) + NORM_EPS) * gfin_ref[...]


def _combine(x, mods, tokw, dest, ys, final_g):
    b, s, d = x.shape
    tm = TOK_TILE
    tiles = s // tm
    return pl.pallas_call(
        functools.partial(_combine_kernel, tm=tm, tiles_per_row=tiles),
        grid_spec=pltpu.PrefetchScalarGridSpec(
            num_scalar_prefetch=1,
            grid=(b, tiles),
            in_specs=[pl.BlockSpec((1, tm, d), lambda i, j, dr: (i, j, 0)),
                      pl.BlockSpec((1, N_MOD, d), lambda i, j, dr: (i, 0, 0)),
                      pl.BlockSpec((1, N_MOD, d), lambda i, j, dr: (0, 0, 0)),
                      pl.BlockSpec((1, tm, LANES), lambda i, j, dr: (i, j, 0)),
                      pl.BlockSpec((1, d), lambda i, j, dr: (0, 0)),
                      pl.BlockSpec(memory_space=pl.ANY)],
            out_specs=pl.BlockSpec((1, tm, d), lambda i, j, dr: (i, j, 0)),
            scratch_shapes=[pltpu.VMEM((2, tm, d), F32), pltpu.VMEM((2, tm, d), F32),
                            pltpu.SemaphoreType.DMA((2,))]),
        out_shape=jax.ShapeDtypeStruct((b, s, d), F32),
        compiler_params=_cparams(("arbitrary", "arbitrary")),
        name="moe_combine",
    )(dest, x, mods, mods, tokw, final_g.reshape(1, d), ys)


def _moe_routing(eidx):
    m = eidx.shape[0]
    oh = (eidx[:, None] == jnp.arange(N_EXPERTS, dtype=jnp.int32)[None, :]).astype(jnp.int32)
    csum = jnp.cumsum(oh, axis=0)
    rank = jnp.sum((csum - oh) * oh, axis=1)
    counts = csum[-1]
    padded = (counts + MOE_ROWS - 1) // MOE_ROWS * MOE_ROWS
    pad_end = jnp.cumsum(padded)
    dest = jnp.sum(oh * (pad_end - padded)[None, :], axis=1) + rank
    n_blocks = m // MOE_ROWS + N_EXPERTS
    starts = jnp.arange(n_blocks, dtype=jnp.int32) * MOE_ROWS
    block_e = jnp.minimum(jnp.sum((starts[:, None] >= pad_end[None, :]).astype(jnp.int32), axis=1), N_EXPERTS - 1)
    pad_cnt = padded - counts
    pad_cum = jnp.cumsum(pad_cnt)
    idx = jnp.arange(N_EXPERTS * MOE_ROWS, dtype=jnp.int32)
    owner = jnp.sum((idx[:, None] >= pad_cum[None, :]).astype(jnp.int32), axis=1)
    own = (jnp.minimum(owner, N_EXPERTS - 1)[:, None] == jnp.arange(N_EXPERTS)[None, :]).astype(jnp.int32)
    in_expert = jnp.sum(own * (pad_end - pad_cnt)[None, :], axis=1) + idx - jnp.sum(
        own * (pad_cum - pad_cnt)[None, :], axis=1)
    pad_rows = jnp.where(owner < N_EXPERTS, in_expert, pad_end[-1] + idx - pad_cum[-1])
    return dest.astype(jnp.int32), block_e, pad_rows.astype(jnp.int32), n_blocks * MOE_ROWS


def _rope_tables(n_lat, ctx_len):
    rows = n_lat // GRID_W
    row = jnp.repeat(jnp.arange(rows, dtype=F32), GRID_W)
    col = jnp.tile(jnp.arange(GRID_W, dtype=F32), rows)
    inv = ROPE_BASE ** (-jnp.arange(0, AX_DIM, 2, dtype=F32) / AX_DIM)
    ar, ac = row[:, None] * inv, col[:, None] * inv
    cos = jnp.concatenate([jnp.cos(ar), jnp.cos(ar), jnp.cos(ac), jnp.cos(ac)], axis=-1)
    sin = jnp.concatenate([-jnp.sin(ar), jnp.sin(ar), -jnp.sin(ac), jnp.sin(ac)], axis=-1)
    n_heads = ROPE_COLS // HD
    cos = jnp.concatenate([jnp.ones((ctx_len, HD), F32), cos], axis=0)
    sin = jnp.concatenate([jnp.zeros((ctx_len, HD), F32), sin], axis=0)
    half = AX_DIM // 2
    j = jnp.arange(HD)
    partner = jnp.where((j % AX_DIM) < half, j + half, j - half)
    perm = (jnp.arange(n_heads)[:, None] * HD + partner[None, :]).reshape(-1)
    return jnp.tile(cos, (1, n_heads)), jnp.tile(sin, (1, n_heads)), perm


def kernel(x, c, ctx, c_ctx, mod_w, mod_b, norm_mix, norm_ffn, norm_final, rec_w_in, rec_w_out, rwkv_mu, rwkv_w0, rwkv_w_up, rwkv_a0, rwkv_a_up, rwkv_g_up, rwkv_k_k, rwkv_k_a, rwkv_r_k, rwkv_ln_w, rwkv_ln_b, hgrn_lb, hgrn_norm, ffn_w_gate, ffn_w_up, ffn_w_down, att_w_in, att_w_out, att_sink, moe_router, moe_router_b, moe_w_gate, moe_w_up, moe_w_down):
    bsz, n_lat, d = x.shape
    ctx_len = ctx.shape[1]

    n_rows = -(-(bsz + 1) // 8) * 8
    cvec = jnp.zeros((n_rows, d), F32).at[:bsz].set(c).at[bsz].set(c_ctx)
    mods = [_adaln(cvec, mod_w[l], mod_b[l]).reshape(n_rows, N_MOD, d) for l in range(2)]

    ops, vv, g_bonus, gam, p_hgrn = _rec_in(ctx, x, mods[0], bsz, norm_mix[0], _bf(rec_w_in[0]), rwkv_mu[0],
                                            rwkv_w0[0], rwkv_w_up[0], rwkv_a0[0], rwkv_a_up[0], rwkv_g_up[0],
                                            rwkv_k_k[0], rwkv_k_a[0], rwkv_r_k[0].reshape(-1))
    oa = _rwkv_scan(ops, vv, gam, ctx_len)
    lb = jnp.cumsum(jax.nn.softmax(hgrn_lb.astype(F32), axis=1), axis=1)[:, 0].reshape(2, 1, B_W)
    ob = _hgrn_scan(p_hgrn, lb, ctx_len)
    xcat = _rec_out(ctx, x, mods[0], bsz, oa, g_bonus, ob, p_hgrn, rwkv_ln_w[0], rwkv_ln_b[0], hgrn_norm[0],
                    _bf(rec_w_out[0]), norm_ffn[0], _bf(ffn_w_gate[0]), _bf(ffn_w_up[0]), _bf(ffn_w_down[0]))

    cos, sin, perm = _rope_tables(n_lat, ctx_len)
    w_att = att_w_in[0]
    qkv = _proj_rope(xcat, mods[1], bsz, norm_mix[1], _bf(w_att), _bf(w_att[:, perm]), cos, sin, ctx_len)
    att = _attention(qkv, att_sink[0], ctx_len)
    x_lat, tokw, eidx, f_lat = _att_out(xcat, mods[1], att, _bf(att_w_out[0]), norm_ffn[1], moe_router[0],
                                        moe_router_b[0], x_skip=ctx_len)
    dest, block_e, pad_rows, n_rows = _moe_routing(eidx[..., :2].reshape(-1))
    xs = _row_scatter(f_lat.reshape(bsz * n_lat, d), dest, pad_rows, n_rows)
    ys = _experts(xs, block_e, _bf(moe_w_gate[0]), _bf(moe_w_up[0]), _bf(moe_w_down[0]))
    return _combine(x_lat, mods[1], tokw, dest, ys, norm_final)
```

```python
import functools
import math

import jax
import jax.numpy as jnp
from jax import lax
from jax.experimental import pallas as pl
from jax.experimental.pallas import tpu as pltpu

F32 = jnp.float32
BF16 = jnp.bfloat16

N_MOD = 6
NORM_EPS = 1e-6
NEG_INF = -1e30

A_HEADS = 8
A_HD = 64
A_W = A_HEADS * A_HD
DECAY_LORA = 64
AAA_LORA = 64
GATE_LORA = 128
RWKV_COLS = 3 * A_W + DECAY_LORA + AAA_LORA + GATE_LORA
GN_EPS = 64e-5

B_HEADS = 4
B_DK = 128
B_W = B_HEADS * B_DK
HGRN_COLS = 5 * B_W

HQ = 16
HKV = 4
GQ = HQ // HKV
HD = 64
WINDOW = 128
ATT_BLOCK = 128
ATT_STACK = GQ
AX_DIM = HD // 2
ROPE_BASE = 10000.0
GRID_W = 64
Q_COLS = HQ * HD
KV_COLS = HKV * HD
ROPE_COLS = Q_COLS + KV_COLS
ATT_COLS = Q_COLS + 2 * KV_COLS

N_EXPERTS = 8
LANES = 128
CHUNK = 64
SCAN_BLOCK = 256
MXU_WIDTH = 256
RWKV_GROUP = MXU_WIDTH // A_HD
TOK_TILE = 256
PROJ_TILE = 768
VMEM_LIMIT = 56 * 1024 * 1024

OP_KAP, OP_RT, OP_KBAR, OP_BBAR, OP_KGAM, OP_BGAM = range(6)
N_OPS = 6


def _cparams(sem):
    return pltpu.CompilerParams(dimension_semantics=sem, vmem_limit_bytes=VMEM_LIMIT)


def _bf(x):
    return x.astype(BF16)


def _dot(a, b):
    return jnp.dot(_bf(a), _bf(b), preferred_element_type=F32)


def _dot_nt(a, b):
    return lax.dot_general(_bf(a), _bf(b), (((1,), (1,)), ((), ())), preferred_element_type=F32)


def _dot_tn(a, b):
    return lax.dot_general(_bf(a), _bf(b), (((0,), (0,)), ((), ())), preferred_element_type=F32)


def _dot_f32(a, b):
    a_hi, b_hi = _bf(a), _bf(b)
    a_lo, b_lo = _bf(a - a_hi.astype(F32)), _bf(b - b_hi.astype(F32))
    acc = jnp.dot(a_hi, b_hi, preferred_element_type=F32)
    acc += jnp.dot(a_hi, b_lo, preferred_element_type=F32)
    acc += jnp.dot(a_lo, b_hi, preferred_element_type=F32)
    return acc


def _sigmoid(x):
    return 0.5 * jnp.tanh(0.5 * x) + 0.5


def _silu(x):
    return x * _sigmoid(x)


def _segsum(x, ones_bd):
    hi = _bf(x)
    lo = _bf(x - hi.astype(F32))
    return jnp.dot(hi, ones_bd, preferred_element_type=F32) + jnp.dot(lo, ones_bd, preferred_element_type=F32)


def _dot_split(a, x):
    hi = _bf(x)
    lo = _bf(x - hi.astype(F32))
    return jnp.dot(a, hi, preferred_element_type=F32) + jnp.dot(a, lo, preferred_element_type=F32)


def _block_ones(width, seg):
    i = jnp.arange(width) // seg
    return (i[:, None] == i[None, :]).astype(BF16)


def _modulate(x, g, modl_ref, modc_ref, row, pos0, ctx_len, pos=None):
    y = x * lax.rsqrt(jnp.mean(x * x, axis=-1, keepdims=True) + NORM_EPS) * g
    if pos is None:
        pos = pos0 + lax.broadcasted_iota(jnp.int32, (x.shape[0], 1), 0)
    is_ctx = pos < ctx_len
    shift = jnp.where(is_ctx, modc_ref[0, row:row + 1, :], modl_ref[0, row:row + 1, :])
    scale = jnp.where(is_ctx, modc_ref[0, row + 1:row + 2, :], modl_ref[0, row + 1:row + 2, :])
    return y * (1.0 + scale) + shift


def _gate_vec(modl_ref, modc_ref, row, pos0, n, ctx_len):
    pos = pos0 + lax.broadcasted_iota(jnp.int32, (n, 1), 0)
    return jnp.where(pos < ctx_len, modc_ref[0, row:row + 1, :], modl_ref[0, row:row + 1, :])


def _mod_specs(d, ctx_row):
    return [pl.BlockSpec((1, N_MOD, d), lambda i, *_: (i, 0, 0)),
            pl.BlockSpec((1, N_MOD, d), lambda *_: (ctx_row, 0, 0))]


def _adaln_kernel(c_ref, w_ref, b_ref, o_ref):
    o_ref[...] = _dot(_silu(c_ref[...]), w_ref[...]) + b_ref[...]


def _adaln(cvec, w, b):
    r, d = cvec.shape
    n = w.shape[1]
    tn = 1024
    return pl.pallas_call(
        _adaln_kernel,
        grid=(n // tn,),
        in_specs=[pl.BlockSpec((r, d), lambda j: (0, 0)),
                  pl.BlockSpec((d, tn), lambda j: (0, j)),
                  pl.BlockSpec((1, tn), lambda j: (0, j))],
        out_specs=pl.BlockSpec((r, tn), lambda j: (0, j)),
        out_shape=jax.ShapeDtypeStruct((r, n), F32),
        compiler_params=_cparams(("parallel",)),
        name="adaln",
    )(cvec, w, b.reshape(1, n))


def _proj_rope_kernel(x_ref, modl_ref, modc_ref, g_ref, w_ref, wrot_ref, cos_ref, sin_ref, o_ref, *, ctx_len, tm,
                      rope_cols):
    pos0 = pl.program_id(1) * tm
    h = _bf(_modulate(x_ref[0], g_ref[...], modl_ref, modc_ref, 0, pos0, ctx_len))
    y = jnp.dot(h, w_ref[...], preferred_element_type=F32)
    yr = jnp.dot(h, wrot_ref[...], preferred_element_type=F32)
    o_ref[0, :, :rope_cols] = y[:, :rope_cols] * cos_ref[...] + yr * sin_ref[...]
    o_ref[0, :, rope_cols:] = y[:, rope_cols:]


def _proj_rope(xcat, mods, ctx_row, g, w, wrot, cos, sin, ctx_len):
    b, s, d = xcat.shape
    n = w.shape[1]
    tm = PROJ_TILE
    rope_cols = wrot.shape[1]
    return pl.pallas_call(
        functools.partial(_proj_rope_kernel, ctx_len=ctx_len, tm=tm, rope_cols=rope_cols),
        grid=(b, s // tm),
        in_specs=[pl.BlockSpec((1, tm, d), lambda i, j: (i, j, 0))] + _mod_specs(d, ctx_row) + [
            pl.BlockSpec((1, d), lambda i, j: (0, 0)),
            pl.BlockSpec((d, n), lambda i, j: (0, 0)),
            pl.BlockSpec((d, rope_cols), lambda i, j: (0, 0)),
            pl.BlockSpec((tm, rope_cols), lambda i, j: (j, 0)),
            pl.BlockSpec((tm, rope_cols), lambda i, j: (j, 0))],
        out_specs=pl.BlockSpec((1, tm, n), lambda i, j: (i, j, 0)),
        out_shape=jax.ShapeDtypeStruct((b, s, n), F32),
        compiler_params=_cparams(("parallel", "parallel")),
        name="proj_rope",
    )(xcat, mods, mods, g.reshape(1, d), w, wrot, cos, sin)


DECAY_SCALE = math.exp(-0.5)


def _stream_specs(tm, d, ctx_len):
    assert ctx_len == tm
    return [pl.BlockSpec((1, tm, d), lambda i, j: (i, 0, 0)),
            pl.BlockSpec((1, tm, d), lambda i, j: (i, jnp.maximum(j - 1, 0), 0))]


def _stream_tile(c_ref, x_ref):
    return jnp.where(pl.program_id(1) == 0, c_ref[0], x_ref[0])


def _rec_in_kernel(c_ref, x_ref, xp_ref, xn_ref, modl_ref, modc_ref, g_ref, w_ref, mu_ref, w0_ref, wup_ref, a0_ref,
                   aup_ref, gup_ref, kk_ref, ka_ref, rk_ref, ones_ref, ops_ref, v_ref, gb_ref, gam_ref, ph_ref, *,
                   ctx_len, seq_len, tm):
    pos0 = pl.program_id(1) * tm
    gvec = g_ref[...]
    h = _bf(_modulate(_stream_tile(c_ref, x_ref), gvec, modl_ref, modc_ref, 0, pos0, ctx_len))
    p = jnp.dot(h, w_ref[:, :RWKV_COLS], preferred_element_type=F32)

    hgrn_pieces = [(RWKV_COLS + c, min(B_W, HGRN_COLS - c)) for c in range(0, HGRN_COLS, B_W)]

    def hgrn_piece():
        if hgrn_pieces:
            lo, width = hgrn_pieces.pop(0)
            ph_ref[0, :, lo - RWKV_COLS:lo - RWKV_COLS + width] = jnp.dot(h, w_ref[:, lo:lo + width],
                                                                          preferred_element_type=F32)

    halo = lax.broadcasted_iota(jnp.int32, (16, 1), 0)
    halo_pos = jnp.where(halo < 8, pos0 - 8 + halo, pos0 + tm - 8 + halo)
    xh = jnp.concatenate([xp_ref[0], xn_ref[0]], axis=0)
    p_halo = jnp.dot(_bf(_modulate(xh, gvec, modl_ref, modc_ref, 0, 0, ctx_len, pos=halo_pos)),
                     w_ref[:, :RWKV_COLS], preferred_element_type=F32)
    rows = lax.broadcasted_iota(jnp.int32, (tm, 1), 0)
    starts_seq = (pos0 == 0) | (pos0 == ctx_len)
    ends_seq = (pos0 + tm == ctx_len) | (pos0 + tm == seq_len)
    prev_halo = jnp.where(starts_seq, 0.0, p_halo[7:8, :])
    next_halo = jnp.where(ends_seq, 0.0, p_halo[8:9, :])
    prev = jnp.where(rows == 0, prev_halo, pltpu.roll(p, 1, 0))
    nxt = jnp.where(rows == tm - 1, next_halo, pltpu.roll(p, tm - 1, 0))
    p = p + mu_ref[...] * (0.5 * (prev + nxt) - p)
    hgrn_piece()

    r = p[:, 0:A_W]
    k = p[:, A_W:2 * A_W]
    v = p[:, 2 * A_W:3 * A_W]
    lo = 3 * A_W
    wd = p[:, lo:lo + DECAY_LORA]
    ad = p[:, lo + DECAY_LORA:lo + DECAY_LORA + AAA_LORA]
    gd = p[:, lo + DECAY_LORA + AAA_LORA:lo + DECAY_LORA + AAA_LORA + GATE_LORA]

    tw = jnp.tanh(wd)
    a = _sigmoid(a0_ref[...] + _dot(ad, aup_ref[...]))
    ones_bd = ones_ref[...]
    kk = k * kk_ref[...]
    kk = kk / jnp.maximum(jnp.sqrt(_segsum(kk * kk, ones_bd)), 1e-12)
    k = k * (1.0 + (a - 1.0) * ka_ref[...])
    b = kk * a
    v_ref[0] = _bf(v)
    gb_ref[0, :, :A_W] = _dot(_sigmoid(gd), gup_ref[...])
    gb_ref[0, :, A_W:] = _segsum(r * k * rk_ref[...], ones_bd) * v
    hgrn_piece()

    trow = lax.broadcasted_iota(jnp.int32, (tm, tm), 0)
    tcol = lax.broadcasted_iota(jnp.int32, (tm, tm), 1)
    same = (trow // CHUNK) == (tcol // CHUNK)
    n_chunks = tm // CHUNK
    for d in range(2):
        lw = -DECAY_SCALE * _sigmoid(w0_ref[d:d + 1, :] + _dot(tw, wup_ref[d]))
        before = (tcol <= trow) if d == 0 else (tcol >= trow)
        g_incl = _dot_split(jnp.where(same & before, 1.0, 0.0).astype(BF16), lw)
        last = [c * CHUNK + (CHUNK - 1 if d == 0 else 0) for c in range(n_chunks)]
        gam = jnp.exp(jnp.concatenate([g_incl[t:t + 1] for t in last], axis=0))
        gam_ref[0, d, 0] = gam
        hgrn_piece()
        e_ng = jnp.exp(-g_incl)
        e_tail = e_ng * jnp.concatenate([jnp.broadcast_to(gam[c:c + 1], (CHUNK, A_W)) for c in range(n_chunks)],
                                        axis=0)
        operands = {OP_KAP: kk * jnp.exp(g_incl - lw),
                    OP_RT: r * jnp.exp(g_incl),
                    OP_KBAR: k * e_ng, OP_BBAR: b * e_ng,
                    OP_KGAM: k * e_tail, OP_BGAM: b * e_tail}
        for sec, val in operands.items():
            ops_ref[0, d, :, sec * A_W:(sec + 1) * A_W] = _bf(val)
        hgrn_piece()
    while hgrn_pieces:
        hgrn_piece()


def _rec_in(ctx, x, mods, ctx_row, g, w, mu, w0, w_up, a0, a_up, g_up, k_k, k_a, r_k):
    b, n_lat, d = x.shape
    ctx_len = ctx.shape[1]
    s = ctx_len + n_lat
    tm = SCAN_BLOCK
    assert n_lat % tm == 0
    nb8 = n_lat // 8
    row = lambda a: a.reshape(1, -1)
    full = lambda a: pl.BlockSpec(a.shape, lambda i, j: (0,) * a.ndim)
    args = [row(g), w, row(mu), w0, w_up, row(a0), a_up, g_up, row(k_k), row(k_a), row(r_k),
            _block_ones(A_W, A_HD)]
    return pl.pallas_call(
        functools.partial(_rec_in_kernel, ctx_len=ctx_len, seq_len=s, tm=tm),
        grid=(b, s // tm),
        in_specs=_stream_specs(tm, d, ctx_len) + [
            pl.BlockSpec((1, 8, d), lambda i, j: (i, jnp.maximum((j - 1) * (tm // 8) - 1, 0), 0)),
            pl.BlockSpec((1, 8, d), lambda i, j: (i, jnp.minimum(j * (tm // 8), nb8 - 1), 0)),
        ] + _mod_specs(d, ctx_row) + [full(a) for a in args],
        out_specs=[pl.BlockSpec((1, 2, tm, N_OPS * A_W), lambda i, j: (i, 0, j, 0)),
                   pl.BlockSpec((1, tm, A_W), lambda i, j: (i, j, 0)),
                   pl.BlockSpec((1, tm, 2 * A_W), lambda i, j: (i, j, 0)),
                   pl.BlockSpec((1, 2, 1, tm // CHUNK, A_W), lambda i, j: (i, 0, j, 0, 0)),
                   pl.BlockSpec((1, tm, HGRN_COLS), lambda i, j: (i, j, 0))],
        out_shape=[jax.ShapeDtypeStruct((b, 2, s, N_OPS * A_W), BF16),
                   jax.ShapeDtypeStruct((b, s, A_W), BF16),
                   jax.ShapeDtypeStruct((b, s, 2 * A_W), F32),
                   jax.ShapeDtypeStruct((b, 2, s // tm, tm // CHUNK, A_W), F32),
                   jax.ShapeDtypeStruct((b, s, HGRN_COLS), F32)],
        compiler_params=_cparams(("parallel", "parallel")),
        name="rec_in",
    )(ctx, x, x, x, mods, mods, *args)


def _chunk_order(d, j, n_ctx_chunks, n_chunks):
    back = jnp.where(j < n_ctx_chunks, n_ctx_chunks - 1 - j, n_chunks - 1 + n_ctx_chunks - j)
    return jnp.where(d == 0, j, back)


def _incl_mask(rev):
    row = lax.broadcasted_iota(jnp.int32, (CHUNK, CHUNK), 0)
    col = lax.broadcasted_iota(jnp.int32, (CHUNK, CHUNK), 1)
    return jnp.where(rev, row - col, col - row) <= 0


def _scan_step_coords(t, n_steps):
    return jnp.minimum(t, n_steps - 1), jnp.maximum(t - 1, 0)


def _rwkv_chunk_kernel(ops_ref, v_ref, gam_ref, o_ref, s_ref, tr_ref, uu_ref, ol_ref, ab_ref,
                       m2_ref, cc_ref, gm_ref, *, n_blocks, n_steps):
    step = pl.program_id(0)
    t_in, t_out = _scan_step_coords(step, n_steps)
    rev = (t_in // n_blocks) % 2 == 1
    rev_out = (t_out // n_blocks) % 2 == 1
    first_out = t_out % n_blocks == 0
    w_slot = step % 2
    r_slot = 1 - w_slot

    @pl.when(step == 0)
    def _():
        s_ref[...] = jnp.zeros_like(s_ref)
        for ref in (tr_ref, uu_ref, ol_ref, ab_ref, m2_ref, cc_ref, gm_ref):
            ref[1] = jnp.zeros(ref.shape[1:], ref.dtype)

    gw = RWKV_GROUP * A_HD
    groups = range(A_HEADS // RWKV_GROUP)
    gsl = [slice(g * gw, (g + 1) * gw) for g in groups]
    row = lax.broadcasted_iota(jnp.int32, (CHUNK, gw), 0)
    col = lax.broadcasted_iota(jnp.int32, (CHUNK, gw), 1) % A_HD
    diff = jnp.where(rev, row - col, col - row)
    strict_c = diff < 0
    incl_c = diff <= 0
    eye_c = jnp.where(diff == 0, 1.0, 0.0)
    bd_mask = (lax.broadcasted_iota(jnp.int32, (gw, gw), 0) // A_HD
               == lax.broadcasted_iota(jnp.int32, (gw, gw), 1) // A_HD)

    def bd(x):
        return jnp.where(bd_mask, jnp.concatenate([x] * RWKV_GROUP, axis=0), jnp.zeros((), x.dtype))

    def stack(x):
        return jnp.concatenate([x[:, h * A_HD:(h + 1) * A_HD] for h in range(RWKV_GROUP)], axis=0)

    n = SCAN_BLOCK // CHUNK
    rows = [pl.ds(pl.multiple_of(jnp.where(rev, (n - 1 - i) * CHUNK, i * CHUNK), CHUNK), CHUNK) for i in range(n)]
    rows_out = [pl.ds(pl.multiple_of(jnp.where(rev_out, (n - 1 - i) * CHUNK, i * CHUNK), CHUNK), CHUNK)
                for i in range(n)]
    units = [(i, g) for i in range(n) for g in groups]

    s = [jnp.where(first_out, 0.0, s_ref[g]) for g in groups]
    zr = {}
    pieces = []

    def advance(i):
        for g in groups:
            ui = i * len(groups) + g
            zr[i, g] = _dot_nt(tr_ref[r_slot, ui], bd(s[g]))
            s[g] = s[g] * gm_ref[r_slot, ui] - jnp.dot(_bf(s[g]), m2_ref[r_slot, ui],
                                                       preferred_element_type=F32) + cc_ref[r_slot, ui]

    def emit_out(i):
        for g in groups:
            ui = i * len(groups) + g
            z = zr[i, g][:CHUNK] + uu_ref[r_slot, ui]
            o_ref[0, 0, rows_out[i], gsl[g]] = (zr[i, g][CHUNK:] + ol_ref[r_slot, ui]
                                                - _dot(ab_ref[r_slot, ui], bd(z)))

    for i in range(n):
        pieces += [functools.partial(advance, i), functools.partial(emit_out, i)]

    def state_piece():
        if pieces:
            pieces.pop(0)()

    def operand(sec, u):
        i, g = u
        return ops_ref[0, 0, rows[i], sec * A_W + g * gw:sec * A_W + (g + 1) * gw]

    kap = {u: operand(OP_KAP, u) for u in units}
    rt = {u: operand(OP_RT, u) for u in units}
    kbar = {u: operand(OP_KBAR, u) for u in units}
    bbar = {u: operand(OP_BBAR, u) for u in units}
    kgam = {u: operand(OP_KGAM, u) for u in units}
    bgam = {u: operand(OP_BGAM, u) for u in units}
    v = {(i, g): v_ref[0, rows[i], gsl[g]] for i, g in units}
    chunk_of = [jnp.where(rev, n - 1 - i, i) for i in range(n)]
    gam = {(i, g): gam_ref[0, 0, 0, pl.ds(chunk_of[i], 1), gsl[g]] for i, g in units}

    x = {u: jnp.concatenate([kap[u], rt[u]], axis=0) for u in units}
    yb = {u: _dot_nt(x[u], bd(bbar[u])) for u in units}
    state_piece()
    a = {u: jnp.where(strict_c, -yb[u][:CHUNK], 0.0) for u in units}
    xs = {u: eye_c + a[u] for u in units}
    pw = {u: _dot(a[u], bd(a[u])) for u in units}
    state_piece()
    for _ in range(4):
        st = {u: _dot(jnp.concatenate([pw[u], xs[u]], axis=0), bd(pw[u])) for u in units}
        pw = {u: st[u][:CHUNK] for u in units}
        xs = {u: xs[u] + st[u][CHUNK:] for u in units}
        state_piece()
    t_inv = {u: xs[u] + _dot(xs[u], bd(pw[u])) for u in units}
    state_piece()
    yk = {u: _dot_nt(x[u], bd(kbar[u])) for u in units}
    ykm = {u: jnp.concatenate([jnp.where(strict_c, yk[u][:CHUNK], 0.0), jnp.where(incl_c, yk[u][CHUNK:], 0.0)],
                              axis=0) for u in units}
    wo = {u: _dot(ykm[u], bd(v[u])) for u in units}
    w1 = {u: wo[u][:CHUNK] for u in units}
    o_loc = {u: wo[u][CHUNK:] for u in units}
    state_piece()
    tk = {u: _dot(t_inv[u], bd(kap[u])) for u in units}
    uu = {u: _dot(t_inv[u], bd(w1[u])) for u in units}
    while pieces:
        state_piece()
    for g in groups:
        s_ref[g] = s[g]
    a_rb = {u: jnp.where(incl_c, yb[u][CHUNK:], 0.0) for u in units}
    tu = {u: _dot_tn(jnp.concatenate([stack(tk[u]), stack(uu[u])], axis=1), bd(bgam[u])) for u in units}
    bd_m2 = {u: _bf(bd(tu[u][:A_HD])) for u in units}
    cc = {u: _dot_tn(stack(v[u]), bd(kgam[u])) - tu[u][A_HD:] for u in units}
    for u in units:
        ui = u[0] * len(groups) + u[1]
        tr_ref[w_slot, ui] = jnp.concatenate([_bf(tk[u]), rt[u]], axis=0)
        uu_ref[w_slot, ui] = uu[u]
        ol_ref[w_slot, ui] = o_loc[u]
        ab_ref[w_slot, ui] = _bf(a_rb[u])
        m2_ref[w_slot, ui] = bd_m2[u]
        cc_ref[w_slot, ui] = cc[u]
        gm_ref[w_slot, ui] = gam[u]


def _rwkv_scan(ops, vv, gam, ctx_len):
    b, _, s, _ = ops.shape
    nc, ncc = s // SCAN_BLOCK, ctx_len // SCAN_BLOCK

    n_steps = b * 2 * nc

    def coords(flat):
        d = (flat // nc) % 2
        return flat // (2 * nc), d, _chunk_order(d, flat % nc, ncc, nc)

    def coords_in(t):
        return coords(_scan_step_coords(t, n_steps)[0])

    def out_map(t):
        bi, d, blk = coords(_scan_step_coords(t, n_steps)[1])
        return bi, d, blk, 0

    def ops_map(t):
        bi, d, blk = coords_in(t)
        return bi, d, blk, 0

    def v_map(t):
        bi, _, blk = coords_in(t)
        return bi, blk, 0

    def gam_map(t):
        bi, d, blk = coords_in(t)
        return bi, d, blk, 0, 0

    n_groups = A_HEADS // RWKV_GROUP
    n_chunks = SCAN_BLOCK // CHUNK
    n_units = n_chunks * n_groups
    gw = RWKV_GROUP * A_HD
    return pl.pallas_call(
        functools.partial(_rwkv_chunk_kernel, n_blocks=nc, n_steps=n_steps),
        grid=(n_steps + 1,),
        in_specs=[pl.BlockSpec((1, 1, SCAN_BLOCK, N_OPS * A_W), ops_map),
                  pl.BlockSpec((1, SCAN_BLOCK, A_W), v_map),
                  pl.BlockSpec((1, 1, 1, n_chunks, A_W), gam_map)],
        out_specs=pl.BlockSpec((1, 1, SCAN_BLOCK, A_W), out_map),
        out_shape=jax.ShapeDtypeStruct((b, 2, s, A_W), F32),
        scratch_shapes=[pltpu.VMEM((n_groups, A_HD, gw), F32),
                        pltpu.VMEM((2, n_units, 2 * CHUNK, gw), BF16),
                        pltpu.VMEM((2, n_units, CHUNK, gw), F32),
                        pltpu.VMEM((2, n_units, CHUNK, gw), F32),
                        pltpu.VMEM((2, n_units, CHUNK, gw), BF16),
                        pltpu.VMEM((2, n_units, gw, gw), BF16),
                        pltpu.VMEM((2, n_units, CHUNK, gw), F32),
                        pltpu.VMEM((2, n_units, 1, gw), F32)],
        compiler_params=_cparams(("arbitrary",)),
        name="rwkv_scan",
    )(ops, vv, gam)


def _hgrn_chunk_kernel(q_ref, i_ref, f_ref, lb_ref, o_ref, s_ref):
    rev = pl.program_id(1) == 1

    @pl.when(pl.program_id(2) == 0)
    def _():
        s_ref[...] = jnp.zeros_like(s_ref)

    incl = _incl_mask(rev)
    incl_bf = jnp.where(incl, 1.0, 0.0).astype(BF16)
    lb = lb_ref[0]
    heads = range(B_HEADS)
    sls = [slice(h * B_DK, (h + 1) * B_DK) for h in heads]

    n = SCAN_BLOCK // CHUNK
    rows = [pl.ds(pl.multiple_of(jnp.where(rev, (n - 1 - i) * CHUNK, i * CHUNK), CHUNK), CHUNK) for i in range(n)]
    units = [(i, h) for i in range(n) for h in heads]
    q_in, k_in, q_st, k_tail, gam, v = ({} for _ in range(6))
    for i in range(n):
        f = lb + (1.0 - lb) * _sigmoid(f_ref[0, rows[i], :])
        logf = jnp.log(f)
        kf = 1.0 - f
        g_incl = _dot_split(incl_bf, logf)
        g_tot = jnp.sum(logf, axis=0, keepdims=True)
        g_mid = g_incl[CHUNK // 2 - 1:CHUNK // 2, :]
        q = _silu(q_ref[0, rows[i], :])
        v_i = i_ref[0, rows[i], :]
        q_in_i = q * jnp.exp(g_incl - g_mid)
        k_in_i = kf * jnp.exp(g_mid - g_incl)
        q_st_i = q_in_i * jnp.exp(g_mid)
        k_tail_i = k_in_i * jnp.exp(g_tot - g_mid)
        for h in heads:
            q_in[i, h], k_in[i, h] = q_in_i[:, sls[h]], k_in_i[:, sls[h]]
            q_st[i, h], k_tail[i, h] = q_st_i[:, sls[h]], k_tail_i[:, sls[h]]
            gam[i, h] = jnp.exp(g_tot)[:, sls[h]]
            v[i, h] = v_i[:, sls[h]]
    att = {u: jnp.where(incl, _dot_nt(q_in[u], k_in[u]), 0.0) for u in units}
    o_loc = {u: _dot(att[u], v[u]) for u in units}
    kv = {u: _dot_tn(v[u], k_tail[u]) for u in units}

    s = [s_ref[h] for h in heads]
    for i in range(n):
        for h in heads:
            o_ref[0, 0, rows[i], sls[h]] = o_loc[i, h] + _dot_nt(q_st[i, h], s[h])
            s[h] = s[h] * gam[i, h] + kv[i, h]
    for h in heads:
        s_ref[h] = s[h]


def _hgrn_scan(p, lb, ctx_len):
    b, s, _ = p.shape
    nc, ncc = s // SCAN_BLOCK, ctx_len // SCAN_BLOCK

    def sec(idx):
        return pl.BlockSpec((1, SCAN_BLOCK, B_W), lambda i, d, j: (i, _chunk_order(d, j, ncc, nc), idx))

    return pl.pallas_call(
        _hgrn_chunk_kernel,
        grid=(b, 2, nc),
        in_specs=[sec(0), sec(1),
                  pl.BlockSpec((1, SCAN_BLOCK, B_W), lambda i, d, j: (i, _chunk_order(d, j, ncc, nc), 2 + d)),
                  pl.BlockSpec((1, 1, B_W), lambda i, d, j: (d, 0, 0))],
        out_specs=pl.BlockSpec((1, 1, SCAN_BLOCK, B_W), lambda i, d, j: (i, d, _chunk_order(d, j, ncc, nc), 0)),
        out_shape=jax.ShapeDtypeStruct((b, 2, s, B_W), F32),
        scratch_shapes=[pltpu.VMEM((B_HEADS, B_DK, B_DK), F32)],
        compiler_params=_cparams(("parallel", "parallel", "arbitrary")),
        name="hgrn_scan",
    )(p, p, p, lb)


def _rec_out_kernel(c_ref, x_ref, modl_ref, modc_ref, oa_ref, g_ref, bonus_ref, ob_ref, gate_ref, lnw_ref, lnb_ref,
                    hgn_ref, ones_a_ref, ones_b_ref, w_ref, gffn_ref, wg_ref, wu_ref, wd_ref, o_ref, *, ctx_len, tm):
    pos0 = pl.program_id(1) * tm
    oa = oa_ref[0, 0] + oa_ref[0, 1]
    ones_a = ones_a_ref[...]
    mean = _segsum(oa, ones_a) * (1.0 / A_HD)
    cen = oa - mean
    var = _segsum(cen * cen, ones_a) * (1.0 / A_HD)
    ya = (cen * lax.rsqrt(var + GN_EPS) * lnw_ref[...] + lnb_ref[...] + bonus_ref[0]) * g_ref[0]
    ob = ob_ref[0, 0] + ob_ref[0, 1]
    ms = _segsum(ob * ob, ones_b_ref[...]) * (1.0 / B_DK)
    yb = ob * lax.rsqrt(ms + NORM_EPS) * hgn_ref[...] * _silu(gate_ref[0])
    y = _dot(ya, w_ref[:A_W, :]) + _dot(yb, w_ref[A_W:, :])
    gate = _gate_vec(modl_ref, modc_ref, 2, pos0, tm, ctx_len)
    x1 = _stream_tile(c_ref, x_ref) + gate * y
    h = _bf(_modulate(x1, gffn_ref[...], modl_ref, modc_ref, 3, pos0, ctx_len))
    act = _silu(jnp.dot(h, wg_ref[...], preferred_element_type=F32)) * jnp.dot(h, wu_ref[...],
                                                                               preferred_element_type=F32)
    o_ref[0] = x1 + _gate_vec(modl_ref, modc_ref, 5, pos0, tm, ctx_len) * _dot(act, wd_ref[...])


def _rec_out(ctx, x, mods, ctx_row, oa, g_bonus, ob, p_hgrn, ln_w, ln_b, hg_norm, w_out, g_ffn, wg, wu, wd):
    b, n_lat, d = x.shape
    ctx_len = ctx.shape[1]
    s = ctx_len + n_lat
    tm = TOK_TILE
    row = lambda a: a.reshape(1, -1)
    full = lambda a: pl.BlockSpec(a.shape, lambda i, j: (0,) * a.ndim, pipeline_mode=pl.Buffered(1))
    consts = [row(ln_w), row(ln_b), row(jnp.tile(hg_norm, B_HEADS)), _block_ones(A_W, A_HD),
              _block_ones(B_W, B_DK), w_out, row(g_ffn), wg, wu, wd]
    return pl.pallas_call(
        functools.partial(_rec_out_kernel, ctx_len=ctx_len, tm=tm),
        grid=(b, s // tm),
        in_specs=_stream_specs(tm, d, ctx_len) + _mod_specs(d, ctx_row) + [
            pl.BlockSpec((1, 2, tm, A_W), lambda i, j: (i, 0, j, 0)),
            pl.BlockSpec((1, tm, A_W), lambda i, j: (i, j, 0)),
            pl.BlockSpec((1, tm, A_W), lambda i, j: (i, j, 1)),
            pl.BlockSpec((1, 2, tm, B_W), lambda i, j: (i, 0, j, 0)),
            pl.BlockSpec((1, tm, B_W), lambda i, j: (i, j, 4)),
        ] + [full(a) for a in consts],
        out_specs=pl.BlockSpec((1, tm, d), lambda i, j: (i, j, 0)),
        out_shape=jax.ShapeDtypeStruct((b, s, d), F32),
        compiler_params=_cparams(("parallel", "parallel")),
        name="rec_out",
    )(ctx, x, mods, mods, oa, g_bonus, g_bonus, ob, p_hgrn, *consts)


def _att_out_kernel(x_ref, modl_ref, modc_ref, y_ref, w_ref, g_ref, rw_ref, rb_ref, o_ref, tw_ref, e_ref, f_ref, *,
                    tm):
    x = x_ref[0] + _gate_vec(modl_ref, modc_ref, 2, 0, tm, 0) * _dot(y_ref[0], w_ref[...])
    o_ref[0] = x
    f = _modulate(x, g_ref[...], modl_ref, modc_ref, 3, 0, 0)
    f_ref[0] = f
    logits = _dot_f32(f, rw_ref[...]) + rb_ref[...]
    lane = lax.broadcasted_iota(jnp.int32, logits.shape, 1).astype(F32)
    logits = jnp.where(lane < N_EXPERTS, logits, NEG_INF)
    ex = jnp.exp(logits - jnp.max(logits, axis=-1, keepdims=True))
    probs = ex / jnp.sum(ex, axis=-1, keepdims=True)
    p1 = jnp.max(probs, axis=-1, keepdims=True)
    i1 = jnp.min(jnp.where(probs == p1, lane, float(LANES)), axis=-1, keepdims=True)
    rest = jnp.where(lane == i1, -1.0, probs)
    p2 = jnp.max(rest, axis=-1, keepdims=True)
    i2 = jnp.min(jnp.where(rest == p2, lane, float(LANES)), axis=-1, keepdims=True)
    tot = p1 + p2
    tw_ref[0] = jnp.where(lane == 0.0, p1 / tot, jnp.where(lane == 1.0, p2 / tot, 0.0))
    e_ref[0] = jnp.where(lane == 0.0, i1, jnp.where(lane == 1.0, i2, 0.0)).astype(jnp.int32)


def _att_out(xcat, mods, y, w, g, router_w, router_b, x_skip):
    b, s, k = y.shape
    d = xcat.shape[-1]
    tm = TOK_TILE
    skip = x_skip // tm
    wpad = jnp.zeros((d, LANES), F32).at[:, :N_EXPERTS].set(router_w)
    bpad = jnp.zeros((1, LANES), F32).at[0, :N_EXPERTS].set(router_b)
    tile_spec = pl.BlockSpec((1, tm, d), lambda i, j: (i, j, 0))
    lane_spec = pl.BlockSpec((1, tm, LANES), lambda i, j: (i, j, 0))
    return pl.pallas_call(
        functools.partial(_att_out_kernel, tm=tm),
        grid=(b, s // tm),
        in_specs=[pl.BlockSpec((1, tm, d), lambda i, j: (i, j + skip, 0))] + _mod_specs(d, 0) + [
            pl.BlockSpec((1, tm, k), lambda i, j: (i, j, 0)),
            pl.BlockSpec((k, d), lambda i, j: (0, 0)),
            pl.BlockSpec((1, d), lambda i, j: (0, 0)),
            pl.BlockSpec((d, LANES), lambda i, j: (0, 0)),
            pl.BlockSpec((1, LANES), lambda i, j: (0, 0))],
        out_specs=[tile_spec, lane_spec, lane_spec, tile_spec],
        out_shape=[jax.ShapeDtypeStruct((b, s, d), F32), jax.ShapeDtypeStruct((b, s, LANES), F32),
                   jax.ShapeDtypeStruct((b, s, LANES), jnp.int32), jax.ShapeDtypeStruct((b, s, d), F32)],
        compiler_params=_cparams(("parallel", "parallel")),
        name="att_out_router",
    )(xcat, mods, mods, y, w, g.reshape(1, d), wpad, bpad)


def _attn_kernel(sink_ref, q_ref, kc_ref, vc_ref, kp_ref, kq_ref, kn_ref, vp_ref, vq_ref, vn_ref, o_ref, *,
                 ctx_len, n_lat):
    i = pl.program_id(1)
    nk = ctx_len + 3 * ATT_BLOCK
    k_all = jnp.concatenate([kc_ref[0], kp_ref[0], kq_ref[0], kn_ref[0]], axis=0)
    v_all = jnp.concatenate([vc_ref[0], vp_ref[0], vq_ref[0], vn_ref[0]], axis=0)
    row = lax.broadcasted_iota(jnp.int32, (ATT_BLOCK, nk), 0)
    col = lax.broadcasted_iota(jnp.int32, (ATT_BLOCK, nk), 1)
    rel = col - (ctx_len + ATT_BLOCK)
    kabs = i * ATT_BLOCK + rel
    band = (jnp.abs(row - rel) <= WINDOW) & (kabs >= 0) & (kabs < n_lat)
    bias = jnp.where((col < ctx_len) | band, 0.0, NEG_INF)
    bias = jnp.concatenate([bias] * ATT_STACK, axis=0)
    grp = lax.broadcasted_iota(jnp.int32, (ATT_STACK * ATT_BLOCK, 1), 0) // ATT_BLOCK
    scale = HD ** -0.5
    for hq0 in range(0, HQ, ATT_STACK):
        hk = hq0 // GQ
        kh = _bf(k_all[:, hk * HD:(hk + 1) * HD])
        vh = _bf(v_all[:, hk * HD:(hk + 1) * HD])
        heads = [hq0 + g for g in range(ATT_STACK)]
        q = jnp.concatenate([q_ref[0, :, hq * HD:(hq + 1) * HD] for hq in heads], axis=0) * scale
        sk = jnp.zeros((ATT_STACK * ATT_BLOCK, 1), F32)
        for g, hq in enumerate(heads):
            sk = jnp.where(grp == g, sink_ref[hq], sk)
        s = _dot_nt(q, kh) + bias
        m = jnp.maximum(jnp.max(s, axis=-1, keepdims=True), sk)
        p = jnp.exp(s - m)
        den = jnp.sum(p, axis=-1, keepdims=True) + jnp.exp(sk - m)
        o = jnp.dot(_bf(p), vh, preferred_element_type=F32) / den
        for g, hq in enumerate(heads):
            o_ref[0, :, hq * HD:(hq + 1) * HD] = o[g * ATT_BLOCK:(g + 1) * ATT_BLOCK]


def _attention(qkv, sink, ctx_len):
    b, s, _ = qkv.shape
    n_lat = s - ctx_len
    nb = n_lat // ATT_BLOCK
    cb = ctx_len // ATT_BLOCK
    kcol = Q_COLS // KV_COLS
    vcol = kcol + 1

    def band(colblk, shift):
        return pl.BlockSpec((1, ATT_BLOCK, KV_COLS),
                            lambda bi, i: (bi, cb + jnp.clip(i + shift, 0, nb - 1), colblk))

    return pl.pallas_call(
        functools.partial(_attn_kernel, ctx_len=ctx_len, n_lat=n_lat),
        grid=(b, nb),
        in_specs=[pl.BlockSpec(memory_space=pltpu.SMEM),
                  pl.BlockSpec((1, ATT_BLOCK, Q_COLS), lambda bi, i: (bi, cb + i, 0)),
                  pl.BlockSpec((1, ctx_len, KV_COLS), lambda bi, i: (bi, 0, kcol)),
                  pl.BlockSpec((1, ctx_len, KV_COLS), lambda bi, i: (bi, 0, vcol)),
                  band(kcol, -1), band(kcol, 0), band(kcol, 1),
                  band(vcol, -1), band(vcol, 0), band(vcol, 1)],
        out_specs=pl.BlockSpec((1, ATT_BLOCK, Q_COLS), lambda bi, i: (bi, i, 0)),
        out_shape=jax.ShapeDtypeStruct((b, n_lat, Q_COLS), F32),
        compiler_params=_cparams(("parallel", "parallel")),
        name="attention",
    )(sink, qkv, qkv, qkv, qkv, qkv, qkv, qkv, qkv, qkv)


MOE_ROWS = 256


def _row_scatter_kernel(dest_ref, pad_ref, f_ref, xs_hbm, zero_ref, fbuf, sem, *, tm, n_pad):
    step = pl.program_id(0)
    base = step * tm
    buf = step % 2

    def wait_tile(b, times):
        for _ in range(times):
            pltpu.make_async_copy(fbuf.at[b], xs_hbm.at[pl.ds(0, tm)], sem.at[b]).wait()

    @pl.when(step == 0)
    def _():
        zero_ref[...] = jnp.zeros_like(zero_ref)

        def zero_row(r, carry):
            pltpu.make_async_copy(zero_ref, xs_hbm.at[pl.ds(pad_ref[r], 1)], sem.at[1]).start()
            return carry

        lax.fori_loop(0, n_pad, zero_row, 0, unroll=8)
        wait_tile(1, n_pad // tm)

    fbuf[buf] = f_ref[...]

    def issue(r, carry):
        slot = 2 * (base + r)
        pltpu.make_async_copy(fbuf.at[buf, pl.ds(r, 1)], xs_hbm.at[pl.ds(dest_ref[slot], 1)], sem.at[buf]).start()
        pltpu.make_async_copy(fbuf.at[buf, pl.ds(r, 1)], xs_hbm.at[pl.ds(dest_ref[slot + 1], 1)],
                              sem.at[buf]).start()
        return carry

    lax.fori_loop(0, tm, issue, 0, unroll=8)

    @pl.when(step > 0)
    def _():
        wait_tile(1 - buf, 2)

    @pl.when(step == pl.num_programs(0) - 1)
    def _():
        wait_tile(buf, 2)


def _row_scatter(f, dest, pad_rows, n_rows):
    n_tok, d = f.shape
    tm = TOK_TILE
    n_pad = pad_rows.shape[0]
    assert n_pad % tm == 0 and 2 * n_tok + n_pad == n_rows
    return pl.pallas_call(
        functools.partial(_row_scatter_kernel, tm=tm, n_pad=n_pad),
        grid_spec=pltpu.PrefetchScalarGridSpec(
            num_scalar_prefetch=2,
            grid=(n_tok // tm,),
            in_specs=[pl.BlockSpec((tm, d), lambda i, dr, pr: (i, 0))],
            out_specs=pl.BlockSpec(memory_space=pl.ANY),
            scratch_shapes=[pltpu.VMEM((1, d), f.dtype), pltpu.VMEM((2, tm, d), f.dtype),
                            pltpu.SemaphoreType.DMA((2,))]),
        out_shape=jax.ShapeDtypeStruct((n_rows, d), f.dtype),
        compiler_params=_cparams(("arbitrary",)),
        name="moe_scatter",
    )(dest, pad_rows, f)


def _expert_kernel(be_ref, x_ref, wg_ref, wu_ref, wd_ref, o_ref):
    h = _bf(x_ref[...])
    act = _silu(jnp.dot(h, wg_ref[0], preferred_element_type=F32)) * jnp.dot(h, wu_ref[0],
                                                                              preferred_element_type=F32)
    o_ref[...] = _dot(act, wd_ref[0])


def _experts(xs, block_e, wg, wu, wd):
    n_rows, d = xs.shape
    ff = wg.shape[2]
    rows = MOE_ROWS
    return pl.pallas_call(
        _expert_kernel,
        grid_spec=pltpu.PrefetchScalarGridSpec(
            num_scalar_prefetch=1,
            grid=(n_rows // rows,),
            in_specs=[pl.BlockSpec((rows, d), lambda i, be: (i, 0)),
                      pl.BlockSpec((1, d, ff), lambda i, be: (be[i], 0, 0)),
                      pl.BlockSpec((1, d, ff), lambda i, be: (be[i], 0, 0)),
                      pl.BlockSpec((1, ff, d), lambda i, be: (be[i], 0, 0))],
            out_specs=pl.BlockSpec((rows, d), lambda i, be: (i, 0))),
        out_shape=jax.ShapeDtypeStruct((n_rows, d), F32),
        compiler_params=_cparams(("arbitrary",)),
        name="moe_experts",
    )(block_e, xs, wg, wu, wd)


def _combine_kernel(dest_ref, x_ref, modl_ref, modc_ref, w_ref, gfin_ref, ys_hbm, o_ref, y1_ref, y2_ref, sem, *,
                    tm, tiles_per_row):
    tile = pl.program_id(0) * tiles_per_row + pl.program_id(1)
    n_tiles = pl.num_programs(0) * tiles_per_row
    buf = tile % 2

    def fetch(t, b):
        def issue(r, carry):
            slot = 2 * (t * tm + r)
            pltpu.make_async_copy(ys_hbm.at[pl.ds(dest_ref[slot], 1)], y1_ref.at[b, pl.ds(r, 1)], sem.at[b]).start()
            pltpu.make_async_copy(ys_hbm.at[pl.ds(dest_ref[slot + 1], 1)], y2_ref.at[b, pl.ds(r, 1)],
                                  sem.at[b]).start()
            return carry

        lax.fori_loop(0, tm, issue, 0, unroll=8)

    @pl.when(tile == 0)
    def _():
        fetch(0, 0)

    @pl.when(tile + 1 < n_tiles)
    def _():
        fetch(tile + 1, 1 - buf)

    pltpu.make_async_copy(ys_hbm.at[pl.ds(0, tm)], y1_ref.at[buf], sem.at[buf]).wait()
    pltpu.make_async_copy(ys_hbm.at[pl.ds(0, tm)], y2_ref.at[buf], sem.at[buf]).wait()
    w = w_ref[0]
    moe = w[:, 0:1] * y1_ref[buf] + w[:, 1:2] * y2_ref[buf]
    y = x_ref[0] + _gate_vec(modl_ref, modc_ref, 5, 0, tm, 0) * moe
    o_ref[0] = y * lax.rsqrt(jnp.mean(y * y, axis=-1, keepdims=True) + NORM_EPS) * gfin_ref[...]


def _combine(x, mods, tokw, dest, ys, final_g):
    b, s, d = x.shape
    tm = TOK_TILE
    tiles = s // tm
    return pl.pallas_call(
        functools.partial(_combine_kernel, tm=tm, tiles_per_row=tiles),
        grid_spec=pltpu.PrefetchScalarGridSpec(
            num_scalar_prefetch=1,
            grid=(b, tiles),
            in_specs=[pl.BlockSpec((1, tm, d), lambda i, j, dr: (i, j, 0)),
                      pl.BlockSpec((1, N_MOD, d), lambda i, j, dr: (i, 0, 0)),
                      pl.BlockSpec((1, N_MOD, d), lambda i, j, dr: (0, 0, 0)),
                      pl.BlockSpec((1, tm, LANES), lambda i, j, dr: (i, j, 0)),
                      pl.BlockSpec((1, d), lambda i, j, dr: (0, 0)),
                      pl.BlockSpec(memory_space=pl.ANY)],
            out_specs=pl.BlockSpec((1, tm, d), lambda i, j, dr: (i, j, 0)),
            scratch_shapes=[pltpu.VMEM((2, tm, d), F32), pltpu.VMEM((2, tm, d), F32),
                            pltpu.SemaphoreType.DMA((2,))]),
        out_shape=jax.ShapeDtypeStruct((b, s, d), F32),
        compiler_params=_cparams(("arbitrary", "arbitrary")),
        name="moe_combine",
    )(dest, x, mods, mods, tokw, final_g.reshape(1, d), ys)


def _moe_routing(eidx):
    m = eidx.shape[0]
    oh = (eidx[:, None] == jnp.arange(N_EXPERTS, dtype=jnp.int32)[None, :]).astype(jnp.int32)
    csum = jnp.cumsum(oh, axis=0)
    rank = jnp.sum((csum - oh) * oh, axis=1)
    counts = csum[-1]
    padded = (counts + MOE_ROWS - 1) // MOE_ROWS * MOE_ROWS
    pad_end = jnp.cumsum(padded)
    dest = jnp.sum(oh * (pad_end - padded)[None, :], axis=1) + rank
    n_blocks = m // MOE_ROWS + N_EXPERTS
    starts = jnp.arange(n_blocks, dtype=jnp.int32) * MOE_ROWS
    block_e = jnp.minimum(jnp.sum((starts[:, None] >= pad_end[None, :]).astype(jnp.int32), axis=1), N_EXPERTS - 1)
    pad_cnt = padded - counts
    pad_cum = jnp.cumsum(pad_cnt)
    idx = jnp.arange(N_EXPERTS * MOE_ROWS, dtype=jnp.int32)
    owner = jnp.sum((idx[:, None] >= pad_cum[None, :]).astype(jnp.int32), axis=1)
    own = (jnp.minimum(owner, N_EXPERTS - 1)[:, None] == jnp.arange(N_EXPERTS)[None, :]).astype(jnp.int32)
    in_expert = jnp.sum(own * (pad_end - pad_cnt)[None, :], axis=1) + idx - jnp.sum(
        own * (pad_cum - pad_cnt)[None, :], axis=1)
    pad_rows = jnp.where(owner < N_EXPERTS, in_expert, pad_end[-1] + idx - pad_cum[-1])
    return dest.astype(jnp.int32), block_e, pad_rows.astype(jnp.int32), n_blocks * MOE_ROWS


def _rope_tables(n_lat, ctx_len):
    rows = n_lat // GRID_W
    row = jnp.repeat(jnp.arange(rows, dtype=F32), GRID_W)
    col = jnp.tile(jnp.arange(GRID_W, dtype=F32), rows)
    inv = ROPE_BASE ** (-jnp.arange(0, AX_DIM, 2, dtype=F32) / AX_DIM)
    ar, ac = row[:, None] * inv, col[:, None] * inv
    cos = jnp.concatenate([jnp.cos(ar), jnp.cos(ar), jnp.cos(ac), jnp.cos(ac)], axis=-1)
    sin = jnp.concatenate([-jnp.sin(ar), jnp.sin(ar), -jnp.sin(ac), jnp.sin(ac)], axis=-1)
    n_heads = ROPE_COLS // HD
    cos = jnp.concatenate([jnp.ones((ctx_len, HD), F32), cos], axis=0)
    sin = jnp.concatenate([jnp.zeros((ctx_len, HD), F32), sin], axis=0)
    half = AX_DIM // 2
    j = jnp.arange(HD)
    partner = jnp.where((j % AX_DIM) < half, j + half, j - half)
    perm = (jnp.arange(n_heads)[:, None] * HD + partner[None, :]).reshape(-1)
    return jnp.tile(cos, (1, n_heads)), jnp.tile(sin, (1, n_heads)), perm


def kernel(x, c, ctx, c_ctx, mod_w, mod_b, norm_mix, norm_ffn, norm_final, rec_w_in, rec_w_out, rwkv_mu, rwkv_w0, rwkv_w_up, rwkv_a0, rwkv_a_up, rwkv_g_up, rwkv_k_k, rwkv_k_a, rwkv_r_k, rwkv_ln_w, rwkv_ln_b, hgrn_lb, hgrn_norm, ffn_w_gate, ffn_w_up, ffn_w_down, att_w_in, att_w_out, att_sink, moe_router, moe_router_b, moe_w_gate, moe_w_up, moe_w_down):
    bsz, n_lat, d = x.shape
    ctx_len = ctx.shape[1]

    n_rows = -(-(bsz + 1) // 8) * 8
    cvec = jnp.zeros((n_rows, d), F32).at[:bsz].set(c).at[bsz].set(c_ctx)
    mods = [_adaln(cvec, mod_w[l], mod_b[l]).reshape(n_rows, N_MOD, d) for l in range(2)]

    ops, vv, g_bonus, gam, p_hgrn = _rec_in(ctx, x, mods[0], bsz, norm_mix[0], _bf(rec_w_in[0]), rwkv_mu[0],
                                            rwkv_w0[0], rwkv_w_up[0], rwkv_a0[0], rwkv_a_up[0], rwkv_g_up[0],
                                            rwkv_k_k[0], rwkv_k_a[0], rwkv_r_k[0].reshape(-1))
    oa = _rwkv_scan(ops, vv, gam, ctx_len)
    lb = jnp.cumsum(jax.nn.softmax(hgrn_lb.astype(F32), axis=1), axis=1)[:, 0].reshape(2, 1, B_W)
    ob = _hgrn_scan(p_hgrn, lb, ctx_len)
    xcat = _rec_out(ctx, x, mods[0], bsz, oa, g_bonus, ob, p_hgrn, rwkv_ln_w[0], rwkv_ln_b[0], hgrn_norm[0],
                    _bf(rec_w_out[0]), norm_ffn[0], _bf(ffn_w_gate[0]), _bf(ffn_w_up[0]), _bf(ffn_w_down[0]))

    cos, sin, perm = _rope_tables(n_lat, ctx_len)
    w_att = att_w_in[0]
    qkv = _proj_rope(xcat, mods[1], bsz, norm_mix[1], _bf(w_att), _bf(w_att[:, perm]), cos, sin, ctx_len)
    att = _attention(qkv, att_sink[0], ctx_len)
    x_lat, tokw, eidx, f_lat = _att_out(xcat, mods[1], att, _bf(att_w_out[0]), norm_ffn[1], moe_router[0],
                                        moe_router_b[0], x_skip=ctx_len)
    dest, block_e, pad_rows, n_rows = _moe_routing(eidx[..., :2].reshape(-1))
    xs = _row_scatter(f_lat.reshape(bsz * n_lat, d), dest, pad_rows, n_rows)
    ys = _experts(xs, block_e, _bf(moe_w_gate[0]), _bf(moe_w_up[0]), _bf(moe_w_down[0]))
    return _combine(x_lat, mods[1], tokw, dest, ys, norm_final)
```

```python
import functools
import math

import jax
import jax.numpy as jnp
from jax import lax
from jax.experimental import pallas as pl
from jax.experimental.pallas import tpu as pltpu

F32 = jnp.float32
BF16 = jnp.bfloat16

N_MOD = 6
NORM_EPS = 1e-6
NEG_INF = -1e30

A_HEADS = 8
A_HD = 64
A_W = A_HEADS * A_HD
DECAY_LORA = 64
AAA_LORA = 64
GATE_LORA = 128
RWKV_COLS = 3 * A_W + DECAY_LORA + AAA_LORA + GATE_LORA
GN_EPS = 64e-5

B_HEADS = 4
B_DK = 128
B_W = B_HEADS * B_DK
HGRN_COLS = 5 * B_W

HQ = 16
HKV = 4
GQ = HQ // HKV
HD = 64
WINDOW = 128
ATT_BLOCK = 128
ATT_STACK = GQ
AX_DIM = HD // 2
ROPE_BASE = 10000.0
GRID_W = 64
Q_COLS = HQ * HD
KV_COLS = HKV * HD
ROPE_COLS = Q_COLS + KV_COLS
ATT_COLS = Q_COLS + 2 * KV_COLS

N_EXPERTS = 8
LANES = 128
CHUNK = 64
SCAN_BLOCK = 256
MXU_WIDTH = 256
RWKV_GROUP = MXU_WIDTH // A_HD
TOK_TILE = 256
PROJ_TILE = 768
ATT_OUT_TILE = 512
VMEM_LIMIT = 56 * 1024 * 1024

OP_KAP, OP_RT, OP_KBAR, OP_BBAR, OP_KGAM, OP_BGAM = range(6)
N_OPS = 6


def _cparams(sem):
    return pltpu.CompilerParams(dimension_semantics=sem, vmem_limit_bytes=VMEM_LIMIT)


def _bf(x):
    return x.astype(BF16)


def _dot(a, b):
    return jnp.dot(_bf(a), _bf(b), preferred_element_type=F32)


def _dot_nt(a, b):
    return lax.dot_general(_bf(a), _bf(b), (((1,), (1,)), ((), ())), preferred_element_type=F32)


def _dot_tn(a, b):
    return lax.dot_general(_bf(a), _bf(b), (((0,), (0,)), ((), ())), preferred_element_type=F32)


def _dot_f32(a, b):
    a_hi, b_hi = _bf(a), _bf(b)
    a_lo, b_lo = _bf(a - a_hi.astype(F32)), _bf(b - b_hi.astype(F32))
    acc = jnp.dot(a_hi, b_hi, preferred_element_type=F32)
    acc += jnp.dot(a_hi, b_lo, preferred_element_type=F32)
    acc += jnp.dot(a_lo, b_hi, preferred_element_type=F32)
    return acc


def _sigmoid(x):
    return 0.5 * jnp.tanh(0.5 * x) + 0.5


def _silu(x):
    return x * _sigmoid(x)


def _segsum(x, ones_bd):
    hi = _bf(x)
    lo = _bf(x - hi.astype(F32))
    return jnp.dot(hi, ones_bd, preferred_element_type=F32) + jnp.dot(lo, ones_bd, preferred_element_type=F32)


def _dot_split(a, x):
    hi = _bf(x)
    lo = _bf(x - hi.astype(F32))
    return jnp.dot(a, hi, preferred_element_type=F32) + jnp.dot(a, lo, preferred_element_type=F32)


def _block_ones(width, seg):
    i = jnp.arange(width) // seg
    return (i[:, None] == i[None, :]).astype(BF16)


def _modulate(x, g, modl_ref, modc_ref, row, pos0, ctx_len, pos=None):
    y = x * lax.rsqrt(jnp.mean(x * x, axis=-1, keepdims=True) + NORM_EPS) * g
    if pos is None:
        pos = pos0 + lax.broadcasted_iota(jnp.int32, (x.shape[0], 1), 0)
    is_ctx = pos < ctx_len
    shift = jnp.where(is_ctx, modc_ref[0, row:row + 1, :], modl_ref[0, row:row + 1, :])
    scale = jnp.where(is_ctx, modc_ref[0, row + 1:row + 2, :], modl_ref[0, row + 1:row + 2, :])
    return y * (1.0 + scale) + shift


def _gate_vec(modl_ref, modc_ref, row, pos0, n, ctx_len):
    pos = pos0 + lax.broadcasted_iota(jnp.int32, (n, 1), 0)
    return jnp.where(pos < ctx_len, modc_ref[0, row:row + 1, :], modl_ref[0, row:row + 1, :])


def _mod_specs(d, ctx_row):
    return [pl.BlockSpec((1, N_MOD, d), lambda i, *_: (i, 0, 0)),
            pl.BlockSpec((1, N_MOD, d), lambda *_: (ctx_row, 0, 0))]


def _adaln_kernel(c_ref, w_ref, b_ref, o_ref):
    o_ref[...] = _dot(_silu(c_ref[...]), w_ref[...]) + b_ref[...]


def _adaln(cvec, w, b):
    r, d = cvec.shape
    n = w.shape[1]
    tn = 1024
    return pl.pallas_call(
        _adaln_kernel,
        grid=(n // tn,),
        in_specs=[pl.BlockSpec((r, d), lambda j: (0, 0)),
                  pl.BlockSpec((d, tn), lambda j: (0, j)),
                  pl.BlockSpec((1, tn), lambda j: (0, j))],
        out_specs=pl.BlockSpec((r, tn), lambda j: (0, j)),
        out_shape=jax.ShapeDtypeStruct((r, n), F32),
        compiler_params=_cparams(("parallel",)),
        name="adaln",
    )(cvec, w, b.reshape(1, n))


def _proj_rope_kernel(x_ref, modl_ref, modc_ref, g_ref, w_ref, wrot_ref, cos_ref, sin_ref, o_ref, *, ctx_len, tm,
                      rope_cols):
    pos0 = pl.program_id(1) * tm
    h = _bf(_modulate(x_ref[0], g_ref[...], modl_ref, modc_ref, 0, pos0, ctx_len))
    y = jnp.dot(h, w_ref[...], preferred_element_type=F32)
    yr = jnp.dot(h, wrot_ref[...], preferred_element_type=F32)
    o_ref[0, :, :rope_cols] = y[:, :rope_cols] * cos_ref[...] + yr * sin_ref[...]
    o_ref[0, :, rope_cols:] = y[:, rope_cols:]


def _proj_rope(xcat, mods, ctx_row, g, w, wrot, cos, sin, ctx_len):
    b, s, d = xcat.shape
    n = w.shape[1]
    tm = PROJ_TILE
    rope_cols = wrot.shape[1]
    return pl.pallas_call(
        functools.partial(_proj_rope_kernel, ctx_len=ctx_len, tm=tm, rope_cols=rope_cols),
        grid=(b, s // tm),
        in_specs=[pl.BlockSpec((1, tm, d), lambda i, j: (i, j, 0))] + _mod_specs(d, ctx_row) + [
            pl.BlockSpec((1, d), lambda i, j: (0, 0)),
            pl.BlockSpec((d, n), lambda i, j: (0, 0)),
            pl.BlockSpec((d, rope_cols), lambda i, j: (0, 0)),
            pl.BlockSpec((tm, rope_cols), lambda i, j: (j, 0)),
            pl.BlockSpec((tm, rope_cols), lambda i, j: (j, 0))],
        out_specs=pl.BlockSpec((1, tm, n), lambda i, j: (i, j, 0)),
        out_shape=jax.ShapeDtypeStruct((b, s, n), F32),
        compiler_params=_cparams(("parallel", "parallel")),
        name="proj_rope",
    )(xcat, mods, mods, g.reshape(1, d), w, wrot, cos, sin)


DECAY_SCALE = math.exp(-0.5)


def _stream_specs(tm, d, ctx_len):
    assert ctx_len == tm
    return [pl.BlockSpec((1, tm, d), lambda i, j: (i, 0, 0)),
            pl.BlockSpec((1, tm, d), lambda i, j: (i, jnp.maximum(j - 1, 0), 0))]


def _stream_tile(c_ref, x_ref):
    return jnp.where(pl.program_id(1) == 0, c_ref[0], x_ref[0])


def _rec_in_kernel(c_ref, x_ref, xp_ref, xn_ref, modl_ref, modc_ref, g_ref, w_ref, mu_ref, w0_ref, wup_ref, a0_ref,
                   aup_ref, gup_ref, kk_ref, ka_ref, rk_ref, ones_ref, ops_ref, v_ref, gb_ref, gam_ref, ph_ref, *,
                   ctx_len, seq_len, tm):
    pos0 = pl.program_id(1) * tm
    gvec = g_ref[...]
    h = _bf(_modulate(_stream_tile(c_ref, x_ref), gvec, modl_ref, modc_ref, 0, pos0, ctx_len))
    p = jnp.dot(h, w_ref[:, :RWKV_COLS], preferred_element_type=F32)

    hgrn_pieces = [(RWKV_COLS + c, min(B_W, HGRN_COLS - c)) for c in range(0, HGRN_COLS, B_W)]

    def hgrn_piece():
        if hgrn_pieces:
            lo, width = hgrn_pieces.pop(0)
            ph_ref[0, :, lo - RWKV_COLS:lo - RWKV_COLS + width] = jnp.dot(h, w_ref[:, lo:lo + width],
                                                                          preferred_element_type=F32)

    halo = lax.broadcasted_iota(jnp.int32, (16, 1), 0)
    halo_pos = jnp.where(halo < 8, pos0 - 8 + halo, pos0 + tm - 8 + halo)
    xh = jnp.concatenate([xp_ref[0], xn_ref[0]], axis=0)
    p_halo = jnp.dot(_bf(_modulate(xh, gvec, modl_ref, modc_ref, 0, 0, ctx_len, pos=halo_pos)),
                     w_ref[:, :RWKV_COLS], preferred_element_type=F32)
    rows = lax.broadcasted_iota(jnp.int32, (tm, 1), 0)
    starts_seq = (pos0 == 0) | (pos0 == ctx_len)
    ends_seq = (pos0 + tm == ctx_len) | (pos0 + tm == seq_len)
    prev_halo = jnp.where(starts_seq, 0.0, p_halo[7:8, :])
    next_halo = jnp.where(ends_seq, 0.0, p_halo[8:9, :])
    prev = jnp.where(rows == 0, prev_halo, pltpu.roll(p, 1, 0))
    nxt = jnp.where(rows == tm - 1, next_halo, pltpu.roll(p, tm - 1, 0))
    p = p + mu_ref[...] * (0.5 * (prev + nxt) - p)
    hgrn_piece()

    r = p[:, 0:A_W]
    k = p[:, A_W:2 * A_W]
    v = p[:, 2 * A_W:3 * A_W]
    lo = 3 * A_W
    wd = p[:, lo:lo + DECAY_LORA]
    ad = p[:, lo + DECAY_LORA:lo + DECAY_LORA + AAA_LORA]
    gd = p[:, lo + DECAY_LORA + AAA_LORA:lo + DECAY_LORA + AAA_LORA + GATE_LORA]

    tw = jnp.tanh(wd)
    a = _sigmoid(a0_ref[...] + _dot(ad, aup_ref[...]))
    ones_bd = ones_ref[...]
    kk = k * kk_ref[...]
    kk = kk / jnp.maximum(jnp.sqrt(_segsum(kk * kk, ones_bd)), 1e-12)
    k = k * (1.0 + (a - 1.0) * ka_ref[...])
    b = kk * a
    v_ref[0] = _bf(v)
    gb_ref[0, :, :A_W] = _dot(_sigmoid(gd), gup_ref[...])
    gb_ref[0, :, A_W:] = _segsum(r * k * rk_ref[...], ones_bd) * v
    hgrn_piece()

    trow = lax.broadcasted_iota(jnp.int32, (tm, tm), 0)
    tcol = lax.broadcasted_iota(jnp.int32, (tm, tm), 1)
    same = (trow // CHUNK) == (tcol // CHUNK)
    n_chunks = tm // CHUNK
    for d in range(2):
        lw = -DECAY_SCALE * _sigmoid(w0_ref[d:d + 1, :] + _dot(tw, wup_ref[d]))
        before = (tcol <= trow) if d == 0 else (tcol >= trow)
        g_incl = _dot_split(jnp.where(same & before, 1.0, 0.0).astype(BF16), lw)
        last = [c * CHUNK + (CHUNK - 1 if d == 0 else 0) for c in range(n_chunks)]
        gam = jnp.exp(jnp.concatenate([g_incl[t:t + 1] for t in last], axis=0))
        gam_ref[0, d, 0] = gam
        hgrn_piece()
        e_ng = jnp.exp(-g_incl)
        e_tail = e_ng * jnp.concatenate([jnp.broadcast_to(gam[c:c + 1], (CHUNK, A_W)) for c in range(n_chunks)],
                                        axis=0)
        operands = {OP_KAP: kk * jnp.exp(g_incl - lw),
                    OP_RT: r * jnp.exp(g_incl),
                    OP_KBAR: k * e_ng, OP_BBAR: b * e_ng,
                    OP_KGAM: k * e_tail, OP_BGAM: b * e_tail}
        for sec, val in operands.items():
            ops_ref[0, d, :, sec * A_W:(sec + 1) * A_W] = _bf(val)
        hgrn_piece()
    while hgrn_pieces:
        hgrn_piece()


def _rec_in(ctx, x, mods, ctx_row, g, w, mu, w0, w_up, a0, a_up, g_up, k_k, k_a, r_k):
    b, n_lat, d = x.shape
    ctx_len = ctx.shape[1]
    s = ctx_len + n_lat
    tm = SCAN_BLOCK
    assert n_lat % tm == 0
    nb8 = n_lat // 8
    row = lambda a: a.reshape(1, -1)
    full = lambda a: pl.BlockSpec(a.shape, lambda i, j: (0,) * a.ndim)
    args = [row(g), w, row(mu), w0, w_up, row(a0), a_up, g_up, row(k_k), row(k_a), row(r_k),
            _block_ones(A_W, A_HD)]
    return pl.pallas_call(
        functools.partial(_rec_in_kernel, ctx_len=ctx_len, seq_len=s, tm=tm),
        grid=(b, s // tm),
        in_specs=_stream_specs(tm, d, ctx_len) + [
            pl.BlockSpec((1, 8, d), lambda i, j: (i, jnp.maximum((j - 1) * (tm // 8) - 1, 0), 0)),
            pl.BlockSpec((1, 8, d), lambda i, j: (i, jnp.minimum(j * (tm // 8), nb8 - 1), 0)),
        ] + _mod_specs(d, ctx_row) + [full(a) for a in args],
        out_specs=[pl.BlockSpec((1, 2, tm, N_OPS * A_W), lambda i, j: (i, 0, j, 0)),
                   pl.BlockSpec((1, tm, A_W), lambda i, j: (i, j, 0)),
                   pl.BlockSpec((1, tm, 2 * A_W), lambda i, j: (i, j, 0)),
                   pl.BlockSpec((1, 2, 1, tm // CHUNK, A_W), lambda i, j: (i, 0, j, 0, 0)),
                   pl.BlockSpec((1, tm, HGRN_COLS), lambda i, j: (i, j, 0))],
        out_shape=[jax.ShapeDtypeStruct((b, 2, s, N_OPS * A_W), BF16),
                   jax.ShapeDtypeStruct((b, s, A_W), BF16),
                   jax.ShapeDtypeStruct((b, s, 2 * A_W), F32),
                   jax.ShapeDtypeStruct((b, 2, s // tm, tm // CHUNK, A_W), F32),
                   jax.ShapeDtypeStruct((b, s, HGRN_COLS), F32)],
        compiler_params=_cparams(("parallel", "parallel")),
        name="rec_in",
    )(ctx, x, x, x, mods, mods, *args)


def _chunk_order(d, j, n_ctx_chunks, n_chunks):
    back = jnp.where(j < n_ctx_chunks, n_ctx_chunks - 1 - j, n_chunks - 1 + n_ctx_chunks - j)
    return jnp.where(d == 0, j, back)


def _incl_mask(rev):
    row = lax.broadcasted_iota(jnp.int32, (CHUNK, CHUNK), 0)
    col = lax.broadcasted_iota(jnp.int32, (CHUNK, CHUNK), 1)
    return jnp.where(rev, row - col, col - row) <= 0


def _scan_step_coords(t, n_steps):
    return jnp.minimum(t, n_steps - 1), jnp.maximum(t - 1, 0)


def _rwkv_chunk_kernel(ops_ref, v_ref, gam_ref, o_ref, s_ref, tr_ref, uu_ref, ol_ref, ab_ref,
                       m2_ref, cc_ref, gm_ref, *, n_blocks, n_steps):
    step = pl.program_id(0)
    t_in, t_out = _scan_step_coords(step, n_steps)
    rev = (t_in // n_blocks) % 2 == 1
    rev_out = (t_out // n_blocks) % 2 == 1
    first_out = t_out % n_blocks == 0
    w_slot = step % 2
    r_slot = 1 - w_slot

    @pl.when(step == 0)
    def _():
        s_ref[...] = jnp.zeros_like(s_ref)
        for ref in (tr_ref, uu_ref, ol_ref, ab_ref, m2_ref, cc_ref, gm_ref):
            ref[1] = jnp.zeros(ref.shape[1:], ref.dtype)

    gw = RWKV_GROUP * A_HD
    groups = range(A_HEADS // RWKV_GROUP)
    gsl = [slice(g * gw, (g + 1) * gw) for g in groups]
    row = lax.broadcasted_iota(jnp.int32, (CHUNK, gw), 0)
    col = lax.broadcasted_iota(jnp.int32, (CHUNK, gw), 1) % A_HD
    diff = jnp.where(rev, row - col, col - row)
    strict_c = diff < 0
    incl_c = diff <= 0
    eye_c = jnp.where(diff == 0, 1.0, 0.0)
    bd_mask = (lax.broadcasted_iota(jnp.int32, (gw, gw), 0) // A_HD
               == lax.broadcasted_iota(jnp.int32, (gw, gw), 1) // A_HD)

    def bd(x):
        return jnp.where(bd_mask, jnp.concatenate([x] * RWKV_GROUP, axis=0), jnp.zeros((), x.dtype))

    def stack(x):
        return jnp.concatenate([x[:, h * A_HD:(h + 1) * A_HD] for h in range(RWKV_GROUP)], axis=0)

    n = SCAN_BLOCK // CHUNK
    rows = [pl.ds(pl.multiple_of(jnp.where(rev, (n - 1 - i) * CHUNK, i * CHUNK), CHUNK), CHUNK) for i in range(n)]
    rows_out = [pl.ds(pl.multiple_of(jnp.where(rev_out, (n - 1 - i) * CHUNK, i * CHUNK), CHUNK), CHUNK)
                for i in range(n)]
    units = [(i, g) for i in range(n) for g in groups]

    s = [jnp.where(first_out, 0.0, s_ref[g]) for g in groups]
    zr = {}
    pieces = []

    def advance(i):
        for g in groups:
            ui = i * len(groups) + g
            zr[i, g] = _dot_nt(tr_ref[r_slot, ui], bd(s[g]))
            s[g] = s[g] * gm_ref[r_slot, ui] - jnp.dot(_bf(s[g]), m2_ref[r_slot, ui],
                                                       preferred_element_type=F32) + cc_ref[r_slot, ui]

    def emit_out(i):
        for g in groups:
            ui = i * len(groups) + g
            z = zr[i, g][:CHUNK] + uu_ref[r_slot, ui]
            o_ref[0, 0, rows_out[i], gsl[g]] = (zr[i, g][CHUNK:] + ol_ref[r_slot, ui]
                                                - _dot(ab_ref[r_slot, ui], bd(z)))

    for i in range(n):
        pieces += [functools.partial(advance, i), functools.partial(emit_out, i)]

    def state_piece():
        if pieces:
            pieces.pop(0)()

    def operand(sec, u):
        i, g = u
        return ops_ref[0, 0, rows[i], sec * A_W + g * gw:sec * A_W + (g + 1) * gw]

    kap = {u: operand(OP_KAP, u) for u in units}
    rt = {u: operand(OP_RT, u) for u in units}
    kbar = {u: operand(OP_KBAR, u) for u in units}
    bbar = {u: operand(OP_BBAR, u) for u in units}
    kgam = {u: operand(OP_KGAM, u) for u in units}
    bgam = {u: operand(OP_BGAM, u) for u in units}
    v = {(i, g): v_ref[0, rows[i], gsl[g]] for i, g in units}
    chunk_of = [jnp.where(rev, n - 1 - i, i) for i in range(n)]
    gam = {(i, g): gam_ref[0, 0, 0, pl.ds(chunk_of[i], 1), gsl[g]] for i, g in units}

    x = {u: jnp.concatenate([kap[u], rt[u]], axis=0) for u in units}
    yb = {u: _dot_nt(x[u], bd(bbar[u])) for u in units}
    state_piece()
    a = {u: jnp.where(strict_c, -yb[u][:CHUNK], 0.0) for u in units}
    xs = {u: eye_c + a[u] for u in units}
    pw = {u: _dot(a[u], bd(a[u])) for u in units}
    state_piece()
    for _ in range(4):
        st = {u: _dot(jnp.concatenate([pw[u], xs[u]], axis=0), bd(pw[u])) for u in units}
        pw = {u: st[u][:CHUNK] for u in units}
        xs = {u: xs[u] + st[u][CHUNK:] for u in units}
        state_piece()
    t_inv = {u: xs[u] + _dot(xs[u], bd(pw[u])) for u in units}
    state_piece()
    yk = {u: _dot_nt(x[u], bd(kbar[u])) for u in units}
    ykm = {u: jnp.concatenate([jnp.where(strict_c, yk[u][:CHUNK], 0.0), jnp.where(incl_c, yk[u][CHUNK:], 0.0)],
                              axis=0) for u in units}
    wo = {u: _dot(ykm[u], bd(v[u])) for u in units}
    w1 = {u: wo[u][:CHUNK] for u in units}
    o_loc = {u: wo[u][CHUNK:] for u in units}
    state_piece()
    tk = {u: _dot(t_inv[u], bd(kap[u])) for u in units}
    uu = {u: _dot(t_inv[u], bd(w1[u])) for u in units}
    while pieces:
        state_piece()
    for g in groups:
        s_ref[g] = s[g]
    a_rb = {u: jnp.where(incl_c, yb[u][CHUNK:], 0.0) for u in units}
    tu = {u: _dot_tn(jnp.concatenate([stack(tk[u]), stack(uu[u])], axis=1), bd(bgam[u])) for u in units}
    bd_m2 = {u: _bf(bd(tu[u][:A_HD])) for u in units}
    cc = {u: _dot_tn(stack(v[u]), bd(kgam[u])) - tu[u][A_HD:] for u in units}
    for u in units:
        ui = u[0] * len(groups) + u[1]
        tr_ref[w_slot, ui] = jnp.concatenate([_bf(tk[u]), rt[u]], axis=0)
        uu_ref[w_slot, ui] = uu[u]
        ol_ref[w_slot, ui] = o_loc[u]
        ab_ref[w_slot, ui] = _bf(a_rb[u])
        m2_ref[w_slot, ui] = bd_m2[u]
        cc_ref[w_slot, ui] = cc[u]
        gm_ref[w_slot, ui] = gam[u]


def _rwkv_scan(ops, vv, gam, ctx_len):
    b, _, s, _ = ops.shape
    nc, ncc = s // SCAN_BLOCK, ctx_len // SCAN_BLOCK

    n_steps = b * 2 * nc

    def coords(flat):
        d = (flat // nc) % 2
        return flat // (2 * nc), d, _chunk_order(d, flat % nc, ncc, nc)

    def coords_in(t):
        return coords(_scan_step_coords(t, n_steps)[0])

    def out_map(t):
        bi, d, blk = coords(_scan_step_coords(t, n_steps)[1])
        return bi, d, blk, 0

    def ops_map(t):
        bi, d, blk = coords_in(t)
        return bi, d, blk, 0

    def v_map(t):
        bi, _, blk = coords_in(t)
        return bi, blk, 0

    def gam_map(t):
        bi, d, blk = coords_in(t)
        return bi, d, blk, 0, 0

    n_groups = A_HEADS // RWKV_GROUP
    n_chunks = SCAN_BLOCK // CHUNK
    n_units = n_chunks * n_groups
    gw = RWKV_GROUP * A_HD
    return pl.pallas_call(
        functools.partial(_rwkv_chunk_kernel, n_blocks=nc, n_steps=n_steps),
        grid=(n_steps + 1,),
        in_specs=[pl.BlockSpec((1, 1, SCAN_BLOCK, N_OPS * A_W), ops_map),
                  pl.BlockSpec((1, SCAN_BLOCK, A_W), v_map),
                  pl.BlockSpec((1, 1, 1, n_chunks, A_W), gam_map)],
        out_specs=pl.BlockSpec((1, 1, SCAN_BLOCK, A_W), out_map),
        out_shape=jax.ShapeDtypeStruct((b, 2, s, A_W), F32),
        scratch_shapes=[pltpu.VMEM((n_groups, A_HD, gw), F32),
                        pltpu.VMEM((2, n_units, 2 * CHUNK, gw), BF16),
                        pltpu.VMEM((2, n_units, CHUNK, gw), F32),
                        pltpu.VMEM((2, n_units, CHUNK, gw), F32),
                        pltpu.VMEM((2, n_units, CHUNK, gw), BF16),
                        pltpu.VMEM((2, n_units, gw, gw), BF16),
                        pltpu.VMEM((2, n_units, CHUNK, gw), F32),
                        pltpu.VMEM((2, n_units, 1, gw), F32)],
        compiler_params=_cparams(("arbitrary",)),
        name="rwkv_scan",
    )(ops, vv, gam)


def _hgrn_chunk_kernel(q_ref, i_ref, f_ref, lb_ref, o_ref, s_ref):
    rev = pl.program_id(1) == 1

    @pl.when(pl.program_id(2) == 0)
    def _():
        s_ref[...] = jnp.zeros_like(s_ref)

    incl = _incl_mask(rev)
    incl_bf = jnp.where(incl, 1.0, 0.0).astype(BF16)
    lb = lb_ref[0]
    heads = range(B_HEADS)
    sls = [slice(h * B_DK, (h + 1) * B_DK) for h in heads]

    n = SCAN_BLOCK // CHUNK
    rows = [pl.ds(pl.multiple_of(jnp.where(rev, (n - 1 - i) * CHUNK, i * CHUNK), CHUNK), CHUNK) for i in range(n)]
    units = [(i, h) for i in range(n) for h in heads]
    q_in, k_in, q_st, k_tail, gam, v = ({} for _ in range(6))
    for i in range(n):
        f = lb + (1.0 - lb) * _sigmoid(f_ref[0, rows[i], :])
        logf = jnp.log(f)
        kf = 1.0 - f
        g_incl = _dot_split(incl_bf, logf)
        g_tot = jnp.sum(logf, axis=0, keepdims=True)
        g_mid = g_incl[CHUNK // 2 - 1:CHUNK // 2, :]
        q = _silu(q_ref[0, rows[i], :])
        v_i = i_ref[0, rows[i], :]
        q_in_i = q * jnp.exp(g_incl - g_mid)
        k_in_i = kf * jnp.exp(g_mid - g_incl)
        q_st_i = q_in_i * jnp.exp(g_mid)
        k_tail_i = k_in_i * jnp.exp(g_tot - g_mid)
        for h in heads:
            q_in[i, h], k_in[i, h] = q_in_i[:, sls[h]], k_in_i[:, sls[h]]
            q_st[i, h], k_tail[i, h] = q_st_i[:, sls[h]], k_tail_i[:, sls[h]]
            gam[i, h] = jnp.exp(g_tot)[:, sls[h]]
            v[i, h] = v_i[:, sls[h]]
    att = {u: jnp.where(incl, _dot_nt(q_in[u], k_in[u]), 0.0) for u in units}
    o_loc = {u: _dot(att[u], v[u]) for u in units}
    kv = {u: _dot_tn(v[u], k_tail[u]) for u in units}

    s = [s_ref[h] for h in heads]
    for i in range(n):
        for h in heads:
            o_ref[0, 0, rows[i], sls[h]] = o_loc[i, h] + _dot_nt(q_st[i, h], s[h])
            s[h] = s[h] * gam[i, h] + kv[i, h]
    for h in heads:
        s_ref[h] = s[h]


def _hgrn_scan(p, lb, ctx_len):
    b, s, _ = p.shape
    nc, ncc = s // SCAN_BLOCK, ctx_len // SCAN_BLOCK

    def sec(idx):
        return pl.BlockSpec((1, SCAN_BLOCK, B_W), lambda i, d, j: (i, _chunk_order(d, j, ncc, nc), idx))

    return pl.pallas_call(
        _hgrn_chunk_kernel,
        grid=(b, 2, nc),
        in_specs=[sec(0), sec(1),
                  pl.BlockSpec((1, SCAN_BLOCK, B_W), lambda i, d, j: (i, _chunk_order(d, j, ncc, nc), 2 + d)),
                  pl.BlockSpec((1, 1, B_W), lambda i, d, j: (d, 0, 0))],
        out_specs=pl.BlockSpec((1, 1, SCAN_BLOCK, B_W), lambda i, d, j: (i, d, _chunk_order(d, j, ncc, nc), 0)),
        out_shape=jax.ShapeDtypeStruct((b, 2, s, B_W), F32),
        scratch_shapes=[pltpu.VMEM((B_HEADS, B_DK, B_DK), F32)],
        compiler_params=_cparams(("parallel", "parallel", "arbitrary")),
        name="hgrn_scan",
    )(p, p, p, lb)


def _rec_out_kernel(c_ref, x_ref, modl_ref, modc_ref, oa_ref, g_ref, bonus_ref, ob_ref, gate_ref, lnw_ref, lnb_ref,
                    hgn_ref, ones_a_ref, ones_b_ref, w_ref, gffn_ref, wg_ref, wu_ref, wd_ref, o_ref, *, ctx_len, tm):
    pos0 = pl.program_id(1) * tm
    oa = oa_ref[0, 0] + oa_ref[0, 1]
    ones_a = ones_a_ref[...]
    mean = _segsum(oa, ones_a) * (1.0 / A_HD)
    cen = oa - mean
    var = _segsum(cen * cen, ones_a) * (1.0 / A_HD)
    ya = (cen * lax.rsqrt(var + GN_EPS) * lnw_ref[...] + lnb_ref[...] + bonus_ref[0]) * g_ref[0]
    ob = ob_ref[0, 0] + ob_ref[0, 1]
    ms = _segsum(ob * ob, ones_b_ref[...]) * (1.0 / B_DK)
    yb = ob * lax.rsqrt(ms + NORM_EPS) * hgn_ref[...] * _silu(gate_ref[0])
    y = _dot(ya, w_ref[:A_W, :]) + _dot(yb, w_ref[A_W:, :])
    gate = _gate_vec(modl_ref, modc_ref, 2, pos0, tm, ctx_len)
    x1 = _stream_tile(c_ref, x_ref) + gate * y
    h = _bf(_modulate(x1, gffn_ref[...], modl_ref, modc_ref, 3, pos0, ctx_len))
    act = _silu(jnp.dot(h, wg_ref[...], preferred_element_type=F32)) * jnp.dot(h, wu_ref[...],
                                                                               preferred_element_type=F32)
    o_ref[0] = x1 + _gate_vec(modl_ref, modc_ref, 5, pos0, tm, ctx_len) * _dot(act, wd_ref[...])


def _rec_out(ctx, x, mods, ctx_row, oa, g_bonus, ob, p_hgrn, ln_w, ln_b, hg_norm, w_out, g_ffn, wg, wu, wd):
    b, n_lat, d = x.shape
    ctx_len = ctx.shape[1]
    s = ctx_len + n_lat
    tm = TOK_TILE
    row = lambda a: a.reshape(1, -1)
    full = lambda a: pl.BlockSpec(a.shape, lambda i, j: (0,) * a.ndim, pipeline_mode=pl.Buffered(1))
    consts = [row(ln_w), row(ln_b), row(jnp.tile(hg_norm, B_HEADS)), _block_ones(A_W, A_HD),
              _block_ones(B_W, B_DK), w_out, row(g_ffn), wg, wu, wd]
    return pl.pallas_call(
        functools.partial(_rec_out_kernel, ctx_len=ctx_len, tm=tm),
        grid=(b, s // tm),
        in_specs=_stream_specs(tm, d, ctx_len) + _mod_specs(d, ctx_row) + [
            pl.BlockSpec((1, 2, tm, A_W), lambda i, j: (i, 0, j, 0)),
            pl.BlockSpec((1, tm, A_W), lambda i, j: (i, j, 0)),
            pl.BlockSpec((1, tm, A_W), lambda i, j: (i, j, 1)),
            pl.BlockSpec((1, 2, tm, B_W), lambda i, j: (i, 0, j, 0)),
            pl.BlockSpec((1, tm, B_W), lambda i, j: (i, j, 4)),
        ] + [full(a) for a in consts],
        out_specs=pl.BlockSpec((1, tm, d), lambda i, j: (i, j, 0)),
        out_shape=jax.ShapeDtypeStruct((b, s, d), F32),
        compiler_params=_cparams(("parallel", "parallel")),
        name="rec_out",
    )(ctx, x, mods, mods, oa, g_bonus, g_bonus, ob, p_hgrn, *consts)


def _att_out_kernel(*refs, tm, n_sub):
    x_parts = refs[:n_sub]
    modl_ref, modc_ref, y_ref, w_ref, g_ref, rw_ref, rb_ref, o_ref, tw_ref, e_ref, f_ref = refs[n_sub:]
    x_in = jnp.concatenate([r[0] for r in x_parts], axis=0)
    x = x_in + _gate_vec(modl_ref, modc_ref, 2, 0, tm, 0) * _dot(y_ref[0], w_ref[...])
    o_ref[0] = x
    f = _modulate(x, g_ref[...], modl_ref, modc_ref, 3, 0, 0)
    f_ref[0] = f
    logits = _dot_f32(f, rw_ref[...]) + rb_ref[...]
    lane = lax.broadcasted_iota(jnp.int32, logits.shape, 1).astype(F32)
    logits = jnp.where(lane < N_EXPERTS, logits, NEG_INF)
    ex = jnp.exp(logits - jnp.max(logits, axis=-1, keepdims=True))
    probs = ex / jnp.sum(ex, axis=-1, keepdims=True)
    p1 = jnp.max(probs, axis=-1, keepdims=True)
    i1 = jnp.min(jnp.where(probs == p1, lane, float(LANES)), axis=-1, keepdims=True)
    rest = jnp.where(lane == i1, -1.0, probs)
    p2 = jnp.max(rest, axis=-1, keepdims=True)
    i2 = jnp.min(jnp.where(rest == p2, lane, float(LANES)), axis=-1, keepdims=True)
    tot = p1 + p2
    tw_ref[0] = jnp.where(lane == 0.0, p1 / tot, jnp.where(lane == 1.0, p2 / tot, 0.0))
    e_ref[0] = jnp.where(lane == 0.0, i1, jnp.where(lane == 1.0, i2, 0.0)).astype(jnp.int32)


def _att_out(xcat, mods, y, w, g, router_w, router_b, x_skip):
    b, s, k = y.shape
    d = xcat.shape[-1]
    tm = ATT_OUT_TILE
    n_sub = tm // x_skip
    assert tm % x_skip == 0 and s % tm == 0
    wpad = jnp.zeros((d, LANES), F32).at[:, :N_EXPERTS].set(router_w)
    bpad = jnp.zeros((1, LANES), F32).at[0, :N_EXPERTS].set(router_b)
    tile_spec = pl.BlockSpec((1, tm, d), lambda i, j: (i, j, 0))
    lane_spec = pl.BlockSpec((1, tm, LANES), lambda i, j: (i, j, 0))
    x_specs = [pl.BlockSpec((1, x_skip, d), functools.partial(lambda i, j, p: (i, j * n_sub + 1 + p, 0), p=p))
               for p in range(n_sub)]
    return pl.pallas_call(
        functools.partial(_att_out_kernel, tm=tm, n_sub=n_sub),
        grid=(b, s // tm),
        in_specs=x_specs + _mod_specs(d, 0) + [
            pl.BlockSpec((1, tm, k), lambda i, j: (i, j, 0)),
            pl.BlockSpec((k, d), lambda i, j: (0, 0)),
            pl.BlockSpec((1, d), lambda i, j: (0, 0)),
            pl.BlockSpec((d, LANES), lambda i, j: (0, 0)),
            pl.BlockSpec((1, LANES), lambda i, j: (0, 0))],
        out_specs=[tile_spec, lane_spec, lane_spec, tile_spec],
        out_shape=[jax.ShapeDtypeStruct((b, s, d), F32), jax.ShapeDtypeStruct((b, s, LANES), F32),
                   jax.ShapeDtypeStruct((b, s, LANES), jnp.int32), jax.ShapeDtypeStruct((b, s, d), F32)],
        compiler_params=_cparams(("parallel", "parallel")),
        name="att_out_router",
    )(*([xcat] * n_sub), mods, mods, y, w, g.reshape(1, d), wpad, bpad)


def _attn_kernel(sink_ref, q_ref, kc_ref, vc_ref, kp_ref, kq_ref, kn_ref, vp_ref, vq_ref, vn_ref, o_ref, *,
                 ctx_len, n_lat):
    i = pl.program_id(1)
    nk = ctx_len + 3 * ATT_BLOCK
    k_all = jnp.concatenate([kc_ref[0], kp_ref[0], kq_ref[0], kn_ref[0]], axis=0)
    v_all = jnp.concatenate([vc_ref[0], vp_ref[0], vq_ref[0], vn_ref[0]], axis=0)
    row = lax.broadcasted_iota(jnp.int32, (ATT_BLOCK, nk), 0)
    col = lax.broadcasted_iota(jnp.int32, (ATT_BLOCK, nk), 1)
    rel = col - (ctx_len + ATT_BLOCK)
    kabs = i * ATT_BLOCK + rel
    band = (jnp.abs(row - rel) <= WINDOW) & (kabs >= 0) & (kabs < n_lat)
    bias = jnp.where((col < ctx_len) | band, 0.0, NEG_INF)
    bias = jnp.concatenate([bias] * ATT_STACK, axis=0)
    grp = lax.broadcasted_iota(jnp.int32, (ATT_STACK * ATT_BLOCK, 1), 0) // ATT_BLOCK
    scale = HD ** -0.5
    for hq0 in range(0, HQ, ATT_STACK):
        hk = hq0 // GQ
        kh = _bf(k_all[:, hk * HD:(hk + 1) * HD])
        vh = _bf(v_all[:, hk * HD:(hk + 1) * HD])
        heads = [hq0 + g for g in range(ATT_STACK)]
        q = jnp.concatenate([q_ref[0, :, hq * HD:(hq + 1) * HD] for hq in heads], axis=0) * scale
        sk = jnp.zeros((ATT_STACK * ATT_BLOCK, 1), F32)
        for g, hq in enumerate(heads):
            sk = jnp.where(grp == g, sink_ref[hq], sk)
        s = _dot_nt(q, kh) + bias
        m = jnp.maximum(jnp.max(s, axis=-1, keepdims=True), sk)
        p = jnp.exp(s - m)
        den = jnp.sum(p, axis=-1, keepdims=True) + jnp.exp(sk - m)
        o = jnp.dot(_bf(p), vh, preferred_element_type=F32) / den
        for g, hq in enumerate(heads):
            o_ref[0, :, hq * HD:(hq + 1) * HD] = o[g * ATT_BLOCK:(g + 1) * ATT_BLOCK]


def _attention(qkv, sink, ctx_len):
    b, s, _ = qkv.shape
    n_lat = s - ctx_len
    nb = n_lat // ATT_BLOCK
    cb = ctx_len // ATT_BLOCK
    kcol = Q_COLS // KV_COLS
    vcol = kcol + 1

    def band(colblk, shift):
        return pl.BlockSpec((1, ATT_BLOCK, KV_COLS),
                            lambda bi, i: (bi, cb + jnp.clip(i + shift, 0, nb - 1), colblk))

    return pl.pallas_call(
        functools.partial(_attn_kernel, ctx_len=ctx_len, n_lat=n_lat),
        grid=(b, nb),
        in_specs=[pl.BlockSpec(memory_space=pltpu.SMEM),
                  pl.BlockSpec((1, ATT_BLOCK, Q_COLS), lambda bi, i: (bi, cb + i, 0)),
                  pl.BlockSpec((1, ctx_len, KV_COLS), lambda bi, i: (bi, 0, kcol)),
                  pl.BlockSpec((1, ctx_len, KV_COLS), lambda bi, i: (bi, 0, vcol)),
                  band(kcol, -1), band(kcol, 0), band(kcol, 1),
                  band(vcol, -1), band(vcol, 0), band(vcol, 1)],
        out_specs=pl.BlockSpec((1, ATT_BLOCK, Q_COLS), lambda bi, i: (bi, i, 0)),
        out_shape=jax.ShapeDtypeStruct((b, n_lat, Q_COLS), F32),
        compiler_params=_cparams(("parallel", "parallel")),
        name="attention",
    )(sink, qkv, qkv, qkv, qkv, qkv, qkv, qkv, qkv, qkv)


MOE_ROWS = 256


def _row_scatter_kernel(dest_ref, pad_ref, f_ref, xs_hbm, zero_ref, fbuf, sem, *, tm, n_pad):
    step = pl.program_id(0)
    base = step * tm
    buf = step % 2

    def wait_tile(b, times):
        for _ in range(times):
            pltpu.make_async_copy(fbuf.at[b], xs_hbm.at[pl.ds(0, tm)], sem.at[b]).wait()

    @pl.when(step == 0)
    def _():
        zero_ref[...] = jnp.zeros_like(zero_ref)

        def zero_row(r, carry):
            pltpu.make_async_copy(zero_ref, xs_hbm.at[pl.ds(pad_ref[r], 1)], sem.at[1]).start()
            return carry

        lax.fori_loop(0, n_pad, zero_row, 0, unroll=8)
        wait_tile(1, n_pad // tm)

    fbuf[buf] = f_ref[...]

    def issue(r, carry):
        slot = 2 * (base + r)
        pltpu.make_async_copy(fbuf.at[buf, pl.ds(r, 1)], xs_hbm.at[pl.ds(dest_ref[slot], 1)], sem.at[buf]).start()
        pltpu.make_async_copy(fbuf.at[buf, pl.ds(r, 1)], xs_hbm.at[pl.ds(dest_ref[slot + 1], 1)],
                              sem.at[buf]).start()
        return carry

    lax.fori_loop(0, tm, issue, 0, unroll=8)

    @pl.when(step > 0)
    def _():
        wait_tile(1 - buf, 2)

    @pl.when(step == pl.num_programs(0) - 1)
    def _():
        wait_tile(buf, 2)


def _row_scatter(f, dest, pad_rows, n_rows):
    n_tok, d = f.shape
    tm = TOK_TILE
    n_pad = pad_rows.shape[0]
    assert n_pad % tm == 0 and 2 * n_tok + n_pad == n_rows
    return pl.pallas_call(
        functools.partial(_row_scatter_kernel, tm=tm, n_pad=n_pad),
        grid_spec=pltpu.PrefetchScalarGridSpec(
            num_scalar_prefetch=2,
            grid=(n_tok // tm,),
            in_specs=[pl.BlockSpec((tm, d), lambda i, dr, pr: (i, 0))],
            out_specs=pl.BlockSpec(memory_space=pl.ANY),
            scratch_shapes=[pltpu.VMEM((1, d), f.dtype), pltpu.VMEM((2, tm, d), f.dtype),
                            pltpu.SemaphoreType.DMA((2,))]),
        out_shape=jax.ShapeDtypeStruct((n_rows, d), f.dtype),
        compiler_params=_cparams(("arbitrary",)),
        name="moe_scatter",
    )(dest, pad_rows, f)


def _expert_kernel(be_ref, x_ref, wg_ref, wu_ref, wd_ref, o_ref):
    h = _bf(x_ref[...])
    act = _silu(jnp.dot(h, wg_ref[0], preferred_element_type=F32)) * jnp.dot(h, wu_ref[0],
                                                                              preferred_element_type=F32)
    o_ref[...] = _dot(act, wd_ref[0])


def _experts(xs, block_e, wg, wu, wd):
    n_rows, d = xs.shape
    ff = wg.shape[2]
    rows = MOE_ROWS
    return pl.pallas_call(
        _expert_kernel,
        grid_spec=pltpu.PrefetchScalarGridSpec(
            num_scalar_prefetch=1,
            grid=(n_rows // rows,),
            in_specs=[pl.BlockSpec((rows, d), lambda i, be: (i, 0)),
                      pl.BlockSpec((1, d, ff), lambda i, be: (be[i], 0, 0)),
                      pl.BlockSpec((1, d, ff), lambda i, be: (be[i], 0, 0)),
                      pl.BlockSpec((1, ff, d), lambda i, be: (be[i], 0, 0))],
            out_specs=pl.BlockSpec((rows, d), lambda i, be: (i, 0))),
        out_shape=jax.ShapeDtypeStruct((n_rows, d), F32),
        compiler_params=_cparams(("arbitrary",)),
        name="moe_experts",
    )(block_e, xs, wg, wu, wd)


def _combine_kernel(dest_ref, x_ref, modl_ref, modc_ref, w_ref, gfin_ref, ys_hbm, o_ref, y1_ref, y2_ref, sem, *,
                    tm, tiles_per_row):
    tile = pl.program_id(0) * tiles_per_row + pl.program_id(1)
    n_tiles = pl.num_programs(0) * tiles_per_row
    buf = tile % 2

    def fetch(t, b):
        def issue(r, carry):
            slot = 2 * (t * tm + r)
            pltpu.make_async_copy(ys_hbm.at[pl.ds(dest_ref[slot], 1)], y1_ref.at[b, pl.ds(r, 1)], sem.at[b]).start()
            pltpu.make_async_copy(ys_hbm.at[pl.ds(dest_ref[slot + 1], 1)], y2_ref.at[b, pl.ds(r, 1)],
                                  sem.at[b]).start()
            return carry

        lax.fori_loop(0, tm, issue, 0, unroll=8)

    @pl.when(tile == 0)
    def _():
        fetch(0, 0)

    @pl.when(tile + 1 < n_tiles)
    def _():
        fetch(tile + 1, 1 - buf)

    pltpu.make_async_copy(ys_hbm.at[pl.ds(0, tm)], y1_ref.at[buf], sem.at[buf]).wait()
    pltpu.make_async_copy(ys_hbm.at[pl.ds(0, tm)], y2_ref.at[buf], sem.at[buf]).wait()
    w = w_ref[0]
    moe = w[:, 0:1] * y1_ref[buf] + w[:, 1:2] * y2_ref[buf]
    y = x_ref[0] + _gate_vec(modl_ref, modc_ref, 5, 0, tm, 0) * moe
    o_ref[0] = y * lax.rsqrt(jnp.mean(y * y, axis=-1, keepdims=True) + NORM_EPS) * gfin_ref[...]


def _combine(x, mods, tokw, dest, ys, final_g):
    b, s, d = x.shape
    tm = TOK_TILE
    tiles = s // tm
    return pl.pallas_call(
        functools.partial(_combine_kernel, tm=tm, tiles_per_row=tiles),
        grid_spec=pltpu.PrefetchScalarGridSpec(
            num_scalar_prefetch=1,
            grid=(b, tiles),
            in_specs=[pl.BlockSpec((1, tm, d), lambda i, j, dr: (i, j, 0)),
                      pl.BlockSpec((1, N_MOD, d), lambda i, j, dr: (i, 0, 0)),
                      pl.BlockSpec((1, N_MOD, d), lambda i, j, dr: (0, 0, 0)),
                      pl.BlockSpec((1, tm, LANES), lambda i, j, dr: (i, j, 0)),
                      pl.BlockSpec((1, d), lambda i, j, dr: (0, 0)),
                      pl.BlockSpec(memory_space=pl.ANY)],
            out_specs=pl.BlockSpec((1, tm, d), lambda i, j, dr: (i, j, 0)),
            scratch_shapes=[pltpu.VMEM((2, tm, d), F32), pltpu.VMEM((2, tm, d), F32),
                            pltpu.SemaphoreType.DMA((2,))]),
        out_shape=jax.ShapeDtypeStruct((b, s, d), F32),
        compiler_params=_cparams(("arbitrary", "arbitrary")),
        name="moe_combine",
    )(dest, x, mods, mods, tokw, final_g.reshape(1, d), ys)


def _moe_routing(eidx):
    m = eidx.shape[0]
    oh = (eidx[:, None] == jnp.arange(N_EXPERTS, dtype=jnp.int32)[None, :]).astype(jnp.int32)
    csum = jnp.cumsum(oh, axis=0)
    rank = jnp.sum((csum - oh) * oh, axis=1)
    counts = csum[-1]
    padded = (counts + MOE_ROWS - 1) // MOE_ROWS * MOE_ROWS
    pad_end = jnp.cumsum(padded)
    dest = jnp.sum(oh * (pad_end - padded)[None, :], axis=1) + rank
    n_blocks = m // MOE_ROWS + N_EXPERTS
    starts = jnp.arange(n_blocks, dtype=jnp.int32) * MOE_ROWS
    block_e = jnp.minimum(jnp.sum((starts[:, None] >= pad_end[None, :]).astype(jnp.int32), axis=1), N_EXPERTS - 1)
    pad_cnt = padded - counts
    pad_cum = jnp.cumsum(pad_cnt)
    idx = jnp.arange(N_EXPERTS * MOE_ROWS, dtype=jnp.int32)
    owner = jnp.sum((idx[:, None] >= pad_cum[None, :]).astype(jnp.int32), axis=1)
    own = (jnp.minimum(owner, N_EXPERTS - 1)[:, None] == jnp.arange(N_EXPERTS)[None, :]).astype(jnp.int32)
    in_expert = jnp.sum(own * (pad_end - pad_cnt)[None, :], axis=1) + idx - jnp.sum(
        own * (pad_cum - pad_cnt)[None, :], axis=1)
    pad_rows = jnp.where(owner < N_EXPERTS, in_expert, pad_end[-1] + idx - pad_cum[-1])
    return dest.astype(jnp.int32), block_e, pad_rows.astype(jnp.int32), n_blocks * MOE_ROWS


def _rope_tables(n_lat, ctx_len):
    rows = n_lat // GRID_W
    row = jnp.repeat(jnp.arange(rows, dtype=F32), GRID_W)
    col = jnp.tile(jnp.arange(GRID_W, dtype=F32), rows)
    inv = ROPE_BASE ** (-jnp.arange(0, AX_DIM, 2, dtype=F32) / AX_DIM)
    ar, ac = row[:, None] * inv, col[:, None] * inv
    cos = jnp.concatenate([jnp.cos(ar), jnp.cos(ar), jnp.cos(ac), jnp.cos(ac)], axis=-1)
    sin = jnp.concatenate([-jnp.sin(ar), jnp.sin(ar), -jnp.sin(ac), jnp.sin(ac)], axis=-1)
    n_heads = ROPE_COLS // HD
    cos = jnp.concatenate([jnp.ones((ctx_len, HD), F32), cos], axis=0)
    sin = jnp.concatenate([jnp.zeros((ctx_len, HD), F32), sin], axis=0)
    half = AX_DIM // 2
    j = jnp.arange(HD)
    partner = jnp.where((j % AX_DIM) < half, j + half, j - half)
    perm = (jnp.arange(n_heads)[:, None] * HD + partner[None, :]).reshape(-1)
    return jnp.tile(cos, (1, n_heads)), jnp.tile(sin, (1, n_heads)), perm


def kernel(x, c, ctx, c_ctx, mod_w, mod_b, norm_mix, norm_ffn, norm_final, rec_w_in, rec_w_out, rwkv_mu, rwkv_w0, rwkv_w_up, rwkv_a0, rwkv_a_up, rwkv_g_up, rwkv_k_k, rwkv_k_a, rwkv_r_k, rwkv_ln_w, rwkv_ln_b, hgrn_lb, hgrn_norm, ffn_w_gate, ffn_w_up, ffn_w_down, att_w_in, att_w_out, att_sink, moe_router, moe_router_b, moe_w_gate, moe_w_up, moe_w_down):
    bsz, n_lat, d = x.shape
    ctx_len = ctx.shape[1]

    n_rows = -(-(bsz + 1) // 8) * 8
    cvec = jnp.zeros((n_rows, d), F32).at[:bsz].set(c).at[bsz].set(c_ctx)
    mods = [_adaln(cvec, mod_w[l], mod_b[l]).reshape(n_rows, N_MOD, d) for l in range(2)]

    ops, vv, g_bonus, gam, p_hgrn = _rec_in(ctx, x, mods[0], bsz, norm_mix[0], _bf(rec_w_in[0]), rwkv_mu[0],
                                            rwkv_w0[0], rwkv_w_up[0], rwkv_a0[0], rwkv_a_up[0], rwkv_g_up[0],
                                            rwkv_k_k[0], rwkv_k_a[0], rwkv_r_k[0].reshape(-1))
    oa = _rwkv_scan(ops, vv, gam, ctx_len)
    lb = jnp.cumsum(jax.nn.softmax(hgrn_lb.astype(F32), axis=1), axis=1)[:, 0].reshape(2, 1, B_W)
    ob = _hgrn_scan(p_hgrn, lb, ctx_len)
    xcat = _rec_out(ctx, x, mods[0], bsz, oa, g_bonus, ob, p_hgrn, rwkv_ln_w[0], rwkv_ln_b[0], hgrn_norm[0],
                    _bf(rec_w_out[0]), norm_ffn[0], _bf(ffn_w_gate[0]), _bf(ffn_w_up[0]), _bf(ffn_w_down[0]))

    cos, sin, perm = _rope_tables(n_lat, ctx_len)
    w_att = att_w_in[0]
    qkv = _proj_rope(xcat, mods[1], bsz, norm_mix[1], _bf(w_att), _bf(w_att[:, perm]), cos, sin, ctx_len)
    att = _attention(qkv, att_sink[0], ctx_len)
    x_lat, tokw, eidx, f_lat = _att_out(xcat, mods[1], att, _bf(att_w_out[0]), norm_ffn[1], moe_router[0],
                                        moe_router_b[0], x_skip=ctx_len)
    dest, block_e, pad_rows, n_rows = _moe_routing(eidx[..., :2].reshape(-1))
    xs = _row_scatter(f_lat.reshape(bsz * n_lat, d), dest, pad_rows, n_rows)
    ys = _experts(xs, block_e, _bf(moe_w_gate[0]), _bf(moe_w_up[0]), _bf(moe_w_down[0]))
    return _combine(x_lat, mods[1], tokw, dest, ys, norm_final)
```

```python
import functools
import math

import jax
import jax.numpy as jnp
from jax import lax
from jax.experimental import pallas as pl
from jax.experimental.pallas import tpu as pltpu

F32 = jnp.float32
BF16 = jnp.bfloat16

N_MOD = 6
NORM_EPS = 1e-6
NEG_INF = -1e30

A_HEADS = 8
A_HD = 64
A_W = A_HEADS * A_HD
DECAY_LORA = 64
AAA_LORA = 64
GATE_LORA = 128
RWKV_COLS = 3 * A_W + DECAY_LORA + AAA_LORA + GATE_LORA
GN_EPS = 64e-5

B_HEADS = 4
B_DK = 128
B_W = B_HEADS * B_DK
HGRN_COLS = 5 * B_W

HQ = 16
HKV = 4
GQ = HQ // HKV
HD = 64
WINDOW = 128
ATT_BLOCK = 128
ATT_STACK = GQ
AX_DIM = HD // 2
ROPE_BASE = 10000.0
GRID_W = 64
Q_COLS = HQ * HD
KV_COLS = HKV * HD
ROPE_COLS = Q_COLS + KV_COLS
ATT_COLS = Q_COLS + 2 * KV_COLS

N_EXPERTS = 8
LANES = 128
CHUNK = 64
SCAN_BLOCK = 256
MXU_WIDTH = 256
RWKV_GROUP = MXU_WIDTH // A_HD
TOK_TILE = 256
PROJ_TILE = 768
ATT_OUT_TILE = 512
VMEM_LIMIT = 56 * 1024 * 1024

OP_KAP, OP_RT, OP_KBAR, OP_BBAR, OP_KGAM, OP_BGAM = range(6)
N_OPS = 6


def _cparams(sem):
    return pltpu.CompilerParams(dimension_semantics=sem, vmem_limit_bytes=VMEM_LIMIT)


def _bf(x):
    return x.astype(BF16)


def _dot(a, b):
    return jnp.dot(_bf(a), _bf(b), preferred_element_type=F32)


def _dot_nt(a, b):
    return lax.dot_general(_bf(a), _bf(b), (((1,), (1,)), ((), ())), preferred_element_type=F32)


def _dot_tn(a, b):
    return lax.dot_general(_bf(a), _bf(b), (((0,), (0,)), ((), ())), preferred_element_type=F32)


def _dot_f32(a, b):
    a_hi, b_hi = _bf(a), _bf(b)
    a_lo, b_lo = _bf(a - a_hi.astype(F32)), _bf(b - b_hi.astype(F32))
    acc = jnp.dot(a_hi, b_hi, preferred_element_type=F32)
    acc += jnp.dot(a_hi, b_lo, preferred_element_type=F32)
    acc += jnp.dot(a_lo, b_hi, preferred_element_type=F32)
    return acc


def _sigmoid(x):
    return 0.5 * jnp.tanh(0.5 * x) + 0.5


def _silu(x):
    return x * _sigmoid(x)


def _segsum(x, ones_bd):
    hi = _bf(x)
    lo = _bf(x - hi.astype(F32))
    return jnp.dot(hi, ones_bd, preferred_element_type=F32) + jnp.dot(lo, ones_bd, preferred_element_type=F32)


def _dot_split(a, x):
    hi = _bf(x)
    lo = _bf(x - hi.astype(F32))
    return jnp.dot(a, hi, preferred_element_type=F32) + jnp.dot(a, lo, preferred_element_type=F32)


def _block_ones(width, seg):
    i = jnp.arange(width) // seg
    return (i[:, None] == i[None, :]).astype(BF16)


def _modulate(x, g, modl_ref, modc_ref, row, pos0, ctx_len, pos=None):
    y = x * lax.rsqrt(jnp.mean(x * x, axis=-1, keepdims=True) + NORM_EPS) * g
    if pos is None:
        pos = pos0 + lax.broadcasted_iota(jnp.int32, (x.shape[0], 1), 0)
    is_ctx = pos < ctx_len
    shift = jnp.where(is_ctx, modc_ref[0, row:row + 1, :], modl_ref[0, row:row + 1, :])
    scale = jnp.where(is_ctx, modc_ref[0, row + 1:row + 2, :], modl_ref[0, row + 1:row + 2, :])
    return y * (1.0 + scale) + shift


def _gate_vec(modl_ref, modc_ref, row, pos0, n, ctx_len):
    pos = pos0 + lax.broadcasted_iota(jnp.int32, (n, 1), 0)
    return jnp.where(pos < ctx_len, modc_ref[0, row:row + 1, :], modl_ref[0, row:row + 1, :])


def _mod_specs(d, ctx_row):
    return [pl.BlockSpec((1, N_MOD, d), lambda i, *_: (i, 0, 0)),
            pl.BlockSpec((1, N_MOD, d), lambda *_: (ctx_row, 0, 0))]


def _adaln_kernel(c_ref, w_ref, b_ref, o_ref):
    o_ref[...] = _dot(_silu(c_ref[...]), w_ref[...]) + b_ref[...]


def _adaln(cvec, w, b):
    r, d = cvec.shape
    n = w.shape[1]
    tn = 1024
    return pl.pallas_call(
        _adaln_kernel,
        grid=(n // tn,),
        in_specs=[pl.BlockSpec((r, d), lambda j: (0, 0)),
                  pl.BlockSpec((d, tn), lambda j: (0, j)),
                  pl.BlockSpec((1, tn), lambda j: (0, j))],
        out_specs=pl.BlockSpec((r, tn), lambda j: (0, j)),
        out_shape=jax.ShapeDtypeStruct((r, n), F32),
        compiler_params=_cparams(("parallel",)),
        name="adaln",
    )(cvec, w, b.reshape(1, n))


def _proj_rope_kernel(x_ref, modl_ref, modc_ref, g_ref, w_ref, wrot_ref, cos_ref, sin_ref, o_ref, *, ctx_len, tm,
                      rope_cols):
    pos0 = pl.program_id(1) * tm
    h = _bf(_modulate(x_ref[0], g_ref[...], modl_ref, modc_ref, 0, pos0, ctx_len))
    y = jnp.dot(h, w_ref[...], preferred_element_type=F32)
    yr = jnp.dot(h, wrot_ref[...], preferred_element_type=F32)
    o_ref[0, :, :rope_cols] = y[:, :rope_cols] * cos_ref[...] + yr * sin_ref[...]
    o_ref[0, :, rope_cols:] = y[:, rope_cols:]


def _proj_rope(xcat, mods, ctx_row, g, w, wrot, cos, sin, ctx_len):
    b, s, d = xcat.shape
    n = w.shape[1]
    tm = PROJ_TILE
    rope_cols = wrot.shape[1]
    return pl.pallas_call(
        functools.partial(_proj_rope_kernel, ctx_len=ctx_len, tm=tm, rope_cols=rope_cols),
        grid=(b, s // tm),
        in_specs=[pl.BlockSpec((1, tm, d), lambda i, j: (i, j, 0))] + _mod_specs(d, ctx_row) + [
            pl.BlockSpec((1, d), lambda i, j: (0, 0)),
            pl.BlockSpec((d, n), lambda i, j: (0, 0)),
            pl.BlockSpec((d, rope_cols), lambda i, j: (0, 0)),
            pl.BlockSpec((tm, rope_cols), lambda i, j: (j, 0)),
            pl.BlockSpec((tm, rope_cols), lambda i, j: (j, 0))],
        out_specs=pl.BlockSpec((1, tm, n), lambda i, j: (i, j, 0)),
        out_shape=jax.ShapeDtypeStruct((b, s, n), F32),
        compiler_params=_cparams(("parallel", "parallel")),
        name="proj_rope",
    )(xcat, mods, mods, g.reshape(1, d), w, wrot, cos, sin)


DECAY_SCALE = math.exp(-0.5)


def _stream_specs(tm, d, ctx_len):
    assert ctx_len == tm
    return [pl.BlockSpec((1, tm, d), lambda i, j: (i, 0, 0)),
            pl.BlockSpec((1, tm, d), lambda i, j: (i, jnp.maximum(j - 1, 0), 0))]


def _stream_tile(c_ref, x_ref):
    return jnp.where(pl.program_id(1) == 0, c_ref[0], x_ref[0])


def _rec_in_kernel(c_ref, x_ref, xp_ref, xn_ref, modl_ref, modc_ref, g_ref, w_ref, mu_ref, w0_ref, wup_ref, a0_ref,
                   aup_ref, gup_ref, kk_ref, ka_ref, rk_ref, ones_ref, ops_ref, v_ref, gb_ref, gam_ref, ph_ref, *,
                   ctx_len, seq_len, tm):
    pos0 = pl.program_id(1) * tm
    gvec = g_ref[...]
    h = _bf(_modulate(_stream_tile(c_ref, x_ref), gvec, modl_ref, modc_ref, 0, pos0, ctx_len))
    p = jnp.dot(h, w_ref[:, :RWKV_COLS], preferred_element_type=F32)

    hgrn_pieces = [(RWKV_COLS + c, min(B_W, HGRN_COLS - c)) for c in range(0, HGRN_COLS, B_W)]

    def hgrn_piece():
        if hgrn_pieces:
            lo, width = hgrn_pieces.pop(0)
            ph_ref[0, :, lo - RWKV_COLS:lo - RWKV_COLS + width] = jnp.dot(h, w_ref[:, lo:lo + width],
                                                                          preferred_element_type=F32)

    halo = lax.broadcasted_iota(jnp.int32, (16, 1), 0)
    halo_pos = jnp.where(halo < 8, pos0 - 8 + halo, pos0 + tm - 8 + halo)
    xh = jnp.concatenate([xp_ref[0], xn_ref[0]], axis=0)
    p_halo = jnp.dot(_bf(_modulate(xh, gvec, modl_ref, modc_ref, 0, 0, ctx_len, pos=halo_pos)),
                     w_ref[:, :RWKV_COLS], preferred_element_type=F32)
    rows = lax.broadcasted_iota(jnp.int32, (tm, 1), 0)
    starts_seq = (pos0 == 0) | (pos0 == ctx_len)
    ends_seq = (pos0 + tm == ctx_len) | (pos0 + tm == seq_len)
    prev_halo = jnp.where(starts_seq, 0.0, p_halo[7:8, :])
    next_halo = jnp.where(ends_seq, 0.0, p_halo[8:9, :])
    prev = jnp.where(rows == 0, prev_halo, pltpu.roll(p, 1, 0))
    nxt = jnp.where(rows == tm - 1, next_halo, pltpu.roll(p, tm - 1, 0))
    p = p + mu_ref[...] * (0.5 * (prev + nxt) - p)
    hgrn_piece()

    r = p[:, 0:A_W]
    k = p[:, A_W:2 * A_W]
    v = p[:, 2 * A_W:3 * A_W]
    lo = 3 * A_W
    wd = p[:, lo:lo + DECAY_LORA]
    ad = p[:, lo + DECAY_LORA:lo + DECAY_LORA + AAA_LORA]
    gd = p[:, lo + DECAY_LORA + AAA_LORA:lo + DECAY_LORA + AAA_LORA + GATE_LORA]

    tw = jnp.tanh(wd)
    a = _sigmoid(a0_ref[...] + _dot(ad, aup_ref[...]))
    ones_bd = ones_ref[...]
    kk = k * kk_ref[...]
    kk = kk / jnp.maximum(jnp.sqrt(_segsum(kk * kk, ones_bd)), 1e-12)
    k = k * (1.0 + (a - 1.0) * ka_ref[...])
    b = kk * a
    v_ref[0] = _bf(v)
    gb_ref[0, :, :A_W] = _dot(_sigmoid(gd), gup_ref[...])
    gb_ref[0, :, A_W:] = _segsum(r * k * rk_ref[...], ones_bd) * v
    hgrn_piece()

    trow = lax.broadcasted_iota(jnp.int32, (tm, tm), 0)
    tcol = lax.broadcasted_iota(jnp.int32, (tm, tm), 1)
    same = (trow // CHUNK) == (tcol // CHUNK)
    n_chunks = tm // CHUNK
    for d in range(2):
        lw = -DECAY_SCALE * _sigmoid(w0_ref[d:d + 1, :] + _dot(tw, wup_ref[d]))
        before = (tcol <= trow) if d == 0 else (tcol >= trow)
        g_incl = _dot_split(jnp.where(same & before, 1.0, 0.0).astype(BF16), lw)
        last = [c * CHUNK + (CHUNK - 1 if d == 0 else 0) for c in range(n_chunks)]
        gam = jnp.exp(jnp.concatenate([g_incl[t:t + 1] for t in last], axis=0))
        gam_ref[0, d, 0] = gam
        hgrn_piece()
        e_ng = jnp.exp(-g_incl)
        e_tail = e_ng * jnp.concatenate([jnp.broadcast_to(gam[c:c + 1], (CHUNK, A_W)) for c in range(n_chunks)],
                                        axis=0)
        operands = {OP_KAP: kk * jnp.exp(g_incl - lw),
                    OP_RT: r * jnp.exp(g_incl),
                    OP_KBAR: k * e_ng, OP_BBAR: b * e_ng,
                    OP_KGAM: k * e_tail, OP_BGAM: b * e_tail}
        for sec, val in operands.items():
            ops_ref[0, d, :, sec * A_W:(sec + 1) * A_W] = _bf(val)
        hgrn_piece()
    while hgrn_pieces:
        hgrn_piece()


def _rec_in(ctx, x, mods, ctx_row, g, w, mu, w0, w_up, a0, a_up, g_up, k_k, k_a, r_k):
    b, n_lat, d = x.shape
    ctx_len = ctx.shape[1]
    s = ctx_len + n_lat
    tm = SCAN_BLOCK
    assert n_lat % tm == 0
    nb8 = n_lat // 8
    row = lambda a: a.reshape(1, -1)
    full = lambda a: pl.BlockSpec(a.shape, lambda i, j: (0,) * a.ndim)
    args = [row(g), w, row(mu), w0, w_up, row(a0), a_up, g_up, row(k_k), row(k_a), row(r_k),
            _block_ones(A_W, A_HD)]
    return pl.pallas_call(
        functools.partial(_rec_in_kernel, ctx_len=ctx_len, seq_len=s, tm=tm),
        grid=(b, s // tm),
        in_specs=_stream_specs(tm, d, ctx_len) + [
            pl.BlockSpec((1, 8, d), lambda i, j: (i, jnp.maximum((j - 1) * (tm // 8) - 1, 0), 0)),
            pl.BlockSpec((1, 8, d), lambda i, j: (i, jnp.minimum(j * (tm // 8), nb8 - 1), 0)),
        ] + _mod_specs(d, ctx_row) + [full(a) for a in args],
        out_specs=[pl.BlockSpec((1, 2, tm, N_OPS * A_W), lambda i, j: (i, 0, j, 0)),
                   pl.BlockSpec((1, tm, A_W), lambda i, j: (i, j, 0)),
                   pl.BlockSpec((1, tm, 2 * A_W), lambda i, j: (i, j, 0)),
                   pl.BlockSpec((1, 2, 1, tm // CHUNK, A_W), lambda i, j: (i, 0, j, 0, 0)),
                   pl.BlockSpec((1, tm, HGRN_COLS), lambda i, j: (i, j, 0))],
        out_shape=[jax.ShapeDtypeStruct((b, 2, s, N_OPS * A_W), BF16),
                   jax.ShapeDtypeStruct((b, s, A_W), BF16),
                   jax.ShapeDtypeStruct((b, s, 2 * A_W), F32),
                   jax.ShapeDtypeStruct((b, 2, s // tm, tm // CHUNK, A_W), F32),
                   jax.ShapeDtypeStruct((b, s, HGRN_COLS), F32)],
        compiler_params=_cparams(("parallel", "parallel")),
        name="rec_in",
    )(ctx, x, x, x, mods, mods, *args)


def _chunk_order(d, j, n_ctx_chunks, n_chunks):
    back = jnp.where(j < n_ctx_chunks, n_ctx_chunks - 1 - j, n_chunks - 1 + n_ctx_chunks - j)
    return jnp.where(d == 0, j, back)


def _incl_mask(rev):
    row = lax.broadcasted_iota(jnp.int32, (CHUNK, CHUNK), 0)
    col = lax.broadcasted_iota(jnp.int32, (CHUNK, CHUNK), 1)
    return jnp.where(rev, row - col, col - row) <= 0


def _scan_step_coords(t, n_steps):
    return jnp.minimum(t, n_steps - 1), jnp.maximum(t - 1, 0)


def _rwkv_chunk_kernel(ops_ref, v_ref, gam_ref, o_ref, s_ref, tr_ref, uu_ref, ol_ref, ab_ref,
                       m2_ref, cc_ref, gm_ref, *, n_blocks, n_steps):
    step = pl.program_id(0)
    t_in, t_out = _scan_step_coords(step, n_steps)
    rev = (t_in // n_blocks) % 2 == 1
    rev_out = (t_out // n_blocks) % 2 == 1
    first_out = t_out % n_blocks == 0
    w_slot = step % 2
    r_slot = 1 - w_slot

    @pl.when(step == 0)
    def _():
        s_ref[...] = jnp.zeros_like(s_ref)
        for ref in (tr_ref, uu_ref, ol_ref, ab_ref, m2_ref, cc_ref, gm_ref):
            ref[1] = jnp.zeros(ref.shape[1:], ref.dtype)

    gw = RWKV_GROUP * A_HD
    groups = range(A_HEADS // RWKV_GROUP)
    gsl = [slice(g * gw, (g + 1) * gw) for g in groups]
    row = lax.broadcasted_iota(jnp.int32, (CHUNK, gw), 0)
    col = lax.broadcasted_iota(jnp.int32, (CHUNK, gw), 1) % A_HD
    diff = jnp.where(rev, row - col, col - row)
    strict_c = diff < 0
    incl_c = diff <= 0
    eye_c = jnp.where(diff == 0, 1.0, 0.0)
    bd_mask = (lax.broadcasted_iota(jnp.int32, (gw, gw), 0) // A_HD
               == lax.broadcasted_iota(jnp.int32, (gw, gw), 1) // A_HD)

    def bd(x):
        return jnp.where(bd_mask, jnp.concatenate([x] * RWKV_GROUP, axis=0), jnp.zeros((), x.dtype))

    def stack(x):
        return jnp.concatenate([x[:, h * A_HD:(h + 1) * A_HD] for h in range(RWKV_GROUP)], axis=0)

    n = SCAN_BLOCK // CHUNK
    rows = [pl.ds(pl.multiple_of(jnp.where(rev, (n - 1 - i) * CHUNK, i * CHUNK), CHUNK), CHUNK) for i in range(n)]
    rows_out = [pl.ds(pl.multiple_of(jnp.where(rev_out, (n - 1 - i) * CHUNK, i * CHUNK), CHUNK), CHUNK)
                for i in range(n)]
    units = [(i, g) for i in range(n) for g in groups]

    s = [jnp.where(first_out, 0.0, s_ref[g]) for g in groups]
    zr = {}
    pieces = []

    def advance(i):
        for g in groups:
            ui = i * len(groups) + g
            zr[i, g] = _dot_nt(tr_ref[r_slot, ui], bd(s[g]))
            s[g] = s[g] * gm_ref[r_slot, ui] - jnp.dot(_bf(s[g]), m2_ref[r_slot, ui],
                                                       preferred_element_type=F32) + cc_ref[r_slot, ui]

    def emit_out(i):
        for g in groups:
            ui = i * len(groups) + g
            z = zr[i, g][:CHUNK] + uu_ref[r_slot, ui]
            o_ref[0, 0, rows_out[i], gsl[g]] = (zr[i, g][CHUNK:] + ol_ref[r_slot, ui]
                                                - _dot(ab_ref[r_slot, ui], bd(z)))

    for i in range(n):
        pieces += [functools.partial(advance, i), functools.partial(emit_out, i)]

    def state_piece():
        if pieces:
            pieces.pop(0)()

    def operand(sec, u):
        i, g = u
        return ops_ref[0, 0, rows[i], sec * A_W + g * gw:sec * A_W + (g + 1) * gw]

    kap = {u: operand(OP_KAP, u) for u in units}
    rt = {u: operand(OP_RT, u) for u in units}
    kbar = {u: operand(OP_KBAR, u) for u in units}
    bbar = {u: operand(OP_BBAR, u) for u in units}
    kgam = {u: operand(OP_KGAM, u) for u in units}
    bgam = {u: operand(OP_BGAM, u) for u in units}
    v = {(i, g): v_ref[0, rows[i], gsl[g]] for i, g in units}
    chunk_of = [jnp.where(rev, n - 1 - i, i) for i in range(n)]
    gam = {(i, g): gam_ref[0, 0, 0, pl.ds(chunk_of[i], 1), gsl[g]] for i, g in units}

    x = {u: jnp.concatenate([kap[u], rt[u]], axis=0) for u in units}
    yb = {u: _dot_nt(x[u], bd(bbar[u])) for u in units}
    state_piece()
    a = {u: jnp.where(strict_c, -yb[u][:CHUNK], 0.0) for u in units}
    xs = {u: eye_c + a[u] for u in units}
    pw = {u: _dot(a[u], bd(a[u])) for u in units}
    state_piece()
    for _ in range(4):
        st = {u: _dot(jnp.concatenate([pw[u], xs[u]], axis=0), bd(pw[u])) for u in units}
        pw = {u: st[u][:CHUNK] for u in units}
        xs = {u: xs[u] + st[u][CHUNK:] for u in units}
        state_piece()
    t_inv = {u: xs[u] + _dot(xs[u], bd(pw[u])) for u in units}
    state_piece()
    yk = {u: _dot_nt(x[u], bd(kbar[u])) for u in units}
    ykm = {u: jnp.concatenate([jnp.where(strict_c, yk[u][:CHUNK], 0.0), jnp.where(incl_c, yk[u][CHUNK:], 0.0)],
                              axis=0) for u in units}
    wo = {u: _dot(ykm[u], bd(v[u])) for u in units}
    w1 = {u: wo[u][:CHUNK] for u in units}
    o_loc = {u: wo[u][CHUNK:] for u in units}
    state_piece()
    tk = {u: _dot(t_inv[u], bd(kap[u])) for u in units}
    uu = {u: _dot(t_inv[u], bd(w1[u])) for u in units}
    while pieces:
        state_piece()
    for g in groups:
        s_ref[g] = s[g]
    a_rb = {u: jnp.where(incl_c, yb[u][CHUNK:], 0.0) for u in units}
    tu = {u: _dot_tn(jnp.concatenate([stack(tk[u]), stack(uu[u])], axis=1), bd(bgam[u])) for u in units}
    bd_m2 = {u: _bf(bd(tu[u][:A_HD])) for u in units}
    cc = {u: _dot_tn(stack(v[u]), bd(kgam[u])) - tu[u][A_HD:] for u in units}
    for u in units:
        ui = u[0] * len(groups) + u[1]
        tr_ref[w_slot, ui] = jnp.concatenate([_bf(tk[u]), rt[u]], axis=0)
        uu_ref[w_slot, ui] = uu[u]
        ol_ref[w_slot, ui] = o_loc[u]
        ab_ref[w_slot, ui] = _bf(a_rb[u])
        m2_ref[w_slot, ui] = bd_m2[u]
        cc_ref[w_slot, ui] = cc[u]
        gm_ref[w_slot, ui] = gam[u]


def _rwkv_scan(ops, vv, gam, ctx_len):
    b, _, s, _ = ops.shape
    nc, ncc = s // SCAN_BLOCK, ctx_len // SCAN_BLOCK

    n_steps = b * 2 * nc

    def coords(flat):
        d = (flat // nc) % 2
        return flat // (2 * nc), d, _chunk_order(d, flat % nc, ncc, nc)

    def coords_in(t):
        return coords(_scan_step_coords(t, n_steps)[0])

    def out_map(t):
        bi, d, blk = coords(_scan_step_coords(t, n_steps)[1])
        return bi, d, blk, 0

    def ops_map(t):
        bi, d, blk = coords_in(t)
        return bi, d, blk, 0

    def v_map(t):
        bi, _, blk = coords_in(t)
        return bi, blk, 0

    def gam_map(t):
        bi, d, blk = coords_in(t)
        return bi, d, blk, 0, 0

    n_groups = A_HEADS // RWKV_GROUP
    n_chunks = SCAN_BLOCK // CHUNK
    n_units = n_chunks * n_groups
    gw = RWKV_GROUP * A_HD
    return pl.pallas_call(
        functools.partial(_rwkv_chunk_kernel, n_blocks=nc, n_steps=n_steps),
        grid=(n_steps + 1,),
        in_specs=[pl.BlockSpec((1, 1, SCAN_BLOCK, N_OPS * A_W), ops_map),
                  pl.BlockSpec((1, SCAN_BLOCK, A_W), v_map),
                  pl.BlockSpec((1, 1, 1, n_chunks, A_W), gam_map)],
        out_specs=pl.BlockSpec((1, 1, SCAN_BLOCK, A_W), out_map),
        out_shape=jax.ShapeDtypeStruct((b, 2, s, A_W), F32),
        scratch_shapes=[pltpu.VMEM((n_groups, A_HD, gw), F32),
                        pltpu.VMEM((2, n_units, 2 * CHUNK, gw), BF16),
                        pltpu.VMEM((2, n_units, CHUNK, gw), F32),
                        pltpu.VMEM((2, n_units, CHUNK, gw), F32),
                        pltpu.VMEM((2, n_units, CHUNK, gw), BF16),
                        pltpu.VMEM((2, n_units, gw, gw), BF16),
                        pltpu.VMEM((2, n_units, CHUNK, gw), F32),
                        pltpu.VMEM((2, n_units, 1, gw), F32)],
        compiler_params=_cparams(("arbitrary",)),
        name="rwkv_scan",
    )(ops, vv, gam)


def _hgrn_chunk_kernel(q_ref, i_ref, f_ref, lb_ref, o_ref, s_ref):
    rev = pl.program_id(1) == 1

    @pl.when(pl.program_id(2) == 0)
    def _():
        s_ref[...] = jnp.zeros_like(s_ref)

    incl = _incl_mask(rev)
    incl_bf = jnp.where(incl, 1.0, 0.0).astype(BF16)
    lb = lb_ref[0]
    heads = range(B_HEADS)
    sls = [slice(h * B_DK, (h + 1) * B_DK) for h in heads]

    n = SCAN_BLOCK // CHUNK
    rows = [pl.ds(pl.multiple_of(jnp.where(rev, (n - 1 - i) * CHUNK, i * CHUNK), CHUNK), CHUNK) for i in range(n)]
    units = [(i, h) for i in range(n) for h in heads]
    q_in, k_in, q_st, k_tail, gam, v = ({} for _ in range(6))
    for i in range(n):
        f = lb + (1.0 - lb) * _sigmoid(f_ref[0, rows[i], :])
        logf = jnp.log(f)
        kf = 1.0 - f
        g_incl = _dot_split(incl_bf, logf)
        g_tot = jnp.sum(logf, axis=0, keepdims=True)
        g_mid = g_incl[CHUNK // 2 - 1:CHUNK // 2, :]
        q = _silu(q_ref[0, rows[i], :])
        v_i = i_ref[0, rows[i], :]
        q_in_i = q * jnp.exp(g_incl - g_mid)
        k_in_i = kf * jnp.exp(g_mid - g_incl)
        q_st_i = q_in_i * jnp.exp(g_mid)
        k_tail_i = k_in_i * jnp.exp(g_tot - g_mid)
        for h in heads:
            q_in[i, h], k_in[i, h] = q_in_i[:, sls[h]], k_in_i[:, sls[h]]
            q_st[i, h], k_tail[i, h] = q_st_i[:, sls[h]], k_tail_i[:, sls[h]]
            gam[i, h] = jnp.exp(g_tot)[:, sls[h]]
            v[i, h] = v_i[:, sls[h]]
    att = {u: jnp.where(incl, _dot_nt(q_in[u], k_in[u]), 0.0) for u in units}
    o_loc = {u: _dot(att[u], v[u]) for u in units}
    kv = {u: _dot_tn(v[u], k_tail[u]) for u in units}

    s = [s_ref[h] for h in heads]
    for i in range(n):
        for h in heads:
            o_ref[0, 0, rows[i], sls[h]] = o_loc[i, h] + _dot_nt(q_st[i, h], s[h])
            s[h] = s[h] * gam[i, h] + kv[i, h]
    for h in heads:
        s_ref[h] = s[h]


def _hgrn_scan(p, lb, ctx_len):
    b, s, _ = p.shape
    nc, ncc = s // SCAN_BLOCK, ctx_len // SCAN_BLOCK

    def sec(idx):
        return pl.BlockSpec((1, SCAN_BLOCK, B_W), lambda i, d, j: (i, _chunk_order(d, j, ncc, nc), idx))

    return pl.pallas_call(
        _hgrn_chunk_kernel,
        grid=(b, 2, nc),
        in_specs=[sec(0), sec(1),
                  pl.BlockSpec((1, SCAN_BLOCK, B_W), lambda i, d, j: (i, _chunk_order(d, j, ncc, nc), 2 + d)),
                  pl.BlockSpec((1, 1, B_W), lambda i, d, j: (d, 0, 0))],
        out_specs=pl.BlockSpec((1, 1, SCAN_BLOCK, B_W), lambda i, d, j: (i, d, _chunk_order(d, j, ncc, nc), 0)),
        out_shape=jax.ShapeDtypeStruct((b, 2, s, B_W), F32),
        scratch_shapes=[pltpu.VMEM((B_HEADS, B_DK, B_DK), F32)],
        compiler_params=_cparams(("parallel", "parallel", "arbitrary")),
        name="hgrn_scan",
    )(p, p, p, lb)


def _rec_out_kernel(c_ref, x_ref, modl_ref, modc_ref, oa_ref, g_ref, bonus_ref, ob_ref, gate_ref, lnw_ref, lnb_ref,
                    hgn_ref, ones_a_ref, ones_b_ref, w_ref, gffn_ref, wg_ref, wu_ref, wd_ref, o_ref, *, ctx_len, tm):
    pos0 = pl.program_id(1) * tm
    oa = oa_ref[0, 0] + oa_ref[0, 1]
    ones_a = ones_a_ref[...]
    mean = _segsum(oa, ones_a) * (1.0 / A_HD)
    cen = oa - mean
    var = _segsum(cen * cen, ones_a) * (1.0 / A_HD)
    ya = (cen * lax.rsqrt(var + GN_EPS) * lnw_ref[...] + lnb_ref[...] + bonus_ref[0]) * g_ref[0]
    ob = ob_ref[0, 0] + ob_ref[0, 1]
    ms = _segsum(ob * ob, ones_b_ref[...]) * (1.0 / B_DK)
    yb = ob * lax.rsqrt(ms + NORM_EPS) * hgn_ref[...] * _silu(gate_ref[0])
    y = _dot(ya, w_ref[:A_W, :]) + _dot(yb, w_ref[A_W:, :])
    gate = _gate_vec(modl_ref, modc_ref, 2, pos0, tm, ctx_len)
    x1 = _stream_tile(c_ref, x_ref) + gate * y
    h = _bf(_modulate(x1, gffn_ref[...], modl_ref, modc_ref, 3, pos0, ctx_len))
    act = _silu(jnp.dot(h, wg_ref[...], preferred_element_type=F32)) * jnp.dot(h, wu_ref[...],
                                                                               preferred_element_type=F32)
    o_ref[0] = x1 + _gate_vec(modl_ref, modc_ref, 5, pos0, tm, ctx_len) * _dot(act, wd_ref[...])


def _rec_out(ctx, x, mods, ctx_row, oa, g_bonus, ob, p_hgrn, ln_w, ln_b, hg_norm, w_out, g_ffn, wg, wu, wd):
    b, n_lat, d = x.shape
    ctx_len = ctx.shape[1]
    s = ctx_len + n_lat
    tm = TOK_TILE
    row = lambda a: a.reshape(1, -1)
    full = lambda a: pl.BlockSpec(a.shape, lambda i, j: (0,) * a.ndim, pipeline_mode=pl.Buffered(1))
    consts = [row(ln_w), row(ln_b), row(jnp.tile(hg_norm, B_HEADS)), _block_ones(A_W, A_HD),
              _block_ones(B_W, B_DK), w_out, row(g_ffn), wg, wu, wd]
    return pl.pallas_call(
        functools.partial(_rec_out_kernel, ctx_len=ctx_len, tm=tm),
        grid=(b, s // tm),
        in_specs=_stream_specs(tm, d, ctx_len) + _mod_specs(d, ctx_row) + [
            pl.BlockSpec((1, 2, tm, A_W), lambda i, j: (i, 0, j, 0)),
            pl.BlockSpec((1, tm, A_W), lambda i, j: (i, j, 0)),
            pl.BlockSpec((1, tm, A_W), lambda i, j: (i, j, 1)),
            pl.BlockSpec((1, 2, tm, B_W), lambda i, j: (i, 0, j, 0)),
            pl.BlockSpec((1, tm, B_W), lambda i, j: (i, j, 4)),
        ] + [full(a) for a in consts],
        out_specs=pl.BlockSpec((1, tm, d), lambda i, j: (i, j, 0)),
        out_shape=jax.ShapeDtypeStruct((b, s, d), F32),
        compiler_params=_cparams(("parallel", "parallel")),
        name="rec_out",
    )(ctx, x, mods, mods, oa, g_bonus, g_bonus, ob, p_hgrn, *consts)


def _att_out_kernel(*refs, tm, n_sub):
    x_parts = refs[:n_sub]
    modl_ref, modc_ref, y_ref, w_ref, g_ref, rw_ref, rb_ref, o_ref, tw_ref, e_ref, f_ref = refs[n_sub:]
    x_in = jnp.concatenate([r[0] for r in x_parts], axis=0)
    x = x_in + _gate_vec(modl_ref, modc_ref, 2, 0, tm, 0) * _dot(y_ref[0], w_ref[...])
    o_ref[0] = x
    f = _modulate(x, g_ref[...], modl_ref, modc_ref, 3, 0, 0)
    f_ref[0] = f
    logits = _dot_f32(f, rw_ref[...]) + rb_ref[...]
    lane = lax.broadcasted_iota(jnp.int32, logits.shape, 1).astype(F32)
    logits = jnp.where(lane < N_EXPERTS, logits, NEG_INF)
    ex = jnp.exp(logits - jnp.max(logits, axis=-1, keepdims=True))
    probs = ex / jnp.sum(ex, axis=-1, keepdims=True)
    p1 = jnp.max(probs, axis=-1, keepdims=True)
    i1 = jnp.min(jnp.where(probs == p1, lane, float(LANES)), axis=-1, keepdims=True)
    rest = jnp.where(lane == i1, -1.0, probs)
    p2 = jnp.max(rest, axis=-1, keepdims=True)
    i2 = jnp.min(jnp.where(rest == p2, lane, float(LANES)), axis=-1, keepdims=True)
    tot = p1 + p2
    tw_ref[0] = jnp.where(lane == 0.0, p1 / tot, jnp.where(lane == 1.0, p2 / tot, 0.0))
    e_ref[0] = jnp.where(lane == 0.0, i1, jnp.where(lane == 1.0, i2, 0.0)).astype(jnp.int32)


def _att_out(xcat, mods, y, w, g, router_w, router_b, x_skip):
    b, s, k = y.shape
    d = xcat.shape[-1]
    tm = ATT_OUT_TILE
    n_sub = tm // x_skip
    assert tm % x_skip == 0 and s % tm == 0
    wpad = jnp.zeros((d, LANES), F32).at[:, :N_EXPERTS].set(router_w)
    bpad = jnp.zeros((1, LANES), F32).at[0, :N_EXPERTS].set(router_b)
    tile_spec = pl.BlockSpec((1, tm, d), lambda i, j: (i, j, 0))
    lane_spec = pl.BlockSpec((1, tm, LANES), lambda i, j: (i, j, 0))
    x_specs = [pl.BlockSpec((1, x_skip, d), functools.partial(lambda i, j, p: (i, j * n_sub + 1 + p, 0), p=p))
               for p in range(n_sub)]
    return pl.pallas_call(
        functools.partial(_att_out_kernel, tm=tm, n_sub=n_sub),
        grid=(b, s // tm),
        in_specs=x_specs + _mod_specs(d, 0) + [
            pl.BlockSpec((1, tm, k), lambda i, j: (i, j, 0)),
            pl.BlockSpec((k, d), lambda i, j: (0, 0)),
            pl.BlockSpec((1, d), lambda i, j: (0, 0)),
            pl.BlockSpec((d, LANES), lambda i, j: (0, 0)),
            pl.BlockSpec((1, LANES), lambda i, j: (0, 0))],
        out_specs=[tile_spec, lane_spec, lane_spec, tile_spec],
        out_shape=[jax.ShapeDtypeStruct((b, s, d), F32), jax.ShapeDtypeStruct((b, s, LANES), F32),
                   jax.ShapeDtypeStruct((b, s, LANES), jnp.int32), jax.ShapeDtypeStruct((b, s, d), F32)],
        compiler_params=_cparams(("parallel", "parallel")),
        name="att_out_router",
    )(*([xcat] * n_sub), mods, mods, y, w, g.reshape(1, d), wpad, bpad)


def _attn_kernel(sink_ref, q_ref, kc_ref, vc_ref, kp_ref, kq_ref, kn_ref, vp_ref, vq_ref, vn_ref, o_ref, *,
                 ctx_len, n_lat):
    i = pl.program_id(1)
    nk = ctx_len + 3 * ATT_BLOCK
    k_all = jnp.concatenate([kc_ref[0], kp_ref[0], kq_ref[0], kn_ref[0]], axis=0)
    v_all = jnp.concatenate([vc_ref[0], vp_ref[0], vq_ref[0], vn_ref[0]], axis=0)
    row = lax.broadcasted_iota(jnp.int32, (ATT_BLOCK, nk), 0)
    col = lax.broadcasted_iota(jnp.int32, (ATT_BLOCK, nk), 1)
    rel = col - (ctx_len + ATT_BLOCK)
    kabs = i * ATT_BLOCK + rel
    band = (jnp.abs(row - rel) <= WINDOW) & (kabs >= 0) & (kabs < n_lat)
    bias = jnp.where((col < ctx_len) | band, 0.0, NEG_INF)
    bias = jnp.concatenate([bias] * ATT_STACK, axis=0)
    grp = lax.broadcasted_iota(jnp.int32, (ATT_STACK * ATT_BLOCK, 1), 0) // ATT_BLOCK
    scale = HD ** -0.5
    for hq0 in range(0, HQ, ATT_STACK):
        hk = hq0 // GQ
        kh = _bf(k_all[:, hk * HD:(hk + 1) * HD])
        vh = _bf(v_all[:, hk * HD:(hk + 1) * HD])
        heads = [hq0 + g for g in range(ATT_STACK)]
        q = jnp.concatenate([q_ref[0, :, hq * HD:(hq + 1) * HD] for hq in heads], axis=0) * scale
        sk = jnp.zeros((ATT_STACK * ATT_BLOCK, 1), F32)
        for g, hq in enumerate(heads):
            sk = jnp.where(grp == g, sink_ref[hq], sk)
        qb = _bf(q)
        s_c = lax.dot_general(qb, kh[:ctx_len], (((1,), (1,)), ((), ())), preferred_element_type=F32)
        s_b = lax.dot_general(qb, kh[ctx_len:], (((1,), (1,)), ((), ())),
                              preferred_element_type=F32) + bias[:, ctx_len:]
        m = jnp.maximum(jnp.maximum(jnp.max(s_c, axis=-1, keepdims=True), jnp.max(s_b, axis=-1, keepdims=True)), sk)
        p_c = jnp.exp(s_c - m)
        p_b = jnp.exp(s_b - m)
        den = jnp.sum(p_c, axis=-1, keepdims=True) + jnp.sum(p_b, axis=-1, keepdims=True) + jnp.exp(sk - m)
        o = (jnp.dot(_bf(p_c), vh[:ctx_len], preferred_element_type=F32)
             + jnp.dot(_bf(p_b), vh[ctx_len:], preferred_element_type=F32)) / den
        for g, hq in enumerate(heads):
            o_ref[0, :, hq * HD:(hq + 1) * HD] = o[g * ATT_BLOCK:(g + 1) * ATT_BLOCK]


def _attention(qkv, sink, ctx_len):
    b, s, _ = qkv.shape
    n_lat = s - ctx_len
    nb = n_lat // ATT_BLOCK
    cb = ctx_len // ATT_BLOCK
    kcol = Q_COLS // KV_COLS
    vcol = kcol + 1

    def band(colblk, shift):
        return pl.BlockSpec((1, ATT_BLOCK, KV_COLS),
                            lambda bi, i: (bi, cb + jnp.clip(i + shift, 0, nb - 1), colblk))

    return pl.pallas_call(
        functools.partial(_attn_kernel, ctx_len=ctx_len, n_lat=n_lat),
        grid=(b, nb),
        in_specs=[pl.BlockSpec(memory_space=pltpu.SMEM),
                  pl.BlockSpec((1, ATT_BLOCK, Q_COLS), lambda bi, i: (bi, cb + i, 0)),
                  pl.BlockSpec((1, ctx_len, KV_COLS), lambda bi, i: (bi, 0, kcol)),
                  pl.BlockSpec((1, ctx_len, KV_COLS), lambda bi, i: (bi, 0, vcol)),
                  band(kcol, -1), band(kcol, 0), band(kcol, 1),
                  band(vcol, -1), band(vcol, 0), band(vcol, 1)],
        out_specs=pl.BlockSpec((1, ATT_BLOCK, Q_COLS), lambda bi, i: (bi, i, 0)),
        out_shape=jax.ShapeDtypeStruct((b, n_lat, Q_COLS), F32),
        compiler_params=_cparams(("parallel", "parallel")),
        name="attention",
    )(sink, qkv, qkv, qkv, qkv, qkv, qkv, qkv, qkv, qkv)


MOE_ROWS = 256


def _row_scatter_kernel(dest_ref, pad_ref, f_ref, xs_hbm, zero_ref, fbuf, sem, *, tm, n_pad):
    step = pl.program_id(0)
    base = step * tm
    buf = step % 2

    def wait_tile(b, times):
        for _ in range(times):
            pltpu.make_async_copy(fbuf.at[b], xs_hbm.at[pl.ds(0, tm)], sem.at[b]).wait()

    @pl.when(step == 0)
    def _():
        zero_ref[...] = jnp.zeros_like(zero_ref)

        def zero_row(r, carry):
            pltpu.make_async_copy(zero_ref, xs_hbm.at[pl.ds(pad_ref[r], 1)], sem.at[1]).start()
            return carry

        lax.fori_loop(0, n_pad, zero_row, 0, unroll=8)
        wait_tile(1, n_pad // tm)

    fbuf[buf] = f_ref[...]

    def issue(r, carry):
        slot = 2 * (base + r)
        pltpu.make_async_copy(fbuf.at[buf, pl.ds(r, 1)], xs_hbm.at[pl.ds(dest_ref[slot], 1)], sem.at[buf]).start()
        pltpu.make_async_copy(fbuf.at[buf, pl.ds(r, 1)], xs_hbm.at[pl.ds(dest_ref[slot + 1], 1)],
                              sem.at[buf]).start()
        return carry

    lax.fori_loop(0, tm, issue, 0, unroll=8)

    @pl.when(step > 0)
    def _():
        wait_tile(1 - buf, 2)

    @pl.when(step == pl.num_programs(0) - 1)
    def _():
        wait_tile(buf, 2)


def _row_scatter(f, dest, pad_rows, n_rows):
    n_tok, d = f.shape
    tm = TOK_TILE
    n_pad = pad_rows.shape[0]
    assert n_pad % tm == 0 and 2 * n_tok + n_pad == n_rows
    return pl.pallas_call(
        functools.partial(_row_scatter_kernel, tm=tm, n_pad=n_pad),
        grid_spec=pltpu.PrefetchScalarGridSpec(
            num_scalar_prefetch=2,
            grid=(n_tok // tm,),
            in_specs=[pl.BlockSpec((tm, d), lambda i, dr, pr: (i, 0))],
            out_specs=pl.BlockSpec(memory_space=pl.ANY),
            scratch_shapes=[pltpu.VMEM((1, d), f.dtype), pltpu.VMEM((2, tm, d), f.dtype),
                            pltpu.SemaphoreType.DMA((2,))]),
        out_shape=jax.ShapeDtypeStruct((n_rows, d), f.dtype),
        compiler_params=_cparams(("arbitrary",)),
        name="moe_scatter",
    )(dest, pad_rows, f)


def _expert_kernel(be_ref, x_ref, wg_ref, wu_ref, wd_ref, o_ref):
    h = _bf(x_ref[...])
    act = _silu(jnp.dot(h, wg_ref[0], preferred_element_type=F32)) * jnp.dot(h, wu_ref[0],
                                                                              preferred_element_type=F32)
    o_ref[...] = _dot(act, wd_ref[0])


def _experts(xs, block_e, wg, wu, wd):
    n_rows, d = xs.shape
    ff = wg.shape[2]
    rows = MOE_ROWS
    return pl.pallas_call(
        _expert_kernel,
        grid_spec=pltpu.PrefetchScalarGridSpec(
            num_scalar_prefetch=1,
            grid=(n_rows // rows,),
            in_specs=[pl.BlockSpec((rows, d), lambda i, be: (i, 0)),
                      pl.BlockSpec((1, d, ff), lambda i, be: (be[i], 0, 0)),
                      pl.BlockSpec((1, d, ff), lambda i, be: (be[i], 0, 0)),
                      pl.BlockSpec((1, ff, d), lambda i, be: (be[i], 0, 0))],
            out_specs=pl.BlockSpec((rows, d), lambda i, be: (i, 0))),
        out_shape=jax.ShapeDtypeStruct((n_rows, d), F32),
        compiler_params=_cparams(("arbitrary",)),
        name="moe_experts",
    )(block_e, xs, wg, wu, wd)


def _combine_kernel(dest_ref, x_ref, modl_ref, modc_ref, w_ref, gfin_ref, ys_hbm, o_ref, y1_ref, y2_ref, sem, *,
                    tm, tiles_per_row):
    tile = pl.program_id(0) * tiles_per_row + pl.program_id(1)
    n_tiles = pl.num_programs(0) * tiles_per_row
    buf = tile % 2

    def fetch(t, b):
        def issue(r, carry):
            slot = 2 * (t * tm + r)
            pltpu.make_async_copy(ys_hbm.at[pl.ds(dest_ref[slot], 1)], y1_ref.at[b, pl.ds(r, 1)], sem.at[b]).start()
            pltpu.make_async_copy(ys_hbm.at[pl.ds(dest_ref[slot + 1], 1)], y2_ref.at[b, pl.ds(r, 1)],
                                  sem.at[b]).start()
            return carry

        lax.fori_loop(0, tm, issue, 0, unroll=32)

    @pl.when(tile == 0)
    def _():
        fetch(0, 0)

    @pl.when(tile + 1 < n_tiles)
    def _():
        fetch(tile + 1, 1 - buf)

    pltpu.make_async_copy(ys_hbm.at[pl.ds(0, tm)], y1_ref.at[buf], sem.at[buf]).wait()
    pltpu.make_async_copy(ys_hbm.at[pl.ds(0, tm)], y2_ref.at[buf], sem.at[buf]).wait()
    w = w_ref[0]
    moe = w[:, 0:1] * y1_ref[buf] + w[:, 1:2] * y2_ref[buf]
    y = x_ref[0] + _gate_vec(modl_ref, modc_ref, 5, 0, tm, 0) * moe
    o_ref[0] = y * lax.rsqrt(jnp.mean(y * y, axis=-1, keepdims=True) + NORM_EPS) * gfin_ref[...]


def _combine(x, mods, tokw, dest, ys, final_g):
    b, s, d = x.shape
    tm = TOK_TILE
    tiles = s // tm
    return pl.pallas_call(
        functools.partial(_combine_kernel, tm=tm, tiles_per_row=tiles),
        grid_spec=pltpu.PrefetchScalarGridSpec(
            num_scalar_prefetch=1,
            grid=(b, tiles),
            in_specs=[pl.BlockSpec((1, tm, d), lambda i, j, dr: (i, j, 0)),
                      pl.BlockSpec((1, N_MOD, d), lambda i, j, dr: (i, 0, 0)),
                      pl.BlockSpec((1, N_MOD, d), lambda i, j, dr: (0, 0, 0)),
                      pl.BlockSpec((1, tm, LANES), lambda i, j, dr: (i, j, 0)),
                      pl.BlockSpec((1, d), lambda i, j, dr: (0, 0)),
                      pl.BlockSpec(memory_space=pl.ANY)],
            out_specs=pl.BlockSpec((1, tm, d), lambda i, j, dr: (i, j, 0)),
            scratch_shapes=[pltpu.VMEM((2, tm, d), F32), pltpu.VMEM((2, tm, d), F32),
                            pltpu.SemaphoreType.DMA((2,))]),
        out_shape=jax.ShapeDtypeStruct((b, s, d), F32),
        compiler_params=_cparams(("arbitrary", "arbitrary")),
        name="moe_combine",
    )(dest, x, mods, mods, tokw, final_g.reshape(1, d), ys)


def _moe_routing(eidx):
    m = eidx.shape[0]
    oh = (eidx[:, None] == jnp.arange(N_EXPERTS, dtype=jnp.int32)[None, :]).astype(jnp.int32)
    csum = jnp.cumsum(oh, axis=0)
    rank = jnp.sum((csum - oh) * oh, axis=1)
    counts = csum[-1]
    padded = (counts + MOE_ROWS - 1) // MOE_ROWS * MOE_ROWS
    pad_end = jnp.cumsum(padded)
    dest = jnp.sum(oh * (pad_end - padded)[None, :], axis=1) + rank
    n_blocks = m // MOE_ROWS + N_EXPERTS
    starts = jnp.arange(n_blocks, dtype=jnp.int32) * MOE_ROWS
    block_e = jnp.minimum(jnp.sum((starts[:, None] >= pad_end[None, :]).astype(jnp.int32), axis=1), N_EXPERTS - 1)
    pad_cnt = padded - counts
    pad_cum = jnp.cumsum(pad_cnt)
    idx = jnp.arange(N_EXPERTS * MOE_ROWS, dtype=jnp.int32)
    owner = jnp.sum((idx[:, None] >= pad_cum[None, :]).astype(jnp.int32), axis=1)
    own = (jnp.minimum(owner, N_EXPERTS - 1)[:, None] == jnp.arange(N_EXPERTS)[None, :]).astype(jnp.int32)
    in_expert = jnp.sum(own * (pad_end - pad_cnt)[None, :], axis=1) + idx - jnp.sum(
        own * (pad_cum - pad_cnt)[None, :], axis=1)
    pad_rows = jnp.where(owner < N_EXPERTS, in_expert, pad_end[-1] + idx - pad_cum[-1])
    return dest.astype(jnp.int32), block_e, pad_rows.astype(jnp.int32), n_blocks * MOE_ROWS


def _rope_tables(n_lat, ctx_len):
    rows = n_lat // GRID_W
    row = jnp.repeat(jnp.arange(rows, dtype=F32), GRID_W)
    col = jnp.tile(jnp.arange(GRID_W, dtype=F32), rows)
    inv = ROPE_BASE ** (-jnp.arange(0, AX_DIM, 2, dtype=F32) / AX_DIM)
    ar, ac = row[:, None] * inv, col[:, None] * inv
    cos = jnp.concatenate([jnp.cos(ar), jnp.cos(ar), jnp.cos(ac), jnp.cos(ac)], axis=-1)
    sin = jnp.concatenate([-jnp.sin(ar), jnp.sin(ar), -jnp.sin(ac), jnp.sin(ac)], axis=-1)
    n_heads = ROPE_COLS // HD
    cos = jnp.concatenate([jnp.ones((ctx_len, HD), F32), cos], axis=0)
    sin = jnp.concatenate([jnp.zeros((ctx_len, HD), F32), sin], axis=0)
    half = AX_DIM // 2
    j = jnp.arange(HD)
    partner = jnp.where((j % AX_DIM) < half, j + half, j - half)
    perm = (jnp.arange(n_heads)[:, None] * HD + partner[None, :]).reshape(-1)
    return jnp.tile(cos, (1, n_heads)), jnp.tile(sin, (1, n_heads)), perm


def kernel(x, c, ctx, c_ctx, mod_w, mod_b, norm_mix, norm_ffn, norm_final, rec_w_in, rec_w_out, rwkv_mu, rwkv_w0, rwkv_w_up, rwkv_a0, rwkv_a_up, rwkv_g_up, rwkv_k_k, rwkv_k_a, rwkv_r_k, rwkv_ln_w, rwkv_ln_b, hgrn_lb, hgrn_norm, ffn_w_gate, ffn_w_up, ffn_w_down, att_w_in, att_w_out, att_sink, moe_router, moe_router_b, moe_w_gate, moe_w_up, moe_w_down):
    bsz, n_lat, d = x.shape
    ctx_len = ctx.shape[1]

    n_rows = -(-(bsz + 1) // 8) * 8
    cvec = jnp.zeros((n_rows, d), F32).at[:bsz].set(c).at[bsz].set(c_ctx)
    mods = [_adaln(cvec, mod_w[l], mod_b[l]).reshape(n_rows, N_MOD, d) for l in range(2)]

    ops, vv, g_bonus, gam, p_hgrn = _rec_in(ctx, x, mods[0], bsz, norm_mix[0], _bf(rec_w_in[0]), rwkv_mu[0],
                                            rwkv_w0[0], rwkv_w_up[0], rwkv_a0[0], rwkv_a_up[0], rwkv_g_up[0],
                                            rwkv_k_k[0], rwkv_k_a[0], rwkv_r_k[0].reshape(-1))
    oa = _rwkv_scan(ops, vv, gam, ctx_len)
    lb = jnp.cumsum(jax.nn.softmax(hgrn_lb.astype(F32), axis=1), axis=1)[:, 0].reshape(2, 1, B_W)
    ob = _hgrn_scan(p_hgrn, lb, ctx_len)
    xcat = _rec_out(ctx, x, mods[0], bsz, oa, g_bonus, ob, p_hgrn, rwkv_ln_w[0], rwkv_ln_b[0], hgrn_norm[0],
                    _bf(rec_w_out[0]), norm_ffn[0], _bf(ffn_w_gate[0]), _bf(ffn_w_up[0]), _bf(ffn_w_down[0]))

    cos, sin, perm = _rope_tables(n_lat, ctx_len)
    w_att = att_w_in[0]
    qkv = _proj_rope(xcat, mods[1], bsz, norm_mix[1], _bf(w_att), _bf(w_att[:, perm]), cos, sin, ctx_len)
    att = _attention(qkv, att_sink[0], ctx_len)
    x_lat, tokw, eidx, f_lat = _att_out(xcat, mods[1], att, _bf(att_w_out[0]), norm_ffn[1], moe_router[0],
                                        moe_router_b[0], x_skip=ctx_len)
    dest, block_e, pad_rows, n_rows = _moe_routing(eidx[..., :2].reshape(-1))
    xs = _row_scatter(f_lat.reshape(bsz * n_lat, d), dest, pad_rows, n_rows)
    ys = _experts(xs, block_e, _bf(moe_w_gate[0]), _bf(moe_w_up[0]), _bf(moe_w_down[0]))
    return _combine(x_lat, mods[1], tokw, dest, ys, norm_final)
```
